```python
import jax
import jax.numpy as jnp
from jax import lax
import numpy as np


D_MODEL = 1024
BATCH = 8
SEQ = 4096
DEPTH = 2

HEAD_DIM = 64
HEADS_PER_MIXER = 4
N_MIXERS = 4
N_HEADS = N_MIXERS * HEADS_PER_MIXER
MIX_WIDTH = N_HEADS * HEAD_DIM
QKV_WIDTH = 3 * HEADS_PER_MIXER * HEAD_DIM
Q_BLOCK = 128
IDX_HEADS = 8
IDX_DIM = 64
DSA_TOPK = 256
DILATED_PAIRS = ((128, 1), (512, 4), (2048, 16))
MOBA_BLOCK = 256
MOBA_TOPK = 3
N_ALIBI_MIXERS = 3
N_ALIBI_HEADS = N_ALIBI_MIXERS * HEADS_PER_MIXER
ALIBI_MAX_EXP = 8.0
N_IN = N_MIXERS * QKV_WIDTH + IDX_HEADS * IDX_DIM + IDX_DIM + IDX_HEADS
D_FF = 2816
N_EXPERTS = 8
TOP_K_EXPERTS = 2
D_FF_EXPERT = 3584
EXPERT_BLOCK = 128
NORM_EPS = 1e-6
ADA_INIT_SCALE = 0.5

kernel_name = "hybrid_sb_dsa_dilated_moba_moe_block"


def _rmsnorm(x, g):
    xf = x.astype(jnp.float32)
    y = xf * lax.rsqrt(jnp.mean(xf * xf, axis=-1, keepdims=True) + NORM_EPS)
    return (y * g.astype(jnp.float32)).astype(x.dtype)


def _alibi_slopes(mixer_pos):
    idx = jnp.arange(HEADS_PER_MIXER, dtype=jnp.float32) * N_ALIBI_MIXERS + (mixer_pos + 1)
    return jnp.exp2(-ALIBI_MAX_EXP * idx / N_ALIBI_HEADS)


def _split_heads(p):
    b, l, _ = p.shape
    p = p.reshape(b, l, 3, HEADS_PER_MIXER, HEAD_DIM)
    return p[:, :, 0].transpose(0, 2, 1, 3), p[:, :, 1].transpose(0, 2, 1, 3), p[:, :, 2].transpose(0, 2, 1, 3)


def _qblock(a, i, axis):
    return lax.dynamic_slice_in_dim(a, i * Q_BLOCK, Q_BLOCK, axis=axis)


def _merge_blocks(out):
    nq, b, h, q, d = out.shape
    return out.transpose(1, 2, 0, 3, 4).reshape(b, h, nq * q, d)


def _stick_breaking_attention(q, k, v):
    b, h, l, dh = q.shape
    scale = dh ** -0.5
    kpos = jnp.arange(l)

    def block(i):
        qs = _qblock(q, i, 2)
        qpos = i * Q_BLOCK + jnp.arange(Q_BLOCK)
        z = jnp.einsum('bhqd,bhkd->bhqk', qs, k).astype(jnp.float32) * scale
        past = kpos[None, :] < qpos[:, None]
        log_1m = jnp.where(past, jax.nn.log_sigmoid(-z), 0.0)
        after = lax.cumsum(log_1m, axis=3, reverse=True) - log_1m
        a = jnp.where(past, jnp.exp(jax.nn.log_sigmoid(z) + after), 0.0)
        return jnp.einsum('bhqk,bhkd->bhqd', a.astype(v.dtype), v)

    return _merge_blocks(lax.map(block, jnp.arange(l // Q_BLOCK)))


def _dsa_attention(q, k, v, q_idx, k_idx, w_idx, slopes):
    b, h, l, dh = q.shape
    n_sel = min(DSA_TOPK, l // 4)
    scale = dh ** -0.5
    kpos = jnp.arange(l)

    def block(i):
        qs = _qblock(q, i, 2)
        qi = _qblock(q_idx, i, 1)
        wi = _qblock(w_idx, i, 1).astype(jnp.float32) * IDX_HEADS ** -0.5
        qpos = i * Q_BLOCK + jnp.arange(Q_BLOCK)
        rel = jax.nn.relu(jnp.einsum('bqhd,bkd->bqhk', qi, k_idx).astype(jnp.float32) * IDX_DIM ** -0.5)
        score = jnp.einsum('bqh,bqhk->bqk', wi, rel)
        score = jnp.where(kpos[None, None, :] <= qpos[None, :, None], score, -jnp.inf)
        sel_score, sel_idx = lax.top_k(score, n_sel)
        kg = jax.vmap(lambda kk, ii: kk[:, ii])(k, sel_idx)
        vg = jax.vmap(lambda vv, ii: vv[:, ii])(v, sel_idx)
        dist = (qpos[None, :, None] - sel_idx).astype(jnp.float32)
        s = jnp.einsum('bhqd,bhqsd->bhqs', qs, kg).astype(jnp.float32) * scale
        s = s - slopes[None, :, None, None] * dist[:, None]
        s = jnp.where(jnp.isfinite(sel_score)[:, None], s, -jnp.inf)
        p = jax.nn.softmax(s, axis=-1)
        return jnp.einsum('bhqs,bhqsd->bhqd', p.astype(v.dtype), vg)

    return _merge_blocks(lax.map(block, jnp.arange(l // Q_BLOCK)))


def _dilated_attention(q, k, v, slopes):
    b, h, l, dh = q.shape
    scale = dh ** -0.5

    def block(i):
        qs = _qblock(q, i, 2)
        qpos = i * Q_BLOCK + jnp.arange(Q_BLOCK)
        outs, lses = [], []
        for window, dilation in DILATED_PAIRS:
            dist = jnp.arange(window // dilation + 1) * dilation
            kpos = qpos[:, None] - dist[None, :]
            kidx = jnp.maximum(kpos, 0)
            kg = k[:, :, kidx]
            vg = v[:, :, kidx]
            s = jnp.einsum('bhqd,bhqnd->bhqn', qs, kg).astype(jnp.float32) * scale
            s = s - slopes[:, None, None] * dist.astype(jnp.float32)
            s = jnp.where(kpos >= 0, s, -jnp.inf)
            m = jnp.max(s, axis=-1, keepdims=True)
            e = jnp.exp(s - m)
            den = jnp.sum(e, axis=-1, keepdims=True)
            outs.append(jnp.einsum('bhqn,bhqnd->bhqd', e, vg.astype(jnp.float32)) / den)
            lses.append(m + jnp.log(den))
        wts = jax.nn.softmax(jnp.stack(lses), axis=0)
        return jnp.sum(wts * jnp.stack(outs), axis=0).astype(v.dtype)

    return _merge_blocks(lax.map(block, jnp.arange(l // Q_BLOCK)))


def _moba_attention(q, k, v, slopes):
    b, h, l, dh = q.shape
    scale = dh ** -0.5
    nb = -(-l // MOBA_BLOCK)
    pad = nb * MOBA_BLOCK - l
    kb = jnp.pad(k, ((0, 0), (0, 0), (0, pad), (0, 0))).reshape(b, h, nb, MOBA_BLOCK, dh)
    vb = jnp.pad(v, ((0, 0), (0, 0), (0, pad), (0, 0))).reshape(b, h, nb, MOBA_BLOCK, dh)
    blk_len = jnp.clip(l - jnp.arange(nb) * MOBA_BLOCK, 1, MOBA_BLOCK).astype(jnp.float32)
    k_mean = kb.astype(jnp.float32).sum(axis=3) / blk_len[:, None]
    n_slots = min(MOBA_TOPK, nb - 1)
    offs = jnp.arange(MOBA_BLOCK)
    gather = jax.vmap(jax.vmap(lambda t, ii: t[ii]))

    def block(i):
        qs = _qblock(q, i, 2)
        qpos = i * Q_BLOCK + jnp.arange(Q_BLOCK)
        own = (i * Q_BLOCK) // MOBA_BLOCK
        k_own = lax.dynamic_index_in_dim(kb, own, axis=2, keepdims=False)
        v_own = lax.dynamic_index_in_dim(vb, own, axis=2, keepdims=False)
        own_pos = own * MOBA_BLOCK + offs
        s_own = jnp.einsum('bhqd,bhxd->bhqx', qs, k_own).astype(jnp.float32) * scale
        s_own = s_own - slopes[:, None, None] * (qpos[:, None] - own_pos[None, :]).astype(jnp.float32)
        s_own = jnp.where(own_pos[None, :] <= qpos[:, None], s_own, -jnp.inf)
        scores = [s_own]
        sel = []
        if n_slots > 0:
            g = jnp.einsum('bhqd,bhnd->bhqn', qs.astype(jnp.float32), k_mean)
            g = jnp.where(jnp.arange(nb) < own, g, -jnp.inf)
            g_val, g_idx = lax.top_k(g, n_slots)
            for j in range(n_slots):
                bj = g_idx[..., j]
                sel.append(bj)
                kg = gather(kb, bj)
                pos = bj[..., None] * MOBA_BLOCK + offs
                s = jnp.einsum('bhqd,bhqxd->bhqx', qs, kg).astype(jnp.float32) * scale
                s = s - slopes[:, None, None] * (qpos[:, None] - pos).astype(jnp.float32)
                s = jnp.where(jnp.isfinite(g_val[..., j:j + 1]), s, -jnp.inf)
                scores.append(s)
        p = jax.nn.softmax(jnp.concatenate(scores, axis=-1), axis=-1).astype(v.dtype)
        out = jnp.einsum('bhqx,bhxd->bhqd', p[..., :MOBA_BLOCK], v_own)
        for j, bj in enumerate(sel):
            vg = gather(vb, bj)
            out = out + jnp.einsum('bhqx,bhqxd->bhqd', p[..., (j + 1) * MOBA_BLOCK:(j + 2) * MOBA_BLOCK], vg)
        return out

    return _merge_blocks(lax.map(block, jnp.arange(l // Q_BLOCK)))


def _token_mixer(h, w_in, g_heads, w_out):
    b, l, _ = h.shape
    p = h @ w_in
    cuts = [int(c) for c in np.cumsum([QKV_WIDTH] * N_MIXERS + [IDX_HEADS * IDX_DIM, IDX_DIM])]
    pa, pb, pc, pd, pqi, pki, pwi = jnp.split(p, cuts, axis=-1)
    o_a = _stick_breaking_attention(*_split_heads(pa))
    o_b = _dsa_attention(*_split_heads(pb), pqi.reshape(b, l, IDX_HEADS, IDX_DIM), pki, pwi, _alibi_slopes(0))
    o_c = _dilated_attention(*_split_heads(pc), _alibi_slopes(1))
    o_d = _moba_attention(*_split_heads(pd), _alibi_slopes(2))
    o = jnp.concatenate([o_a, o_b, o_c, o_d], axis=1)
    o = _rmsnorm(o, g_heads[None, :, None, :])
    o = o.transpose(0, 2, 1, 3).reshape(b, l, MIX_WIDTH)
    return o @ w_out


def _swiglu(h, wg, wu, wd):
    return (jax.nn.silu(h @ wg) * (h @ wu)) @ wd


def _moe_ffn(h, w_router, w_gate, w_up, w_down):
    t, d = h.shape
    logits = h.astype(jnp.float32) @ w_router.astype(jnp.float32)
    top_val, top_idx = lax.top_k(logits, TOP_K_EXPERTS)
    gate = jax.nn.softmax(top_val, axis=-1)
    n_assign = t * TOP_K_EXPERTS
    flat_e = top_idx.reshape(-1).astype(jnp.int32)
    flat_tok = jnp.arange(n_assign, dtype=jnp.int32) // TOP_K_EXPERTS
    flat_g = gate.reshape(-1)
    order = jnp.argsort(flat_e)
    e_s, tok_s, g_s = flat_e[order], flat_tok[order], flat_g[order]
    counts = jnp.bincount(flat_e, length=N_EXPERTS).astype(jnp.int32)
    padded = (counts + EXPERT_BLOCK - 1) // EXPERT_BLOCK * EXPERT_BLOCK
    pad_end = jnp.cumsum(padded)
    pad_start = pad_end - padded
    start = jnp.cumsum(counts) - counts
    dest = pad_start[e_s] + jnp.arange(n_assign, dtype=jnp.int32) - start[e_s]
    n_slots = -(-n_assign // EXPERT_BLOCK) * EXPERT_BLOCK + N_EXPERTS * EXPERT_BLOCK
    n_blocks = n_slots // EXPERT_BLOCK
    slot_tok = jnp.zeros((n_slots,), jnp.int32).at[dest].set(tok_s)
    block_e = jnp.minimum(jnp.searchsorted(pad_end, jnp.arange(n_blocks, dtype=jnp.int32) * EXPERT_BLOCK, side='right'), N_EXPERTS - 1)
    xs = h[slot_tok].reshape(n_blocks, EXPERT_BLOCK, d)

    def expert_block(args):
        xb, e = args
        return _swiglu(xb, w_gate[e], w_up[e], w_down[e])

    ys = lax.map(expert_block, (xs, block_e)).reshape(n_slots, d)
    y_assign = ys[dest] * g_s[:, None].astype(h.dtype)
    return jnp.zeros_like(h).at[tok_s].add(y_assign)


def setup_inputs(seed: int = 0) -> dict:
    key = jax.random.key(seed)
    ks = jax.random.split(key, 17)
    n_dense = (DEPTH + 1) // 2
    n_moe = DEPTH // 2
    f32 = jnp.float32

    def nrm(k, shape, fan_in):
        return jax.random.normal(k, shape, f32) * fan_in ** -0.5

    def gain(k, shape):
        return 1.0 + 0.02 * jax.random.normal(k, shape, f32)

    return {
        'x': jax.random.normal(ks[0], (BATCH, SEQ, D_MODEL), f32),
        'c': jax.random.normal(ks[1], (BATCH, D_MODEL), f32),
        'w_ada': nrm(ks[2], (DEPTH, D_MODEL, 6 * D_MODEL), D_MODEL) * ADA_INIT_SCALE,
        'b_ada': 0.02 * jax.random.normal(ks[3], (DEPTH, 6 * D_MODEL), f32),
        'g_mix': gain(ks[4], (DEPTH, D_MODEL)),
        'w_in': nrm(ks[5], (DEPTH, D_MODEL, N_IN), D_MODEL),
        'g_heads': gain(ks[6], (DEPTH, N_HEADS, HEAD_DIM)),
        'w_out': nrm(ks[7], (DEPTH, MIX_WIDTH, D_MODEL), MIX_WIDTH),
        'g_ffn': gain(ks[8], (DEPTH, D_MODEL)),
        'w_ff_gate': nrm(ks[9], (n_dense, D_MODEL, D_FF), D_MODEL),
        'w_ff_up': nrm(ks[10], (n_dense, D_MODEL, D_FF), D_MODEL),
        'w_ff_down': nrm(ks[11], (n_dense, D_FF, D_MODEL), D_FF),
        'w_router': nrm(ks[12], (n_moe, D_MODEL, N_EXPERTS), D_MODEL),
        'w_exp_gate': nrm(ks[13], (n_moe, N_EXPERTS, D_MODEL, D_FF_EXPERT), D_MODEL),
        'w_exp_up': nrm(ks[14], (n_moe, N_EXPERTS, D_MODEL, D_FF_EXPERT), D_MODEL),
        'w_exp_down': nrm(ks[15], (n_moe, N_EXPERTS, D_FF_EXPERT, D_MODEL), D_FF_EXPERT),
        'g_final': gain(ks[16], (D_MODEL,)),
    }


def reference(x, c, w_ada, b_ada, g_mix, w_in, g_heads, w_out, g_ffn, w_ff_gate, w_ff_up, w_ff_down, w_router, w_exp_gate, w_exp_up, w_exp_down, g_final):
    b, l, d = x.shape
    cond = jax.nn.silu(c)
    for layer in range(DEPTH):
        mod = (cond @ w_ada[layer] + b_ada[layer])[:, None, :]
        shift_m, scale_m, gate_m, shift_f, scale_f, gate_f = jnp.split(mod, 6, axis=-1)
        hm = _rmsnorm(x, g_mix[layer]) * (1 + scale_m) + shift_m
        x = x + gate_m * _token_mixer(hm, w_in[layer], g_heads[layer], w_out[layer])
        hf = _rmsnorm(x, g_ffn[layer]) * (1 + scale_f) + shift_f
        i = layer // 2
        if layer % 2 == 0:
            y = _swiglu(hf, w_ff_gate[i], w_ff_up[i], w_ff_down[i])
        else:
            y = _moe_ffn(hf.reshape(b * l, d), w_router[i], w_exp_gate[i], w_exp_up[i], w_exp_down[i]).reshape(b, l, d)
        x = x + gate_f * y
    return _rmsnorm(x, g_final)
```

```python
import functools

import numpy as np
import jax
import jax.numpy as jnp
from jax import lax
from jax.experimental import pallas as pl
from jax.experimental.pallas import tpu as pltpu

F32 = jnp.float32
BF16 = jnp.bfloat16

HEAD_DIM = 64
HEADS_PER_MIXER = 4
MIXER_WIDTH = HEADS_PER_MIXER * HEAD_DIM
QKV_WIDTH = 3 * MIXER_WIDTH
IDX_HEADS = 8
IDX_DIM = 64
DSA_TOPK = 256
DILATIONS = (1, 4, 16)
DILATED_STEPS = 128
MOBA_BLOCK = 256
MOBA_TOPK = 3
N_EXPERTS = 8
TOP_K_EXPERTS = 2
NORM_EPS = 1e-6
ATTN_SCALE = HEAD_DIM ** -0.5

LANES = 128
Q_TILE = 128
ROW_TILE = 512
FFN_TILE = 1408
EXPERT_ROWS = 512
EXPERT_FF_TILE = 896
GATHER_ROWS = 512
VMEM_LIMIT = 56 * 1024 * 1024
MASKED = -1e30
SB_UNDERFLOW = 104.0

_NT = (((1,), (1,)), ((), ()))


def _alibi_slopes(mixer_pos):
    idx = np.arange(HEADS_PER_MIXER, dtype=np.float32) * 3 + (mixer_pos + 1)
    return tuple(float(s) for s in np.exp2(-8.0 * idx / 12.0).astype(np.float32))


def _params(*semantics):
    return pltpu.CompilerParams(dimension_semantics=semantics, vmem_limit_bytes=VMEM_LIMIT)


def _modulated_norm(x, gain, shift, scale):
    y = x * lax.rsqrt(jnp.mean(x * x, axis=-1, keepdims=True) + NORM_EPS) * gain
    return y * (1.0 + scale) + shift


def _head_norm(acc, gain):
    return acc * lax.rsqrt(jnp.mean(acc * acc, axis=-1, keepdims=True) + NORM_EPS) * gain


def _head_cols(h):
    return slice(h * HEAD_DIM, (h + 1) * HEAD_DIM)


def _ada_kernel(c_ref, w_ref, b_ref, o_ref):
    c = c_ref[...]
    cond = c / (1.0 + jnp.exp(-c))
    o_ref[0, 0] = jnp.dot(cond, w_ref[0], preferred_element_type=F32,
                          precision=lax.Precision.HIGHEST) + b_ref[0, 0]


def _ada_modulation(c, w_ada, b_ada):
    depth, d, _ = w_ada.shape
    b = c.shape[0]
    out = pl.pallas_call(
        _ada_kernel,
        grid=(depth, 6),
        in_specs=[pl.BlockSpec((b, d), lambda l, k: (0, 0)),
                  pl.BlockSpec((1, d, d), lambda l, k: (l, 0, k)),
                  pl.BlockSpec((1, 1, 1, d), lambda l, k: (l, k, 0, 0))],
        out_specs=pl.BlockSpec((1, 1, b, d), lambda l, k: (l, k, 0, 0)),
        out_shape=jax.ShapeDtypeStruct((depth, 6, b, d), F32),
        compiler_params=_params("arbitrary", "arbitrary"),
        name="ada_modulation",
    )(c, w_ada, b_ada.reshape(depth, 6, 1, d))
    return out.transpose(0, 2, 1, 3)


IN_WIDTHS = (QKV_WIDTH, QKV_WIDTH, QKV_WIDTH, QKV_WIDTH, IDX_HEADS * IDX_DIM + LANES, LANES)


def _in_proj_kernel(x_ref, mod_ref, g_ref, w_ref, oa, ob, oc, od, oidx, ow):
    h = _modulated_norm(x_ref[...], g_ref[...], mod_ref[0, 0:1, :], mod_ref[0, 1:2, :]).astype(BF16)
    off = 0
    for o_ref, width in zip((oa, ob, oc, od, oidx, ow), IN_WIDTHS):
        o_ref[...] = jnp.dot(h, w_ref[:, off:off + width], preferred_element_type=F32).astype(o_ref.dtype)
        off += width


def _pack_w_in(w_in):
    d = w_in.shape[0]
    n_qkv = 4 * QKV_WIDTH
    n_qi = IDX_HEADS * IDX_DIM
    z = lambda n: jnp.zeros((d, n), w_in.dtype)
    return jnp.concatenate([w_in[:, :n_qkv + n_qi + IDX_DIM], z(LANES - IDX_DIM),
                            w_in[:, n_qkv + n_qi + IDX_DIM:], z(LANES - IDX_HEADS)], axis=1).astype(BF16)


def _in_projection(x2, mod, g_mix, w_packed, seq):
    t, d = x2.shape
    tm = min(ROW_TILE, seq)
    per_batch = seq // tm
    dtypes = (BF16, BF16, BF16, BF16, BF16, F32)
    return pl.pallas_call(
        _in_proj_kernel,
        grid=(t // tm,),
        in_specs=[pl.BlockSpec((tm, d), lambda i: (i, 0)),
                  pl.BlockSpec((1, 6, d), lambda i: (i // per_batch, 0, 0)),
                  pl.BlockSpec((1, d), lambda i: (0, 0)),
                  pl.BlockSpec(w_packed.shape, lambda i: (0, 0))],
        out_specs=[pl.BlockSpec((tm, w), lambda i: (i, 0)) for w in IN_WIDTHS],
        out_shape=[jax.ShapeDtypeStruct((t, w), dt) for w, dt in zip(IN_WIDTHS, dtypes)],
        compiler_params=_params("arbitrary"),
        name="in_projection",
    )(x2, mod, g_mix.reshape(1, d), w_packed)


def _sb_kernel(q_ref, k_ref, v_ref, g_ref, o_ref):
    tq = q_ref.shape[1]
    i = pl.program_id(1)
    qpos = i * tq + lax.broadcasted_iota(jnp.int32, (tq, 1), 0)
    lane = lax.broadcasted_iota(jnp.int32, (1, tq), 1)
    later = (lax.broadcasted_iota(jnp.int32, (tq, tq), 0) > lax.broadcasted_iota(jnp.int32, (tq, tq), 1))
    later = jnp.where(later, 1.0, 0.0).astype(BF16)

    for h in range(HEADS_PER_MIXER):
        cols = _head_cols(h)
        q = q_ref[0, :, cols]

        def body(carry, q=q, cols=cols):
            j, tail, acc = carry
            start = pl.multiple_of(j * tq, tq)
            k = k_ref[0, pl.ds(start, tq), cols]
            v = v_ref[0, pl.ds(start, tq), cols]
            z = lax.dot_general(q, k, _NT, preferred_element_type=F32) * ATTN_SCALE
            softplus = jnp.maximum(z, 0.0) + jnp.log(1.0 + jnp.exp(-jnp.abs(z)))
            past = (start + lane) < qpos
            log_1m = jnp.where(past, -softplus, 0.0)
            hi = log_1m.astype(BF16)
            lo = (log_1m - hi.astype(F32)).astype(BF16)
            after = (jnp.dot(hi, later, preferred_element_type=F32)
                     + jnp.dot(lo, later, preferred_element_type=F32) + tail)
            a = jnp.where(past, jnp.exp(z - softplus + after), 0.0)
            acc = acc + jnp.dot(a.astype(BF16), v, preferred_element_type=F32)
            tail = tail + jnp.sum(log_1m, axis=1, keepdims=True)
            return j - 1, tail, acc

        def cond(carry):
            j, tail, _ = carry
            return jnp.logical_and(j >= 0, jnp.max(tail) > -SB_UNDERFLOW)

        _, _, acc = lax.while_loop(cond, body, (i, jnp.zeros((tq, 1), F32), jnp.zeros((tq, HEAD_DIM), F32)))
        o_ref[0, :, cols] = _head_norm(acc, g_ref[h:h + 1, :]).astype(o_ref.dtype)


def _stick_breaking_mixer(p, g_heads):
    b, seq, _ = p.shape
    tq = min(Q_TILE, seq)
    return pl.pallas_call(
        _sb_kernel,
        grid=(b, seq // tq),
        in_specs=[pl.BlockSpec((1, tq, MIXER_WIDTH), lambda bi, i: (bi, i, 0)),
                  pl.BlockSpec((1, seq, MIXER_WIDTH), lambda bi, i: (bi, 0, 1)),
                  pl.BlockSpec((1, seq, MIXER_WIDTH), lambda bi, i: (bi, 0, 2)),
                  pl.BlockSpec((HEADS_PER_MIXER, HEAD_DIM), lambda bi, i: (0, 0))],
        out_specs=pl.BlockSpec((1, tq, MIXER_WIDTH), lambda bi, i: (bi, i, 0)),
        out_shape=jax.ShapeDtypeStruct((b, seq, MIXER_WIDTH), BF16),
        compiler_params=_params("arbitrary", "arbitrary"),
        name="stick_breaking",
    )(p, p, p, g_heads)


def _dsa_kernel(q_ref, k_ref, v_ref, qi_ref, ki_ref, w_ref, g_ref, o_ref, sc_ref, *, slopes):
    tq = q_ref.shape[1]
    tk = sc_ref.shape[2]
    i = pl.program_id(1)
    n_blocks = ((i + 1) * tq + tk - 1) // tk
    qpos = i * tq + lax.broadcasted_iota(jnp.int32, (tq, 1), 0)
    lane = lax.broadcasted_iota(jnp.int32, (1, tk), 1)
    neg_inf = float("-inf")

    w = w_ref[0][:, 0:IDX_HEADS] * (IDX_HEADS ** -0.5 * IDX_DIM ** -0.5)
    w_cols = [w[:, h:h + 1] for h in range(IDX_HEADS)]
    qi = qi_ref[0]

    def score_body(j, carry):
        start = pl.multiple_of(j * tk, tk)
        ki = ki_ref[0, pl.ds(start, tk), 0:IDX_DIM]
        sc = jnp.zeros((tq, tk), F32)
        for h in range(IDX_HEADS):
            x = lax.dot_general(qi[:, h * IDX_DIM:(h + 1) * IDX_DIM], ki, _NT, preferred_element_type=F32)
            sc = sc + w_cols[h] * jnp.maximum(x, 0.0)
        sc_ref[j] = jnp.where((start + lane) <= qpos, sc + 0.0, neg_inf)
        return carry

    lax.fori_loop(0, n_blocks, score_body, 0)

    def count(pred):
        def body(j, cnt):
            c = jnp.where(pred(sc_ref[j]), 1.0, 0.0)
            for part in range(tk // LANES):
                cnt = cnt + c[:, part * LANES:(part + 1) * LANES]
            return cnt
        cnt = lax.fori_loop(0, n_blocks, body, jnp.zeros((tq, LANES), F32))
        return jnp.sum(cnt, axis=1, keepdims=True)

    int_min = jnp.int32(-2 ** 31)

    def ordered_to_float(u):
        key = u ^ int_min
        bits = jnp.where(key >= 0, key, key ^ jnp.int32(0x7FFFFFFF))
        return lax.bitcast_convert_type(bits, F32)

    def search_body(step, prefix):
        cand = prefix | jnp.left_shift(jnp.int32(1), 31 - step)
        cand_f = ordered_to_float(cand)
        n_ge = count(lambda s: s >= cand_f)
        return jnp.where(n_ge >= DSA_TOPK, cand, prefix)

    prefix = lax.fori_loop(0, 32, search_body, jnp.zeros((tq, 1), jnp.int32))
    keep_all = (qpos + 1) <= DSA_TOPK
    tau = jnp.where(keep_all, neg_inf, ordered_to_float(prefix))
    n_gt = count(lambda s: s > tau)
    n_ties = jnp.where(keep_all, 0.0, DSA_TOPK - n_gt)

    earlier = (lax.broadcasted_iota(jnp.int32, (tk, tk), 0) < lax.broadcasted_iota(jnp.int32, (tk, tk), 1))
    earlier = jnp.where(earlier, 1.0, 0.0).astype(BF16)
    qs = [q_ref[0, :, _head_cols(h)] for h in range(HEADS_PER_MIXER)]

    def attn_body(j, carry):
        ties_seen, stats = carry
        start = pl.multiple_of(j * tk, tk)
        sc = sc_ref[j]
        tie = jnp.where(jnp.logical_and(sc == tau, sc > neg_inf), 1.0, 0.0)
        rank = jnp.dot(tie.astype(BF16), earlier, preferred_element_type=F32) + ties_seen
        sel = jnp.where(sc > tau, 1.0, jnp.where(rank < n_ties, tie, 0.0)) > 0.5
        ties_seen = ties_seen + jnp.sum(tie, axis=1, keepdims=True)
        dist = (qpos - (start + lane)).astype(F32)
        new_stats = []
        for h in range(HEADS_PER_MIXER):
            m, l, acc = stats[h]
            cols = _head_cols(h)
            k = k_ref[0, pl.ds(start, tk), cols]
            v = v_ref[0, pl.ds(start, tk), cols]
            s = lax.dot_general(qs[h], k, _NT, preferred_element_type=F32) * ATTN_SCALE - slopes[h] * dist
            s = jnp.where(sel, s, MASKED)
            m_new = jnp.maximum(m, jnp.max(s, axis=1, keepdims=True))
            alpha = jnp.exp(m - m_new)
            p = jnp.where(sel, jnp.exp(s - m_new), 0.0)
            l = alpha * l + jnp.sum(p, axis=1, keepdims=True)
            acc = alpha * acc + jnp.dot(p.astype(BF16), v, preferred_element_type=F32)
            new_stats.append((m_new, l, acc))
        return ties_seen, tuple(new_stats)

    init = tuple((jnp.full((tq, 1), MASKED, F32), jnp.zeros((tq, 1), F32), jnp.zeros((tq, HEAD_DIM), F32))
                 for _ in range(HEADS_PER_MIXER))
    _, stats = lax.fori_loop(0, n_blocks, attn_body, (jnp.zeros((tq, 1), F32), init))
    for h in range(HEADS_PER_MIXER):
        _, l, acc = stats[h]
        o_ref[0, :, _head_cols(h)] = _head_norm(acc / l, g_ref[h:h + 1, :]).astype(o_ref.dtype)


def _dsa_mixer(p, p_idx, p_w, g_heads):
    b, seq, _ = p.shape
    tq = min(Q_TILE, seq)
    tk = min(2 * Q_TILE, seq)
    qi_width = IDX_HEADS * IDX_DIM
    return pl.pallas_call(
        functools.partial(_dsa_kernel, slopes=_alibi_slopes(0)),
        grid=(b, seq // tq),
        in_specs=[pl.BlockSpec((1, tq, MIXER_WIDTH), lambda bi, i: (bi, i, 0)),
                  pl.BlockSpec((1, seq, MIXER_WIDTH), lambda bi, i: (bi, 0, 1)),
                  pl.BlockSpec((1, seq, MIXER_WIDTH), lambda bi, i: (bi, 0, 2)),
                  pl.BlockSpec((1, tq, qi_width), lambda bi, i: (bi, i, 0)),
                  pl.BlockSpec((1, seq, LANES), lambda bi, i: (bi, 0, qi_width // LANES)),
                  pl.BlockSpec((1, tq, LANES), lambda bi, i: (bi, i, 0)),
                  pl.BlockSpec((HEADS_PER_MIXER, HEAD_DIM), lambda bi, i: (0, 0))],
        out_specs=pl.BlockSpec((1, tq, MIXER_WIDTH), lambda bi, i: (bi, i, 0)),
        out_shape=jax.ShapeDtypeStruct((b, seq, MIXER_WIDTH), BF16),
        scratch_shapes=[pltpu.VMEM((seq // tk, tq, tk), F32)],
        compiler_params=_params("arbitrary", "arbitrary"),
        name="dsa",
    )(p, p, p, p_idx, p_idx, p_w, g_heads)


def _band_kernel(q_ref, kp_ref, kc_ref, vp_ref, vc_ref, o_ref, lse_ref, *, dilation, slopes):
    tq = q_ref.shape[1]
    ui = pl.program_id(2)
    u_q = ui * tq + lax.broadcasted_iota(jnp.int32, (tq, 1), 0)
    u_k = (ui - 1) * tq + lax.broadcasted_iota(jnp.int32, (1, 2 * tq), 1)
    steps = u_q - u_k
    valid = jnp.logical_and(jnp.logical_and(steps >= 0, steps <= DILATED_STEPS), u_k >= 0)
    dist = (steps * dilation).astype(F32)
    lane = lax.broadcasted_iota(jnp.int32, (1, LANES), 1)
    lse_all = jnp.zeros((tq, LANES), F32)
    for h in range(HEADS_PER_MIXER):
        cols = _head_cols(h)
        k = jnp.concatenate([kp_ref[0, :, cols], kc_ref[0, :, cols]], axis=0)
        v = jnp.concatenate([vp_ref[0, :, cols], vc_ref[0, :, cols]], axis=0)
        s = lax.dot_general(q_ref[0, :, cols], k, _NT, preferred_element_type=F32) * ATTN_SCALE - slopes[h] * dist
        s = jnp.where(valid, s, MASKED)
        m = jnp.max(s, axis=1, keepdims=True)
        e = jnp.where(valid, jnp.exp(s - m), 0.0)
        den = jnp.sum(e, axis=1, keepdims=True)
        o_ref[0, :, cols] = jnp.dot(e.astype(BF16), v, preferred_element_type=F32) / den
        lse_all = lse_all + jnp.where(lane == h, m + jnp.log(den), 0.0)
    lse_ref[0] = lse_all


def _dilated_branch(p, dilation, slopes):
    b, seq, _ = p.shape
    classes = dilation
    length = seq // dilation
    tq = min(Q_TILE, length)
    view = p.reshape(b, length, classes * QKV_WIDTH)
    spec = lambda part, prev: pl.BlockSpec(
        (1, tq, MIXER_WIDTH),
        (lambda bi, c, ui: (bi, jnp.maximum(ui - 1, 0), c * 3 + part)) if prev
        else (lambda bi, c, ui: (bi, ui, c * 3 + part)))
    out, lse = pl.pallas_call(
        functools.partial(_band_kernel, dilation=dilation, slopes=slopes),
        grid=(b, classes, length // tq),
        in_specs=[spec(0, False), spec(1, True), spec(1, False), spec(2, True), spec(2, False)],
        out_specs=[pl.BlockSpec((1, tq, MIXER_WIDTH), lambda bi, c, ui: (bi, ui, c)),
                   pl.BlockSpec((1, tq, LANES), lambda bi, c, ui: (bi, ui, c))],
        out_shape=[jax.ShapeDtypeStruct((b, length, classes * MIXER_WIDTH), F32),
                   jax.ShapeDtypeStruct((b, length, classes * LANES), F32)],
        compiler_params=_params("arbitrary", "arbitrary", "arbitrary"),
        name=f"dilated_r{dilation}",
    )(view, view, view, view, view)
    return out.reshape(b, seq, MIXER_WIDTH), lse.reshape(b, seq, LANES)


def _dilated_merge_kernel(o1, o2, o3, l1, l2, l3, g_ref, o_ref):
    for h in range(HEADS_PER_MIXER):
        cols = _head_cols(h)
        lses = [l[0][:, h:h + 1] for l in (l1, l2, l3)]
        top = jnp.maximum(jnp.maximum(lses[0], lses[1]), lses[2])
        wts = [jnp.exp(l - top) for l in lses]
        mixed = (wts[0] * o1[0, :, cols] + wts[1] * o2[0, :, cols] + wts[2] * o3[0, :, cols]) / (wts[0] + wts[1] + wts[2])
        o_ref[0, :, cols] = _head_norm(mixed, g_ref[h:h + 1, :]).astype(o_ref.dtype)


def _dilated_mixer(p, g_heads):
    b, seq, _ = p.shape
    slopes = _alibi_slopes(1)
    branches = [_dilated_branch(p, r, slopes) for r in DILATIONS]
    tq = min(ROW_TILE, seq)
    o_spec = pl.BlockSpec((1, tq, MIXER_WIDTH), lambda bi, i: (bi, i, 0))
    l_spec = pl.BlockSpec((1, tq, LANES), lambda bi, i: (bi, i, 0))
    return pl.pallas_call(
        _dilated_merge_kernel,
        grid=(b, seq // tq),
        in_specs=[o_spec] * 3 + [l_spec] * 3 + [pl.BlockSpec((HEADS_PER_MIXER, HEAD_DIM), lambda bi, i: (0, 0))],
        out_specs=o_spec,
        out_shape=jax.ShapeDtypeStruct((b, seq, MIXER_WIDTH), BF16),
        compiler_params=_params("arbitrary", "arbitrary"),
        name="dilated_merge",
    )(*[o for o, _ in branches], *[l for _, l in branches], g_heads)


def _moba_kernel(q_ref, k_ref, v_ref, g_ref, o_ref, kmean_ref, *, slopes):
    tq = q_ref.shape[1]
    seq = k_ref.shape[1]
    n_kv = seq // MOBA_BLOCK
    i = pl.program_id(1)

    @pl.when(i == 0)
    def _():
        kmean_ref[...] = jnp.zeros_like(kmean_ref)
        for n in range(n_kv):
            blk = k_ref[0, n * MOBA_BLOCK:(n + 1) * MOBA_BLOCK, :].astype(F32)
            kmean_ref[n:n + 1, :] = jnp.sum(blk, axis=0, keepdims=True) * (1.0 / MOBA_BLOCK)

    own = (i * tq) // MOBA_BLOCK
    qpos = i * tq + lax.broadcasted_iota(jnp.int32, (tq, 1), 0)
    lane = lax.broadcasted_iota(jnp.int32, (1, MOBA_BLOCK), 1)
    blk_id = lax.broadcasted_iota(jnp.int32, (1, LANES), 1)
    blk_f = blk_id.astype(F32)
    neg_inf = float("-inf")

    for h in range(HEADS_PER_MIXER):
        cols = _head_cols(h)
        q = q_ref[0, :, cols]

        gate = lax.dot_general(q.astype(F32), kmean_ref[:, cols], _NT, preferred_element_type=F32,
                               precision=lax.Precision.HIGHEST)
        gate = jnp.where(blk_id < own, gate, neg_inf)
        chosen = jnp.zeros((tq, LANES), F32)
        for _ in range(MOBA_TOPK):
            top = jnp.max(gate, axis=1, keepdims=True)
            is_top = jnp.logical_and(gate == top, top > neg_inf)
            first = jnp.min(jnp.where(is_top, blk_f, float(LANES)), axis=1, keepdims=True)
            pick = blk_f == first
            chosen = jnp.where(pick, 1.0, chosen)
            gate = jnp.where(pick, neg_inf, gate)

        def scores(start, q=q, cols=cols, h=h):
            k = k_ref[0, pl.ds(start, MOBA_BLOCK), cols]
            dist = (qpos - (start + lane)).astype(F32)
            return lax.dot_general(q, k, _NT, preferred_element_type=F32) * ATTN_SCALE - slopes[h] * dist

        own_start = pl.multiple_of(own * MOBA_BLOCK, MOBA_BLOCK)
        causal = (own_start + lane) <= qpos
        s = jnp.where(causal, scores(own_start), MASKED)
        m = jnp.max(s, axis=1, keepdims=True)
        p = jnp.where(causal, jnp.exp(s - m), 0.0)
        l = jnp.sum(p, axis=1, keepdims=True)
        acc = jnp.dot(p.astype(BF16), v_ref[0, pl.ds(own_start, MOBA_BLOCK), cols], preferred_element_type=F32)

        def body(j, carry, chosen=chosen, cols=cols, scores=scores):
            m, l, acc = carry
            start = pl.multiple_of(j * MOBA_BLOCK, MOBA_BLOCK)
            picked = jnp.max(jnp.where(blk_id == j, chosen, 0.0), axis=1, keepdims=True) > 0.5
            s = jnp.where(picked, scores(start), MASKED)
            m_new = jnp.maximum(m, jnp.max(s, axis=1, keepdims=True))
            alpha = jnp.exp(m - m_new)
            p = jnp.where(picked, jnp.exp(s - m_new), 0.0)
            l = alpha * l + jnp.sum(p, axis=1, keepdims=True)
            acc = alpha * acc + jnp.dot(p.astype(BF16), v_ref[0, pl.ds(start, MOBA_BLOCK), cols],
                                        preferred_element_type=F32)
            return m_new, l, acc

        m, l, acc = lax.fori_loop(0, own, body, (m, l, acc))
        o_ref[0, :, cols] = _head_norm(acc / l, g_ref[h:h + 1, :]).astype(o_ref.dtype)


def _moba_mixer(p, g_heads):
    b, seq, _ = p.shape
    tq = min(Q_TILE, seq)
    return pl.pallas_call(
        functools.partial(_moba_kernel, slopes=_alibi_slopes(2)),
        grid=(b, seq // tq),
        in_specs=[pl.BlockSpec((1, tq, MIXER_WIDTH), lambda bi, i: (bi, i, 0)),
                  pl.BlockSpec((1, seq, MIXER_WIDTH), lambda bi, i: (bi, 0, 1)),
                  pl.BlockSpec((1, seq, MIXER_WIDTH), lambda bi, i: (bi, 0, 2)),
                  pl.BlockSpec((HEADS_PER_MIXER, HEAD_DIM), lambda bi, i: (0, 0))],
        out_specs=pl.BlockSpec((1, tq, MIXER_WIDTH), lambda bi, i: (bi, i, 0)),
        out_shape=jax.ShapeDtypeStruct((b, seq, MIXER_WIDTH), BF16),
        scratch_shapes=[pltpu.VMEM((LANES, MIXER_WIDTH), F32)],
        compiler_params=_params("arbitrary", "arbitrary"),
        name="moba",
    )(p, p, p, g_heads)


def _out_proj_kernel(oa, ob, oc, od, w_ref, x_ref, mod_ref, o_ref):
    acc = jnp.zeros(x_ref.shape, F32)
    for m, o in enumerate((oa, ob, oc, od)):
        acc = acc + jnp.dot(o[...], w_ref[m * MIXER_WIDTH:(m + 1) * MIXER_WIDTH, :], preferred_element_type=F32)
    o_ref[...] = x_ref[...] + mod_ref[0, 2:3, :] * acc


def _out_projection(mixed, w_out, x2, mod, seq):
    t, d = x2.shape
    tm = min(ROW_TILE, seq)
    per_batch = seq // tm
    o_spec = pl.BlockSpec((tm, MIXER_WIDTH), lambda i: (i, 0))
    return pl.pallas_call(
        _out_proj_kernel,
        grid=(t // tm,),
        in_specs=[o_spec] * 4 + [pl.BlockSpec(w_out.shape, lambda i: (0, 0)),
                                 pl.BlockSpec((tm, d), lambda i: (i, 0)),
                                 pl.BlockSpec((1, 6, d), lambda i: (i // per_batch, 0, 0))],
        out_specs=pl.BlockSpec((tm, d), lambda i: (i, 0)),
        out_shape=jax.ShapeDtypeStruct((t, d), F32),
        compiler_params=_params("arbitrary"),
        name="out_projection",
    )(*[o.reshape(t, MIXER_WIDTH) for o in mixed], w_out, x2, mod)


def _ffn_kernel(x_ref, mod_ref, g_ref, wg_ref, wu_ref, wd_ref, o_ref, h_ref, acc_ref):
    f = pl.program_id(1)

    @pl.when(f == 0)
    def _():
        h_ref[...] = _modulated_norm(x_ref[...], g_ref[...], mod_ref[0, 3:4, :], mod_ref[0, 4:5, :]).astype(BF16)
        acc_ref[...] = jnp.zeros_like(acc_ref)

    h = h_ref[...]
    gate = jnp.dot(h, wg_ref[...], preferred_element_type=F32)
    up = jnp.dot(h, wu_ref[...], preferred_element_type=F32)
    act = (gate / (1.0 + jnp.exp(-gate)) * up).astype(BF16)
    acc_ref[...] += jnp.dot(act, wd_ref[...], preferred_element_type=F32)

    @pl.when(f == pl.num_programs(1) - 1)
    def _():
        o_ref[...] = x_ref[...] + mod_ref[0, 5:6, :] * acc_ref[...]


def _dense_ffn(x2, mod, g_ffn, wg, wu, wd, seq):
    t, d = x2.shape
    d_ff = wg.shape[1]
    tm = min(ROW_TILE, seq)
    tf = FFN_TILE if d_ff % FFN_TILE == 0 else d_ff
    per_batch = seq // tm
    return pl.pallas_call(
        _ffn_kernel,
        grid=(t // tm, d_ff // tf),
        in_specs=[pl.BlockSpec((tm, d), lambda i, f: (i, 0)),
                  pl.BlockSpec((1, 6, d), lambda i, f: (i // per_batch, 0, 0)),
                  pl.BlockSpec((1, d), lambda i, f: (0, 0)),
                  pl.BlockSpec((d, tf), lambda i, f: (0, f)),
                  pl.BlockSpec((d, tf), lambda i, f: (0, f)),
                  pl.BlockSpec((tf, d), lambda i, f: (f, 0))],
        out_specs=pl.BlockSpec((tm, d), lambda i, f: (i, 0)),
        out_shape=jax.ShapeDtypeStruct((t, d), F32),
        scratch_shapes=[pltpu.VMEM((tm, d), BF16), pltpu.VMEM((tm, d), F32)],
        compiler_params=_params("arbitrary", "arbitrary"),
        name="dense_ffn",
    )(x2, mod, g_ffn.reshape(1, d), wg.astype(BF16), wu.astype(BF16), wd.astype(BF16))


def _router_kernel(x_ref, mod_ref, g_ref, wr_ref, h_ref, logit_ref):
    h = _modulated_norm(x_ref[...], g_ref[...], mod_ref[0, 3:4, :], mod_ref[0, 4:5, :])
    h_ref[...] = h
    logit_ref[...] = jnp.dot(h, wr_ref[...], preferred_element_type=F32, precision=lax.Precision.HIGHEST)


def _router(x2, mod, g_ffn, w_router, seq):
    t, d = x2.shape
    tm = min(ROW_TILE, seq)
    per_batch = seq // tm
    wr = jnp.zeros((d, LANES), F32).at[:, :N_EXPERTS].set(w_router.astype(F32))
    return pl.pallas_call(
        _router_kernel,
        grid=(t // tm,),
        in_specs=[pl.BlockSpec((tm, d), lambda i: (i, 0)),
                  pl.BlockSpec((1, 6, d), lambda i: (i // per_batch, 0, 0)),
                  pl.BlockSpec((1, d), lambda i: (0, 0)),
                  pl.BlockSpec((d, LANES), lambda i: (0, 0))],
        out_specs=[pl.BlockSpec((tm, d), lambda i: (i, 0)), pl.BlockSpec((tm, LANES), lambda i: (i, 0))],
        out_shape=[jax.ShapeDtypeStruct((t, d), F32), jax.ShapeDtypeStruct((t, LANES), F32)],
        compiler_params=_params("arbitrary"),
        name="moe_router",
    )(x2, mod, g_ffn.reshape(1, d), wr)


def _row_copy(src_hbm, row, dst_ref, r, sem):
    return pltpu.make_async_copy(src_hbm.at[pl.ds(row, 1), :], dst_ref.at[pl.ds(r, 1), :], sem)


def _gather_kernel(idx_ref, src_hbm, o_ref, sem):
    rows = o_ref.shape[0]

    def start(r, c):
        _row_copy(src_hbm, idx_ref[0, 0, r], o_ref, r, sem).start()
        return c

    def wait(r, c):
        _row_copy(src_hbm, idx_ref[0, 0, r], o_ref, r, sem).wait()
        return c

    lax.fori_loop(0, rows, start, 0)
    lax.fori_loop(0, rows, wait, 0)


def _gather_rows(src, idx, rows_per_step):
    n = idx.shape[0]
    d = src.shape[1]
    steps = n // rows_per_step
    return pl.pallas_call(
        _gather_kernel,
        grid=(steps,),
        in_specs=[pl.BlockSpec((1, 1, rows_per_step), lambda i: (i, 0, 0), memory_space=pltpu.SMEM),
                  pl.BlockSpec(memory_space=pl.ANY)],
        out_specs=pl.BlockSpec((rows_per_step, d), lambda i: (i, 0)),
        out_shape=jax.ShapeDtypeStruct((n, d), src.dtype),
        scratch_shapes=[pltpu.SemaphoreType.DMA(())],
        compiler_params=_params("arbitrary"),
        name="gather_rows",
    )(idx.reshape(steps, 1, rows_per_step), src)


def _expert_kernel(be_ref, used_ref, xs_ref, wg_ref, wu_ref, wd_ref, o_ref, xb_ref, acc_ref):
    m = pl.program_id(0)
    f = pl.program_id(1)
    live = m < used_ref[0]

    @pl.when(jnp.logical_and(live, f == 0))
    def _():
        xb_ref[...] = xs_ref[...].astype(BF16)
        acc_ref[...] = jnp.zeros_like(acc_ref)

    @pl.when(live)
    def _():
        h = xb_ref[...]
        gate = jnp.dot(h, wg_ref[0], preferred_element_type=F32)
        up = jnp.dot(h, wu_ref[0], preferred_element_type=F32)
        act = (gate / (1.0 + jnp.exp(-gate)) * up).astype(BF16)
        acc_ref[...] += jnp.dot(act, wd_ref[0], preferred_element_type=F32)

    @pl.when(f == pl.num_programs(1) - 1)
    def _():
        o_ref[...] = jnp.where(live, acc_ref[...], 0.0)


def _expert_ffn(xs, block_expert, n_used, wg, wu, wd):
    n_slots, d = xs.shape
    d_ff = wg.shape[2]
    tm = EXPERT_ROWS
    tf = EXPERT_FF_TILE if d_ff % EXPERT_FF_TILE == 0 else d_ff
    nf = d_ff // tf

    def fcol(m, f, used):
        return jnp.where(m < used[0], f, nf - 1)

    grid_spec = pltpu.PrefetchScalarGridSpec(
        num_scalar_prefetch=2,
        grid=(n_slots // tm, nf),
        in_specs=[pl.BlockSpec((tm, d), lambda m, f, be, used: (m, 0)),
                  pl.BlockSpec((1, d, tf), lambda m, f, be, used: (be[m], 0, fcol(m, f, used))),
                  pl.BlockSpec((1, d, tf), lambda m, f, be, used: (be[m], 0, fcol(m, f, used))),
                  pl.BlockSpec((1, tf, d), lambda m, f, be, used: (be[m], fcol(m, f, used), 0))],
        out_specs=pl.BlockSpec((tm, d), lambda m, f, be, used: (m, 0)),
        scratch_shapes=[pltpu.VMEM((tm, d), BF16), pltpu.VMEM((tm, d), F32)])
    return pl.pallas_call(
        _expert_kernel,
        grid_spec=grid_spec,
        out_shape=jax.ShapeDtypeStruct((n_slots, d), F32),
        compiler_params=_params("arbitrary", "arbitrary"),
        name="expert_ffn",
    )(block_expert, n_used, xs, wg.astype(BF16), wu.astype(BF16), wd.astype(BF16))


def _combine_kernel(d0_ref, d1_ref, ys_hbm, x_ref, gates_ref, mod_ref, o_ref, y0_ref, y1_ref, sem):
    rows = o_ref.shape[0]

    def start(r, c):
        _row_copy(ys_hbm, d0_ref[0, 0, r], y0_ref, r, sem.at[0]).start()
        _row_copy(ys_hbm, d1_ref[0, 0, r], y1_ref, r, sem.at[1]).start()
        return c

    def wait(r, c):
        _row_copy(ys_hbm, d0_ref[0, 0, r], y0_ref, r, sem.at[0]).wait()
        _row_copy(ys_hbm, d1_ref[0, 0, r], y1_ref, r, sem.at[1]).wait()
        return c

    lax.fori_loop(0, rows, start, 0)
    lax.fori_loop(0, rows, wait, 0)
    gates = gates_ref[...]
    y = y0_ref[...] * gates[:, 0:1] + y1_ref[...] * gates[:, 1:2]
    o_ref[...] = x_ref[...] + mod_ref[0, 5:6, :] * y


def _moe_combine(ys, dest0, dest1, gates, x2, mod, seq):
    t, d = x2.shape
    tm = min(GATHER_ROWS, seq)
    steps = t // tm
    per_batch = seq // tm
    idx_spec = pl.BlockSpec((1, 1, tm), lambda i: (i, 0, 0), memory_space=pltpu.SMEM)
    return pl.pallas_call(
        _combine_kernel,
        grid=(steps,),
        in_specs=[idx_spec, idx_spec, pl.BlockSpec(memory_space=pl.ANY),
                  pl.BlockSpec((tm, d), lambda i: (i, 0)),
                  pl.BlockSpec((tm, TOP_K_EXPERTS), lambda i: (i, 0)),
                  pl.BlockSpec((1, 6, d), lambda i: (i // per_batch, 0, 0))],
        out_specs=pl.BlockSpec((tm, d), lambda i: (i, 0)),
        out_shape=jax.ShapeDtypeStruct((t, d), F32),
        scratch_shapes=[pltpu.VMEM((tm, d), F32), pltpu.VMEM((tm, d), F32), pltpu.SemaphoreType.DMA((2,))],
        compiler_params=_params("arbitrary"),
        name="moe_combine",
    )(dest0.reshape(steps, 1, tm), dest1.reshape(steps, 1, tm), ys, x2, gates, mod)


def _moe_ffn(x2, mod, g_ffn, w_router, wg, wu, wd, seq):
    t, d = x2.shape
    h, logits = _router(x2, mod, g_ffn, w_router, seq)
    top_val, top_idx = lax.top_k(logits[:, :N_EXPERTS], TOP_K_EXPERTS)
    gates = jax.nn.softmax(top_val, axis=-1)

    n_assign = t * TOP_K_EXPERTS
    flat_e = top_idx.reshape(-1).astype(jnp.int32)
    onehot = (flat_e[:, None] == jnp.arange(N_EXPERTS, dtype=jnp.int32)[None, :]).astype(jnp.int32)
    rank = jnp.take_along_axis(jnp.cumsum(onehot, axis=0), flat_e[:, None], axis=1)[:, 0] - 1
    counts = jnp.sum(onehot, axis=0)
    padded = (counts + EXPERT_ROWS - 1) // EXPERT_ROWS * EXPERT_ROWS
    pad_end = jnp.cumsum(padded)
    dest = (pad_end - padded)[flat_e] + rank
    n_slots = (n_assign // EXPERT_ROWS + N_EXPERTS) * EXPERT_ROWS
    n_blocks = n_slots // EXPERT_ROWS
    slot_tok = jnp.zeros((n_slots,), jnp.int32).at[dest].set(jnp.arange(n_assign, dtype=jnp.int32) // TOP_K_EXPERTS)
    block_start = jnp.arange(n_blocks, dtype=jnp.int32) * EXPERT_ROWS
    block_expert = jnp.minimum(jnp.searchsorted(pad_end, block_start, side="right"), N_EXPERTS - 1).astype(jnp.int32)
    n_used = (pad_end[-1:] // EXPERT_ROWS).astype(jnp.int32)

    xs = _gather_rows(h, slot_tok, GATHER_ROWS)
    ys = _expert_ffn(xs, block_expert, n_used, wg, wu, wd)
    dest2 = dest.reshape(t, TOP_K_EXPERTS)
    return _moe_combine(ys, dest2[:, 0], dest2[:, 1], gates, x2, mod, seq)


def _final_norm_kernel(x_ref, g_ref, o_ref):
    x = x_ref[...]
    o_ref[...] = x * lax.rsqrt(jnp.mean(x * x, axis=-1, keepdims=True) + NORM_EPS) * g_ref[...]


def _final_norm(x2, g_final):
    t, d = x2.shape
    tm = min(ROW_TILE, t)
    return pl.pallas_call(
        _final_norm_kernel,
        grid=(t // tm,),
        in_specs=[pl.BlockSpec((tm, d), lambda i: (i, 0)), pl.BlockSpec((1, d), lambda i: (0, 0))],
        out_specs=pl.BlockSpec((tm, d), lambda i: (i, 0)),
        out_shape=jax.ShapeDtypeStruct((t, d), F32),
        compiler_params=_params("arbitrary"),
        name="final_norm",
    )(x2, g_final.reshape(1, d))


def _token_mixer(x2, mod, g_mix, w_in, g_heads, w_out, batch, seq):
    pa, pb, pc, pd, p_idx, p_w = _in_projection(x2, mod, g_mix, _pack_w_in(w_in), seq)
    shape3 = lambda a: a.reshape(batch, seq, a.shape[-1])
    gh = g_heads.reshape(4, HEADS_PER_MIXER, HEAD_DIM)
    mixed = (_stick_breaking_mixer(shape3(pa), gh[0]),
             _dsa_mixer(shape3(pb), shape3(p_idx), shape3(p_w), gh[1]),
             _dilated_mixer(shape3(pc), gh[2]),
             _moba_mixer(shape3(pd), gh[3]))
    return _out_projection(mixed, w_out.astype(BF16), x2, mod, seq)


def kernel(x, c, w_ada, b_ada, g_mix, w_in, g_heads, w_out, g_ffn, w_ff_gate, w_ff_up, w_ff_down, w_router, w_exp_gate, w_exp_up, w_exp_down, g_final):
    batch, seq, d = x.shape
    depth = w_ada.shape[0]
    mods = _ada_modulation(c, w_ada, b_ada)
    x2 = x.reshape(batch * seq, d)
    for layer in range(depth):
        mod = mods[layer]
        x2 = _token_mixer(x2, mod, g_mix[layer], w_in[layer], g_heads[layer], w_out[layer], batch, seq)
        i = layer // 2
        if layer % 2 == 0:
            x2 = _dense_ffn(x2, mod, g_ffn[layer], w_ff_gate[i], w_ff_up[i], w_ff_down[i], seq)
        else:
            x2 = _moe_ffn(x2, mod, g_ffn[layer], w_router[i], w_exp_gate[i], w_exp_up[i], w_exp_down[i], seq)
    return _final_norm(x2, g_final).reshape(batch, seq, d)
```

```python
import functools

import numpy as np
import jax
import jax.numpy as jnp
from jax import lax
from jax.experimental import pallas as pl
from jax.experimental.pallas import tpu as pltpu

F32 = jnp.float32
BF16 = jnp.bfloat16

HEAD_DIM = 64
HEADS_PER_MIXER = 4
MIXER_WIDTH = HEADS_PER_MIXER * HEAD_DIM
QKV_WIDTH = 3 * MIXER_WIDTH
IDX_HEADS = 8
IDX_DIM = 64
DSA_TOPK = 256
DILATIONS = (1, 4, 16)
DILATED_STEPS = 128
MOBA_BLOCK = 256
MOBA_TOPK = 3
N_EXPERTS = 8
TOP_K_EXPERTS = 2
NORM_EPS = 1e-6
ATTN_SCALE = HEAD_DIM ** -0.5

LANES = 128
Q_TILE = 128
ROW_TILE = 512
FFN_TILE = 1408
EXPERT_ROWS = 512
EXPERT_FF_TILE = 896
GATHER_ROWS = 512
VMEM_LIMIT = 56 * 1024 * 1024
MASKED = -1e30
SB_UNDERFLOW = 104.0

_NT = (((1,), (1,)), ((), ()))


def _alibi_slopes(mixer_pos):
    idx = np.arange(HEADS_PER_MIXER, dtype=np.float32) * 3 + (mixer_pos + 1)
    return tuple(float(s) for s in np.exp2(-8.0 * idx / 12.0).astype(np.float32))


def _params(*semantics):
    return pltpu.CompilerParams(dimension_semantics=semantics, vmem_limit_bytes=VMEM_LIMIT)


def _modulated_norm(x, gain, shift, scale):
    y = x * lax.rsqrt(jnp.mean(x * x, axis=-1, keepdims=True) + NORM_EPS) * gain
    return y * (1.0 + scale) + shift


def _head_norm(acc, gain):
    return acc * lax.rsqrt(jnp.mean(acc * acc, axis=-1, keepdims=True) + NORM_EPS) * gain


def _head_cols(h):
    return slice(h * HEAD_DIM, (h + 1) * HEAD_DIM)


def _ada_kernel(c_ref, w_ref, b_ref, o_ref):
    c = c_ref[...]
    cond = c / (1.0 + jnp.exp(-c))
    o_ref[0, 0] = jnp.dot(cond, w_ref[0], preferred_element_type=F32,
                          precision=lax.Precision.HIGHEST) + b_ref[0, 0]


def _ada_modulation(c, w_ada, b_ada):
    depth, d, _ = w_ada.shape
    b = c.shape[0]
    out = pl.pallas_call(
        _ada_kernel,
        grid=(depth, 6),
        in_specs=[pl.BlockSpec((b, d), lambda l, k: (0, 0)),
                  pl.BlockSpec((1, d, d), lambda l, k: (l, 0, k)),
                  pl.BlockSpec((1, 1, 1, d), lambda l, k: (l, k, 0, 0))],
        out_specs=pl.BlockSpec((1, 1, b, d), lambda l, k: (l, k, 0, 0)),
        out_shape=jax.ShapeDtypeStruct((depth, 6, b, d), F32),
        compiler_params=_params("arbitrary", "arbitrary"),
        name="ada_modulation",
    )(c, w_ada, b_ada.reshape(depth, 6, 1, d))
    return out.transpose(0, 2, 1, 3)


IN_WIDTHS = (QKV_WIDTH, QKV_WIDTH, QKV_WIDTH, QKV_WIDTH, IDX_HEADS * IDX_DIM + LANES, LANES)


def _in_proj_kernel(x_ref, mod_ref, g_ref, w_ref, oa, ob, oc, od, oidx, ow):
    h = _modulated_norm(x_ref[...], g_ref[...], mod_ref[0, 0:1, :], mod_ref[0, 1:2, :]).astype(BF16)
    off = 0
    for o_ref, width in zip((oa, ob, oc, od, oidx, ow), IN_WIDTHS):
        o_ref[...] = jnp.dot(h, w_ref[:, off:off + width], preferred_element_type=F32).astype(o_ref.dtype)
        off += width


def _pack_w_in(w_in):
    d = w_in.shape[0]
    n_qkv = 4 * QKV_WIDTH
    n_qi = IDX_HEADS * IDX_DIM
    z = lambda n: jnp.zeros((d, n), w_in.dtype)
    return jnp.concatenate([w_in[:, :n_qkv + n_qi + IDX_DIM], z(LANES - IDX_DIM),
                            w_in[:, n_qkv + n_qi + IDX_DIM:], z(LANES - IDX_HEADS)], axis=1).astype(BF16)


def _in_projection(x2, mod, g_mix, w_packed, seq):
    t, d = x2.shape
    tm = min(ROW_TILE, seq)
    per_batch = seq // tm
    dtypes = (BF16, BF16, BF16, BF16, BF16, F32)
    return pl.pallas_call(
        _in_proj_kernel,
        grid=(t // tm,),
        in_specs=[pl.BlockSpec((tm, d), lambda i: (i, 0)),
                  pl.BlockSpec((1, 6, d), lambda i: (i // per_batch, 0, 0)),
                  pl.BlockSpec((1, d), lambda i: (0, 0)),
                  pl.BlockSpec(w_packed.shape, lambda i: (0, 0))],
        out_specs=[pl.BlockSpec((tm, w), lambda i: (i, 0)) for w in IN_WIDTHS],
        out_shape=[jax.ShapeDtypeStruct((t, w), dt) for w, dt in zip(IN_WIDTHS, dtypes)],
        compiler_params=_params("arbitrary"),
        name="in_projection",
    )(x2, mod, g_mix.reshape(1, d), w_packed)


def _sb_kernel(q_ref, k_ref, v_ref, g_ref, o_ref):
    tq = q_ref.shape[1]
    i = pl.program_id(1)
    row = lax.broadcasted_iota(jnp.int32, (tq, 1), 0)
    lane = lax.broadcasted_iota(jnp.int32, (1, tq), 1)
    later = (lax.broadcasted_iota(jnp.int32, (tq, tq), 0) > lax.broadcasted_iota(jnp.int32, (tq, tq), 1))
    later = jnp.where(later, 1.0, 0.0).astype(BF16)

    qs = [q_ref[0, :, _head_cols(h)] * ATTN_SCALE for h in range(HEADS_PER_MIXER)]

    def block(h, start, tail, past=None):
        cols = _head_cols(h)
        k = k_ref[0, pl.ds(start, tq), cols]
        v = v_ref[0, pl.ds(start, tq), cols]
        z = lax.dot_general(qs[h], k, _NT, preferred_element_type=F32)
        softplus = jnp.maximum(z, 0.0) + jnp.log(1.0 + jnp.exp(-jnp.abs(z)))
        log_1m = -softplus if past is None else jnp.where(past, -softplus, 0.0)
        hi = log_1m.astype(BF16)
        lo = (log_1m - hi.astype(F32)).astype(BF16)
        after = (jnp.dot(hi, later, preferred_element_type=F32)
                 + jnp.dot(lo, later, preferred_element_type=F32) + tail)
        a = jnp.exp(z - softplus + after)
        if past is not None:
            a = jnp.where(past, a, 0.0)
        return (tail + jnp.sum(log_1m, axis=1, keepdims=True),
                jnp.dot(a.astype(BF16), v, preferred_element_type=F32))

    diag = pl.multiple_of(i * tq, tq)
    state = tuple(block(h, diag, jnp.zeros((tq, 1), F32), past=lane < row) for h in range(HEADS_PER_MIXER))

    def body(carry):
        j, state = carry
        start = pl.multiple_of(j * tq, tq)
        new_state = []
        for h in range(HEADS_PER_MIXER):
            tail, acc = state[h]
            tail, av = block(h, start, tail)
            new_state.append((tail, acc + av))
        return j - 1, tuple(new_state)

    def cond(carry):
        j, state = carry
        worst = functools.reduce(jnp.maximum, [tail for tail, _ in state])
        return jnp.logical_and(j >= 0, jnp.max(worst) > -SB_UNDERFLOW)

    _, state = lax.while_loop(cond, body, (i - 1, state))
    for h in range(HEADS_PER_MIXER):
        o_ref[0, :, _head_cols(h)] = _head_norm(state[h][1], g_ref[h:h + 1, :]).astype(o_ref.dtype)


def _stick_breaking_mixer(p, g_heads):
    b, seq, _ = p.shape
    tq = min(Q_TILE, seq)
    return pl.pallas_call(
        _sb_kernel,
        grid=(b, seq // tq),
        in_specs=[pl.BlockSpec((1, tq, MIXER_WIDTH), lambda bi, i: (bi, i, 0)),
                  pl.BlockSpec((1, seq, MIXER_WIDTH), lambda bi, i: (bi, 0, 1)),
                  pl.BlockSpec((1, seq, MIXER_WIDTH), lambda bi, i: (bi, 0, 2)),
                  pl.BlockSpec((HEADS_PER_MIXER, HEAD_DIM), lambda bi, i: (0, 0))],
        out_specs=pl.BlockSpec((1, tq, MIXER_WIDTH), lambda bi, i: (bi, i, 0)),
        out_shape=jax.ShapeDtypeStruct((b, seq, MIXER_WIDTH), BF16),
        compiler_params=_params("arbitrary", "arbitrary"),
        name="stick_breaking",
    )(p, p, p, g_heads)


def _dsa_kernel(q_ref, k_ref, v_ref, qi_ref, ki_ref, w_ref, g_ref, o_ref, sc_ref, *, slopes):
    tq = q_ref.shape[1]
    tk = sc_ref.shape[2]
    i = pl.program_id(1)
    n_blocks = ((i + 1) * tq + tk - 1) // tk
    qpos = i * tq + lax.broadcasted_iota(jnp.int32, (tq, 1), 0)
    lane = lax.broadcasted_iota(jnp.int32, (1, tk), 1)
    neg_inf = float("-inf")

    w = w_ref[0][:, 0:IDX_HEADS] * (IDX_HEADS ** -0.5 * IDX_DIM ** -0.5)
    w_cols = [w[:, h:h + 1] for h in range(IDX_HEADS)]
    qi = qi_ref[0]

    def score_body(j, carry):
        start = pl.multiple_of(j * tk, tk)
        ki = ki_ref[0, pl.ds(start, tk), 0:IDX_DIM]
        sc = jnp.zeros((tq, tk), F32)
        for h in range(IDX_HEADS):
            x = lax.dot_general(qi[:, h * IDX_DIM:(h + 1) * IDX_DIM], ki, _NT, preferred_element_type=F32)
            sc = sc + w_cols[h] * jnp.maximum(x, 0.0)
        sc_ref[j] = jnp.where((start + lane) <= qpos, sc + 0.0, neg_inf)
        return carry

    lax.fori_loop(0, n_blocks, score_body, 0)

    def count(pred):
        def body(j, cnt):
            c = jnp.where(pred(sc_ref[j]), 1.0, 0.0)
            for part in range(tk // LANES):
                cnt = cnt + c[:, part * LANES:(part + 1) * LANES]
            return cnt
        cnt = lax.fori_loop(0, n_blocks, body, jnp.zeros((tq, LANES), F32))
        return jnp.sum(cnt, axis=1, keepdims=True)

    int_min = jnp.int32(-2 ** 31)

    def ordered_to_float(u):
        key = u ^ int_min
        bits = jnp.where(key >= 0, key, key ^ jnp.int32(0x7FFFFFFF))
        return lax.bitcast_convert_type(bits, F32)

    def search_body(step, prefix):
        cand = prefix | jnp.left_shift(jnp.int32(1), 31 - step)
        cand_f = ordered_to_float(cand)
        n_ge = count(lambda s: s >= cand_f)
        return jnp.where(n_ge >= DSA_TOPK, cand, prefix)

    prefix = lax.fori_loop(0, 32, search_body, jnp.zeros((tq, 1), jnp.int32))
    keep_all = (qpos + 1) <= DSA_TOPK
    tau = jnp.where(keep_all, neg_inf, ordered_to_float(prefix))
    n_gt = count(lambda s: s > tau)
    n_ties = jnp.where(keep_all, 0.0, DSA_TOPK - n_gt)

    earlier = (lax.broadcasted_iota(jnp.int32, (tk, tk), 0) < lax.broadcasted_iota(jnp.int32, (tk, tk), 1))
    earlier = jnp.where(earlier, 1.0, 0.0).astype(BF16)
    qs = [q_ref[0, :, _head_cols(h)] * ATTN_SCALE for h in range(HEADS_PER_MIXER)]

    def attn_body(j, carry):
        ties_seen, stats = carry
        start = pl.multiple_of(j * tk, tk)
        sc = sc_ref[j]
        tie = jnp.where(sc == tau, 1.0, 0.0)
        rank = jnp.dot(tie.astype(BF16), earlier, preferred_element_type=F32) + ties_seen
        sel = jnp.where(sc > tau, 1.0, jnp.where(rank < n_ties, tie, 0.0)) > 0.5
        ties_seen = ties_seen + jnp.sum(tie, axis=1, keepdims=True)
        key_pos = (lane + (j * tk - i * tq)).astype(F32)
        new_stats = []
        for h in range(HEADS_PER_MIXER):
            m, l, acc = stats[h]
            cols = _head_cols(h)
            k = k_ref[0, pl.ds(start, tk), cols]
            v = v_ref[0, pl.ds(start, tk), cols]
            s = lax.dot_general(qs[h], k, _NT, preferred_element_type=F32) + slopes[h] * key_pos
            s = jnp.where(sel, s, MASKED)
            m_blk = jnp.max(s, axis=1, keepdims=True)
            p = jnp.exp(s - m_blk)
            l_blk = jnp.sum(p, axis=1, keepdims=True)
            pv_blk = jnp.dot(p.astype(BF16), v, preferred_element_type=F32)
            m_new = jnp.maximum(m, m_blk)
            w_old = jnp.exp(m - m_new)
            w_blk = jnp.where(m_blk > 0.5 * MASKED, jnp.exp(m_blk - m_new), 0.0)
            new_stats.append((m_new, w_old * l + w_blk * l_blk, w_old * acc + w_blk * pv_blk))
        return ties_seen, tuple(new_stats)

    init = tuple((jnp.full((tq, 1), MASKED, F32), jnp.zeros((tq, 1), F32), jnp.zeros((tq, HEAD_DIM), F32))
                 for _ in range(HEADS_PER_MIXER))
    _, stats = lax.fori_loop(0, n_blocks, attn_body, (jnp.zeros((tq, 1), F32), init))
    for h in range(HEADS_PER_MIXER):
        _, l, acc = stats[h]
        o_ref[0, :, _head_cols(h)] = _head_norm(acc / l, g_ref[h:h + 1, :]).astype(o_ref.dtype)


def _dsa_mixer(p, p_idx, p_w, g_heads):
    b, seq, _ = p.shape
    tq = min(2 * Q_TILE, seq)
    tk = tq
    qi_width = IDX_HEADS * IDX_DIM
    return pl.pallas_call(
        functools.partial(_dsa_kernel, slopes=_alibi_slopes(0)),
        grid=(b, seq // tq),
        in_specs=[pl.BlockSpec((1, tq, MIXER_WIDTH), lambda bi, i: (bi, i, 0)),
                  pl.BlockSpec((1, seq, MIXER_WIDTH), lambda bi, i: (bi, 0, 1)),
                  pl.BlockSpec((1, seq, MIXER_WIDTH), lambda bi, i: (bi, 0, 2)),
                  pl.BlockSpec((1, tq, qi_width), lambda bi, i: (bi, i, 0)),
                  pl.BlockSpec((1, seq, LANES), lambda bi, i: (bi, 0, qi_width // LANES)),
                  pl.BlockSpec((1, tq, LANES), lambda bi, i: (bi, i, 0)),
                  pl.BlockSpec((HEADS_PER_MIXER, HEAD_DIM), lambda bi, i: (0, 0))],
        out_specs=pl.BlockSpec((1, tq, MIXER_WIDTH), lambda bi, i: (bi, i, 0)),
        out_shape=jax.ShapeDtypeStruct((b, seq, MIXER_WIDTH), BF16),
        scratch_shapes=[pltpu.VMEM((seq // tk, tq, tk), F32)],
        compiler_params=_params("arbitrary", "arbitrary"),
        name="dsa",
    )(p, p, p, p_idx, p_idx, p_w, g_heads)


def _band_kernel(q_ref, kp_ref, kc_ref, vp_ref, vc_ref, o_ref, lse_ref, *, dilation, slopes):
    tq = q_ref.shape[1]
    ui = pl.program_id(2)
    u_q = ui * tq + lax.broadcasted_iota(jnp.int32, (tq, 1), 0)
    u_k = (ui - 1) * tq + lax.broadcasted_iota(jnp.int32, (1, 2 * tq), 1)
    steps = u_q - u_k
    valid = jnp.logical_and(jnp.logical_and(steps >= 0, steps <= DILATED_STEPS), u_k >= 0)
    dist = (steps * dilation).astype(F32)
    lane = lax.broadcasted_iota(jnp.int32, (1, LANES), 1)
    lse_all = jnp.zeros((tq, LANES), F32)
    for h in range(HEADS_PER_MIXER):
        cols = _head_cols(h)
        k = jnp.concatenate([kp_ref[0, :, cols], kc_ref[0, :, cols]], axis=0)
        v = jnp.concatenate([vp_ref[0, :, cols], vc_ref[0, :, cols]], axis=0)
        s = lax.dot_general(q_ref[0, :, cols], k, _NT, preferred_element_type=F32) * ATTN_SCALE - slopes[h] * dist
        s = jnp.where(valid, s, MASKED)
        m = jnp.max(s, axis=1, keepdims=True)
        e = jnp.where(valid, jnp.exp(s - m), 0.0)
        den = jnp.sum(e, axis=1, keepdims=True)
        o_ref[0, :, cols] = jnp.dot(e.astype(BF16), v, preferred_element_type=F32) / den
        lse_all = lse_all + jnp.where(lane == h, m + jnp.log(den), 0.0)
    lse_ref[0] = lse_all


def _dilated_branch(p, dilation, slopes):
    b, seq, _ = p.shape
    classes = dilation
    length = seq // dilation
    tq = min(Q_TILE, length)
    view = p.reshape(b, length, classes * QKV_WIDTH)
    spec = lambda part, prev: pl.BlockSpec(
        (1, tq, MIXER_WIDTH),
        (lambda bi, c, ui: (bi, jnp.maximum(ui - 1, 0), c * 3 + part)) if prev
        else (lambda bi, c, ui: (bi, ui, c * 3 + part)))
    out, lse = pl.pallas_call(
        functools.partial(_band_kernel, dilation=dilation, slopes=slopes),
        grid=(b, classes, length // tq),
        in_specs=[spec(0, False), spec(1, True), spec(1, False), spec(2, True), spec(2, False)],
        out_specs=[pl.BlockSpec((1, tq, MIXER_WIDTH), lambda bi, c, ui: (bi, ui, c)),
                   pl.BlockSpec((1, tq, LANES), lambda bi, c, ui: (bi, ui, c))],
        out_shape=[jax.ShapeDtypeStruct((b, length, classes * MIXER_WIDTH), F32),
                   jax.ShapeDtypeStruct((b, length, classes * LANES), F32)],
        compiler_params=_params("arbitrary", "arbitrary", "arbitrary"),
        name=f"dilated_r{dilation}",
    )(view, view, view, view, view)
    return out.reshape(b, seq, MIXER_WIDTH), lse.reshape(b, seq, LANES)


def _dilated_merge_kernel(o1, o2, o3, l1, l2, l3, g_ref, o_ref):
    for h in range(HEADS_PER_MIXER):
        cols = _head_cols(h)
        lses = [l[0][:, h:h + 1] for l in (l1, l2, l3)]
        top = jnp.maximum(jnp.maximum(lses[0], lses[1]), lses[2])
        wts = [jnp.exp(l - top) for l in lses]
        mixed = (wts[0] * o1[0, :, cols] + wts[1] * o2[0, :, cols] + wts[2] * o3[0, :, cols]) / (wts[0] + wts[1] + wts[2])
        o_ref[0, :, cols] = _head_norm(mixed, g_ref[h:h + 1, :]).astype(o_ref.dtype)


def _dilated_mixer(p, g_heads):
    b, seq, _ = p.shape
    slopes = _alibi_slopes(1)
    branches = [_dilated_branch(p, r, slopes) for r in DILATIONS]
    tq = min(ROW_TILE, seq)
    o_spec = pl.BlockSpec((1, tq, MIXER_WIDTH), lambda bi, i: (bi, i, 0))
    l_spec = pl.BlockSpec((1, tq, LANES), lambda bi, i: (bi, i, 0))
    return pl.pallas_call(
        _dilated_merge_kernel,
        grid=(b, seq // tq),
        in_specs=[o_spec] * 3 + [l_spec] * 3 + [pl.BlockSpec((HEADS_PER_MIXER, HEAD_DIM), lambda bi, i: (0, 0))],
        out_specs=o_spec,
        out_shape=jax.ShapeDtypeStruct((b, seq, MIXER_WIDTH), BF16),
        compiler_params=_params("arbitrary", "arbitrary"),
        name="dilated_merge",
    )(*[o for o, _ in branches], *[l for _, l in branches], g_heads)


def _moba_kernel(q_ref, k_ref, v_ref, g_ref, o_ref, kmean_ref, *, slopes):
    tq = q_ref.shape[1]
    n_kv = k_ref.shape[1] // MOBA_BLOCK
    own = pl.program_id(1)

    @pl.when(own == 0)
    def _():
        kmean_ref[...] = jnp.zeros_like(kmean_ref)
        for n in range(n_kv):
            blk = k_ref[0, n * MOBA_BLOCK:(n + 1) * MOBA_BLOCK, :].astype(F32)
            kmean_ref[n:n + 1, :] = jnp.sum(blk, axis=0, keepdims=True) * (1.0 / MOBA_BLOCK)

    row = lax.broadcasted_iota(jnp.int32, (tq, 1), 0)
    lane = lax.broadcasted_iota(jnp.int32, (1, MOBA_BLOCK), 1)
    blk_id = lax.broadcasted_iota(jnp.int32, (1, LANES), 1)
    blk_f = blk_id.astype(F32)
    neg_inf = float("-inf")

    qs, chosen = [], []
    for h in range(HEADS_PER_MIXER):
        q = q_ref[0, :, _head_cols(h)]
        gate = lax.dot_general(q.astype(F32), kmean_ref[:, _head_cols(h)], _NT, preferred_element_type=F32,
                               precision=lax.Precision.HIGHEST)
        gate = jnp.where(blk_id < own, gate, neg_inf)
        picks = jnp.zeros((tq, LANES), F32)
        for _ in range(MOBA_TOPK):
            top = jnp.max(gate, axis=1, keepdims=True)
            is_top = jnp.logical_and(gate == top, top > neg_inf)
            first = jnp.min(jnp.where(is_top, blk_f, float(LANES)), axis=1, keepdims=True)
            pick = blk_f == first
            picks = jnp.where(pick, 1.0, picks)
            gate = jnp.where(pick, neg_inf, gate)
        chosen.append(picks)
        qs.append(q * ATTN_SCALE)

    def block_softmax(h, start, key_pos, keep=None):
        cols = _head_cols(h)
        k = k_ref[0, pl.ds(start, MOBA_BLOCK), cols]
        v = v_ref[0, pl.ds(start, MOBA_BLOCK), cols]
        s = lax.dot_general(qs[h], k, _NT, preferred_element_type=F32) + slopes[h] * key_pos
        if keep is not None:
            s = jnp.where(keep, s, MASKED)
        m = jnp.max(s, axis=1, keepdims=True)
        p = jnp.exp(s - m)
        return m, jnp.sum(p, axis=1, keepdims=True), jnp.dot(p.astype(BF16), v, preferred_element_type=F32)

    own_start = pl.multiple_of(own * MOBA_BLOCK, MOBA_BLOCK)
    stats = tuple(block_softmax(h, own_start, lane.astype(F32), keep=lane <= row) for h in range(HEADS_PER_MIXER))

    def body(j, stats):
        start = pl.multiple_of(j * MOBA_BLOCK, MOBA_BLOCK)
        key_pos = (lane + (j - own) * MOBA_BLOCK).astype(F32)
        merged = []
        for h in range(HEADS_PER_MIXER):
            m_blk, l_blk, pv_blk = block_softmax(h, start, key_pos)
            picked = jnp.max(jnp.where(blk_id == j, chosen[h], 0.0), axis=1, keepdims=True) > 0.5
            m_blk = jnp.where(picked, m_blk, MASKED)
            m, l, acc = stats[h]
            m_new = jnp.maximum(m, m_blk)
            w_old = jnp.exp(m - m_new)
            w_blk = jnp.exp(m_blk - m_new)
            merged.append((m_new, w_old * l + w_blk * l_blk, w_old * acc + w_blk * pv_blk))
        return tuple(merged)

    stats = lax.fori_loop(0, own, body, stats)
    for h in range(HEADS_PER_MIXER):
        _, l, acc = stats[h]
        o_ref[0, :, _head_cols(h)] = _head_norm(acc / l, g_ref[h:h + 1, :]).astype(o_ref.dtype)


def _moba_mixer(p, g_heads):
    b, seq, _ = p.shape
    tq = MOBA_BLOCK
    return pl.pallas_call(
        functools.partial(_moba_kernel, slopes=_alibi_slopes(2)),
        grid=(b, seq // tq),
        in_specs=[pl.BlockSpec((1, tq, MIXER_WIDTH), lambda bi, i: (bi, i, 0)),
                  pl.BlockSpec((1, seq, MIXER_WIDTH), lambda bi, i: (bi, 0, 1)),
                  pl.BlockSpec((1, seq, MIXER_WIDTH), lambda bi, i: (bi, 0, 2)),
                  pl.BlockSpec((HEADS_PER_MIXER, HEAD_DIM), lambda bi, i: (0, 0))],
        out_specs=pl.BlockSpec((1, tq, MIXER_WIDTH), lambda bi, i: (bi, i, 0)),
        out_shape=jax.ShapeDtypeStruct((b, seq, MIXER_WIDTH), BF16),
        scratch_shapes=[pltpu.VMEM((LANES, MIXER_WIDTH), F32)],
        compiler_params=_params("arbitrary", "arbitrary"),
        name="moba",
    )(p, p, p, g_heads)


def _out_proj_kernel(oa, ob, oc, od, w_ref, x_ref, mod_ref, o_ref):
    acc = jnp.zeros(x_ref.shape, F32)
    for m, o in enumerate((oa, ob, oc, od)):
        acc = acc + jnp.dot(o[...], w_ref[m * MIXER_WIDTH:(m + 1) * MIXER_WIDTH, :], preferred_element_type=F32)
    o_ref[...] = x_ref[...] + mod_ref[0, 2:3, :] * acc


def _out_projection(mixed, w_out, x2, mod, seq):
    t, d = x2.shape
    tm = min(ROW_TILE, seq)
    per_batch = seq // tm
    o_spec = pl.BlockSpec((tm, MIXER_WIDTH), lambda i: (i, 0))
    return pl.pallas_call(
        _out_proj_kernel,
        grid=(t // tm,),
        in_specs=[o_spec] * 4 + [pl.BlockSpec(w_out.shape, lambda i: (0, 0)),
                                 pl.BlockSpec((tm, d), lambda i: (i, 0)),
                                 pl.BlockSpec((1, 6, d), lambda i: (i // per_batch, 0, 0))],
        out_specs=pl.BlockSpec((tm, d), lambda i: (i, 0)),
        out_shape=jax.ShapeDtypeStruct((t, d), F32),
        compiler_params=_params("arbitrary"),
        name="out_projection",
    )(*[o.reshape(t, MIXER_WIDTH) for o in mixed], w_out, x2, mod)


def _ffn_kernel(x_ref, mod_ref, g_ref, wg_ref, wu_ref, wd_ref, o_ref, h_ref, acc_ref):
    f = pl.program_id(1)

    @pl.when(f == 0)
    def _():
        h_ref[...] = _modulated_norm(x_ref[...], g_ref[...], mod_ref[0, 3:4, :], mod_ref[0, 4:5, :]).astype(BF16)
        acc_ref[...] = jnp.zeros_like(acc_ref)

    h = h_ref[...]
    gate = jnp.dot(h, wg_ref[...], preferred_element_type=F32)
    up = jnp.dot(h, wu_ref[...], preferred_element_type=F32)
    act = (gate / (1.0 + jnp.exp(-gate)) * up).astype(BF16)
    acc_ref[...] += jnp.dot(act, wd_ref[...], preferred_element_type=F32)

    @pl.when(f == pl.num_programs(1) - 1)
    def _():
        o_ref[...] = x_ref[...] + mod_ref[0, 5:6, :] * acc_ref[...]


def _dense_ffn(x2, mod, g_ffn, wg, wu, wd, seq):
    t, d = x2.shape
    d_ff = wg.shape[1]
    tm = min(ROW_TILE, seq)
    tf = FFN_TILE if d_ff % FFN_TILE == 0 else d_ff
    per_batch = seq // tm
    return pl.pallas_call(
        _ffn_kernel,
        grid=(t // tm, d_ff // tf),
        in_specs=[pl.BlockSpec((tm, d), lambda i, f: (i, 0)),
                  pl.BlockSpec((1, 6, d), lambda i, f: (i // per_batch, 0, 0)),
                  pl.BlockSpec((1, d), lambda i, f: (0, 0)),
                  pl.BlockSpec((d, tf), lambda i, f: (0, f)),
                  pl.BlockSpec((d, tf), lambda i, f: (0, f)),
                  pl.BlockSpec((tf, d), lambda i, f: (f, 0))],
        out_specs=pl.BlockSpec((tm, d), lambda i, f: (i, 0)),
        out_shape=jax.ShapeDtypeStruct((t, d), F32),
        scratch_shapes=[pltpu.VMEM((tm, d), BF16), pltpu.VMEM((tm, d), F32)],
        compiler_params=_params("arbitrary", "arbitrary"),
        name="dense_ffn",
    )(x2, mod, g_ffn.reshape(1, d), wg.astype(BF16), wu.astype(BF16), wd.astype(BF16))


def _router_kernel(x_ref, mod_ref, g_ref, wr_ref, h_ref, logit_ref):
    h = _modulated_norm(x_ref[...], g_ref[...], mod_ref[0, 3:4, :], mod_ref[0, 4:5, :])
    h_ref[...] = h
    logit_ref[...] = jnp.dot(h, wr_ref[...], preferred_element_type=F32, precision=lax.Precision.HIGHEST)


def _router(x2, mod, g_ffn, w_router, seq):
    t, d = x2.shape
    tm = min(ROW_TILE, seq)
    per_batch = seq // tm
    wr = jnp.zeros((d, LANES), F32).at[:, :N_EXPERTS].set(w_router.astype(F32))
    return pl.pallas_call(
        _router_kernel,
        grid=(t // tm,),
        in_specs=[pl.BlockSpec((tm, d), lambda i: (i, 0)),
                  pl.BlockSpec((1, 6, d), lambda i: (i // per_batch, 0, 0)),
                  pl.BlockSpec((1, d), lambda i: (0, 0)),
                  pl.BlockSpec((d, LANES), lambda i: (0, 0))],
        out_specs=[pl.BlockSpec((tm, d), lambda i: (i, 0)), pl.BlockSpec((tm, LANES), lambda i: (i, 0))],
        out_shape=[jax.ShapeDtypeStruct((t, d), F32), jax.ShapeDtypeStruct((t, LANES), F32)],
        compiler_params=_params("arbitrary"),
        name="moe_router",
    )(x2, mod, g_ffn.reshape(1, d), wr)


def _row_copy(src_hbm, row, dst_ref, r, sem):
    return pltpu.make_async_copy(src_hbm.at[pl.ds(row, 1), :], dst_ref.at[pl.ds(r, 1), :], sem)


def _gather_kernel(idx_ref, src_hbm, o_ref, sem):
    rows = o_ref.shape[0]

    def start(r, c):
        _row_copy(src_hbm, idx_ref[0, 0, r], o_ref, r, sem).start()
        return c

    def wait(r, c):
        _row_copy(src_hbm, idx_ref[0, 0, r], o_ref, r, sem).wait()
        return c

    lax.fori_loop(0, rows, start, 0)
    lax.fori_loop(0, rows, wait, 0)


def _gather_rows(src, idx, rows_per_step):
    n = idx.shape[0]
    d = src.shape[1]
    steps = n // rows_per_step
    return pl.pallas_call(
        _gather_kernel,
        grid=(steps,),
        in_specs=[pl.BlockSpec((1, 1, rows_per_step), lambda i: (i, 0, 0), memory_space=pltpu.SMEM),
                  pl.BlockSpec(memory_space=pl.ANY)],
        out_specs=pl.BlockSpec((rows_per_step, d), lambda i: (i, 0)),
        out_shape=jax.ShapeDtypeStruct((n, d), src.dtype),
        scratch_shapes=[pltpu.SemaphoreType.DMA(())],
        compiler_params=_params("arbitrary"),
        name="gather_rows",
    )(idx.reshape(steps, 1, rows_per_step), src)


def _expert_kernel(be_ref, used_ref, xs_ref, wg_ref, wu_ref, wd_ref, o_ref, xb_ref, acc_ref):
    m = pl.program_id(0)
    f = pl.program_id(1)
    live = m < used_ref[0]

    @pl.when(jnp.logical_and(live, f == 0))
    def _():
        xb_ref[...] = xs_ref[...].astype(BF16)
        acc_ref[...] = jnp.zeros_like(acc_ref)

    @pl.when(live)
    def _():
        h = xb_ref[...]
        gate = jnp.dot(h, wg_ref[0], preferred_element_type=F32)
        up = jnp.dot(h, wu_ref[0], preferred_element_type=F32)
        act = (gate / (1.0 + jnp.exp(-gate)) * up).astype(BF16)
        acc_ref[...] += jnp.dot(act, wd_ref[0], preferred_element_type=F32)

    @pl.when(f == pl.num_programs(1) - 1)
    def _():
        o_ref[...] = jnp.where(live, acc_ref[...], 0.0)


def _expert_ffn(xs, block_expert, n_used, wg, wu, wd):
    n_slots, d = xs.shape
    d_ff = wg.shape[2]
    tm = EXPERT_ROWS
    tf = EXPERT_FF_TILE if d_ff % EXPERT_FF_TILE == 0 else d_ff
    nf = d_ff // tf

    def fcol(m, f, used):
        return jnp.where(m < used[0], f, nf - 1)

    grid_spec = pltpu.PrefetchScalarGridSpec(
        num_scalar_prefetch=2,
        grid=(n_slots // tm, nf),
        in_specs=[pl.BlockSpec((tm, d), lambda m, f, be, used: (m, 0)),
                  pl.BlockSpec((1, d, tf), lambda m, f, be, used: (be[m], 0, fcol(m, f, used))),
                  pl.BlockSpec((1, d, tf), lambda m, f, be, used: (be[m], 0, fcol(m, f, used))),
                  pl.BlockSpec((1, tf, d), lambda m, f, be, used: (be[m], fcol(m, f, used), 0))],
        out_specs=pl.BlockSpec((tm, d), lambda m, f, be, used: (m, 0)),
        scratch_shapes=[pltpu.VMEM((tm, d), BF16), pltpu.VMEM((tm, d), F32)])
    return pl.pallas_call(
        _expert_kernel,
        grid_spec=grid_spec,
        out_shape=jax.ShapeDtypeStruct((n_slots, d), F32),
        compiler_params=_params("arbitrary", "arbitrary"),
        name="expert_ffn",
    )(block_expert, n_used, xs, wg.astype(BF16), wu.astype(BF16), wd.astype(BF16))


def _combine_kernel(d0_ref, d1_ref, ys_hbm, x_ref, gates_ref, mod_ref, o_ref, y0_ref, y1_ref, sem):
    rows = o_ref.shape[0]

    def start(r, c):
        _row_copy(ys_hbm, d0_ref[0, 0, r], y0_ref, r, sem.at[0]).start()
        _row_copy(ys_hbm, d1_ref[0, 0, r], y1_ref, r, sem.at[1]).start()
        return c

    def wait(r, c):
        _row_copy(ys_hbm, d0_ref[0, 0, r], y0_ref, r, sem.at[0]).wait()
        _row_copy(ys_hbm, d1_ref[0, 0, r], y1_ref, r, sem.at[1]).wait()
        return c

    lax.fori_loop(0, rows, start, 0)
    lax.fori_loop(0, rows, wait, 0)
    gates = gates_ref[...]
    y = y0_ref[...] * gates[:, 0:1] + y1_ref[...] * gates[:, 1:2]
    o_ref[...] = x_ref[...] + mod_ref[0, 5:6, :] * y


def _moe_combine(ys, dest0, dest1, gates, x2, mod, seq):
    t, d = x2.shape
    tm = min(GATHER_ROWS, seq)
    steps = t // tm
    per_batch = seq // tm
    idx_spec = pl.BlockSpec((1, 1, tm), lambda i: (i, 0, 0), memory_space=pltpu.SMEM)
    return pl.pallas_call(
        _combine_kernel,
        grid=(steps,),
        in_specs=[idx_spec, idx_spec, pl.BlockSpec(memory_space=pl.ANY),
                  pl.BlockSpec((tm, d), lambda i: (i, 0)),
                  pl.BlockSpec((tm, TOP_K_EXPERTS), lambda i: (i, 0)),
                  pl.BlockSpec((1, 6, d), lambda i: (i // per_batch, 0, 0))],
        out_specs=pl.BlockSpec((tm, d), lambda i: (i, 0)),
        out_shape=jax.ShapeDtypeStruct((t, d), F32),
        scratch_shapes=[pltpu.VMEM((tm, d), F32), pltpu.VMEM((tm, d), F32), pltpu.SemaphoreType.DMA((2,))],
        compiler_params=_params("arbitrary"),
        name="moe_combine",
    )(dest0.reshape(steps, 1, tm), dest1.reshape(steps, 1, tm), ys, x2, gates, mod)


def _moe_ffn(x2, mod, g_ffn, w_router, wg, wu, wd, seq):
    t, d = x2.shape
    h, logits = _router(x2, mod, g_ffn, w_router, seq)
    top_val, top_idx = lax.top_k(logits[:, :N_EXPERTS], TOP_K_EXPERTS)
    gates = jax.nn.softmax(top_val, axis=-1)

    n_assign = t * TOP_K_EXPERTS
    flat_e = top_idx.reshape(-1).astype(jnp.int32)
    onehot = (flat_e[:, None] == jnp.arange(N_EXPERTS, dtype=jnp.int32)[None, :]).astype(jnp.int32)
    rank = jnp.take_along_axis(jnp.cumsum(onehot, axis=0), flat_e[:, None], axis=1)[:, 0] - 1
    counts = jnp.sum(onehot, axis=0)
    padded = (counts + EXPERT_ROWS - 1) // EXPERT_ROWS * EXPERT_ROWS
    pad_end = jnp.cumsum(padded)
    dest = (pad_end - padded)[flat_e] + rank
    n_slots = (n_assign // EXPERT_ROWS + N_EXPERTS) * EXPERT_ROWS
    n_blocks = n_slots // EXPERT_ROWS
    slot_tok = jnp.zeros((n_slots,), jnp.int32).at[dest].set(jnp.arange(n_assign, dtype=jnp.int32) // TOP_K_EXPERTS)
    block_start = jnp.arange(n_blocks, dtype=jnp.int32) * EXPERT_ROWS
    block_expert = jnp.minimum(jnp.searchsorted(pad_end, block_start, side="right"), N_EXPERTS - 1).astype(jnp.int32)
    n_used = (pad_end[-1:] // EXPERT_ROWS).astype(jnp.int32)

    xs = _gather_rows(h, slot_tok, GATHER_ROWS)
    ys = _expert_ffn(xs, block_expert, n_used, wg, wu, wd)
    dest2 = dest.reshape(t, TOP_K_EXPERTS)
    return _moe_combine(ys, dest2[:, 0], dest2[:, 1], gates, x2, mod, seq)


def _final_norm_kernel(x_ref, g_ref, o_ref):
    x = x_ref[...]
    o_ref[...] = x * lax.rsqrt(jnp.mean(x * x, axis=-1, keepdims=True) + NORM_EPS) * g_ref[...]


def _final_norm(x2, g_final):
    t, d = x2.shape
    tm = min(ROW_TILE, t)
    return pl.pallas_call(
        _final_norm_kernel,
        grid=(t // tm,),
        in_specs=[pl.BlockSpec((tm, d), lambda i: (i, 0)), pl.BlockSpec((1, d), lambda i: (0, 0))],
        out_specs=pl.BlockSpec((tm, d), lambda i: (i, 0)),
        out_shape=jax.ShapeDtypeStruct((t, d), F32),
        compiler_params=_params("arbitrary"),
        name="final_norm",
    )(x2, g_final.reshape(1, d))


def _token_mixer(x2, mod, g_mix, w_in, g_heads, w_out, batch, seq):
    pa, pb, pc, pd, p_idx, p_w = _in_projection(x2, mod, g_mix, _pack_w_in(w_in), seq)
    shape3 = lambda a: a.reshape(batch, seq, a.shape[-1])
    gh = g_heads.reshape(4, HEADS_PER_MIXER, HEAD_DIM)
    mixed = (_stick_breaking_mixer(shape3(pa), gh[0]),
             _dsa_mixer(shape3(pb), shape3(p_idx), shape3(p_w), gh[1]),
             _dilated_mixer(shape3(pc), gh[2]),
             _moba_mixer(shape3(pd), gh[3]))
    return _out_projection(mixed, w_out.astype(BF16), x2, mod, seq)


def kernel(x, c, w_ada, b_ada, g_mix, w_in, g_heads, w_out, g_ffn, w_ff_gate, w_ff_up, w_ff_down, w_router, w_exp_gate, w_exp_up, w_exp_down, g_final):
    batch, seq, d = x.shape
    depth = w_ada.shape[0]
    mods = _ada_modulation(c, w_ada, b_ada)
    x2 = x.reshape(batch * seq, d)
    for layer in range(depth):
        mod = mods[layer]
        x2 = _token_mixer(x2, mod, g_mix[layer], w_in[layer], g_heads[layer], w_out[layer], batch, seq)
        i = layer // 2
        if layer % 2 == 0:
            x2 = _dense_ffn(x2, mod, g_ffn[layer], w_ff_gate[i], w_ff_up[i], w_ff_down[i], seq)
        else:
            x2 = _moe_ffn(x2, mod, g_ffn[layer], w_router[i], w_exp_gate[i], w_exp_up[i], w_exp_down[i], seq)
    return _final_norm(x2, g_final).reshape(batch, seq, d)
```

```python
import functools

import numpy as np
import jax
import jax.numpy as jnp
from jax import lax
from jax.experimental import pallas as pl
from jax.experimental.pallas import tpu as pltpu

F32 = jnp.float32
BF16 = jnp.bfloat16

HEAD_DIM = 64
HEADS_PER_MIXER = 4
MIXER_WIDTH = HEADS_PER_MIXER * HEAD_DIM
QKV_WIDTH = 3 * MIXER_WIDTH
IDX_HEADS = 8
IDX_DIM = 64
DSA_TOPK = 256
DILATIONS = (1, 4, 16)
DILATED_STEPS = 128
MOBA_BLOCK = 256
MOBA_TOPK = 3
N_EXPERTS = 8
TOP_K_EXPERTS = 2
NORM_EPS = 1e-6
ATTN_SCALE = HEAD_DIM ** -0.5

LANES = 128
Q_TILE = 128
ROW_TILE = 512
FFN_TILE = 1408
EXPERT_ROWS = 512
EXPERT_FF_TILE = 896
GATHER_ROWS = 512
VMEM_LIMIT = 56 * 1024 * 1024
MASKED = -1e30
SB_UNDERFLOW = 104.0

_NT = (((1,), (1,)), ((), ()))
_TN = (((0,), (0,)), ((), ()))


def _alibi_slopes(mixer_pos):
    idx = np.arange(HEADS_PER_MIXER, dtype=np.float32) * 3 + (mixer_pos + 1)
    return tuple(float(s) for s in np.exp2(-8.0 * idx / 12.0).astype(np.float32))


def _params(*semantics):
    return pltpu.CompilerParams(dimension_semantics=semantics, vmem_limit_bytes=VMEM_LIMIT)


def _modulated_norm(x, gain, shift, scale):
    y = x * lax.rsqrt(jnp.mean(x * x, axis=-1, keepdims=True) + NORM_EPS) * gain
    return y * (1.0 + scale) + shift


def _head_norm(acc, gain):
    return acc * lax.rsqrt(jnp.mean(acc * acc, axis=-1, keepdims=True) + NORM_EPS) * gain


def _head_cols(h):
    return slice(h * HEAD_DIM, (h + 1) * HEAD_DIM)


def _ada_kernel(c_ref, w_ref, b_ref, o_ref):
    c = c_ref[...]
    cond = c / (1.0 + jnp.exp(-c))
    o_ref[0, 0] = jnp.dot(cond, w_ref[0], preferred_element_type=F32,
                          precision=lax.Precision.HIGHEST) + b_ref[0, 0]


def _ada_modulation(c, w_ada, b_ada):
    depth, d, _ = w_ada.shape
    b = c.shape[0]
    out = pl.pallas_call(
        _ada_kernel,
        grid=(depth, 6),
        in_specs=[pl.BlockSpec((b, d), lambda l, k: (0, 0)),
                  pl.BlockSpec((1, d, d), lambda l, k: (l, 0, k)),
                  pl.BlockSpec((1, 1, 1, d), lambda l, k: (l, k, 0, 0))],
        out_specs=pl.BlockSpec((1, 1, b, d), lambda l, k: (l, k, 0, 0)),
        out_shape=jax.ShapeDtypeStruct((depth, 6, b, d), F32),
        compiler_params=_params("arbitrary", "arbitrary"),
        name="ada_modulation",
    )(c, w_ada, b_ada.reshape(depth, 6, 1, d))
    return out.transpose(0, 2, 1, 3)


IN_WIDTHS = (QKV_WIDTH, QKV_WIDTH, QKV_WIDTH, QKV_WIDTH, IDX_HEADS * IDX_DIM + LANES, LANES)


def _in_proj_kernel(x_ref, mod_ref, g_ref, w_ref, oa, ob, oc, od, oidx, ow):
    h = _modulated_norm(x_ref[...], g_ref[...], mod_ref[0, 0:1, :], mod_ref[0, 1:2, :]).astype(BF16)
    off = 0
    for o_ref, width in zip((oa, ob, oc, od, oidx, ow), IN_WIDTHS):
        o_ref[...] = jnp.dot(h, w_ref[:, off:off + width], preferred_element_type=F32).astype(o_ref.dtype)
        off += width


def _pack_w_in(w_in):
    d = w_in.shape[0]
    n_qkv = 4 * QKV_WIDTH
    n_qi = IDX_HEADS * IDX_DIM
    z = lambda n: jnp.zeros((d, n), w_in.dtype)
    return jnp.concatenate([w_in[:, :n_qkv + n_qi + IDX_DIM], z(LANES - IDX_DIM),
                            w_in[:, n_qkv + n_qi + IDX_DIM:], z(LANES - IDX_HEADS)], axis=1).astype(BF16)


def _in_projection(x2, mod, g_mix, w_packed, seq):
    t, d = x2.shape
    tm = min(ROW_TILE, seq)
    per_batch = seq // tm
    dtypes = (BF16, BF16, BF16, BF16, BF16, F32)
    return pl.pallas_call(
        _in_proj_kernel,
        grid=(t // tm,),
        in_specs=[pl.BlockSpec((tm, d), lambda i: (i, 0)),
                  pl.BlockSpec((1, 6, d), lambda i: (i // per_batch, 0, 0)),
                  pl.BlockSpec((1, d), lambda i: (0, 0)),
                  pl.BlockSpec(w_packed.shape, lambda i: (0, 0))],
        out_specs=[pl.BlockSpec((tm, w), lambda i: (i, 0)) for w in IN_WIDTHS],
        out_shape=[jax.ShapeDtypeStruct((t, w), dt) for w, dt in zip(IN_WIDTHS, dtypes)],
        compiler_params=_params("arbitrary"),
        name="in_projection",
    )(x2, mod, g_mix.reshape(1, d), w_packed)


def _sb_kernel(q_ref, k_ref, v_ref, g_ref, o_ref):
    tq = q_ref.shape[1]
    i = pl.program_id(1)
    row = lax.broadcasted_iota(jnp.int32, (tq, 1), 0)
    lane = lax.broadcasted_iota(jnp.int32, (1, tq), 1)
    later = (lax.broadcasted_iota(jnp.int32, (tq, tq), 0) > lax.broadcasted_iota(jnp.int32, (tq, tq), 1))
    later = jnp.where(later, 1.0, 0.0).astype(BF16)

    qs = [q_ref[0, :, _head_cols(h)] * ATTN_SCALE for h in range(HEADS_PER_MIXER)]

    def block(h, start, tail, past=None):
        cols = _head_cols(h)
        k = k_ref[0, pl.ds(start, tq), cols]
        v = v_ref[0, pl.ds(start, tq), cols]
        z = lax.dot_general(qs[h], k, _NT, preferred_element_type=F32)
        softplus = jnp.maximum(z, 0.0) + jnp.log(1.0 + jnp.exp(-jnp.abs(z)))
        log_1m = -softplus if past is None else jnp.where(past, -softplus, 0.0)
        hi = log_1m.astype(BF16)
        lo = (log_1m - hi.astype(F32)).astype(BF16)
        after = (jnp.dot(hi, later, preferred_element_type=F32)
                 + jnp.dot(lo, later, preferred_element_type=F32) + tail)
        a = jnp.exp(z - softplus + after)
        if past is not None:
            a = jnp.where(past, a, 0.0)
        return (tail + jnp.sum(log_1m, axis=1, keepdims=True),
                jnp.dot(a.astype(BF16), v, preferred_element_type=F32))

    diag = pl.multiple_of(i * tq, tq)
    state = tuple(block(h, diag, jnp.zeros((tq, 1), F32), past=lane < row) for h in range(HEADS_PER_MIXER))

    def body(carry):
        j, state = carry
        start = pl.multiple_of(j * tq, tq)
        new_state = []
        for h in range(HEADS_PER_MIXER):
            tail, acc = state[h]
            tail, av = block(h, start, tail)
            new_state.append((tail, acc + av))
        return j - 1, tuple(new_state)

    def cond(carry):
        j, state = carry
        worst = functools.reduce(jnp.maximum, [tail for tail, _ in state])
        return jnp.logical_and(j >= 0, jnp.max(worst) > -SB_UNDERFLOW)

    _, state = lax.while_loop(cond, body, (i - 1, state))
    for h in range(HEADS_PER_MIXER):
        o_ref[0, :, _head_cols(h)] = _head_norm(state[h][1], g_ref[h:h + 1, :]).astype(o_ref.dtype)


def _stick_breaking_mixer(p, g_heads):
    b, seq, _ = p.shape
    tq = min(Q_TILE, seq)
    return pl.pallas_call(
        _sb_kernel,
        grid=(b, seq // tq),
        in_specs=[pl.BlockSpec((1, tq, MIXER_WIDTH), lambda bi, i: (bi, i, 0)),
                  pl.BlockSpec((1, seq, MIXER_WIDTH), lambda bi, i: (bi, 0, 1)),
                  pl.BlockSpec((1, seq, MIXER_WIDTH), lambda bi, i: (bi, 0, 2)),
                  pl.BlockSpec((HEADS_PER_MIXER, HEAD_DIM), lambda bi, i: (0, 0))],
        out_specs=pl.BlockSpec((1, tq, MIXER_WIDTH), lambda bi, i: (bi, i, 0)),
        out_shape=jax.ShapeDtypeStruct((b, seq, MIXER_WIDTH), BF16),
        compiler_params=_params("arbitrary", "arbitrary"),
        name="stick_breaking",
    )(p, p, p, g_heads)


def _dsa_kernel(q_ref, k_ref, v_ref, qi_ref, ki_ref, w_ref, g_ref, o_ref, sc_ref, *, slopes):
    tq = q_ref.shape[1]
    tk = sc_ref.shape[1]
    i = pl.program_id(1)
    n_blocks = ((i + 1) * tq + tk - 1) // tk
    qpos = i * tq + lax.broadcasted_iota(jnp.int32, (1, tq), 1)
    key_row = lax.broadcasted_iota(jnp.int32, (tk, 1), 0)
    neg_inf = float("-inf")

    w_t = w_ref[0].T * (IDX_HEADS ** -0.5 * IDX_DIM ** -0.5)
    w_rows = [w_t[h:h + 1, :] for h in range(IDX_HEADS)]
    qi = qi_ref[0]

    def score_body(j, carry):
        start = pl.multiple_of(j * tk, tk)
        ki = ki_ref[0, pl.ds(start, tk), 0:IDX_DIM]
        sc = jnp.zeros((tk, tq), F32)
        for h in range(IDX_HEADS):
            x = lax.dot_general(ki, qi[:, h * IDX_DIM:(h + 1) * IDX_DIM], _NT, preferred_element_type=F32)
            sc = sc + w_rows[h] * jnp.maximum(x, 0.0)
        sc_ref[j] = jnp.where((start + key_row) <= qpos, sc + 0.0, neg_inf)
        return carry

    lax.fori_loop(0, n_blocks, score_body, 0)

    def count(pred):
        def body(j, cnt):
            c = jnp.where(pred(sc_ref[j]), 1.0, 0.0)
            return cnt + jnp.sum(c.reshape(tk // 8, 8, tq), axis=0)
        cnt = lax.fori_loop(0, n_blocks, body, jnp.zeros((8, tq), F32))
        return jnp.sum(cnt, axis=0, keepdims=True)

    int_min = jnp.int32(-2 ** 31)

    def ordered_to_float(u):
        key = u ^ int_min
        bits = jnp.where(key >= 0, key, key ^ jnp.int32(0x7FFFFFFF))
        return lax.bitcast_convert_type(bits, F32)

    def search_body(step, prefix):
        cand = prefix | jnp.left_shift(jnp.int32(1), 31 - step)
        cand_f = ordered_to_float(cand)
        n_ge = count(lambda s: s >= cand_f)
        return jnp.where(n_ge >= DSA_TOPK, cand, prefix)

    prefix = lax.fori_loop(0, 32, search_body, jnp.zeros((1, tq), jnp.int32))
    keep_all = (qpos + 1) <= DSA_TOPK
    tau = jnp.where(keep_all, neg_inf, ordered_to_float(prefix))
    n_gt = count(lambda s: s > tau)
    n_ties = jnp.where(keep_all, 0.0, DSA_TOPK - n_gt)

    earlier = (lax.broadcasted_iota(jnp.int32, (tk, tk), 1) < lax.broadcasted_iota(jnp.int32, (tk, tk), 0))
    earlier = jnp.where(earlier, 1.0, 0.0).astype(BF16)
    qs = [q_ref[0, :, _head_cols(h)] * ATTN_SCALE for h in range(HEADS_PER_MIXER)]
    key_bias = [slopes[h] * lax.broadcasted_iota(jnp.int32, (tk, tq), 0).astype(F32)
                for h in range(HEADS_PER_MIXER)]

    def attn_body(j, carry):
        ties_seen, stats = carry
        start = pl.multiple_of(j * tk, tk)
        sc = sc_ref[j]
        tie = jnp.where(sc == tau, 1.0, 0.0)
        rank = jnp.dot(earlier, tie.astype(BF16), preferred_element_type=F32) + ties_seen
        sel = jnp.where(sc > tau, 1.0, jnp.where(rank < n_ties, tie, 0.0)) > 0.5
        ties_seen = ties_seen + jnp.sum(tie, axis=0, keepdims=True)
        block_pos = (j * tk - i * tq).astype(F32)
        new_stats = []
        for h in range(HEADS_PER_MIXER):
            m, l, acc = stats[h]
            cols = _head_cols(h)
            k = k_ref[0, pl.ds(start, tk), cols]
            v = v_ref[0, pl.ds(start, tk), cols]
            s = lax.dot_general(k, qs[h], _NT, preferred_element_type=F32) + key_bias[h]
            s = jnp.where(sel, s, MASKED)
            m_loc = jnp.max(s, axis=0, keepdims=True)
            p = jnp.exp(s - m_loc)
            l_blk = jnp.sum(p, axis=0, keepdims=True)
            pv_blk = lax.dot_general(v, p.astype(BF16), _TN, preferred_element_type=F32)
            m_blk = m_loc + slopes[h] * block_pos
            m_new = jnp.maximum(m, m_blk)
            w_old = jnp.exp(m - m_new)
            w_blk = jnp.where(m_loc > 0.5 * MASKED, jnp.exp(m_blk - m_new), 0.0)
            new_stats.append((m_new, w_old * l + w_blk * l_blk, w_old * acc + w_blk * pv_blk))
        return ties_seen, tuple(new_stats)

    init = tuple((jnp.full((1, tq), MASKED, F32), jnp.zeros((1, tq), F32), jnp.zeros((HEAD_DIM, tq), F32))
                 for _ in range(HEADS_PER_MIXER))
    _, stats = lax.fori_loop(0, n_blocks, attn_body, (jnp.zeros((1, tq), F32), init))
    outs = []
    for h in range(HEADS_PER_MIXER):
        _, l, acc = stats[h]
        o = acc / l
        outs.append(o * lax.rsqrt(jnp.mean(o * o, axis=0, keepdims=True) + NORM_EPS) * g_ref[:, h:h + 1])
    o_ref[0] = jnp.concatenate(outs, axis=0).T.astype(o_ref.dtype)


def _dsa_mixer(p, p_idx, p_w, g_heads):
    b, seq, _ = p.shape
    tq = min(2 * Q_TILE, seq)
    tk = tq
    qi_width = IDX_HEADS * IDX_DIM
    return pl.pallas_call(
        functools.partial(_dsa_kernel, slopes=_alibi_slopes(0)),
        grid=(b, seq // tq),
        in_specs=[pl.BlockSpec((1, tq, MIXER_WIDTH), lambda bi, i: (bi, i, 0)),
                  pl.BlockSpec((1, seq, MIXER_WIDTH), lambda bi, i: (bi, 0, 1)),
                  pl.BlockSpec((1, seq, MIXER_WIDTH), lambda bi, i: (bi, 0, 2)),
                  pl.BlockSpec((1, tq, qi_width), lambda bi, i: (bi, i, 0)),
                  pl.BlockSpec((1, seq, LANES), lambda bi, i: (bi, 0, qi_width // LANES)),
                  pl.BlockSpec((1, tq, LANES), lambda bi, i: (bi, i, 0)),
                  pl.BlockSpec((HEAD_DIM, HEADS_PER_MIXER), lambda bi, i: (0, 0))],
        out_specs=pl.BlockSpec((1, tq, MIXER_WIDTH), lambda bi, i: (bi, i, 0)),
        out_shape=jax.ShapeDtypeStruct((b, seq, MIXER_WIDTH), BF16),
        scratch_shapes=[pltpu.VMEM((seq // tk, tk, tq), F32)],
        compiler_params=_params("arbitrary", "arbitrary"),
        name="dsa",
    )(p, p, p, p_idx, p_idx, p_w, g_heads.T)


def _band_kernel(q_ref, kp_ref, kc_ref, vp_ref, vc_ref, o_ref, lse_ref, *, dilation, slopes):
    tq = q_ref.shape[1]
    ui = pl.program_id(2)
    u_q = ui * tq + lax.broadcasted_iota(jnp.int32, (tq, 1), 0)
    u_k = (ui - 1) * tq + lax.broadcasted_iota(jnp.int32, (1, 2 * tq), 1)
    steps = u_q - u_k
    valid = jnp.logical_and(jnp.logical_and(steps >= 0, steps <= DILATED_STEPS), u_k >= 0)
    dist = (steps * dilation).astype(F32)
    lane = lax.broadcasted_iota(jnp.int32, (1, LANES), 1)
    lse_all = jnp.zeros((tq, LANES), F32)
    for h in range(HEADS_PER_MIXER):
        cols = _head_cols(h)
        k = jnp.concatenate([kp_ref[0, :, cols], kc_ref[0, :, cols]], axis=0)
        v = jnp.concatenate([vp_ref[0, :, cols], vc_ref[0, :, cols]], axis=0)
        s = lax.dot_general(q_ref[0, :, cols], k, _NT, preferred_element_type=F32) * ATTN_SCALE - slopes[h] * dist
        s = jnp.where(valid, s, MASKED)
        m = jnp.max(s, axis=1, keepdims=True)
        e = jnp.where(valid, jnp.exp(s - m), 0.0)
        den = jnp.sum(e, axis=1, keepdims=True)
        o_ref[0, :, cols] = jnp.dot(e.astype(BF16), v, preferred_element_type=F32) / den
        lse_all = lse_all + jnp.where(lane == h, m + jnp.log(den), 0.0)
    lse_ref[0] = lse_all


def _dilated_branch(p, dilation, slopes):
    b, seq, _ = p.shape
    classes = dilation
    length = seq // dilation
    tq = min(Q_TILE, length)
    view = p.reshape(b, length, classes * QKV_WIDTH)
    spec = lambda part, prev: pl.BlockSpec(
        (1, tq, MIXER_WIDTH),
        (lambda bi, c, ui: (bi, jnp.maximum(ui - 1, 0), c * 3 + part)) if prev
        else (lambda bi, c, ui: (bi, ui, c * 3 + part)))
    out, lse = pl.pallas_call(
        functools.partial(_band_kernel, dilation=dilation, slopes=slopes),
        grid=(b, classes, length // tq),
        in_specs=[spec(0, False), spec(1, True), spec(1, False), spec(2, True), spec(2, False)],
        out_specs=[pl.BlockSpec((1, tq, MIXER_WIDTH), lambda bi, c, ui: (bi, ui, c)),
                   pl.BlockSpec((1, tq, LANES), lambda bi, c, ui: (bi, ui, c))],
        out_shape=[jax.ShapeDtypeStruct((b, length, classes * MIXER_WIDTH), F32),
                   jax.ShapeDtypeStruct((b, length, classes * LANES), F32)],
        compiler_params=_params("arbitrary", "arbitrary", "arbitrary"),
        name=f"dilated_r{dilation}",
    )(view, view, view, view, view)
    return out.reshape(b, seq, MIXER_WIDTH), lse.reshape(b, seq, LANES)


def _dilated_merge_kernel(o1, o2, o3, l1, l2, l3, g_ref, o_ref):
    for h in range(HEADS_PER_MIXER):
        cols = _head_cols(h)
        lses = [l[0][:, h:h + 1] for l in (l1, l2, l3)]
        top = jnp.maximum(jnp.maximum(lses[0], lses[1]), lses[2])
        wts = [jnp.exp(l - top) for l in lses]
        mixed = (wts[0] * o1[0, :, cols] + wts[1] * o2[0, :, cols] + wts[2] * o3[0, :, cols]) / (wts[0] + wts[1] + wts[2])
        o_ref[0, :, cols] = _head_norm(mixed, g_ref[h:h + 1, :]).astype(o_ref.dtype)


def _dilated_mixer(p, g_heads):
    b, seq, _ = p.shape
    slopes = _alibi_slopes(1)
    branches = [_dilated_branch(p, r, slopes) for r in DILATIONS]
    tq = min(ROW_TILE, seq)
    o_spec = pl.BlockSpec((1, tq, MIXER_WIDTH), lambda bi, i: (bi, i, 0))
    l_spec = pl.BlockSpec((1, tq, LANES), lambda bi, i: (bi, i, 0))
    return pl.pallas_call(
        _dilated_merge_kernel,
        grid=(b, seq // tq),
        in_specs=[o_spec] * 3 + [l_spec] * 3 + [pl.BlockSpec((HEADS_PER_MIXER, HEAD_DIM), lambda bi, i: (0, 0))],
        out_specs=o_spec,
        out_shape=jax.ShapeDtypeStruct((b, seq, MIXER_WIDTH), BF16),
        compiler_params=_params("arbitrary", "arbitrary"),
        name="dilated_merge",
    )(*[o for o, _ in branches], *[l for _, l in branches], g_heads)


def _moba_kernel(q_ref, k_ref, v_ref, g_ref, o_ref, kmean_ref, *, slopes):
    tq = q_ref.shape[1]
    n_kv = k_ref.shape[1] // MOBA_BLOCK
    own = pl.program_id(1)

    @pl.when(own == 0)
    def _():
        kmean_ref[...] = jnp.zeros_like(kmean_ref)
        for n in range(n_kv):
            blk = k_ref[0, n * MOBA_BLOCK:(n + 1) * MOBA_BLOCK, :].astype(F32)
            kmean_ref[n:n + 1, :] = jnp.sum(blk, axis=0, keepdims=True) * (1.0 / MOBA_BLOCK)

    row = lax.broadcasted_iota(jnp.int32, (tq, 1), 0)
    lane = lax.broadcasted_iota(jnp.int32, (1, MOBA_BLOCK), 1)
    blk_id = lax.broadcasted_iota(jnp.int32, (1, LANES), 1)
    blk_f = blk_id.astype(F32)
    neg_inf = float("-inf")

    qs, chosen = [], []
    for h in range(HEADS_PER_MIXER):
        q = q_ref[0, :, _head_cols(h)]
        gate = lax.dot_general(q.astype(F32), kmean_ref[:, _head_cols(h)], _NT, preferred_element_type=F32,
                               precision=lax.Precision.HIGHEST)
        gate = jnp.where(blk_id < own, gate, neg_inf)
        picks = jnp.zeros((tq, LANES), F32)
        for _ in range(MOBA_TOPK):
            top = jnp.max(gate, axis=1, keepdims=True)
            is_top = jnp.logical_and(gate == top, top > neg_inf)
            first = jnp.min(jnp.where(is_top, blk_f, float(LANES)), axis=1, keepdims=True)
            pick = blk_f == first
            picks = jnp.where(pick, 1.0, picks)
            gate = jnp.where(pick, neg_inf, gate)
        chosen.append(picks)
        qs.append(q * ATTN_SCALE)

    def block_softmax(h, start, key_pos, keep=None):
        cols = _head_cols(h)
        k = k_ref[0, pl.ds(start, MOBA_BLOCK), cols]
        v = v_ref[0, pl.ds(start, MOBA_BLOCK), cols]
        s = lax.dot_general(qs[h], k, _NT, preferred_element_type=F32) + slopes[h] * key_pos
        if keep is not None:
            s = jnp.where(keep, s, MASKED)
        m = jnp.max(s, axis=1, keepdims=True)
        p = jnp.exp(s - m)
        return m, jnp.sum(p, axis=1, keepdims=True), jnp.dot(p.astype(BF16), v, preferred_element_type=F32)

    own_start = pl.multiple_of(own * MOBA_BLOCK, MOBA_BLOCK)
    stats = tuple(block_softmax(h, own_start, lane.astype(F32), keep=lane <= row) for h in range(HEADS_PER_MIXER))

    def body(j, stats):
        start = pl.multiple_of(j * MOBA_BLOCK, MOBA_BLOCK)
        key_pos = (lane + (j - own) * MOBA_BLOCK).astype(F32)
        merged = []
        for h in range(HEADS_PER_MIXER):
            m_blk, l_blk, pv_blk = block_softmax(h, start, key_pos)
            picked = jnp.max(jnp.where(blk_id == j, chosen[h], 0.0), axis=1, keepdims=True) > 0.5
            m_blk = jnp.where(picked, m_blk, MASKED)
            m, l, acc = stats[h]
            m_new = jnp.maximum(m, m_blk)
            w_old = jnp.exp(m - m_new)
            w_blk = jnp.exp(m_blk - m_new)
            merged.append((m_new, w_old * l + w_blk * l_blk, w_old * acc + w_blk * pv_blk))
        return tuple(merged)

    stats = lax.fori_loop(0, own, body, stats)
    for h in range(HEADS_PER_MIXER):
        _, l, acc = stats[h]
        o_ref[0, :, _head_cols(h)] = _head_norm(acc / l, g_ref[h:h + 1, :]).astype(o_ref.dtype)


def _moba_mixer(p, g_heads):
    b, seq, _ = p.shape
    tq = MOBA_BLOCK
    return pl.pallas_call(
        functools.partial(_moba_kernel, slopes=_alibi_slopes(2)),
        grid=(b, seq // tq),
        in_specs=[pl.BlockSpec((1, tq, MIXER_WIDTH), lambda bi, i: (bi, i, 0)),
                  pl.BlockSpec((1, seq, MIXER_WIDTH), lambda bi, i: (bi, 0, 1)),
                  pl.BlockSpec((1, seq, MIXER_WIDTH), lambda bi, i: (bi, 0, 2)),
                  pl.BlockSpec((HEADS_PER_MIXER, HEAD_DIM), lambda bi, i: (0, 0))],
        out_specs=pl.BlockSpec((1, tq, MIXER_WIDTH), lambda bi, i: (bi, i, 0)),
        out_shape=jax.ShapeDtypeStruct((b, seq, MIXER_WIDTH), BF16),
        scratch_shapes=[pltpu.VMEM((LANES, MIXER_WIDTH), F32)],
        compiler_params=_params("arbitrary", "arbitrary"),
        name="moba",
    )(p, p, p, g_heads)


def _out_proj_kernel(oa, ob, oc, od, w_ref, x_ref, mod_ref, o_ref):
    acc = jnp.zeros(x_ref.shape, F32)
    for m, o in enumerate((oa, ob, oc, od)):
        acc = acc + jnp.dot(o[...], w_ref[m * MIXER_WIDTH:(m + 1) * MIXER_WIDTH, :], preferred_element_type=F32)
    o_ref[...] = x_ref[...] + mod_ref[0, 2:3, :] * acc


def _out_projection(mixed, w_out, x2, mod, seq):
    t, d = x2.shape
    tm = min(ROW_TILE, seq)
    per_batch = seq // tm
    o_spec = pl.BlockSpec((tm, MIXER_WIDTH), lambda i: (i, 0))
    return pl.pallas_call(
        _out_proj_kernel,
        grid=(t // tm,),
        in_specs=[o_spec] * 4 + [pl.BlockSpec(w_out.shape, lambda i: (0, 0)),
                                 pl.BlockSpec((tm, d), lambda i: (i, 0)),
                                 pl.BlockSpec((1, 6, d), lambda i: (i // per_batch, 0, 0))],
        out_specs=pl.BlockSpec((tm, d), lambda i: (i, 0)),
        out_shape=jax.ShapeDtypeStruct((t, d), F32),
        compiler_params=_params("arbitrary"),
        name="out_projection",
    )(*[o.reshape(t, MIXER_WIDTH) for o in mixed], w_out, x2, mod)


def _ffn_kernel(x_ref, mod_ref, g_ref, wg_ref, wu_ref, wd_ref, o_ref, h_ref, acc_ref):
    f = pl.program_id(1)

    @pl.when(f == 0)
    def _():
        h_ref[...] = _modulated_norm(x_ref[...], g_ref[...], mod_ref[0, 3:4, :], mod_ref[0, 4:5, :]).astype(BF16)
        acc_ref[...] = jnp.zeros_like(acc_ref)

    h = h_ref[...]
    gate = jnp.dot(h, wg_ref[...], preferred_element_type=F32)
    up = jnp.dot(h, wu_ref[...], preferred_element_type=F32)
    act = (gate / (1.0 + jnp.exp(-gate)) * up).astype(BF16)
    acc_ref[...] += jnp.dot(act, wd_ref[...], preferred_element_type=F32)

    @pl.when(f == pl.num_programs(1) - 1)
    def _():
        o_ref[...] = x_ref[...] + mod_ref[0, 5:6, :] * acc_ref[...]


def _dense_ffn(x2, mod, g_ffn, wg, wu, wd, seq):
    t, d = x2.shape
    d_ff = wg.shape[1]
    tm = min(ROW_TILE, seq)
    tf = FFN_TILE if d_ff % FFN_TILE == 0 else d_ff
    per_batch = seq // tm
    return pl.pallas_call(
        _ffn_kernel,
        grid=(t // tm, d_ff // tf),
        in_specs=[pl.BlockSpec((tm, d), lambda i, f: (i, 0)),
                  pl.BlockSpec((1, 6, d), lambda i, f: (i // per_batch, 0, 0)),
                  pl.BlockSpec((1, d), lambda i, f: (0, 0)),
                  pl.BlockSpec((d, tf), lambda i, f: (0, f)),
                  pl.BlockSpec((d, tf), lambda i, f: (0, f)),
                  pl.BlockSpec((tf, d), lambda i, f: (f, 0))],
        out_specs=pl.BlockSpec((tm, d), lambda i, f: (i, 0)),
        out_shape=jax.ShapeDtypeStruct((t, d), F32),
        scratch_shapes=[pltpu.VMEM((tm, d), BF16), pltpu.VMEM((tm, d), F32)],
        compiler_params=_params("arbitrary", "arbitrary"),
        name="dense_ffn",
    )(x2, mod, g_ffn.reshape(1, d), wg.astype(BF16), wu.astype(BF16), wd.astype(BF16))


def _router_kernel(x_ref, mod_ref, g_ref, wr_ref, h_ref, logit_ref):
    h = _modulated_norm(x_ref[...], g_ref[...], mod_ref[0, 3:4, :], mod_ref[0, 4:5, :])
    h_ref[...] = h
    logit_ref[...] = jnp.dot(h, wr_ref[...], preferred_element_type=F32, precision=lax.Precision.HIGHEST)


def _router(x2, mod, g_ffn, w_router, seq):
    t, d = x2.shape
    tm = min(ROW_TILE, seq)
    per_batch = seq // tm
    wr = jnp.zeros((d, LANES), F32).at[:, :N_EXPERTS].set(w_router.astype(F32))
    return pl.pallas_call(
        _router_kernel,
        grid=(t // tm,),
        in_specs=[pl.BlockSpec((tm, d), lambda i: (i, 0)),
                  pl.BlockSpec((1, 6, d), lambda i: (i // per_batch, 0, 0)),
                  pl.BlockSpec((1, d), lambda i: (0, 0)),
                  pl.BlockSpec((d, LANES), lambda i: (0, 0))],
        out_specs=[pl.BlockSpec((tm, d), lambda i: (i, 0)), pl.BlockSpec((tm, LANES), lambda i: (i, 0))],
        out_shape=[jax.ShapeDtypeStruct((t, d), F32), jax.ShapeDtypeStruct((t, LANES), F32)],
        compiler_params=_params("arbitrary"),
        name="moe_router",
    )(x2, mod, g_ffn.reshape(1, d), wr)


def _row_copy(src_hbm, row, dst_ref, r, sem):
    return pltpu.make_async_copy(src_hbm.at[pl.ds(row, 1), :], dst_ref.at[pl.ds(r, 1), :], sem)


def _gather_kernel(idx_ref, src_hbm, o_ref, sem):
    rows = o_ref.shape[0]

    def start(r, c):
        _row_copy(src_hbm, idx_ref[0, 0, r], o_ref, r, sem).start()
        return c

    def wait(r, c):
        _row_copy(src_hbm, idx_ref[0, 0, r], o_ref, r, sem).wait()
        return c

    lax.fori_loop(0, rows, start, 0)
    lax.fori_loop(0, rows, wait, 0)


def _gather_rows(src, idx, rows_per_step):
    n = idx.shape[0]
    d = src.shape[1]
    steps = n // rows_per_step
    return pl.pallas_call(
        _gather_kernel,
        grid=(steps,),
        in_specs=[pl.BlockSpec((1, 1, rows_per_step), lambda i: (i, 0, 0), memory_space=pltpu.SMEM),
                  pl.BlockSpec(memory_space=pl.ANY)],
        out_specs=pl.BlockSpec((rows_per_step, d), lambda i: (i, 0)),
        out_shape=jax.ShapeDtypeStruct((n, d), src.dtype),
        scratch_shapes=[pltpu.SemaphoreType.DMA(())],
        compiler_params=_params("arbitrary"),
        name="gather_rows",
    )(idx.reshape(steps, 1, rows_per_step), src)


def _expert_kernel(be_ref, used_ref, xs_ref, wg_ref, wu_ref, wd_ref, o_ref, xb_ref, acc_ref):
    m = pl.program_id(0)
    f = pl.program_id(1)
    live = m < used_ref[0]

    @pl.when(jnp.logical_and(live, f == 0))
    def _():
        xb_ref[...] = xs_ref[...].astype(BF16)
        acc_ref[...] = jnp.zeros_like(acc_ref)

    @pl.when(live)
    def _():
        h = xb_ref[...]
        gate = jnp.dot(h, wg_ref[0], preferred_element_type=F32)
        up = jnp.dot(h, wu_ref[0], preferred_element_type=F32)
        act = (gate / (1.0 + jnp.exp(-gate)) * up).astype(BF16)
        acc_ref[...] += jnp.dot(act, wd_ref[0], preferred_element_type=F32)

    @pl.when(f == pl.num_programs(1) - 1)
    def _():
        o_ref[...] = jnp.where(live, acc_ref[...], 0.0)


def _expert_ffn(xs, block_expert, n_used, wg, wu, wd):
    n_slots, d = xs.shape
    d_ff = wg.shape[2]
    tm = EXPERT_ROWS
    tf = EXPERT_FF_TILE if d_ff % EXPERT_FF_TILE == 0 else d_ff
    nf = d_ff // tf

    def fcol(m, f, used):
        return jnp.where(m < used[0], f, nf - 1)

    grid_spec = pltpu.PrefetchScalarGridSpec(
        num_scalar_prefetch=2,
        grid=(n_slots // tm, nf),
        in_specs=[pl.BlockSpec((tm, d), lambda m, f, be, used: (m, 0)),
                  pl.BlockSpec((1, d, tf), lambda m, f, be, used: (be[m], 0, fcol(m, f, used))),
                  pl.BlockSpec((1, d, tf), lambda m, f, be, used: (be[m], 0, fcol(m, f, used))),
                  pl.BlockSpec((1, tf, d), lambda m, f, be, used: (be[m], fcol(m, f, used), 0))],
        out_specs=pl.BlockSpec((tm, d), lambda m, f, be, used: (m, 0)),
        scratch_shapes=[pltpu.VMEM((tm, d), BF16), pltpu.VMEM((tm, d), F32)])
    return pl.pallas_call(
        _expert_kernel,
        grid_spec=grid_spec,
        out_shape=jax.ShapeDtypeStruct((n_slots, d), F32),
        compiler_params=_params("arbitrary", "arbitrary"),
        name="expert_ffn",
    )(block_expert, n_used, xs, wg.astype(BF16), wu.astype(BF16), wd.astype(BF16))


def _combine_kernel(d0_ref, d1_ref, ys_hbm, x_ref, gates_ref, mod_ref, o_ref, y0_ref, y1_ref, sem):
    rows = o_ref.shape[0]

    def start(r, c):
        _row_copy(ys_hbm, d0_ref[0, 0, r], y0_ref, r, sem.at[0]).start()
        _row_copy(ys_hbm, d1_ref[0, 0, r], y1_ref, r, sem.at[1]).start()
        return c

    def wait(r, c):
        _row_copy(ys_hbm, d0_ref[0, 0, r], y0_ref, r, sem.at[0]).wait()
        _row_copy(ys_hbm, d1_ref[0, 0, r], y1_ref, r, sem.at[1]).wait()
        return c

    lax.fori_loop(0, rows, start, 0)
    lax.fori_loop(0, rows, wait, 0)
    gates = gates_ref[...]
    y = y0_ref[...] * gates[:, 0:1] + y1_ref[...] * gates[:, 1:2]
    o_ref[...] = x_ref[...] + mod_ref[0, 5:6, :] * y


def _moe_combine(ys, dest0, dest1, gates, x2, mod, seq):
    t, d = x2.shape
    tm = min(GATHER_ROWS, seq)
    steps = t // tm
    per_batch = seq // tm
    idx_spec = pl.BlockSpec((1, 1, tm), lambda i: (i, 0, 0), memory_space=pltpu.SMEM)
    return pl.pallas_call(
        _combine_kernel,
        grid=(steps,),
        in_specs=[idx_spec, idx_spec, pl.BlockSpec(memory_space=pl.ANY),
                  pl.BlockSpec((tm, d), lambda i: (i, 0)),
                  pl.BlockSpec((tm, TOP_K_EXPERTS), lambda i: (i, 0)),
                  pl.BlockSpec((1, 6, d), lambda i: (i // per_batch, 0, 0))],
        out_specs=pl.BlockSpec((tm, d), lambda i: (i, 0)),
        out_shape=jax.ShapeDtypeStruct((t, d), F32),
        scratch_shapes=[pltpu.VMEM((tm, d), F32), pltpu.VMEM((tm, d), F32), pltpu.SemaphoreType.DMA((2,))],
        compiler_params=_params("arbitrary"),
        name="moe_combine",
    )(dest0.reshape(steps, 1, tm), dest1.reshape(steps, 1, tm), ys, x2, gates, mod)


def _moe_ffn(x2, mod, g_ffn, w_router, wg, wu, wd, seq):
    t, d = x2.shape
    h, logits = _router(x2, mod, g_ffn, w_router, seq)
    top_val, top_idx = lax.top_k(logits[:, :N_EXPERTS], TOP_K_EXPERTS)
    gates = jax.nn.softmax(top_val, axis=-1)

    n_assign = t * TOP_K_EXPERTS
    flat_e = top_idx.reshape(-1).astype(jnp.int32)
    onehot = (flat_e[:, None] == jnp.arange(N_EXPERTS, dtype=jnp.int32)[None, :]).astype(jnp.int32)
    rank = jnp.take_along_axis(jnp.cumsum(onehot, axis=0), flat_e[:, None], axis=1)[:, 0] - 1
    counts = jnp.sum(onehot, axis=0)
    padded = (counts + EXPERT_ROWS - 1) // EXPERT_ROWS * EXPERT_ROWS
    pad_end = jnp.cumsum(padded)
    dest = (pad_end - padded)[flat_e] + rank
    n_slots = (n_assign // EXPERT_ROWS + N_EXPERTS) * EXPERT_ROWS
    n_blocks = n_slots // EXPERT_ROWS
    slot_tok = jnp.zeros((n_slots,), jnp.int32).at[dest].set(jnp.arange(n_assign, dtype=jnp.int32) // TOP_K_EXPERTS)
    block_start = jnp.arange(n_blocks, dtype=jnp.int32) * EXPERT_ROWS
    block_expert = jnp.minimum(jnp.searchsorted(pad_end, block_start, side="right"), N_EXPERTS - 1).astype(jnp.int32)
    n_used = (pad_end[-1:] // EXPERT_ROWS).astype(jnp.int32)

    xs = _gather_rows(h, slot_tok, GATHER_ROWS)
    ys = _expert_ffn(xs, block_expert, n_used, wg, wu, wd)
    dest2 = dest.reshape(t, TOP_K_EXPERTS)
    return _moe_combine(ys, dest2[:, 0], dest2[:, 1], gates, x2, mod, seq)


def _final_norm_kernel(x_ref, g_ref, o_ref):
    x = x_ref[...]
    o_ref[...] = x * lax.rsqrt(jnp.mean(x * x, axis=-1, keepdims=True) + NORM_EPS) * g_ref[...]


def _final_norm(x2, g_final):
    t, d = x2.shape
    tm = min(ROW_TILE, t)
    return pl.pallas_call(
        _final_norm_kernel,
        grid=(t // tm,),
        in_specs=[pl.BlockSpec((tm, d), lambda i: (i, 0)), pl.BlockSpec((1, d), lambda i: (0, 0))],
        out_specs=pl.BlockSpec((tm, d), lambda i: (i, 0)),
        out_shape=jax.ShapeDtypeStruct((t, d), F32),
        compiler_params=_params("arbitrary"),
        name="final_norm",
    )(x2, g_final.reshape(1, d))


def _token_mixer(x2, mod, g_mix, w_in, g_heads, w_out, batch, seq):
    pa, pb, pc, pd, p_idx, p_w = _in_projection(x2, mod, g_mix, _pack_w_in(w_in), seq)
    shape3 = lambda a: a.reshape(batch, seq, a.shape[-1])
    gh = g_heads.reshape(4, HEADS_PER_MIXER, HEAD_DIM)
    mixed = (_stick_breaking_mixer(shape3(pa), gh[0]),
             _dsa_mixer(shape3(pb), shape3(p_idx), shape3(p_w), gh[1]),
             _dilated_mixer(shape3(pc), gh[2]),
             _moba_mixer(shape3(pd), gh[3]))
    return _out_projection(mixed, w_out.astype(BF16), x2, mod, seq)


def kernel(x, c, w_ada, b_ada, g_mix, w_in, g_heads, w_out, g_ffn, w_ff_gate, w_ff_up, w_ff_down, w_router, w_exp_gate, w_exp_up, w_exp_down, g_final):
    batch, seq, d = x.shape
    depth = w_ada.shape[0]
    mods = _ada_modulation(c, w_ada, b_ada)
    x2 = x.reshape(batch * seq, d)
    for layer in range(depth):
        mod = mods[layer]
        x2 = _token_mixer(x2, mod, g_mix[layer], w_in[layer], g_heads[layer], w_out[layer], batch, seq)
        i = layer // 2
        if layer % 2 == 0:
            x2 = _dense_ffn(x2, mod, g_ffn[layer], w_ff_gate[i], w_ff_up[i], w_ff_down[i], seq)
        else:
            x2 = _moe_ffn(x2, mod, g_ffn[layer], w_router[i], w_exp_gate[i], w_exp_up[i], w_exp_down[i], seq)
    return _final_norm(x2, g_final).reshape(batch, seq, d)
```

```python
import functools

import numpy as np
import jax
import jax.numpy as jnp
from jax import lax
from jax.experimental import pallas as pl
from jax.experimental.pallas import tpu as pltpu

F32 = jnp.float32
BF16 = jnp.bfloat16

HEAD_DIM = 64
HEADS_PER_MIXER = 4
MIXER_WIDTH = HEADS_PER_MIXER * HEAD_DIM
QKV_WIDTH = 3 * MIXER_WIDTH
IDX_HEADS = 8
IDX_DIM = 64
DSA_TOPK = 256
DILATIONS = (1, 4, 16)
DILATED_STEPS = 128
MOBA_BLOCK = 256
MOBA_TOPK = 3
N_EXPERTS = 8
TOP_K_EXPERTS = 2
NORM_EPS = 1e-6
ATTN_SCALE = HEAD_DIM ** -0.5

LANES = 128
Q_TILE = 128
ROW_TILE = 512
FFN_TILE = 1408
EXPERT_ROWS = 512
EXPERT_FF_TILE = 896
GATHER_ROWS = 512
VMEM_LIMIT = 56 * 1024 * 1024
MASKED = -1e30
SB_UNDERFLOW = 104.0

_NT = (((1,), (1,)), ((), ()))
_TN = (((0,), (0,)), ((), ()))


def _alibi_slopes(mixer_pos):
    idx = np.arange(HEADS_PER_MIXER, dtype=np.float32) * 3 + (mixer_pos + 1)
    return tuple(float(s) for s in np.exp2(-8.0 * idx / 12.0).astype(np.float32))


def _params(*semantics):
    return pltpu.CompilerParams(dimension_semantics=semantics, vmem_limit_bytes=VMEM_LIMIT)


def _modulated_norm(x, gain, shift, scale):
    y = x * lax.rsqrt(jnp.mean(x * x, axis=-1, keepdims=True) + NORM_EPS) * gain
    return y * (1.0 + scale) + shift


def _head_norm(acc, gain):
    return acc * lax.rsqrt(jnp.mean(acc * acc, axis=-1, keepdims=True) + NORM_EPS) * gain


def _head_cols(h):
    return slice(h * HEAD_DIM, (h + 1) * HEAD_DIM)


def _ada_kernel(c_ref, w_ref, b_ref, o_ref):
    c = c_ref[...]
    cond = c / (1.0 + jnp.exp(-c))
    o_ref[0, 0] = jnp.dot(cond, w_ref[0], preferred_element_type=F32,
                          precision=lax.Precision.HIGHEST) + b_ref[0, 0]


def _ada_modulation(c, w_ada, b_ada):
    depth, d, _ = w_ada.shape
    b = c.shape[0]
    out = pl.pallas_call(
        _ada_kernel,
        grid=(depth, 6),
        in_specs=[pl.BlockSpec((b, d), lambda l, k: (0, 0)),
                  pl.BlockSpec((1, d, d), lambda l, k: (l, 0, k)),
                  pl.BlockSpec((1, 1, 1, d), lambda l, k: (l, k, 0, 0))],
        out_specs=pl.BlockSpec((1, 1, b, d), lambda l, k: (l, k, 0, 0)),
        out_shape=jax.ShapeDtypeStruct((depth, 6, b, d), F32),
        compiler_params=_params("arbitrary", "arbitrary"),
        name="ada_modulation",
    )(c, w_ada, b_ada.reshape(depth, 6, 1, d))
    return out.transpose(0, 2, 1, 3)


IN_WIDTHS = (QKV_WIDTH, QKV_WIDTH, QKV_WIDTH, QKV_WIDTH, IDX_HEADS * IDX_DIM + LANES, LANES)


def _in_proj_kernel(x_ref, mod_ref, g_ref, w_ref, oa, ob, oc, od, oidx, ow):
    h = _modulated_norm(x_ref[...], g_ref[...], mod_ref[0, 0:1, :], mod_ref[0, 1:2, :]).astype(BF16)
    off = 0
    for o_ref, width in zip((oa, ob, oc, od, oidx, ow), IN_WIDTHS):
        o_ref[...] = jnp.dot(h, w_ref[:, off:off + width], preferred_element_type=F32).astype(o_ref.dtype)
        off += width


def _pack_w_in(w_in):
    d = w_in.shape[0]
    n_qkv = 4 * QKV_WIDTH
    n_qi = IDX_HEADS * IDX_DIM
    z = lambda n: jnp.zeros((d, n), w_in.dtype)
    return jnp.concatenate([w_in[:, :n_qkv + n_qi + IDX_DIM], z(LANES - IDX_DIM),
                            w_in[:, n_qkv + n_qi + IDX_DIM:], z(LANES - IDX_HEADS)], axis=1).astype(BF16)


def _in_projection(x2, mod, g_mix, w_packed, seq):
    t, d = x2.shape
    tm = min(ROW_TILE, seq)
    per_batch = seq // tm
    dtypes = (BF16, BF16, BF16, BF16, BF16, F32)
    return pl.pallas_call(
        _in_proj_kernel,
        grid=(t // tm,),
        in_specs=[pl.BlockSpec((tm, d), lambda i: (i, 0)),
                  pl.BlockSpec((1, 6, d), lambda i: (i // per_batch, 0, 0)),
                  pl.BlockSpec((1, d), lambda i: (0, 0)),
                  pl.BlockSpec(w_packed.shape, lambda i: (0, 0))],
        out_specs=[pl.BlockSpec((tm, w), lambda i: (i, 0)) for w in IN_WIDTHS],
        out_shape=[jax.ShapeDtypeStruct((t, w), dt) for w, dt in zip(IN_WIDTHS, dtypes)],
        compiler_params=_params("arbitrary"),
        name="in_projection",
    )(x2, mod, g_mix.reshape(1, d), w_packed)


def _sb_kernel(q_ref, k_ref, v_ref, g_ref, o_ref):
    tq = q_ref.shape[1]
    i = pl.program_id(1)
    row = lax.broadcasted_iota(jnp.int32, (tq, 1), 0)
    lane = lax.broadcasted_iota(jnp.int32, (1, tq), 1)
    later = (lax.broadcasted_iota(jnp.int32, (tq, tq), 0) > lax.broadcasted_iota(jnp.int32, (tq, tq), 1))
    later = jnp.where(later, 1.0, 0.0).astype(BF16)

    qs = [q_ref[0, :, _head_cols(h)] * ATTN_SCALE for h in range(HEADS_PER_MIXER)]

    def block(h, start, tail, past=None):
        cols = _head_cols(h)
        k = k_ref[0, pl.ds(start, tq), cols]
        v = v_ref[0, pl.ds(start, tq), cols]
        z = lax.dot_general(qs[h], k, _NT, preferred_element_type=F32)
        softplus = jnp.maximum(z, 0.0) + jnp.log(1.0 + jnp.exp(-jnp.abs(z)))
        log_1m = -softplus if past is None else jnp.where(past, -softplus, 0.0)
        hi = log_1m.astype(BF16)
        lo = (log_1m - hi.astype(F32)).astype(BF16)
        after = (jnp.dot(hi, later, preferred_element_type=F32)
                 + jnp.dot(lo, later, preferred_element_type=F32) + tail)
        a = jnp.exp(z - softplus + after)
        if past is not None:
            a = jnp.where(past, a, 0.0)
        return (tail + jnp.sum(log_1m, axis=1, keepdims=True),
                jnp.dot(a.astype(BF16), v, preferred_element_type=F32))

    diag = pl.multiple_of(i * tq, tq)
    state = tuple(block(h, diag, jnp.zeros((tq, 1), F32), past=lane < row) for h in range(HEADS_PER_MIXER))

    def body(carry):
        j, state = carry
        start = pl.multiple_of(j * tq, tq)
        new_state = []
        for h in range(HEADS_PER_MIXER):
            tail, acc = state[h]
            tail, av = block(h, start, tail)
            new_state.append((tail, acc + av))
        return j - 1, tuple(new_state)

    def cond(carry):
        j, state = carry
        worst = functools.reduce(jnp.maximum, [tail for tail, _ in state])
        return jnp.logical_and(j >= 0, jnp.max(worst) > -SB_UNDERFLOW)

    _, state = lax.while_loop(cond, body, (i - 1, state))
    for h in range(HEADS_PER_MIXER):
        o_ref[0, :, _head_cols(h)] = _head_norm(state[h][1], g_ref[h:h + 1, :]).astype(o_ref.dtype)


def _stick_breaking_mixer(p, g_heads):
    b, seq, _ = p.shape
    tq = min(Q_TILE, seq)
    return pl.pallas_call(
        _sb_kernel,
        grid=(b, seq // tq),
        in_specs=[pl.BlockSpec((1, tq, MIXER_WIDTH), lambda bi, i: (bi, i, 0)),
                  pl.BlockSpec((1, seq, MIXER_WIDTH), lambda bi, i: (bi, 0, 1)),
                  pl.BlockSpec((1, seq, MIXER_WIDTH), lambda bi, i: (bi, 0, 2)),
                  pl.BlockSpec((HEADS_PER_MIXER, HEAD_DIM), lambda bi, i: (0, 0))],
        out_specs=pl.BlockSpec((1, tq, MIXER_WIDTH), lambda bi, i: (bi, i, 0)),
        out_shape=jax.ShapeDtypeStruct((b, seq, MIXER_WIDTH), BF16),
        compiler_params=_params("arbitrary", "arbitrary"),
        name="stick_breaking",
    )(p, p, p, g_heads)


def _dsa_kernel(q_ref, k_ref, v_ref, qi_ref, ki_ref, w_ref, g_ref, o_ref, sc_ref, s_ref, p_ref, *, slopes):
    tq = q_ref.shape[1]
    tk = sc_ref.shape[1]
    i = pl.program_id(1)
    n_blocks = ((i + 1) * tq + tk - 1) // tk
    qpos = i * tq + lax.broadcasted_iota(jnp.int32, (1, tq), 1)
    key_row = lax.broadcasted_iota(jnp.int32, (tk, 1), 0)
    neg_inf = float("-inf")

    w_t = w_ref[0].T * (IDX_HEADS ** -0.5 * IDX_DIM ** -0.5)
    w_rows = [w_t[h:h + 1, :] for h in range(IDX_HEADS)]
    qi = qi_ref[0]

    def score_body(j, carry):
        start = pl.multiple_of(j * tk, tk)
        ki = ki_ref[0, pl.ds(start, tk), 0:IDX_DIM]
        sc = jnp.zeros((tk, tq), F32)
        for h in range(IDX_HEADS):
            x = lax.dot_general(ki, qi[:, h * IDX_DIM:(h + 1) * IDX_DIM], _NT, preferred_element_type=F32)
            sc = sc + w_rows[h] * jnp.maximum(x, 0.0)
        sc_ref[j] = jnp.where((start + key_row) <= qpos, sc + 0.0, neg_inf)
        return carry

    lax.fori_loop(0, n_blocks, score_body, 0)

    def count(pred):
        def body(j, cnt):
            c = jnp.where(pred(sc_ref[j]), 1.0, 0.0)
            return cnt + jnp.sum(c.reshape(tk // 8, 8, tq), axis=0)
        cnt = lax.fori_loop(0, n_blocks, body, jnp.zeros((8, tq), F32))
        return jnp.sum(cnt, axis=0, keepdims=True)

    int_min = jnp.int32(-2 ** 31)

    def ordered_to_float(u):
        key = u ^ int_min
        bits = jnp.where(key >= 0, key, key ^ jnp.int32(0x7FFFFFFF))
        return lax.bitcast_convert_type(bits, F32)

    def search_body(step, carry):
        prefix, n_at_prefix = carry
        cand = prefix | jnp.left_shift(jnp.int32(1), 31 - step)
        cand_f = ordered_to_float(cand)
        n_ge = count(lambda s: s >= cand_f)
        keep = n_ge >= DSA_TOPK
        return jnp.where(keep, cand, prefix), jnp.where(keep, n_ge, n_at_prefix)

    prefix, n_ge_tau = lax.fori_loop(0, 32, search_body,
                                     (jnp.zeros((1, tq), jnp.int32), jnp.zeros((1, tq), F32)))
    keep_all = (qpos + 1) <= DSA_TOPK
    tau = jnp.where(keep_all, float(np.finfo(np.float32).min), ordered_to_float(prefix))
    no_ties = jnp.min(jnp.where(jnp.logical_or(keep_all, n_ge_tau == DSA_TOPK), 1.0, 0.0)) > 0.5

    earlier = (lax.broadcasted_iota(jnp.int32, (tk, tk), 1) < lax.broadcasted_iota(jnp.int32, (tk, tk), 0))
    earlier = jnp.where(earlier, 1.0, 0.0).astype(BF16)
    qs = [q_ref[0, :, _head_cols(h)] * ATTN_SCALE for h in range(HEADS_PER_MIXER)]
    key_bias = [slopes[h] * lax.broadcasted_iota(jnp.int32, (tk, tq), 0).astype(F32)
                for h in range(HEADS_PER_MIXER)]

    def attn_body(j, carry, n_ties=None):
        ties_seen, stats = carry
        start = pl.multiple_of(j * tk, tk)
        sc = sc_ref[j]
        if n_ties is None:
            sel = sc >= tau
        else:
            tie = jnp.where(sc == tau, 1.0, 0.0)
            rank = jnp.dot(earlier, tie.astype(BF16), preferred_element_type=F32) + ties_seen
            sel = jnp.where(sc > tau, 1.0, jnp.where(rank < n_ties, tie, 0.0)) > 0.5
            ties_seen = ties_seen + jnp.sum(tie, axis=0, keepdims=True)
        block_pos = (j * tk - i * tq).astype(F32)
        for h in range(HEADS_PER_MIXER):
            k = k_ref[0, pl.ds(start, tk), _head_cols(h)]
            s = lax.dot_general(k, qs[h], _NT, preferred_element_type=F32) + key_bias[h]
            s_ref[h] = jnp.where(sel, s, MASKED)
        locals_ = []
        for h in range(HEADS_PER_MIXER):
            s = s_ref[h]
            m_loc = jnp.max(s, axis=0, keepdims=True)
            p = jnp.exp(s - m_loc)
            locals_.append((m_loc, jnp.sum(p, axis=0, keepdims=True)))
            p_ref[h] = p.astype(BF16)
        new_stats = []
        for h in range(HEADS_PER_MIXER):
            m, l, acc = stats[h]
            m_loc, l_blk = locals_[h]
            v = v_ref[0, pl.ds(start, tk), _head_cols(h)]
            pv_blk = lax.dot_general(v, p_ref[h], _TN, preferred_element_type=F32)
            m_blk = m_loc + slopes[h] * block_pos
            m_new = jnp.maximum(m, m_blk)
            w_old = jnp.exp(m - m_new)
            w_blk = jnp.where(m_loc > 0.5 * MASKED, jnp.exp(m_blk - m_new), 0.0)
            new_stats.append((m_new, w_old * l + w_blk * l_blk, w_old * acc + w_blk * pv_blk))
        return ties_seen, tuple(new_stats)

    init = tuple((jnp.full((1, tq), MASKED, F32), jnp.zeros((1, tq), F32), jnp.zeros((HEAD_DIM, tq), F32))
                 for _ in range(HEADS_PER_MIXER))
    start_carry = (jnp.zeros((1, tq), F32), init)

    def attend_without_ties():
        return lax.fori_loop(0, n_blocks, attn_body, start_carry)[1]

    def attend_with_ties():
        n_gt = count(lambda s: s > tau)
        n_ties = jnp.where(keep_all, 1e9, DSA_TOPK - n_gt)
        return lax.fori_loop(0, n_blocks, functools.partial(attn_body, n_ties=n_ties), start_carry)[1]

    stats = lax.cond(no_ties, attend_without_ties, attend_with_ties)
    outs = []
    for h in range(HEADS_PER_MIXER):
        _, l, acc = stats[h]
        o = acc / l
        outs.append(o * lax.rsqrt(jnp.mean(o * o, axis=0, keepdims=True) + NORM_EPS) * g_ref[:, h:h + 1])
    o_ref[0] = jnp.concatenate(outs, axis=0).T.astype(o_ref.dtype)


def _dsa_mixer(p, p_idx, p_w, g_heads):
    b, seq, _ = p.shape
    tq = min(2 * Q_TILE, seq)
    tk = tq
    qi_width = IDX_HEADS * IDX_DIM
    return pl.pallas_call(
        functools.partial(_dsa_kernel, slopes=_alibi_slopes(0)),
        grid=(b, seq // tq),
        in_specs=[pl.BlockSpec((1, tq, MIXER_WIDTH), lambda bi, i: (bi, i, 0)),
                  pl.BlockSpec((1, seq, MIXER_WIDTH), lambda bi, i: (bi, 0, 1)),
                  pl.BlockSpec((1, seq, MIXER_WIDTH), lambda bi, i: (bi, 0, 2)),
                  pl.BlockSpec((1, tq, qi_width), lambda bi, i: (bi, i, 0)),
                  pl.BlockSpec((1, seq, LANES), lambda bi, i: (bi, 0, qi_width // LANES)),
                  pl.BlockSpec((1, tq, LANES), lambda bi, i: (bi, i, 0)),
                  pl.BlockSpec((HEAD_DIM, HEADS_PER_MIXER), lambda bi, i: (0, 0))],
        out_specs=pl.BlockSpec((1, tq, MIXER_WIDTH), lambda bi, i: (bi, i, 0)),
        out_shape=jax.ShapeDtypeStruct((b, seq, MIXER_WIDTH), BF16),
        scratch_shapes=[pltpu.VMEM((seq // tk, tk, tq), F32),
                        pltpu.VMEM((HEADS_PER_MIXER, tk, tq), F32),
                        pltpu.VMEM((HEADS_PER_MIXER, tk, tq), BF16)],
        compiler_params=_params("arbitrary", "arbitrary"),
        name="dsa",
    )(p, p, p, p_idx, p_idx, p_w, g_heads.T)


def _band_kernel(q_ref, kp_ref, kc_ref, vp_ref, vc_ref, o_ref, lse_ref, *, dilation, slopes):
    tq = q_ref.shape[1]
    ui = pl.program_id(2)
    u_q = ui * tq + lax.broadcasted_iota(jnp.int32, (tq, 1), 0)
    u_k = (ui - 1) * tq + lax.broadcasted_iota(jnp.int32, (1, 2 * tq), 1)
    steps = u_q - u_k
    valid = jnp.logical_and(jnp.logical_and(steps >= 0, steps <= DILATED_STEPS), u_k >= 0)
    dist = (steps * dilation).astype(F32)
    lane = lax.broadcasted_iota(jnp.int32, (1, LANES), 1)
    lse_all = jnp.zeros((tq, LANES), F32)
    for h in range(HEADS_PER_MIXER):
        cols = _head_cols(h)
        k = jnp.concatenate([kp_ref[0, :, cols], kc_ref[0, :, cols]], axis=0)
        v = jnp.concatenate([vp_ref[0, :, cols], vc_ref[0, :, cols]], axis=0)
        s = lax.dot_general(q_ref[0, :, cols], k, _NT, preferred_element_type=F32) * ATTN_SCALE - slopes[h] * dist
        s = jnp.where(valid, s, MASKED)
        m = jnp.max(s, axis=1, keepdims=True)
        e = jnp.where(valid, jnp.exp(s - m), 0.0)
        den = jnp.sum(e, axis=1, keepdims=True)
        o_ref[0, :, cols] = jnp.dot(e.astype(BF16), v, preferred_element_type=F32) / den
        lse_all = lse_all + jnp.where(lane == h, m + jnp.log(den), 0.0)
    lse_ref[0] = lse_all


def _dilated_branch(p, dilation, slopes):
    b, seq, _ = p.shape
    classes = dilation
    length = seq // dilation
    tq = min(Q_TILE, length)
    view = p.reshape(b, length, classes * QKV_WIDTH)
    spec = lambda part, prev: pl.BlockSpec(
        (1, tq, MIXER_WIDTH),
        (lambda bi, c, ui: (bi, jnp.maximum(ui - 1, 0), c * 3 + part)) if prev
        else (lambda bi, c, ui: (bi, ui, c * 3 + part)))
    out, lse = pl.pallas_call(
        functools.partial(_band_kernel, dilation=dilation, slopes=slopes),
        grid=(b, classes, length // tq),
        in_specs=[spec(0, False), spec(1, True), spec(1, False), spec(2, True), spec(2, False)],
        out_specs=[pl.BlockSpec((1, tq, MIXER_WIDTH), lambda bi, c, ui: (bi, ui, c)),
                   pl.BlockSpec((1, tq, LANES), lambda bi, c, ui: (bi, ui, c))],
        out_shape=[jax.ShapeDtypeStruct((b, length, classes * MIXER_WIDTH), F32),
                   jax.ShapeDtypeStruct((b, length, classes * LANES), F32)],
        compiler_params=_params("arbitrary", "arbitrary", "arbitrary"),
        name=f"dilated_r{dilation}",
    )(view, view, view, view, view)
    return out.reshape(b, seq, MIXER_WIDTH), lse.reshape(b, seq, LANES)


def _dilated_merge_kernel(o1, o2, o3, l1, l2, l3, g_ref, o_ref):
    for h in range(HEADS_PER_MIXER):
        cols = _head_cols(h)
        lses = [l[0][:, h:h + 1] for l in (l1, l2, l3)]
        top = jnp.maximum(jnp.maximum(lses[0], lses[1]), lses[2])
        wts = [jnp.exp(l - top) for l in lses]
        mixed = (wts[0] * o1[0, :, cols] + wts[1] * o2[0, :, cols] + wts[2] * o3[0, :, cols]) / (wts[0] + wts[1] + wts[2])
        o_ref[0, :, cols] = _head_norm(mixed, g_ref[h:h + 1, :]).astype(o_ref.dtype)


def _dilated_mixer(p, g_heads):
    b, seq, _ = p.shape
    slopes = _alibi_slopes(1)
    branches = [_dilated_branch(p, r, slopes) for r in DILATIONS]
    tq = min(ROW_TILE, seq)
    o_spec = pl.BlockSpec((1, tq, MIXER_WIDTH), lambda bi, i: (bi, i, 0))
    l_spec = pl.BlockSpec((1, tq, LANES), lambda bi, i: (bi, i, 0))
    return pl.pallas_call(
        _dilated_merge_kernel,
        grid=(b, seq // tq),
        in_specs=[o_spec] * 3 + [l_spec] * 3 + [pl.BlockSpec((HEADS_PER_MIXER, HEAD_DIM), lambda bi, i: (0, 0))],
        out_specs=o_spec,
        out_shape=jax.ShapeDtypeStruct((b, seq, MIXER_WIDTH), BF16),
        compiler_params=_params("arbitrary", "arbitrary"),
        name="dilated_merge",
    )(*[o for o, _ in branches], *[l for _, l in branches], g_heads)


def _moba_kernel(q_ref, k_ref, v_ref, g_ref, o_ref, kmean_ref, *, slopes):
    tq = q_ref.shape[1]
    n_kv = k_ref.shape[1] // MOBA_BLOCK
    own = pl.program_id(1)

    @pl.when(own == 0)
    def _():
        kmean_ref[...] = jnp.zeros_like(kmean_ref)
        for n in range(n_kv):
            blk = k_ref[0, n * MOBA_BLOCK:(n + 1) * MOBA_BLOCK, :].astype(F32)
            kmean_ref[n:n + 1, :] = jnp.sum(blk, axis=0, keepdims=True) * (1.0 / MOBA_BLOCK)

    row = lax.broadcasted_iota(jnp.int32, (tq, 1), 0)
    lane = lax.broadcasted_iota(jnp.int32, (1, MOBA_BLOCK), 1)
    blk_id = lax.broadcasted_iota(jnp.int32, (1, LANES), 1)
    blk_f = blk_id.astype(F32)
    neg_inf = float("-inf")

    qs, chosen = [], []
    for h in range(HEADS_PER_MIXER):
        q = q_ref[0, :, _head_cols(h)]
        gate = lax.dot_general(q.astype(F32), kmean_ref[:, _head_cols(h)], _NT, preferred_element_type=F32,
                               precision=lax.Precision.HIGHEST)
        gate = jnp.where(blk_id < own, gate, neg_inf)
        picks = jnp.zeros((tq, LANES), F32)
        for _ in range(MOBA_TOPK):
            top = jnp.max(gate, axis=1, keepdims=True)
            is_top = jnp.logical_and(gate == top, top > neg_inf)
            first = jnp.min(jnp.where(is_top, blk_f, float(LANES)), axis=1, keepdims=True)
            pick = blk_f == first
            picks = jnp.where(pick, 1.0, picks)
            gate = jnp.where(pick, neg_inf, gate)
        chosen.append(picks)
        qs.append(q * ATTN_SCALE)

    def block_softmax(h, start, key_pos, keep=None):
        cols = _head_cols(h)
        k = k_ref[0, pl.ds(start, MOBA_BLOCK), cols]
        v = v_ref[0, pl.ds(start, MOBA_BLOCK), cols]
        s = lax.dot_general(qs[h], k, _NT, preferred_element_type=F32) + slopes[h] * key_pos
        if keep is not None:
            s = jnp.where(keep, s, MASKED)
        m = jnp.max(s, axis=1, keepdims=True)
        p = jnp.exp(s - m)
        return m, jnp.sum(p, axis=1, keepdims=True), jnp.dot(p.astype(BF16), v, preferred_element_type=F32)

    own_start = pl.multiple_of(own * MOBA_BLOCK, MOBA_BLOCK)
    stats = tuple(block_softmax(h, own_start, lane.astype(F32), keep=lane <= row) for h in range(HEADS_PER_MIXER))

    def body(j, stats):
        start = pl.multiple_of(j * MOBA_BLOCK, MOBA_BLOCK)
        key_pos = (lane + (j - own) * MOBA_BLOCK).astype(F32)
        merged = []
        for h in range(HEADS_PER_MIXER):
            m_blk, l_blk, pv_blk = block_softmax(h, start, key_pos)
            picked = jnp.max(jnp.where(blk_id == j, chosen[h], 0.0), axis=1, keepdims=True) > 0.5
            m_blk = jnp.where(picked, m_blk, MASKED)
            m, l, acc = stats[h]
            m_new = jnp.maximum(m, m_blk)
            w_old = jnp.exp(m - m_new)
            w_blk = jnp.exp(m_blk - m_new)
            merged.append((m_new, w_old * l + w_blk * l_blk, w_old * acc + w_blk * pv_blk))
        return tuple(merged)

    stats = lax.fori_loop(0, own, body, stats)
    for h in range(HEADS_PER_MIXER):
        _, l, acc = stats[h]
        o_ref[0, :, _head_cols(h)] = _head_norm(acc / l, g_ref[h:h + 1, :]).astype(o_ref.dtype)


def _moba_mixer(p, g_heads):
    b, seq, _ = p.shape
    tq = MOBA_BLOCK
    return pl.pallas_call(
        functools.partial(_moba_kernel, slopes=_alibi_slopes(2)),
        grid=(b, seq // tq),
        in_specs=[pl.BlockSpec((1, tq, MIXER_WIDTH), lambda bi, i: (bi, i, 0)),
                  pl.BlockSpec((1, seq, MIXER_WIDTH), lambda bi, i: (bi, 0, 1)),
                  pl.BlockSpec((1, seq, MIXER_WIDTH), lambda bi, i: (bi, 0, 2)),
                  pl.BlockSpec((HEADS_PER_MIXER, HEAD_DIM), lambda bi, i: (0, 0))],
        out_specs=pl.BlockSpec((1, tq, MIXER_WIDTH), lambda bi, i: (bi, i, 0)),
        out_shape=jax.ShapeDtypeStruct((b, seq, MIXER_WIDTH), BF16),
        scratch_shapes=[pltpu.VMEM((LANES, MIXER_WIDTH), F32)],
        compiler_params=_params("arbitrary", "arbitrary"),
        name="moba",
    )(p, p, p, g_heads)


def _out_proj_kernel(oa, ob, oc, od, w_ref, x_ref, mod_ref, o_ref):
    acc = jnp.zeros(x_ref.shape, F32)
    for m, o in enumerate((oa, ob, oc, od)):
        acc = acc + jnp.dot(o[...], w_ref[m * MIXER_WIDTH:(m + 1) * MIXER_WIDTH, :], preferred_element_type=F32)
    o_ref[...] = x_ref[...] + mod_ref[0, 2:3, :] * acc


def _out_projection(mixed, w_out, x2, mod, seq):
    t, d = x2.shape
    tm = min(ROW_TILE, seq)
    per_batch = seq // tm
    o_spec = pl.BlockSpec((tm, MIXER_WIDTH), lambda i: (i, 0))
    return pl.pallas_call(
        _out_proj_kernel,
        grid=(t // tm,),
        in_specs=[o_spec] * 4 + [pl.BlockSpec(w_out.shape, lambda i: (0, 0)),
                                 pl.BlockSpec((tm, d), lambda i: (i, 0)),
                                 pl.BlockSpec((1, 6, d), lambda i: (i // per_batch, 0, 0))],
        out_specs=pl.BlockSpec((tm, d), lambda i: (i, 0)),
        out_shape=jax.ShapeDtypeStruct((t, d), F32),
        compiler_params=_params("arbitrary"),
        name="out_projection",
    )(*[o.reshape(t, MIXER_WIDTH) for o in mixed], w_out, x2, mod)


def _ffn_kernel(x_ref, mod_ref, g_ref, wg_ref, wu_ref, wd_ref, o_ref, h_ref, acc_ref):
    f = pl.program_id(1)

    @pl.when(f == 0)
    def _():
        h_ref[...] = _modulated_norm(x_ref[...], g_ref[...], mod_ref[0, 3:4, :], mod_ref[0, 4:5, :]).astype(BF16)
        acc_ref[...] = jnp.zeros_like(acc_ref)

    h = h_ref[...]
    gate = jnp.dot(h, wg_ref[...], preferred_element_type=F32)
    up = jnp.dot(h, wu_ref[...], preferred_element_type=F32)
    act = (gate / (1.0 + jnp.exp(-gate)) * up).astype(BF16)
    acc_ref[...] += jnp.dot(act, wd_ref[...], preferred_element_type=F32)

    @pl.when(f == pl.num_programs(1) - 1)
    def _():
        o_ref[...] = x_ref[...] + mod_ref[0, 5:6, :] * acc_ref[...]


def _dense_ffn(x2, mod, g_ffn, wg, wu, wd, seq):
    t, d = x2.shape
    d_ff = wg.shape[1]
    tm = min(ROW_TILE, seq)
    tf = FFN_TILE if d_ff % FFN_TILE == 0 else d_ff
    per_batch = seq // tm
    return pl.pallas_call(
        _ffn_kernel,
        grid=(t // tm, d_ff // tf),
        in_specs=[pl.BlockSpec((tm, d), lambda i, f: (i, 0)),
                  pl.BlockSpec((1, 6, d), lambda i, f: (i // per_batch, 0, 0)),
                  pl.BlockSpec((1, d), lambda i, f: (0, 0)),
                  pl.BlockSpec((d, tf), lambda i, f: (0, f)),
                  pl.BlockSpec((d, tf), lambda i, f: (0, f)),
                  pl.BlockSpec((tf, d), lambda i, f: (f, 0))],
        out_specs=pl.BlockSpec((tm, d), lambda i, f: (i, 0)),
        out_shape=jax.ShapeDtypeStruct((t, d), F32),
        scratch_shapes=[pltpu.VMEM((tm, d), BF16), pltpu.VMEM((tm, d), F32)],
        compiler_params=_params("arbitrary", "arbitrary"),
        name="dense_ffn",
    )(x2, mod, g_ffn.reshape(1, d), wg.astype(BF16), wu.astype(BF16), wd.astype(BF16))


def _store_row_tiles(dst_ref, value):
    rows, d = value.shape
    chunks = d // LANES
    for s in range(chunks):
        dst_ref[pl.ds(s, rows, stride=chunks), :] = value[:, s * LANES:(s + 1) * LANES]


def _load_row_tile_chunk(src_ref, s, rows, chunks):
    return src_ref[pl.ds(s, rows, stride=chunks), :]


def _router_kernel(x_ref, mod_ref, g_ref, wr_ref, h_ref, logit_ref):
    h = _modulated_norm(x_ref[...], g_ref[...], mod_ref[0, 3:4, :], mod_ref[0, 4:5, :])
    _store_row_tiles(h_ref, h)
    logit_ref[...] = jnp.dot(h, wr_ref[...], preferred_element_type=F32, precision=lax.Precision.HIGHEST)


def _router(x2, mod, g_ffn, w_router, seq):
    t, d = x2.shape
    tm = min(ROW_TILE, seq)
    per_batch = seq // tm
    wr = jnp.zeros((d, LANES), F32).at[:, :N_EXPERTS].set(w_router.astype(F32))
    return pl.pallas_call(
        _router_kernel,
        grid=(t // tm,),
        in_specs=[pl.BlockSpec((tm, d), lambda i: (i, 0)),
                  pl.BlockSpec((1, 6, d), lambda i: (i // per_batch, 0, 0)),
                  pl.BlockSpec((1, d), lambda i: (0, 0)),
                  pl.BlockSpec((d, LANES), lambda i: (0, 0))],
        out_specs=[pl.BlockSpec((tm * (d // LANES), LANES), lambda i: (i, 0)),
                   pl.BlockSpec((tm, LANES), lambda i: (i, 0))],
        out_shape=[jax.ShapeDtypeStruct((t * (d // LANES), LANES), F32), jax.ShapeDtypeStruct((t, LANES), F32)],
        compiler_params=_params("arbitrary"),
        name="moe_router",
    )(x2, mod, g_ffn.reshape(1, d), wr)


def _row_copy(src_hbm, row, dst_ref, r, sem, chunks):
    src = src_hbm.at[pl.ds(pl.multiple_of(row * chunks, chunks), chunks), :]
    return pltpu.make_async_copy(src, dst_ref.at[pl.ds(pl.multiple_of(r * chunks, chunks), chunks), :], sem)


def _gather_kernel(idx_ref, src_hbm, o_ref, sem, *, chunks):
    rows = o_ref.shape[0] // chunks

    def start(r, c):
        _row_copy(src_hbm, idx_ref[0, 0, r], o_ref, r, sem, chunks).start()
        return c

    def wait(r, c):
        _row_copy(src_hbm, idx_ref[0, 0, r], o_ref, r, sem, chunks).wait()
        return c

    lax.fori_loop(0, rows, start, 0)
    lax.fori_loop(0, rows, wait, 0)


def _gather_rows(src, idx, rows_per_step, chunks):
    n = idx.shape[0]
    steps = n // rows_per_step
    return pl.pallas_call(
        functools.partial(_gather_kernel, chunks=chunks),
        grid=(steps,),
        in_specs=[pl.BlockSpec((1, 1, rows_per_step), lambda i: (i, 0, 0), memory_space=pltpu.SMEM),
                  pl.BlockSpec(memory_space=pl.ANY)],
        out_specs=pl.BlockSpec((rows_per_step * chunks, LANES), lambda i: (i, 0)),
        out_shape=jax.ShapeDtypeStruct((n * chunks, LANES), src.dtype),
        scratch_shapes=[pltpu.SemaphoreType.DMA(())],
        compiler_params=_params("arbitrary"),
        name="gather_rows",
    )(idx.reshape(steps, 1, rows_per_step), src)


def _expert_kernel(be_ref, used_ref, xs_ref, wg_ref, wu_ref, wd_ref, o_ref, xb_ref, acc_ref):
    m = pl.program_id(0)
    f = pl.program_id(1)
    live = m < used_ref[0]
    rows, d = xb_ref.shape
    chunks = d // LANES

    @pl.when(jnp.logical_and(live, f == 0))
    def _():
        for s in range(chunks):
            xb_ref[:, s * LANES:(s + 1) * LANES] = _load_row_tile_chunk(xs_ref, s, rows, chunks).astype(BF16)
        acc_ref[...] = jnp.zeros_like(acc_ref)

    @pl.when(live)
    def _():
        h = xb_ref[...]
        gate = jnp.dot(h, wg_ref[0], preferred_element_type=F32)
        up = jnp.dot(h, wu_ref[0], preferred_element_type=F32)
        act = (gate / (1.0 + jnp.exp(-gate)) * up).astype(BF16)
        acc_ref[...] += jnp.dot(act, wd_ref[0], preferred_element_type=F32)

    @pl.when(f == pl.num_programs(1) - 1)
    def _():
        _store_row_tiles(o_ref, jnp.where(live, acc_ref[...], 0.0))


def _expert_ffn(xs, block_expert, n_used, wg, wu, wd):
    d = wg.shape[1]
    chunks = d // LANES
    n_slots = xs.shape[0] // chunks
    d_ff = wg.shape[2]
    tm = EXPERT_ROWS
    tf = EXPERT_FF_TILE if d_ff % EXPERT_FF_TILE == 0 else d_ff
    nf = d_ff // tf

    def fcol(m, f, used):
        return jnp.where(m < used[0], f, nf - 1)

    grid_spec = pltpu.PrefetchScalarGridSpec(
        num_scalar_prefetch=2,
        grid=(n_slots // tm, nf),
        in_specs=[pl.BlockSpec((tm * chunks, LANES), lambda m, f, be, used: (m, 0)),
                  pl.BlockSpec((1, d, tf), lambda m, f, be, used: (be[m], 0, fcol(m, f, used))),
                  pl.BlockSpec((1, d, tf), lambda m, f, be, used: (be[m], 0, fcol(m, f, used))),
                  pl.BlockSpec((1, tf, d), lambda m, f, be, used: (be[m], fcol(m, f, used), 0))],
        out_specs=pl.BlockSpec((tm * chunks, LANES), lambda m, f, be, used: (m, 0)),
        scratch_shapes=[pltpu.VMEM((tm, d), BF16), pltpu.VMEM((tm, d), F32)])
    return pl.pallas_call(
        _expert_kernel,
        grid_spec=grid_spec,
        out_shape=jax.ShapeDtypeStruct((n_slots * chunks, LANES), F32),
        compiler_params=_params("arbitrary", "arbitrary"),
        name="expert_ffn",
    )(block_expert, n_used, xs, wg.astype(BF16), wu.astype(BF16), wd.astype(BF16))


def _combine_kernel(d0_ref, d1_ref, ys_hbm, x_ref, gates_ref, mod_ref, o_ref, y0_ref, y1_ref, sem):
    rows, d = o_ref.shape
    chunks = d // LANES

    def start(r, c):
        _row_copy(ys_hbm, d0_ref[0, 0, r], y0_ref, r, sem.at[0], chunks).start()
        _row_copy(ys_hbm, d1_ref[0, 0, r], y1_ref, r, sem.at[1], chunks).start()
        return c

    def wait(r, c):
        _row_copy(ys_hbm, d0_ref[0, 0, r], y0_ref, r, sem.at[0], chunks).wait()
        _row_copy(ys_hbm, d1_ref[0, 0, r], y1_ref, r, sem.at[1], chunks).wait()
        return c

    lax.fori_loop(0, rows, start, 0)
    lax.fori_loop(0, rows, wait, 0)
    gates = gates_ref[...]
    for s in range(chunks):
        cols = slice(s * LANES, (s + 1) * LANES)
        y = (_load_row_tile_chunk(y0_ref, s, rows, chunks) * gates[:, 0:1]
             + _load_row_tile_chunk(y1_ref, s, rows, chunks) * gates[:, 1:2])
        o_ref[:, cols] = x_ref[:, cols] + mod_ref[0, 5:6, cols] * y


def _moe_combine(ys, dest0, dest1, gates, x2, mod, seq):
    t, d = x2.shape
    tm = min(GATHER_ROWS, seq)
    steps = t // tm
    per_batch = seq // tm
    idx_spec = pl.BlockSpec((1, 1, tm), lambda i: (i, 0, 0), memory_space=pltpu.SMEM)
    return pl.pallas_call(
        _combine_kernel,
        grid=(steps,),
        in_specs=[idx_spec, idx_spec, pl.BlockSpec(memory_space=pl.ANY),
                  pl.BlockSpec((tm, d), lambda i: (i, 0)),
                  pl.BlockSpec((tm, TOP_K_EXPERTS), lambda i: (i, 0)),
                  pl.BlockSpec((1, 6, d), lambda i: (i // per_batch, 0, 0))],
        out_specs=pl.BlockSpec((tm, d), lambda i: (i, 0)),
        out_shape=jax.ShapeDtypeStruct((t, d), F32),
        scratch_shapes=[pltpu.VMEM((tm * (d // LANES), LANES), F32), pltpu.VMEM((tm * (d // LANES), LANES), F32),
                        pltpu.SemaphoreType.DMA((2,))],
        compiler_params=_params("arbitrary"),
        name="moe_combine",
    )(dest0.reshape(steps, 1, tm), dest1.reshape(steps, 1, tm), ys, x2, gates, mod)


def _moe_ffn(x2, mod, g_ffn, w_router, wg, wu, wd, seq):
    t, d = x2.shape
    h, logits = _router(x2, mod, g_ffn, w_router, seq)
    top_val, top_idx = lax.top_k(logits[:, :N_EXPERTS], TOP_K_EXPERTS)
    gates = jax.nn.softmax(top_val, axis=-1)

    n_assign = t * TOP_K_EXPERTS
    flat_e = top_idx.reshape(-1).astype(jnp.int32)
    onehot = (flat_e[:, None] == jnp.arange(N_EXPERTS, dtype=jnp.int32)[None, :]).astype(jnp.int32)
    rank = jnp.take_along_axis(jnp.cumsum(onehot, axis=0), flat_e[:, None], axis=1)[:, 0] - 1
    counts = jnp.sum(onehot, axis=0)
    padded = (counts + EXPERT_ROWS - 1) // EXPERT_ROWS * EXPERT_ROWS
    pad_end = jnp.cumsum(padded)
    dest = (pad_end - padded)[flat_e] + rank
    n_slots = (n_assign // EXPERT_ROWS + N_EXPERTS) * EXPERT_ROWS
    n_blocks = n_slots // EXPERT_ROWS
    slot_tok = jnp.zeros((n_slots,), jnp.int32).at[dest].set(jnp.arange(n_assign, dtype=jnp.int32) // TOP_K_EXPERTS)
    block_start = jnp.arange(n_blocks, dtype=jnp.int32) * EXPERT_ROWS
    block_expert = jnp.minimum(jnp.searchsorted(pad_end, block_start, side="right"), N_EXPERTS - 1).astype(jnp.int32)
    n_used = (pad_end[-1:] // EXPERT_ROWS).astype(jnp.int32)

    xs = _gather_rows(h, slot_tok, GATHER_ROWS, d // LANES)
    ys = _expert_ffn(xs, block_expert, n_used, wg, wu, wd)
    dest2 = dest.reshape(t, TOP_K_EXPERTS)
    return _moe_combine(ys, dest2[:, 0], dest2[:, 1], gates, x2, mod, seq)


def _final_norm_kernel(x_ref, g_ref, o_ref):
    x = x_ref[...]
    o_ref[...] = x * lax.rsqrt(jnp.mean(x * x, axis=-1, keepdims=True) + NORM_EPS) * g_ref[...]


def _final_norm(x2, g_final):
    t, d = x2.shape
    tm = min(ROW_TILE, t)
    return pl.pallas_call(
        _final_norm_kernel,
        grid=(t // tm,),
        in_specs=[pl.BlockSpec((tm, d), lambda i: (i, 0)), pl.BlockSpec((1, d), lambda i: (0, 0))],
        out_specs=pl.BlockSpec((tm, d), lambda i: (i, 0)),
        out_shape=jax.ShapeDtypeStruct((t, d), F32),
        compiler_params=_params("arbitrary"),
        name="final_norm",
    )(x2, g_final.reshape(1, d))


def _token_mixer(x2, mod, g_mix, w_in, g_heads, w_out, batch, seq):
    pa, pb, pc, pd, p_idx, p_w = _in_projection(x2, mod, g_mix, _pack_w_in(w_in), seq)
    shape3 = lambda a: a.reshape(batch, seq, a.shape[-1])
    gh = g_heads.reshape(4, HEADS_PER_MIXER, HEAD_DIM)
    mixed = (_stick_breaking_mixer(shape3(pa), gh[0]),
             _dsa_mixer(shape3(pb), shape3(p_idx), shape3(p_w), gh[1]),
             _dilated_mixer(shape3(pc), gh[2]),
             _moba_mixer(shape3(pd), gh[3]))
    return _out_projection(mixed, w_out.astype(BF16), x2, mod, seq)


def kernel(x, c, w_ada, b_ada, g_mix, w_in, g_heads, w_out, g_ffn, w_ff_gate, w_ff_up, w_ff_down, w_router, w_exp_gate, w_exp_up, w_exp_down, g_final):
    batch, seq, d = x.shape
    depth = w_ada.shape[0]
    mods = _ada_modulation(c, w_ada, b_ada)
    x2 = x.reshape(batch * seq, d)
    for layer in range(depth):
        mod = mods[layer]
        x2 = _token_mixer(x2, mod, g_mix[layer], w_in[layer], g_heads[layer], w_out[layer], batch, seq)
        i = layer // 2
        if layer % 2 == 0:
            x2 = _dense_ffn(x2, mod, g_ffn[layer], w_ff_gate[i], w_ff_up[i], w_ff_down[i], seq)
        else:
            x2 = _moe_ffn(x2, mod, g_ffn[layer], w_router[i], w_exp_gate[i], w_exp_up[i], w_exp_down[i], seq)
    return _final_norm(x2, g_final).reshape(batch, seq, d)
```

```python
import functools

import numpy as np
import jax
import jax.numpy as jnp
from jax import lax
from jax.experimental import pallas as pl
from jax.experimental.pallas import tpu as pltpu

F32 = jnp.float32
BF16 = jnp.bfloat16

HEAD_DIM = 64
HEADS_PER_MIXER = 4
MIXER_WIDTH = HEADS_PER_MIXER * HEAD_DIM
QKV_WIDTH = 3 * MIXER_WIDTH
IDX_HEADS = 8
IDX_DIM = 64
DSA_TOPK = 256
DILATIONS = (1, 4, 16)
DILATED_STEPS = 128
MOBA_BLOCK = 256
MOBA_TOPK = 3
N_EXPERTS = 8
TOP_K_EXPERTS = 2
NORM_EPS = 1e-6
ATTN_SCALE = HEAD_DIM ** -0.5

LANES = 128
Q_TILE = 128
ROW_TILE = 512
FFN_TILE = 1408
EXPERT_ROWS = 512
EXPERT_FF_TILE = 896
GATHER_ROWS = 512
VMEM_LIMIT = 56 * 1024 * 1024
MASKED = -1e30
SB_UNDERFLOW = 104.0

_NT = (((1,), (1,)), ((), ()))
_TN = (((0,), (0,)), ((), ()))


def _alibi_slopes(mixer_pos):
    idx = np.arange(HEADS_PER_MIXER, dtype=np.float32) * 3 + (mixer_pos + 1)
    return tuple(float(s) for s in np.exp2(-8.0 * idx / 12.0).astype(np.float32))


def _params(*semantics):
    return pltpu.CompilerParams(dimension_semantics=semantics, vmem_limit_bytes=VMEM_LIMIT)


def _modulated_norm(x, gain, shift, scale):
    y = x * lax.rsqrt(jnp.mean(x * x, axis=-1, keepdims=True) + NORM_EPS) * gain
    return y * (1.0 + scale) + shift


def _head_norm(acc, gain):
    return acc * lax.rsqrt(jnp.mean(acc * acc, axis=-1, keepdims=True) + NORM_EPS) * gain


def _head_cols(h):
    return slice(h * HEAD_DIM, (h + 1) * HEAD_DIM)


def _ada_kernel(c_ref, w_ref, b_ref, o_ref):
    c = c_ref[...]
    cond = c / (1.0 + jnp.exp(-c))
    o_ref[0, 0] = jnp.dot(cond, w_ref[0], preferred_element_type=F32,
                          precision=lax.Precision.HIGHEST) + b_ref[0, 0]


def _ada_modulation(c, w_ada, b_ada):
    depth, d, _ = w_ada.shape
    b = c.shape[0]
    out = pl.pallas_call(
        _ada_kernel,
        grid=(depth, 6),
        in_specs=[pl.BlockSpec((b, d), lambda l, k: (0, 0)),
                  pl.BlockSpec((1, d, d), lambda l, k: (l, 0, k)),
                  pl.BlockSpec((1, 1, 1, d), lambda l, k: (l, k, 0, 0))],
        out_specs=pl.BlockSpec((1, 1, b, d), lambda l, k: (l, k, 0, 0)),
        out_shape=jax.ShapeDtypeStruct((depth, 6, b, d), F32),
        compiler_params=_params("arbitrary", "arbitrary"),
        name="ada_modulation",
    )(c, w_ada, b_ada.reshape(depth, 6, 1, d))
    return out.transpose(0, 2, 1, 3)


IN_WIDTHS = (QKV_WIDTH, QKV_WIDTH, QKV_WIDTH, QKV_WIDTH, IDX_HEADS * IDX_DIM + LANES, LANES)


def _in_proj_kernel(x_ref, mod_ref, g_ref, w_ref, oa, ob, oc, od, oidx, ow):
    h = _modulated_norm(x_ref[...], g_ref[...], mod_ref[0, 0:1, :], mod_ref[0, 1:2, :]).astype(BF16)
    off = 0
    for o_ref, width in zip((oa, ob, oc, od, oidx, ow), IN_WIDTHS):
        o_ref[...] = jnp.dot(h, w_ref[:, off:off + width], preferred_element_type=F32).astype(o_ref.dtype)
        off += width


def _pack_w_in(w_in):
    d = w_in.shape[0]
    n_qkv = 4 * QKV_WIDTH
    n_qi = IDX_HEADS * IDX_DIM
    z = lambda n: jnp.zeros((d, n), w_in.dtype)
    return jnp.concatenate([w_in[:, :n_qkv + n_qi + IDX_DIM], z(LANES - IDX_DIM),
                            w_in[:, n_qkv + n_qi + IDX_DIM:], z(LANES - IDX_HEADS)], axis=1).astype(BF16)


def _in_projection(x2, mod, g_mix, w_packed, seq):
    t, d = x2.shape
    tm = min(ROW_TILE, seq)
    per_batch = seq // tm
    dtypes = (BF16, BF16, BF16, BF16, BF16, F32)
    return pl.pallas_call(
        _in_proj_kernel,
        grid=(t // tm,),
        in_specs=[pl.BlockSpec((tm, d), lambda i: (i, 0)),
                  pl.BlockSpec((1, 6, d), lambda i: (i // per_batch, 0, 0)),
                  pl.BlockSpec((1, d), lambda i: (0, 0)),
                  pl.BlockSpec(w_packed.shape, lambda i: (0, 0))],
        out_specs=[pl.BlockSpec((tm, w), lambda i: (i, 0)) for w in IN_WIDTHS],
        out_shape=[jax.ShapeDtypeStruct((t, w), dt) for w, dt in zip(IN_WIDTHS, dtypes)],
        compiler_params=_params("arbitrary"),
        name="in_projection",
    )(x2, mod, g_mix.reshape(1, d), w_packed)


def _sb_kernel(q_ref, k_ref, v_ref, g_ref, o_ref):
    tq = q_ref.shape[1]
    i = pl.program_id(1)
    row = lax.broadcasted_iota(jnp.int32, (tq, 1), 0)
    lane = lax.broadcasted_iota(jnp.int32, (1, tq), 1)
    later = (lax.broadcasted_iota(jnp.int32, (tq, tq), 0) > lax.broadcasted_iota(jnp.int32, (tq, tq), 1))
    later = jnp.where(later, 1.0, 0.0).astype(BF16)

    qs = [q_ref[0, :, _head_cols(h)] * ATTN_SCALE for h in range(HEADS_PER_MIXER)]

    def block(h, start, tail, past=None):
        cols = _head_cols(h)
        k = k_ref[0, pl.ds(start, tq), cols]
        v = v_ref[0, pl.ds(start, tq), cols]
        z = lax.dot_general(qs[h], k, _NT, preferred_element_type=F32)
        softplus = jnp.maximum(z, 0.0) + jnp.log(1.0 + jnp.exp(-jnp.abs(z)))
        log_1m = -softplus if past is None else jnp.where(past, -softplus, 0.0)
        hi = log_1m.astype(BF16)
        lo = (log_1m - hi.astype(F32)).astype(BF16)
        after = (jnp.dot(hi, later, preferred_element_type=F32)
                 + jnp.dot(lo, later, preferred_element_type=F32) + tail)
        a = jnp.exp(z - softplus + after)
        if past is not None:
            a = jnp.where(past, a, 0.0)
        return (tail + jnp.sum(log_1m, axis=1, keepdims=True),
                jnp.dot(a.astype(BF16), v, preferred_element_type=F32))

    diag = pl.multiple_of(i * tq, tq)
    state = tuple(block(h, diag, jnp.zeros((tq, 1), F32), past=lane < row) for h in range(HEADS_PER_MIXER))

    def body(carry):
        j, state = carry
        start = pl.multiple_of(j * tq, tq)
        new_state = []
        for h in range(HEADS_PER_MIXER):
            tail, acc = state[h]
            tail, av = block(h, start, tail)
            new_state.append((tail, acc + av))
        return j - 1, tuple(new_state)

    def cond(carry):
        j, state = carry
        worst = functools.reduce(jnp.maximum, [tail for tail, _ in state])
        return jnp.logical_and(j >= 0, jnp.max(worst) > -SB_UNDERFLOW)

    _, state = lax.while_loop(cond, body, (i - 1, state))
    for h in range(HEADS_PER_MIXER):
        o_ref[0, :, _head_cols(h)] = _head_norm(state[h][1], g_ref[h:h + 1, :]).astype(o_ref.dtype)


def _stick_breaking_mixer(p, g_heads):
    b, seq, _ = p.shape
    tq = min(Q_TILE, seq)
    return pl.pallas_call(
        _sb_kernel,
        grid=(b, seq // tq),
        in_specs=[pl.BlockSpec((1, tq, MIXER_WIDTH), lambda bi, i: (bi, i, 0)),
                  pl.BlockSpec((1, seq, MIXER_WIDTH), lambda bi, i: (bi, 0, 1)),
                  pl.BlockSpec((1, seq, MIXER_WIDTH), lambda bi, i: (bi, 0, 2)),
                  pl.BlockSpec((HEADS_PER_MIXER, HEAD_DIM), lambda bi, i: (0, 0))],
        out_specs=pl.BlockSpec((1, tq, MIXER_WIDTH), lambda bi, i: (bi, i, 0)),
        out_shape=jax.ShapeDtypeStruct((b, seq, MIXER_WIDTH), BF16),
        compiler_params=_params("arbitrary", "arbitrary"),
        name="stick_breaking",
    )(p, p, p, g_heads)


def _dsa_kernel(q_ref, k_ref, v_ref, qi_ref, ki_ref, w_ref, g_ref, o_ref, sc_ref, s_ref, p_ref, *, slopes):
    tq = q_ref.shape[1]
    tk = sc_ref.shape[1]
    i = pl.program_id(1)
    n_blocks = ((i + 1) * tq + tk - 1) // tk
    qpos = i * tq + lax.broadcasted_iota(jnp.int32, (1, tq), 1)
    key_row = lax.broadcasted_iota(jnp.int32, (tk, 1), 0)
    neg_inf = float("-inf")

    w_t = w_ref[0].T * (IDX_HEADS ** -0.5 * IDX_DIM ** -0.5)
    w_rows = [w_t[h:h + 1, :] for h in range(IDX_HEADS)]
    qi = qi_ref[0]

    def score_body(j, carry):
        start = pl.multiple_of(j * tk, tk)
        ki = ki_ref[0, pl.ds(start, tk), 0:IDX_DIM]
        sc = jnp.zeros((tk, tq), F32)
        for h in range(IDX_HEADS):
            x = lax.dot_general(ki, qi[:, h * IDX_DIM:(h + 1) * IDX_DIM], _NT, preferred_element_type=F32)
            sc = sc + w_rows[h] * jnp.maximum(x, 0.0)
        sc_ref[j] = jnp.where((start + key_row) <= qpos, sc + 0.0, neg_inf)
        return carry

    lax.fori_loop(0, n_blocks, score_body, 0)

    def count(pred):
        def body(j, cnt):
            c = jnp.where(pred(sc_ref[j]), 1.0, 0.0)
            return cnt + jnp.sum(c.reshape(tk // 8, 8, tq), axis=0)
        cnt = lax.fori_loop(0, n_blocks, body, jnp.zeros((8, tq), F32))
        return jnp.sum(cnt, axis=0, keepdims=True)

    int_min = jnp.int32(-2 ** 31)

    def ordered_to_float(u):
        key = u ^ int_min
        bits = jnp.where(key >= 0, key, key ^ jnp.int32(0x7FFFFFFF))
        return lax.bitcast_convert_type(bits, F32)

    def search_body(step, carry):
        prefix, n_at_prefix = carry
        cand = prefix | jnp.left_shift(jnp.int32(1), 31 - step)
        cand_f = ordered_to_float(cand)
        n_ge = count(lambda s: s >= cand_f)
        keep = n_ge >= DSA_TOPK
        return jnp.where(keep, cand, prefix), jnp.where(keep, n_ge, n_at_prefix)

    prefix, n_ge_tau = lax.fori_loop(0, 32, search_body,
                                     (jnp.zeros((1, tq), jnp.int32), jnp.zeros((1, tq), F32)))
    keep_all = (qpos + 1) <= DSA_TOPK
    tau = jnp.where(keep_all, float(np.finfo(np.float32).min), ordered_to_float(prefix))
    no_ties = jnp.min(jnp.where(jnp.logical_or(keep_all, n_ge_tau == DSA_TOPK), 1.0, 0.0)) > 0.5

    earlier = (lax.broadcasted_iota(jnp.int32, (tk, tk), 1) < lax.broadcasted_iota(jnp.int32, (tk, tk), 0))
    earlier = jnp.where(earlier, 1.0, 0.0).astype(BF16)
    qs = [q_ref[0, :, _head_cols(h)] * ATTN_SCALE for h in range(HEADS_PER_MIXER)]
    key_bias = [slopes[h] * lax.broadcasted_iota(jnp.int32, (tk, tq), 0).astype(F32)
                for h in range(HEADS_PER_MIXER)]

    def attn_body(j, carry, n_ties=None):
        ties_seen, stats = carry
        start = pl.multiple_of(j * tk, tk)
        sc = sc_ref[j]
        if n_ties is None:
            sel = sc >= tau
        else:
            tie = jnp.where(sc == tau, 1.0, 0.0)
            rank = jnp.dot(earlier, tie.astype(BF16), preferred_element_type=F32) + ties_seen
            sel = jnp.where(sc > tau, 1.0, jnp.where(rank < n_ties, tie, 0.0)) > 0.5
            ties_seen = ties_seen + jnp.sum(tie, axis=0, keepdims=True)
        block_pos = (j * tk - i * tq).astype(F32)
        for h in range(HEADS_PER_MIXER):
            k = k_ref[0, pl.ds(start, tk), _head_cols(h)]
            s = lax.dot_general(k, qs[h], _NT, preferred_element_type=F32) + key_bias[h]
            s_ref[h] = jnp.where(sel, s, MASKED)
        locals_ = []
        for h in range(HEADS_PER_MIXER):
            s = s_ref[h]
            m_loc = jnp.max(s, axis=0, keepdims=True)
            p = jnp.exp(s - m_loc)
            locals_.append((m_loc, jnp.sum(p, axis=0, keepdims=True)))
            p_ref[h] = p.astype(BF16)
        new_stats = []
        for h in range(HEADS_PER_MIXER):
            m, l, acc = stats[h]
            m_loc, l_blk = locals_[h]
            v = v_ref[0, pl.ds(start, tk), _head_cols(h)]
            pv_blk = lax.dot_general(v, p_ref[h], _TN, preferred_element_type=F32)
            m_blk = m_loc + slopes[h] * block_pos
            m_new = jnp.maximum(m, m_blk)
            w_old = jnp.exp(m - m_new)
            w_blk = jnp.where(m_loc > 0.5 * MASKED, jnp.exp(m_blk - m_new), 0.0)
            new_stats.append((m_new, w_old * l + w_blk * l_blk, w_old * acc + w_blk * pv_blk))
        return ties_seen, tuple(new_stats)

    init = tuple((jnp.full((1, tq), MASKED, F32), jnp.zeros((1, tq), F32), jnp.zeros((HEAD_DIM, tq), F32))
                 for _ in range(HEADS_PER_MIXER))
    start_carry = (jnp.zeros((1, tq), F32), init)

    def attend_without_ties():
        return lax.fori_loop(0, n_blocks, attn_body, start_carry)[1]

    def attend_with_ties():
        n_gt = count(lambda s: s > tau)
        n_ties = jnp.where(keep_all, 1e9, DSA_TOPK - n_gt)
        return lax.fori_loop(0, n_blocks, functools.partial(attn_body, n_ties=n_ties), start_carry)[1]

    stats = lax.cond(no_ties, attend_without_ties, attend_with_ties)
    outs = []
    for h in range(HEADS_PER_MIXER):
        _, l, acc = stats[h]
        o = acc / l
        outs.append(o * lax.rsqrt(jnp.mean(o * o, axis=0, keepdims=True) + NORM_EPS) * g_ref[:, h:h + 1])
    o_ref[0] = jnp.concatenate(outs, axis=0).T.astype(o_ref.dtype)


def _dsa_mixer(p, p_idx, p_w, g_heads):
    b, seq, _ = p.shape
    tq = min(2 * Q_TILE, seq)
    tk = tq
    qi_width = IDX_HEADS * IDX_DIM
    return pl.pallas_call(
        functools.partial(_dsa_kernel, slopes=_alibi_slopes(0)),
        grid=(b, seq // tq),
        in_specs=[pl.BlockSpec((1, tq, MIXER_WIDTH), lambda bi, i: (bi, i, 0)),
                  pl.BlockSpec((1, seq, MIXER_WIDTH), lambda bi, i: (bi, 0, 1)),
                  pl.BlockSpec((1, seq, MIXER_WIDTH), lambda bi, i: (bi, 0, 2)),
                  pl.BlockSpec((1, tq, qi_width), lambda bi, i: (bi, i, 0)),
                  pl.BlockSpec((1, seq, LANES), lambda bi, i: (bi, 0, qi_width // LANES)),
                  pl.BlockSpec((1, tq, LANES), lambda bi, i: (bi, i, 0)),
                  pl.BlockSpec((HEAD_DIM, HEADS_PER_MIXER), lambda bi, i: (0, 0))],
        out_specs=pl.BlockSpec((1, tq, MIXER_WIDTH), lambda bi, i: (bi, i, 0)),
        out_shape=jax.ShapeDtypeStruct((b, seq, MIXER_WIDTH), BF16),
        scratch_shapes=[pltpu.VMEM((seq // tk, tk, tq), F32),
                        pltpu.VMEM((HEADS_PER_MIXER, tk, tq), F32),
                        pltpu.VMEM((HEADS_PER_MIXER, tk, tq), BF16)],
        compiler_params=_params("arbitrary", "arbitrary"),
        name="dsa",
    )(p, p, p, p_idx, p_idx, p_w, g_heads.T)


def _band_kernel(q_ref, kp_ref, kc_ref, vp_ref, vc_ref, o_ref, lse_ref, *, dilation, slopes):
    tq = q_ref.shape[1]
    ui = pl.program_id(2)
    u_q = ui * tq + lax.broadcasted_iota(jnp.int32, (tq, 1), 0)
    u_k = (ui - 1) * tq + lax.broadcasted_iota(jnp.int32, (1, 2 * tq), 1)
    steps = u_q - u_k
    valid = jnp.logical_and(jnp.logical_and(steps >= 0, steps <= DILATED_STEPS), u_k >= 0)
    dist = (steps * dilation).astype(F32)
    lane = lax.broadcasted_iota(jnp.int32, (1, LANES), 1)
    lse_all = jnp.zeros((tq, LANES), F32)
    for h in range(HEADS_PER_MIXER):
        cols = _head_cols(h)
        k = jnp.concatenate([kp_ref[0, :, cols], kc_ref[0, :, cols]], axis=0)
        v = jnp.concatenate([vp_ref[0, :, cols], vc_ref[0, :, cols]], axis=0)
        s = lax.dot_general(q_ref[0, :, cols], k, _NT, preferred_element_type=F32) * ATTN_SCALE - slopes[h] * dist
        s = jnp.where(valid, s, MASKED)
        m = jnp.max(s, axis=1, keepdims=True)
        e = jnp.where(valid, jnp.exp(s - m), 0.0)
        den = jnp.sum(e, axis=1, keepdims=True)
        o_ref[0, :, cols] = jnp.dot(e.astype(BF16), v, preferred_element_type=F32) / den
        lse_all = lse_all + jnp.where(lane == h, m + jnp.log(den), 0.0)
    lse_ref[0] = lse_all


def _dilated_branch(p, dilation, slopes):
    b, seq, _ = p.shape
    classes = dilation
    length = seq // dilation
    tq = min(Q_TILE, length)
    view = p.reshape(b, length, classes * QKV_WIDTH)
    spec = lambda part, prev: pl.BlockSpec(
        (1, tq, MIXER_WIDTH),
        (lambda bi, c, ui: (bi, jnp.maximum(ui - 1, 0), c * 3 + part)) if prev
        else (lambda bi, c, ui: (bi, ui, c * 3 + part)))
    out, lse = pl.pallas_call(
        functools.partial(_band_kernel, dilation=dilation, slopes=slopes),
        grid=(b, classes, length // tq),
        in_specs=[spec(0, False), spec(1, True), spec(1, False), spec(2, True), spec(2, False)],
        out_specs=[pl.BlockSpec((1, tq, MIXER_WIDTH), lambda bi, c, ui: (bi, ui, c)),
                   pl.BlockSpec((1, tq, LANES), lambda bi, c, ui: (bi, ui, c))],
        out_shape=[jax.ShapeDtypeStruct((b, length, classes * MIXER_WIDTH), F32),
                   jax.ShapeDtypeStruct((b, length, classes * LANES), F32)],
        compiler_params=_params("arbitrary", "arbitrary", "arbitrary"),
        name=f"dilated_r{dilation}",
    )(view, view, view, view, view)
    return out.reshape(b, seq, MIXER_WIDTH), lse.reshape(b, seq, LANES)


def _dilated_merge_kernel(o1, o2, o3, l1, l2, l3, g_ref, o_ref):
    for h in range(HEADS_PER_MIXER):
        cols = _head_cols(h)
        lses = [l[0][:, h:h + 1] for l in (l1, l2, l3)]
        top = jnp.maximum(jnp.maximum(lses[0], lses[1]), lses[2])
        wts = [jnp.exp(l - top) for l in lses]
        mixed = (wts[0] * o1[0, :, cols] + wts[1] * o2[0, :, cols] + wts[2] * o3[0, :, cols]) / (wts[0] + wts[1] + wts[2])
        o_ref[0, :, cols] = _head_norm(mixed, g_ref[h:h + 1, :]).astype(o_ref.dtype)


def _dilated_mixer(p, g_heads):
    b, seq, _ = p.shape
    slopes = _alibi_slopes(1)
    branches = [_dilated_branch(p, r, slopes) for r in DILATIONS]
    tq = min(ROW_TILE, seq)
    o_spec = pl.BlockSpec((1, tq, MIXER_WIDTH), lambda bi, i: (bi, i, 0))
    l_spec = pl.BlockSpec((1, tq, LANES), lambda bi, i: (bi, i, 0))
    return pl.pallas_call(
        _dilated_merge_kernel,
        grid=(b, seq // tq),
        in_specs=[o_spec] * 3 + [l_spec] * 3 + [pl.BlockSpec((HEADS_PER_MIXER, HEAD_DIM), lambda bi, i: (0, 0))],
        out_specs=o_spec,
        out_shape=jax.ShapeDtypeStruct((b, seq, MIXER_WIDTH), BF16),
        compiler_params=_params("arbitrary", "arbitrary"),
        name="dilated_merge",
    )(*[o for o, _ in branches], *[l for _, l in branches], g_heads)


def _moba_kernel(q_ref, k_ref, v_ref, g_ref, o_ref, kmean_ref, chosen_ref, s_ref, p_ref, *, slopes):
    tq = q_ref.shape[1]
    n_kv = k_ref.shape[1] // MOBA_BLOCK
    own = pl.program_id(1)

    @pl.when(own == 0)
    def _():
        kmean_ref[...] = jnp.zeros_like(kmean_ref)
        for n in range(n_kv):
            blk = k_ref[0, n * MOBA_BLOCK:(n + 1) * MOBA_BLOCK, :].astype(F32)
            kmean_ref[n:n + 1, :] = jnp.sum(blk, axis=0, keepdims=True) * (1.0 / MOBA_BLOCK)

    key_row = lax.broadcasted_iota(jnp.int32, (MOBA_BLOCK, 1), 0)
    q_lane = lax.broadcasted_iota(jnp.int32, (1, tq), 1)
    blk_row = lax.broadcasted_iota(jnp.int32, (LANES, 1), 0)
    blk_f = blk_row.astype(F32)
    neg_inf = float("-inf")

    qs = []
    for h in range(HEADS_PER_MIXER):
        q = q_ref[0, :, _head_cols(h)]
        gate = lax.dot_general(kmean_ref[:, _head_cols(h)], q.astype(F32), _NT, preferred_element_type=F32,
                               precision=lax.Precision.HIGHEST)
        gate = jnp.where(blk_row < own, gate, neg_inf)
        picks = jnp.zeros((LANES, tq), F32)
        for _ in range(MOBA_TOPK):
            top = jnp.max(gate, axis=0, keepdims=True)
            is_top = jnp.logical_and(gate == top, top > neg_inf)
            first = jnp.min(jnp.where(is_top, blk_f, float(LANES)), axis=0, keepdims=True)
            pick = blk_f == first
            picks = jnp.where(pick, 1.0, picks)
            gate = jnp.where(pick, neg_inf, gate)
        chosen_ref[h] = picks
        qs.append(q * ATTN_SCALE)

    key_bias = [slopes[h] * lax.broadcasted_iota(jnp.int32, (MOBA_BLOCK, tq), 0).astype(F32)
                for h in range(HEADS_PER_MIXER)]

    def block_softmax(start, keep=None):
        for h in range(HEADS_PER_MIXER):
            k = k_ref[0, pl.ds(start, MOBA_BLOCK), _head_cols(h)]
            s = lax.dot_general(k, qs[h], _NT, preferred_element_type=F32) + key_bias[h]
            s_ref[h] = s if keep is None else jnp.where(keep, s, MASKED)
        pieces = []
        for h in range(HEADS_PER_MIXER):
            s = s_ref[h]
            m_loc = jnp.max(s, axis=0, keepdims=True)
            p = jnp.exp(s - m_loc)
            pieces.append((m_loc, jnp.sum(p, axis=0, keepdims=True)))
            p_ref[h] = p.astype(BF16)
        out = []
        for h in range(HEADS_PER_MIXER):
            v = v_ref[0, pl.ds(start, MOBA_BLOCK), _head_cols(h)]
            out.append(pieces[h] + (lax.dot_general(v, p_ref[h], _TN, preferred_element_type=F32),))
        return out

    own_start = pl.multiple_of(own * MOBA_BLOCK, MOBA_BLOCK)
    stats = tuple(block_softmax(own_start, keep=key_row <= q_lane))

    def body(j, stats):
        start = pl.multiple_of(j * MOBA_BLOCK, MOBA_BLOCK)
        block_pos = ((j - own) * MOBA_BLOCK).astype(F32)
        merged = []
        for h, (m_loc, l_blk, pv_blk) in enumerate(block_softmax(start)):
            picked = chosen_ref[h, pl.ds(j, 1), :] > 0.5
            m_blk = jnp.where(picked, m_loc + slopes[h] * block_pos, MASKED)
            m, l, acc = stats[h]
            m_new = jnp.maximum(m, m_blk)
            w_old = jnp.exp(m - m_new)
            w_blk = jnp.exp(m_blk - m_new)
            merged.append((m_new, w_old * l + w_blk * l_blk, w_old * acc + w_blk * pv_blk))
        return tuple(merged)

    stats = lax.fori_loop(0, own, body, stats)
    outs = []
    for h in range(HEADS_PER_MIXER):
        _, l, acc = stats[h]
        o = acc / l
        outs.append(o * lax.rsqrt(jnp.mean(o * o, axis=0, keepdims=True) + NORM_EPS) * g_ref[:, h:h + 1])
    o_ref[0] = jnp.concatenate(outs, axis=0).T.astype(o_ref.dtype)


def _moba_mixer(p, g_heads):
    b, seq, _ = p.shape
    tq = MOBA_BLOCK
    return pl.pallas_call(
        functools.partial(_moba_kernel, slopes=_alibi_slopes(2)),
        grid=(b, seq // tq),
        in_specs=[pl.BlockSpec((1, tq, MIXER_WIDTH), lambda bi, i: (bi, i, 0)),
                  pl.BlockSpec((1, seq, MIXER_WIDTH), lambda bi, i: (bi, 0, 1)),
                  pl.BlockSpec((1, seq, MIXER_WIDTH), lambda bi, i: (bi, 0, 2)),
                  pl.BlockSpec((HEAD_DIM, HEADS_PER_MIXER), lambda bi, i: (0, 0))],
        out_specs=pl.BlockSpec((1, tq, MIXER_WIDTH), lambda bi, i: (bi, i, 0)),
        out_shape=jax.ShapeDtypeStruct((b, seq, MIXER_WIDTH), BF16),
        scratch_shapes=[pltpu.VMEM((LANES, MIXER_WIDTH), F32),
                        pltpu.VMEM((HEADS_PER_MIXER, LANES, tq), F32),
                        pltpu.VMEM((HEADS_PER_MIXER, MOBA_BLOCK, tq), F32),
                        pltpu.VMEM((HEADS_PER_MIXER, MOBA_BLOCK, tq), BF16)],
        compiler_params=_params("arbitrary", "arbitrary"),
        name="moba",
    )(p, p, p, g_heads.T)


def _out_proj_kernel(oa, ob, oc, od, w_ref, x_ref, mod_ref, o_ref):
    acc = jnp.zeros(x_ref.shape, F32)
    for m, o in enumerate((oa, ob, oc, od)):
        acc = acc + jnp.dot(o[...], w_ref[m * MIXER_WIDTH:(m + 1) * MIXER_WIDTH, :], preferred_element_type=F32)
    o_ref[...] = x_ref[...] + mod_ref[0, 2:3, :] * acc


def _out_projection(mixed, w_out, x2, mod, seq):
    t, d = x2.shape
    tm = min(ROW_TILE, seq)
    per_batch = seq // tm
    o_spec = pl.BlockSpec((tm, MIXER_WIDTH), lambda i: (i, 0))
    return pl.pallas_call(
        _out_proj_kernel,
        grid=(t // tm,),
        in_specs=[o_spec] * 4 + [pl.BlockSpec(w_out.shape, lambda i: (0, 0)),
                                 pl.BlockSpec((tm, d), lambda i: (i, 0)),
                                 pl.BlockSpec((1, 6, d), lambda i: (i // per_batch, 0, 0))],
        out_specs=pl.BlockSpec((tm, d), lambda i: (i, 0)),
        out_shape=jax.ShapeDtypeStruct((t, d), F32),
        compiler_params=_params("arbitrary"),
        name="out_projection",
    )(*[o.reshape(t, MIXER_WIDTH) for o in mixed], w_out, x2, mod)


def _ffn_kernel(x_ref, mod_ref, g_ref, wg_ref, wu_ref, wd_ref, o_ref, h_ref, acc_ref):
    f = pl.program_id(1)

    @pl.when(f == 0)
    def _():
        h_ref[...] = _modulated_norm(x_ref[...], g_ref[...], mod_ref[0, 3:4, :], mod_ref[0, 4:5, :]).astype(BF16)
        acc_ref[...] = jnp.zeros_like(acc_ref)

    h = h_ref[...]
    gate = jnp.dot(h, wg_ref[...], preferred_element_type=F32)
    up = jnp.dot(h, wu_ref[...], preferred_element_type=F32)
    act = (gate / (1.0 + jnp.exp(-gate)) * up).astype(BF16)
    acc_ref[...] += jnp.dot(act, wd_ref[...], preferred_element_type=F32)

    @pl.when(f == pl.num_programs(1) - 1)
    def _():
        o_ref[...] = x_ref[...] + mod_ref[0, 5:6, :] * acc_ref[...]


def _dense_ffn(x2, mod, g_ffn, wg, wu, wd, seq):
    t, d = x2.shape
    d_ff = wg.shape[1]
    tm = min(ROW_TILE, seq)
    tf = FFN_TILE if d_ff % FFN_TILE == 0 else d_ff
    per_batch = seq // tm
    return pl.pallas_call(
        _ffn_kernel,
        grid=(t // tm, d_ff // tf),
        in_specs=[pl.BlockSpec((tm, d), lambda i, f: (i, 0)),
                  pl.BlockSpec((1, 6, d), lambda i, f: (i // per_batch, 0, 0)),
                  pl.BlockSpec((1, d), lambda i, f: (0, 0)),
                  pl.BlockSpec((d, tf), lambda i, f: (0, f)),
                  pl.BlockSpec((d, tf), lambda i, f: (0, f)),
                  pl.BlockSpec((tf, d), lambda i, f: (f, 0))],
        out_specs=pl.BlockSpec((tm, d), lambda i, f: (i, 0)),
        out_shape=jax.ShapeDtypeStruct((t, d), F32),
        scratch_shapes=[pltpu.VMEM((tm, d), BF16), pltpu.VMEM((tm, d), F32)],
        compiler_params=_params("arbitrary", "arbitrary"),
        name="dense_ffn",
    )(x2, mod, g_ffn.reshape(1, d), wg.astype(BF16), wu.astype(BF16), wd.astype(BF16))


def _store_row_tiles(dst_ref, value):
    rows, d = value.shape
    chunks = d // LANES
    for s in range(chunks):
        dst_ref[pl.ds(s, rows, stride=chunks), :] = value[:, s * LANES:(s + 1) * LANES]


def _load_row_tile_chunk(src_ref, s, rows, chunks):
    return src_ref[pl.ds(s, rows, stride=chunks), :]


def _router_kernel(x_ref, mod_ref, g_ref, wr_ref, h_ref, logit_ref):
    h = _modulated_norm(x_ref[...], g_ref[...], mod_ref[0, 3:4, :], mod_ref[0, 4:5, :])
    _store_row_tiles(h_ref, h)
    logit_ref[...] = jnp.dot(h, wr_ref[...], preferred_element_type=F32, precision=lax.Precision.HIGHEST)


def _router(x2, mod, g_ffn, w_router, seq):
    t, d = x2.shape
    tm = min(ROW_TILE, seq)
    per_batch = seq // tm
    wr = jnp.zeros((d, LANES), F32).at[:, :N_EXPERTS].set(w_router.astype(F32))
    return pl.pallas_call(
        _router_kernel,
        grid=(t // tm,),
        in_specs=[pl.BlockSpec((tm, d), lambda i: (i, 0)),
                  pl.BlockSpec((1, 6, d), lambda i: (i // per_batch, 0, 0)),
                  pl.BlockSpec((1, d), lambda i: (0, 0)),
                  pl.BlockSpec((d, LANES), lambda i: (0, 0))],
        out_specs=[pl.BlockSpec((tm * (d // LANES), LANES), lambda i: (i, 0)),
                   pl.BlockSpec((tm, LANES), lambda i: (i, 0))],
        out_shape=[jax.ShapeDtypeStruct((t * (d // LANES), LANES), F32), jax.ShapeDtypeStruct((t, LANES), F32)],
        compiler_params=_params("arbitrary"),
        name="moe_router",
    )(x2, mod, g_ffn.reshape(1, d), wr)


def _row_copy(src_hbm, row, dst_ref, r, sem, chunks):
    src = src_hbm.at[pl.ds(pl.multiple_of(row * chunks, chunks), chunks), :]
    return pltpu.make_async_copy(src, dst_ref.at[pl.ds(pl.multiple_of(r * chunks, chunks), chunks), :], sem)


def _gather_kernel(idx_ref, src_hbm, o_ref, sem, *, chunks):
    rows = o_ref.shape[0] // chunks

    def start(pair, c):
        for lane in range(2):
            r = 2 * pair + lane
            _row_copy(src_hbm, idx_ref[0, 0, r], o_ref, r, sem.at[lane], chunks).start(priority=lane)
        return c

    def wait(pair, c):
        for lane in range(2):
            r = 2 * pair + lane
            _row_copy(src_hbm, idx_ref[0, 0, r], o_ref, r, sem.at[lane], chunks).wait()
        return c

    lax.fori_loop(0, rows // 2, start, 0)
    lax.fori_loop(0, rows // 2, wait, 0)


def _gather_rows(src, idx, rows_per_step, chunks):
    n = idx.shape[0]
    steps = n // rows_per_step
    return pl.pallas_call(
        functools.partial(_gather_kernel, chunks=chunks),
        grid=(steps,),
        in_specs=[pl.BlockSpec((1, 1, rows_per_step), lambda i: (i, 0, 0), memory_space=pltpu.SMEM),
                  pl.BlockSpec(memory_space=pl.ANY)],
        out_specs=pl.BlockSpec((rows_per_step * chunks, LANES), lambda i: (i, 0)),
        out_shape=jax.ShapeDtypeStruct((n * chunks, LANES), src.dtype),
        scratch_shapes=[pltpu.SemaphoreType.DMA((2,))],
        compiler_params=_params("arbitrary"),
        name="gather_rows",
    )(idx.reshape(steps, 1, rows_per_step), src)


def _expert_kernel(be_ref, used_ref, xs_ref, wg_ref, wu_ref, wd_ref, o_ref, xb_ref, acc_ref):
    m = pl.program_id(0)
    f = pl.program_id(1)
    live = m < used_ref[0]
    rows, d = xb_ref.shape
    chunks = d // LANES

    @pl.when(jnp.logical_and(live, f == 0))
    def _():
        for s in range(chunks):
            xb_ref[:, s * LANES:(s + 1) * LANES] = _load_row_tile_chunk(xs_ref, s, rows, chunks).astype(BF16)
        acc_ref[...] = jnp.zeros_like(acc_ref)

    @pl.when(live)
    def _():
        h = xb_ref[...]
        gate = jnp.dot(h, wg_ref[0], preferred_element_type=F32)
        up = jnp.dot(h, wu_ref[0], preferred_element_type=F32)
        act = (gate / (1.0 + jnp.exp(-gate)) * up).astype(BF16)
        acc_ref[...] += jnp.dot(act, wd_ref[0], preferred_element_type=F32)

    @pl.when(f == pl.num_programs(1) - 1)
    def _():
        _store_row_tiles(o_ref, jnp.where(live, acc_ref[...], 0.0))


def _expert_ffn(xs, block_expert, n_used, wg, wu, wd):
    d = wg.shape[1]
    chunks = d // LANES
    n_slots = xs.shape[0] // chunks
    d_ff = wg.shape[2]
    tm = EXPERT_ROWS
    tf = EXPERT_FF_TILE if d_ff % EXPERT_FF_TILE == 0 else d_ff
    nf = d_ff // tf

    def fcol(m, f, used):
        return jnp.where(m < used[0], f, nf - 1)

    grid_spec = pltpu.PrefetchScalarGridSpec(
        num_scalar_prefetch=2,
        grid=(n_slots // tm, nf),
        in_specs=[pl.BlockSpec((tm * chunks, LANES), lambda m, f, be, used: (m, 0)),
                  pl.BlockSpec((1, d, tf), lambda m, f, be, used: (be[m], 0, fcol(m, f, used))),
                  pl.BlockSpec((1, d, tf), lambda m, f, be, used: (be[m], 0, fcol(m, f, used))),
                  pl.BlockSpec((1, tf, d), lambda m, f, be, used: (be[m], fcol(m, f, used), 0))],
        out_specs=pl.BlockSpec((tm * chunks, LANES), lambda m, f, be, used: (m, 0)),
        scratch_shapes=[pltpu.VMEM((tm, d), BF16), pltpu.VMEM((tm, d), F32)])
    return pl.pallas_call(
        _expert_kernel,
        grid_spec=grid_spec,
        out_shape=jax.ShapeDtypeStruct((n_slots * chunks, LANES), F32),
        compiler_params=_params("arbitrary", "arbitrary"),
        name="expert_ffn",
    )(block_expert, n_used, xs, wg.astype(BF16), wu.astype(BF16), wd.astype(BF16))


def _combine_kernel(d0_ref, d1_ref, ys_hbm, x_ref, gates_ref, mod_ref, o_ref, y0_ref, y1_ref, sem):
    rows, d = o_ref.shape
    chunks = d // LANES

    def start(r, c):
        _row_copy(ys_hbm, d0_ref[0, 0, r], y0_ref, r, sem.at[0], chunks).start(priority=0)
        _row_copy(ys_hbm, d1_ref[0, 0, r], y1_ref, r, sem.at[1], chunks).start(priority=1)
        return c

    def wait(r, c):
        _row_copy(ys_hbm, d0_ref[0, 0, r], y0_ref, r, sem.at[0], chunks).wait()
        _row_copy(ys_hbm, d1_ref[0, 0, r], y1_ref, r, sem.at[1], chunks).wait()
        return c

    lax.fori_loop(0, rows, start, 0)
    lax.fori_loop(0, rows, wait, 0)
    gates = gates_ref[...]
    for s in range(chunks):
        cols = slice(s * LANES, (s + 1) * LANES)
        y = (_load_row_tile_chunk(y0_ref, s, rows, chunks) * gates[:, 0:1]
             + _load_row_tile_chunk(y1_ref, s, rows, chunks) * gates[:, 1:2])
        o_ref[:, cols] = x_ref[:, cols] + mod_ref[0, 5:6, cols] * y


def _moe_combine(ys, dest0, dest1, gates, x2, mod, seq):
    t, d = x2.shape
    tm = min(GATHER_ROWS, seq)
    steps = t // tm
    per_batch = seq // tm
    idx_spec = pl.BlockSpec((1, 1, tm), lambda i: (i, 0, 0), memory_space=pltpu.SMEM)
    return pl.pallas_call(
        _combine_kernel,
        grid=(steps,),
        in_specs=[idx_spec, idx_spec, pl.BlockSpec(memory_space=pl.ANY),
                  pl.BlockSpec((tm, d), lambda i: (i, 0)),
                  pl.BlockSpec((tm, TOP_K_EXPERTS), lambda i: (i, 0)),
                  pl.BlockSpec((1, 6, d), lambda i: (i // per_batch, 0, 0))],
        out_specs=pl.BlockSpec((tm, d), lambda i: (i, 0)),
        out_shape=jax.ShapeDtypeStruct((t, d), F32),
        scratch_shapes=[pltpu.VMEM((tm * (d // LANES), LANES), F32), pltpu.VMEM((tm * (d // LANES), LANES), F32),
                        pltpu.SemaphoreType.DMA((2,))],
        compiler_params=_params("arbitrary"),
        name="moe_combine",
    )(dest0.reshape(steps, 1, tm), dest1.reshape(steps, 1, tm), ys, x2, gates, mod)


def _moe_ffn(x2, mod, g_ffn, w_router, wg, wu, wd, seq):
    t, d = x2.shape
    h, logits = _router(x2, mod, g_ffn, w_router, seq)
    top_val, top_idx = lax.top_k(logits[:, :N_EXPERTS], TOP_K_EXPERTS)
    gates = jax.nn.softmax(top_val, axis=-1)

    n_assign = t * TOP_K_EXPERTS
    flat_e = top_idx.reshape(-1).astype(jnp.int32)
    onehot = (flat_e[:, None] == jnp.arange(N_EXPERTS, dtype=jnp.int32)[None, :]).astype(jnp.int32)
    rank = jnp.take_along_axis(jnp.cumsum(onehot, axis=0), flat_e[:, None], axis=1)[:, 0] - 1
    counts = jnp.sum(onehot, axis=0)
    padded = (counts + EXPERT_ROWS - 1) // EXPERT_ROWS * EXPERT_ROWS
    pad_end = jnp.cumsum(padded)
    dest = (pad_end - padded)[flat_e] + rank
    n_slots = (n_assign // EXPERT_ROWS + N_EXPERTS) * EXPERT_ROWS
    n_blocks = n_slots // EXPERT_ROWS
    slot_tok = jnp.zeros((n_slots,), jnp.int32).at[dest].set(jnp.arange(n_assign, dtype=jnp.int32) // TOP_K_EXPERTS)
    block_start = jnp.arange(n_blocks, dtype=jnp.int32) * EXPERT_ROWS
    block_expert = jnp.minimum(jnp.searchsorted(pad_end, block_start, side="right"), N_EXPERTS - 1).astype(jnp.int32)
    n_used = (pad_end[-1:] // EXPERT_ROWS).astype(jnp.int32)

    xs = _gather_rows(h, slot_tok, GATHER_ROWS, d // LANES)
    ys = _expert_ffn(xs, block_expert, n_used, wg, wu, wd)
    dest2 = dest.reshape(t, TOP_K_EXPERTS)
    return _moe_combine(ys, dest2[:, 0], dest2[:, 1], gates, x2, mod, seq)


def _final_norm_kernel(x_ref, g_ref, o_ref):
    x = x_ref[...]
    o_ref[...] = x * lax.rsqrt(jnp.mean(x * x, axis=-1, keepdims=True) + NORM_EPS) * g_ref[...]


def _final_norm(x2, g_final):
    t, d = x2.shape
    tm = min(ROW_TILE, t)
    return pl.pallas_call(
        _final_norm_kernel,
        grid=(t // tm,),
        in_specs=[pl.BlockSpec((tm, d), lambda i: (i, 0)), pl.BlockSpec((1, d), lambda i: (0, 0))],
        out_specs=pl.BlockSpec((tm, d), lambda i: (i, 0)),
        out_shape=jax.ShapeDtypeStruct((t, d), F32),
        compiler_params=_params("arbitrary"),
        name="final_norm",
    )(x2, g_final.reshape(1, d))


def _token_mixer(x2, mod, g_mix, w_in, g_heads, w_out, batch, seq):
    pa, pb, pc, pd, p_idx, p_w = _in_projection(x2, mod, g_mix, _pack_w_in(w_in), seq)
    shape3 = lambda a: a.reshape(batch, seq, a.shape[-1])
    gh = g_heads.reshape(4, HEADS_PER_MIXER, HEAD_DIM)
    mixed = (_stick_breaking_mixer(shape3(pa), gh[0]),
             _dsa_mixer(shape3(pb), shape3(p_idx), shape3(p_w), gh[1]),
             _dilated_mixer(shape3(pc), gh[2]),
             _moba_mixer(shape3(pd), gh[3]))
    return _out_projection(mixed, w_out.astype(BF16), x2, mod, seq)


def kernel(x, c, w_ada, b_ada, g_mix, w_in, g_heads, w_out, g_ffn, w_ff_gate, w_ff_up, w_ff_down, w_router, w_exp_gate, w_exp_up, w_exp_down, g_final):
    batch, seq, d = x.shape
    depth = w_ada.shape[0]
    mods = _ada_modulation(c, w_ada, b_ada)
    x2 = x.reshape(batch * seq, d)
    for layer in range(depth):
        mod = mods[layer]
        x2 = _token_mixer(x2, mod, g_mix[layer], w_in[layer], g_heads[layer], w_out[layer], batch, seq)
        i = layer // 2
        if layer % 2 == 0:
            x2 = _dense_ffn(x2, mod, g_ffn[layer], w_ff_gate[i], w_ff_up[i], w_ff_down[i], seq)
        else:
            x2 = _moe_ffn(x2, mod, g_ffn[layer], w_router[i], w_exp_gate[i], w_exp_up[i], w_exp_down[i], seq)
    return _final_norm(x2, g_final).reshape(batch, seq, d)
```

```python
import functools

import numpy as np
import jax
import jax.numpy as jnp
from jax import lax
from jax.experimental import pallas as pl
from jax.experimental.pallas import tpu as pltpu

F32 = jnp.float32
BF16 = jnp.bfloat16

HEAD_DIM = 64
HEADS_PER_MIXER = 4
MIXER_WIDTH = HEADS_PER_MIXER * HEAD_DIM
QKV_WIDTH = 3 * MIXER_WIDTH
IDX_HEADS = 8
IDX_DIM = 64
DSA_TOPK = 256
DILATIONS = (1, 4, 16)
DILATED_STEPS = 128
MOBA_BLOCK = 256
MOBA_TOPK = 3
N_EXPERTS = 8
TOP_K_EXPERTS = 2
NORM_EPS = 1e-6
ATTN_SCALE = HEAD_DIM ** -0.5

LANES = 128
Q_TILE = 128
ROW_TILE = 512
FFN_TILE = 1408
EXPERT_ROWS = 512
EXPERT_FF_TILE = 1792
GATHER_ROWS = 512
VMEM_LIMIT = 56 * 1024 * 1024
MASKED = -1e30
SB_UNDERFLOW = 104.0

_NT = (((1,), (1,)), ((), ()))
_TN = (((0,), (0,)), ((), ()))


def _alibi_slopes(mixer_pos):
    idx = np.arange(HEADS_PER_MIXER, dtype=np.float32) * 3 + (mixer_pos + 1)
    return tuple(float(s) for s in np.exp2(-8.0 * idx / 12.0).astype(np.float32))


def _params(*semantics):
    return pltpu.CompilerParams(dimension_semantics=semantics, vmem_limit_bytes=VMEM_LIMIT)


def _modulated_norm(x, gain, shift, scale):
    y = x * lax.rsqrt(jnp.mean(x * x, axis=-1, keepdims=True) + NORM_EPS) * gain
    return y * (1.0 + scale) + shift


def _head_norm(acc, gain):
    return acc * lax.rsqrt(jnp.mean(acc * acc, axis=-1, keepdims=True) + NORM_EPS) * gain


def _head_cols(h):
    return slice(h * HEAD_DIM, (h + 1) * HEAD_DIM)


def _ada_kernel(c_ref, w_ref, b_ref, o_ref):
    c = c_ref[...]
    cond = c / (1.0 + jnp.exp(-c))
    o_ref[0, 0] = jnp.dot(cond, w_ref[0], preferred_element_type=F32,
                          precision=lax.Precision.HIGHEST) + b_ref[0, 0]


def _ada_modulation(c, w_ada, b_ada):
    depth, d, _ = w_ada.shape
    b = c.shape[0]
    out = pl.pallas_call(
        _ada_kernel,
        grid=(depth, 6),
        in_specs=[pl.BlockSpec((b, d), lambda l, k: (0, 0)),
                  pl.BlockSpec((1, d, d), lambda l, k: (l, 0, k)),
                  pl.BlockSpec((1, 1, 1, d), lambda l, k: (l, k, 0, 0))],
        out_specs=pl.BlockSpec((1, 1, b, d), lambda l, k: (l, k, 0, 0)),
        out_shape=jax.ShapeDtypeStruct((depth, 6, b, d), F32),
        compiler_params=_params("arbitrary", "arbitrary"),
        name="ada_modulation",
    )(c, w_ada, b_ada.reshape(depth, 6, 1, d))
    return out.transpose(0, 2, 1, 3)


IN_WIDTHS = (QKV_WIDTH, QKV_WIDTH, QKV_WIDTH, QKV_WIDTH, IDX_HEADS * IDX_DIM + LANES, LANES)


def _in_proj_kernel(x_ref, mod_ref, g_ref, w_ref, oa, ob, oc, od, oidx, ow, oc4, oc16, pc_ref):
    h = _modulated_norm(x_ref[...], g_ref[...], mod_ref[0, 0:1, :], mod_ref[0, 1:2, :]).astype(BF16)
    tm = x_ref.shape[0]
    off = 0
    for o_ref, width in zip((oa, ob, oc, od, oidx, ow), IN_WIDTHS):
        val = jnp.dot(h, w_ref[:, off:off + width], preferred_element_type=F32)
        o_ref[...] = val.astype(o_ref.dtype)
        if o_ref is oc:
            for s in range(QKV_WIDTH // LANES):
                pc_ref[s * tm:(s + 1) * tm, :] = val[:, s * LANES:(s + 1) * LANES]
        off += width
    for r, view in zip(DILATIONS[1:], (oc4, oc16)):
        for c in range(r):
            for s in range(QKV_WIDTH // LANES):
                col = c * QKV_WIDTH + s * LANES
                view[:, col:col + LANES] = pc_ref[pl.ds(s * tm + c, tm // r, stride=r), :].astype(view.dtype)


def _pack_w_in(w_in):
    d = w_in.shape[0]
    n_qkv = 4 * QKV_WIDTH
    n_qi = IDX_HEADS * IDX_DIM
    z = lambda n: jnp.zeros((d, n), w_in.dtype)
    return jnp.concatenate([w_in[:, :n_qkv + n_qi + IDX_DIM], z(LANES - IDX_DIM),
                            w_in[:, n_qkv + n_qi + IDX_DIM:], z(LANES - IDX_HEADS)], axis=1).astype(BF16)


def _in_projection(x2, mod, g_mix, w_packed, seq):
    t, d = x2.shape
    tm = min(ROW_TILE, seq)
    per_batch = seq // tm
    dtypes = (BF16, BF16, BF16, BF16, BF16, F32)
    views = DILATIONS[1:]
    return pl.pallas_call(
        _in_proj_kernel,
        grid=(t // tm,),
        in_specs=[pl.BlockSpec((tm, d), lambda i: (i, 0)),
                  pl.BlockSpec((1, 6, d), lambda i: (i // per_batch, 0, 0)),
                  pl.BlockSpec((1, d), lambda i: (0, 0)),
                  pl.BlockSpec(w_packed.shape, lambda i: (0, 0))],
        out_specs=([pl.BlockSpec((tm, w), lambda i: (i, 0)) for w in IN_WIDTHS]
                   + [pl.BlockSpec((tm // r, r * QKV_WIDTH), lambda i: (i, 0)) for r in views]),
        out_shape=([jax.ShapeDtypeStruct((t, w), dt) for w, dt in zip(IN_WIDTHS, dtypes)]
                   + [jax.ShapeDtypeStruct((t // r, r * QKV_WIDTH), BF16) for r in views]),
        scratch_shapes=[pltpu.VMEM((tm * (QKV_WIDTH // LANES), LANES), F32)],
        compiler_params=_params("arbitrary"),
        name="in_projection",
    )(x2, mod, g_mix.reshape(1, d), w_packed)


def _sb_kernel(q_ref, k_ref, v_ref, g_ref, o_ref):
    tq = q_ref.shape[1]
    i = pl.program_id(1)
    row = lax.broadcasted_iota(jnp.int32, (tq, 1), 0)
    lane = lax.broadcasted_iota(jnp.int32, (1, tq), 1)
    later = (lax.broadcasted_iota(jnp.int32, (tq, tq), 0) > lax.broadcasted_iota(jnp.int32, (tq, tq), 1))
    later = jnp.where(later, 1.0, 0.0).astype(BF16)

    qs = [q_ref[0, :, _head_cols(h)] * ATTN_SCALE for h in range(HEADS_PER_MIXER)]

    def block(h, start, tail, past=None):
        cols = _head_cols(h)
        k = k_ref[0, pl.ds(start, tq), cols]
        v = v_ref[0, pl.ds(start, tq), cols]
        z = lax.dot_general(qs[h], k, _NT, preferred_element_type=F32)
        softplus = jnp.maximum(z, 0.0) + jnp.log(1.0 + jnp.exp(-jnp.abs(z)))
        log_1m = -softplus if past is None else jnp.where(past, -softplus, 0.0)
        hi = log_1m.astype(BF16)
        lo = (log_1m - hi.astype(F32)).astype(BF16)
        after = (jnp.dot(hi, later, preferred_element_type=F32)
                 + jnp.dot(lo, later, preferred_element_type=F32) + tail)
        a = jnp.exp(z - softplus + after)
        if past is not None:
            a = jnp.where(past, a, 0.0)
        return (tail + jnp.sum(log_1m, axis=1, keepdims=True),
                jnp.dot(a.astype(BF16), v, preferred_element_type=F32))

    diag = pl.multiple_of(i * tq, tq)
    state = tuple(block(h, diag, jnp.zeros((tq, 1), F32), past=lane < row) for h in range(HEADS_PER_MIXER))

    def body(carry):
        j, state = carry
        start = pl.multiple_of(j * tq, tq)
        new_state = []
        for h in range(HEADS_PER_MIXER):
            tail, acc = state[h]
            tail, av = block(h, start, tail)
            new_state.append((tail, acc + av))
        return j - 1, tuple(new_state)

    def cond(carry):
        j, state = carry
        worst = functools.reduce(jnp.maximum, [tail for tail, _ in state])
        return jnp.logical_and(j >= 0, jnp.max(worst) > -SB_UNDERFLOW)

    _, state = lax.while_loop(cond, body, (i - 1, state))
    for h in range(HEADS_PER_MIXER):
        o_ref[0, :, _head_cols(h)] = _head_norm(state[h][1], g_ref[h:h + 1, :]).astype(o_ref.dtype)


def _stick_breaking_mixer(p, g_heads):
    b, seq, _ = p.shape
    tq = min(Q_TILE, seq)
    return pl.pallas_call(
        _sb_kernel,
        grid=(b, seq // tq),
        in_specs=[pl.BlockSpec((1, tq, MIXER_WIDTH), lambda bi, i: (bi, i, 0)),
                  pl.BlockSpec((1, seq, MIXER_WIDTH), lambda bi, i: (bi, 0, 1)),
                  pl.BlockSpec((1, seq, MIXER_WIDTH), lambda bi, i: (bi, 0, 2)),
                  pl.BlockSpec((HEADS_PER_MIXER, HEAD_DIM), lambda bi, i: (0, 0))],
        out_specs=pl.BlockSpec((1, tq, MIXER_WIDTH), lambda bi, i: (bi, i, 0)),
        out_shape=jax.ShapeDtypeStruct((b, seq, MIXER_WIDTH), BF16),
        compiler_params=_params("arbitrary", "arbitrary"),
        name="stick_breaking",
    )(p, p, p, g_heads)


def _dsa_kernel(q_ref, k_ref, v_ref, qi_ref, ki_ref, w_ref, g_ref, o_ref, sc_ref, s_ref, p_ref, *, slopes):
    tq = q_ref.shape[1]
    tk = sc_ref.shape[1]
    i = pl.program_id(1)
    n_blocks = ((i + 1) * tq + tk - 1) // tk
    qpos = i * tq + lax.broadcasted_iota(jnp.int32, (1, tq), 1)
    key_row = lax.broadcasted_iota(jnp.int32, (tk, 1), 0)
    neg_inf = float("-inf")

    w_t = w_ref[0].T * (IDX_HEADS ** -0.5 * IDX_DIM ** -0.5)
    w_rows = [w_t[h:h + 1, :] for h in range(IDX_HEADS)]
    qi = qi_ref[0]

    def score_body(j, carry):
        start = pl.multiple_of(j * tk, tk)
        ki = ki_ref[0, pl.ds(start, tk), 0:IDX_DIM]
        sc = jnp.zeros((tk, tq), F32)
        for h in range(IDX_HEADS):
            x = lax.dot_general(ki, qi[:, h * IDX_DIM:(h + 1) * IDX_DIM], _NT, preferred_element_type=F32)
            sc = sc + w_rows[h] * jnp.maximum(x, 0.0)
        sc_ref[j] = jnp.where((start + key_row) <= qpos, sc + 0.0, neg_inf)
        return carry

    lax.fori_loop(0, n_blocks, score_body, 0)

    def count(pred):
        def add_block(j, cnt):
            c = jnp.where(pred(sc_ref[j]), 1.0, 0.0)
            return cnt + jnp.sum(c.reshape(tk // 8, 8, tq), axis=0)

        def pair(jj, cnt):
            return add_block(2 * jj + 1, add_block(2 * jj, cnt))

        cnt = lax.fori_loop(0, n_blocks // 2, pair, jnp.zeros((8, tq), F32))
        cnt = lax.cond(n_blocks % 2 == 1, lambda c: add_block(n_blocks - 1, c), lambda c: c, cnt)
        return jnp.sum(cnt, axis=0, keepdims=True)

    int_min = jnp.int32(-2 ** 31)

    def ordered_to_float(u):
        key = u ^ int_min
        bits = jnp.where(key >= 0, key, key ^ jnp.int32(0x7FFFFFFF))
        return lax.bitcast_convert_type(bits, F32)

    def search_body(step, carry):
        prefix, n_at_prefix = carry
        cand = prefix | jnp.left_shift(jnp.int32(1), 31 - step)
        cand_f = ordered_to_float(cand)
        n_ge = count(lambda s: s >= cand_f)
        keep = n_ge >= DSA_TOPK
        return jnp.where(keep, cand, prefix), jnp.where(keep, n_ge, n_at_prefix)

    prefix, n_ge_tau = lax.fori_loop(0, 32, search_body,
                                     (jnp.zeros((1, tq), jnp.int32), jnp.zeros((1, tq), F32)))
    keep_all = (qpos + 1) <= DSA_TOPK
    tau = jnp.where(keep_all, float(np.finfo(np.float32).min), ordered_to_float(prefix))
    no_ties = jnp.min(jnp.where(jnp.logical_or(keep_all, n_ge_tau == DSA_TOPK), 1.0, 0.0)) > 0.5

    earlier = (lax.broadcasted_iota(jnp.int32, (tk, tk), 1) < lax.broadcasted_iota(jnp.int32, (tk, tk), 0))
    earlier = jnp.where(earlier, 1.0, 0.0).astype(BF16)
    qs = [q_ref[0, :, _head_cols(h)] * ATTN_SCALE for h in range(HEADS_PER_MIXER)]
    key_bias = [slopes[h] * lax.broadcasted_iota(jnp.int32, (tk, tq), 0).astype(F32)
                for h in range(HEADS_PER_MIXER)]

    def attn_body(j, carry, n_ties=None):
        ties_seen, stats = carry
        start = pl.multiple_of(j * tk, tk)
        sc = sc_ref[j]
        if n_ties is None:
            sel = sc >= tau
        else:
            tie = jnp.where(sc == tau, 1.0, 0.0)
            rank = jnp.dot(earlier, tie.astype(BF16), preferred_element_type=F32) + ties_seen
            sel = jnp.where(sc > tau, 1.0, jnp.where(rank < n_ties, tie, 0.0)) > 0.5
            ties_seen = ties_seen + jnp.sum(tie, axis=0, keepdims=True)
        block_pos = (j * tk - i * tq).astype(F32)
        for h in range(HEADS_PER_MIXER):
            k = k_ref[0, pl.ds(start, tk), _head_cols(h)]
            s = lax.dot_general(k, qs[h], _NT, preferred_element_type=F32) + key_bias[h]
            s_ref[h] = jnp.where(sel, s, MASKED)
        locals_ = []
        for h in range(HEADS_PER_MIXER):
            s = s_ref[h]
            m_loc = jnp.max(s, axis=0, keepdims=True)
            p = jnp.exp(s - m_loc)
            locals_.append((m_loc, jnp.sum(p, axis=0, keepdims=True)))
            p_ref[h] = p.astype(BF16)
        new_stats = []
        for h in range(HEADS_PER_MIXER):
            m, l, acc = stats[h]
            m_loc, l_blk = locals_[h]
            v = v_ref[0, pl.ds(start, tk), _head_cols(h)]
            pv_blk = lax.dot_general(v, p_ref[h], _TN, preferred_element_type=F32)
            m_blk = m_loc + slopes[h] * block_pos
            m_new = jnp.maximum(m, m_blk)
            w_old = jnp.exp(m - m_new)
            w_blk = jnp.where(m_loc > 0.5 * MASKED, jnp.exp(m_blk - m_new), 0.0)
            new_stats.append((m_new, w_old * l + w_blk * l_blk, w_old * acc + w_blk * pv_blk))
        return ties_seen, tuple(new_stats)

    init = tuple((jnp.full((1, tq), MASKED, F32), jnp.zeros((1, tq), F32), jnp.zeros((HEAD_DIM, tq), F32))
                 for _ in range(HEADS_PER_MIXER))
    start_carry = (jnp.zeros((1, tq), F32), init)

    def attend_without_ties():
        return lax.fori_loop(0, n_blocks, attn_body, start_carry)[1]

    def attend_with_ties():
        n_gt = count(lambda s: s > tau)
        n_ties = jnp.where(keep_all, 1e9, DSA_TOPK - n_gt)
        return lax.fori_loop(0, n_blocks, functools.partial(attn_body, n_ties=n_ties), start_carry)[1]

    stats = lax.cond(no_ties, attend_without_ties, attend_with_ties)
    outs = []
    for h in range(HEADS_PER_MIXER):
        _, l, acc = stats[h]
        o = acc / l
        outs.append(o * lax.rsqrt(jnp.mean(o * o, axis=0, keepdims=True) + NORM_EPS) * g_ref[:, h:h + 1])
    o_ref[0] = jnp.concatenate(outs, axis=0).T.astype(o_ref.dtype)


def _dsa_mixer(p, p_idx, p_w, g_heads):
    b, seq, _ = p.shape
    tq = min(2 * Q_TILE, seq)
    tk = tq
    qi_width = IDX_HEADS * IDX_DIM
    return pl.pallas_call(
        functools.partial(_dsa_kernel, slopes=_alibi_slopes(0)),
        grid=(b, seq // tq),
        in_specs=[pl.BlockSpec((1, tq, MIXER_WIDTH), lambda bi, i: (bi, i, 0)),
                  pl.BlockSpec((1, seq, MIXER_WIDTH), lambda bi, i: (bi, 0, 1)),
                  pl.BlockSpec((1, seq, MIXER_WIDTH), lambda bi, i: (bi, 0, 2)),
                  pl.BlockSpec((1, tq, qi_width), lambda bi, i: (bi, i, 0)),
                  pl.BlockSpec((1, seq, LANES), lambda bi, i: (bi, 0, qi_width // LANES)),
                  pl.BlockSpec((1, tq, LANES), lambda bi, i: (bi, i, 0)),
                  pl.BlockSpec((HEAD_DIM, HEADS_PER_MIXER), lambda bi, i: (0, 0))],
        out_specs=pl.BlockSpec((1, tq, MIXER_WIDTH), lambda bi, i: (bi, i, 0)),
        out_shape=jax.ShapeDtypeStruct((b, seq, MIXER_WIDTH), BF16),
        scratch_shapes=[pltpu.VMEM((seq // tk, tk, tq), F32),
                        pltpu.VMEM((HEADS_PER_MIXER, tk, tq), F32),
                        pltpu.VMEM((HEADS_PER_MIXER, tk, tq), BF16)],
        compiler_params=_params("arbitrary", "arbitrary"),
        name="dsa",
    )(p, p, p, p_idx, p_idx, p_w, g_heads.T)


def _band_kernel(q_ref, kp_ref, kc_ref, vp_ref, vc_ref, o_ref, lse_ref, *, dilation, slopes):
    tq = q_ref.shape[1]
    ui = pl.program_id(2)
    u_q = ui * tq + lax.broadcasted_iota(jnp.int32, (tq, 1), 0)
    u_k = (ui - 1) * tq + lax.broadcasted_iota(jnp.int32, (1, 2 * tq), 1)
    steps = u_q - u_k
    valid = jnp.logical_and(jnp.logical_and(steps >= 0, steps <= DILATED_STEPS), u_k >= 0)
    dist = (steps * dilation).astype(F32)
    lane = lax.broadcasted_iota(jnp.int32, (1, LANES), 1)
    lse_all = jnp.zeros((tq, LANES), F32)
    for h in range(HEADS_PER_MIXER):
        cols = _head_cols(h)
        k = jnp.concatenate([kp_ref[0, :, cols], kc_ref[0, :, cols]], axis=0)
        v = jnp.concatenate([vp_ref[0, :, cols], vc_ref[0, :, cols]], axis=0)
        s = lax.dot_general(q_ref[0, :, cols], k, _NT, preferred_element_type=F32) * ATTN_SCALE - slopes[h] * dist
        s = jnp.where(valid, s, MASKED)
        m = jnp.max(s, axis=1, keepdims=True)
        e = jnp.where(valid, jnp.exp(s - m), 0.0)
        den = jnp.sum(e, axis=1, keepdims=True)
        o_ref[0, :, cols] = jnp.dot(e.astype(BF16), v, preferred_element_type=F32) / den
        lse_all = lse_all + jnp.where(lane == h, m + jnp.log(den), 0.0)
    lse_ref[0] = lse_all


def _dilated_branch(view, dilation, slopes):
    b, length, _ = view.shape
    classes = dilation
    tq = min(Q_TILE, length)
    spec = lambda part, prev: pl.BlockSpec(
        (1, tq, MIXER_WIDTH),
        (lambda bi, c, ui: (bi, jnp.maximum(ui - 1, 0), c * 3 + part)) if prev
        else (lambda bi, c, ui: (bi, ui, c * 3 + part)))
    out, lse = pl.pallas_call(
        functools.partial(_band_kernel, dilation=dilation, slopes=slopes),
        grid=(b, classes, length // tq),
        in_specs=[spec(0, False), spec(1, True), spec(1, False), spec(2, True), spec(2, False)],
        out_specs=[pl.BlockSpec((1, tq, MIXER_WIDTH), lambda bi, c, ui: (bi, ui, c)),
                   pl.BlockSpec((1, tq, LANES), lambda bi, c, ui: (bi, ui, c))],
        out_shape=[jax.ShapeDtypeStruct((b, length, classes * MIXER_WIDTH), F32),
                   jax.ShapeDtypeStruct((b, length, classes * LANES), F32)],
        compiler_params=_params("arbitrary", "arbitrary", "arbitrary"),
        name=f"dilated_r{dilation}",
    )(view, view, view, view, view)
    return out, lse


def _dilated_merge_kernel(o1, o2, o3, l1, l2, l3, g_ref, o_ref, *scratch):
    tq = o_ref.shape[1]
    chunks = MIXER_WIDTH // LANES
    outs = [[o1[0, :, s * LANES:(s + 1) * LANES] for s in range(chunks)]]
    lses = [l1[0]]
    for r, o_view, l_view, o_nat, l_nat in zip(DILATIONS[1:], (o2, o3), (l2, l3), scratch[0::2], scratch[1::2]):
        for c in range(r):
            for s in range(chunks):
                col = c * MIXER_WIDTH + s * LANES
                o_nat[pl.ds(s * tq + c, tq // r, stride=r), :] = o_view[0, :, col:col + LANES]
            l_nat[pl.ds(c, tq // r, stride=r), :] = l_view[0, :, c * LANES:(c + 1) * LANES]
        outs.append([o_nat[s * tq:(s + 1) * tq, :] for s in range(chunks)])
        lses.append(l_nat[...])
    heads_per_chunk = LANES // HEAD_DIM
    for h in range(HEADS_PER_MIXER):
        sub = slice((h % heads_per_chunk) * HEAD_DIM, (h % heads_per_chunk + 1) * HEAD_DIM)
        lse_h = [l[:, h:h + 1] for l in lses]
        top = functools.reduce(jnp.maximum, lse_h)
        wts = [jnp.exp(l - top) for l in lse_h]
        mixed = sum(w * o[h // heads_per_chunk][:, sub] for w, o in zip(wts, outs)) / sum(wts)
        o_ref[0, :, _head_cols(h)] = _head_norm(mixed, g_ref[h:h + 1, :]).astype(o_ref.dtype)


def _dilated_mixer(views, g_heads):
    b, seq, _ = views[0].shape
    slopes = _alibi_slopes(1)
    branches = [_dilated_branch(v, r, slopes) for v, r in zip(views, DILATIONS)]
    tq = min(ROW_TILE, seq)
    o_specs = [pl.BlockSpec((1, tq // r, r * MIXER_WIDTH), lambda bi, i: (bi, i, 0)) for r in DILATIONS]
    l_specs = [pl.BlockSpec((1, tq // r, r * LANES), lambda bi, i: (bi, i, 0)) for r in DILATIONS]
    scratch = []
    for _ in DILATIONS[1:]:
        scratch += [pltpu.VMEM((tq * (MIXER_WIDTH // LANES), LANES), F32), pltpu.VMEM((tq, LANES), F32)]
    return pl.pallas_call(
        _dilated_merge_kernel,
        grid=(b, seq // tq),
        in_specs=o_specs + l_specs + [pl.BlockSpec((HEADS_PER_MIXER, HEAD_DIM), lambda bi, i: (0, 0))],
        out_specs=pl.BlockSpec((1, tq, MIXER_WIDTH), lambda bi, i: (bi, i, 0)),
        out_shape=jax.ShapeDtypeStruct((b, seq, MIXER_WIDTH), BF16),
        scratch_shapes=scratch,
        compiler_params=_params("arbitrary", "arbitrary"),
        name="dilated_merge",
    )(*[o for o, _ in branches], *[l for _, l in branches], g_heads)


def _moba_kernel(q_ref, k_ref, v_ref, g_ref, o_ref, kmean_ref, chosen_ref, s_ref, p_ref, *, slopes):
    tq = q_ref.shape[1]
    n_kv = k_ref.shape[1] // MOBA_BLOCK
    own = pl.program_id(1)

    @pl.when(own == 0)
    def _():
        kmean_ref[...] = jnp.zeros_like(kmean_ref)
        for n in range(n_kv):
            blk = k_ref[0, n * MOBA_BLOCK:(n + 1) * MOBA_BLOCK, :].astype(F32)
            kmean_ref[n:n + 1, :] = jnp.sum(blk, axis=0, keepdims=True) * (1.0 / MOBA_BLOCK)

    key_row = lax.broadcasted_iota(jnp.int32, (MOBA_BLOCK, 1), 0)
    q_lane = lax.broadcasted_iota(jnp.int32, (1, tq), 1)
    blk_row = lax.broadcasted_iota(jnp.int32, (LANES, 1), 0)
    blk_f = blk_row.astype(F32)
    neg_inf = float("-inf")

    qs = []
    for h in range(HEADS_PER_MIXER):
        q = q_ref[0, :, _head_cols(h)]
        gate = lax.dot_general(kmean_ref[:, _head_cols(h)], q.astype(F32), _NT, preferred_element_type=F32,
                               precision=lax.Precision.HIGHEST)
        gate = jnp.where(blk_row < own, gate, neg_inf)
        picks = jnp.zeros((LANES, tq), F32)
        for _ in range(MOBA_TOPK):
            top = jnp.max(gate, axis=0, keepdims=True)
            is_top = jnp.logical_and(gate == top, top > neg_inf)
            first = jnp.min(jnp.where(is_top, blk_f, float(LANES)), axis=0, keepdims=True)
            pick = blk_f == first
            picks = jnp.where(pick, 1.0, picks)
            gate = jnp.where(pick, neg_inf, gate)
        chosen_ref[h] = picks
        qs.append(q * ATTN_SCALE)

    key_bias = [slopes[h] * lax.broadcasted_iota(jnp.int32, (MOBA_BLOCK, tq), 0).astype(F32)
                for h in range(HEADS_PER_MIXER)]

    def block_softmax(start, keep=None):
        for h in range(HEADS_PER_MIXER):
            k = k_ref[0, pl.ds(start, MOBA_BLOCK), _head_cols(h)]
            s = lax.dot_general(k, qs[h], _NT, preferred_element_type=F32) + key_bias[h]
            s_ref[h] = s if keep is None else jnp.where(keep, s, MASKED)
        pieces = []
        for h in range(HEADS_PER_MIXER):
            s = s_ref[h]
            m_loc = jnp.max(s, axis=0, keepdims=True)
            p = jnp.exp(s - m_loc)
            pieces.append((m_loc, jnp.sum(p, axis=0, keepdims=True)))
            p_ref[h] = p.astype(BF16)
        out = []
        for h in range(HEADS_PER_MIXER):
            v = v_ref[0, pl.ds(start, MOBA_BLOCK), _head_cols(h)]
            out.append(pieces[h] + (lax.dot_general(v, p_ref[h], _TN, preferred_element_type=F32),))
        return out

    own_start = pl.multiple_of(own * MOBA_BLOCK, MOBA_BLOCK)
    stats = tuple(block_softmax(own_start, keep=key_row <= q_lane))

    def body(j, stats):
        start = pl.multiple_of(j * MOBA_BLOCK, MOBA_BLOCK)
        block_pos = ((j - own) * MOBA_BLOCK).astype(F32)
        merged = []
        for h, (m_loc, l_blk, pv_blk) in enumerate(block_softmax(start)):
            picked = chosen_ref[h, pl.ds(j, 1), :] > 0.5
            m_blk = jnp.where(picked, m_loc + slopes[h] * block_pos, MASKED)
            m, l, acc = stats[h]
            m_new = jnp.maximum(m, m_blk)
            w_old = jnp.exp(m - m_new)
            w_blk = jnp.exp(m_blk - m_new)
            merged.append((m_new, w_old * l + w_blk * l_blk, w_old * acc + w_blk * pv_blk))
        return tuple(merged)

    stats = lax.fori_loop(0, own, body, stats)
    outs = []
    for h in range(HEADS_PER_MIXER):
        _, l, acc = stats[h]
        o = acc / l
        outs.append(o * lax.rsqrt(jnp.mean(o * o, axis=0, keepdims=True) + NORM_EPS) * g_ref[:, h:h + 1])
    o_ref[0] = jnp.concatenate(outs, axis=0).T.astype(o_ref.dtype)


def _moba_mixer(p, g_heads):
    b, seq, _ = p.shape
    tq = MOBA_BLOCK
    return pl.pallas_call(
        functools.partial(_moba_kernel, slopes=_alibi_slopes(2)),
        grid=(b, seq // tq),
        in_specs=[pl.BlockSpec((1, tq, MIXER_WIDTH), lambda bi, i: (bi, i, 0)),
                  pl.BlockSpec((1, seq, MIXER_WIDTH), lambda bi, i: (bi, 0, 1)),
                  pl.BlockSpec((1, seq, MIXER_WIDTH), lambda bi, i: (bi, 0, 2)),
                  pl.BlockSpec((HEAD_DIM, HEADS_PER_MIXER), lambda bi, i: (0, 0))],
        out_specs=pl.BlockSpec((1, tq, MIXER_WIDTH), lambda bi, i: (bi, i, 0)),
        out_shape=jax.ShapeDtypeStruct((b, seq, MIXER_WIDTH), BF16),
        scratch_shapes=[pltpu.VMEM((LANES, MIXER_WIDTH), F32),
                        pltpu.VMEM((HEADS_PER_MIXER, LANES, tq), F32),
                        pltpu.VMEM((HEADS_PER_MIXER, MOBA_BLOCK, tq), F32),
                        pltpu.VMEM((HEADS_PER_MIXER, MOBA_BLOCK, tq), BF16)],
        compiler_params=_params("arbitrary", "arbitrary"),
        name="moba",
    )(p, p, p, g_heads.T)


def _out_proj_kernel(oa, ob, oc, od, w_ref, x_ref, mod_ref, o_ref):
    acc = jnp.zeros(x_ref.shape, F32)
    for m, o in enumerate((oa, ob, oc, od)):
        acc = acc + jnp.dot(o[...], w_ref[m * MIXER_WIDTH:(m + 1) * MIXER_WIDTH, :], preferred_element_type=F32)
    o_ref[...] = x_ref[...] + mod_ref[0, 2:3, :] * acc


def _out_projection(mixed, w_out, x2, mod, seq):
    t, d = x2.shape
    tm = min(ROW_TILE, seq)
    per_batch = seq // tm
    o_spec = pl.BlockSpec((tm, MIXER_WIDTH), lambda i: (i, 0))
    return pl.pallas_call(
        _out_proj_kernel,
        grid=(t // tm,),
        in_specs=[o_spec] * 4 + [pl.BlockSpec(w_out.shape, lambda i: (0, 0)),
                                 pl.BlockSpec((tm, d), lambda i: (i, 0)),
                                 pl.BlockSpec((1, 6, d), lambda i: (i // per_batch, 0, 0))],
        out_specs=pl.BlockSpec((tm, d), lambda i: (i, 0)),
        out_shape=jax.ShapeDtypeStruct((t, d), F32),
        compiler_params=_params("arbitrary"),
        name="out_projection",
    )(*[o.reshape(t, MIXER_WIDTH) for o in mixed], w_out, x2, mod)


def _ffn_kernel(x_ref, mod_ref, g_ref, wg_ref, wu_ref, wd_ref, o_ref, h_ref, acc_ref):
    f = pl.program_id(1)

    @pl.when(f == 0)
    def _():
        h_ref[...] = _modulated_norm(x_ref[...], g_ref[...], mod_ref[0, 3:4, :], mod_ref[0, 4:5, :]).astype(BF16)
        acc_ref[...] = jnp.zeros_like(acc_ref)

    h = h_ref[...]
    gate = jnp.dot(h, wg_ref[...], preferred_element_type=F32)
    up = jnp.dot(h, wu_ref[...], preferred_element_type=F32)
    act = (gate / (1.0 + jnp.exp(-gate)) * up).astype(BF16)
    acc_ref[...] += jnp.dot(act, wd_ref[...], preferred_element_type=F32)

    @pl.when(f == pl.num_programs(1) - 1)
    def _():
        o_ref[...] = x_ref[...] + mod_ref[0, 5:6, :] * acc_ref[...]


def _dense_ffn(x2, mod, g_ffn, wg, wu, wd, seq):
    t, d = x2.shape
    d_ff = wg.shape[1]
    tm = min(ROW_TILE, seq)
    tf = FFN_TILE if d_ff % FFN_TILE == 0 else d_ff
    per_batch = seq // tm
    return pl.pallas_call(
        _ffn_kernel,
        grid=(t // tm, d_ff // tf),
        in_specs=[pl.BlockSpec((tm, d), lambda i, f: (i, 0)),
                  pl.BlockSpec((1, 6, d), lambda i, f: (i // per_batch, 0, 0)),
                  pl.BlockSpec((1, d), lambda i, f: (0, 0)),
                  pl.BlockSpec((d, tf), lambda i, f: (0, f)),
                  pl.BlockSpec((d, tf), lambda i, f: (0, f)),
                  pl.BlockSpec((tf, d), lambda i, f: (f, 0))],
        out_specs=pl.BlockSpec((tm, d), lambda i, f: (i, 0)),
        out_shape=jax.ShapeDtypeStruct((t, d), F32),
        scratch_shapes=[pltpu.VMEM((tm, d), BF16), pltpu.VMEM((tm, d), F32)],
        compiler_params=_params("arbitrary", "arbitrary"),
        name="dense_ffn",
    )(x2, mod, g_ffn.reshape(1, d), wg.astype(BF16), wu.astype(BF16), wd.astype(BF16))


def _store_row_tiles(dst_ref, value):
    rows, d = value.shape
    chunks = d // LANES
    for s in range(chunks):
        dst_ref[pl.ds(s, rows, stride=chunks), :] = value[:, s * LANES:(s + 1) * LANES]


def _load_row_tile_chunk(src_ref, s, rows, chunks):
    return src_ref[pl.ds(s, rows, stride=chunks), :]


def _router_kernel(x_ref, mod_ref, g_ref, wr_ref, h_ref, logit_ref):
    h = _modulated_norm(x_ref[...], g_ref[...], mod_ref[0, 3:4, :], mod_ref[0, 4:5, :])
    _store_row_tiles(h_ref, h)
    logit_ref[...] = jnp.dot(h, wr_ref[...], preferred_element_type=F32, precision=lax.Precision.HIGHEST)


def _router(x2, mod, g_ffn, w_router, seq):
    t, d = x2.shape
    tm = min(ROW_TILE, seq)
    per_batch = seq // tm
    wr = jnp.zeros((d, LANES), F32).at[:, :N_EXPERTS].set(w_router.astype(F32))
    return pl.pallas_call(
        _router_kernel,
        grid=(t // tm,),
        in_specs=[pl.BlockSpec((tm, d), lambda i: (i, 0)),
                  pl.BlockSpec((1, 6, d), lambda i: (i // per_batch, 0, 0)),
                  pl.BlockSpec((1, d), lambda i: (0, 0)),
                  pl.BlockSpec((d, LANES), lambda i: (0, 0))],
        out_specs=[pl.BlockSpec((tm * (d // LANES), LANES), lambda i: (i, 0)),
                   pl.BlockSpec((tm, LANES), lambda i: (i, 0))],
        out_shape=[jax.ShapeDtypeStruct((t * (d // LANES), LANES), F32), jax.ShapeDtypeStruct((t, LANES), F32)],
        compiler_params=_params("arbitrary"),
        name="moe_router",
    )(x2, mod, g_ffn.reshape(1, d), wr)


def _row_copy(src_hbm, row, dst_ref, r, sem, chunks):
    src = src_hbm.at[pl.ds(pl.multiple_of(row * chunks, chunks), chunks), :]
    return pltpu.make_async_copy(src, dst_ref.at[pl.ds(pl.multiple_of(r * chunks, chunks), chunks), :], sem)


def _gather_kernel(idx_ref, src_hbm, o_ref, sem, *, chunks):
    rows = o_ref.shape[0] // chunks

    def start(pair, c):
        for lane in range(2):
            r = 2 * pair + lane
            _row_copy(src_hbm, idx_ref[0, 0, r], o_ref, r, sem.at[lane], chunks).start(priority=lane)
        return c

    def wait(pair, c):
        for lane in range(2):
            r = 2 * pair + lane
            _row_copy(src_hbm, idx_ref[0, 0, r], o_ref, r, sem.at[lane], chunks).wait()
        return c

    lax.fori_loop(0, rows // 2, start, 0)
    lax.fori_loop(0, rows // 2, wait, 0)


def _gather_rows(src, idx, rows_per_step, chunks):
    n = idx.shape[0]
    steps = n // rows_per_step
    return pl.pallas_call(
        functools.partial(_gather_kernel, chunks=chunks),
        grid=(steps,),
        in_specs=[pl.BlockSpec((1, 1, rows_per_step), lambda i: (i, 0, 0), memory_space=pltpu.SMEM),
                  pl.BlockSpec(memory_space=pl.ANY)],
        out_specs=pl.BlockSpec((rows_per_step * chunks, LANES), lambda i: (i, 0)),
        out_shape=jax.ShapeDtypeStruct((n * chunks, LANES), src.dtype),
        scratch_shapes=[pltpu.SemaphoreType.DMA((2,))],
        compiler_params=_params("arbitrary"),
        name="gather_rows",
    )(idx.reshape(steps, 1, rows_per_step), src)


def _expert_kernel(be_ref, used_ref, xs_ref, wg_ref, wu_ref, wd_ref, o_ref, xb_ref, acc_ref):
    m = pl.program_id(0)
    f = pl.program_id(1)
    live = m < used_ref[0]
    rows, d = xb_ref.shape
    chunks = d // LANES

    @pl.when(jnp.logical_and(live, f == 0))
    def _():
        for s in range(chunks):
            xb_ref[:, s * LANES:(s + 1) * LANES] = _load_row_tile_chunk(xs_ref, s, rows, chunks).astype(BF16)
        acc_ref[...] = jnp.zeros_like(acc_ref)

    @pl.when(live)
    def _():
        h = xb_ref[...]
        gate = jnp.dot(h, wg_ref[0], preferred_element_type=F32)
        up = jnp.dot(h, wu_ref[0], preferred_element_type=F32)
        act = (gate / (1.0 + jnp.exp(-gate)) * up).astype(BF16)
        acc_ref[...] += jnp.dot(act, wd_ref[0], preferred_element_type=F32)

    @pl.when(f == pl.num_programs(1) - 1)
    def _():
        _store_row_tiles(o_ref, jnp.where(live, acc_ref[...], 0.0))


def _expert_ffn(xs, block_expert, n_used, wg, wu, wd):
    d = wg.shape[1]
    chunks = d // LANES
    n_slots = xs.shape[0] // chunks
    d_ff = wg.shape[2]
    tm = EXPERT_ROWS
    tf = EXPERT_FF_TILE if d_ff % EXPERT_FF_TILE == 0 else d_ff
    nf = d_ff // tf

    def fcol(m, f, used):
        return jnp.where(m < used[0], f, nf - 1)

    grid_spec = pltpu.PrefetchScalarGridSpec(
        num_scalar_prefetch=2,
        grid=(n_slots // tm, nf),
        in_specs=[pl.BlockSpec((tm * chunks, LANES), lambda m, f, be, used: (m, 0)),
                  pl.BlockSpec((1, d, tf), lambda m, f, be, used: (be[m], 0, fcol(m, f, used))),
                  pl.BlockSpec((1, d, tf), lambda m, f, be, used: (be[m], 0, fcol(m, f, used))),
                  pl.BlockSpec((1, tf, d), lambda m, f, be, used: (be[m], fcol(m, f, used), 0))],
        out_specs=pl.BlockSpec((tm * chunks, LANES), lambda m, f, be, used: (m, 0)),
        scratch_shapes=[pltpu.VMEM((tm, d), BF16), pltpu.VMEM((tm, d), F32)])
    return pl.pallas_call(
        _expert_kernel,
        grid_spec=grid_spec,
        out_shape=jax.ShapeDtypeStruct((n_slots * chunks, LANES), F32),
        compiler_params=_params("arbitrary", "arbitrary"),
        name="expert_ffn",
    )(block_expert, n_used, xs, wg.astype(BF16), wu.astype(BF16), wd.astype(BF16))


def _combine_kernel(d0_ref, d1_ref, ys_hbm, x_ref, gates_ref, mod_ref, o_ref, y0_ref, y1_ref, sem):
    rows, d = o_ref.shape
    chunks = d // LANES

    def start(r, c):
        _row_copy(ys_hbm, d0_ref[0, 0, r], y0_ref, r, sem.at[0], chunks).start(priority=0)
        _row_copy(ys_hbm, d1_ref[0, 0, r], y1_ref, r, sem.at[1], chunks).start(priority=1)
        return c

    def wait(r, c):
        _row_copy(ys_hbm, d0_ref[0, 0, r], y0_ref, r, sem.at[0], chunks).wait()
        _row_copy(ys_hbm, d1_ref[0, 0, r], y1_ref, r, sem.at[1], chunks).wait()
        return c

    lax.fori_loop(0, rows, start, 0)
    lax.fori_loop(0, rows, wait, 0)
    gates = gates_ref[...]
    for s in range(chunks):
        cols = slice(s * LANES, (s + 1) * LANES)
        y = (_load_row_tile_chunk(y0_ref, s, rows, chunks) * gates[:, 0:1]
             + _load_row_tile_chunk(y1_ref, s, rows, chunks) * gates[:, 1:2])
        o_ref[:, cols] = x_ref[:, cols] + mod_ref[0, 5:6, cols] * y


def _moe_combine(ys, dest0, dest1, gates, x2, mod, seq):
    t, d = x2.shape
    tm = min(GATHER_ROWS, seq)
    steps = t // tm
    per_batch = seq // tm
    idx_spec = pl.BlockSpec((1, 1, tm), lambda i: (i, 0, 0), memory_space=pltpu.SMEM)
    return pl.pallas_call(
        _combine_kernel,
        grid=(steps,),
        in_specs=[idx_spec, idx_spec, pl.BlockSpec(memory_space=pl.ANY),
                  pl.BlockSpec((tm, d), lambda i: (i, 0)),
                  pl.BlockSpec((tm, TOP_K_EXPERTS), lambda i: (i, 0)),
                  pl.BlockSpec((1, 6, d), lambda i: (i // per_batch, 0, 0))],
        out_specs=pl.BlockSpec((tm, d), lambda i: (i, 0)),
        out_shape=jax.ShapeDtypeStruct((t, d), F32),
        scratch_shapes=[pltpu.VMEM((tm * (d // LANES), LANES), F32), pltpu.VMEM((tm * (d // LANES), LANES), F32),
                        pltpu.SemaphoreType.DMA((2,))],
        compiler_params=_params("arbitrary"),
        name="moe_combine",
    )(dest0.reshape(steps, 1, tm), dest1.reshape(steps, 1, tm), ys, x2, gates, mod)


def _moe_ffn(x2, mod, g_ffn, w_router, wg, wu, wd, seq):
    t, d = x2.shape
    h, logits = _router(x2, mod, g_ffn, w_router, seq)
    top_val, top_idx = lax.top_k(logits[:, :N_EXPERTS], TOP_K_EXPERTS)
    gates = jax.nn.softmax(top_val, axis=-1)

    n_assign = t * TOP_K_EXPERTS
    flat_e = top_idx.reshape(-1).astype(jnp.int32)
    onehot = (flat_e[:, None] == jnp.arange(N_EXPERTS, dtype=jnp.int32)[None, :]).astype(jnp.int32)
    rank = jnp.take_along_axis(jnp.cumsum(onehot, axis=0), flat_e[:, None], axis=1)[:, 0] - 1
    counts = jnp.sum(onehot, axis=0)
    padded = (counts + EXPERT_ROWS - 1) // EXPERT_ROWS * EXPERT_ROWS
    pad_end = jnp.cumsum(padded)
    dest = (pad_end - padded)[flat_e] + rank
    n_slots = (n_assign // EXPERT_ROWS + N_EXPERTS) * EXPERT_ROWS
    n_blocks = n_slots // EXPERT_ROWS
    slot_tok = jnp.zeros((n_slots,), jnp.int32).at[dest].set(jnp.arange(n_assign, dtype=jnp.int32) // TOP_K_EXPERTS)
    block_start = jnp.arange(n_blocks, dtype=jnp.int32) * EXPERT_ROWS
    block_expert = jnp.minimum(jnp.searchsorted(pad_end, block_start, side="right"), N_EXPERTS - 1).astype(jnp.int32)
    n_used = (pad_end[-1:] // EXPERT_ROWS).astype(jnp.int32)

    xs = _gather_rows(h, slot_tok, GATHER_ROWS, d // LANES)
    ys = _expert_ffn(xs, block_expert, n_used, wg, wu, wd)
    dest2 = dest.reshape(t, TOP_K_EXPERTS)
    return _moe_combine(ys, dest2[:, 0], dest2[:, 1], gates, x2, mod, seq)


def _final_norm_kernel(x_ref, g_ref, o_ref):
    x = x_ref[...]
    o_ref[...] = x * lax.rsqrt(jnp.mean(x * x, axis=-1, keepdims=True) + NORM_EPS) * g_ref[...]


def _final_norm(x2, g_final):
    t, d = x2.shape
    tm = min(ROW_TILE, t)
    return pl.pallas_call(
        _final_norm_kernel,
        grid=(t // tm,),
        in_specs=[pl.BlockSpec((tm, d), lambda i: (i, 0)), pl.BlockSpec((1, d), lambda i: (0, 0))],
        out_specs=pl.BlockSpec((tm, d), lambda i: (i, 0)),
        out_shape=jax.ShapeDtypeStruct((t, d), F32),
        compiler_params=_params("arbitrary"),
        name="final_norm",
    )(x2, g_final.reshape(1, d))


def _token_mixer(x2, mod, g_mix, w_in, g_heads, w_out, batch, seq):
    pa, pb, pc, pd, p_idx, p_w, *pc_views = _in_projection(x2, mod, g_mix, _pack_w_in(w_in), seq)
    shape3 = lambda a: a.reshape(batch, -1, a.shape[-1])
    gh = g_heads.reshape(4, HEADS_PER_MIXER, HEAD_DIM)
    mixed = (_stick_breaking_mixer(shape3(pa), gh[0]),
             _dsa_mixer(shape3(pb), shape3(p_idx), shape3(p_w), gh[1]),
             _dilated_mixer([shape3(v) for v in [pc] + pc_views], gh[2]),
             _moba_mixer(shape3(pd), gh[3]))
    return _out_projection(mixed, w_out.astype(BF16), x2, mod, seq)


def kernel(x, c, w_ada, b_ada, g_mix, w_in, g_heads, w_out, g_ffn, w_ff_gate, w_ff_up, w_ff_down, w_router, w_exp_gate, w_exp_up, w_exp_down, g_final):
    batch, seq, d = x.shape
    depth = w_ada.shape[0]
    mods = _ada_modulation(c, w_ada, b_ada)
    x2 = x.reshape(batch * seq, d)
    for layer in range(depth):
        mod = mods[layer]
        x2 = _token_mixer(x2, mod, g_mix[layer], w_in[layer], g_heads[layer], w_out[layer], batch, seq)
        i = layer // 2
        if layer % 2 == 0:
            x2 = _dense_ffn(x2, mod, g_ffn[layer], w_ff_gate[i], w_ff_up[i], w_ff_down[i], seq)
        else:
            x2 = _moe_ffn(x2, mod, g_ffn[layer], w_router[i], w_exp_gate[i], w_exp_up[i], w_exp_down[i], seq)
    return _final_norm(x2, g_final).reshape(batch, seq, d)
```

```python
import functools

import numpy as np
import jax
import jax.numpy as jnp
from jax import lax
from jax.experimental import pallas as pl
from jax.experimental.pallas import tpu as pltpu

F32 = jnp.float32
BF16 = jnp.bfloat16

HEAD_DIM = 64
HEADS_PER_MIXER = 4
MIXER_WIDTH = HEADS_PER_MIXER * HEAD_DIM
QKV_WIDTH = 3 * MIXER_WIDTH
IDX_HEADS = 8
IDX_DIM = 64
DSA_TOPK = 256
DILATIONS = (1, 4, 16)
DILATED_STEPS = 128
MOBA_BLOCK = 256
MOBA_TOPK = 3
N_EXPERTS = 8
TOP_K_EXPERTS = 2
NORM_EPS = 1e-6
ATTN_SCALE = HEAD_DIM ** -0.5

LANES = 128
Q_TILE = 128
ROW_TILE = 512
FFN_TILE = 1408
EXPERT_ROWS = 512
EXPERT_FF_TILE = 1792
GATHER_ROWS = 512
VMEM_LIMIT = 56 * 1024 * 1024
MASKED = -1e30
SB_UNDERFLOW = 104.0
SB_FIRST_BLOCKS = 3

_NT = (((1,), (1,)), ((), ()))
_TN = (((0,), (0,)), ((), ()))


def _alibi_slopes(mixer_pos):
    idx = np.arange(HEADS_PER_MIXER, dtype=np.float32) * 3 + (mixer_pos + 1)
    return tuple(float(s) for s in np.exp2(-8.0 * idx / 12.0).astype(np.float32))


def _params(*semantics):
    return pltpu.CompilerParams(dimension_semantics=semantics, vmem_limit_bytes=VMEM_LIMIT)


def _modulated_norm(x, gain, shift, scale):
    y = x * lax.rsqrt(jnp.mean(x * x, axis=-1, keepdims=True) + NORM_EPS) * gain
    return y * (1.0 + scale) + shift


def _head_norm(acc, gain):
    return acc * lax.rsqrt(jnp.mean(acc * acc, axis=-1, keepdims=True) + NORM_EPS) * gain


def _head_cols(h):
    return slice(h * HEAD_DIM, (h + 1) * HEAD_DIM)


def _paired_loop(n, body, carry):
    carry = lax.fori_loop(0, n // 2, lambda jj, c: body(2 * jj + 1, body(2 * jj, c, 0), 1), carry)
    return lax.cond(n % 2 == 1, lambda c: body(n - 1, c, 0), lambda c: c, carry)


def _ada_kernel(c_ref, w_ref, b_ref, o_ref):
    c = c_ref[...]
    cond = c / (1.0 + jnp.exp(-c))
    o_ref[0, 0] = jnp.dot(cond, w_ref[0], preferred_element_type=F32,
                          precision=lax.Precision.HIGHEST) + b_ref[0, 0]


def _ada_modulation(c, w_ada, b_ada):
    depth, d, _ = w_ada.shape
    b = c.shape[0]
    out = pl.pallas_call(
        _ada_kernel,
        grid=(depth, 6),
        in_specs=[pl.BlockSpec((b, d), lambda l, k: (0, 0)),
                  pl.BlockSpec((1, d, d), lambda l, k: (l, 0, k)),
                  pl.BlockSpec((1, 1, 1, d), lambda l, k: (l, k, 0, 0))],
        out_specs=pl.BlockSpec((1, 1, b, d), lambda l, k: (l, k, 0, 0)),
        out_shape=jax.ShapeDtypeStruct((depth, 6, b, d), F32),
        compiler_params=_params("arbitrary", "arbitrary"),
        name="ada_modulation",
    )(c, w_ada, b_ada.reshape(depth, 6, 1, d))
    return out.transpose(0, 2, 1, 3)


IN_WIDTHS = (QKV_WIDTH, QKV_WIDTH, QKV_WIDTH, QKV_WIDTH, IDX_HEADS * IDX_DIM + LANES, LANES)


def _in_proj_kernel(x_ref, mod_ref, g_ref, w_ref, oa, ob, oc, od, oidx, ow, oc4, oc16, pc_ref):
    h = _modulated_norm(x_ref[...], g_ref[...], mod_ref[0, 0:1, :], mod_ref[0, 1:2, :]).astype(BF16)
    tm = x_ref.shape[0]
    off = 0
    for o_ref, width in zip((oa, ob, oc, od, oidx, ow), IN_WIDTHS):
        val = jnp.dot(h, w_ref[:, off:off + width], preferred_element_type=F32)
        o_ref[...] = val.astype(o_ref.dtype)
        if o_ref is oc:
            for s in range(QKV_WIDTH // LANES):
                pc_ref[s * tm:(s + 1) * tm, :] = val[:, s * LANES:(s + 1) * LANES]
        off += width
    for r, view in zip(DILATIONS[1:], (oc4, oc16)):
        for c in range(r):
            for s in range(QKV_WIDTH // LANES):
                col = c * QKV_WIDTH + s * LANES
                view[:, col:col + LANES] = pc_ref[pl.ds(s * tm + c, tm // r, stride=r), :].astype(view.dtype)


def _pack_w_in(w_in):
    d = w_in.shape[0]
    n_qkv = 4 * QKV_WIDTH
    n_qi = IDX_HEADS * IDX_DIM
    z = lambda n: jnp.zeros((d, n), w_in.dtype)
    return jnp.concatenate([w_in[:, :n_qkv + n_qi + IDX_DIM], z(LANES - IDX_DIM),
                            w_in[:, n_qkv + n_qi + IDX_DIM:], z(LANES - IDX_HEADS)], axis=1).astype(BF16)


def _in_projection(x2, mod, g_mix, w_packed, seq):
    t, d = x2.shape
    tm = min(ROW_TILE, seq)
    per_batch = seq // tm
    dtypes = (BF16, BF16, BF16, BF16, BF16, F32)
    views = DILATIONS[1:]
    return pl.pallas_call(
        _in_proj_kernel,
        grid=(t // tm,),
        in_specs=[pl.BlockSpec((tm, d), lambda i: (i, 0)),
                  pl.BlockSpec((1, 6, d), lambda i: (i // per_batch, 0, 0)),
                  pl.BlockSpec((1, d), lambda i: (0, 0)),
                  pl.BlockSpec(w_packed.shape, lambda i: (0, 0))],
        out_specs=([pl.BlockSpec((tm, w), lambda i: (i, 0)) for w in IN_WIDTHS]
                   + [pl.BlockSpec((tm // r, r * QKV_WIDTH), lambda i: (i, 0)) for r in views]),
        out_shape=([jax.ShapeDtypeStruct((t, w), dt) for w, dt in zip(IN_WIDTHS, dtypes)]
                   + [jax.ShapeDtypeStruct((t // r, r * QKV_WIDTH), BF16) for r in views]),
        scratch_shapes=[pltpu.VMEM((tm * (QKV_WIDTH // LANES), LANES), F32)],
        compiler_params=_params("arbitrary"),
        name="in_projection",
    )(x2, mod, g_mix.reshape(1, d), w_packed)


def _sb_kernel(q_ref, k_ref, v_ref, g_ref, o_ref):
    tq = q_ref.shape[1]
    i = pl.program_id(1)
    row = lax.broadcasted_iota(jnp.int32, (tq, 1), 0)
    lane = lax.broadcasted_iota(jnp.int32, (1, tq), 1)
    later = (lax.broadcasted_iota(jnp.int32, (tq, tq), 0) > lax.broadcasted_iota(jnp.int32, (tq, tq), 1))
    later = jnp.where(later, 1.0, 0.0).astype(BF16)

    qs = [q_ref[0, :, _head_cols(h)] * ATTN_SCALE for h in range(HEADS_PER_MIXER)]

    heads = range(HEADS_PER_MIXER)

    def blocks(starts, keeps, tails):
        z = jnp.concatenate([lax.dot_general(qs[h], k_ref[0, pl.ds(st, tq), _head_cols(h)], _NT,
                                             preferred_element_type=F32) for st in starts for h in heads], axis=0)
        softplus = jnp.maximum(z, 0.0) + jnp.log(1.0 + jnp.exp(-jnp.abs(z)))
        keep = None
        if any(kp is not None for kp in keeps):
            ones = jnp.ones((tq, tq), F32)
            keep = jnp.concatenate([ones if kp is None else kp for kp in keeps for _ in heads], axis=0)
        log_1m = -softplus if keep is None else -softplus * keep
        hi = log_1m.astype(BF16)
        lo = (log_1m - hi.astype(F32)).astype(BF16)
        inside = jnp.dot(hi, later, preferred_element_type=F32) + jnp.dot(lo, later, preferred_element_type=F32)
        block_sum = jnp.sum(log_1m, axis=1, keepdims=True)
        piece = lambda x, b, h: x[(b * HEADS_PER_MIXER + h) * tq:(b * HEADS_PER_MIXER + h + 1) * tq]
        tail_cols = []
        tails = list(tails)
        for b in range(len(starts)):
            tail_cols += tails
            tails = [tails[h] + piece(block_sum, b, h) for h in heads]
        a = jnp.exp(z - softplus + inside + jnp.concatenate(tail_cols, axis=0))
        if keep is not None:
            a = a * keep
        a = a.astype(BF16)
        av = [sum(jnp.dot(piece(a, b, h), v_ref[0, pl.ds(st, tq), _head_cols(h)], preferred_element_type=F32)
                  for b, st in enumerate(starts)) for h in heads]
        return tails, av

    starts = [pl.multiple_of(jnp.maximum(i - n, 0) * tq, tq) for n in range(SB_FIRST_BLOCKS)]
    keeps = [jnp.where(lane < row, 1.0, 0.0)]
    keeps += [jnp.full((tq, tq), jnp.where(i >= n, 1.0, 0.0), F32) for n in range(1, SB_FIRST_BLOCKS)]
    tails, accs = blocks(starts, keeps, [jnp.zeros((tq, 1), F32)] * HEADS_PER_MIXER)
    state = tuple(zip(tails, accs))

    def body(carry):
        j, state = carry
        tails, av = blocks([pl.multiple_of(j * tq, tq)], [None], [tail for tail, _ in state])
        return j - 1, tuple((tails[h], state[h][1] + av[h]) for h in heads)

    def cond(carry):
        j, state = carry
        worst = functools.reduce(jnp.maximum, [tail for tail, _ in state])
        return jnp.logical_and(j >= 0, jnp.max(worst) > -SB_UNDERFLOW)

    _, state = lax.while_loop(cond, body, (i - SB_FIRST_BLOCKS, state))
    for h in range(HEADS_PER_MIXER):
        o_ref[0, :, _head_cols(h)] = _head_norm(state[h][1], g_ref[h:h + 1, :]).astype(o_ref.dtype)


def _stick_breaking_mixer(p, g_heads):
    b, seq, _ = p.shape
    tq = min(Q_TILE, seq)
    return pl.pallas_call(
        _sb_kernel,
        grid=(b, seq // tq),
        in_specs=[pl.BlockSpec((1, tq, MIXER_WIDTH), lambda bi, i: (bi, i, 0)),
                  pl.BlockSpec((1, seq, MIXER_WIDTH), lambda bi, i: (bi, 0, 1)),
                  pl.BlockSpec((1, seq, MIXER_WIDTH), lambda bi, i: (bi, 0, 2)),
                  pl.BlockSpec((HEADS_PER_MIXER, HEAD_DIM), lambda bi, i: (0, 0))],
        out_specs=pl.BlockSpec((1, tq, MIXER_WIDTH), lambda bi, i: (bi, i, 0)),
        out_shape=jax.ShapeDtypeStruct((b, seq, MIXER_WIDTH), BF16),
        compiler_params=_params("arbitrary", "arbitrary"),
        name="stick_breaking",
    )(p, p, p, g_heads)


def _dsa_kernel(q_ref, k_ref, v_ref, qi_ref, ki_ref, w_ref, g_ref, o_ref, sc_ref, s_ref, p_ref, *, slopes):
    tq = q_ref.shape[1]
    tk = sc_ref.shape[1]
    i = pl.program_id(1)
    n_blocks = ((i + 1) * tq + tk - 1) // tk
    qpos = i * tq + lax.broadcasted_iota(jnp.int32, (1, tq), 1)
    key_row = lax.broadcasted_iota(jnp.int32, (tk, 1), 0)
    neg_inf = float("-inf")

    w_t = w_ref[0].T * (IDX_HEADS ** -0.5 * IDX_DIM ** -0.5)
    w_rows = [w_t[h:h + 1, :] for h in range(IDX_HEADS)]
    qi = qi_ref[0]

    def score_body(j, carry, slot):
        start = pl.multiple_of(j * tk, tk)
        ki = ki_ref[0, pl.ds(start, tk), 0:IDX_DIM]
        sc = jnp.zeros((tk, tq), F32)
        for h in range(IDX_HEADS):
            x = lax.dot_general(ki, qi[:, h * IDX_DIM:(h + 1) * IDX_DIM], _NT, preferred_element_type=F32)
            sc = sc + w_rows[h] * jnp.maximum(x, 0.0)
        sc_ref[j] = jnp.where((start + key_row) <= qpos, sc + 0.0, neg_inf)
        return carry

    _paired_loop(n_blocks, score_body, 0)

    def count(pred):
        def add_block(j, cnt, slot):
            c = jnp.where(pred(sc_ref[j]), 1.0, 0.0)
            return cnt + jnp.sum(c.reshape(tk // 8, 8, tq), axis=0)

        cnt = _paired_loop(n_blocks, add_block, jnp.zeros((8, tq), F32))
        return jnp.sum(cnt, axis=0, keepdims=True)

    int_min = jnp.int32(-2 ** 31)

    def ordered_to_float(u):
        key = u ^ int_min
        bits = jnp.where(key >= 0, key, key ^ jnp.int32(0x7FFFFFFF))
        return lax.bitcast_convert_type(bits, F32)

    def search_body(step, carry):
        prefix, n_at_prefix = carry
        cand = prefix | jnp.left_shift(jnp.int32(1), 31 - step)
        cand_f = ordered_to_float(cand)
        n_ge = count(lambda s: s >= cand_f)
        keep = n_ge >= DSA_TOPK
        return jnp.where(keep, cand, prefix), jnp.where(keep, n_ge, n_at_prefix)

    prefix, n_ge_tau = lax.fori_loop(0, 32, search_body,
                                     (jnp.zeros((1, tq), jnp.int32), jnp.zeros((1, tq), F32)))
    keep_all = (qpos + 1) <= DSA_TOPK
    tau = jnp.where(keep_all, float(np.finfo(np.float32).min), ordered_to_float(prefix))
    no_ties = jnp.min(jnp.where(jnp.logical_or(keep_all, n_ge_tau == DSA_TOPK), 1.0, 0.0)) > 0.5

    earlier = (lax.broadcasted_iota(jnp.int32, (tk, tk), 1) < lax.broadcasted_iota(jnp.int32, (tk, tk), 0))
    earlier = jnp.where(earlier, 1.0, 0.0).astype(BF16)
    qs = [q_ref[0, :, _head_cols(h)] * ATTN_SCALE for h in range(HEADS_PER_MIXER)]
    key_bias = [slopes[h] * lax.broadcasted_iota(jnp.int32, (tk, tq), 0).astype(F32)
                for h in range(HEADS_PER_MIXER)]

    def attn_body(j, carry, slot, n_ties=None):
        ties_seen, stats = carry
        s_slot = [s_ref.at[slot * HEADS_PER_MIXER + h] for h in range(HEADS_PER_MIXER)]
        p_slot = [p_ref.at[slot * HEADS_PER_MIXER + h] for h in range(HEADS_PER_MIXER)]
        start = pl.multiple_of(j * tk, tk)
        sc = sc_ref[j]
        if n_ties is None:
            sel = sc >= tau
        else:
            tie = jnp.where(sc == tau, 1.0, 0.0)
            rank = jnp.dot(earlier, tie.astype(BF16), preferred_element_type=F32) + ties_seen
            sel = jnp.where(sc > tau, 1.0, jnp.where(rank < n_ties, tie, 0.0)) > 0.5
            ties_seen = ties_seen + jnp.sum(tie, axis=0, keepdims=True)
        block_pos = (j * tk - i * tq).astype(F32)
        for h in range(HEADS_PER_MIXER):
            k = k_ref[0, pl.ds(start, tk), _head_cols(h)]
            s = lax.dot_general(k, qs[h], _NT, preferred_element_type=F32) + key_bias[h]
            s_slot[h][...] = jnp.where(sel, s, MASKED)
        locals_ = []
        for h in range(HEADS_PER_MIXER):
            s = s_slot[h][...]
            m_loc = jnp.max(s, axis=0, keepdims=True)
            p = jnp.exp(s - m_loc)
            locals_.append((m_loc, jnp.sum(p, axis=0, keepdims=True)))
            p_slot[h][...] = p.astype(BF16)
        new_stats = []
        for h in range(HEADS_PER_MIXER):
            m, l, acc = stats[h]
            m_loc, l_blk = locals_[h]
            v = v_ref[0, pl.ds(start, tk), _head_cols(h)]
            pv_blk = lax.dot_general(v, p_slot[h][...], _TN, preferred_element_type=F32)
            m_blk = m_loc + slopes[h] * block_pos
            m_new = jnp.maximum(m, m_blk)
            w_old = jnp.exp(m - m_new)
            w_blk = jnp.where(m_loc > 0.5 * MASKED, jnp.exp(m_blk - m_new), 0.0)
            new_stats.append((m_new, w_old * l + w_blk * l_blk, w_old * acc + w_blk * pv_blk))
        return ties_seen, tuple(new_stats)

    init = tuple((jnp.full((1, tq), MASKED, F32), jnp.zeros((1, tq), F32), jnp.zeros((HEAD_DIM, tq), F32))
                 for _ in range(HEADS_PER_MIXER))
    start_carry = (jnp.zeros((1, tq), F32), init)

    def attend_without_ties():
        return _paired_loop(n_blocks, attn_body, start_carry)[1]

    def attend_with_ties():
        n_gt = count(lambda s: s > tau)
        n_ties = jnp.where(keep_all, 1e9, DSA_TOPK - n_gt)
        return _paired_loop(n_blocks, functools.partial(attn_body, n_ties=n_ties), start_carry)[1]

    stats = lax.cond(no_ties, attend_without_ties, attend_with_ties)
    outs = []
    for h in range(HEADS_PER_MIXER):
        _, l, acc = stats[h]
        o = acc / l
        outs.append(o * lax.rsqrt(jnp.mean(o * o, axis=0, keepdims=True) + NORM_EPS) * g_ref[:, h:h + 1])
    o_ref[0] = jnp.concatenate(outs, axis=0).T.astype(o_ref.dtype)


def _dsa_mixer(p, p_idx, p_w, g_heads):
    b, seq, _ = p.shape
    tq = min(2 * Q_TILE, seq)
    tk = tq
    qi_width = IDX_HEADS * IDX_DIM
    return pl.pallas_call(
        functools.partial(_dsa_kernel, slopes=_alibi_slopes(0)),
        grid=(b, seq // tq),
        in_specs=[pl.BlockSpec((1, tq, MIXER_WIDTH), lambda bi, i: (bi, i, 0)),
                  pl.BlockSpec((1, seq, MIXER_WIDTH), lambda bi, i: (bi, 0, 1)),
                  pl.BlockSpec((1, seq, MIXER_WIDTH), lambda bi, i: (bi, 0, 2)),
                  pl.BlockSpec((1, tq, qi_width), lambda bi, i: (bi, i, 0)),
                  pl.BlockSpec((1, seq, LANES), lambda bi, i: (bi, 0, qi_width // LANES)),
                  pl.BlockSpec((1, tq, LANES), lambda bi, i: (bi, i, 0)),
                  pl.BlockSpec((HEAD_DIM, HEADS_PER_MIXER), lambda bi, i: (0, 0))],
        out_specs=pl.BlockSpec((1, tq, MIXER_WIDTH), lambda bi, i: (bi, i, 0)),
        out_shape=jax.ShapeDtypeStruct((b, seq, MIXER_WIDTH), BF16),
        scratch_shapes=[pltpu.VMEM((seq // tk, tk, tq), F32),
                        pltpu.VMEM((2 * HEADS_PER_MIXER, tk, tq), F32),
                        pltpu.VMEM((2 * HEADS_PER_MIXER, tk, tq), BF16)],
        compiler_params=_params("arbitrary", "arbitrary"),
        name="dsa",
    )(p, p, p, p_idx, p_idx, p_w, g_heads.T)


def _band_kernel(q_ref, kp_ref, kc_ref, vp_ref, vc_ref, o_ref, lse_ref, *, dilation, slopes):
    tq = q_ref.shape[1]
    ui = pl.program_id(2)
    u_q = ui * tq + lax.broadcasted_iota(jnp.int32, (tq, 1), 0)
    u_k = ui * tq - DILATED_STEPS + lax.broadcasted_iota(jnp.int32, (1, DILATED_STEPS + tq), 1)
    steps = u_q - u_k
    valid = jnp.logical_and(jnp.logical_and(steps >= 0, steps <= DILATED_STEPS), u_k >= 0)
    dist = (steps * dilation).astype(F32)
    lane = lax.broadcasted_iota(jnp.int32, (1, LANES), 1)
    lse_all = jnp.zeros((tq, LANES), F32)
    for h in range(HEADS_PER_MIXER):
        cols = _head_cols(h)
        k = jnp.concatenate([kp_ref[0, :, cols], kc_ref[0, :, cols]], axis=0)
        v = jnp.concatenate([vp_ref[0, :, cols], vc_ref[0, :, cols]], axis=0)
        s = lax.dot_general(q_ref[0, :, cols], k, _NT, preferred_element_type=F32) * ATTN_SCALE - slopes[h] * dist
        s = jnp.where(valid, s, MASKED)
        m = jnp.max(s, axis=1, keepdims=True)
        e = jnp.where(valid, jnp.exp(s - m), 0.0)
        den = jnp.sum(e, axis=1, keepdims=True)
        o_ref[0, :, cols] = jnp.dot(e.astype(BF16), v, preferred_element_type=F32) / den
        lse_all = lse_all + jnp.where(lane == h, m + jnp.log(den), 0.0)
    lse_ref[0] = lse_all


def _dilated_branch(view, dilation, slopes):
    b, length, _ = view.shape
    classes = dilation
    tq = min(2 * Q_TILE, length)
    back = tq // DILATED_STEPS
    spec = lambda part, prev: (
        pl.BlockSpec((1, DILATED_STEPS, MIXER_WIDTH),
                     lambda bi, c, ui: (bi, jnp.maximum(ui * back - 1, 0), c * 3 + part)) if prev
        else pl.BlockSpec((1, tq, MIXER_WIDTH), lambda bi, c, ui: (bi, ui, c * 3 + part)))
    out, lse = pl.pallas_call(
        functools.partial(_band_kernel, dilation=dilation, slopes=slopes),
        grid=(b, classes, length // tq),
        in_specs=[spec(0, False), spec(1, True), spec(1, False), spec(2, True), spec(2, False)],
        out_specs=[pl.BlockSpec((1, tq, MIXER_WIDTH), lambda bi, c, ui: (bi, ui, c)),
                   pl.BlockSpec((1, tq, LANES), lambda bi, c, ui: (bi, ui, c))],
        out_shape=[jax.ShapeDtypeStruct((b, length, classes * MIXER_WIDTH), F32),
                   jax.ShapeDtypeStruct((b, length, classes * LANES), F32)],
        compiler_params=_params("arbitrary", "arbitrary", "arbitrary"),
        name=f"dilated_r{dilation}",
    )(view, view, view, view, view)
    return out, lse


def _dilated_merge_kernel(o1, o2, o3, l1, l2, l3, g_ref, o_ref, *scratch):
    tq = o_ref.shape[1]
    chunks = MIXER_WIDTH // LANES
    outs = [[o1[0, :, s * LANES:(s + 1) * LANES] for s in range(chunks)]]
    lses = [l1[0]]
    for r, o_view, l_view, o_nat, l_nat in zip(DILATIONS[1:], (o2, o3), (l2, l3), scratch[0::2], scratch[1::2]):
        for c in range(r):
            for s in range(chunks):
                col = c * MIXER_WIDTH + s * LANES
                o_nat[pl.ds(s * tq + c, tq // r, stride=r), :] = o_view[0, :, col:col + LANES]
            l_nat[pl.ds(c, tq // r, stride=r), :] = l_view[0, :, c * LANES:(c + 1) * LANES]
        outs.append([o_nat[s * tq:(s + 1) * tq, :] for s in range(chunks)])
        lses.append(l_nat[...])
    heads_per_chunk = LANES // HEAD_DIM
    for h in range(HEADS_PER_MIXER):
        sub = slice((h % heads_per_chunk) * HEAD_DIM, (h % heads_per_chunk + 1) * HEAD_DIM)
        lse_h = [l[:, h:h + 1] for l in lses]
        top = functools.reduce(jnp.maximum, lse_h)
        wts = [jnp.exp(l - top) for l in lse_h]
        mixed = sum(w * o[h // heads_per_chunk][:, sub] for w, o in zip(wts, outs)) / sum(wts)
        o_ref[0, :, _head_cols(h)] = _head_norm(mixed, g_ref[h:h + 1, :]).astype(o_ref.dtype)


def _dilated_mixer(views, g_heads):
    b, seq, _ = views[0].shape
    slopes = _alibi_slopes(1)
    branches = [_dilated_branch(v, r, slopes) for v, r in zip(views, DILATIONS)]
    tq = min(ROW_TILE, seq)
    o_specs = [pl.BlockSpec((1, tq // r, r * MIXER_WIDTH), lambda bi, i: (bi, i, 0)) for r in DILATIONS]
    l_specs = [pl.BlockSpec((1, tq // r, r * LANES), lambda bi, i: (bi, i, 0)) for r in DILATIONS]
    scratch = []
    for _ in DILATIONS[1:]:
        scratch += [pltpu.VMEM((tq * (MIXER_WIDTH // LANES), LANES), F32), pltpu.VMEM((tq, LANES), F32)]
    return pl.pallas_call(
        _dilated_merge_kernel,
        grid=(b, seq // tq),
        in_specs=o_specs + l_specs + [pl.BlockSpec((HEADS_PER_MIXER, HEAD_DIM), lambda bi, i: (0, 0))],
        out_specs=pl.BlockSpec((1, tq, MIXER_WIDTH), lambda bi, i: (bi, i, 0)),
        out_shape=jax.ShapeDtypeStruct((b, seq, MIXER_WIDTH), BF16),
        scratch_shapes=scratch,
        compiler_params=_params("arbitrary", "arbitrary"),
        name="dilated_merge",
    )(*[o for o, _ in branches], *[l for _, l in branches], g_heads)


def _moba_kernel(q_ref, k_ref, v_ref, g_ref, o_ref, kmean_ref, chosen_ref, s_ref, p_ref, *, slopes):
    tq = q_ref.shape[1]
    n_kv = k_ref.shape[1] // MOBA_BLOCK
    own = pl.program_id(1)

    @pl.when(own == 0)
    def _():
        kmean_ref[...] = jnp.zeros_like(kmean_ref)
        for n in range(n_kv):
            blk = k_ref[0, n * MOBA_BLOCK:(n + 1) * MOBA_BLOCK, :].astype(F32)
            kmean_ref[n:n + 1, :] = jnp.sum(blk, axis=0, keepdims=True) * (1.0 / MOBA_BLOCK)

    key_row = lax.broadcasted_iota(jnp.int32, (MOBA_BLOCK, 1), 0)
    q_lane = lax.broadcasted_iota(jnp.int32, (1, tq), 1)
    blk_row = lax.broadcasted_iota(jnp.int32, (LANES, 1), 0)
    blk_f = blk_row.astype(F32)
    neg_inf = float("-inf")

    qs = []
    for h in range(HEADS_PER_MIXER):
        q = q_ref[0, :, _head_cols(h)]
        gate = lax.dot_general(kmean_ref[:, _head_cols(h)], q.astype(F32), _NT, preferred_element_type=F32,
                               precision=lax.Precision.HIGHEST)
        gate = jnp.where(blk_row < own, gate, neg_inf)
        picks = jnp.zeros((LANES, tq), F32)
        for _ in range(MOBA_TOPK):
            top = jnp.max(gate, axis=0, keepdims=True)
            is_top = jnp.logical_and(gate == top, top > neg_inf)
            first = jnp.min(jnp.where(is_top, blk_f, float(LANES)), axis=0, keepdims=True)
            pick = blk_f == first
            picks = jnp.where(pick, 1.0, picks)
            gate = jnp.where(pick, neg_inf, gate)
        chosen_ref[h] = picks
        qs.append(q * ATTN_SCALE)

    key_bias = [slopes[h] * lax.broadcasted_iota(jnp.int32, (MOBA_BLOCK, tq), 0).astype(F32)
                for h in range(HEADS_PER_MIXER)]

    def block_softmax(start, keep=None):
        for h in range(HEADS_PER_MIXER):
            k = k_ref[0, pl.ds(start, MOBA_BLOCK), _head_cols(h)]
            s = lax.dot_general(k, qs[h], _NT, preferred_element_type=F32) + key_bias[h]
            s_ref[h] = s if keep is None else jnp.where(keep, s, MASKED)
        pieces = []
        for h in range(HEADS_PER_MIXER):
            s = s_ref[h]
            m_loc = jnp.max(s, axis=0, keepdims=True)
            p = jnp.exp(s - m_loc)
            pieces.append((m_loc, jnp.sum(p, axis=0, keepdims=True)))
            p_ref[h] = p.astype(BF16)
        out = []
        for h in range(HEADS_PER_MIXER):
            v = v_ref[0, pl.ds(start, MOBA_BLOCK), _head_cols(h)]
            out.append(pieces[h] + (lax.dot_general(v, p_ref[h], _TN, preferred_element_type=F32),))
        return out

    own_start = pl.multiple_of(own * MOBA_BLOCK, MOBA_BLOCK)
    stats = tuple(block_softmax(own_start, keep=key_row <= q_lane))

    def body(j, stats):
        start = pl.multiple_of(j * MOBA_BLOCK, MOBA_BLOCK)
        block_pos = ((j - own) * MOBA_BLOCK).astype(F32)
        merged = []
        for h, (m_loc, l_blk, pv_blk) in enumerate(block_softmax(start)):
            picked = chosen_ref[h, pl.ds(j, 1), :] > 0.5
            m_blk = jnp.where(picked, m_loc + slopes[h] * block_pos, MASKED)
            m, l, acc = stats[h]
            m_new = jnp.maximum(m, m_blk)
            w_old = jnp.exp(m - m_new)
            w_blk = jnp.exp(m_blk - m_new)
            merged.append((m_new, w_old * l + w_blk * l_blk, w_old * acc + w_blk * pv_blk))
        return tuple(merged)

    stats = lax.fori_loop(0, own, body, stats)
    outs = []
    for h in range(HEADS_PER_MIXER):
        _, l, acc = stats[h]
        o = acc / l
        outs.append(o * lax.rsqrt(jnp.mean(o * o, axis=0, keepdims=True) + NORM_EPS) * g_ref[:, h:h + 1])
    o_ref[0] = jnp.concatenate(outs, axis=0).T.astype(o_ref.dtype)


def _moba_mixer(p, g_heads):
    b, seq, _ = p.shape
    tq = MOBA_BLOCK
    return pl.pallas_call(
        functools.partial(_moba_kernel, slopes=_alibi_slopes(2)),
        grid=(b, seq // tq),
        in_specs=[pl.BlockSpec((1, tq, MIXER_WIDTH), lambda bi, i: (bi, i, 0)),
                  pl.BlockSpec((1, seq, MIXER_WIDTH), lambda bi, i: (bi, 0, 1)),
                  pl.BlockSpec((1, seq, MIXER_WIDTH), lambda bi, i: (bi, 0, 2)),
                  pl.BlockSpec((HEAD_DIM, HEADS_PER_MIXER), lambda bi, i: (0, 0))],
        out_specs=pl.BlockSpec((1, tq, MIXER_WIDTH), lambda bi, i: (bi, i, 0)),
        out_shape=jax.ShapeDtypeStruct((b, seq, MIXER_WIDTH), BF16),
        scratch_shapes=[pltpu.VMEM((LANES, MIXER_WIDTH), F32),
                        pltpu.VMEM((HEADS_PER_MIXER, LANES, tq), F32),
                        pltpu.VMEM((HEADS_PER_MIXER, MOBA_BLOCK, tq), F32),
                        pltpu.VMEM((HEADS_PER_MIXER, MOBA_BLOCK, tq), BF16)],
        compiler_params=_params("arbitrary", "arbitrary"),
        name="moba",
    )(p, p, p, g_heads.T)


def _out_proj_kernel(oa, ob, oc, od, w_ref, x_ref, mod_ref, o_ref):
    acc = jnp.zeros(x_ref.shape, F32)
    for m, o in enumerate((oa, ob, oc, od)):
        acc = acc + jnp.dot(o[...], w_ref[m * MIXER_WIDTH:(m + 1) * MIXER_WIDTH, :], preferred_element_type=F32)
    o_ref[...] = x_ref[...] + mod_ref[0, 2:3, :] * acc


def _out_projection(mixed, w_out, x2, mod, seq):
    t, d = x2.shape
    tm = min(ROW_TILE, seq)
    per_batch = seq // tm
    o_spec = pl.BlockSpec((tm, MIXER_WIDTH), lambda i: (i, 0))
    return pl.pallas_call(
        _out_proj_kernel,
        grid=(t // tm,),
        in_specs=[o_spec] * 4 + [pl.BlockSpec(w_out.shape, lambda i: (0, 0)),
                                 pl.BlockSpec((tm, d), lambda i: (i, 0)),
                                 pl.BlockSpec((1, 6, d), lambda i: (i // per_batch, 0, 0))],
        out_specs=pl.BlockSpec((tm, d), lambda i: (i, 0)),
        out_shape=jax.ShapeDtypeStruct((t, d), F32),
        compiler_params=_params("arbitrary"),
        name="out_projection",
    )(*[o.reshape(t, MIXER_WIDTH) for o in mixed], w_out, x2, mod)


def _ffn_kernel(x_ref, mod_ref, g_ref, wg_ref, wu_ref, wd_ref, o_ref, h_ref, acc_ref):
    f = pl.program_id(1)

    @pl.when(f == 0)
    def _():
        h_ref[...] = _modulated_norm(x_ref[...], g_ref[...], mod_ref[0, 3:4, :], mod_ref[0, 4:5, :]).astype(BF16)
        acc_ref[...] = jnp.zeros_like(acc_ref)

    h = h_ref[...]
    gate = jnp.dot(h, wg_ref[...], preferred_element_type=F32)
    up = jnp.dot(h, wu_ref[...], preferred_element_type=F32)
    act = (gate / (1.0 + jnp.exp(-gate)) * up).astype(BF16)
    acc_ref[...] += jnp.dot(act, wd_ref[...], preferred_element_type=F32)

    @pl.when(f == pl.num_programs(1) - 1)
    def _():
        o_ref[...] = x_ref[...] + mod_ref[0, 5:6, :] * acc_ref[...]


def _dense_ffn(x2, mod, g_ffn, wg, wu, wd, seq):
    t, d = x2.shape
    d_ff = wg.shape[1]
    tm = min(ROW_TILE, seq)
    tf = FFN_TILE if d_ff % FFN_TILE == 0 else d_ff
    per_batch = seq // tm
    return pl.pallas_call(
        _ffn_kernel,
        grid=(t // tm, d_ff // tf),
        in_specs=[pl.BlockSpec((tm, d), lambda i, f: (i, 0)),
                  pl.BlockSpec((1, 6, d), lambda i, f: (i // per_batch, 0, 0)),
                  pl.BlockSpec((1, d), lambda i, f: (0, 0)),
                  pl.BlockSpec((d, tf), lambda i, f: (0, f)),
                  pl.BlockSpec((d, tf), lambda i, f: (0, f)),
                  pl.BlockSpec((tf, d), lambda i, f: (f, 0))],
        out_specs=pl.BlockSpec((tm, d), lambda i, f: (i, 0)),
        out_shape=jax.ShapeDtypeStruct((t, d), F32),
        scratch_shapes=[pltpu.VMEM((tm, d), BF16), pltpu.VMEM((tm, d), F32)],
        compiler_params=_params("arbitrary", "arbitrary"),
        name="dense_ffn",
    )(x2, mod, g_ffn.reshape(1, d), wg.astype(BF16), wu.astype(BF16), wd.astype(BF16))


def _store_row_tiles(dst_ref, value):
    rows, d = value.shape
    chunks = d // LANES
    for s in range(chunks):
        dst_ref[pl.ds(s, rows, stride=chunks), :] = value[:, s * LANES:(s + 1) * LANES]


def _load_row_tile_chunk(src_ref, s, rows, chunks):
    return src_ref[pl.ds(s, rows, stride=chunks), :]


def _router_kernel(x_ref, mod_ref, g_ref, wr_ref, h_ref, logit_ref):
    h = _modulated_norm(x_ref[...], g_ref[...], mod_ref[0, 3:4, :], mod_ref[0, 4:5, :])
    _store_row_tiles(h_ref, h)
    logit_ref[...] = jnp.dot(h, wr_ref[...], preferred_element_type=F32, precision=lax.Precision.HIGHEST)


def _router(x2, mod, g_ffn, w_router, seq):
    t, d = x2.shape
    tm = min(ROW_TILE, seq)
    per_batch = seq // tm
    wr = jnp.zeros((d, LANES), F32).at[:, :N_EXPERTS].set(w_router.astype(F32))
    return pl.pallas_call(
        _router_kernel,
        grid=(t // tm,),
        in_specs=[pl.BlockSpec((tm, d), lambda i: (i, 0)),
                  pl.BlockSpec((1, 6, d), lambda i: (i // per_batch, 0, 0)),
                  pl.BlockSpec((1, d), lambda i: (0, 0)),
                  pl.BlockSpec((d, LANES), lambda i: (0, 0))],
        out_specs=[pl.BlockSpec((tm * (d // LANES), LANES), lambda i: (i, 0)),
                   pl.BlockSpec((tm, LANES), lambda i: (i, 0))],
        out_shape=[jax.ShapeDtypeStruct((t * (d // LANES), LANES), F32), jax.ShapeDtypeStruct((t, LANES), F32)],
        compiler_params=_params("arbitrary"),
        name="moe_router",
    )(x2, mod, g_ffn.reshape(1, d), wr)


def _row_copy(src_hbm, row, dst_ref, r, sem, chunks):
    src = src_hbm.at[pl.ds(pl.multiple_of(row * chunks, chunks), chunks), :]
    return pltpu.make_async_copy(src, dst_ref.at[pl.ds(pl.multiple_of(r * chunks, chunks), chunks), :], sem)


def _gather_kernel(idx_ref, src_hbm, o_ref, sem, *, chunks):
    rows = o_ref.shape[0] // chunks

    def start(pair, c):
        for lane in range(2):
            r = 2 * pair + lane
            _row_copy(src_hbm, idx_ref[0, 0, r], o_ref, r, sem.at[lane], chunks).start(priority=lane)
        return c

    def wait(pair, c):
        for lane in range(2):
            r = 2 * pair + lane
            _row_copy(src_hbm, idx_ref[0, 0, r], o_ref, r, sem.at[lane], chunks).wait()
        return c

    lax.fori_loop(0, rows // 2, start, 0)
    lax.fori_loop(0, rows // 2, wait, 0)


def _gather_rows(src, idx, rows_per_step, chunks):
    n = idx.shape[0]
    steps = n // rows_per_step
    return pl.pallas_call(
        functools.partial(_gather_kernel, chunks=chunks),
        grid=(steps,),
        in_specs=[pl.BlockSpec((1, 1, rows_per_step), lambda i: (i, 0, 0), memory_space=pltpu.SMEM),
                  pl.BlockSpec(memory_space=pl.ANY)],
        out_specs=pl.BlockSpec((rows_per_step * chunks, LANES), lambda i: (i, 0)),
        out_shape=jax.ShapeDtypeStruct((n * chunks, LANES), src.dtype),
        scratch_shapes=[pltpu.SemaphoreType.DMA((2,))],
        compiler_params=_params("arbitrary"),
        name="gather_rows",
    )(idx.reshape(steps, 1, rows_per_step), src)


def _expert_kernel(be_ref, used_ref, xs_ref, wg_ref, wu_ref, wd_ref, o_ref, xb_ref, acc_ref):
    m = pl.program_id(0)
    f = pl.program_id(1)
    live = m < used_ref[0]
    rows, d = xb_ref.shape
    chunks = d // LANES

    @pl.when(jnp.logical_and(live, f == 0))
    def _():
        for s in range(chunks):
            xb_ref[:, s * LANES:(s + 1) * LANES] = _load_row_tile_chunk(xs_ref, s, rows, chunks).astype(BF16)
        acc_ref[...] = jnp.zeros_like(acc_ref)

    @pl.when(live)
    def _():
        h = xb_ref[...]
        gate = jnp.dot(h, wg_ref[0], preferred_element_type=F32)
        up = jnp.dot(h, wu_ref[0], preferred_element_type=F32)
        act = (gate / (1.0 + jnp.exp(-gate)) * up).astype(BF16)
        acc_ref[...] += jnp.dot(act, wd_ref[0], preferred_element_type=F32)

    @pl.when(f == pl.num_programs(1) - 1)
    def _():
        _store_row_tiles(o_ref, jnp.where(live, acc_ref[...], 0.0))


def _expert_ffn(xs, block_expert, n_used, wg, wu, wd):
    d = wg.shape[1]
    chunks = d // LANES
    n_slots = xs.shape[0] // chunks
    d_ff = wg.shape[2]
    tm = EXPERT_ROWS
    tf = EXPERT_FF_TILE if d_ff % EXPERT_FF_TILE == 0 else d_ff
    nf = d_ff // tf

    def fcol(m, f, used):
        return jnp.where(m < used[0], f, nf - 1)

    grid_spec = pltpu.PrefetchScalarGridSpec(
        num_scalar_prefetch=2,
        grid=(n_slots // tm, nf),
        in_specs=[pl.BlockSpec((tm * chunks, LANES), lambda m, f, be, used: (m, 0)),
                  pl.BlockSpec((1, d, tf), lambda m, f, be, used: (be[m], 0, fcol(m, f, used))),
                  pl.BlockSpec((1, d, tf), lambda m, f, be, used: (be[m], 0, fcol(m, f, used))),
                  pl.BlockSpec((1, tf, d), lambda m, f, be, used: (be[m], fcol(m, f, used), 0))],
        out_specs=pl.BlockSpec((tm * chunks, LANES), lambda m, f, be, used: (m, 0)),
        scratch_shapes=[pltpu.VMEM((tm, d), BF16), pltpu.VMEM((tm, d), F32)])
    return pl.pallas_call(
        _expert_kernel,
        grid_spec=grid_spec,
        out_shape=jax.ShapeDtypeStruct((n_slots * chunks, LANES), F32),
        compiler_params=_params("arbitrary", "arbitrary"),
        name="expert_ffn",
    )(block_expert, n_used, xs, wg.astype(BF16), wu.astype(BF16), wd.astype(BF16))


def _combine_kernel(d0_ref, d1_ref, ys_hbm, x_ref, gates_ref, mod_ref, o_ref, y0_ref, y1_ref, sem):
    rows, d = o_ref.shape
    chunks = d // LANES

    def start(r, c):
        _row_copy(ys_hbm, d0_ref[0, 0, r], y0_ref, r, sem.at[0], chunks).start(priority=0)
        _row_copy(ys_hbm, d1_ref[0, 0, r], y1_ref, r, sem.at[1], chunks).start(priority=1)
        return c

    def wait(r, c):
        _row_copy(ys_hbm, d0_ref[0, 0, r], y0_ref, r, sem.at[0], chunks).wait()
        _row_copy(ys_hbm, d1_ref[0, 0, r], y1_ref, r, sem.at[1], chunks).wait()
        return c

    lax.fori_loop(0, rows, start, 0)
    lax.fori_loop(0, rows, wait, 0)
    gates = gates_ref[...]
    for s in range(chunks):
        cols = slice(s * LANES, (s + 1) * LANES)
        y = (_load_row_tile_chunk(y0_ref, s, rows, chunks) * gates[:, 0:1]
             + _load_row_tile_chunk(y1_ref, s, rows, chunks) * gates[:, 1:2])
        o_ref[:, cols] = x_ref[:, cols] + mod_ref[0, 5:6, cols] * y


def _moe_combine(ys, dest0, dest1, gates, x2, mod, seq):
    t, d = x2.shape
    tm = min(GATHER_ROWS, seq)
    steps = t // tm
    per_batch = seq // tm
    idx_spec = pl.BlockSpec((1, 1, tm), lambda i: (i, 0, 0), memory_space=pltpu.SMEM)
    return pl.pallas_call(
        _combine_kernel,
        grid=(steps,),
        in_specs=[idx_spec, idx_spec, pl.BlockSpec(memory_space=pl.ANY),
                  pl.BlockSpec((tm, d), lambda i: (i, 0)),
                  pl.BlockSpec((tm, TOP_K_EXPERTS), lambda i: (i, 0)),
                  pl.BlockSpec((1, 6, d), lambda i: (i // per_batch, 0, 0))],
        out_specs=pl.BlockSpec((tm, d), lambda i: (i, 0)),
        out_shape=jax.ShapeDtypeStruct((t, d), F32),
        scratch_shapes=[pltpu.VMEM((tm * (d // LANES), LANES), F32), pltpu.VMEM((tm * (d // LANES), LANES), F32),
                        pltpu.SemaphoreType.DMA((2,))],
        compiler_params=_params("arbitrary"),
        name="moe_combine",
    )(dest0.reshape(steps, 1, tm), dest1.reshape(steps, 1, tm), ys, x2, gates, mod)


def _moe_ffn(x2, mod, g_ffn, w_router, wg, wu, wd, seq):
    t, d = x2.shape
    h, logits = _router(x2, mod, g_ffn, w_router, seq)
    top_val, top_idx = lax.top_k(logits[:, :N_EXPERTS], TOP_K_EXPERTS)
    gates = jax.nn.softmax(top_val, axis=-1)

    n_assign = t * TOP_K_EXPERTS
    flat_e = top_idx.reshape(-1).astype(jnp.int32)
    onehot = (flat_e[:, None] == jnp.arange(N_EXPERTS, dtype=jnp.int32)[None, :]).astype(jnp.int32)
    rank = jnp.take_along_axis(jnp.cumsum(onehot, axis=0), flat_e[:, None], axis=1)[:, 0] - 1
    counts = jnp.sum(onehot, axis=0)
    padded = (counts + EXPERT_ROWS - 1) // EXPERT_ROWS * EXPERT_ROWS
    pad_end = jnp.cumsum(padded)
    dest = (pad_end - padded)[flat_e] + rank
    n_slots = (n_assign // EXPERT_ROWS + N_EXPERTS) * EXPERT_ROWS
    n_blocks = n_slots // EXPERT_ROWS
    slot_tok = jnp.zeros((n_slots,), jnp.int32).at[dest].set(jnp.arange(n_assign, dtype=jnp.int32) // TOP_K_EXPERTS)
    block_start = jnp.arange(n_blocks, dtype=jnp.int32) * EXPERT_ROWS
    block_expert = jnp.minimum(jnp.searchsorted(pad_end, block_start, side="right"), N_EXPERTS - 1).astype(jnp.int32)
    n_used = (pad_end[-1:] // EXPERT_ROWS).astype(jnp.int32)

    xs = _gather_rows(h, slot_tok, GATHER_ROWS, d // LANES)
    ys = _expert_ffn(xs, block_expert, n_used, wg, wu, wd)
    dest2 = dest.reshape(t, TOP_K_EXPERTS)
    return _moe_combine(ys, dest2[:, 0], dest2[:, 1], gates, x2, mod, seq)


def _final_norm_kernel(x_ref, g_ref, o_ref):
    x = x_ref[...]
    o_ref[...] = x * lax.rsqrt(jnp.mean(x * x, axis=-1, keepdims=True) + NORM_EPS) * g_ref[...]


def _final_norm(x2, g_final):
    t, d = x2.shape
    tm = min(ROW_TILE, t)
    return pl.pallas_call(
        _final_norm_kernel,
        grid=(t // tm,),
        in_specs=[pl.BlockSpec((tm, d), lambda i: (i, 0)), pl.BlockSpec((1, d), lambda i: (0, 0))],
        out_specs=pl.BlockSpec((tm, d), lambda i: (i, 0)),
        out_shape=jax.ShapeDtypeStruct((t, d), F32),
        compiler_params=_params("arbitrary"),
        name="final_norm",
    )(x2, g_final.reshape(1, d))


def _token_mixer(x2, mod, g_mix, w_in, g_heads, w_out, batch, seq):
    pa, pb, pc, pd, p_idx, p_w, *pc_views = _in_projection(x2, mod, g_mix, _pack_w_in(w_in), seq)
    shape3 = lambda a: a.reshape(batch, -1, a.shape[-1])
    gh = g_heads.reshape(4, HEADS_PER_MIXER, HEAD_DIM)
    mixed = (_stick_breaking_mixer(shape3(pa), gh[0]),
             _dsa_mixer(shape3(pb), shape3(p_idx), shape3(p_w), gh[1]),
             _dilated_mixer([shape3(v) for v in [pc] + pc_views], gh[2]),
             _moba_mixer(shape3(pd), gh[3]))
    return _out_projection(mixed, w_out.astype(BF16), x2, mod, seq)


def kernel(x, c, w_ada, b_ada, g_mix, w_in, g_heads, w_out, g_ffn, w_ff_gate, w_ff_up, w_ff_down, w_router, w_exp_gate, w_exp_up, w_exp_down, g_final):
    batch, seq, d = x.shape
    depth = w_ada.shape[0]
    mods = _ada_modulation(c, w_ada, b_ada)
    x2 = x.reshape(batch * seq, d)
    for layer in range(depth):
        mod = mods[layer]
        x2 = _token_mixer(x2, mod, g_mix[layer], w_in[layer], g_heads[layer], w_out[layer], batch, seq)
        i = layer // 2
        if layer % 2 == 0:
            x2 = _dense_ffn(x2, mod, g_ffn[layer], w_ff_gate[i], w_ff_up[i], w_ff_down[i], seq)
        else:
            x2 = _moe_ffn(x2, mod, g_ffn[layer], w_router[i], w_exp_gate[i], w_exp_up[i], w_exp_down[i], seq)
    return _final_norm(x2, g_final).reshape(batch, seq, d)
```

```python
import functools

import numpy as np
import jax
import jax.numpy as jnp
from jax import lax
from jax.experimental import pallas as pl
from jax.experimental.pallas import tpu as pltpu

F32 = jnp.float32
BF16 = jnp.bfloat16

HEAD_DIM = 64
HEADS_PER_MIXER = 4
MIXER_WIDTH = HEADS_PER_MIXER * HEAD_DIM
QKV_WIDTH = 3 * MIXER_WIDTH
IDX_HEADS = 8
IDX_DIM = 64
DSA_TOPK = 256
DILATIONS = (1, 4, 16)
DILATED_STEPS = 128
MOBA_BLOCK = 256
MOBA_TOPK = 3
N_EXPERTS = 8
TOP_K_EXPERTS = 2
NORM_EPS = 1e-6
ATTN_SCALE = HEAD_DIM ** -0.5

LANES = 128
Q_TILE = 128
ROW_TILE = 512
FFN_TILE = 1408
EXPERT_ROWS = 512
EXPERT_FF_TILE = 1792
GATHER_ROWS = 512
VMEM_LIMIT = 56 * 1024 * 1024
MASKED = -1e30
SB_UNDERFLOW = 104.0
SB_FIRST_BLOCKS = 3

_NT = (((1,), (1,)), ((), ()))
_TN = (((0,), (0,)), ((), ()))


def _alibi_slopes(mixer_pos):
    idx = np.arange(HEADS_PER_MIXER, dtype=np.float32) * 3 + (mixer_pos + 1)
    return tuple(float(s) for s in np.exp2(-8.0 * idx / 12.0).astype(np.float32))


def _params(*semantics):
    return pltpu.CompilerParams(dimension_semantics=semantics, vmem_limit_bytes=VMEM_LIMIT)


def _modulated_norm(x, gain, shift, scale):
    y = x * lax.rsqrt(jnp.mean(x * x, axis=-1, keepdims=True) + NORM_EPS) * gain
    return y * (1.0 + scale) + shift


def _head_norm(acc, gain):
    return acc * lax.rsqrt(jnp.mean(acc * acc, axis=-1, keepdims=True) + NORM_EPS) * gain


def _head_cols(h):
    return slice(h * HEAD_DIM, (h + 1) * HEAD_DIM)


def _paired_loop(n, body, carry):
    carry = lax.fori_loop(0, n // 2, lambda jj, c: body(2 * jj + 1, body(2 * jj, c, 0), 1), carry)
    return lax.cond(n % 2 == 1, lambda c: body(n - 1, c, 0), lambda c: c, carry)


def _ada_kernel(c_ref, w_ref, b_ref, o_ref):
    c = c_ref[...]
    cond = c / (1.0 + jnp.exp(-c))
    o_ref[0, 0] = jnp.dot(cond, w_ref[0], preferred_element_type=F32,
                          precision=lax.Precision.HIGHEST) + b_ref[0, 0]


def _ada_modulation(c, w_ada, b_ada):
    depth, d, _ = w_ada.shape
    b = c.shape[0]
    out = pl.pallas_call(
        _ada_kernel,
        grid=(depth, 6),
        in_specs=[pl.BlockSpec((b, d), lambda l, k: (0, 0)),
                  pl.BlockSpec((1, d, d), lambda l, k: (l, 0, k)),
                  pl.BlockSpec((1, 1, 1, d), lambda l, k: (l, k, 0, 0))],
        out_specs=pl.BlockSpec((1, 1, b, d), lambda l, k: (l, k, 0, 0)),
        out_shape=jax.ShapeDtypeStruct((depth, 6, b, d), F32),
        compiler_params=_params("arbitrary", "arbitrary"),
        name="ada_modulation",
    )(c, w_ada, b_ada.reshape(depth, 6, 1, d))
    return out.transpose(0, 2, 1, 3)


IN_WIDTHS = (QKV_WIDTH, QKV_WIDTH, QKV_WIDTH, QKV_WIDTH, IDX_HEADS * IDX_DIM + LANES, LANES)


def _in_proj_kernel(x_ref, mod_ref, g_ref, w_ref, oa, ob, oc, od, oidx, ow, oc4, oc16, pc_ref):
    h = _modulated_norm(x_ref[...], g_ref[...], mod_ref[0, 0:1, :], mod_ref[0, 1:2, :]).astype(BF16)
    tm = x_ref.shape[0]
    off = 0
    for o_ref, width in zip((oa, ob, oc, od, oidx, ow), IN_WIDTHS):
        val = jnp.dot(h, w_ref[:, off:off + width], preferred_element_type=F32)
        o_ref[...] = val.astype(o_ref.dtype)
        if o_ref is oc:
            for s in range(QKV_WIDTH // LANES):
                pc_ref[s * tm:(s + 1) * tm, :] = val[:, s * LANES:(s + 1) * LANES]
        off += width
    for r, view in zip(DILATIONS[1:], (oc4, oc16)):
        for c in range(r):
            for s in range(QKV_WIDTH // LANES):
                col = c * QKV_WIDTH + s * LANES
                view[:, col:col + LANES] = pc_ref[pl.ds(s * tm + c, tm // r, stride=r), :].astype(view.dtype)


def _pack_w_in(w_in):
    d = w_in.shape[0]
    n_qkv = 4 * QKV_WIDTH
    n_qi = IDX_HEADS * IDX_DIM
    z = lambda n: jnp.zeros((d, n), w_in.dtype)
    return jnp.concatenate([w_in[:, :n_qkv + n_qi + IDX_DIM], z(LANES - IDX_DIM),
                            w_in[:, n_qkv + n_qi + IDX_DIM:], z(LANES - IDX_HEADS)], axis=1).astype(BF16)


def _in_projection(x2, mod, g_mix, w_packed, seq):
    t, d = x2.shape
    tm = min(ROW_TILE, seq)
    per_batch = seq // tm
    dtypes = (BF16, BF16, BF16, BF16, BF16, F32)
    views = DILATIONS[1:]
    return pl.pallas_call(
        _in_proj_kernel,
        grid=(t // tm,),
        in_specs=[pl.BlockSpec((tm, d), lambda i: (i, 0)),
                  pl.BlockSpec((1, 6, d), lambda i: (i // per_batch, 0, 0)),
                  pl.BlockSpec((1, d), lambda i: (0, 0)),
                  pl.BlockSpec(w_packed.shape, lambda i: (0, 0))],
        out_specs=([pl.BlockSpec((tm, w), lambda i: (i, 0)) for w in IN_WIDTHS]
                   + [pl.BlockSpec((tm // r, r * QKV_WIDTH), lambda i: (i, 0)) for r in views]),
        out_shape=([jax.ShapeDtypeStruct((t, w), dt) for w, dt in zip(IN_WIDTHS, dtypes)]
                   + [jax.ShapeDtypeStruct((t // r, r * QKV_WIDTH), BF16) for r in views]),
        scratch_shapes=[pltpu.VMEM((tm * (QKV_WIDTH // LANES), LANES), F32)],
        compiler_params=_params("arbitrary"),
        name="in_projection",
    )(x2, mod, g_mix.reshape(1, d), w_packed)


def _sb_kernel(q_ref, k_ref, v_ref, g_ref, o_ref):
    tq = q_ref.shape[1]
    i = pl.program_id(1)
    row = lax.broadcasted_iota(jnp.int32, (tq, 1), 0)
    lane = lax.broadcasted_iota(jnp.int32, (1, tq), 1)
    later = (lax.broadcasted_iota(jnp.int32, (tq, tq), 0) > lax.broadcasted_iota(jnp.int32, (tq, tq), 1))
    later = jnp.where(later, 1.0, 0.0).astype(BF16)

    qs = [q_ref[0, :, _head_cols(h)] * ATTN_SCALE for h in range(HEADS_PER_MIXER)]

    heads = range(HEADS_PER_MIXER)

    def blocks(starts, keeps, tails):
        z = jnp.concatenate([lax.dot_general(qs[h], k_ref[0, pl.ds(st, tq), _head_cols(h)], _NT,
                                             preferred_element_type=F32) for st in starts for h in heads], axis=0)
        softplus = jnp.maximum(z, 0.0) + jnp.log(1.0 + jnp.exp(-jnp.abs(z)))
        keep = None
        if any(kp is not None for kp in keeps):
            ones = jnp.ones((tq, tq), F32)
            keep = jnp.concatenate([ones if kp is None else kp for kp in keeps for _ in heads], axis=0)
        log_1m = -softplus if keep is None else -softplus * keep
        hi = log_1m.astype(BF16)
        lo = (log_1m - hi.astype(F32)).astype(BF16)
        inside = jnp.dot(hi, later, preferred_element_type=F32) + jnp.dot(lo, later, preferred_element_type=F32)
        block_sum = jnp.sum(log_1m, axis=1, keepdims=True)
        piece = lambda x, b, h: x[(b * HEADS_PER_MIXER + h) * tq:(b * HEADS_PER_MIXER + h + 1) * tq]
        tail_cols = []
        tails = list(tails)
        for b in range(len(starts)):
            tail_cols += tails
            tails = [tails[h] + piece(block_sum, b, h) for h in heads]
        a = jnp.exp(z - softplus + inside + jnp.concatenate(tail_cols, axis=0))
        if keep is not None:
            a = a * keep
        a = a.astype(BF16)
        av = [sum(jnp.dot(piece(a, b, h), v_ref[0, pl.ds(st, tq), _head_cols(h)], preferred_element_type=F32)
                  for b, st in enumerate(starts)) for h in heads]
        return tails, av

    starts = [pl.multiple_of(jnp.maximum(i - n, 0) * tq, tq) for n in range(SB_FIRST_BLOCKS)]
    keeps = [jnp.where(lane < row, 1.0, 0.0)]
    keeps += [jnp.full((tq, tq), jnp.where(i >= n, 1.0, 0.0), F32) for n in range(1, SB_FIRST_BLOCKS)]
    tails, accs = blocks(starts, keeps, [jnp.zeros((tq, 1), F32)] * HEADS_PER_MIXER)
    state = tuple(zip(tails, accs))

    def body(carry):
        j, state = carry
        tails, av = blocks([pl.multiple_of(j * tq, tq)], [None], [tail for tail, _ in state])
        return j - 1, tuple((tails[h], state[h][1] + av[h]) for h in heads)

    def cond(carry):
        j, state = carry
        worst = functools.reduce(jnp.maximum, [tail for tail, _ in state])
        return jnp.logical_and(j >= 0, jnp.max(worst) > -SB_UNDERFLOW)

    _, state = lax.while_loop(cond, body, (i - SB_FIRST_BLOCKS, state))
    for h in range(HEADS_PER_MIXER):
        o_ref[0, :, _head_cols(h)] = _head_norm(state[h][1], g_ref[h:h + 1, :]).astype(o_ref.dtype)


def _stick_breaking_mixer(p, g_heads):
    b, seq, _ = p.shape
    tq = min(Q_TILE, seq)
    return pl.pallas_call(
        _sb_kernel,
        grid=(b, seq // tq),
        in_specs=[pl.BlockSpec((1, tq, MIXER_WIDTH), lambda bi, i: (bi, i, 0)),
                  pl.BlockSpec((1, seq, MIXER_WIDTH), lambda bi, i: (bi, 0, 1)),
                  pl.BlockSpec((1, seq, MIXER_WIDTH), lambda bi, i: (bi, 0, 2)),
                  pl.BlockSpec((HEADS_PER_MIXER, HEAD_DIM), lambda bi, i: (0, 0))],
        out_specs=pl.BlockSpec((1, tq, MIXER_WIDTH), lambda bi, i: (bi, i, 0)),
        out_shape=jax.ShapeDtypeStruct((b, seq, MIXER_WIDTH), BF16),
        compiler_params=_params("arbitrary", "arbitrary"),
        name="stick_breaking",
    )(p, p, p, g_heads)


def _dsa_kernel(q_ref, k_ref, v_ref, qi_ref, ki_ref, w_ref, g_ref, o_ref, sc_ref, s_ref, p_ref, *, slopes):
    tq = q_ref.shape[1]
    tk = sc_ref.shape[1]
    i = pl.program_id(1)
    n_blocks = ((i + 1) * tq + tk - 1) // tk
    qpos = i * tq + lax.broadcasted_iota(jnp.int32, (1, tq), 1)
    key_row = lax.broadcasted_iota(jnp.int32, (tk, 1), 0)
    neg_inf = float("-inf")

    w_t = w_ref[0].T * (IDX_HEADS ** -0.5 * IDX_DIM ** -0.5)
    w_rows = [w_t[h:h + 1, :] for h in range(IDX_HEADS)]
    qi = qi_ref[0]

    def score_body(j, carry, slot):
        start = pl.multiple_of(j * tk, tk)
        ki = ki_ref[0, pl.ds(start, tk), 0:IDX_DIM]
        sc = jnp.zeros((tk, tq), F32)
        for h in range(IDX_HEADS):
            x = lax.dot_general(ki, qi[:, h * IDX_DIM:(h + 1) * IDX_DIM], _NT, preferred_element_type=F32)
            sc = sc + w_rows[h] * jnp.maximum(x, 0.0)
        sc_ref[j] = jnp.where((start + key_row) <= qpos, sc + 0.0, neg_inf)
        return carry

    _paired_loop(n_blocks, score_body, 0)

    def count(pred):
        def add_block(j, cnt, slot):
            c = jnp.where(pred(sc_ref[j]), 1.0, 0.0)
            return cnt + jnp.sum(c.reshape(tk // 8, 8, tq), axis=0)

        cnt = _paired_loop(n_blocks, add_block, jnp.zeros((8, tq), F32))
        return jnp.sum(cnt, axis=0, keepdims=True)

    int_min = jnp.int32(-2 ** 31)

    def ordered_to_float(u):
        key = u ^ int_min
        bits = jnp.where(key >= 0, key, key ^ jnp.int32(0x7FFFFFFF))
        return lax.bitcast_convert_type(bits, F32)

    def search_body(step, carry):
        prefix, n_at_prefix = carry
        cand = prefix | jnp.left_shift(jnp.int32(1), 31 - step)
        cand_f = ordered_to_float(cand)
        n_ge = count(lambda s: s >= cand_f)
        keep = n_ge >= DSA_TOPK
        return jnp.where(keep, cand, prefix), jnp.where(keep, n_ge, n_at_prefix)

    prefix, n_ge_tau = lax.fori_loop(0, 32, search_body,
                                     (jnp.zeros((1, tq), jnp.int32), jnp.zeros((1, tq), F32)))
    keep_all = (qpos + 1) <= DSA_TOPK
    tau = jnp.where(keep_all, float(np.finfo(np.float32).min), ordered_to_float(prefix))
    no_ties = jnp.min(jnp.where(jnp.logical_or(keep_all, n_ge_tau == DSA_TOPK), 1.0, 0.0)) > 0.5

    earlier = (lax.broadcasted_iota(jnp.int32, (tk, tk), 1) < lax.broadcasted_iota(jnp.int32, (tk, tk), 0))
    earlier = jnp.where(earlier, 1.0, 0.0).astype(BF16)
    qs = [q_ref[0, :, _head_cols(h)] * ATTN_SCALE for h in range(HEADS_PER_MIXER)]
    key_bias = [slopes[h] * lax.broadcasted_iota(jnp.int32, (tk, tq), 0).astype(F32)
                for h in range(HEADS_PER_MIXER)]

    def attn_body(j, carry, slot, n_ties=None):
        ties_seen, stats = carry
        s_slot = [s_ref.at[slot * HEADS_PER_MIXER + h] for h in range(HEADS_PER_MIXER)]
        p_slot = [p_ref.at[slot * HEADS_PER_MIXER + h] for h in range(HEADS_PER_MIXER)]
        start = pl.multiple_of(j * tk, tk)
        sc = sc_ref[j]
        if n_ties is None:
            sel = sc >= tau
        else:
            tie = jnp.where(sc == tau, 1.0, 0.0)
            rank = jnp.dot(earlier, tie.astype(BF16), preferred_element_type=F32) + ties_seen
            sel = jnp.where(sc > tau, 1.0, jnp.where(rank < n_ties, tie, 0.0)) > 0.5
            ties_seen = ties_seen + jnp.sum(tie, axis=0, keepdims=True)
        block_pos = (j * tk - i * tq).astype(F32)
        for h in range(HEADS_PER_MIXER):
            k = k_ref[0, pl.ds(start, tk), _head_cols(h)]
            s = lax.dot_general(k, qs[h], _NT, preferred_element_type=F32) + key_bias[h]
            s_slot[h][...] = jnp.where(sel, s, MASKED)
        locals_ = []
        for h in range(HEADS_PER_MIXER):
            s = s_slot[h][...]
            m_loc = jnp.max(s, axis=0, keepdims=True)
            p = jnp.exp(s - m_loc)
            locals_.append((m_loc, jnp.sum(p, axis=0, keepdims=True)))
            p_slot[h][...] = p.astype(BF16)
        new_stats = []
        for h in range(HEADS_PER_MIXER):
            m, l, acc = stats[h]
            m_loc, l_blk = locals_[h]
            v = v_ref[0, pl.ds(start, tk), _head_cols(h)]
            pv_blk = lax.dot_general(v, p_slot[h][...], _TN, preferred_element_type=F32)
            m_blk = m_loc + slopes[h] * block_pos
            m_new = jnp.maximum(m, m_blk)
            w_old = jnp.exp(m - m_new)
            w_blk = jnp.where(m_loc > 0.5 * MASKED, jnp.exp(m_blk - m_new), 0.0)
            new_stats.append((m_new, w_old * l + w_blk * l_blk, w_old * acc + w_blk * pv_blk))
        return ties_seen, tuple(new_stats)

    init = tuple((jnp.full((1, tq), MASKED, F32), jnp.zeros((1, tq), F32), jnp.zeros((HEAD_DIM, tq), F32))
                 for _ in range(HEADS_PER_MIXER))
    start_carry = (jnp.zeros((1, tq), F32), init)

    def attend_without_ties():
        return _paired_loop(n_blocks, attn_body, start_carry)[1]

    def attend_with_ties():
        n_gt = count(lambda s: s > tau)
        n_ties = jnp.where(keep_all, 1e9, DSA_TOPK - n_gt)
        return _paired_loop(n_blocks, functools.partial(attn_body, n_ties=n_ties), start_carry)[1]

    stats = lax.cond(no_ties, attend_without_ties, attend_with_ties)
    outs = []
    for h in range(HEADS_PER_MIXER):
        _, l, acc = stats[h]
        o = acc / l
        outs.append(o * lax.rsqrt(jnp.mean(o * o, axis=0, keepdims=True) + NORM_EPS) * g_ref[:, h:h + 1])
    o_ref[0] = jnp.concatenate(outs, axis=0).T.astype(o_ref.dtype)


def _dsa_mixer(p, p_idx, p_w, g_heads):
    b, seq, _ = p.shape
    tq = min(2 * Q_TILE, seq)
    tk = tq
    qi_width = IDX_HEADS * IDX_DIM
    return pl.pallas_call(
        functools.partial(_dsa_kernel, slopes=_alibi_slopes(0)),
        grid=(b, seq // tq),
        in_specs=[pl.BlockSpec((1, tq, MIXER_WIDTH), lambda bi, i: (bi, i, 0)),
                  pl.BlockSpec((1, seq, MIXER_WIDTH), lambda bi, i: (bi, 0, 1)),
                  pl.BlockSpec((1, seq, MIXER_WIDTH), lambda bi, i: (bi, 0, 2)),
                  pl.BlockSpec((1, tq, qi_width), lambda bi, i: (bi, i, 0)),
                  pl.BlockSpec((1, seq, LANES), lambda bi, i: (bi, 0, qi_width // LANES)),
                  pl.BlockSpec((1, tq, LANES), lambda bi, i: (bi, i, 0)),
                  pl.BlockSpec((HEAD_DIM, HEADS_PER_MIXER), lambda bi, i: (0, 0))],
        out_specs=pl.BlockSpec((1, tq, MIXER_WIDTH), lambda bi, i: (bi, i, 0)),
        out_shape=jax.ShapeDtypeStruct((b, seq, MIXER_WIDTH), BF16),
        scratch_shapes=[pltpu.VMEM((seq // tk, tk, tq), F32),
                        pltpu.VMEM((2 * HEADS_PER_MIXER, tk, tq), F32),
                        pltpu.VMEM((2 * HEADS_PER_MIXER, tk, tq), BF16)],
        compiler_params=_params("arbitrary", "arbitrary"),
        name="dsa",
    )(p, p, p, p_idx, p_idx, p_w, g_heads.T)


def _band_kernel(q_ref, kp_ref, kc_ref, vp_ref, vc_ref, o_ref, lse_ref, *, dilation, slopes):
    tq = q_ref.shape[1]
    ui = pl.program_id(2)
    heads = range(HEADS_PER_MIXER)
    u_q = jnp.concatenate([ui * tq + lax.broadcasted_iota(jnp.int32, (tq, 1), 0)] * HEADS_PER_MIXER, axis=0)
    u_k = ui * tq - DILATED_STEPS + lax.broadcasted_iota(jnp.int32, (1, DILATED_STEPS + tq), 1)
    steps = u_q - u_k
    valid = jnp.logical_and(jnp.logical_and(steps >= 0, steps <= DILATED_STEPS), u_k >= 0)
    slope_col = jnp.concatenate([jnp.full((tq, 1), slopes[h], F32) for h in heads], axis=0)
    s = jnp.concatenate(
        [lax.dot_general(q_ref[0, :, _head_cols(h)],
                         jnp.concatenate([kp_ref[0, :, _head_cols(h)], kc_ref[0, :, _head_cols(h)]], axis=0),
                         _NT, preferred_element_type=F32) for h in heads], axis=0)
    s = jnp.where(valid, s * ATTN_SCALE - slope_col * (steps * dilation).astype(F32), MASKED)
    m = jnp.max(s, axis=1, keepdims=True)
    e = jnp.exp(s - m)
    den = jnp.sum(e, axis=1, keepdims=True)
    p = e.astype(BF16)
    lse = m + jnp.log(den)
    lane = lax.broadcasted_iota(jnp.int32, (1, LANES), 1)
    lse_all = jnp.zeros((tq, LANES), F32)
    for h in heads:
        rows = slice(h * tq, (h + 1) * tq)
        v = jnp.concatenate([vp_ref[0, :, _head_cols(h)], vc_ref[0, :, _head_cols(h)]], axis=0)
        o_ref[0, :, _head_cols(h)] = jnp.dot(p[rows], v, preferred_element_type=F32) / den[rows]
        lse_all = lse_all + jnp.where(lane == h, lse[rows], 0.0)
    lse_ref[0] = lse_all


def _dilated_branch(view, dilation, slopes):
    b, length, _ = view.shape
    classes = dilation
    tq = min(2 * Q_TILE, length)
    back = tq // DILATED_STEPS
    spec = lambda part, prev: (
        pl.BlockSpec((1, DILATED_STEPS, MIXER_WIDTH),
                     lambda bi, c, ui: (bi, jnp.maximum(ui * back - 1, 0), c * 3 + part)) if prev
        else pl.BlockSpec((1, tq, MIXER_WIDTH), lambda bi, c, ui: (bi, ui, c * 3 + part)))
    out, lse = pl.pallas_call(
        functools.partial(_band_kernel, dilation=dilation, slopes=slopes),
        grid=(b, classes, length // tq),
        in_specs=[spec(0, False), spec(1, True), spec(1, False), spec(2, True), spec(2, False)],
        out_specs=[pl.BlockSpec((1, tq, MIXER_WIDTH), lambda bi, c, ui: (bi, ui, c)),
                   pl.BlockSpec((1, tq, LANES), lambda bi, c, ui: (bi, ui, c))],
        out_shape=[jax.ShapeDtypeStruct((b, length, classes * MIXER_WIDTH), F32),
                   jax.ShapeDtypeStruct((b, length, classes * LANES), F32)],
        compiler_params=_params("arbitrary", "arbitrary", "arbitrary"),
        name=f"dilated_r{dilation}",
    )(view, view, view, view, view)
    return out, lse


def _dilated_merge_kernel(o1, o2, o3, l1, l2, l3, g_ref, o_ref, *scratch):
    tq = o_ref.shape[1]
    chunks = MIXER_WIDTH // LANES
    outs = [[o1[0, :, s * LANES:(s + 1) * LANES] for s in range(chunks)]]
    lses = [l1[0]]
    for r, o_view, l_view, o_nat, l_nat in zip(DILATIONS[1:], (o2, o3), (l2, l3), scratch[0::2], scratch[1::2]):
        for c in range(r):
            for s in range(chunks):
                col = c * MIXER_WIDTH + s * LANES
                o_nat[pl.ds(s * tq + c, tq // r, stride=r), :] = o_view[0, :, col:col + LANES]
            l_nat[pl.ds(c, tq // r, stride=r), :] = l_view[0, :, c * LANES:(c + 1) * LANES]
        outs.append([o_nat[s * tq:(s + 1) * tq, :] for s in range(chunks)])
        lses.append(l_nat[...])
    heads_per_chunk = LANES // HEAD_DIM
    for h in range(HEADS_PER_MIXER):
        sub = slice((h % heads_per_chunk) * HEAD_DIM, (h % heads_per_chunk + 1) * HEAD_DIM)
        lse_h = [l[:, h:h + 1] for l in lses]
        top = functools.reduce(jnp.maximum, lse_h)
        wts = [jnp.exp(l - top) for l in lse_h]
        mixed = sum(w * o[h // heads_per_chunk][:, sub] for w, o in zip(wts, outs)) / sum(wts)
        o_ref[0, :, _head_cols(h)] = _head_norm(mixed, g_ref[h:h + 1, :]).astype(o_ref.dtype)


def _dilated_mixer(views, g_heads):
    b, seq, _ = views[0].shape
    slopes = _alibi_slopes(1)
    branches = [_dilated_branch(v, r, slopes) for v, r in zip(views, DILATIONS)]
    tq = min(ROW_TILE, seq)
    o_specs = [pl.BlockSpec((1, tq // r, r * MIXER_WIDTH), lambda bi, i: (bi, i, 0)) for r in DILATIONS]
    l_specs = [pl.BlockSpec((1, tq // r, r * LANES), lambda bi, i: (bi, i, 0)) for r in DILATIONS]
    scratch = []
    for _ in DILATIONS[1:]:
        scratch += [pltpu.VMEM((tq * (MIXER_WIDTH // LANES), LANES), F32), pltpu.VMEM((tq, LANES), F32)]
    return pl.pallas_call(
        _dilated_merge_kernel,
        grid=(b, seq // tq),
        in_specs=o_specs + l_specs + [pl.BlockSpec((HEADS_PER_MIXER, HEAD_DIM), lambda bi, i: (0, 0))],
        out_specs=pl.BlockSpec((1, tq, MIXER_WIDTH), lambda bi, i: (bi, i, 0)),
        out_shape=jax.ShapeDtypeStruct((b, seq, MIXER_WIDTH), BF16),
        scratch_shapes=scratch,
        compiler_params=_params("arbitrary", "arbitrary"),
        name="dilated_merge",
    )(*[o for o, _ in branches], *[l for _, l in branches], g_heads)


def _moba_kernel(q_ref, k_ref, v_ref, g_ref, o_ref, kmean_ref, chosen_ref, s_ref, p_ref, *, slopes):
    tq = q_ref.shape[1]
    n_kv = k_ref.shape[1] // MOBA_BLOCK
    own = pl.program_id(1)

    @pl.when(own == 0)
    def _():
        kmean_ref[...] = jnp.zeros_like(kmean_ref)
        for n in range(n_kv):
            blk = k_ref[0, n * MOBA_BLOCK:(n + 1) * MOBA_BLOCK, :].astype(F32)
            kmean_ref[n:n + 1, :] = jnp.sum(blk, axis=0, keepdims=True) * (1.0 / MOBA_BLOCK)

    key_row = lax.broadcasted_iota(jnp.int32, (MOBA_BLOCK, 1), 0)
    q_lane = lax.broadcasted_iota(jnp.int32, (1, tq), 1)
    blk_row = lax.broadcasted_iota(jnp.int32, (LANES, 1), 0)
    blk_f = blk_row.astype(F32)
    neg_inf = float("-inf")

    qs = []
    for h in range(HEADS_PER_MIXER):
        q = q_ref[0, :, _head_cols(h)]
        gate = lax.dot_general(kmean_ref[:, _head_cols(h)], q.astype(F32), _NT, preferred_element_type=F32,
                               precision=lax.Precision.HIGHEST)
        gate = jnp.where(blk_row < own, gate, neg_inf)
        picks = jnp.zeros((LANES, tq), F32)
        for _ in range(MOBA_TOPK):
            top = jnp.max(gate, axis=0, keepdims=True)
            is_top = jnp.logical_and(gate == top, top > neg_inf)
            first = jnp.min(jnp.where(is_top, blk_f, float(LANES)), axis=0, keepdims=True)
            pick = blk_f == first
            picks = jnp.where(pick, 1.0, picks)
            gate = jnp.where(pick, neg_inf, gate)
        chosen_ref[h] = picks
        qs.append(q * ATTN_SCALE)

    key_bias = [slopes[h] * lax.broadcasted_iota(jnp.int32, (MOBA_BLOCK, tq), 0).astype(F32)
                for h in range(HEADS_PER_MIXER)]

    def block_softmax(start, keep=None, slot=0):
        s_slot = [s_ref.at[slot * HEADS_PER_MIXER + h] for h in range(HEADS_PER_MIXER)]
        p_slot = [p_ref.at[slot * HEADS_PER_MIXER + h] for h in range(HEADS_PER_MIXER)]
        for h in range(HEADS_PER_MIXER):
            k = k_ref[0, pl.ds(start, MOBA_BLOCK), _head_cols(h)]
            s = lax.dot_general(k, qs[h], _NT, preferred_element_type=F32) + key_bias[h]
            s_slot[h][...] = s if keep is None else jnp.where(keep, s, MASKED)
        pieces = []
        for h in range(HEADS_PER_MIXER):
            s = s_slot[h][...]
            m_loc = jnp.max(s, axis=0, keepdims=True)
            p = jnp.exp(s - m_loc)
            pieces.append((m_loc, jnp.sum(p, axis=0, keepdims=True)))
            p_slot[h][...] = p.astype(BF16)
        out = []
        for h in range(HEADS_PER_MIXER):
            v = v_ref[0, pl.ds(start, MOBA_BLOCK), _head_cols(h)]
            out.append(pieces[h] + (lax.dot_general(v, p_slot[h][...], _TN, preferred_element_type=F32),))
        return out

    own_start = pl.multiple_of(own * MOBA_BLOCK, MOBA_BLOCK)
    stats = tuple(block_softmax(own_start, keep=key_row <= q_lane))

    def body(j, stats, slot):
        start = pl.multiple_of(j * MOBA_BLOCK, MOBA_BLOCK)
        block_pos = ((j - own) * MOBA_BLOCK).astype(F32)
        merged = []
        for h, (m_loc, l_blk, pv_blk) in enumerate(block_softmax(start, slot=slot)):
            picked = chosen_ref[h, pl.ds(j, 1), :] > 0.5
            m_blk = jnp.where(picked, m_loc + slopes[h] * block_pos, MASKED)
            m, l, acc = stats[h]
            m_new = jnp.maximum(m, m_blk)
            w_old = jnp.exp(m - m_new)
            w_blk = jnp.exp(m_blk - m_new)
            merged.append((m_new, w_old * l + w_blk * l_blk, w_old * acc + w_blk * pv_blk))
        return tuple(merged)

    stats = _paired_loop(own, body, stats)
    outs = []
    for h in range(HEADS_PER_MIXER):
        _, l, acc = stats[h]
        o = acc / l
        outs.append(o * lax.rsqrt(jnp.mean(o * o, axis=0, keepdims=True) + NORM_EPS) * g_ref[:, h:h + 1])
    o_ref[0] = jnp.concatenate(outs, axis=0).T.astype(o_ref.dtype)


def _moba_mixer(p, g_heads):
    b, seq, _ = p.shape
    tq = MOBA_BLOCK
    return pl.pallas_call(
        functools.partial(_moba_kernel, slopes=_alibi_slopes(2)),
        grid=(b, seq // tq),
        in_specs=[pl.BlockSpec((1, tq, MIXER_WIDTH), lambda bi, i: (bi, i, 0)),
                  pl.BlockSpec((1, seq, MIXER_WIDTH), lambda bi, i: (bi, 0, 1)),
                  pl.BlockSpec((1, seq, MIXER_WIDTH), lambda bi, i: (bi, 0, 2)),
                  pl.BlockSpec((HEAD_DIM, HEADS_PER_MIXER), lambda bi, i: (0, 0))],
        out_specs=pl.BlockSpec((1, tq, MIXER_WIDTH), lambda bi, i: (bi, i, 0)),
        out_shape=jax.ShapeDtypeStruct((b, seq, MIXER_WIDTH), BF16),
        scratch_shapes=[pltpu.VMEM((LANES, MIXER_WIDTH), F32),
                        pltpu.VMEM((HEADS_PER_MIXER, LANES, tq), F32),
                        pltpu.VMEM((2 * HEADS_PER_MIXER, MOBA_BLOCK, tq), F32),
                        pltpu.VMEM((2 * HEADS_PER_MIXER, MOBA_BLOCK, tq), BF16)],
        compiler_params=_params("arbitrary", "arbitrary"),
        name="moba",
    )(p, p, p, g_heads.T)


def _out_proj_kernel(oa, ob, oc, od, w_ref, x_ref, mod_ref, o_ref):
    acc = jnp.zeros(x_ref.shape, F32)
    for m, o in enumerate((oa, ob, oc, od)):
        acc = acc + jnp.dot(o[...], w_ref[m * MIXER_WIDTH:(m + 1) * MIXER_WIDTH, :], preferred_element_type=F32)
    o_ref[...] = x_ref[...] + mod_ref[0, 2:3, :] * acc


def _out_projection(mixed, w_out, x2, mod, seq):
    t, d = x2.shape
    tm = min(ROW_TILE, seq)
    per_batch = seq // tm
    o_spec = pl.BlockSpec((tm, MIXER_WIDTH), lambda i: (i, 0))
    return pl.pallas_call(
        _out_proj_kernel,
        grid=(t // tm,),
        in_specs=[o_spec] * 4 + [pl.BlockSpec(w_out.shape, lambda i: (0, 0)),
                                 pl.BlockSpec((tm, d), lambda i: (i, 0)),
                                 pl.BlockSpec((1, 6, d), lambda i: (i // per_batch, 0, 0))],
        out_specs=pl.BlockSpec((tm, d), lambda i: (i, 0)),
        out_shape=jax.ShapeDtypeStruct((t, d), F32),
        compiler_params=_params("arbitrary"),
        name="out_projection",
    )(*[o.reshape(t, MIXER_WIDTH) for o in mixed], w_out, x2, mod)


def _ffn_kernel(x_ref, mod_ref, g_ref, wg_ref, wu_ref, wd_ref, o_ref, h_ref, acc_ref):
    f = pl.program_id(1)

    @pl.when(f == 0)
    def _():
        h_ref[...] = _modulated_norm(x_ref[...], g_ref[...], mod_ref[0, 3:4, :], mod_ref[0, 4:5, :]).astype(BF16)
        acc_ref[...] = jnp.zeros_like(acc_ref)

    h = h_ref[...]
    gate = jnp.dot(h, wg_ref[...], preferred_element_type=F32)
    up = jnp.dot(h, wu_ref[...], preferred_element_type=F32)
    act = (gate / (1.0 + jnp.exp(-gate)) * up).astype(BF16)
    acc_ref[...] += jnp.dot(act, wd_ref[...], preferred_element_type=F32)

    @pl.when(f == pl.num_programs(1) - 1)
    def _():
        o_ref[...] = x_ref[...] + mod_ref[0, 5:6, :] * acc_ref[...]


def _dense_ffn(x2, mod, g_ffn, wg, wu, wd, seq):
    t, d = x2.shape
    d_ff = wg.shape[1]
    tm = min(ROW_TILE, seq)
    tf = FFN_TILE if d_ff % FFN_TILE == 0 else d_ff
    per_batch = seq // tm
    return pl.pallas_call(
        _ffn_kernel,
        grid=(t // tm, d_ff // tf),
        in_specs=[pl.BlockSpec((tm, d), lambda i, f: (i, 0)),
                  pl.BlockSpec((1, 6, d), lambda i, f: (i // per_batch, 0, 0)),
                  pl.BlockSpec((1, d), lambda i, f: (0, 0)),
                  pl.BlockSpec((d, tf), lambda i, f: (0, f)),
                  pl.BlockSpec((d, tf), lambda i, f: (0, f)),
                  pl.BlockSpec((tf, d), lambda i, f: (f, 0))],
        out_specs=pl.BlockSpec((tm, d), lambda i, f: (i, 0)),
        out_shape=jax.ShapeDtypeStruct((t, d), F32),
        scratch_shapes=[pltpu.VMEM((tm, d), BF16), pltpu.VMEM((tm, d), F32)],
        compiler_params=_params("arbitrary", "arbitrary"),
        name="dense_ffn",
    )(x2, mod, g_ffn.reshape(1, d), wg.astype(BF16), wu.astype(BF16), wd.astype(BF16))


def _store_row_tiles(dst_ref, value):
    rows, d = value.shape
    chunks = d // LANES
    for s in range(chunks):
        dst_ref[pl.ds(s, rows, stride=chunks), :] = value[:, s * LANES:(s + 1) * LANES]


def _load_row_tile_chunk(src_ref, s, rows, chunks):
    return src_ref[pl.ds(s, rows, stride=chunks), :]


def _router_kernel(x_ref, mod_ref, g_ref, wr_ref, h_ref, logit_ref):
    h = _modulated_norm(x_ref[...], g_ref[...], mod_ref[0, 3:4, :], mod_ref[0, 4:5, :])
    _store_row_tiles(h_ref, h)
    logit_ref[...] = jnp.dot(h, wr_ref[...], preferred_element_type=F32, precision=lax.Precision.HIGHEST)


def _router(x2, mod, g_ffn, w_router, seq):
    t, d = x2.shape
    tm = min(ROW_TILE, seq)
    per_batch = seq // tm
    wr = jnp.zeros((d, LANES), F32).at[:, :N_EXPERTS].set(w_router.astype(F32))
    return pl.pallas_call(
        _router_kernel,
        grid=(t // tm,),
        in_specs=[pl.BlockSpec((tm, d), lambda i: (i, 0)),
                  pl.BlockSpec((1, 6, d), lambda i: (i // per_batch, 0, 0)),
                  pl.BlockSpec((1, d), lambda i: (0, 0)),
                  pl.BlockSpec((d, LANES), lambda i: (0, 0))],
        out_specs=[pl.BlockSpec((tm * (d // LANES), LANES), lambda i: (i, 0)),
                   pl.BlockSpec((tm, LANES), lambda i: (i, 0))],
        out_shape=[jax.ShapeDtypeStruct((t * (d // LANES), LANES), F32), jax.ShapeDtypeStruct((t, LANES), F32)],
        compiler_params=_params("arbitrary"),
        name="moe_router",
    )(x2, mod, g_ffn.reshape(1, d), wr)


def _row_copy(src_hbm, row, dst_ref, r, sem, chunks):
    src = src_hbm.at[pl.ds(pl.multiple_of(row * chunks, chunks), chunks), :]
    return pltpu.make_async_copy(src, dst_ref.at[pl.ds(pl.multiple_of(r * chunks, chunks), chunks), :], sem)


def _gather_kernel(idx_ref, src_hbm, o_ref, sem, *, chunks):
    rows = o_ref.shape[0] // chunks

    def start(pair, c):
        for lane in range(2):
            r = 2 * pair + lane
            _row_copy(src_hbm, idx_ref[0, 0, r], o_ref, r, sem.at[lane], chunks).start(priority=lane)
        return c

    def wait(pair, c):
        for lane in range(2):
            r = 2 * pair + lane
            _row_copy(src_hbm, idx_ref[0, 0, r], o_ref, r, sem.at[lane], chunks).wait()
        return c

    lax.fori_loop(0, rows // 2, start, 0)
    lax.fori_loop(0, rows // 2, wait, 0)


def _gather_rows(src, idx, rows_per_step, chunks):
    n = idx.shape[0]
    steps = n // rows_per_step
    return pl.pallas_call(
        functools.partial(_gather_kernel, chunks=chunks),
        grid=(steps,),
        in_specs=[pl.BlockSpec((1, 1, rows_per_step), lambda i: (i, 0, 0), memory_space=pltpu.SMEM),
                  pl.BlockSpec(memory_space=pl.ANY)],
        out_specs=pl.BlockSpec((rows_per_step * chunks, LANES), lambda i: (i, 0)),
        out_shape=jax.ShapeDtypeStruct((n * chunks, LANES), src.dtype),
        scratch_shapes=[pltpu.SemaphoreType.DMA((2,))],
        compiler_params=_params("arbitrary"),
        name="gather_rows",
    )(idx.reshape(steps, 1, rows_per_step), src)


def _expert_kernel(be_ref, used_ref, xs_ref, wg_ref, wu_ref, wd_ref, o_ref, xb_ref, acc_ref):
    m = pl.program_id(0)
    f = pl.program_id(1)
    live = m < used_ref[0]
    rows, d = xb_ref.shape
    chunks = d // LANES

    @pl.when(jnp.logical_and(live, f == 0))
    def _():
        for s in range(chunks):
            xb_ref[:, s * LANES:(s + 1) * LANES] = _load_row_tile_chunk(xs_ref, s, rows, chunks).astype(BF16)
        acc_ref[...] = jnp.zeros_like(acc_ref)

    @pl.when(live)
    def _():
        h = xb_ref[...]
        gate = jnp.dot(h, wg_ref[0], preferred_element_type=F32)
        up = jnp.dot(h, wu_ref[0], preferred_element_type=F32)
        act = (gate / (1.0 + jnp.exp(-gate)) * up).astype(BF16)
        acc_ref[...] += jnp.dot(act, wd_ref[0], preferred_element_type=F32)

    @pl.when(f == pl.num_programs(1) - 1)
    def _():
        _store_row_tiles(o_ref, jnp.where(live, acc_ref[...], 0.0))


def _expert_ffn(xs, block_expert, n_used, wg, wu, wd):
    d = wg.shape[1]
    chunks = d // LANES
    n_slots = xs.shape[0] // chunks
    d_ff = wg.shape[2]
    tm = EXPERT_ROWS
    tf = EXPERT_FF_TILE if d_ff % EXPERT_FF_TILE == 0 else d_ff
    nf = d_ff // tf

    def fcol(m, f, used):
        return jnp.where(m < used[0], f, nf - 1)

    grid_spec = pltpu.PrefetchScalarGridSpec(
        num_scalar_prefetch=2,
        grid=(n_slots // tm, nf),
        in_specs=[pl.BlockSpec((tm * chunks, LANES), lambda m, f, be, used: (m, 0)),
                  pl.BlockSpec((1, d, tf), lambda m, f, be, used: (be[m], 0, fcol(m, f, used))),
                  pl.BlockSpec((1, d, tf), lambda m, f, be, used: (be[m], 0, fcol(m, f, used))),
                  pl.BlockSpec((1, tf, d), lambda m, f, be, used: (be[m], fcol(m, f, used), 0))],
        out_specs=pl.BlockSpec((tm * chunks, LANES), lambda m, f, be, used: (m, 0)),
        scratch_shapes=[pltpu.VMEM((tm, d), BF16), pltpu.VMEM((tm, d), F32)])
    return pl.pallas_call(
        _expert_kernel,
        grid_spec=grid_spec,
        out_shape=jax.ShapeDtypeStruct((n_slots * chunks, LANES), F32),
        compiler_params=_params("arbitrary", "arbitrary"),
        name="expert_ffn",
    )(block_expert, n_used, xs, wg.astype(BF16), wu.astype(BF16), wd.astype(BF16))


def _combine_kernel(d0_ref, d1_ref, ys_hbm, x_ref, gates_ref, mod_ref, o_ref, y0_ref, y1_ref, sem):
    rows, d = o_ref.shape
    chunks = d // LANES

    def start(r, c):
        _row_copy(ys_hbm, d0_ref[0, 0, r], y0_ref, r, sem.at[0], chunks).start(priority=0)
        _row_copy(ys_hbm, d1_ref[0, 0, r], y1_ref, r, sem.at[1], chunks).start(priority=1)
        return c

    def wait(r, c):
        _row_copy(ys_hbm, d0_ref[0, 0, r], y0_ref, r, sem.at[0], chunks).wait()
        _row_copy(ys_hbm, d1_ref[0, 0, r], y1_ref, r, sem.at[1], chunks).wait()
        return c

    lax.fori_loop(0, rows, start, 0)
    lax.fori_loop(0, rows, wait, 0)
    gates = gates_ref[...]
    for s in range(chunks):
        cols = slice(s * LANES, (s + 1) * LANES)
        y = (_load_row_tile_chunk(y0_ref, s, rows, chunks) * gates[:, 0:1]
             + _load_row_tile_chunk(y1_ref, s, rows, chunks) * gates[:, 1:2])
        o_ref[:, cols] = x_ref[:, cols] + mod_ref[0, 5:6, cols] * y


def _moe_combine(ys, dest0, dest1, gates, x2, mod, seq):
    t, d = x2.shape
    tm = min(GATHER_ROWS, seq)
    steps = t // tm
    per_batch = seq // tm
    idx_spec = pl.BlockSpec((1, 1, tm), lambda i: (i, 0, 0), memory_space=pltpu.SMEM)
    return pl.pallas_call(
        _combine_kernel,
        grid=(steps,),
        in_specs=[idx_spec, idx_spec, pl.BlockSpec(memory_space=pl.ANY),
                  pl.BlockSpec((tm, d), lambda i: (i, 0)),
                  pl.BlockSpec((tm, TOP_K_EXPERTS), lambda i: (i, 0)),
                  pl.BlockSpec((1, 6, d), lambda i: (i // per_batch, 0, 0))],
        out_specs=pl.BlockSpec((tm, d), lambda i: (i, 0)),
        out_shape=jax.ShapeDtypeStruct((t, d), F32),
        scratch_shapes=[pltpu.VMEM((tm * (d // LANES), LANES), F32), pltpu.VMEM((tm * (d // LANES), LANES), F32),
                        pltpu.SemaphoreType.DMA((2,))],
        compiler_params=_params("arbitrary"),
        name="moe_combine",
    )(dest0.reshape(steps, 1, tm), dest1.reshape(steps, 1, tm), ys, x2, gates, mod)


def _moe_ffn(x2, mod, g_ffn, w_router, wg, wu, wd, seq):
    t, d = x2.shape
    h, logits = _router(x2, mod, g_ffn, w_router, seq)
    top_val, top_idx = lax.top_k(logits[:, :N_EXPERTS], TOP_K_EXPERTS)
    gates = jax.nn.softmax(top_val, axis=-1)

    n_assign = t * TOP_K_EXPERTS
    flat_e = top_idx.reshape(-1).astype(jnp.int32)
    onehot = (flat_e[:, None] == jnp.arange(N_EXPERTS, dtype=jnp.int32)[None, :]).astype(jnp.int32)
    rank = jnp.take_along_axis(jnp.cumsum(onehot, axis=0), flat_e[:, None], axis=1)[:, 0] - 1
    counts = jnp.sum(onehot, axis=0)
    padded = (counts + EXPERT_ROWS - 1) // EXPERT_ROWS * EXPERT_ROWS
    pad_end = jnp.cumsum(padded)
    dest = (pad_end - padded)[flat_e] + rank
    n_slots = (n_assign // EXPERT_ROWS + N_EXPERTS) * EXPERT_ROWS
    n_blocks = n_slots // EXPERT_ROWS
    slot_tok = jnp.zeros((n_slots,), jnp.int32).at[dest].set(jnp.arange(n_assign, dtype=jnp.int32) // TOP_K_EXPERTS)
    block_start = jnp.arange(n_blocks, dtype=jnp.int32) * EXPERT_ROWS
    block_expert = jnp.minimum(jnp.searchsorted(pad_end, block_start, side="right"), N_EXPERTS - 1).astype(jnp.int32)
    n_used = (pad_end[-1:] // EXPERT_ROWS).astype(jnp.int32)

    xs = _gather_rows(h, slot_tok, GATHER_ROWS, d // LANES)
    ys = _expert_ffn(xs, block_expert, n_used, wg, wu, wd)
    dest2 = dest.reshape(t, TOP_K_EXPERTS)
    return _moe_combine(ys, dest2[:, 0], dest2[:, 1], gates, x2, mod, seq)


def _final_norm_kernel(x_ref, g_ref, o_ref):
    x = x_ref[...]
    o_ref[...] = x * lax.rsqrt(jnp.mean(x * x, axis=-1, keepdims=True) + NORM_EPS) * g_ref[...]


def _final_norm(x2, g_final):
    t, d = x2.shape
    tm = min(ROW_TILE, t)
    return pl.pallas_call(
        _final_norm_kernel,
        grid=(t // tm,),
        in_specs=[pl.BlockSpec((tm, d), lambda i: (i, 0)), pl.BlockSpec((1, d), lambda i: (0, 0))],
        out_specs=pl.BlockSpec((tm, d), lambda i: (i, 0)),
        out_shape=jax.ShapeDtypeStruct((t, d), F32),
        compiler_params=_params("arbitrary"),
        name="final_norm",
    )(x2, g_final.reshape(1, d))


def _token_mixer(x2, mod, g_mix, w_in, g_heads, w_out, batch, seq):
    pa, pb, pc, pd, p_idx, p_w, *pc_views = _in_projection(x2, mod, g_mix, _pack_w_in(w_in), seq)
    shape3 = lambda a: a.reshape(batch, -1, a.shape[-1])
    gh = g_heads.reshape(4, HEADS_PER_MIXER, HEAD_DIM)
    mixed = (_stick_breaking_mixer(shape3(pa), gh[0]),
             _dsa_mixer(shape3(pb), shape3(p_idx), shape3(p_w), gh[1]),
             _dilated_mixer([shape3(v) for v in [pc] + pc_views], gh[2]),
             _moba_mixer(shape3(pd), gh[3]))
    return _out_projection(mixed, w_out.astype(BF16), x2, mod, seq)


def kernel(x, c, w_ada, b_ada, g_mix, w_in, g_heads, w_out, g_ffn, w_ff_gate, w_ff_up, w_ff_down, w_router, w_exp_gate, w_exp_up, w_exp_down, g_final):
    batch, seq, d = x.shape
    depth = w_ada.shape[0]
    mods = _ada_modulation(c, w_ada, b_ada)
    x2 = x.reshape(batch * seq, d)
    for layer in range(depth):
        mod = mods[layer]
        x2 = _token_mixer(x2, mod, g_mix[layer], w_in[layer], g_heads[layer], w_out[layer], batch, seq)
        i = layer // 2
        if layer % 2 == 0:
            x2 = _dense_ffn(x2, mod, g_ffn[layer], w_ff_gate[i], w_ff_up[i], w_ff_down[i], seq)
        else:
            x2 = _moe_ffn(x2, mod, g_ffn[layer], w_router[i], w_exp_gate[i], w_exp_up[i], w_exp_down[i], seq)
    return _final_norm(x2, g_final).reshape(batch, seq, d)
```

```python
import functools

import numpy as np
import jax
import jax.numpy as jnp
from jax import lax
from jax.experimental import pallas as pl
from jax.experimental.pallas import tpu as pltpu

F32 = jnp.float32
BF16 = jnp.bfloat16

HEAD_DIM = 64
HEADS_PER_MIXER = 4
MIXER_WIDTH = HEADS_PER_MIXER * HEAD_DIM
QKV_WIDTH = 3 * MIXER_WIDTH
IDX_HEADS = 8
IDX_DIM = 64
DSA_TOPK = 256
DILATIONS = (1, 4, 16)
DILATED_STEPS = 128
MOBA_BLOCK = 256
MOBA_TOPK = 3
N_EXPERTS = 8
TOP_K_EXPERTS = 2
NORM_EPS = 1e-6
ATTN_SCALE = HEAD_DIM ** -0.5

LANES = 128
Q_TILE = 128
ROW_TILE = 512
FFN_TILE = 1408
EXPERT_ROWS = 512
EXPERT_FF_TILE = 1792
GATHER_ROWS = 512
VMEM_LIMIT = 56 * 1024 * 1024
MASKED = -1e30
SB_UNDERFLOW = 104.0
SB_FIRST_BLOCKS = 3

_NT = (((1,), (1,)), ((), ()))
_TN = (((0,), (0,)), ((), ()))


def _alibi_slopes(mixer_pos):
    idx = np.arange(HEADS_PER_MIXER, dtype=np.float32) * 3 + (mixer_pos + 1)
    return tuple(float(s) for s in np.exp2(-8.0 * idx / 12.0).astype(np.float32))


def _params(*semantics):
    return pltpu.CompilerParams(dimension_semantics=semantics, vmem_limit_bytes=VMEM_LIMIT)


def _modulated_norm(x, gain, shift, scale):
    y = x * lax.rsqrt(jnp.mean(x * x, axis=-1, keepdims=True) + NORM_EPS) * gain
    return y * (1.0 + scale) + shift


def _head_norm(acc, gain):
    return acc * lax.rsqrt(jnp.mean(acc * acc, axis=-1, keepdims=True) + NORM_EPS) * gain


def _head_cols(h):
    return slice(h * HEAD_DIM, (h + 1) * HEAD_DIM)


def _paired_loop(n, body, carry):
    carry = lax.fori_loop(0, n // 2, lambda jj, c: body(2 * jj + 1, body(2 * jj, c, 0), 1), carry)
    return lax.cond(n % 2 == 1, lambda c: body(n - 1, c, 0), lambda c: c, carry)


def _ada_kernel(c_ref, w_ref, b_ref, o_ref):
    c = c_ref[...]
    cond = c / (1.0 + jnp.exp(-c))
    o_ref[0, 0] = jnp.dot(cond, w_ref[0], preferred_element_type=F32,
                          precision=lax.Precision.HIGHEST) + b_ref[0, 0]


def _ada_modulation(c, w_ada, b_ada):
    depth, d, _ = w_ada.shape
    b = c.shape[0]
    out = pl.pallas_call(
        _ada_kernel,
        grid=(depth, 6),
        in_specs=[pl.BlockSpec((b, d), lambda l, k: (0, 0)),
                  pl.BlockSpec((1, d, d), lambda l, k: (l, 0, k)),
                  pl.BlockSpec((1, 1, 1, d), lambda l, k: (l, k, 0, 0))],
        out_specs=pl.BlockSpec((1, 1, b, d), lambda l, k: (l, k, 0, 0)),
        out_shape=jax.ShapeDtypeStruct((depth, 6, b, d), F32),
        compiler_params=_params("arbitrary", "arbitrary"),
        name="ada_modulation",
    )(c, w_ada, b_ada.reshape(depth, 6, 1, d))
    return out.transpose(0, 2, 1, 3)


IN_WIDTHS = (QKV_WIDTH, QKV_WIDTH, QKV_WIDTH, QKV_WIDTH, IDX_HEADS * IDX_DIM + LANES, LANES)


def _in_proj_kernel(x_ref, mod_ref, g_ref, w_ref, oa, ob, oc, od, oidx, ow, oc4, oc16, pc_ref):
    h = _modulated_norm(x_ref[...], g_ref[...], mod_ref[0, 0:1, :], mod_ref[0, 1:2, :]).astype(BF16)
    tm = x_ref.shape[0]
    off = 0
    for o_ref, width in zip((oa, ob, oc, od, oidx, ow), IN_WIDTHS):
        val = jnp.dot(h, w_ref[:, off:off + width], preferred_element_type=F32)
        o_ref[...] = val.astype(o_ref.dtype)
        if o_ref is oc:
            for s in range(QKV_WIDTH // LANES):
                pc_ref[s * tm:(s + 1) * tm, :] = val[:, s * LANES:(s + 1) * LANES]
        off += width
    for r, view in zip(DILATIONS[1:], (oc4, oc16)):
        for c in range(r):
            for s in range(QKV_WIDTH // LANES):
                col = c * QKV_WIDTH + s * LANES
                view[:, col:col + LANES] = pc_ref[pl.ds(s * tm + c, tm // r, stride=r), :].astype(view.dtype)


def _pack_w_in(w_in):
    d = w_in.shape[0]
    n_qkv = 4 * QKV_WIDTH
    n_qi = IDX_HEADS * IDX_DIM
    z = lambda n: jnp.zeros((d, n), w_in.dtype)
    return jnp.concatenate([w_in[:, :n_qkv + n_qi + IDX_DIM], z(LANES - IDX_DIM),
                            w_in[:, n_qkv + n_qi + IDX_DIM:], z(LANES - IDX_HEADS)], axis=1).astype(BF16)


def _in_projection(x2, mod, g_mix, w_packed, seq):
    t, d = x2.shape
    tm = min(ROW_TILE, seq)
    per_batch = seq // tm
    dtypes = (BF16, BF16, BF16, BF16, BF16, F32)
    views = DILATIONS[1:]
    return pl.pallas_call(
        _in_proj_kernel,
        grid=(t // tm,),
        in_specs=[pl.BlockSpec((tm, d), lambda i: (i, 0)),
                  pl.BlockSpec((1, 6, d), lambda i: (i // per_batch, 0, 0)),
                  pl.BlockSpec((1, d), lambda i: (0, 0)),
                  pl.BlockSpec(w_packed.shape, lambda i: (0, 0))],
        out_specs=([pl.BlockSpec((tm, w), lambda i: (i, 0)) for w in IN_WIDTHS]
                   + [pl.BlockSpec((tm // r, r * QKV_WIDTH), lambda i: (i, 0)) for r in views]),
        out_shape=([jax.ShapeDtypeStruct((t, w), dt) for w, dt in zip(IN_WIDTHS, dtypes)]
                   + [jax.ShapeDtypeStruct((t // r, r * QKV_WIDTH), BF16) for r in views]),
        scratch_shapes=[pltpu.VMEM((tm * (QKV_WIDTH // LANES), LANES), F32)],
        compiler_params=_params("arbitrary"),
        name="in_projection",
    )(x2, mod, g_mix.reshape(1, d), w_packed)


def _sb_kernel(q_ref, k_ref, v_ref, g_ref, o_ref):
    tq = q_ref.shape[1]
    i = pl.program_id(1)
    row = lax.broadcasted_iota(jnp.int32, (tq, 1), 0)
    lane = lax.broadcasted_iota(jnp.int32, (1, tq), 1)
    later = (lax.broadcasted_iota(jnp.int32, (tq, tq), 0) > lax.broadcasted_iota(jnp.int32, (tq, tq), 1))
    later = jnp.where(later, 1.0, 0.0).astype(BF16)

    qs = [q_ref[0, :, _head_cols(h)] * ATTN_SCALE for h in range(HEADS_PER_MIXER)]

    heads = range(HEADS_PER_MIXER)

    def blocks(starts, keeps, tails):
        z = jnp.concatenate([lax.dot_general(qs[h], k_ref[0, pl.ds(st, tq), _head_cols(h)], _NT,
                                             preferred_element_type=F32) for st in starts for h in heads], axis=0)
        softplus = jnp.maximum(z, 0.0) + jnp.log(1.0 + jnp.exp(-jnp.abs(z)))
        keep = None
        if any(kp is not None for kp in keeps):
            ones = jnp.ones((tq, tq), F32)
            keep = jnp.concatenate([ones if kp is None else kp for kp in keeps for _ in heads], axis=0)
        log_1m = -softplus if keep is None else -softplus * keep
        hi = log_1m.astype(BF16)
        lo = (log_1m - hi.astype(F32)).astype(BF16)
        inside = jnp.dot(hi, later, preferred_element_type=F32) + jnp.dot(lo, later, preferred_element_type=F32)
        block_sum = jnp.sum(log_1m, axis=1, keepdims=True)
        piece = lambda x, b, h: x[(b * HEADS_PER_MIXER + h) * tq:(b * HEADS_PER_MIXER + h + 1) * tq]
        tail_cols = []
        tails = list(tails)
        for b in range(len(starts)):
            tail_cols += tails
            tails = [tails[h] + piece(block_sum, b, h) for h in heads]
        a = jnp.exp(z - softplus + inside + jnp.concatenate(tail_cols, axis=0))
        if keep is not None:
            a = a * keep
        a = a.astype(BF16)
        av = [sum(jnp.dot(piece(a, b, h), v_ref[0, pl.ds(st, tq), _head_cols(h)], preferred_element_type=F32)
                  for b, st in enumerate(starts)) for h in heads]
        return tails, av

    starts = [pl.multiple_of(jnp.maximum(i - n, 0) * tq, tq) for n in range(SB_FIRST_BLOCKS)]
    keeps = [jnp.where(lane < row, 1.0, 0.0)]
    keeps += [jnp.full((tq, tq), jnp.where(i >= n, 1.0, 0.0), F32) for n in range(1, SB_FIRST_BLOCKS)]
    tails, accs = blocks(starts, keeps, [jnp.zeros((tq, 1), F32)] * HEADS_PER_MIXER)
    state = tuple(zip(tails, accs))

    def body(carry):
        j, state = carry
        tails, av = blocks([pl.multiple_of(j * tq, tq)], [None], [tail for tail, _ in state])
        return j - 1, tuple((tails[h], state[h][1] + av[h]) for h in heads)

    def cond(carry):
        j, state = carry
        worst = functools.reduce(jnp.maximum, [tail for tail, _ in state])
        return jnp.logical_and(j >= 0, jnp.max(worst) > -SB_UNDERFLOW)

    _, state = lax.while_loop(cond, body, (i - SB_FIRST_BLOCKS, state))
    for h in range(HEADS_PER_MIXER):
        o_ref[0, :, _head_cols(h)] = _head_norm(state[h][1], g_ref[h:h + 1, :]).astype(o_ref.dtype)


def _stick_breaking_mixer(p, g_heads):
    b, seq, _ = p.shape
    tq = min(Q_TILE, seq)
    return pl.pallas_call(
        _sb_kernel,
        grid=(b, seq // tq),
        in_specs=[pl.BlockSpec((1, tq, MIXER_WIDTH), lambda bi, i: (bi, i, 0)),
                  pl.BlockSpec((1, seq, MIXER_WIDTH), lambda bi, i: (bi, 0, 1)),
                  pl.BlockSpec((1, seq, MIXER_WIDTH), lambda bi, i: (bi, 0, 2)),
                  pl.BlockSpec((HEADS_PER_MIXER, HEAD_DIM), lambda bi, i: (0, 0))],
        out_specs=pl.BlockSpec((1, tq, MIXER_WIDTH), lambda bi, i: (bi, i, 0)),
        out_shape=jax.ShapeDtypeStruct((b, seq, MIXER_WIDTH), BF16),
        compiler_params=_params("arbitrary", "arbitrary"),
        name="stick_breaking",
    )(p, p, p, g_heads)


def _dsa_kernel(q_ref, k_ref, v_ref, qi_ref, ki_ref, w_ref, g_ref, o_ref, sc_ref, sc16_ref, s_ref, p_ref, *, slopes):
    tq = q_ref.shape[1]
    tk = sc_ref.shape[1]
    i = pl.program_id(1)
    n_blocks = ((i + 1) * tq + tk - 1) // tk
    qpos = i * tq + lax.broadcasted_iota(jnp.int32, (1, tq), 1)
    key_row = lax.broadcasted_iota(jnp.int32, (tk, 1), 0)
    neg_inf = float("-inf")

    w_t = w_ref[0].T * (IDX_HEADS ** -0.5 * IDX_DIM ** -0.5)
    w_rows = [w_t[h:h + 1, :] for h in range(IDX_HEADS)]
    qi = qi_ref[0]

    def top_half(x):
        bits = lax.bitcast_convert_type(x, jnp.int32) & jnp.int32(-65536)
        return lax.bitcast_convert_type(bits, F32).astype(BF16)

    def score_body(j, carry, slot):
        start = pl.multiple_of(j * tk, tk)
        ki = ki_ref[0, pl.ds(start, tk), 0:IDX_DIM]
        sc = jnp.zeros((tk, tq), F32)
        for h in range(IDX_HEADS):
            x = lax.dot_general(ki, qi[:, h * IDX_DIM:(h + 1) * IDX_DIM], _NT, preferred_element_type=F32)
            sc = sc + w_rows[h] * jnp.maximum(x, 0.0)
        sc = jnp.where((start + key_row) <= qpos, sc + 0.0, neg_inf)
        sc_ref[j] = sc
        sc16_ref[j] = top_half(sc)
        return carry

    _paired_loop(n_blocks, score_body, 0)

    def count(pred):
        def add_block(j, cnt, slot):
            c = jnp.where(pred(sc_ref[j]), 1.0, 0.0)
            return cnt + jnp.sum(c.reshape(tk // 8, 8, tq), axis=0)

        cnt = _paired_loop(n_blocks, add_block, jnp.zeros((8, tq), F32))
        return jnp.sum(cnt, axis=0, keepdims=True)

    int_min = jnp.int32(-2 ** 31)

    def ordered_to_float(u):
        key = u ^ int_min
        bits = jnp.where(key >= 0, key, key ^ jnp.int32(0x7FFFFFFF))
        return lax.bitcast_convert_type(bits, F32)

    def count_top_half(cand16):
        def add_block(j, cnt, slot):
            c = jnp.where(sc16_ref[j] >= cand16, jnp.ones((), BF16), jnp.zeros((), BF16))
            for g in range(tk // 16):
                cnt = cnt + c[g * 16:(g + 1) * 16]
            return cnt

        cnt = _paired_loop(n_blocks, add_block, jnp.zeros((16, tq), BF16))
        return jnp.sum(cnt.astype(F32), axis=0, keepdims=True)

    def search_body(step, carry, half):
        prefix, n_at_prefix = carry
        cand = prefix | jnp.left_shift(jnp.int32(1), 31 - step)
        cand_f = ordered_to_float(cand)
        n_ge = count_top_half(top_half(cand_f)) if half else count(lambda s: s >= cand_f)
        keep = n_ge >= DSA_TOPK
        return jnp.where(keep, cand, prefix), jnp.where(keep, n_ge, n_at_prefix)

    carry = (jnp.zeros((1, tq), jnp.int32), jnp.zeros((1, tq), F32))
    carry = lax.fori_loop(0, 16, functools.partial(search_body, half=True), carry)
    prefix, n_ge_tau = lax.fori_loop(16, 32, functools.partial(search_body, half=False), carry)
    keep_all = (qpos + 1) <= DSA_TOPK
    tau = jnp.where(keep_all, float(np.finfo(np.float32).min), ordered_to_float(prefix))
    no_ties = jnp.min(jnp.where(jnp.logical_or(keep_all, n_ge_tau == DSA_TOPK), 1.0, 0.0)) > 0.5

    earlier = (lax.broadcasted_iota(jnp.int32, (tk, tk), 1) < lax.broadcasted_iota(jnp.int32, (tk, tk), 0))
    earlier = jnp.where(earlier, 1.0, 0.0).astype(BF16)
    qs = [q_ref[0, :, _head_cols(h)] * ATTN_SCALE for h in range(HEADS_PER_MIXER)]
    key_bias = [slopes[h] * lax.broadcasted_iota(jnp.int32, (tk, tq), 0).astype(F32)
                for h in range(HEADS_PER_MIXER)]

    def attn_body(j, carry, slot, n_ties=None):
        ties_seen, stats = carry
        s_slot = [s_ref.at[slot * HEADS_PER_MIXER + h] for h in range(HEADS_PER_MIXER)]
        p_slot = [p_ref.at[slot * HEADS_PER_MIXER + h] for h in range(HEADS_PER_MIXER)]
        start = pl.multiple_of(j * tk, tk)
        sc = sc_ref[j]
        if n_ties is None:
            sel = sc >= tau
        else:
            tie = jnp.where(sc == tau, 1.0, 0.0)
            rank = jnp.dot(earlier, tie.astype(BF16), preferred_element_type=F32) + ties_seen
            sel = jnp.where(sc > tau, 1.0, jnp.where(rank < n_ties, tie, 0.0)) > 0.5
            ties_seen = ties_seen + jnp.sum(tie, axis=0, keepdims=True)
        block_pos = (j * tk - i * tq).astype(F32)
        for h in range(HEADS_PER_MIXER):
            k = k_ref[0, pl.ds(start, tk), _head_cols(h)]
            s = lax.dot_general(k, qs[h], _NT, preferred_element_type=F32) + key_bias[h]
            s_slot[h][...] = jnp.where(sel, s, MASKED)
        locals_ = []
        for h in range(HEADS_PER_MIXER):
            s = s_slot[h][...]
            m_loc = jnp.max(s, axis=0, keepdims=True)
            p = jnp.exp(s - m_loc)
            locals_.append((m_loc, jnp.sum(p, axis=0, keepdims=True)))
            p_slot[h][...] = p.astype(BF16)
        new_stats = []
        for h in range(HEADS_PER_MIXER):
            m, l, acc = stats[h]
            m_loc, l_blk = locals_[h]
            v = v_ref[0, pl.ds(start, tk), _head_cols(h)]
            pv_blk = lax.dot_general(v, p_slot[h][...], _TN, preferred_element_type=F32)
            m_blk = m_loc + slopes[h] * block_pos
            m_new = jnp.maximum(m, m_blk)
            w_old = jnp.exp(m - m_new)
            w_blk = jnp.where(m_loc > 0.5 * MASKED, jnp.exp(m_blk - m_new), 0.0)
            new_stats.append((m_new, w_old * l + w_blk * l_blk, w_old * acc + w_blk * pv_blk))
        return ties_seen, tuple(new_stats)

    init = tuple((jnp.full((1, tq), MASKED, F32), jnp.zeros((1, tq), F32), jnp.zeros((HEAD_DIM, tq), F32))
                 for _ in range(HEADS_PER_MIXER))
    start_carry = (jnp.zeros((1, tq), F32), init)

    def attend_without_ties():
        return _paired_loop(n_blocks, attn_body, start_carry)[1]

    def attend_with_ties():
        n_gt = count(lambda s: s > tau)
        n_ties = jnp.where(keep_all, 1e9, DSA_TOPK - n_gt)
        return _paired_loop(n_blocks, functools.partial(attn_body, n_ties=n_ties), start_carry)[1]

    stats = lax.cond(no_ties, attend_without_ties, attend_with_ties)
    outs = []
    for h in range(HEADS_PER_MIXER):
        _, l, acc = stats[h]
        o = acc / l
        outs.append(o * lax.rsqrt(jnp.mean(o * o, axis=0, keepdims=True) + NORM_EPS) * g_ref[:, h:h + 1])
    o_ref[0] = jnp.concatenate(outs, axis=0).T.astype(o_ref.dtype)


def _dsa_mixer(p, p_idx, p_w, g_heads):
    b, seq, _ = p.shape
    tq = min(2 * Q_TILE, seq)
    tk = tq
    assert seq // 16 <= 256, "the packed-bf16 count accumulators are exact only up to 256 adds"
    qi_width = IDX_HEADS * IDX_DIM
    return pl.pallas_call(
        functools.partial(_dsa_kernel, slopes=_alibi_slopes(0)),
        grid=(b, seq // tq),
        in_specs=[pl.BlockSpec((1, tq, MIXER_WIDTH), lambda bi, i: (bi, i, 0)),
                  pl.BlockSpec((1, seq, MIXER_WIDTH), lambda bi, i: (bi, 0, 1)),
                  pl.BlockSpec((1, seq, MIXER_WIDTH), lambda bi, i: (bi, 0, 2)),
                  pl.BlockSpec((1, tq, qi_width), lambda bi, i: (bi, i, 0)),
                  pl.BlockSpec((1, seq, LANES), lambda bi, i: (bi, 0, qi_width // LANES)),
                  pl.BlockSpec((1, tq, LANES), lambda bi, i: (bi, i, 0)),
                  pl.BlockSpec((HEAD_DIM, HEADS_PER_MIXER), lambda bi, i: (0, 0))],
        out_specs=pl.BlockSpec((1, tq, MIXER_WIDTH), lambda bi, i: (bi, i, 0)),
        out_shape=jax.ShapeDtypeStruct((b, seq, MIXER_WIDTH), BF16),
        scratch_shapes=[pltpu.VMEM((seq // tk, tk, tq), F32),
                        pltpu.VMEM((seq // tk, tk, tq), BF16),
                        pltpu.VMEM((2 * HEADS_PER_MIXER, tk, tq), F32),
                        pltpu.VMEM((2 * HEADS_PER_MIXER, tk, tq), BF16)],
        compiler_params=_params("arbitrary", "arbitrary"),
        name="dsa",
    )(p, p, p, p_idx, p_idx, p_w, g_heads.T)


def _band_kernel(q_ref, kp_ref, kc_ref, vp_ref, vc_ref, o_ref, lse_ref, *, dilation, slopes):
    tq = q_ref.shape[1]
    ui = pl.program_id(2)
    heads = range(HEADS_PER_MIXER)
    u_q = jnp.concatenate([ui * tq + lax.broadcasted_iota(jnp.int32, (tq, 1), 0)] * HEADS_PER_MIXER, axis=0)
    u_k = ui * tq - DILATED_STEPS + lax.broadcasted_iota(jnp.int32, (1, DILATED_STEPS + tq), 1)
    steps = u_q - u_k
    valid = jnp.logical_and(jnp.logical_and(steps >= 0, steps <= DILATED_STEPS), u_k >= 0)
    slope_col = jnp.concatenate([jnp.full((tq, 1), slopes[h], F32) for h in heads], axis=0)
    s = jnp.concatenate(
        [lax.dot_general(q_ref[0, :, _head_cols(h)],
                         jnp.concatenate([kp_ref[0, :, _head_cols(h)], kc_ref[0, :, _head_cols(h)]], axis=0),
                         _NT, preferred_element_type=F32) for h in heads], axis=0)
    s = jnp.where(valid, s * ATTN_SCALE - slope_col * (steps * dilation).astype(F32), MASKED)
    m = jnp.max(s, axis=1, keepdims=True)
    e = jnp.exp(s - m)
    den = jnp.sum(e, axis=1, keepdims=True)
    p = e.astype(BF16)
    lse = m + jnp.log(den)
    lane = lax.broadcasted_iota(jnp.int32, (1, LANES), 1)
    lse_all = jnp.zeros((tq, LANES), F32)
    for h in heads:
        rows = slice(h * tq, (h + 1) * tq)
        v = jnp.concatenate([vp_ref[0, :, _head_cols(h)], vc_ref[0, :, _head_cols(h)]], axis=0)
        o_ref[0, :, _head_cols(h)] = jnp.dot(p[rows], v, preferred_element_type=F32) / den[rows]
        lse_all = lse_all + jnp.where(lane == h, lse[rows], 0.0)
    lse_ref[0] = lse_all


def _dilated_branch(view, dilation, slopes):
    b, length, _ = view.shape
    classes = dilation
    tq = min(2 * Q_TILE, length)
    back = tq // DILATED_STEPS
    spec = lambda part, prev: (
        pl.BlockSpec((1, DILATED_STEPS, MIXER_WIDTH),
                     lambda bi, c, ui: (bi, jnp.maximum(ui * back - 1, 0), c * 3 + part)) if prev
        else pl.BlockSpec((1, tq, MIXER_WIDTH), lambda bi, c, ui: (bi, ui, c * 3 + part)))
    out, lse = pl.pallas_call(
        functools.partial(_band_kernel, dilation=dilation, slopes=slopes),
        grid=(b, classes, length // tq),
        in_specs=[spec(0, False), spec(1, True), spec(1, False), spec(2, True), spec(2, False)],
        out_specs=[pl.BlockSpec((1, tq, MIXER_WIDTH), lambda bi, c, ui: (bi, ui, c)),
                   pl.BlockSpec((1, tq, LANES), lambda bi, c, ui: (bi, ui, c))],
        out_shape=[jax.ShapeDtypeStruct((b, length, classes * MIXER_WIDTH), F32),
                   jax.ShapeDtypeStruct((b, length, classes * LANES), F32)],
        compiler_params=_params("arbitrary", "arbitrary", "arbitrary"),
        name=f"dilated_r{dilation}",
    )(view, view, view, view, view)
    return out, lse


def _dilated_merge_kernel(o1, o2, o3, l1, l2, l3, g_ref, o_ref, *scratch):
    tq = o_ref.shape[1]
    chunks = MIXER_WIDTH // LANES
    outs = [[o1[0, :, s * LANES:(s + 1) * LANES] for s in range(chunks)]]
    lses = [l1[0]]
    for r, o_view, l_view, o_nat, l_nat in zip(DILATIONS[1:], (o2, o3), (l2, l3), scratch[0::2], scratch[1::2]):
        for c in range(r):
            for s in range(chunks):
                col = c * MIXER_WIDTH + s * LANES
                o_nat[pl.ds(s * tq + c, tq // r, stride=r), :] = o_view[0, :, col:col + LANES]
            l_nat[pl.ds(c, tq // r, stride=r), :] = l_view[0, :, c * LANES:(c + 1) * LANES]
        outs.append([o_nat[s * tq:(s + 1) * tq, :] for s in range(chunks)])
        lses.append(l_nat[...])
    heads_per_chunk = LANES // HEAD_DIM
    for h in range(HEADS_PER_MIXER):
        sub = slice((h % heads_per_chunk) * HEAD_DIM, (h % heads_per_chunk + 1) * HEAD_DIM)
        lse_h = [l[:, h:h + 1] for l in lses]
        top = functools.reduce(jnp.maximum, lse_h)
        wts = [jnp.exp(l - top) for l in lse_h]
        mixed = sum(w * o[h // heads_per_chunk][:, sub] for w, o in zip(wts, outs)) / sum(wts)
        o_ref[0, :, _head_cols(h)] = _head_norm(mixed, g_ref[h:h + 1, :]).astype(o_ref.dtype)


def _dilated_mixer(views, g_heads):
    b, seq, _ = views[0].shape
    slopes = _alibi_slopes(1)
    branches = [_dilated_branch(v, r, slopes) for v, r in zip(views, DILATIONS)]
    tq = min(ROW_TILE, seq)
    o_specs = [pl.BlockSpec((1, tq // r, r * MIXER_WIDTH), lambda bi, i: (bi, i, 0)) for r in DILATIONS]
    l_specs = [pl.BlockSpec((1, tq // r, r * LANES), lambda bi, i: (bi, i, 0)) for r in DILATIONS]
    scratch = []
    for _ in DILATIONS[1:]:
        scratch += [pltpu.VMEM((tq * (MIXER_WIDTH // LANES), LANES), F32), pltpu.VMEM((tq, LANES), F32)]
    return pl.pallas_call(
        _dilated_merge_kernel,
        grid=(b, seq // tq),
        in_specs=o_specs + l_specs + [pl.BlockSpec((HEADS_PER_MIXER, HEAD_DIM), lambda bi, i: (0, 0))],
        out_specs=pl.BlockSpec((1, tq, MIXER_WIDTH), lambda bi, i: (bi, i, 0)),
        out_shape=jax.ShapeDtypeStruct((b, seq, MIXER_WIDTH), BF16),
        scratch_shapes=scratch,
        compiler_params=_params("arbitrary", "arbitrary"),
        name="dilated_merge",
    )(*[o for o, _ in branches], *[l for _, l in branches], g_heads)


def _moba_kernel(q_ref, k_ref, v_ref, g_ref, o_ref, kmean_ref, chosen_ref, s_ref, p_ref, *, slopes):
    tq = q_ref.shape[1]
    n_kv = k_ref.shape[1] // MOBA_BLOCK
    own = pl.program_id(1)

    @pl.when(own == 0)
    def _():
        kmean_ref[...] = jnp.zeros_like(kmean_ref)
        for n in range(n_kv):
            blk = k_ref[0, n * MOBA_BLOCK:(n + 1) * MOBA_BLOCK, :].astype(F32)
            kmean_ref[n:n + 1, :] = jnp.sum(blk, axis=0, keepdims=True) * (1.0 / MOBA_BLOCK)

    key_row = lax.broadcasted_iota(jnp.int32, (MOBA_BLOCK, 1), 0)
    q_lane = lax.broadcasted_iota(jnp.int32, (1, tq), 1)
    blk_row = lax.broadcasted_iota(jnp.int32, (LANES, 1), 0)
    blk_f = blk_row.astype(F32)
    neg_inf = float("-inf")

    qs = []
    for h in range(HEADS_PER_MIXER):
        q = q_ref[0, :, _head_cols(h)]
        gate = lax.dot_general(kmean_ref[:, _head_cols(h)], q.astype(F32), _NT, preferred_element_type=F32,
                               precision=lax.Precision.HIGHEST)
        gate = jnp.where(blk_row < own, gate, neg_inf)
        picks = jnp.zeros((LANES, tq), F32)
        for _ in range(MOBA_TOPK):
            top = jnp.max(gate, axis=0, keepdims=True)
            is_top = jnp.logical_and(gate == top, top > neg_inf)
            first = jnp.min(jnp.where(is_top, blk_f, float(LANES)), axis=0, keepdims=True)
            pick = blk_f == first
            picks = jnp.where(pick, 1.0, picks)
            gate = jnp.where(pick, neg_inf, gate)
        chosen_ref[h] = picks
        qs.append(q * ATTN_SCALE)

    key_bias = [slopes[h] * lax.broadcasted_iota(jnp.int32, (MOBA_BLOCK, tq), 0).astype(F32)
                for h in range(HEADS_PER_MIXER)]

    def block_softmax(start, keep=None, slot=0):
        s_slot = [s_ref.at[slot * HEADS_PER_MIXER + h] for h in range(HEADS_PER_MIXER)]
        p_slot = [p_ref.at[slot * HEADS_PER_MIXER + h] for h in range(HEADS_PER_MIXER)]
        for h in range(HEADS_PER_MIXER):
            k = k_ref[0, pl.ds(start, MOBA_BLOCK), _head_cols(h)]
            s = lax.dot_general(k, qs[h], _NT, preferred_element_type=F32) + key_bias[h]
            s_slot[h][...] = s if keep is None else jnp.where(keep, s, MASKED)
        pieces = []
        for h in range(HEADS_PER_MIXER):
            s = s_slot[h][...]
            m_loc = jnp.max(s, axis=0, keepdims=True)
            p = jnp.exp(s - m_loc)
            pieces.append((m_loc, jnp.sum(p, axis=0, keepdims=True)))
            p_slot[h][...] = p.astype(BF16)
        out = []
        for h in range(HEADS_PER_MIXER):
            v = v_ref[0, pl.ds(start, MOBA_BLOCK), _head_cols(h)]
            out.append(pieces[h] + (lax.dot_general(v, p_slot[h][...], _TN, preferred_element_type=F32),))
        return out

    own_start = pl.multiple_of(own * MOBA_BLOCK, MOBA_BLOCK)
    stats = tuple(block_softmax(own_start, keep=key_row <= q_lane))

    def body(j, stats, slot):
        start = pl.multiple_of(j * MOBA_BLOCK, MOBA_BLOCK)
        block_pos = ((j - own) * MOBA_BLOCK).astype(F32)
        merged = []
        for h, (m_loc, l_blk, pv_blk) in enumerate(block_softmax(start, slot=slot)):
            picked = chosen_ref[h, pl.ds(j, 1), :] > 0.5
            m_blk = jnp.where(picked, m_loc + slopes[h] * block_pos, MASKED)
            m, l, acc = stats[h]
            m_new = jnp.maximum(m, m_blk)
            w_old = jnp.exp(m - m_new)
            w_blk = jnp.exp(m_blk - m_new)
            merged.append((m_new, w_old * l + w_blk * l_blk, w_old * acc + w_blk * pv_blk))
        return tuple(merged)

    stats = _paired_loop(own, body, stats)
    outs = []
    for h in range(HEADS_PER_MIXER):
        _, l, acc = stats[h]
        o = acc / l
        outs.append(o * lax.rsqrt(jnp.mean(o * o, axis=0, keepdims=True) + NORM_EPS) * g_ref[:, h:h + 1])
    o_ref[0] = jnp.concatenate(outs, axis=0).T.astype(o_ref.dtype)


def _moba_mixer(p, g_heads):
    b, seq, _ = p.shape
    tq = MOBA_BLOCK
    return pl.pallas_call(
        functools.partial(_moba_kernel, slopes=_alibi_slopes(2)),
        grid=(b, seq // tq),
        in_specs=[pl.BlockSpec((1, tq, MIXER_WIDTH), lambda bi, i: (bi, i, 0)),
                  pl.BlockSpec((1, seq, MIXER_WIDTH), lambda bi, i: (bi, 0, 1)),
                  pl.BlockSpec((1, seq, MIXER_WIDTH), lambda bi, i: (bi, 0, 2)),
                  pl.BlockSpec((HEAD_DIM, HEADS_PER_MIXER), lambda bi, i: (0, 0))],
        out_specs=pl.BlockSpec((1, tq, MIXER_WIDTH), lambda bi, i: (bi, i, 0)),
        out_shape=jax.ShapeDtypeStruct((b, seq, MIXER_WIDTH), BF16),
        scratch_shapes=[pltpu.VMEM((LANES, MIXER_WIDTH), F32),
                        pltpu.VMEM((HEADS_PER_MIXER, LANES, tq), F32),
                        pltpu.VMEM((2 * HEADS_PER_MIXER, MOBA_BLOCK, tq), F32),
                        pltpu.VMEM((2 * HEADS_PER_MIXER, MOBA_BLOCK, tq), BF16)],
        compiler_params=_params("arbitrary", "arbitrary"),
        name="moba",
    )(p, p, p, g_heads.T)


def _out_proj_kernel(oa, ob, oc, od, w_ref, x_ref, mod_ref, o_ref):
    acc = jnp.zeros(x_ref.shape, F32)
    for m, o in enumerate((oa, ob, oc, od)):
        acc = acc + jnp.dot(o[...], w_ref[m * MIXER_WIDTH:(m + 1) * MIXER_WIDTH, :], preferred_element_type=F32)
    o_ref[...] = x_ref[...] + mod_ref[0, 2:3, :] * acc


def _out_projection(mixed, w_out, x2, mod, seq):
    t, d = x2.shape
    tm = min(ROW_TILE, seq)
    per_batch = seq // tm
    o_spec = pl.BlockSpec((tm, MIXER_WIDTH), lambda i: (i, 0))
    return pl.pallas_call(
        _out_proj_kernel,
        grid=(t // tm,),
        in_specs=[o_spec] * 4 + [pl.BlockSpec(w_out.shape, lambda i: (0, 0)),
                                 pl.BlockSpec((tm, d), lambda i: (i, 0)),
                                 pl.BlockSpec((1, 6, d), lambda i: (i // per_batch, 0, 0))],
        out_specs=pl.BlockSpec((tm, d), lambda i: (i, 0)),
        out_shape=jax.ShapeDtypeStruct((t, d), F32),
        compiler_params=_params("arbitrary"),
        name="out_projection",
    )(*[o.reshape(t, MIXER_WIDTH) for o in mixed], w_out, x2, mod)


def _ffn_kernel(x_ref, mod_ref, g_ref, wg_ref, wu_ref, wd_ref, o_ref, h_ref, acc_ref):
    f = pl.program_id(1)

    @pl.when(f == 0)
    def _():
        h_ref[...] = _modulated_norm(x_ref[...], g_ref[...], mod_ref[0, 3:4, :], mod_ref[0, 4:5, :]).astype(BF16)
        acc_ref[...] = jnp.zeros_like(acc_ref)

    h = h_ref[...]
    gate = jnp.dot(h, wg_ref[...], preferred_element_type=F32)
    up = jnp.dot(h, wu_ref[...], preferred_element_type=F32)
    act = (gate / (1.0 + jnp.exp(-gate)) * up).astype(BF16)
    acc_ref[...] += jnp.dot(act, wd_ref[...], preferred_element_type=F32)

    @pl.when(f == pl.num_programs(1) - 1)
    def _():
        o_ref[...] = x_ref[...] + mod_ref[0, 5:6, :] * acc_ref[...]


def _dense_ffn(x2, mod, g_ffn, wg, wu, wd, seq):
    t, d = x2.shape
    d_ff = wg.shape[1]
    tm = min(ROW_TILE, seq)
    tf = FFN_TILE if d_ff % FFN_TILE == 0 else d_ff
    per_batch = seq // tm
    return pl.pallas_call(
        _ffn_kernel,
        grid=(t // tm, d_ff // tf),
        in_specs=[pl.BlockSpec((tm, d), lambda i, f: (i, 0)),
                  pl.BlockSpec((1, 6, d), lambda i, f: (i // per_batch, 0, 0)),
                  pl.BlockSpec((1, d), lambda i, f: (0, 0)),
                  pl.BlockSpec((d, tf), lambda i, f: (0, f)),
                  pl.BlockSpec((d, tf), lambda i, f: (0, f)),
                  pl.BlockSpec((tf, d), lambda i, f: (f, 0))],
        out_specs=pl.BlockSpec((tm, d), lambda i, f: (i, 0)),
        out_shape=jax.ShapeDtypeStruct((t, d), F32),
        scratch_shapes=[pltpu.VMEM((tm, d), BF16), pltpu.VMEM((tm, d), F32)],
        compiler_params=_params("arbitrary", "arbitrary"),
        name="dense_ffn",
    )(x2, mod, g_ffn.reshape(1, d), wg.astype(BF16), wu.astype(BF16), wd.astype(BF16))


def _store_row_tiles(dst_ref, value):
    rows, d = value.shape
    chunks = d // LANES
    for s in range(chunks):
        dst_ref[pl.ds(s, rows, stride=chunks), :] = value[:, s * LANES:(s + 1) * LANES]


def _load_row_tile_chunk(src_ref, s, rows, chunks):
    return src_ref[pl.ds(s, rows, stride=chunks), :]


def _router_kernel(x_ref, mod_ref, g_ref, wr_ref, h_ref, logit_ref):
    h = _modulated_norm(x_ref[...], g_ref[...], mod_ref[0, 3:4, :], mod_ref[0, 4:5, :])
    _store_row_tiles(h_ref, h)
    logit_ref[...] = jnp.dot(h, wr_ref[...], preferred_element_type=F32, precision=lax.Precision.HIGHEST)


def _router(x2, mod, g_ffn, w_router, seq):
    t, d = x2.shape
    tm = min(ROW_TILE, seq)
    per_batch = seq // tm
    wr = jnp.zeros((d, LANES), F32).at[:, :N_EXPERTS].set(w_router.astype(F32))
    return pl.pallas_call(
        _router_kernel,
        grid=(t // tm,),
        in_specs=[pl.BlockSpec((tm, d), lambda i: (i, 0)),
                  pl.BlockSpec((1, 6, d), lambda i: (i // per_batch, 0, 0)),
                  pl.BlockSpec((1, d), lambda i: (0, 0)),
                  pl.BlockSpec((d, LANES), lambda i: (0, 0))],
        out_specs=[pl.BlockSpec((tm * (d // LANES), LANES), lambda i: (i, 0)),
                   pl.BlockSpec((tm, LANES), lambda i: (i, 0))],
        out_shape=[jax.ShapeDtypeStruct((t * (d // LANES), LANES), F32), jax.ShapeDtypeStruct((t, LANES), F32)],
        compiler_params=_params("arbitrary"),
        name="moe_router",
    )(x2, mod, g_ffn.reshape(1, d), wr)


def _row_copy(src_hbm, row, dst_ref, r, sem, chunks):
    src = src_hbm.at[pl.ds(pl.multiple_of(row * chunks, chunks), chunks), :]
    return pltpu.make_async_copy(src, dst_ref.at[pl.ds(pl.multiple_of(r * chunks, chunks), chunks), :], sem)


def _expert_kernel(be_ref, used_ref, cur_idx_ref, nxt_idx_ref, h_hbm, wg_ref, wu_ref, wd_ref, o_ref,
                   xin_ref, xb_ref, acc_ref, sem):
    m = pl.program_id(0)
    f = pl.program_id(1)
    n_used = used_ref[0]
    live = m < n_used
    rows, d = xb_ref.shape
    chunks = d // LANES

    def row_copies(idx_ref, slot, start):
        def body(pair, c):
            for lane in range(2):
                r = 2 * pair + lane
                copy = _row_copy(h_hbm, idx_ref[0, 0, r], xin_ref.at[slot], r, sem.at[slot, lane], chunks)
                if start:
                    copy.start(priority=lane)
                else:
                    copy.wait()
            return c
        lax.fori_loop(0, rows // 2, body, 0)

    @pl.when(jnp.logical_and(live, f == 0))
    def _():
        @pl.when(m == 0)
        def _():
            row_copies(cur_idx_ref, 0, start=True)

        for slot in range(2):
            @pl.when(m % 2 == slot)
            def _(slot=slot):
                row_copies(cur_idx_ref, slot, start=False)
                for s in range(chunks):
                    xb_ref[:, s * LANES:(s + 1) * LANES] = _load_row_tile_chunk(
                        xin_ref.at[slot], s, rows, chunks).astype(BF16)

                @pl.when(m + 1 < n_used)
                def _():
                    row_copies(nxt_idx_ref, 1 - slot, start=True)

        acc_ref[...] = jnp.zeros_like(acc_ref)

    @pl.when(live)
    def _():
        h = xb_ref[...]
        gate = jnp.dot(h, wg_ref[0], preferred_element_type=F32)
        up = jnp.dot(h, wu_ref[0], preferred_element_type=F32)
        act = (gate / (1.0 + jnp.exp(-gate)) * up).astype(BF16)
        acc_ref[...] += jnp.dot(act, wd_ref[0], preferred_element_type=F32)

    @pl.when(f == pl.num_programs(1) - 1)
    def _():
        _store_row_tiles(o_ref, jnp.where(live, acc_ref[...], 0.0))


def _expert_ffn(h_rows, slot_tok, block_expert, n_used, wg, wu, wd):
    d = wg.shape[1]
    chunks = d // LANES
    n_slots = slot_tok.shape[0]
    d_ff = wg.shape[2]
    tm = EXPERT_ROWS
    tf = EXPERT_FF_TILE if d_ff % EXPERT_FF_TILE == 0 else d_ff
    nf = d_ff // tf
    nm = n_slots // tm

    def fcol(m, f, used):
        return jnp.where(m < used[0], f, nf - 1)

    idx_spec = lambda shift: pl.BlockSpec((1, 1, tm), lambda m, f, be, used: (jnp.minimum(m + shift, nm - 1), 0, 0),
                                          memory_space=pltpu.SMEM)
    grid_spec = pltpu.PrefetchScalarGridSpec(
        num_scalar_prefetch=2,
        grid=(nm, nf),
        in_specs=[idx_spec(0), idx_spec(1), pl.BlockSpec(memory_space=pl.ANY),
                  pl.BlockSpec((1, d, tf), lambda m, f, be, used: (be[m], 0, fcol(m, f, used))),
                  pl.BlockSpec((1, d, tf), lambda m, f, be, used: (be[m], 0, fcol(m, f, used))),
                  pl.BlockSpec((1, tf, d), lambda m, f, be, used: (be[m], fcol(m, f, used), 0))],
        out_specs=pl.BlockSpec((tm * chunks, LANES), lambda m, f, be, used: (m, 0)),
        scratch_shapes=[pltpu.VMEM((2, tm * chunks, LANES), F32), pltpu.VMEM((tm, d), BF16),
                        pltpu.VMEM((tm, d), F32), pltpu.SemaphoreType.DMA((2, 2))])
    idx = slot_tok.reshape(nm, 1, tm)
    return pl.pallas_call(
        _expert_kernel,
        grid_spec=grid_spec,
        out_shape=jax.ShapeDtypeStruct((n_slots * chunks, LANES), F32),
        compiler_params=_params("arbitrary", "arbitrary"),
        name="expert_ffn",
    )(block_expert, n_used, idx, idx, h_rows, wg.astype(BF16), wu.astype(BF16), wd.astype(BF16))


def _combine_kernel(d0_ref, d1_ref, ys_hbm, x_ref, gates_ref, mod_ref, o_ref, y0_ref, y1_ref, sem):
    rows, d = o_ref.shape
    chunks = d // LANES

    def start(r, c):
        _row_copy(ys_hbm, d0_ref[0, 0, r], y0_ref, r, sem.at[0], chunks).start(priority=0)
        _row_copy(ys_hbm, d1_ref[0, 0, r], y1_ref, r, sem.at[1], chunks).start(priority=1)
        return c

    def wait(r, c):
        _row_copy(ys_hbm, d0_ref[0, 0, r], y0_ref, r, sem.at[0], chunks).wait()
        _row_copy(ys_hbm, d1_ref[0, 0, r], y1_ref, r, sem.at[1], chunks).wait()
        return c

    lax.fori_loop(0, rows, start, 0)
    lax.fori_loop(0, rows, wait, 0)
    gates = gates_ref[...]
    for s in range(chunks):
        cols = slice(s * LANES, (s + 1) * LANES)
        y = (_load_row_tile_chunk(y0_ref, s, rows, chunks) * gates[:, 0:1]
             + _load_row_tile_chunk(y1_ref, s, rows, chunks) * gates[:, 1:2])
        o_ref[:, cols] = x_ref[:, cols] + mod_ref[0, 5:6, cols] * y


def _moe_combine(ys, dest0, dest1, gates, x2, mod, seq):
    t, d = x2.shape
    tm = min(GATHER_ROWS, seq)
    steps = t // tm
    per_batch = seq // tm
    idx_spec = pl.BlockSpec((1, 1, tm), lambda i: (i, 0, 0), memory_space=pltpu.SMEM)
    return pl.pallas_call(
        _combine_kernel,
        grid=(steps,),
        in_specs=[idx_spec, idx_spec, pl.BlockSpec(memory_space=pl.ANY),
                  pl.BlockSpec((tm, d), lambda i: (i, 0)),
                  pl.BlockSpec((tm, TOP_K_EXPERTS), lambda i: (i, 0)),
                  pl.BlockSpec((1, 6, d), lambda i: (i // per_batch, 0, 0))],
        out_specs=pl.BlockSpec((tm, d), lambda i: (i, 0)),
        out_shape=jax.ShapeDtypeStruct((t, d), F32),
        scratch_shapes=[pltpu.VMEM((tm * (d // LANES), LANES), F32), pltpu.VMEM((tm * (d // LANES), LANES), F32),
                        pltpu.SemaphoreType.DMA((2,))],
        compiler_params=_params("arbitrary"),
        name="moe_combine",
    )(dest0.reshape(steps, 1, tm), dest1.reshape(steps, 1, tm), ys, x2, gates, mod)


def _moe_ffn(x2, mod, g_ffn, w_router, wg, wu, wd, seq):
    t, d = x2.shape
    h, logits = _router(x2, mod, g_ffn, w_router, seq)
    top_val, top_idx = lax.top_k(logits[:, :N_EXPERTS], TOP_K_EXPERTS)
    gates = jax.nn.softmax(top_val, axis=-1)

    n_assign = t * TOP_K_EXPERTS
    flat_e = top_idx.reshape(-1).astype(jnp.int32)
    onehot = (flat_e[:, None] == jnp.arange(N_EXPERTS, dtype=jnp.int32)[None, :]).astype(jnp.int32)
    rank = jnp.take_along_axis(jnp.cumsum(onehot, axis=0), flat_e[:, None], axis=1)[:, 0] - 1
    counts = jnp.sum(onehot, axis=0)
    padded = (counts + EXPERT_ROWS - 1) // EXPERT_ROWS * EXPERT_ROWS
    pad_end = jnp.cumsum(padded)
    dest = (pad_end - padded)[flat_e] + rank
    n_slots = (n_assign // EXPERT_ROWS + N_EXPERTS) * EXPERT_ROWS
    n_blocks = n_slots // EXPERT_ROWS
    slot_tok = jnp.zeros((n_slots,), jnp.int32).at[dest].set(jnp.arange(n_assign, dtype=jnp.int32) // TOP_K_EXPERTS)
    block_start = jnp.arange(n_blocks, dtype=jnp.int32) * EXPERT_ROWS
    block_expert = jnp.minimum(jnp.searchsorted(pad_end, block_start, side="right"), N_EXPERTS - 1).astype(jnp.int32)
    n_used = (pad_end[-1:] // EXPERT_ROWS).astype(jnp.int32)

    ys = _expert_ffn(h, slot_tok, block_expert, n_used, wg, wu, wd)
    dest2 = dest.reshape(t, TOP_K_EXPERTS)
    return _moe_combine(ys, dest2[:, 0], dest2[:, 1], gates, x2, mod, seq)


def _final_norm_kernel(x_ref, g_ref, o_ref):
    x = x_ref[...]
    o_ref[...] = x * lax.rsqrt(jnp.mean(x * x, axis=-1, keepdims=True) + NORM_EPS) * g_ref[...]


def _final_norm(x2, g_final):
    t, d = x2.shape
    tm = min(ROW_TILE, t)
    return pl.pallas_call(
        _final_norm_kernel,
        grid=(t // tm,),
        in_specs=[pl.BlockSpec((tm, d), lambda i: (i, 0)), pl.BlockSpec((1, d), lambda i: (0, 0))],
        out_specs=pl.BlockSpec((tm, d), lambda i: (i, 0)),
        out_shape=jax.ShapeDtypeStruct((t, d), F32),
        compiler_params=_params("arbitrary"),
        name="final_norm",
    )(x2, g_final.reshape(1, d))


def _token_mixer(x2, mod, g_mix, w_in, g_heads, w_out, batch, seq):
    pa, pb, pc, pd, p_idx, p_w, *pc_views = _in_projection(x2, mod, g_mix, _pack_w_in(w_in), seq)
    shape3 = lambda a: a.reshape(batch, -1, a.shape[-1])
    gh = g_heads.reshape(4, HEADS_PER_MIXER, HEAD_DIM)
    mixed = (_stick_breaking_mixer(shape3(pa), gh[0]),
             _dsa_mixer(shape3(pb), shape3(p_idx), shape3(p_w), gh[1]),
             _dilated_mixer([shape3(v) for v in [pc] + pc_views], gh[2]),
             _moba_mixer(shape3(pd), gh[3]))
    return _out_projection(mixed, w_out.astype(BF16), x2, mod, seq)


def kernel(x, c, w_ada, b_ada, g_mix, w_in, g_heads, w_out, g_ffn, w_ff_gate, w_ff_up, w_ff_down, w_router, w_exp_gate, w_exp_up, w_exp_down, g_final):
    batch, seq, d = x.shape
    depth = w_ada.shape[0]
    mods = _ada_modulation(c, w_ada, b_ada)
    x2 = x.reshape(batch * seq, d)
    for layer in range(depth):
        mod = mods[layer]
        x2 = _token_mixer(x2, mod, g_mix[layer], w_in[layer], g_heads[layer], w_out[layer], batch, seq)
        i = layer // 2
        if layer % 2 == 0:
            x2 = _dense_ffn(x2, mod, g_ffn[layer], w_ff_gate[i], w_ff_up[i], w_ff_down[i], seq)
        else:
            x2 = _moe_ffn(x2, mod, g_ffn[layer], w_router[i], w_exp_gate[i], w_exp_up[i], w_exp_down[i], seq)
    return _final_norm(x2, g_final).reshape(batch, seq, d)
```

```python
import functools

import numpy as np
import jax
import jax.numpy as jnp
from jax import lax
from jax.experimental import pallas as pl
from jax.experimental.pallas import tpu as pltpu

F32 = jnp.float32
BF16 = jnp.bfloat16

HEAD_DIM = 64
HEADS_PER_MIXER = 4
MIXER_WIDTH = HEADS_PER_MIXER * HEAD_DIM
QKV_WIDTH = 3 * MIXER_WIDTH
IDX_HEADS = 8
IDX_DIM = 64
DSA_TOPK = 256
DILATIONS = (1, 4, 16)
DILATED_STEPS = 128
MOBA_BLOCK = 256
MOBA_TOPK = 3
N_EXPERTS = 8
TOP_K_EXPERTS = 2
NORM_EPS = 1e-6
ATTN_SCALE = HEAD_DIM ** -0.5

LANES = 128
Q_TILE = 128
ROW_TILE = 512
FFN_TILE = 1408
EXPERT_ROWS = 512
EXPERT_FF_TILE = 1792
GATHER_ROWS = 512
ROWS_PER_TRIP = 8
VMEM_LIMIT = 56 * 1024 * 1024
MASKED = -1e30
SB_UNDERFLOW = 104.0
SB_FIRST_BLOCKS = 3

_NT = (((1,), (1,)), ((), ()))
_TN = (((0,), (0,)), ((), ()))


def _alibi_slopes(mixer_pos):
    idx = np.arange(HEADS_PER_MIXER, dtype=np.float32) * 3 + (mixer_pos + 1)
    return tuple(float(s) for s in np.exp2(-8.0 * idx / 12.0).astype(np.float32))


def _params(*semantics):
    return pltpu.CompilerParams(dimension_semantics=semantics, vmem_limit_bytes=VMEM_LIMIT)


def _modulated_norm(x, gain, shift, scale):
    y = x * lax.rsqrt(jnp.mean(x * x, axis=-1, keepdims=True) + NORM_EPS) * gain
    return y * (1.0 + scale) + shift


def _head_norm(acc, gain):
    return acc * lax.rsqrt(jnp.mean(acc * acc, axis=-1, keepdims=True) + NORM_EPS) * gain


def _head_cols(h):
    return slice(h * HEAD_DIM, (h + 1) * HEAD_DIM)


def _paired_loop(n, body, carry):
    carry = lax.fori_loop(0, n // 2, lambda jj, c: body(2 * jj + 1, body(2 * jj, c, 0), 1), carry)
    return lax.cond(n % 2 == 1, lambda c: body(n - 1, c, 0), lambda c: c, carry)


def _ada_kernel(c_ref, w_ref, b_ref, o_ref):
    c = c_ref[...]
    cond = c / (1.0 + jnp.exp(-c))
    o_ref[0, 0] = jnp.dot(cond, w_ref[0], preferred_element_type=F32,
                          precision=lax.Precision.HIGHEST) + b_ref[0, 0]


def _ada_modulation(c, w_ada, b_ada):
    depth, d, _ = w_ada.shape
    b = c.shape[0]
    out = pl.pallas_call(
        _ada_kernel,
        grid=(depth, 6),
        in_specs=[pl.BlockSpec((b, d), lambda l, k: (0, 0)),
                  pl.BlockSpec((1, d, d), lambda l, k: (l, 0, k)),
                  pl.BlockSpec((1, 1, 1, d), lambda l, k: (l, k, 0, 0))],
        out_specs=pl.BlockSpec((1, 1, b, d), lambda l, k: (l, k, 0, 0)),
        out_shape=jax.ShapeDtypeStruct((depth, 6, b, d), F32),
        compiler_params=_params("arbitrary", "arbitrary"),
        name="ada_modulation",
    )(c, w_ada, b_ada.reshape(depth, 6, 1, d))
    return out.transpose(0, 2, 1, 3)


IN_WIDTHS = (QKV_WIDTH, QKV_WIDTH, QKV_WIDTH, QKV_WIDTH, IDX_HEADS * IDX_DIM + LANES, LANES)


def _in_proj_kernel(x_ref, mod_ref, g_ref, w_ref, oa, ob, oc, od, oidx, ow, oc4, oc16, pc_ref):
    h = _modulated_norm(x_ref[...], g_ref[...], mod_ref[0, 0:1, :], mod_ref[0, 1:2, :]).astype(BF16)
    tm = x_ref.shape[0]
    off = 0
    for o_ref, width in zip((oa, ob, oc, od, oidx, ow), IN_WIDTHS):
        val = jnp.dot(h, w_ref[:, off:off + width], preferred_element_type=F32)
        o_ref[...] = val.astype(o_ref.dtype)
        if o_ref is oc:
            for s in range(QKV_WIDTH // LANES):
                pc_ref[s * tm:(s + 1) * tm, :] = val[:, s * LANES:(s + 1) * LANES]
        off += width
    for r, view in zip(DILATIONS[1:], (oc4, oc16)):
        for c in range(r):
            for s in range(QKV_WIDTH // LANES):
                col = c * QKV_WIDTH + s * LANES
                view[:, col:col + LANES] = pc_ref[pl.ds(s * tm + c, tm // r, stride=r), :].astype(view.dtype)


def _pack_w_in(w_in):
    d = w_in.shape[0]
    n_qkv = 4 * QKV_WIDTH
    n_qi = IDX_HEADS * IDX_DIM
    z = lambda n: jnp.zeros((d, n), w_in.dtype)
    return jnp.concatenate([w_in[:, :n_qkv + n_qi + IDX_DIM], z(LANES - IDX_DIM),
                            w_in[:, n_qkv + n_qi + IDX_DIM:], z(LANES - IDX_HEADS)], axis=1).astype(BF16)


def _in_projection(x2, mod, g_mix, w_packed, seq):
    t, d = x2.shape
    tm = min(ROW_TILE, seq)
    per_batch = seq // tm
    dtypes = (BF16, BF16, BF16, BF16, BF16, F32)
    views = DILATIONS[1:]
    return pl.pallas_call(
        _in_proj_kernel,
        grid=(t // tm,),
        in_specs=[pl.BlockSpec((tm, d), lambda i: (i, 0)),
                  pl.BlockSpec((1, 6, d), lambda i: (i // per_batch, 0, 0)),
                  pl.BlockSpec((1, d), lambda i: (0, 0)),
                  pl.BlockSpec(w_packed.shape, lambda i: (0, 0))],
        out_specs=([pl.BlockSpec((tm, w), lambda i: (i, 0)) for w in IN_WIDTHS]
                   + [pl.BlockSpec((tm // r, r * QKV_WIDTH), lambda i: (i, 0)) for r in views]),
        out_shape=([jax.ShapeDtypeStruct((t, w), dt) for w, dt in zip(IN_WIDTHS, dtypes)]
                   + [jax.ShapeDtypeStruct((t // r, r * QKV_WIDTH), BF16) for r in views]),
        scratch_shapes=[pltpu.VMEM((tm * (QKV_WIDTH // LANES), LANES), F32)],
        compiler_params=_params("arbitrary"),
        name="in_projection",
    )(x2, mod, g_mix.reshape(1, d), w_packed)


def _sb_kernel(q_ref, k_ref, v_ref, g_ref, o_ref):
    tq = q_ref.shape[1]
    i = pl.program_id(1)
    row = lax.broadcasted_iota(jnp.int32, (tq, 1), 0)
    lane = lax.broadcasted_iota(jnp.int32, (1, tq), 1)
    later = (lax.broadcasted_iota(jnp.int32, (tq, tq), 0) > lax.broadcasted_iota(jnp.int32, (tq, tq), 1))
    later = jnp.where(later, 1.0, 0.0).astype(BF16)

    qs = [q_ref[0, :, _head_cols(h)] * ATTN_SCALE for h in range(HEADS_PER_MIXER)]

    heads = range(HEADS_PER_MIXER)

    def blocks(starts, keeps, tails):
        z = jnp.concatenate([lax.dot_general(qs[h], k_ref[0, pl.ds(st, tq), _head_cols(h)], _NT,
                                             preferred_element_type=F32) for st in starts for h in heads], axis=0)
        softplus = jnp.maximum(z, 0.0) + jnp.log(1.0 + jnp.exp(-jnp.abs(z)))
        keep = None
        if any(kp is not None for kp in keeps):
            ones = jnp.ones((tq, tq), F32)
            keep = jnp.concatenate([ones if kp is None else kp for kp in keeps for _ in heads], axis=0)
        log_1m = -softplus if keep is None else -softplus * keep
        hi = log_1m.astype(BF16)
        lo = (log_1m - hi.astype(F32)).astype(BF16)
        inside = jnp.dot(hi, later, preferred_element_type=F32) + jnp.dot(lo, later, preferred_element_type=F32)
        block_sum = jnp.sum(log_1m, axis=1, keepdims=True)
        piece = lambda x, b, h: x[(b * HEADS_PER_MIXER + h) * tq:(b * HEADS_PER_MIXER + h + 1) * tq]
        tail_cols = []
        tails = list(tails)
        for b in range(len(starts)):
            tail_cols += tails
            tails = [tails[h] + piece(block_sum, b, h) for h in heads]
        a = jnp.exp(z - softplus + inside + jnp.concatenate(tail_cols, axis=0))
        if keep is not None:
            a = a * keep
        a = a.astype(BF16)
        av = [sum(jnp.dot(piece(a, b, h), v_ref[0, pl.ds(st, tq), _head_cols(h)], preferred_element_type=F32)
                  for b, st in enumerate(starts)) for h in heads]
        return tails, av

    starts = [pl.multiple_of(jnp.maximum(i - n, 0) * tq, tq) for n in range(SB_FIRST_BLOCKS)]
    keeps = [jnp.where(lane < row, 1.0, 0.0)]
    keeps += [jnp.full((tq, tq), jnp.where(i >= n, 1.0, 0.0), F32) for n in range(1, SB_FIRST_BLOCKS)]
    tails, accs = blocks(starts, keeps, [jnp.zeros((tq, 1), F32)] * HEADS_PER_MIXER)
    state = tuple(zip(tails, accs))

    def body(carry):
        j, state = carry
        tails, av = blocks([pl.multiple_of(j * tq, tq)], [None], [tail for tail, _ in state])
        return j - 1, tuple((tails[h], state[h][1] + av[h]) for h in heads)

    def cond(carry):
        j, state = carry
        worst = functools.reduce(jnp.maximum, [tail for tail, _ in state])
        return jnp.logical_and(j >= 0, jnp.max(worst) > -SB_UNDERFLOW)

    _, state = lax.while_loop(cond, body, (i - SB_FIRST_BLOCKS, state))
    for h in range(HEADS_PER_MIXER):
        o_ref[0, :, _head_cols(h)] = _head_norm(state[h][1], g_ref[h:h + 1, :]).astype(o_ref.dtype)


def _stick_breaking_mixer(p, g_heads):
    b, seq, _ = p.shape
    tq = min(Q_TILE, seq)
    return pl.pallas_call(
        _sb_kernel,
        grid=(b, seq // tq),
        in_specs=[pl.BlockSpec((1, tq, MIXER_WIDTH), lambda bi, i: (bi, i, 0)),
                  pl.BlockSpec((1, seq, MIXER_WIDTH), lambda bi, i: (bi, 0, 1)),
                  pl.BlockSpec((1, seq, MIXER_WIDTH), lambda bi, i: (bi, 0, 2)),
                  pl.BlockSpec((HEADS_PER_MIXER, HEAD_DIM), lambda bi, i: (0, 0))],
        out_specs=pl.BlockSpec((1, tq, MIXER_WIDTH), lambda bi, i: (bi, i, 0)),
        out_shape=jax.ShapeDtypeStruct((b, seq, MIXER_WIDTH), BF16),
        compiler_params=_params("arbitrary", "arbitrary"),
        name="stick_breaking",
    )(p, p, p, g_heads)


def _dsa_kernel(q_ref, k_ref, v_ref, qi_ref, ki_ref, w_ref, g_ref, o_ref, sc_ref, sc16_ref, s_ref, p_ref, *, slopes):
    tq = q_ref.shape[1]
    tk = sc_ref.shape[1]
    i = pl.program_id(1)
    n_blocks = ((i + 1) * tq + tk - 1) // tk
    qpos = i * tq + lax.broadcasted_iota(jnp.int32, (1, tq), 1)
    key_row = lax.broadcasted_iota(jnp.int32, (tk, 1), 0)
    neg_inf = float("-inf")

    w_t = w_ref[0].T * (IDX_HEADS ** -0.5 * IDX_DIM ** -0.5)
    w_rows = [w_t[h:h + 1, :] for h in range(IDX_HEADS)]
    qi = qi_ref[0]

    def top_half(x):
        bits = lax.bitcast_convert_type(x, jnp.int32) & jnp.int32(-65536)
        return lax.bitcast_convert_type(bits, F32).astype(BF16)

    def score_body(j, carry, slot):
        start = pl.multiple_of(j * tk, tk)
        ki = ki_ref[0, pl.ds(start, tk), 0:IDX_DIM]
        sc = jnp.zeros((tk, tq), F32)
        for h in range(IDX_HEADS):
            x = lax.dot_general(ki, qi[:, h * IDX_DIM:(h + 1) * IDX_DIM], _NT, preferred_element_type=F32)
            sc = sc + w_rows[h] * jnp.maximum(x, 0.0)
        sc = jnp.where((start + key_row) <= qpos, sc + 0.0, neg_inf)
        sc_ref[j] = sc
        sc16_ref[j] = top_half(sc)
        return carry

    _paired_loop(n_blocks, score_body, 0)

    def count(pred):
        def add_block(j, cnt, slot):
            c = jnp.where(pred(sc_ref[j]), 1.0, 0.0)
            return cnt + jnp.sum(c.reshape(tk // 8, 8, tq), axis=0)

        cnt = _paired_loop(n_blocks, add_block, jnp.zeros((8, tq), F32))
        return jnp.sum(cnt, axis=0, keepdims=True)

    int_min = jnp.int32(-2 ** 31)

    def ordered_to_float(u):
        key = u ^ int_min
        bits = jnp.where(key >= 0, key, key ^ jnp.int32(0x7FFFFFFF))
        return lax.bitcast_convert_type(bits, F32)

    def count_top_half(cand16):
        def add_block(j, cnt, slot):
            c = jnp.where(sc16_ref[j] >= cand16, jnp.ones((), BF16), jnp.zeros((), BF16))
            for g in range(tk // 16):
                cnt = cnt + c[g * 16:(g + 1) * 16]
            return cnt

        cnt = _paired_loop(n_blocks, add_block, jnp.zeros((16, tq), BF16))
        return jnp.sum(cnt.astype(F32), axis=0, keepdims=True)

    def search_body(step, carry, half):
        prefix, n_at_prefix = carry
        cand = prefix | jnp.left_shift(jnp.int32(1), 31 - step)
        cand_f = ordered_to_float(cand)
        n_ge = count_top_half(top_half(cand_f)) if half else count(lambda s: s >= cand_f)
        keep = n_ge >= DSA_TOPK
        return jnp.where(keep, cand, prefix), jnp.where(keep, n_ge, n_at_prefix)

    carry = (jnp.zeros((1, tq), jnp.int32), jnp.zeros((1, tq), F32))
    carry = lax.fori_loop(0, 16, functools.partial(search_body, half=True), carry)
    prefix, n_ge_tau = lax.fori_loop(16, 32, functools.partial(search_body, half=False), carry)
    keep_all = (qpos + 1) <= DSA_TOPK
    tau = jnp.where(keep_all, float(np.finfo(np.float32).min), ordered_to_float(prefix))
    no_ties = jnp.min(jnp.where(jnp.logical_or(keep_all, n_ge_tau == DSA_TOPK), 1.0, 0.0)) > 0.5

    earlier = (lax.broadcasted_iota(jnp.int32, (tk, tk), 1) < lax.broadcasted_iota(jnp.int32, (tk, tk), 0))
    earlier = jnp.where(earlier, 1.0, 0.0).astype(BF16)
    qs = [q_ref[0, :, _head_cols(h)] * ATTN_SCALE for h in range(HEADS_PER_MIXER)]
    key_bias = [slopes[h] * lax.broadcasted_iota(jnp.int32, (tk, tq), 0).astype(F32)
                for h in range(HEADS_PER_MIXER)]

    def attn_body(j, carry, slot, n_ties=None):
        ties_seen, stats = carry
        s_slot = [s_ref.at[slot * HEADS_PER_MIXER + h] for h in range(HEADS_PER_MIXER)]
        p_slot = [p_ref.at[slot * HEADS_PER_MIXER + h] for h in range(HEADS_PER_MIXER)]
        start = pl.multiple_of(j * tk, tk)
        sc = sc_ref[j]
        if n_ties is None:
            sel = sc >= tau
        else:
            tie = jnp.where(sc == tau, 1.0, 0.0)
            rank = jnp.dot(earlier, tie.astype(BF16), preferred_element_type=F32) + ties_seen
            sel = jnp.where(sc > tau, 1.0, jnp.where(rank < n_ties, tie, 0.0)) > 0.5
            ties_seen = ties_seen + jnp.sum(tie, axis=0, keepdims=True)
        block_pos = (j * tk - i * tq).astype(F32)
        for h in range(HEADS_PER_MIXER):
            k = k_ref[0, pl.ds(start, tk), _head_cols(h)]
            s = lax.dot_general(k, qs[h], _NT, preferred_element_type=F32) + key_bias[h]
            s_slot[h][...] = jnp.where(sel, s, MASKED)
        locals_ = []
        for h in range(HEADS_PER_MIXER):
            s = s_slot[h][...]
            m_loc = jnp.max(s, axis=0, keepdims=True)
            p = jnp.exp(s - m_loc)
            locals_.append((m_loc, jnp.sum(p, axis=0, keepdims=True)))
            p_slot[h][...] = p.astype(BF16)
        new_stats = []
        for h in range(HEADS_PER_MIXER):
            m, l, acc = stats[h]
            m_loc, l_blk = locals_[h]
            v = v_ref[0, pl.ds(start, tk), _head_cols(h)]
            pv_blk = lax.dot_general(v, p_slot[h][...], _TN, preferred_element_type=F32)
            m_blk = m_loc + slopes[h] * block_pos
            m_new = jnp.maximum(m, m_blk)
            w_old = jnp.exp(m - m_new)
            w_blk = jnp.where(m_loc > 0.5 * MASKED, jnp.exp(m_blk - m_new), 0.0)
            new_stats.append((m_new, w_old * l + w_blk * l_blk, w_old * acc + w_blk * pv_blk))
        return ties_seen, tuple(new_stats)

    init = tuple((jnp.full((1, tq), MASKED, F32), jnp.zeros((1, tq), F32), jnp.zeros((HEAD_DIM, tq), F32))
                 for _ in range(HEADS_PER_MIXER))
    start_carry = (jnp.zeros((1, tq), F32), init)

    def attend_without_ties():
        return _paired_loop(n_blocks, attn_body, start_carry)[1]

    def attend_with_ties():
        n_gt = count(lambda s: s > tau)
        n_ties = jnp.where(keep_all, 1e9, DSA_TOPK - n_gt)
        return _paired_loop(n_blocks, functools.partial(attn_body, n_ties=n_ties), start_carry)[1]

    stats = lax.cond(no_ties, attend_without_ties, attend_with_ties)
    outs = []
    for h in range(HEADS_PER_MIXER):
        _, l, acc = stats[h]
        o = acc / l
        outs.append(o * lax.rsqrt(jnp.mean(o * o, axis=0, keepdims=True) + NORM_EPS) * g_ref[:, h:h + 1])
    o_ref[0] = jnp.concatenate(outs, axis=0).T.astype(o_ref.dtype)


def _dsa_mixer(p, p_idx, p_w, g_heads):
    b, seq, _ = p.shape
    tq = min(2 * Q_TILE, seq)
    tk = tq
    assert seq // 16 <= 256, "the packed-bf16 count accumulators are exact only up to 256 adds"
    qi_width = IDX_HEADS * IDX_DIM
    return pl.pallas_call(
        functools.partial(_dsa_kernel, slopes=_alibi_slopes(0)),
        grid=(b, seq // tq),
        in_specs=[pl.BlockSpec((1, tq, MIXER_WIDTH), lambda bi, i: (bi, i, 0)),
                  pl.BlockSpec((1, seq, MIXER_WIDTH), lambda bi, i: (bi, 0, 1)),
                  pl.BlockSpec((1, seq, MIXER_WIDTH), lambda bi, i: (bi, 0, 2)),
                  pl.BlockSpec((1, tq, qi_width), lambda bi, i: (bi, i, 0)),
                  pl.BlockSpec((1, seq, LANES), lambda bi, i: (bi, 0, qi_width // LANES)),
                  pl.BlockSpec((1, tq, LANES), lambda bi, i: (bi, i, 0)),
                  pl.BlockSpec((HEAD_DIM, HEADS_PER_MIXER), lambda bi, i: (0, 0))],
        out_specs=pl.BlockSpec((1, tq, MIXER_WIDTH), lambda bi, i: (bi, i, 0)),
        out_shape=jax.ShapeDtypeStruct((b, seq, MIXER_WIDTH), BF16),
        scratch_shapes=[pltpu.VMEM((seq // tk, tk, tq), F32),
                        pltpu.VMEM((seq // tk, tk, tq), BF16),
                        pltpu.VMEM((2 * HEADS_PER_MIXER, tk, tq), F32),
                        pltpu.VMEM((2 * HEADS_PER_MIXER, tk, tq), BF16)],
        compiler_params=_params("arbitrary", "arbitrary"),
        name="dsa",
    )(p, p, p, p_idx, p_idx, p_w, g_heads.T)


def _band_kernel(q_ref, kp_ref, kc_ref, vp_ref, vc_ref, o_ref, lse_ref, *, dilation, slopes):
    tq = q_ref.shape[1]
    ui = pl.program_id(2)
    heads = range(HEADS_PER_MIXER)
    u_q = jnp.concatenate([ui * tq + lax.broadcasted_iota(jnp.int32, (tq, 1), 0)] * HEADS_PER_MIXER, axis=0)
    u_k = ui * tq - DILATED_STEPS + lax.broadcasted_iota(jnp.int32, (1, DILATED_STEPS + tq), 1)
    steps = u_q - u_k
    valid = jnp.logical_and(jnp.logical_and(steps >= 0, steps <= DILATED_STEPS), u_k >= 0)
    slope_col = jnp.concatenate([jnp.full((tq, 1), slopes[h], F32) for h in heads], axis=0)
    s = jnp.concatenate(
        [lax.dot_general(q_ref[0, :, _head_cols(h)],
                         jnp.concatenate([kp_ref[0, :, _head_cols(h)], kc_ref[0, :, _head_cols(h)]], axis=0),
                         _NT, preferred_element_type=F32) for h in heads], axis=0)
    s = jnp.where(valid, s * ATTN_SCALE - slope_col * (steps * dilation).astype(F32), MASKED)
    m = jnp.max(s, axis=1, keepdims=True)
    e = jnp.exp(s - m)
    den = jnp.sum(e, axis=1, keepdims=True)
    p = e.astype(BF16)
    lse = m + jnp.log(den)
    lane = lax.broadcasted_iota(jnp.int32, (1, LANES), 1)
    lse_all = jnp.zeros((tq, LANES), F32)
    for h in heads:
        rows = slice(h * tq, (h + 1) * tq)
        v = jnp.concatenate([vp_ref[0, :, _head_cols(h)], vc_ref[0, :, _head_cols(h)]], axis=0)
        o_ref[0, :, _head_cols(h)] = jnp.dot(p[rows], v, preferred_element_type=F32) / den[rows]
        lse_all = lse_all + jnp.where(lane == h, lse[rows], 0.0)
    lse_ref[0] = lse_all


def _dilated_branch(view, dilation, slopes):
    b, length, _ = view.shape
    classes = dilation
    tq = min(2 * Q_TILE, length)
    back = tq // DILATED_STEPS
    spec = lambda part, prev: (
        pl.BlockSpec((1, DILATED_STEPS, MIXER_WIDTH),
                     lambda bi, c, ui: (bi, jnp.maximum(ui * back - 1, 0), c * 3 + part)) if prev
        else pl.BlockSpec((1, tq, MIXER_WIDTH), lambda bi, c, ui: (bi, ui, c * 3 + part)))
    out, lse = pl.pallas_call(
        functools.partial(_band_kernel, dilation=dilation, slopes=slopes),
        grid=(b, classes, length // tq),
        in_specs=[spec(0, False), spec(1, True), spec(1, False), spec(2, True), spec(2, False)],
        out_specs=[pl.BlockSpec((1, tq, MIXER_WIDTH), lambda bi, c, ui: (bi, ui, c)),
                   pl.BlockSpec((1, tq, LANES), lambda bi, c, ui: (bi, ui, c))],
        out_shape=[jax.ShapeDtypeStruct((b, length, classes * MIXER_WIDTH), F32),
                   jax.ShapeDtypeStruct((b, length, classes * LANES), F32)],
        compiler_params=_params("arbitrary", "arbitrary", "arbitrary"),
        name=f"dilated_r{dilation}",
    )(view, view, view, view, view)
    return out, lse


def _dilated_merge_kernel(o1, o2, o3, l1, l2, l3, g_ref, o_ref, *scratch):
    tq = o_ref.shape[1]
    chunks = MIXER_WIDTH // LANES
    outs = [[o1[0, :, s * LANES:(s + 1) * LANES] for s in range(chunks)]]
    lses = [l1[0]]
    for r, o_view, l_view, o_nat, l_nat in zip(DILATIONS[1:], (o2, o3), (l2, l3), scratch[0::2], scratch[1::2]):
        for c in range(r):
            for s in range(chunks):
                col = c * MIXER_WIDTH + s * LANES
                o_nat[pl.ds(s * tq + c, tq // r, stride=r), :] = o_view[0, :, col:col + LANES]
            l_nat[pl.ds(c, tq // r, stride=r), :] = l_view[0, :, c * LANES:(c + 1) * LANES]
        outs.append([o_nat[s * tq:(s + 1) * tq, :] for s in range(chunks)])
        lses.append(l_nat[...])
    heads_per_chunk = LANES // HEAD_DIM
    for h in range(HEADS_PER_MIXER):
        sub = slice((h % heads_per_chunk) * HEAD_DIM, (h % heads_per_chunk + 1) * HEAD_DIM)
        lse_h = [l[:, h:h + 1] for l in lses]
        top = functools.reduce(jnp.maximum, lse_h)
        wts = [jnp.exp(l - top) for l in lse_h]
        mixed = sum(w * o[h // heads_per_chunk][:, sub] for w, o in zip(wts, outs)) / sum(wts)
        o_ref[0, :, _head_cols(h)] = _head_norm(mixed, g_ref[h:h + 1, :]).astype(o_ref.dtype)


def _dilated_mixer(views, g_heads):
    b, seq, _ = views[0].shape
    slopes = _alibi_slopes(1)
    branches = [_dilated_branch(v, r, slopes) for v, r in zip(views, DILATIONS)]
    tq = min(ROW_TILE, seq)
    o_specs = [pl.BlockSpec((1, tq // r, r * MIXER_WIDTH), lambda bi, i: (bi, i, 0)) for r in DILATIONS]
    l_specs = [pl.BlockSpec((1, tq // r, r * LANES), lambda bi, i: (bi, i, 0)) for r in DILATIONS]
    scratch = []
    for _ in DILATIONS[1:]:
        scratch += [pltpu.VMEM((tq * (MIXER_WIDTH // LANES), LANES), F32), pltpu.VMEM((tq, LANES), F32)]
    return pl.pallas_call(
        _dilated_merge_kernel,
        grid=(b, seq // tq),
        in_specs=o_specs + l_specs + [pl.BlockSpec((HEADS_PER_MIXER, HEAD_DIM), lambda bi, i: (0, 0))],
        out_specs=pl.BlockSpec((1, tq, MIXER_WIDTH), lambda bi, i: (bi, i, 0)),
        out_shape=jax.ShapeDtypeStruct((b, seq, MIXER_WIDTH), BF16),
        scratch_shapes=scratch,
        compiler_params=_params("arbitrary", "arbitrary"),
        name="dilated_merge",
    )(*[o for o, _ in branches], *[l for _, l in branches], g_heads)


def _moba_kernel(q_ref, k_ref, v_ref, g_ref, o_ref, kmean_ref, chosen_ref, s_ref, p_ref, *, slopes):
    tq = q_ref.shape[1]
    n_kv = k_ref.shape[1] // MOBA_BLOCK
    own = pl.program_id(1)

    @pl.when(own == 0)
    def _():
        kmean_ref[...] = jnp.zeros_like(kmean_ref)
        for n in range(n_kv):
            blk = k_ref[0, n * MOBA_BLOCK:(n + 1) * MOBA_BLOCK, :].astype(F32)
            kmean_ref[n:n + 1, :] = jnp.sum(blk, axis=0, keepdims=True) * (1.0 / MOBA_BLOCK)

    key_row = lax.broadcasted_iota(jnp.int32, (MOBA_BLOCK, 1), 0)
    q_lane = lax.broadcasted_iota(jnp.int32, (1, tq), 1)
    blk_row = lax.broadcasted_iota(jnp.int32, (LANES, 1), 0)
    blk_f = blk_row.astype(F32)
    neg_inf = float("-inf")

    qs = []
    for h in range(HEADS_PER_MIXER):
        q = q_ref[0, :, _head_cols(h)]
        gate = lax.dot_general(kmean_ref[:, _head_cols(h)], q.astype(F32), _NT, preferred_element_type=F32,
                               precision=lax.Precision.HIGHEST)
        gate = jnp.where(blk_row < own, gate, neg_inf)
        picks = jnp.zeros((LANES, tq), F32)
        for _ in range(MOBA_TOPK):
            top = jnp.max(gate, axis=0, keepdims=True)
            is_top = jnp.logical_and(gate == top, top > neg_inf)
            first = jnp.min(jnp.where(is_top, blk_f, float(LANES)), axis=0, keepdims=True)
            pick = blk_f == first
            picks = jnp.where(pick, 1.0, picks)
            gate = jnp.where(pick, neg_inf, gate)
        chosen_ref[h] = picks
        qs.append(q * ATTN_SCALE)

    key_bias = [slopes[h] * lax.broadcasted_iota(jnp.int32, (MOBA_BLOCK, tq), 0).astype(F32)
                for h in range(HEADS_PER_MIXER)]

    def block_softmax(start, keep=None, slot=0):
        s_slot = [s_ref.at[slot * HEADS_PER_MIXER + h] for h in range(HEADS_PER_MIXER)]
        p_slot = [p_ref.at[slot * HEADS_PER_MIXER + h] for h in range(HEADS_PER_MIXER)]
        for h in range(HEADS_PER_MIXER):
            k = k_ref[0, pl.ds(start, MOBA_BLOCK), _head_cols(h)]
            s = lax.dot_general(k, qs[h], _NT, preferred_element_type=F32) + key_bias[h]
            s_slot[h][...] = s if keep is None else jnp.where(keep, s, MASKED)
        pieces = []
        for h in range(HEADS_PER_MIXER):
            s = s_slot[h][...]
            m_loc = jnp.max(s, axis=0, keepdims=True)
            p = jnp.exp(s - m_loc)
            pieces.append((m_loc, jnp.sum(p, axis=0, keepdims=True)))
            p_slot[h][...] = p.astype(BF16)
        out = []
        for h in range(HEADS_PER_MIXER):
            v = v_ref[0, pl.ds(start, MOBA_BLOCK), _head_cols(h)]
            out.append(pieces[h] + (lax.dot_general(v, p_slot[h][...], _TN, preferred_element_type=F32),))
        return out

    own_start = pl.multiple_of(own * MOBA_BLOCK, MOBA_BLOCK)
    stats = tuple(block_softmax(own_start, keep=key_row <= q_lane))

    def body(j, stats, slot):
        start = pl.multiple_of(j * MOBA_BLOCK, MOBA_BLOCK)
        block_pos = ((j - own) * MOBA_BLOCK).astype(F32)
        merged = []
        for h, (m_loc, l_blk, pv_blk) in enumerate(block_softmax(start, slot=slot)):
            picked = chosen_ref[h, pl.ds(j, 1), :] > 0.5
            m_blk = jnp.where(picked, m_loc + slopes[h] * block_pos, MASKED)
            m, l, acc = stats[h]
            m_new = jnp.maximum(m, m_blk)
            w_old = jnp.exp(m - m_new)
            w_blk = jnp.exp(m_blk - m_new)
            merged.append((m_new, w_old * l + w_blk * l_blk, w_old * acc + w_blk * pv_blk))
        return tuple(merged)

    stats = _paired_loop(own, body, stats)
    outs = []
    for h in range(HEADS_PER_MIXER):
        _, l, acc = stats[h]
        o = acc / l
        outs.append(o * lax.rsqrt(jnp.mean(o * o, axis=0, keepdims=True) + NORM_EPS) * g_ref[:, h:h + 1])
    o_ref[0] = jnp.concatenate(outs, axis=0).T.astype(o_ref.dtype)


def _moba_mixer(p, g_heads):
    b, seq, _ = p.shape
    tq = MOBA_BLOCK
    return pl.pallas_call(
        functools.partial(_moba_kernel, slopes=_alibi_slopes(2)),
        grid=(b, seq // tq),
        in_specs=[pl.BlockSpec((1, tq, MIXER_WIDTH), lambda bi, i: (bi, i, 0)),
                  pl.BlockSpec((1, seq, MIXER_WIDTH), lambda bi, i: (bi, 0, 1)),
                  pl.BlockSpec((1, seq, MIXER_WIDTH), lambda bi, i: (bi, 0, 2)),
                  pl.BlockSpec((HEAD_DIM, HEADS_PER_MIXER), lambda bi, i: (0, 0))],
        out_specs=pl.BlockSpec((1, tq, MIXER_WIDTH), lambda bi, i: (bi, i, 0)),
        out_shape=jax.ShapeDtypeStruct((b, seq, MIXER_WIDTH), BF16),
        scratch_shapes=[pltpu.VMEM((LANES, MIXER_WIDTH), F32),
                        pltpu.VMEM((HEADS_PER_MIXER, LANES, tq), F32),
                        pltpu.VMEM((2 * HEADS_PER_MIXER, MOBA_BLOCK, tq), F32),
                        pltpu.VMEM((2 * HEADS_PER_MIXER, MOBA_BLOCK, tq), BF16)],
        compiler_params=_params("arbitrary", "arbitrary"),
        name="moba",
    )(p, p, p, g_heads.T)


def _out_proj_kernel(oa, ob, oc, od, w_ref, x_ref, mod_ref, o_ref):
    acc = jnp.zeros(x_ref.shape, F32)
    for m, o in enumerate((oa, ob, oc, od)):
        acc = acc + jnp.dot(o[...], w_ref[m * MIXER_WIDTH:(m + 1) * MIXER_WIDTH, :], preferred_element_type=F32)
    o_ref[...] = x_ref[...] + mod_ref[0, 2:3, :] * acc


def _out_projection(mixed, w_out, x2, mod, seq):
    t, d = x2.shape
    tm = min(ROW_TILE, seq)
    per_batch = seq // tm
    o_spec = pl.BlockSpec((tm, MIXER_WIDTH), lambda i: (i, 0))
    return pl.pallas_call(
        _out_proj_kernel,
        grid=(t // tm,),
        in_specs=[o_spec] * 4 + [pl.BlockSpec(w_out.shape, lambda i: (0, 0)),
                                 pl.BlockSpec((tm, d), lambda i: (i, 0)),
                                 pl.BlockSpec((1, 6, d), lambda i: (i // per_batch, 0, 0))],
        out_specs=pl.BlockSpec((tm, d), lambda i: (i, 0)),
        out_shape=jax.ShapeDtypeStruct((t, d), F32),
        compiler_params=_params("arbitrary"),
        name="out_projection",
    )(*[o.reshape(t, MIXER_WIDTH) for o in mixed], w_out, x2, mod)


def _ffn_kernel(x_ref, mod_ref, g_ref, wg_ref, wu_ref, wd_ref, o_ref, h_ref, acc_ref):
    f = pl.program_id(1)

    @pl.when(f == 0)
    def _():
        h_ref[...] = _modulated_norm(x_ref[...], g_ref[...], mod_ref[0, 3:4, :], mod_ref[0, 4:5, :]).astype(BF16)
        acc_ref[...] = jnp.zeros_like(acc_ref)

    h = h_ref[...]
    gate = jnp.dot(h, wg_ref[...], preferred_element_type=F32)
    up = jnp.dot(h, wu_ref[...], preferred_element_type=F32)
    act = (gate / (1.0 + jnp.exp(-gate)) * up).astype(BF16)
    acc_ref[...] += jnp.dot(act, wd_ref[...], preferred_element_type=F32)

    @pl.when(f == pl.num_programs(1) - 1)
    def _():
        o_ref[...] = x_ref[...] + mod_ref[0, 5:6, :] * acc_ref[...]


def _dense_ffn(x2, mod, g_ffn, wg, wu, wd, seq):
    t, d = x2.shape
    d_ff = wg.shape[1]
    tm = min(ROW_TILE, seq)
    tf = FFN_TILE if d_ff % FFN_TILE == 0 else d_ff
    per_batch = seq // tm
    return pl.pallas_call(
        _ffn_kernel,
        grid=(t // tm, d_ff // tf),
        in_specs=[pl.BlockSpec((tm, d), lambda i, f: (i, 0)),
                  pl.BlockSpec((1, 6, d), lambda i, f: (i // per_batch, 0, 0)),
                  pl.BlockSpec((1, d), lambda i, f: (0, 0)),
                  pl.BlockSpec((d, tf), lambda i, f: (0, f)),
                  pl.BlockSpec((d, tf), lambda i, f: (0, f)),
                  pl.BlockSpec((tf, d), lambda i, f: (f, 0))],
        out_specs=pl.BlockSpec((tm, d), lambda i, f: (i, 0)),
        out_shape=jax.ShapeDtypeStruct((t, d), F32),
        scratch_shapes=[pltpu.VMEM((tm, d), BF16), pltpu.VMEM((tm, d), F32)],
        compiler_params=_params("arbitrary", "arbitrary"),
        name="dense_ffn",
    )(x2, mod, g_ffn.reshape(1, d), wg.astype(BF16), wu.astype(BF16), wd.astype(BF16))


def _store_row_tiles(dst_ref, value):
    rows, d = value.shape
    chunks = d // LANES
    for s in range(chunks):
        dst_ref[pl.ds(s, rows, stride=chunks), :] = value[:, s * LANES:(s + 1) * LANES]


def _load_row_tile_chunk(src_ref, s, rows, chunks):
    return src_ref[pl.ds(s, rows, stride=chunks), :]


def _router_kernel(x_ref, mod_ref, g_ref, wr_ref, h_ref, logit_ref):
    h = _modulated_norm(x_ref[...], g_ref[...], mod_ref[0, 3:4, :], mod_ref[0, 4:5, :])
    _store_row_tiles(h_ref, h)
    logit_ref[...] = jnp.dot(h, wr_ref[...], preferred_element_type=F32, precision=lax.Precision.HIGHEST)


def _router(x2, mod, g_ffn, w_router, seq):
    t, d = x2.shape
    tm = min(ROW_TILE, seq)
    per_batch = seq // tm
    wr = jnp.zeros((d, LANES), F32).at[:, :N_EXPERTS].set(w_router.astype(F32))
    return pl.pallas_call(
        _router_kernel,
        grid=(t // tm,),
        in_specs=[pl.BlockSpec((tm, d), lambda i: (i, 0)),
                  pl.BlockSpec((1, 6, d), lambda i: (i // per_batch, 0, 0)),
                  pl.BlockSpec((1, d), lambda i: (0, 0)),
                  pl.BlockSpec((d, LANES), lambda i: (0, 0))],
        out_specs=[pl.BlockSpec((tm * (d // LANES), LANES), lambda i: (i, 0)),
                   pl.BlockSpec((tm, LANES), lambda i: (i, 0))],
        out_shape=[jax.ShapeDtypeStruct((t * (d // LANES), LANES), F32), jax.ShapeDtypeStruct((t, LANES), F32)],
        compiler_params=_params("arbitrary"),
        name="moe_router",
    )(x2, mod, g_ffn.reshape(1, d), wr)


def _row_copy(src_hbm, row, dst_ref, r, sem, chunks):
    src = src_hbm.at[pl.ds(pl.multiple_of(row * chunks, chunks), chunks), :]
    return pltpu.make_async_copy(src, dst_ref.at[pl.ds(pl.multiple_of(r * chunks, chunks), chunks), :], sem)


def _expert_kernel(be_ref, used_ref, cur_idx_ref, nxt_idx_ref, h_hbm, wg_ref, wu_ref, wd_ref, o_ref,
                   xin_ref, xb_ref, acc_ref, sem):
    m = pl.program_id(0)
    f = pl.program_id(1)
    n_used = used_ref[0]
    live = m < n_used
    rows, d = xb_ref.shape
    chunks = d // LANES

    def row_copies(idx_ref, slot, start):
        if not start:
            half = rows * chunks // 2
            for lane in range(2):
                pltpu.make_async_copy(h_hbm.at[pl.ds(0, half), :], xin_ref.at[slot, pl.ds(0, half), :],
                                      sem.at[slot, lane]).wait()
            return

        def body(group, c):
            for u in range(ROWS_PER_TRIP):
                r = ROWS_PER_TRIP * group + u
                _row_copy(h_hbm, idx_ref[0, 0, r], xin_ref.at[slot], r, sem.at[slot, u % 2],
                          chunks).start(priority=u % 2)
            return c
        lax.fori_loop(0, rows // ROWS_PER_TRIP, body, 0)

    @pl.when(jnp.logical_and(live, f == 0))
    def _():
        @pl.when(m == 0)
        def _():
            row_copies(cur_idx_ref, 0, start=True)

        for slot in range(2):
            @pl.when(m % 2 == slot)
            def _(slot=slot):
                row_copies(cur_idx_ref, slot, start=False)
                for s in range(chunks):
                    xb_ref[:, s * LANES:(s + 1) * LANES] = _load_row_tile_chunk(
                        xin_ref.at[slot], s, rows, chunks).astype(BF16)

                @pl.when(m + 1 < n_used)
                def _():
                    row_copies(nxt_idx_ref, 1 - slot, start=True)

        acc_ref[...] = jnp.zeros_like(acc_ref)

    @pl.when(live)
    def _():
        h = xb_ref[...]
        gate = jnp.dot(h, wg_ref[0], preferred_element_type=F32)
        up = jnp.dot(h, wu_ref[0], preferred_element_type=F32)
        act = (gate / (1.0 + jnp.exp(-gate)) * up).astype(BF16)
        acc_ref[...] += jnp.dot(act, wd_ref[0], preferred_element_type=F32)

    @pl.when(f == pl.num_programs(1) - 1)
    def _():
        _store_row_tiles(o_ref, jnp.where(live, acc_ref[...], 0.0))


def _expert_ffn(h_rows, slot_tok, block_expert, n_used, wg, wu, wd):
    d = wg.shape[1]
    chunks = d // LANES
    n_slots = slot_tok.shape[0]
    d_ff = wg.shape[2]
    tm = EXPERT_ROWS
    tf = EXPERT_FF_TILE if d_ff % EXPERT_FF_TILE == 0 else d_ff
    nf = d_ff // tf
    nm = n_slots // tm

    def fcol(m, f, used):
        return jnp.where(m < used[0], f, nf - 1)

    idx_spec = lambda shift: pl.BlockSpec((1, 1, tm), lambda m, f, be, used: (jnp.minimum(m + shift, nm - 1), 0, 0),
                                          memory_space=pltpu.SMEM)
    grid_spec = pltpu.PrefetchScalarGridSpec(
        num_scalar_prefetch=2,
        grid=(nm, nf),
        in_specs=[idx_spec(0), idx_spec(1), pl.BlockSpec(memory_space=pl.ANY),
                  pl.BlockSpec((1, d, tf), lambda m, f, be, used: (be[m], 0, fcol(m, f, used))),
                  pl.BlockSpec((1, d, tf), lambda m, f, be, used: (be[m], 0, fcol(m, f, used))),
                  pl.BlockSpec((1, tf, d), lambda m, f, be, used: (be[m], fcol(m, f, used), 0))],
        out_specs=pl.BlockSpec((tm * chunks, LANES), lambda m, f, be, used: (m, 0)),
        scratch_shapes=[pltpu.VMEM((2, tm * chunks, LANES), F32), pltpu.VMEM((tm, d), BF16),
                        pltpu.VMEM((tm, d), F32), pltpu.SemaphoreType.DMA((2, 2))])
    idx = slot_tok.reshape(nm, 1, tm)
    return pl.pallas_call(
        _expert_kernel,
        grid_spec=grid_spec,
        out_shape=jax.ShapeDtypeStruct((n_slots * chunks, LANES), F32),
        compiler_params=_params("arbitrary", "arbitrary"),
        name="expert_ffn",
    )(block_expert, n_used, idx, idx, h_rows, wg.astype(BF16), wu.astype(BF16), wd.astype(BF16))


def _combine_kernel(d0_ref, d1_ref, ys_hbm, x_ref, gates_ref, mod_ref, o_ref, y0_ref, y1_ref, sem):
    rows, d = o_ref.shape
    chunks = d // LANES

    def start(group, c):
        for u in range(ROWS_PER_TRIP // 2):
            r = ROWS_PER_TRIP // 2 * group + u
            _row_copy(ys_hbm, d0_ref[0, 0, r], y0_ref, r, sem.at[0], chunks).start(priority=0)
            _row_copy(ys_hbm, d1_ref[0, 0, r], y1_ref, r, sem.at[1], chunks).start(priority=1)
        return c

    lax.fori_loop(0, rows // (ROWS_PER_TRIP // 2), start, 0)
    for k, y_ref in enumerate((y0_ref, y1_ref)):
        pltpu.make_async_copy(ys_hbm.at[pl.ds(0, rows * chunks), :], y_ref, sem.at[k]).wait()
    gates = gates_ref[...]
    for s in range(chunks):
        cols = slice(s * LANES, (s + 1) * LANES)
        y = (_load_row_tile_chunk(y0_ref, s, rows, chunks) * gates[:, 0:1]
             + _load_row_tile_chunk(y1_ref, s, rows, chunks) * gates[:, 1:2])
        o_ref[:, cols] = x_ref[:, cols] + mod_ref[0, 5:6, cols] * y


def _moe_combine(ys, dest0, dest1, gates, x2, mod, seq):
    t, d = x2.shape
    tm = min(GATHER_ROWS, seq)
    steps = t // tm
    per_batch = seq // tm
    idx_spec = pl.BlockSpec((1, 1, tm), lambda i: (i, 0, 0), memory_space=pltpu.SMEM)
    return pl.pallas_call(
        _combine_kernel,
        grid=(steps,),
        in_specs=[idx_spec, idx_spec, pl.BlockSpec(memory_space=pl.ANY),
                  pl.BlockSpec((tm, d), lambda i: (i, 0)),
                  pl.BlockSpec((tm, TOP_K_EXPERTS), lambda i: (i, 0)),
                  pl.BlockSpec((1, 6, d), lambda i: (i // per_batch, 0, 0))],
        out_specs=pl.BlockSpec((tm, d), lambda i: (i, 0)),
        out_shape=jax.ShapeDtypeStruct((t, d), F32),
        scratch_shapes=[pltpu.VMEM((tm * (d // LANES), LANES), F32), pltpu.VMEM((tm * (d // LANES), LANES), F32),
                        pltpu.SemaphoreType.DMA((2,))],
        compiler_params=_params("arbitrary"),
        name="moe_combine",
    )(dest0.reshape(steps, 1, tm), dest1.reshape(steps, 1, tm), ys, x2, gates, mod)


def _moe_ffn(x2, mod, g_ffn, w_router, wg, wu, wd, seq):
    t, d = x2.shape
    h, logits = _router(x2, mod, g_ffn, w_router, seq)
    top_val, top_idx = lax.top_k(logits[:, :N_EXPERTS], TOP_K_EXPERTS)
    gates = jax.nn.softmax(top_val, axis=-1)

    n_assign = t * TOP_K_EXPERTS
    flat_e = top_idx.reshape(-1).astype(jnp.int32)
    onehot = (flat_e[:, None] == jnp.arange(N_EXPERTS, dtype=jnp.int32)[None, :]).astype(jnp.int32)
    rank = jnp.take_along_axis(jnp.cumsum(onehot, axis=0), flat_e[:, None], axis=1)[:, 0] - 1
    counts = jnp.sum(onehot, axis=0)
    padded = (counts + EXPERT_ROWS - 1) // EXPERT_ROWS * EXPERT_ROWS
    pad_end = jnp.cumsum(padded)
    dest = (pad_end - padded)[flat_e] + rank
    n_slots = (n_assign // EXPERT_ROWS + N_EXPERTS) * EXPERT_ROWS
    n_blocks = n_slots // EXPERT_ROWS
    slot_tok = jnp.zeros((n_slots,), jnp.int32).at[dest].set(jnp.arange(n_assign, dtype=jnp.int32) // TOP_K_EXPERTS)
    block_start = jnp.arange(n_blocks, dtype=jnp.int32) * EXPERT_ROWS
    block_expert = jnp.minimum(jnp.searchsorted(pad_end, block_start, side="right"), N_EXPERTS - 1).astype(jnp.int32)
    n_used = (pad_end[-1:] // EXPERT_ROWS).astype(jnp.int32)

    ys = _expert_ffn(h, slot_tok, block_expert, n_used, wg, wu, wd)
    dest2 = dest.reshape(t, TOP_K_EXPERTS)
    return _moe_combine(ys, dest2[:, 0], dest2[:, 1], gates, x2, mod, seq)


def _final_norm_kernel(x_ref, g_ref, o_ref):
    x = x_ref[...]
    o_ref[...] = x * lax.rsqrt(jnp.mean(x * x, axis=-1, keepdims=True) + NORM_EPS) * g_ref[...]


def _final_norm(x2, g_final):
    t, d = x2.shape
    tm = min(ROW_TILE, t)
    return pl.pallas_call(
        _final_norm_kernel,
        grid=(t // tm,),
        in_specs=[pl.BlockSpec((tm, d), lambda i: (i, 0)), pl.BlockSpec((1, d), lambda i: (0, 0))],
        out_specs=pl.BlockSpec((tm, d), lambda i: (i, 0)),
        out_shape=jax.ShapeDtypeStruct((t, d), F32),
        compiler_params=_params("arbitrary"),
        name="final_norm",
    )(x2, g_final.reshape(1, d))


def _token_mixer(x2, mod, g_mix, w_in, g_heads, w_out, batch, seq):
    pa, pb, pc, pd, p_idx, p_w, *pc_views = _in_projection(x2, mod, g_mix, _pack_w_in(w_in), seq)
    shape3 = lambda a: a.reshape(batch, -1, a.shape[-1])
    gh = g_heads.reshape(4, HEADS_PER_MIXER, HEAD_DIM)
    mixed = (_stick_breaking_mixer(shape3(pa), gh[0]),
             _dsa_mixer(shape3(pb), shape3(p_idx), shape3(p_w), gh[1]),
             _dilated_mixer([shape3(v) for v in [pc] + pc_views], gh[2]),
             _moba_mixer(shape3(pd), gh[3]))
    return _out_projection(mixed, w_out.astype(BF16), x2, mod, seq)


def kernel(x, c, w_ada, b_ada, g_mix, w_in, g_heads, w_out, g_ffn, w_ff_gate, w_ff_up, w_ff_down, w_router, w_exp_gate, w_exp_up, w_exp_down, g_final):
    batch, seq, d = x.shape
    depth = w_ada.shape[0]
    mods = _ada_modulation(c, w_ada, b_ada)
    x2 = x.reshape(batch * seq, d)
    for layer in range(depth):
        mod = mods[layer]
        x2 = _token_mixer(x2, mod, g_mix[layer], w_in[layer], g_heads[layer], w_out[layer], batch, seq)
        i = layer // 2
        if layer % 2 == 0:
            x2 = _dense_ffn(x2, mod, g_ffn[layer], w_ff_gate[i], w_ff_up[i], w_ff_down[i], seq)
        else:
            x2 = _moe_ffn(x2, mod, g_ffn[layer], w_router[i], w_exp_gate[i], w_exp_up[i], w_exp_down[i], seq)
    return _final_norm(x2, g_final).reshape(batch, seq, d)
```

```python
import functools

import numpy as np
import jax
import jax.numpy as jnp
from jax import lax
from jax.experimental import pallas as pl
from jax.experimental.pallas import tpu as pltpu

F32 = jnp.float32
BF16 = jnp.bfloat16

HEAD_DIM = 64
HEADS_PER_MIXER = 4
MIXER_WIDTH = HEADS_PER_MIXER * HEAD_DIM
QKV_WIDTH = 3 * MIXER_WIDTH
IDX_HEADS = 8
IDX_DIM = 64
DSA_TOPK = 256
DILATIONS = (1, 4, 16)
DILATED_STEPS = 128
MOBA_BLOCK = 256
MOBA_TOPK = 3
N_EXPERTS = 8
TOP_K_EXPERTS = 2
NORM_EPS = 1e-6
ATTN_SCALE = HEAD_DIM ** -0.5

LANES = 128
SUBLANES = 8
Q_TILE = 128
ROW_TILE = 512
FFN_TILE = 1408
EXPERT_ROWS = 512
EXPERT_FF_TILE = 1792
GATHER_ROWS = 512
ROWS_PER_TRIP = 8
VMEM_LIMIT = 56 * 1024 * 1024
MASKED = -1e30
SB_UNDERFLOW = 104.0
SB_FIRST_BLOCKS = 3

_NT = (((1,), (1,)), ((), ()))
_TN = (((0,), (0,)), ((), ()))


def _alibi_slopes(mixer_pos):
    idx = np.arange(HEADS_PER_MIXER, dtype=np.float32) * 3 + (mixer_pos + 1)
    return tuple(float(s) for s in np.exp2(-8.0 * idx / 12.0).astype(np.float32))


def _params(*semantics):
    return pltpu.CompilerParams(dimension_semantics=semantics, vmem_limit_bytes=VMEM_LIMIT)


def _modulated_norm(x, gain, shift, scale):
    y = x * lax.rsqrt(jnp.mean(x * x, axis=-1, keepdims=True) + NORM_EPS) * gain
    return y * (1.0 + scale) + shift


def _head_norm(acc, gain):
    return acc * lax.rsqrt(jnp.mean(acc * acc, axis=-1, keepdims=True) + NORM_EPS) * gain


def _head_cols(h):
    return slice(h * HEAD_DIM, (h + 1) * HEAD_DIM)


def _paired_loop(n, body, carry):
    carry = lax.fori_loop(0, n // 2, lambda jj, c: body(2 * jj + 1, body(2 * jj, c, 0), 1), carry)
    return lax.cond(n % 2 == 1, lambda c: body(n - 1, c, 0), lambda c: c, carry)


def _ada_kernel(c_ref, w_ref, b_ref, o_ref):
    c = c_ref[...]
    cond = c / (1.0 + jnp.exp(-c))
    o_ref[0, 0] = jnp.dot(cond, w_ref[0], preferred_element_type=F32,
                          precision=lax.Precision.HIGHEST) + b_ref[0, 0]


def _ada_modulation(c, w_ada, b_ada):
    depth, d, _ = w_ada.shape
    b = c.shape[0]
    out = pl.pallas_call(
        _ada_kernel,
        grid=(depth, 6),
        in_specs=[pl.BlockSpec((b, d), lambda l, k: (0, 0)),
                  pl.BlockSpec((1, d, d), lambda l, k: (l, 0, k)),
                  pl.BlockSpec((1, 1, 1, d), lambda l, k: (l, k, 0, 0))],
        out_specs=pl.BlockSpec((1, 1, b, d), lambda l, k: (l, k, 0, 0)),
        out_shape=jax.ShapeDtypeStruct((depth, 6, b, d), F32),
        compiler_params=_params("arbitrary", "arbitrary"),
        name="ada_modulation",
    )(c, w_ada, b_ada.reshape(depth, 6, 1, d))
    return out.transpose(0, 2, 1, 3)


IN_WIDTHS = (QKV_WIDTH, QKV_WIDTH, QKV_WIDTH, QKV_WIDTH, IDX_HEADS * IDX_DIM + LANES, LANES)


def _in_proj_kernel(x_ref, mod_ref, g_ref, w_ref, oa, ob, oc, od, oidx, ow, oc4, oc16, pc_ref):
    h = _modulated_norm(x_ref[...], g_ref[...], mod_ref[0, 0:1, :], mod_ref[0, 1:2, :]).astype(BF16)
    tm = x_ref.shape[0]
    off = 0
    for o_ref, width in zip((oa, ob, oc, od, oidx, ow), IN_WIDTHS):
        val = jnp.dot(h, w_ref[:, off:off + width], preferred_element_type=F32)
        o_ref[...] = val.astype(o_ref.dtype)
        if o_ref is oc:
            for s in range(QKV_WIDTH // LANES):
                pc_ref[s * tm:(s + 1) * tm, :] = val[:, s * LANES:(s + 1) * LANES]
        off += width
    for r, view in zip(DILATIONS[1:], (oc4, oc16)):
        for c in range(r):
            for s in range(QKV_WIDTH // LANES):
                col = c * QKV_WIDTH + s * LANES
                view[:, col:col + LANES] = pc_ref[pl.ds(s * tm + c, tm // r, stride=r), :].astype(view.dtype)


def _pack_w_in(w_in):
    d = w_in.shape[0]
    n_qkv = 4 * QKV_WIDTH
    n_qi = IDX_HEADS * IDX_DIM
    z = lambda n: jnp.zeros((d, n), w_in.dtype)
    return jnp.concatenate([w_in[:, :n_qkv + n_qi + IDX_DIM], z(LANES - IDX_DIM),
                            w_in[:, n_qkv + n_qi + IDX_DIM:], z(LANES - IDX_HEADS)], axis=1).astype(BF16)


def _in_projection(x2, mod, g_mix, w_packed, seq):
    t, d = x2.shape
    tm = min(ROW_TILE, seq)
    per_batch = seq // tm
    dtypes = (BF16, BF16, BF16, BF16, BF16, F32)
    views = DILATIONS[1:]
    return pl.pallas_call(
        _in_proj_kernel,
        grid=(t // tm,),
        in_specs=[pl.BlockSpec((tm, d), lambda i: (i, 0)),
                  pl.BlockSpec((1, 6, d), lambda i: (i // per_batch, 0, 0)),
                  pl.BlockSpec((1, d), lambda i: (0, 0)),
                  pl.BlockSpec(w_packed.shape, lambda i: (0, 0))],
        out_specs=([pl.BlockSpec((tm, w), lambda i: (i, 0)) for w in IN_WIDTHS]
                   + [pl.BlockSpec((tm // r, r * QKV_WIDTH), lambda i: (i, 0)) for r in views]),
        out_shape=([jax.ShapeDtypeStruct((t, w), dt) for w, dt in zip(IN_WIDTHS, dtypes)]
                   + [jax.ShapeDtypeStruct((t // r, r * QKV_WIDTH), BF16) for r in views]),
        scratch_shapes=[pltpu.VMEM((tm * (QKV_WIDTH // LANES), LANES), F32)],
        compiler_params=_params("arbitrary"),
        name="in_projection",
    )(x2, mod, g_mix.reshape(1, d), w_packed)


def _sb_kernel(q_ref, k_ref, v_ref, g_ref, o_ref):
    tq = q_ref.shape[1]
    i = pl.program_id(1)
    row = lax.broadcasted_iota(jnp.int32, (tq, 1), 0)
    lane = lax.broadcasted_iota(jnp.int32, (1, tq), 1)
    later = (lax.broadcasted_iota(jnp.int32, (tq, tq), 0) > lax.broadcasted_iota(jnp.int32, (tq, tq), 1))
    later = jnp.where(later, 1.0, 0.0).astype(BF16)

    qs = [q_ref[0, :, _head_cols(h)] * ATTN_SCALE for h in range(HEADS_PER_MIXER)]

    heads = range(HEADS_PER_MIXER)

    def blocks(starts, keeps, tails):
        z = jnp.concatenate([lax.dot_general(qs[h], k_ref[0, pl.ds(st, tq), _head_cols(h)], _NT,
                                             preferred_element_type=F32) for st in starts for h in heads], axis=0)
        softplus = jnp.maximum(z, 0.0) + jnp.log(1.0 + jnp.exp(-jnp.abs(z)))
        keep = None
        if any(kp is not None for kp in keeps):
            ones = jnp.ones((tq, tq), F32)
            keep = jnp.concatenate([ones if kp is None else kp for kp in keeps for _ in heads], axis=0)
        log_1m = -softplus if keep is None else -softplus * keep
        hi = log_1m.astype(BF16)
        lo = (log_1m - hi.astype(F32)).astype(BF16)
        inside = jnp.dot(hi, later, preferred_element_type=F32) + jnp.dot(lo, later, preferred_element_type=F32)
        block_sum = jnp.sum(log_1m, axis=1, keepdims=True)
        piece = lambda x, b, h: x[(b * HEADS_PER_MIXER + h) * tq:(b * HEADS_PER_MIXER + h + 1) * tq]
        tail_cols = []
        tails = list(tails)
        for b in range(len(starts)):
            tail_cols += tails
            tails = [tails[h] + piece(block_sum, b, h) for h in heads]
        a = jnp.exp(z - softplus + inside + jnp.concatenate(tail_cols, axis=0))
        if keep is not None:
            a = a * keep
        a = a.astype(BF16)
        av = [sum(jnp.dot(piece(a, b, h), v_ref[0, pl.ds(st, tq), _head_cols(h)], preferred_element_type=F32)
                  for b, st in enumerate(starts)) for h in heads]
        return tails, av

    starts = [pl.multiple_of(jnp.maximum(i - n, 0) * tq, tq) for n in range(SB_FIRST_BLOCKS)]
    keeps = [jnp.where(lane < row, 1.0, 0.0)]
    keeps += [jnp.full((tq, tq), jnp.where(i >= n, 1.0, 0.0), F32) for n in range(1, SB_FIRST_BLOCKS)]
    tails, accs = blocks(starts, keeps, [jnp.zeros((tq, 1), F32)] * HEADS_PER_MIXER)
    state = tuple(zip(tails, accs))

    def body(carry):
        j, state = carry
        tails, av = blocks([pl.multiple_of(j * tq, tq)], [None], [tail for tail, _ in state])
        return j - 1, tuple((tails[h], state[h][1] + av[h]) for h in heads)

    def cond(carry):
        j, state = carry
        worst = functools.reduce(jnp.maximum, [tail for tail, _ in state])
        return jnp.logical_and(j >= 0, jnp.max(worst) > -SB_UNDERFLOW)

    _, state = lax.while_loop(cond, body, (i - SB_FIRST_BLOCKS, state))
    for h in range(HEADS_PER_MIXER):
        o_ref[0, :, _head_cols(h)] = _head_norm(state[h][1], g_ref[h:h + 1, :]).astype(o_ref.dtype)


def _stick_breaking_mixer(p, g_heads):
    b, seq, _ = p.shape
    tq = min(Q_TILE, seq)
    return pl.pallas_call(
        _sb_kernel,
        grid=(b, seq // tq),
        in_specs=[pl.BlockSpec((1, tq, MIXER_WIDTH), lambda bi, i: (bi, i, 0)),
                  pl.BlockSpec((1, seq, MIXER_WIDTH), lambda bi, i: (bi, 0, 1)),
                  pl.BlockSpec((1, seq, MIXER_WIDTH), lambda bi, i: (bi, 0, 2)),
                  pl.BlockSpec((HEADS_PER_MIXER, HEAD_DIM), lambda bi, i: (0, 0))],
        out_specs=pl.BlockSpec((1, tq, MIXER_WIDTH), lambda bi, i: (bi, i, 0)),
        out_shape=jax.ShapeDtypeStruct((b, seq, MIXER_WIDTH), BF16),
        compiler_params=_params("arbitrary", "arbitrary"),
        name="stick_breaking",
    )(p, p, p, g_heads)


def _dsa_kernel(q_ref, k_ref, v_ref, qi_ref, ki_ref, w_ref, g_ref, o_ref, sc_ref, sc16_ref, s_ref, p_ref, *, slopes):
    tq = q_ref.shape[1]
    tk = sc_ref.shape[1]
    i = pl.program_id(1)
    n_blocks = ((i + 1) * tq + tk - 1) // tk
    qpos = i * tq + lax.broadcasted_iota(jnp.int32, (1, tq), 1)
    key_row = lax.broadcasted_iota(jnp.int32, (tk, 1), 0)
    neg_inf = float("-inf")

    w_t = w_ref[0].T * (IDX_HEADS ** -0.5 * IDX_DIM ** -0.5)
    w_rows = [w_t[h:h + 1, :] for h in range(IDX_HEADS)]
    qi = qi_ref[0]

    def top_half(x):
        bits = lax.bitcast_convert_type(x, jnp.int32) & jnp.int32(-65536)
        return lax.bitcast_convert_type(bits, F32).astype(BF16)

    def score_body(j, carry, slot):
        start = pl.multiple_of(j * tk, tk)
        ki = ki_ref[0, pl.ds(start, tk), 0:IDX_DIM]
        sc = jnp.zeros((tk, tq), F32)
        for h in range(IDX_HEADS):
            x = lax.dot_general(ki, qi[:, h * IDX_DIM:(h + 1) * IDX_DIM], _NT, preferred_element_type=F32)
            sc = sc + w_rows[h] * jnp.maximum(x, 0.0)
        sc = jnp.where((start + key_row) <= qpos, sc + 0.0, neg_inf)
        sc_ref[j] = sc
        sc16_ref[j] = top_half(sc)
        return carry

    _paired_loop(n_blocks, score_body, 0)

    def count(pred):
        def add_block(j, cnt, slot):
            c = jnp.where(pred(sc_ref[j]), 1.0, 0.0)
            return cnt + jnp.sum(c.reshape(tk // 8, 8, tq), axis=0)

        cnt = _paired_loop(n_blocks, add_block, jnp.zeros((8, tq), F32))
        return jnp.sum(cnt, axis=0, keepdims=True)

    int_min = jnp.int32(-2 ** 31)

    def ordered_to_float(u):
        key = u ^ int_min
        bits = jnp.where(key >= 0, key, key ^ jnp.int32(0x7FFFFFFF))
        return lax.bitcast_convert_type(bits, F32)

    def count_top_half(cand16):
        def add_block(j, cnt, slot):
            c = jnp.where(sc16_ref[j] >= cand16, jnp.ones((), BF16), jnp.zeros((), BF16))
            for g in range(tk // 16):
                cnt = cnt + c[g * 16:(g + 1) * 16]
            return cnt

        cnt = _paired_loop(n_blocks, add_block, jnp.zeros((16, tq), BF16))
        return jnp.sum(cnt.astype(F32), axis=0, keepdims=True)

    def search_body(step, carry, half):
        prefix, n_at_prefix = carry
        cand = prefix | jnp.left_shift(jnp.int32(1), 31 - step)
        cand_f = ordered_to_float(cand)
        n_ge = count_top_half(top_half(cand_f)) if half else count(lambda s: s >= cand_f)
        keep = n_ge >= DSA_TOPK
        return jnp.where(keep, cand, prefix), jnp.where(keep, n_ge, n_at_prefix)

    carry = (jnp.zeros((1, tq), jnp.int32), jnp.zeros((1, tq), F32))
    carry = lax.fori_loop(0, 16, functools.partial(search_body, half=True), carry)
    prefix, n_ge_tau = lax.fori_loop(16, 32, functools.partial(search_body, half=False), carry)
    keep_all = (qpos + 1) <= DSA_TOPK
    tau = jnp.where(keep_all, float(np.finfo(np.float32).min), ordered_to_float(prefix))
    no_ties = jnp.min(jnp.where(jnp.logical_or(keep_all, n_ge_tau == DSA_TOPK), 1.0, 0.0)) > 0.5

    earlier = (lax.broadcasted_iota(jnp.int32, (tk, tk), 1) < lax.broadcasted_iota(jnp.int32, (tk, tk), 0))
    earlier = jnp.where(earlier, 1.0, 0.0).astype(BF16)
    qs = [q_ref[0, :, _head_cols(h)] * ATTN_SCALE for h in range(HEADS_PER_MIXER)]
    key_bias = [slopes[h] * lax.broadcasted_iota(jnp.int32, (tk, tq), 0).astype(F32)
                for h in range(HEADS_PER_MIXER)]

    def attn_body(j, carry, slot, n_ties=None):
        ties_seen, stats = carry
        s_slot = [s_ref.at[slot * HEADS_PER_MIXER + h] for h in range(HEADS_PER_MIXER)]
        p_slot = [p_ref.at[slot * HEADS_PER_MIXER + h] for h in range(HEADS_PER_MIXER)]
        start = pl.multiple_of(j * tk, tk)
        sc = sc_ref[j]
        if n_ties is None:
            sel = sc >= tau
        else:
            tie = jnp.where(sc == tau, 1.0, 0.0)
            rank = jnp.dot(earlier, tie.astype(BF16), preferred_element_type=F32) + ties_seen
            sel = jnp.where(sc > tau, 1.0, jnp.where(rank < n_ties, tie, 0.0)) > 0.5
            ties_seen = ties_seen + jnp.sum(tie, axis=0, keepdims=True)
        block_pos = (j * tk - i * tq).astype(F32)
        for h in range(HEADS_PER_MIXER):
            k = k_ref[0, pl.ds(start, tk), _head_cols(h)]
            s = lax.dot_general(k, qs[h], _NT, preferred_element_type=F32) + key_bias[h]
            s_slot[h][...] = jnp.where(sel, s, MASKED)
        locals_ = []
        for h in range(HEADS_PER_MIXER):
            s = s_slot[h][...]
            m_loc = jnp.max(s, axis=0, keepdims=True)
            p = jnp.exp(s - m_loc)
            locals_.append((m_loc, jnp.sum(p, axis=0, keepdims=True)))
            p_slot[h][...] = p.astype(BF16)
        new_stats = []
        for h in range(HEADS_PER_MIXER):
            m, l, acc = stats[h]
            m_loc, l_blk = locals_[h]
            v = v_ref[0, pl.ds(start, tk), _head_cols(h)]
            pv_blk = lax.dot_general(v, p_slot[h][...], _TN, preferred_element_type=F32)
            m_blk = m_loc + slopes[h] * block_pos
            m_new = jnp.maximum(m, m_blk)
            w_old = jnp.exp(m - m_new)
            w_blk = jnp.where(m_loc > 0.5 * MASKED, jnp.exp(m_blk - m_new), 0.0)
            new_stats.append((m_new, w_old * l + w_blk * l_blk, w_old * acc + w_blk * pv_blk))
        return ties_seen, tuple(new_stats)

    init = tuple((jnp.full((1, tq), MASKED, F32), jnp.zeros((1, tq), F32), jnp.zeros((HEAD_DIM, tq), F32))
                 for _ in range(HEADS_PER_MIXER))
    start_carry = (jnp.zeros((1, tq), F32), init)

    def attend_without_ties():
        return _paired_loop(n_blocks, attn_body, start_carry)[1]

    def attend_with_ties():
        n_gt = count(lambda s: s > tau)
        n_ties = jnp.where(keep_all, 1e9, DSA_TOPK - n_gt)
        return _paired_loop(n_blocks, functools.partial(attn_body, n_ties=n_ties), start_carry)[1]

    stats = lax.cond(no_ties, attend_without_ties, attend_with_ties)
    outs = []
    for h in range(HEADS_PER_MIXER):
        _, l, acc = stats[h]
        o = acc / l
        outs.append(o * lax.rsqrt(jnp.mean(o * o, axis=0, keepdims=True) + NORM_EPS) * g_ref[:, h:h + 1])
    o_ref[0] = jnp.concatenate(outs, axis=0).T.astype(o_ref.dtype)


def _dsa_mixer(p, p_idx, p_w, g_heads):
    b, seq, _ = p.shape
    tq = min(4 * Q_TILE, seq)
    tk = tq
    assert seq // 16 <= 256, "the packed-bf16 count accumulators are exact only up to 256 adds"
    qi_width = IDX_HEADS * IDX_DIM
    return pl.pallas_call(
        functools.partial(_dsa_kernel, slopes=_alibi_slopes(0)),
        grid=(b, seq // tq),
        in_specs=[pl.BlockSpec((1, tq, MIXER_WIDTH), lambda bi, i: (bi, i, 0)),
                  pl.BlockSpec((1, seq, MIXER_WIDTH), lambda bi, i: (bi, 0, 1)),
                  pl.BlockSpec((1, seq, MIXER_WIDTH), lambda bi, i: (bi, 0, 2)),
                  pl.BlockSpec((1, tq, qi_width), lambda bi, i: (bi, i, 0)),
                  pl.BlockSpec((1, seq, LANES), lambda bi, i: (bi, 0, qi_width // LANES)),
                  pl.BlockSpec((1, tq, LANES), lambda bi, i: (bi, i, 0)),
                  pl.BlockSpec((HEAD_DIM, HEADS_PER_MIXER), lambda bi, i: (0, 0))],
        out_specs=pl.BlockSpec((1, tq, MIXER_WIDTH), lambda bi, i: (bi, i, 0)),
        out_shape=jax.ShapeDtypeStruct((b, seq, MIXER_WIDTH), BF16),
        scratch_shapes=[pltpu.VMEM((seq // tk, tk, tq), F32),
                        pltpu.VMEM((seq // tk, tk, tq), BF16),
                        pltpu.VMEM((2 * HEADS_PER_MIXER, tk, tq), F32),
                        pltpu.VMEM((2 * HEADS_PER_MIXER, tk, tq), BF16)],
        compiler_params=_params("arbitrary", "arbitrary"),
        name="dsa",
    )(p, p, p, p_idx, p_idx, p_w, g_heads.T)


def _band_kernel(q_ref, kp_ref, kc_ref, vp_ref, vc_ref, o_ref, lse_ref, *, dilation, slopes):
    tq = q_ref.shape[1]
    ui = pl.program_id(2)
    heads = range(HEADS_PER_MIXER)
    u_q = jnp.concatenate([ui * tq + lax.broadcasted_iota(jnp.int32, (tq, 1), 0)] * HEADS_PER_MIXER, axis=0)
    u_k = ui * tq - DILATED_STEPS + lax.broadcasted_iota(jnp.int32, (1, DILATED_STEPS + tq), 1)
    steps = u_q - u_k
    valid = jnp.logical_and(jnp.logical_and(steps >= 0, steps <= DILATED_STEPS), u_k >= 0)
    slope_col = jnp.concatenate([jnp.full((tq, 1), slopes[h], F32) for h in heads], axis=0)
    s = jnp.concatenate(
        [lax.dot_general(q_ref[0, :, _head_cols(h)],
                         jnp.concatenate([kp_ref[0, :, _head_cols(h)], kc_ref[0, :, _head_cols(h)]], axis=0),
                         _NT, preferred_element_type=F32) for h in heads], axis=0)
    s = jnp.where(valid, s * ATTN_SCALE - slope_col * (steps * dilation).astype(F32), MASKED)
    m = jnp.max(s, axis=1, keepdims=True)
    e = jnp.exp(s - m)
    den = jnp.sum(e, axis=1, keepdims=True)
    p = e.astype(BF16)
    lse = m + jnp.log(den)
    lane = lax.broadcasted_iota(jnp.int32, (1, LANES), 1)
    lse_all = jnp.zeros((tq, LANES), F32)
    for h in heads:
        rows = slice(h * tq, (h + 1) * tq)
        v = jnp.concatenate([vp_ref[0, :, _head_cols(h)], vc_ref[0, :, _head_cols(h)]], axis=0)
        o_ref[0, :, _head_cols(h)] = jnp.dot(p[rows], v, preferred_element_type=F32) / den[rows]
        lse_all = lse_all + jnp.where(lane == h, lse[rows], 0.0)
    lse_ref[0] = lse_all


def _dilated_branch(view, dilation, slopes):
    b, length, _ = view.shape
    classes = dilation
    tq = min(2 * Q_TILE, length)
    back = tq // DILATED_STEPS
    spec = lambda part, prev: (
        pl.BlockSpec((1, DILATED_STEPS, MIXER_WIDTH),
                     lambda bi, c, ui: (bi, jnp.maximum(ui * back - 1, 0), c * 3 + part)) if prev
        else pl.BlockSpec((1, tq, MIXER_WIDTH), lambda bi, c, ui: (bi, ui, c * 3 + part)))
    out, lse = pl.pallas_call(
        functools.partial(_band_kernel, dilation=dilation, slopes=slopes),
        grid=(b, classes, length // tq),
        in_specs=[spec(0, False), spec(1, True), spec(1, False), spec(2, True), spec(2, False)],
        out_specs=[pl.BlockSpec((1, tq, MIXER_WIDTH), lambda bi, c, ui: (bi, ui, c)),
                   pl.BlockSpec((1, tq, LANES), lambda bi, c, ui: (bi, ui, c))],
        out_shape=[jax.ShapeDtypeStruct((b, length, classes * MIXER_WIDTH), F32),
                   jax.ShapeDtypeStruct((b, length, classes * LANES), F32)],
        compiler_params=_params("arbitrary", "arbitrary", "arbitrary"),
        name=f"dilated_r{dilation}",
    )(view, view, view, view, view)
    return out, lse


def _dilated_merge_kernel(o1, o2, o3, l1, l2, l3, g_ref, o_ref, *scratch):
    tq = o_ref.shape[1]
    chunks = MIXER_WIDTH // LANES
    outs = [[o1[0, :, s * LANES:(s + 1) * LANES] for s in range(chunks)]]
    lses = [l1[0]]
    for r, o_view, l_view, o_nat, l_nat in zip(DILATIONS[1:], (o2, o3), (l2, l3), scratch[0::2], scratch[1::2]):
        for c in range(r):
            for s in range(chunks):
                col = c * MIXER_WIDTH + s * LANES
                o_nat[pl.ds(s * tq + c, tq // r, stride=r), :] = o_view[0, :, col:col + LANES]
            l_nat[pl.ds(c, tq // r, stride=r), :] = l_view[0, :, c * LANES:(c + 1) * LANES]
        outs.append([o_nat[s * tq:(s + 1) * tq, :] for s in range(chunks)])
        lses.append(l_nat[...])
    heads_per_chunk = LANES // HEAD_DIM
    for h in range(HEADS_PER_MIXER):
        sub = slice((h % heads_per_chunk) * HEAD_DIM, (h % heads_per_chunk + 1) * HEAD_DIM)
        lse_h = [l[:, h:h + 1] for l in lses]
        top = functools.reduce(jnp.maximum, lse_h)
        wts = [jnp.exp(l - top) for l in lse_h]
        mixed = sum(w * o[h // heads_per_chunk][:, sub] for w, o in zip(wts, outs)) / sum(wts)
        o_ref[0, :, _head_cols(h)] = _head_norm(mixed, g_ref[h:h + 1, :]).astype(o_ref.dtype)


def _dilated_mixer(views, g_heads):
    b, seq, _ = views[0].shape
    slopes = _alibi_slopes(1)
    branches = [_dilated_branch(v, r, slopes) for v, r in zip(views, DILATIONS)]
    tq = min(ROW_TILE, seq)
    o_specs = [pl.BlockSpec((1, tq // r, r * MIXER_WIDTH), lambda bi, i: (bi, i, 0)) for r in DILATIONS]
    l_specs = [pl.BlockSpec((1, tq // r, r * LANES), lambda bi, i: (bi, i, 0)) for r in DILATIONS]
    scratch = []
    for _ in DILATIONS[1:]:
        scratch += [pltpu.VMEM((tq * (MIXER_WIDTH // LANES), LANES), F32), pltpu.VMEM((tq, LANES), F32)]
    return pl.pallas_call(
        _dilated_merge_kernel,
        grid=(b, seq // tq),
        in_specs=o_specs + l_specs + [pl.BlockSpec((HEADS_PER_MIXER, HEAD_DIM), lambda bi, i: (0, 0))],
        out_specs=pl.BlockSpec((1, tq, MIXER_WIDTH), lambda bi, i: (bi, i, 0)),
        out_shape=jax.ShapeDtypeStruct((b, seq, MIXER_WIDTH), BF16),
        scratch_shapes=scratch,
        compiler_params=_params("arbitrary", "arbitrary"),
        name="dilated_merge",
    )(*[o for o, _ in branches], *[l for _, l in branches], g_heads)


def _moba_kernel(q_ref, k_ref, v_ref, g_ref, o_ref, kmean_ref, chosen_ref, s_ref, p_ref, *, slopes):
    tq = q_ref.shape[1]
    n_kv = k_ref.shape[1] // MOBA_BLOCK
    own = pl.program_id(1)

    @pl.when(own == 0)
    def _():
        kmean_ref[...] = jnp.zeros_like(kmean_ref)
        for n in range(n_kv):
            blk = k_ref[0, n * MOBA_BLOCK:(n + 1) * MOBA_BLOCK, :].astype(F32)
            kmean_ref[n:n + 1, :] = jnp.sum(blk, axis=0, keepdims=True) * (1.0 / MOBA_BLOCK)

    key_row = lax.broadcasted_iota(jnp.int32, (MOBA_BLOCK, 1), 0)
    q_lane = lax.broadcasted_iota(jnp.int32, (1, tq), 1)
    blk_rows = kmean_ref.shape[0]
    blk_row = lax.broadcasted_iota(jnp.int32, (blk_rows, 1), 0)
    blk_f = blk_row.astype(F32)
    neg_inf = float("-inf")

    qs = []
    for h in range(HEADS_PER_MIXER):
        q = q_ref[0, :, _head_cols(h)]
        gate = lax.dot_general(kmean_ref[:, _head_cols(h)], q.astype(F32), _NT, preferred_element_type=F32,
                               precision=lax.Precision.HIGHEST)
        gate = jnp.where(blk_row < own, gate, neg_inf)
        picks = jnp.zeros((blk_rows, tq), F32)
        for _ in range(MOBA_TOPK):
            top = jnp.max(gate, axis=0, keepdims=True)
            is_top = jnp.logical_and(gate == top, top > neg_inf)
            first = jnp.min(jnp.where(is_top, blk_f, float(blk_rows)), axis=0, keepdims=True)
            pick = blk_f == first
            picks = jnp.where(pick, 1.0, picks)
            gate = jnp.where(pick, neg_inf, gate)
        chosen_ref[h] = picks
        qs.append(q * ATTN_SCALE)

    key_bias = [slopes[h] * lax.broadcasted_iota(jnp.int32, (MOBA_BLOCK, tq), 0).astype(F32)
                for h in range(HEADS_PER_MIXER)]

    def block_softmax(start, keep=None, slot=0):
        s_slot = [s_ref.at[slot * HEADS_PER_MIXER + h] for h in range(HEADS_PER_MIXER)]
        p_slot = [p_ref.at[slot * HEADS_PER_MIXER + h] for h in range(HEADS_PER_MIXER)]
        for h in range(HEADS_PER_MIXER):
            k = k_ref[0, pl.ds(start, MOBA_BLOCK), _head_cols(h)]
            s = lax.dot_general(k, qs[h], _NT, preferred_element_type=F32) + key_bias[h]
            s_slot[h][...] = s if keep is None else jnp.where(keep, s, MASKED)
        pieces = []
        for h in range(HEADS_PER_MIXER):
            s = s_slot[h][...]
            m_loc = jnp.max(s, axis=0, keepdims=True)
            p = jnp.exp(s - m_loc)
            pieces.append((m_loc, jnp.sum(p, axis=0, keepdims=True)))
            p_slot[h][...] = p.astype(BF16)
        out = []
        for h in range(HEADS_PER_MIXER):
            v = v_ref[0, pl.ds(start, MOBA_BLOCK), _head_cols(h)]
            out.append(pieces[h] + (lax.dot_general(v, p_slot[h][...], _TN, preferred_element_type=F32),))
        return out

    own_start = pl.multiple_of(own * MOBA_BLOCK, MOBA_BLOCK)
    stats = tuple(block_softmax(own_start, keep=key_row <= q_lane))

    def body(j, stats, slot):
        start = pl.multiple_of(j * MOBA_BLOCK, MOBA_BLOCK)
        block_pos = ((j - own) * MOBA_BLOCK).astype(F32)
        merged = []
        for h, (m_loc, l_blk, pv_blk) in enumerate(block_softmax(start, slot=slot)):
            picked = chosen_ref[h, pl.ds(j, 1), :] > 0.5
            m_blk = jnp.where(picked, m_loc + slopes[h] * block_pos, MASKED)
            m, l, acc = stats[h]
            m_new = jnp.maximum(m, m_blk)
            w_old = jnp.exp(m - m_new)
            w_blk = jnp.exp(m_blk - m_new)
            merged.append((m_new, w_old * l + w_blk * l_blk, w_old * acc + w_blk * pv_blk))
        return tuple(merged)

    stats = _paired_loop(own, body, stats)
    outs = []
    for h in range(HEADS_PER_MIXER):
        _, l, acc = stats[h]
        o = acc / l
        outs.append(o * lax.rsqrt(jnp.mean(o * o, axis=0, keepdims=True) + NORM_EPS) * g_ref[:, h:h + 1])
    o_ref[0] = jnp.concatenate(outs, axis=0).T.astype(o_ref.dtype)


def _moba_mixer(p, g_heads):
    b, seq, _ = p.shape
    tq = MOBA_BLOCK
    blk_rows = -(-(seq // MOBA_BLOCK) // SUBLANES) * SUBLANES
    return pl.pallas_call(
        functools.partial(_moba_kernel, slopes=_alibi_slopes(2)),
        grid=(b, seq // tq),
        in_specs=[pl.BlockSpec((1, tq, MIXER_WIDTH), lambda bi, i: (bi, i, 0)),
                  pl.BlockSpec((1, seq, MIXER_WIDTH), lambda bi, i: (bi, 0, 1)),
                  pl.BlockSpec((1, seq, MIXER_WIDTH), lambda bi, i: (bi, 0, 2)),
                  pl.BlockSpec((HEAD_DIM, HEADS_PER_MIXER), lambda bi, i: (0, 0))],
        out_specs=pl.BlockSpec((1, tq, MIXER_WIDTH), lambda bi, i: (bi, i, 0)),
        out_shape=jax.ShapeDtypeStruct((b, seq, MIXER_WIDTH), BF16),
        scratch_shapes=[pltpu.VMEM((blk_rows, MIXER_WIDTH), F32),
                        pltpu.VMEM((HEADS_PER_MIXER, blk_rows, tq), F32),
                        pltpu.VMEM((2 * HEADS_PER_MIXER, MOBA_BLOCK, tq), F32),
                        pltpu.VMEM((2 * HEADS_PER_MIXER, MOBA_BLOCK, tq), BF16)],
        compiler_params=_params("arbitrary", "arbitrary"),
        name="moba",
    )(p, p, p, g_heads.T)


def _out_proj_kernel(oa, ob, oc, od, w_ref, x_ref, mod_ref, o_ref):
    acc = jnp.zeros(x_ref.shape, F32)
    for m, o in enumerate((oa, ob, oc, od)):
        acc = acc + jnp.dot(o[...], w_ref[m * MIXER_WIDTH:(m + 1) * MIXER_WIDTH, :], preferred_element_type=F32)
    o_ref[...] = x_ref[...] + mod_ref[0, 2:3, :] * acc


def _out_projection(mixed, w_out, x2, mod, seq):
    t, d = x2.shape
    tm = min(ROW_TILE, seq)
    per_batch = seq // tm
    o_spec = pl.BlockSpec((tm, MIXER_WIDTH), lambda i: (i, 0))
    return pl.pallas_call(
        _out_proj_kernel,
        grid=(t // tm,),
        in_specs=[o_spec] * 4 + [pl.BlockSpec(w_out.shape, lambda i: (0, 0)),
                                 pl.BlockSpec((tm, d), lambda i: (i, 0)),
                                 pl.BlockSpec((1, 6, d), lambda i: (i // per_batch, 0, 0))],
        out_specs=pl.BlockSpec((tm, d), lambda i: (i, 0)),
        out_shape=jax.ShapeDtypeStruct((t, d), F32),
        compiler_params=_params("arbitrary"),
        name="out_projection",
    )(*[o.reshape(t, MIXER_WIDTH) for o in mixed], w_out, x2, mod)


def _ffn_kernel(x_ref, mod_ref, g_ref, wg_ref, wu_ref, wd_ref, o_ref, h_ref, acc_ref):
    f = pl.program_id(1)

    @pl.when(f == 0)
    def _():
        h_ref[...] = _modulated_norm(x_ref[...], g_ref[...], mod_ref[0, 3:4, :], mod_ref[0, 4:5, :]).astype(BF16)
        acc_ref[...] = jnp.zeros_like(acc_ref)

    h = h_ref[...]
    gate = jnp.dot(h, wg_ref[...], preferred_element_type=F32)
    up = jnp.dot(h, wu_ref[...], preferred_element_type=F32)
    act = (gate / (1.0 + jnp.exp(-gate)) * up).astype(BF16)
    acc_ref[...] += jnp.dot(act, wd_ref[...], preferred_element_type=F32)

    @pl.when(f == pl.num_programs(1) - 1)
    def _():
        o_ref[...] = x_ref[...] + mod_ref[0, 5:6, :] * acc_ref[...]


def _dense_ffn(x2, mod, g_ffn, wg, wu, wd, seq):
    t, d = x2.shape
    d_ff = wg.shape[1]
    tm = min(ROW_TILE, seq)
    tf = FFN_TILE if d_ff % FFN_TILE == 0 else d_ff
    per_batch = seq // tm
    return pl.pallas_call(
        _ffn_kernel,
        grid=(t // tm, d_ff // tf),
        in_specs=[pl.BlockSpec((tm, d), lambda i, f: (i, 0)),
                  pl.BlockSpec((1, 6, d), lambda i, f: (i // per_batch, 0, 0)),
                  pl.BlockSpec((1, d), lambda i, f: (0, 0)),
                  pl.BlockSpec((d, tf), lambda i, f: (0, f)),
                  pl.BlockSpec((d, tf), lambda i, f: (0, f)),
                  pl.BlockSpec((tf, d), lambda i, f: (f, 0))],
        out_specs=pl.BlockSpec((tm, d), lambda i, f: (i, 0)),
        out_shape=jax.ShapeDtypeStruct((t, d), F32),
        scratch_shapes=[pltpu.VMEM((tm, d), BF16), pltpu.VMEM((tm, d), F32)],
        compiler_params=_params("arbitrary", "arbitrary"),
        name="dense_ffn",
    )(x2, mod, g_ffn.reshape(1, d), wg.astype(BF16), wu.astype(BF16), wd.astype(BF16))


def _store_row_tiles(dst_ref, value):
    rows, d = value.shape
    chunks = d // LANES
    for s in range(chunks):
        dst_ref[pl.ds(s, rows, stride=chunks), :] = value[:, s * LANES:(s + 1) * LANES]


def _load_row_tile_chunk(src_ref, s, rows, chunks):
    return src_ref[pl.ds(s, rows, stride=chunks), :]


def _router_kernel(x_ref, mod_ref, g_ref, wr_ref, h_ref, logit_ref):
    h = _modulated_norm(x_ref[...], g_ref[...], mod_ref[0, 3:4, :], mod_ref[0, 4:5, :])
    _store_row_tiles(h_ref, h)
    logit_ref[...] = jnp.dot(h, wr_ref[...], preferred_element_type=F32, precision=lax.Precision.HIGHEST)


def _router(x2, mod, g_ffn, w_router, seq):
    t, d = x2.shape
    tm = min(ROW_TILE, seq)
    per_batch = seq // tm
    wr = jnp.zeros((d, LANES), F32).at[:, :N_EXPERTS].set(w_router.astype(F32))
    return pl.pallas_call(
        _router_kernel,
        grid=(t // tm,),
        in_specs=[pl.BlockSpec((tm, d), lambda i: (i, 0)),
                  pl.BlockSpec((1, 6, d), lambda i: (i // per_batch, 0, 0)),
                  pl.BlockSpec((1, d), lambda i: (0, 0)),
                  pl.BlockSpec((d, LANES), lambda i: (0, 0))],
        out_specs=[pl.BlockSpec((tm * (d // LANES), LANES), lambda i: (i, 0)),
                   pl.BlockSpec((tm, LANES), lambda i: (i, 0))],
        out_shape=[jax.ShapeDtypeStruct((t * (d // LANES), LANES), F32), jax.ShapeDtypeStruct((t, LANES), F32)],
        compiler_params=_params("arbitrary"),
        name="moe_router",
    )(x2, mod, g_ffn.reshape(1, d), wr)


def _row_copy(src_hbm, row, dst_ref, r, sem, chunks):
    src = src_hbm.at[pl.ds(pl.multiple_of(row * chunks, chunks), chunks), :]
    return pltpu.make_async_copy(src, dst_ref.at[pl.ds(pl.multiple_of(r * chunks, chunks), chunks), :], sem)


def _expert_kernel(be_ref, used_ref, cur_idx_ref, nxt_idx_ref, h_hbm, wg_ref, wu_ref, wd_ref, o_ref,
                   xin_ref, xb_ref, acc_ref, sem):
    m = pl.program_id(0)
    f = pl.program_id(1)
    n_used = used_ref[0]
    live = m < n_used
    rows, d = xb_ref.shape
    chunks = d // LANES

    def row_copies(idx_ref, slot, start):
        if not start:
            half = rows * chunks // 2
            for lane in range(2):
                pltpu.make_async_copy(h_hbm.at[pl.ds(0, half), :], xin_ref.at[slot, pl.ds(0, half), :],
                                      sem.at[slot, lane]).wait()
            return

        def body(group, c):
            for u in range(ROWS_PER_TRIP):
                r = ROWS_PER_TRIP * group + u
                _row_copy(h_hbm, idx_ref[0, 0, r], xin_ref.at[slot], r, sem.at[slot, u % 2],
                          chunks).start(priority=u % 2)
            return c
        lax.fori_loop(0, rows // ROWS_PER_TRIP, body, 0)

    @pl.when(jnp.logical_and(live, f == 0))
    def _():
        @pl.when(m == 0)
        def _():
            row_copies(cur_idx_ref, 0, start=True)

        for slot in range(2):
            @pl.when(m % 2 == slot)
            def _(slot=slot):
                row_copies(cur_idx_ref, slot, start=False)
                for s in range(chunks):
                    xb_ref[:, s * LANES:(s + 1) * LANES] = _load_row_tile_chunk(
                        xin_ref.at[slot], s, rows, chunks).astype(BF16)

                @pl.when(m + 1 < n_used)
                def _():
                    row_copies(nxt_idx_ref, 1 - slot, start=True)

        acc_ref[...] = jnp.zeros_like(acc_ref)

    @pl.when(live)
    def _():
        h = xb_ref[...]
        gate = jnp.dot(h, wg_ref[0], preferred_element_type=F32)
        up = jnp.dot(h, wu_ref[0], preferred_element_type=F32)
        act = (gate / (1.0 + jnp.exp(-gate)) * up).astype(BF16)
        acc_ref[...] += jnp.dot(act, wd_ref[0], preferred_element_type=F32)

    @pl.when(f == pl.num_programs(1) - 1)
    def _():
        _store_row_tiles(o_ref, jnp.where(live, acc_ref[...], 0.0))


def _expert_ffn(h_rows, slot_tok, block_expert, n_used, wg, wu, wd):
    d = wg.shape[1]
    chunks = d // LANES
    n_slots = slot_tok.shape[0]
    d_ff = wg.shape[2]
    tm = EXPERT_ROWS
    tf = EXPERT_FF_TILE if d_ff % EXPERT_FF_TILE == 0 else d_ff
    nf = d_ff // tf
    nm = n_slots // tm

    def fcol(m, f, used):
        return jnp.where(m < used[0], f, nf - 1)

    idx_spec = lambda shift: pl.BlockSpec((1, 1, tm), lambda m, f, be, used: (jnp.minimum(m + shift, nm - 1), 0, 0),
                                          memory_space=pltpu.SMEM)
    grid_spec = pltpu.PrefetchScalarGridSpec(
        num_scalar_prefetch=2,
        grid=(nm, nf),
        in_specs=[idx_spec(0), idx_spec(1), pl.BlockSpec(memory_space=pl.ANY),
                  pl.BlockSpec((1, d, tf), lambda m, f, be, used: (be[m], 0, fcol(m, f, used))),
                  pl.BlockSpec((1, d, tf), lambda m, f, be, used: (be[m], 0, fcol(m, f, used))),
                  pl.BlockSpec((1, tf, d), lambda m, f, be, used: (be[m], fcol(m, f, used), 0))],
        out_specs=pl.BlockSpec((tm * chunks, LANES), lambda m, f, be, used: (m, 0)),
        scratch_shapes=[pltpu.VMEM((2, tm * chunks, LANES), F32), pltpu.VMEM((tm, d), BF16),
                        pltpu.VMEM((tm, d), F32), pltpu.SemaphoreType.DMA((2, 2))])
    idx = slot_tok.reshape(nm, 1, tm)
    return pl.pallas_call(
        _expert_kernel,
        grid_spec=grid_spec,
        out_shape=jax.ShapeDtypeStruct((n_slots * chunks, LANES), F32),
        compiler_params=_params("arbitrary", "arbitrary"),
        name="expert_ffn",
    )(block_expert, n_used, idx, idx, h_rows, wg.astype(BF16), wu.astype(BF16), wd.astype(BF16))


def _combine_kernel(d0_ref, d1_ref, ys_hbm, x_ref, gates_ref, mod_ref, o_ref, y0_ref, y1_ref, sem):
    rows, d = o_ref.shape
    chunks = d // LANES

    def start(group, c):
        for u in range(ROWS_PER_TRIP // 2):
            r = ROWS_PER_TRIP // 2 * group + u
            _row_copy(ys_hbm, d0_ref[0, 0, r], y0_ref, r, sem.at[0], chunks).start(priority=0)
            _row_copy(ys_hbm, d1_ref[0, 0, r], y1_ref, r, sem.at[1], chunks).start(priority=1)
        return c

    lax.fori_loop(0, rows // (ROWS_PER_TRIP // 2), start, 0)
    for k, y_ref in enumerate((y0_ref, y1_ref)):
        pltpu.make_async_copy(ys_hbm.at[pl.ds(0, rows * chunks), :], y_ref, sem.at[k]).wait()
    gates = gates_ref[...]
    for s in range(chunks):
        cols = slice(s * LANES, (s + 1) * LANES)
        y = (_load_row_tile_chunk(y0_ref, s, rows, chunks) * gates[:, 0:1]
             + _load_row_tile_chunk(y1_ref, s, rows, chunks) * gates[:, 1:2])
        o_ref[:, cols] = x_ref[:, cols] + mod_ref[0, 5:6, cols] * y


def _moe_combine(ys, dest0, dest1, gates, x2, mod, seq):
    t, d = x2.shape
    tm = min(GATHER_ROWS, seq)
    steps = t // tm
    per_batch = seq // tm
    idx_spec = pl.BlockSpec((1, 1, tm), lambda i: (i, 0, 0), memory_space=pltpu.SMEM)
    return pl.pallas_call(
        _combine_kernel,
        grid=(steps,),
        in_specs=[idx_spec, idx_spec, pl.BlockSpec(memory_space=pl.ANY),
                  pl.BlockSpec((tm, d), lambda i: (i, 0)),
                  pl.BlockSpec((tm, TOP_K_EXPERTS), lambda i: (i, 0)),
                  pl.BlockSpec((1, 6, d), lambda i: (i // per_batch, 0, 0))],
        out_specs=pl.BlockSpec((tm, d), lambda i: (i, 0)),
        out_shape=jax.ShapeDtypeStruct((t, d), F32),
        scratch_shapes=[pltpu.VMEM((tm * (d // LANES), LANES), F32), pltpu.VMEM((tm * (d // LANES), LANES), F32),
                        pltpu.SemaphoreType.DMA((2,))],
        compiler_params=_params("arbitrary"),
        name="moe_combine",
    )(dest0.reshape(steps, 1, tm), dest1.reshape(steps, 1, tm), ys, x2, gates, mod)


def _moe_ffn(x2, mod, g_ffn, w_router, wg, wu, wd, seq):
    t, d = x2.shape
    h, logits = _router(x2, mod, g_ffn, w_router, seq)
    top_val, top_idx = lax.top_k(logits[:, :N_EXPERTS], TOP_K_EXPERTS)
    gates = jax.nn.softmax(top_val, axis=-1)

    n_assign = t * TOP_K_EXPERTS
    flat_e = top_idx.reshape(-1).astype(jnp.int32)
    onehot = (flat_e[:, None] == jnp.arange(N_EXPERTS, dtype=jnp.int32)[None, :]).astype(jnp.int32)
    rank = jnp.take_along_axis(jnp.cumsum(onehot, axis=0), flat_e[:, None], axis=1)[:, 0] - 1
    counts = jnp.sum(onehot, axis=0)
    padded = (counts + EXPERT_ROWS - 1) // EXPERT_ROWS * EXPERT_ROWS
    pad_end = jnp.cumsum(padded)
    dest = (pad_end - padded)[flat_e] + rank
    n_slots = (n_assign // EXPERT_ROWS + N_EXPERTS) * EXPERT_ROWS
    n_blocks = n_slots // EXPERT_ROWS
    slot_tok = jnp.zeros((n_slots,), jnp.int32).at[dest].set(jnp.arange(n_assign, dtype=jnp.int32) // TOP_K_EXPERTS)
    block_start = jnp.arange(n_blocks, dtype=jnp.int32) * EXPERT_ROWS
    block_expert = jnp.minimum(jnp.searchsorted(pad_end, block_start, side="right"), N_EXPERTS - 1).astype(jnp.int32)
    n_used = (pad_end[-1:] // EXPERT_ROWS).astype(jnp.int32)

    ys = _expert_ffn(h, slot_tok, block_expert, n_used, wg, wu, wd)
    dest2 = dest.reshape(t, TOP_K_EXPERTS)
    return _moe_combine(ys, dest2[:, 0], dest2[:, 1], gates, x2, mod, seq)


def _final_norm_kernel(x_ref, g_ref, o_ref):
    x = x_ref[...]
    o_ref[...] = x * lax.rsqrt(jnp.mean(x * x, axis=-1, keepdims=True) + NORM_EPS) * g_ref[...]


def _final_norm(x2, g_final):
    t, d = x2.shape
    tm = min(ROW_TILE, t)
    return pl.pallas_call(
        _final_norm_kernel,
        grid=(t // tm,),
        in_specs=[pl.BlockSpec((tm, d), lambda i: (i, 0)), pl.BlockSpec((1, d), lambda i: (0, 0))],
        out_specs=pl.BlockSpec((tm, d), lambda i: (i, 0)),
        out_shape=jax.ShapeDtypeStruct((t, d), F32),
        compiler_params=_params("arbitrary"),
        name="final_norm",
    )(x2, g_final.reshape(1, d))


def _token_mixer(x2, mod, g_mix, w_in, g_heads, w_out, batch, seq):
    pa, pb, pc, pd, p_idx, p_w, *pc_views = _in_projection(x2, mod, g_mix, _pack_w_in(w_in), seq)
    shape3 = lambda a: a.reshape(batch, -1, a.shape[-1])
    gh = g_heads.reshape(4, HEADS_PER_MIXER, HEAD_DIM)
    mixed = (_stick_breaking_mixer(shape3(pa), gh[0]),
             _dsa_mixer(shape3(pb), shape3(p_idx), shape3(p_w), gh[1]),
             _dilated_mixer([shape3(v) for v in [pc] + pc_views], gh[2]),
             _moba_mixer(shape3(pd), gh[3]))
    return _out_projection(mixed, w_out.astype(BF16), x2, mod, seq)


def kernel(x, c, w_ada, b_ada, g_mix, w_in, g_heads, w_out, g_ffn, w_ff_gate, w_ff_up, w_ff_down, w_router, w_exp_gate, w_exp_up, w_exp_down, g_final):
    batch, seq, d = x.shape
    depth = w_ada.shape[0]
    mods = _ada_modulation(c, w_ada, b_ada)
    x2 = x.reshape(batch * seq, d)
    for layer in range(depth):
        mod = mods[layer]
        x2 = _token_mixer(x2, mod, g_mix[layer], w_in[layer], g_heads[layer], w_out[layer], batch, seq)
        i = layer // 2
        if layer % 2 == 0:
            x2 = _dense_ffn(x2, mod, g_ffn[layer], w_ff_gate[i], w_ff_up[i], w_ff_down[i], seq)
        else:
            x2 = _moe_ffn(x2, mod, g_ffn[layer], w_router[i], w_exp_gate[i], w_exp_up[i], w_exp_down[i], seq)
    return _final_norm(x2, g_final).reshape(batch, seq, d)
```

```python
import functools

import numpy as np
import jax
import jax.numpy as jnp
from jax import lax
from jax.experimental import pallas as pl
from jax.experimental.pallas import tpu as pltpu

F32 = jnp.float32
BF16 = jnp.bfloat16

HEAD_DIM = 64
HEADS_PER_MIXER = 4
MIXER_WIDTH = HEADS_PER_MIXER * HEAD_DIM
QKV_WIDTH = 3 * MIXER_WIDTH
IDX_HEADS = 8
IDX_DIM = 64
DSA_TOPK = 256
SEARCH_STEPS_PER_TEST = 4
DILATIONS = (1, 4, 16)
DILATED_STEPS = 128
MOBA_BLOCK = 256
MOBA_TOPK = 3
N_EXPERTS = 8
TOP_K_EXPERTS = 2
NORM_EPS = 1e-6
ATTN_SCALE = HEAD_DIM ** -0.5

LANES = 128
SUBLANES = 8
Q_TILE = 128
ROW_TILE = 512
FFN_TILE = 1408
EXPERT_ROWS = 512
EXPERT_FF_TILE = 1792
GATHER_ROWS = 512
ROWS_PER_TRIP = 8
VMEM_LIMIT = 56 * 1024 * 1024
MASKED = -1e30
SB_UNDERFLOW = 104.0
SB_FIRST_BLOCKS = 3

_NT = (((1,), (1,)), ((), ()))
_TN = (((0,), (0,)), ((), ()))


def _alibi_slopes(mixer_pos):
    idx = np.arange(HEADS_PER_MIXER, dtype=np.float32) * 3 + (mixer_pos + 1)
    return tuple(float(s) for s in np.exp2(-8.0 * idx / 12.0).astype(np.float32))


def _params(*semantics):
    return pltpu.CompilerParams(dimension_semantics=semantics, vmem_limit_bytes=VMEM_LIMIT)


def _modulated_norm(x, gain, shift, scale):
    y = x * lax.rsqrt(jnp.mean(x * x, axis=-1, keepdims=True) + NORM_EPS) * gain
    return y * (1.0 + scale) + shift


def _head_norm(acc, gain):
    return acc * lax.rsqrt(jnp.mean(acc * acc, axis=-1, keepdims=True) + NORM_EPS) * gain


def _head_cols(h):
    return slice(h * HEAD_DIM, (h + 1) * HEAD_DIM)


def _paired_loop(n, body, carry):
    carry = lax.fori_loop(0, n // 2, lambda jj, c: body(2 * jj + 1, body(2 * jj, c, 0), 1), carry)
    return lax.cond(n % 2 == 1, lambda c: body(n - 1, c, 0), lambda c: c, carry)


def _ada_kernel(c_ref, w_ref, b_ref, o_ref):
    c = c_ref[...]
    cond = c / (1.0 + jnp.exp(-c))
    o_ref[0, 0] = jnp.dot(cond, w_ref[0], preferred_element_type=F32,
                          precision=lax.Precision.HIGHEST) + b_ref[0, 0]


def _ada_modulation(c, w_ada, b_ada):
    depth, d, _ = w_ada.shape
    b = c.shape[0]
    out = pl.pallas_call(
        _ada_kernel,
        grid=(depth, 6),
        in_specs=[pl.BlockSpec((b, d), lambda l, k: (0, 0)),
                  pl.BlockSpec((1, d, d), lambda l, k: (l, 0, k)),
                  pl.BlockSpec((1, 1, 1, d), lambda l, k: (l, k, 0, 0))],
        out_specs=pl.BlockSpec((1, 1, b, d), lambda l, k: (l, k, 0, 0)),
        out_shape=jax.ShapeDtypeStruct((depth, 6, b, d), F32),
        compiler_params=_params("arbitrary", "arbitrary"),
        name="ada_modulation",
    )(c, w_ada, b_ada.reshape(depth, 6, 1, d))
    return out.transpose(0, 2, 1, 3)


IN_WIDTHS = (QKV_WIDTH, QKV_WIDTH, QKV_WIDTH, QKV_WIDTH, IDX_HEADS * IDX_DIM + LANES, LANES)


def _in_proj_kernel(x_ref, mod_ref, g_ref, w_ref, oa, ob, oc, od, oidx, ow, oc4, oc16, pc_ref):
    h = _modulated_norm(x_ref[...], g_ref[...], mod_ref[0, 0:1, :], mod_ref[0, 1:2, :]).astype(BF16)
    tm = x_ref.shape[0]
    off = 0
    for o_ref, width in zip((oa, ob, oc, od, oidx, ow), IN_WIDTHS):
        val = jnp.dot(h, w_ref[:, off:off + width], preferred_element_type=F32)
        o_ref[...] = val.astype(o_ref.dtype)
        if o_ref is oc:
            for s in range(QKV_WIDTH // LANES):
                pc_ref[s * tm:(s + 1) * tm, :] = val[:, s * LANES:(s + 1) * LANES]
        off += width
    for r, view in zip(DILATIONS[1:], (oc4, oc16)):
        for c in range(r):
            for s in range(QKV_WIDTH // LANES):
                col = c * QKV_WIDTH + s * LANES
                view[:, col:col + LANES] = pc_ref[pl.ds(s * tm + c, tm // r, stride=r), :].astype(view.dtype)


def _pack_w_in(w_in):
    d = w_in.shape[0]
    n_qkv = 4 * QKV_WIDTH
    n_qi = IDX_HEADS * IDX_DIM
    z = lambda n: jnp.zeros((d, n), w_in.dtype)
    return jnp.concatenate([w_in[:, :n_qkv + n_qi + IDX_DIM], z(LANES - IDX_DIM),
                            w_in[:, n_qkv + n_qi + IDX_DIM:], z(LANES - IDX_HEADS)], axis=1).astype(BF16)


def _in_projection(x2, mod, g_mix, w_packed, seq):
    t, d = x2.shape
    tm = min(ROW_TILE, seq)
    per_batch = seq // tm
    dtypes = (BF16, BF16, BF16, BF16, BF16, F32)
    views = DILATIONS[1:]
    return pl.pallas_call(
        _in_proj_kernel,
        grid=(t // tm,),
        in_specs=[pl.BlockSpec((tm, d), lambda i: (i, 0)),
                  pl.BlockSpec((1, 6, d), lambda i: (i // per_batch, 0, 0)),
                  pl.BlockSpec((1, d), lambda i: (0, 0)),
                  pl.BlockSpec(w_packed.shape, lambda i: (0, 0))],
        out_specs=([pl.BlockSpec((tm, w), lambda i: (i, 0)) for w in IN_WIDTHS]
                   + [pl.BlockSpec((tm // r, r * QKV_WIDTH), lambda i: (i, 0)) for r in views]),
        out_shape=([jax.ShapeDtypeStruct((t, w), dt) for w, dt in zip(IN_WIDTHS, dtypes)]
                   + [jax.ShapeDtypeStruct((t // r, r * QKV_WIDTH), BF16) for r in views]),
        scratch_shapes=[pltpu.VMEM((tm * (QKV_WIDTH // LANES), LANES), F32)],
        compiler_params=_params("arbitrary"),
        name="in_projection",
    )(x2, mod, g_mix.reshape(1, d), w_packed)


def _sb_kernel(q_ref, k_ref, v_ref, g_ref, o_ref):
    tq = q_ref.shape[1]
    i = pl.program_id(1)
    row = lax.broadcasted_iota(jnp.int32, (tq, 1), 0)
    lane = lax.broadcasted_iota(jnp.int32, (1, tq), 1)
    later = (lax.broadcasted_iota(jnp.int32, (tq, tq), 0) > lax.broadcasted_iota(jnp.int32, (tq, tq), 1))
    later = jnp.where(later, 1.0, 0.0).astype(BF16)

    qs = [q_ref[0, :, _head_cols(h)] * ATTN_SCALE for h in range(HEADS_PER_MIXER)]

    heads = range(HEADS_PER_MIXER)

    def blocks(starts, keeps, tails):
        z = jnp.concatenate([lax.dot_general(qs[h], k_ref[0, pl.ds(st, tq), _head_cols(h)], _NT,
                                             preferred_element_type=F32) for st in starts for h in heads], axis=0)
        softplus = jnp.maximum(z, 0.0) + jnp.log(1.0 + jnp.exp(-jnp.abs(z)))
        keep = None
        if any(kp is not None for kp in keeps):
            ones = jnp.ones((tq, tq), F32)
            keep = jnp.concatenate([ones if kp is None else kp for kp in keeps for _ in heads], axis=0)
        log_1m = -softplus if keep is None else -softplus * keep
        hi = log_1m.astype(BF16)
        lo = (log_1m - hi.astype(F32)).astype(BF16)
        inside = jnp.dot(hi, later, preferred_element_type=F32) + jnp.dot(lo, later, preferred_element_type=F32)
        block_sum = jnp.sum(log_1m, axis=1, keepdims=True)
        piece = lambda x, b, h: x[(b * HEADS_PER_MIXER + h) * tq:(b * HEADS_PER_MIXER + h + 1) * tq]
        tail_cols = []
        tails = list(tails)
        for b in range(len(starts)):
            tail_cols += tails
            tails = [tails[h] + piece(block_sum, b, h) for h in heads]
        a = jnp.exp(z - softplus + inside + jnp.concatenate(tail_cols, axis=0))
        if keep is not None:
            a = a * keep
        a = a.astype(BF16)
        av = [sum(jnp.dot(piece(a, b, h), v_ref[0, pl.ds(st, tq), _head_cols(h)], preferred_element_type=F32)
                  for b, st in enumerate(starts)) for h in heads]
        return tails, av

    starts = [pl.multiple_of(jnp.maximum(i - n, 0) * tq, tq) for n in range(SB_FIRST_BLOCKS)]
    keeps = [jnp.where(lane < row, 1.0, 0.0)]
    keeps += [jnp.full((tq, tq), jnp.where(i >= n, 1.0, 0.0), F32) for n in range(1, SB_FIRST_BLOCKS)]
    tails, accs = blocks(starts, keeps, [jnp.zeros((tq, 1), F32)] * HEADS_PER_MIXER)
    state = tuple(zip(tails, accs))

    def body(carry):
        j, state = carry
        tails, av = blocks([pl.multiple_of(j * tq, tq)], [None], [tail for tail, _ in state])
        return j - 1, tuple((tails[h], state[h][1] + av[h]) for h in heads)

    def cond(carry):
        j, state = carry
        worst = functools.reduce(jnp.maximum, [tail for tail, _ in state])
        return jnp.logical_and(j >= 0, jnp.max(worst) > -SB_UNDERFLOW)

    _, state = lax.while_loop(cond, body, (i - SB_FIRST_BLOCKS, state))
    for h in range(HEADS_PER_MIXER):
        o_ref[0, :, _head_cols(h)] = _head_norm(state[h][1], g_ref[h:h + 1, :]).astype(o_ref.dtype)


def _stick_breaking_mixer(p, g_heads):
    b, seq, _ = p.shape
    tq = min(Q_TILE, seq)
    return pl.pallas_call(
        _sb_kernel,
        grid=(b, seq // tq),
        in_specs=[pl.BlockSpec((1, tq, MIXER_WIDTH), lambda bi, i: (bi, i, 0)),
                  pl.BlockSpec((1, seq, MIXER_WIDTH), lambda bi, i: (bi, 0, 1)),
                  pl.BlockSpec((1, seq, MIXER_WIDTH), lambda bi, i: (bi, 0, 2)),
                  pl.BlockSpec((HEADS_PER_MIXER, HEAD_DIM), lambda bi, i: (0, 0))],
        out_specs=pl.BlockSpec((1, tq, MIXER_WIDTH), lambda bi, i: (bi, i, 0)),
        out_shape=jax.ShapeDtypeStruct((b, seq, MIXER_WIDTH), BF16),
        compiler_params=_params("arbitrary", "arbitrary"),
        name="stick_breaking",
    )(p, p, p, g_heads)


def _dsa_kernel(q_ref, k_ref, v_ref, qi_ref, ki_ref, w_ref, g_ref, o_ref, sc_ref, sc16_ref, s_ref, p_ref, *, slopes):
    tq = q_ref.shape[1]
    tk = sc_ref.shape[1]
    i = pl.program_id(1)
    n_blocks = ((i + 1) * tq + tk - 1) // tk
    qpos = i * tq + lax.broadcasted_iota(jnp.int32, (1, tq), 1)
    key_row = lax.broadcasted_iota(jnp.int32, (tk, 1), 0)
    neg_inf = float("-inf")

    w_t = w_ref[0].T * (IDX_HEADS ** -0.5 * IDX_DIM ** -0.5)
    w_rows = [w_t[h:h + 1, :] for h in range(IDX_HEADS)]
    qi = qi_ref[0]

    def top_half(x):
        bits = lax.bitcast_convert_type(x, jnp.int32) & jnp.int32(-65536)
        return lax.bitcast_convert_type(bits, F32).astype(BF16)

    def score_body(j, carry, slot):
        start = pl.multiple_of(j * tk, tk)
        ki = ki_ref[0, pl.ds(start, tk), 0:IDX_DIM]
        sc = jnp.zeros((tk, tq), F32)
        for h in range(IDX_HEADS):
            x = lax.dot_general(ki, qi[:, h * IDX_DIM:(h + 1) * IDX_DIM], _NT, preferred_element_type=F32)
            sc = sc + w_rows[h] * jnp.maximum(x, 0.0)
        sc = jnp.where((start + key_row) <= qpos, sc + 0.0, neg_inf)
        sc_ref[j] = sc
        sc16_ref[j] = top_half(sc)
        return carry

    _paired_loop(n_blocks, score_body, 0)

    def count(pred):
        def add_block(j, cnt, slot):
            c = jnp.where(pred(sc_ref[j]), 1.0, 0.0)
            return cnt + jnp.sum(c.reshape(tk // 8, 8, tq), axis=0)

        cnt = _paired_loop(n_blocks, add_block, jnp.zeros((8, tq), F32))
        return jnp.sum(cnt, axis=0, keepdims=True)

    int_min = jnp.int32(-2 ** 31)

    def ordered_to_float(u):
        key = u ^ int_min
        bits = jnp.where(key >= 0, key, key ^ jnp.int32(0x7FFFFFFF))
        return lax.bitcast_convert_type(bits, F32)

    def count_top_half(cand16):
        def add_block(j, cnt, slot):
            c = jnp.where(sc16_ref[j] >= cand16, jnp.ones((), BF16), jnp.zeros((), BF16))
            for g in range(tk // 16):
                cnt = cnt + c[g * 16:(g + 1) * 16]
            return cnt

        cnt = _paired_loop(n_blocks, add_block, jnp.zeros((16, tq), BF16))
        return jnp.sum(cnt.astype(F32), axis=0, keepdims=True)

    def search_body(step, carry, half):
        prefix, n_at_prefix = carry
        cand = prefix | jnp.left_shift(jnp.int32(1), 31 - step)
        cand_f = ordered_to_float(cand)
        n_ge = count_top_half(top_half(cand_f)) if half else count(lambda s: s >= cand_f)
        keep = n_ge >= DSA_TOPK
        return jnp.where(keep, cand, prefix), jnp.where(keep, n_ge, n_at_prefix)

    keep_all = (qpos + 1) <= DSA_TOPK

    def settled(n_at_prefix):
        return jnp.min(jnp.where(jnp.logical_or(keep_all, n_at_prefix == DSA_TOPK), 1.0, 0.0)) > 0.5

    def lower_half_steps(carry):
        step, prefix, n_at_prefix = carry
        state = (prefix, n_at_prefix)
        for u in range(SEARCH_STEPS_PER_TEST):
            state = search_body(step + u, state, half=False)
        return (step + SEARCH_STEPS_PER_TEST,) + state

    carry = (jnp.zeros((1, tq), jnp.int32), jnp.zeros((1, tq), F32))
    carry = lax.fori_loop(0, 16, functools.partial(search_body, half=True), carry)
    _, prefix, n_ge_tau = lax.while_loop(lambda c: jnp.logical_and(c[0] < 32, jnp.logical_not(settled(c[2]))),
                                         lower_half_steps, (jnp.int32(16),) + carry)
    tau = jnp.where(keep_all, float(np.finfo(np.float32).min), ordered_to_float(prefix))
    no_ties = settled(n_ge_tau)

    earlier = (lax.broadcasted_iota(jnp.int32, (tk, tk), 1) < lax.broadcasted_iota(jnp.int32, (tk, tk), 0))
    earlier = jnp.where(earlier, 1.0, 0.0).astype(BF16)
    qs = [q_ref[0, :, _head_cols(h)] * ATTN_SCALE for h in range(HEADS_PER_MIXER)]
    key_bias = [slopes[h] * lax.broadcasted_iota(jnp.int32, (tk, tq), 0).astype(F32)
                for h in range(HEADS_PER_MIXER)]

    def attn_body(j, carry, slot, n_ties=None):
        ties_seen, stats = carry
        s_slot = [s_ref.at[slot * HEADS_PER_MIXER + h] for h in range(HEADS_PER_MIXER)]
        p_slot = [p_ref.at[slot * HEADS_PER_MIXER + h] for h in range(HEADS_PER_MIXER)]
        start = pl.multiple_of(j * tk, tk)
        sc = sc_ref[j]
        if n_ties is None:
            sel = sc >= tau
        else:
            tie = jnp.where(sc == tau, 1.0, 0.0)
            rank = jnp.dot(earlier, tie.astype(BF16), preferred_element_type=F32) + ties_seen
            sel = jnp.where(sc > tau, 1.0, jnp.where(rank < n_ties, tie, 0.0)) > 0.5
            ties_seen = ties_seen + jnp.sum(tie, axis=0, keepdims=True)
        block_pos = (j * tk - i * tq).astype(F32)
        for h in range(HEADS_PER_MIXER):
            k = k_ref[0, pl.ds(start, tk), _head_cols(h)]
            s = lax.dot_general(k, qs[h], _NT, preferred_element_type=F32) + key_bias[h]
            s_slot[h][...] = jnp.where(sel, s, MASKED)
        locals_ = []
        for h in range(HEADS_PER_MIXER):
            s = s_slot[h][...]
            m_loc = jnp.max(s, axis=0, keepdims=True)
            p = jnp.exp(s - m_loc)
            locals_.append((m_loc, jnp.sum(p, axis=0, keepdims=True)))
            p_slot[h][...] = p.astype(BF16)
        new_stats = []
        for h in range(HEADS_PER_MIXER):
            m, l, acc = stats[h]
            m_loc, l_blk = locals_[h]
            v = v_ref[0, pl.ds(start, tk), _head_cols(h)]
            pv_blk = lax.dot_general(v, p_slot[h][...], _TN, preferred_element_type=F32)
            m_blk = m_loc + slopes[h] * block_pos
            m_new = jnp.maximum(m, m_blk)
            w_old = jnp.exp(m - m_new)
            w_blk = jnp.where(m_loc > 0.5 * MASKED, jnp.exp(m_blk - m_new), 0.0)
            new_stats.append((m_new, w_old * l + w_blk * l_blk, w_old * acc + w_blk * pv_blk))
        return ties_seen, tuple(new_stats)

    init = tuple((jnp.full((1, tq), MASKED, F32), jnp.zeros((1, tq), F32), jnp.zeros((HEAD_DIM, tq), F32))
                 for _ in range(HEADS_PER_MIXER))
    start_carry = (jnp.zeros((1, tq), F32), init)

    def attend_without_ties():
        return _paired_loop(n_blocks, attn_body, start_carry)[1]

    def attend_with_ties():
        n_gt = count(lambda s: s > tau)
        n_ties = jnp.where(keep_all, 1e9, DSA_TOPK - n_gt)
        return _paired_loop(n_blocks, functools.partial(attn_body, n_ties=n_ties), start_carry)[1]

    stats = lax.cond(no_ties, attend_without_ties, attend_with_ties)
    outs = []
    for h in range(HEADS_PER_MIXER):
        _, l, acc = stats[h]
        o = acc / l
        outs.append(o * lax.rsqrt(jnp.mean(o * o, axis=0, keepdims=True) + NORM_EPS) * g_ref[:, h:h + 1])
    o_ref[0] = jnp.concatenate(outs, axis=0).T.astype(o_ref.dtype)


def _dsa_mixer(p, p_idx, p_w, g_heads):
    b, seq, _ = p.shape
    tq = min(4 * Q_TILE, seq)
    tk = tq
    assert seq // 16 <= 256, "the packed-bf16 count accumulators are exact only up to 256 adds"
    qi_width = IDX_HEADS * IDX_DIM
    return pl.pallas_call(
        functools.partial(_dsa_kernel, slopes=_alibi_slopes(0)),
        grid=(b, seq // tq),
        in_specs=[pl.BlockSpec((1, tq, MIXER_WIDTH), lambda bi, i: (bi, i, 0)),
                  pl.BlockSpec((1, seq, MIXER_WIDTH), lambda bi, i: (bi, 0, 1)),
                  pl.BlockSpec((1, seq, MIXER_WIDTH), lambda bi, i: (bi, 0, 2)),
                  pl.BlockSpec((1, tq, qi_width), lambda bi, i: (bi, i, 0)),
                  pl.BlockSpec((1, seq, LANES), lambda bi, i: (bi, 0, qi_width // LANES)),
                  pl.BlockSpec((1, tq, LANES), lambda bi, i: (bi, i, 0)),
                  pl.BlockSpec((HEAD_DIM, HEADS_PER_MIXER), lambda bi, i: (0, 0))],
        out_specs=pl.BlockSpec((1, tq, MIXER_WIDTH), lambda bi, i: (bi, i, 0)),
        out_shape=jax.ShapeDtypeStruct((b, seq, MIXER_WIDTH), BF16),
        scratch_shapes=[pltpu.VMEM((seq // tk, tk, tq), F32),
                        pltpu.VMEM((seq // tk, tk, tq), BF16),
                        pltpu.VMEM((2 * HEADS_PER_MIXER, tk, tq), F32),
                        pltpu.VMEM((2 * HEADS_PER_MIXER, tk, tq), BF16)],
        compiler_params=_params("arbitrary", "arbitrary"),
        name="dsa",
    )(p, p, p, p_idx, p_idx, p_w, g_heads.T)


def _band_kernel(q_ref, kp_ref, kc_ref, vp_ref, vc_ref, o_ref, lse_ref, *, dilation, slopes):
    tq = q_ref.shape[1]
    ui = pl.program_id(2)
    heads = range(HEADS_PER_MIXER)
    u_q = jnp.concatenate([ui * tq + lax.broadcasted_iota(jnp.int32, (tq, 1), 0)] * HEADS_PER_MIXER, axis=0)
    u_k = ui * tq - DILATED_STEPS + lax.broadcasted_iota(jnp.int32, (1, DILATED_STEPS + tq), 1)
    steps = u_q - u_k
    valid = jnp.logical_and(jnp.logical_and(steps >= 0, steps <= DILATED_STEPS), u_k >= 0)
    slope_col = jnp.concatenate([jnp.full((tq, 1), slopes[h], F32) for h in heads], axis=0)
    s = jnp.concatenate(
        [lax.dot_general(q_ref[0, :, _head_cols(h)],
                         jnp.concatenate([kp_ref[0, :, _head_cols(h)], kc_ref[0, :, _head_cols(h)]], axis=0),
                         _NT, preferred_element_type=F32) for h in heads], axis=0)
    s = jnp.where(valid, s * ATTN_SCALE - slope_col * (steps * dilation).astype(F32), MASKED)
    m = jnp.max(s, axis=1, keepdims=True)
    e = jnp.exp(s - m)
    den = jnp.sum(e, axis=1, keepdims=True)
    p = e.astype(BF16)
    lse = m + jnp.log(den)
    lane = lax.broadcasted_iota(jnp.int32, (1, LANES), 1)
    lse_all = jnp.zeros((tq, LANES), F32)
    for h in heads:
        rows = slice(h * tq, (h + 1) * tq)
        v = jnp.concatenate([vp_ref[0, :, _head_cols(h)], vc_ref[0, :, _head_cols(h)]], axis=0)
        o_ref[0, :, _head_cols(h)] = jnp.dot(p[rows], v, preferred_element_type=F32) / den[rows]
        lse_all = lse_all + jnp.where(lane == h, lse[rows], 0.0)
    lse_ref[0] = lse_all


def _dilated_branch(view, dilation, slopes):
    b, length, _ = view.shape
    classes = dilation
    tq = min(2 * Q_TILE, length)
    back = tq // DILATED_STEPS
    spec = lambda part, prev: (
        pl.BlockSpec((1, DILATED_STEPS, MIXER_WIDTH),
                     lambda bi, c, ui: (bi, jnp.maximum(ui * back - 1, 0), c * 3 + part)) if prev
        else pl.BlockSpec((1, tq, MIXER_WIDTH), lambda bi, c, ui: (bi, ui, c * 3 + part)))
    out, lse = pl.pallas_call(
        functools.partial(_band_kernel, dilation=dilation, slopes=slopes),
        grid=(b, classes, length // tq),
        in_specs=[spec(0, False), spec(1, True), spec(1, False), spec(2, True), spec(2, False)],
        out_specs=[pl.BlockSpec((1, tq, MIXER_WIDTH), lambda bi, c, ui: (bi, ui, c)),
                   pl.BlockSpec((1, tq, LANES), lambda bi, c, ui: (bi, ui, c))],
        out_shape=[jax.ShapeDtypeStruct((b, length, classes * MIXER_WIDTH), F32),
                   jax.ShapeDtypeStruct((b, length, classes * LANES), F32)],
        compiler_params=_params("arbitrary", "arbitrary", "arbitrary"),
        name=f"dilated_r{dilation}",
    )(view, view, view, view, view)
    return out, lse


def _dilated_merge_kernel(o1, o2, o3, l1, l2, l3, g_ref, o_ref, *scratch):
    tq = o_ref.shape[1]
    chunks = MIXER_WIDTH // LANES
    outs = [[o1[0, :, s * LANES:(s + 1) * LANES] for s in range(chunks)]]
    lses = [l1[0]]
    for r, o_view, l_view, o_nat, l_nat in zip(DILATIONS[1:], (o2, o3), (l2, l3), scratch[0::2], scratch[1::2]):
        for c in range(r):
            for s in range(chunks):
                col = c * MIXER_WIDTH + s * LANES
                o_nat[pl.ds(s * tq + c, tq // r, stride=r), :] = o_view[0, :, col:col + LANES]
            l_nat[pl.ds(c, tq // r, stride=r), :] = l_view[0, :, c * LANES:(c + 1) * LANES]
        outs.append([o_nat[s * tq:(s + 1) * tq, :] for s in range(chunks)])
        lses.append(l_nat[...])
    heads_per_chunk = LANES // HEAD_DIM
    for h in range(HEADS_PER_MIXER):
        sub = slice((h % heads_per_chunk) * HEAD_DIM, (h % heads_per_chunk + 1) * HEAD_DIM)
        lse_h = [l[:, h:h + 1] for l in lses]
        top = functools.reduce(jnp.maximum, lse_h)
        wts = [jnp.exp(l - top) for l in lse_h]
        mixed = sum(w * o[h // heads_per_chunk][:, sub] for w, o in zip(wts, outs)) / sum(wts)
        o_ref[0, :, _head_cols(h)] = _head_norm(mixed, g_ref[h:h + 1, :]).astype(o_ref.dtype)


def _dilated_mixer(views, g_heads):
    b, seq, _ = views[0].shape
    slopes = _alibi_slopes(1)
    branches = [_dilated_branch(v, r, slopes) for v, r in zip(views, DILATIONS)]
    tq = min(ROW_TILE, seq)
    o_specs = [pl.BlockSpec((1, tq // r, r * MIXER_WIDTH), lambda bi, i: (bi, i, 0)) for r in DILATIONS]
    l_specs = [pl.BlockSpec((1, tq // r, r * LANES), lambda bi, i: (bi, i, 0)) for r in DILATIONS]
    scratch = []
    for _ in DILATIONS[1:]:
        scratch += [pltpu.VMEM((tq * (MIXER_WIDTH // LANES), LANES), F32), pltpu.VMEM((tq, LANES), F32)]
    return pl.pallas_call(
        _dilated_merge_kernel,
        grid=(b, seq // tq),
        in_specs=o_specs + l_specs + [pl.BlockSpec((HEADS_PER_MIXER, HEAD_DIM), lambda bi, i: (0, 0))],
        out_specs=pl.BlockSpec((1, tq, MIXER_WIDTH), lambda bi, i: (bi, i, 0)),
        out_shape=jax.ShapeDtypeStruct((b, seq, MIXER_WIDTH), BF16),
        scratch_shapes=scratch,
        compiler_params=_params("arbitrary", "arbitrary"),
        name="dilated_merge",
    )(*[o for o, _ in branches], *[l for _, l in branches], g_heads)


def _moba_kernel(q_ref, k_ref, v_ref, g_ref, o_ref, kmean_ref, chosen_ref, s_ref, p_ref, *, slopes):
    tq = q_ref.shape[1]
    n_kv = k_ref.shape[1] // MOBA_BLOCK
    own = pl.program_id(1)

    @pl.when(own == 0)
    def _():
        kmean_ref[...] = jnp.zeros_like(kmean_ref)
        for n in range(n_kv):
            blk = k_ref[0, n * MOBA_BLOCK:(n + 1) * MOBA_BLOCK, :].astype(F32)
            kmean_ref[n:n + 1, :] = jnp.sum(blk, axis=0, keepdims=True) * (1.0 / MOBA_BLOCK)

    key_row = lax.broadcasted_iota(jnp.int32, (MOBA_BLOCK, 1), 0)
    q_lane = lax.broadcasted_iota(jnp.int32, (1, tq), 1)
    blk_rows = kmean_ref.shape[0]
    blk_row = lax.broadcasted_iota(jnp.int32, (blk_rows, 1), 0)
    blk_f = blk_row.astype(F32)
    neg_inf = float("-inf")

    qs = []
    for h in range(HEADS_PER_MIXER):
        q = q_ref[0, :, _head_cols(h)]
        gate = lax.dot_general(kmean_ref[:, _head_cols(h)], q.astype(F32), _NT, preferred_element_type=F32,
                               precision=lax.Precision.HIGHEST)
        gate = jnp.where(blk_row < own, gate, neg_inf)
        picks = jnp.zeros((blk_rows, tq), F32)
        for _ in range(MOBA_TOPK):
            top = jnp.max(gate, axis=0, keepdims=True)
            is_top = jnp.logical_and(gate == top, top > neg_inf)
            first = jnp.min(jnp.where(is_top, blk_f, float(blk_rows)), axis=0, keepdims=True)
            pick = blk_f == first
            picks = jnp.where(pick, 1.0, picks)
            gate = jnp.where(pick, neg_inf, gate)
        chosen_ref[h] = picks
        qs.append(q * ATTN_SCALE)

    key_bias = [slopes[h] * lax.broadcasted_iota(jnp.int32, (MOBA_BLOCK, tq), 0).astype(F32)
                for h in range(HEADS_PER_MIXER)]

    def block_softmax(start, keep=None, slot=0):
        s_slot = [s_ref.at[slot * HEADS_PER_MIXER + h] for h in range(HEADS_PER_MIXER)]
        p_slot = [p_ref.at[slot * HEADS_PER_MIXER + h] for h in range(HEADS_PER_MIXER)]
        for h in range(HEADS_PER_MIXER):
            k = k_ref[0, pl.ds(start, MOBA_BLOCK), _head_cols(h)]
            s = lax.dot_general(k, qs[h], _NT, preferred_element_type=F32) + key_bias[h]
            s_slot[h][...] = s if keep is None else jnp.where(keep, s, MASKED)
        pieces = []
        for h in range(HEADS_PER_MIXER):
            s = s_slot[h][...]
            m_loc = jnp.max(s, axis=0, keepdims=True)
            p = jnp.exp(s - m_loc)
            pieces.append((m_loc, jnp.sum(p, axis=0, keepdims=True)))
            p_slot[h][...] = p.astype(BF16)
        out = []
        for h in range(HEADS_PER_MIXER):
            v = v_ref[0, pl.ds(start, MOBA_BLOCK), _head_cols(h)]
            out.append(pieces[h] + (lax.dot_general(v, p_slot[h][...], _TN, preferred_element_type=F32),))
        return out

    own_start = pl.multiple_of(own * MOBA_BLOCK, MOBA_BLOCK)
    stats = tuple(block_softmax(own_start, keep=key_row <= q_lane))

    def body(j, stats, slot):
        start = pl.multiple_of(j * MOBA_BLOCK, MOBA_BLOCK)
        block_pos = ((j - own) * MOBA_BLOCK).astype(F32)
        merged = []
        for h, (m_loc, l_blk, pv_blk) in enumerate(block_softmax(start, slot=slot)):
            picked = chosen_ref[h, pl.ds(j, 1), :] > 0.5
            m_blk = jnp.where(picked, m_loc + slopes[h] * block_pos, MASKED)
            m, l, acc = stats[h]
            m_new = jnp.maximum(m, m_blk)
            w_old = jnp.exp(m - m_new)
            w_blk = jnp.exp(m_blk - m_new)
            merged.append((m_new, w_old * l + w_blk * l_blk, w_old * acc + w_blk * pv_blk))
        return tuple(merged)

    stats = _paired_loop(own, body, stats)
    outs = []
    for h in range(HEADS_PER_MIXER):
        _, l, acc = stats[h]
        o = acc / l
        outs.append(o * lax.rsqrt(jnp.mean(o * o, axis=0, keepdims=True) + NORM_EPS) * g_ref[:, h:h + 1])
    o_ref[0] = jnp.concatenate(outs, axis=0).T.astype(o_ref.dtype)


def _moba_mixer(p, g_heads):
    b, seq, _ = p.shape
    tq = MOBA_BLOCK
    blk_rows = -(-(seq // MOBA_BLOCK) // SUBLANES) * SUBLANES
    return pl.pallas_call(
        functools.partial(_moba_kernel, slopes=_alibi_slopes(2)),
        grid=(b, seq // tq),
        in_specs=[pl.BlockSpec((1, tq, MIXER_WIDTH), lambda bi, i: (bi, i, 0)),
                  pl.BlockSpec((1, seq, MIXER_WIDTH), lambda bi, i: (bi, 0, 1)),
                  pl.BlockSpec((1, seq, MIXER_WIDTH), lambda bi, i: (bi, 0, 2)),
                  pl.BlockSpec((HEAD_DIM, HEADS_PER_MIXER), lambda bi, i: (0, 0))],
        out_specs=pl.BlockSpec((1, tq, MIXER_WIDTH), lambda bi, i: (bi, i, 0)),
        out_shape=jax.ShapeDtypeStruct((b, seq, MIXER_WIDTH), BF16),
        scratch_shapes=[pltpu.VMEM((blk_rows, MIXER_WIDTH), F32),
                        pltpu.VMEM((HEADS_PER_MIXER, blk_rows, tq), F32),
                        pltpu.VMEM((2 * HEADS_PER_MIXER, MOBA_BLOCK, tq), F32),
                        pltpu.VMEM((2 * HEADS_PER_MIXER, MOBA_BLOCK, tq), BF16)],
        compiler_params=_params("arbitrary", "arbitrary"),
        name="moba",
    )(p, p, p, g_heads.T)


def _out_proj_kernel(oa, ob, oc, od, w_ref, x_ref, mod_ref, o_ref):
    acc = jnp.zeros(x_ref.shape, F32)
    for m, o in enumerate((oa, ob, oc, od)):
        acc = acc + jnp.dot(o[...], w_ref[m * MIXER_WIDTH:(m + 1) * MIXER_WIDTH, :], preferred_element_type=F32)
    o_ref[...] = x_ref[...] + mod_ref[0, 2:3, :] * acc


def _out_projection(mixed, w_out, x2, mod, seq):
    t, d = x2.shape
    tm = min(ROW_TILE, seq)
    per_batch = seq // tm
    o_spec = pl.BlockSpec((tm, MIXER_WIDTH), lambda i: (i, 0))
    return pl.pallas_call(
        _out_proj_kernel,
        grid=(t // tm,),
        in_specs=[o_spec] * 4 + [pl.BlockSpec(w_out.shape, lambda i: (0, 0)),
                                 pl.BlockSpec((tm, d), lambda i: (i, 0)),
                                 pl.BlockSpec((1, 6, d), lambda i: (i // per_batch, 0, 0))],
        out_specs=pl.BlockSpec((tm, d), lambda i: (i, 0)),
        out_shape=jax.ShapeDtypeStruct((t, d), F32),
        compiler_params=_params("arbitrary"),
        name="out_projection",
    )(*[o.reshape(t, MIXER_WIDTH) for o in mixed], w_out, x2, mod)


def _rms_gain(x, gain):
    return x * lax.rsqrt(jnp.mean(x * x, axis=-1, keepdims=True) + NORM_EPS) * gain


def _ffn_kernel(x_ref, mod_ref, g_ref, fg_ref, wg_ref, wu_ref, wd_ref, o_ref, h_ref, acc_ref, *, final_norm):
    f = pl.program_id(1)

    @pl.when(f == 0)
    def _():
        h_ref[...] = _modulated_norm(x_ref[...], g_ref[...], mod_ref[0, 3:4, :], mod_ref[0, 4:5, :]).astype(BF16)
        acc_ref[...] = jnp.zeros_like(acc_ref)

    h = h_ref[...]
    gate = jnp.dot(h, wg_ref[...], preferred_element_type=F32)
    up = jnp.dot(h, wu_ref[...], preferred_element_type=F32)
    act = (gate / (1.0 + jnp.exp(-gate)) * up).astype(BF16)
    acc_ref[...] += jnp.dot(act, wd_ref[...], preferred_element_type=F32)

    @pl.when(f == pl.num_programs(1) - 1)
    def _():
        out = x_ref[...] + mod_ref[0, 5:6, :] * acc_ref[...]
        o_ref[...] = _rms_gain(out, fg_ref[...]) if final_norm else out


def _dense_ffn(x2, mod, g_ffn, wg, wu, wd, seq, g_final, final_norm):
    t, d = x2.shape
    d_ff = wg.shape[1]
    tm = min(ROW_TILE, seq)
    tf = FFN_TILE if d_ff % FFN_TILE == 0 else d_ff
    per_batch = seq // tm
    return pl.pallas_call(
        functools.partial(_ffn_kernel, final_norm=final_norm),
        grid=(t // tm, d_ff // tf),
        in_specs=[pl.BlockSpec((tm, d), lambda i, f: (i, 0)),
                  pl.BlockSpec((1, 6, d), lambda i, f: (i // per_batch, 0, 0)),
                  pl.BlockSpec((1, d), lambda i, f: (0, 0)),
                  pl.BlockSpec((1, d), lambda i, f: (0, 0)),
                  pl.BlockSpec((d, tf), lambda i, f: (0, f)),
                  pl.BlockSpec((d, tf), lambda i, f: (0, f)),
                  pl.BlockSpec((tf, d), lambda i, f: (f, 0))],
        out_specs=pl.BlockSpec((tm, d), lambda i, f: (i, 0)),
        out_shape=jax.ShapeDtypeStruct((t, d), F32),
        scratch_shapes=[pltpu.VMEM((tm, d), BF16), pltpu.VMEM((tm, d), F32)],
        compiler_params=_params("arbitrary", "arbitrary"),
        name="dense_ffn",
    )(x2, mod, g_ffn.reshape(1, d), g_final.reshape(1, d), wg.astype(BF16), wu.astype(BF16), wd.astype(BF16))


def _store_row_tiles(dst_ref, value):
    rows, d = value.shape
    chunks = d // LANES
    for s in range(chunks):
        dst_ref[pl.ds(s, rows, stride=chunks), :] = value[:, s * LANES:(s + 1) * LANES]


def _load_row_tile_chunk(src_ref, s, rows, chunks):
    return src_ref[pl.ds(s, rows, stride=chunks), :]


def _router_kernel(x_ref, mod_ref, g_ref, wr_ref, h_ref, logit_ref):
    h = _modulated_norm(x_ref[...], g_ref[...], mod_ref[0, 3:4, :], mod_ref[0, 4:5, :])
    _store_row_tiles(h_ref, h)
    logit_ref[...] = jnp.dot(h, wr_ref[...], preferred_element_type=F32, precision=lax.Precision.HIGHEST)


def _router(x2, mod, g_ffn, w_router, seq):
    t, d = x2.shape
    tm = min(ROW_TILE, seq)
    per_batch = seq // tm
    wr = jnp.zeros((d, LANES), F32).at[:, :N_EXPERTS].set(w_router.astype(F32))
    return pl.pallas_call(
        _router_kernel,
        grid=(t // tm,),
        in_specs=[pl.BlockSpec((tm, d), lambda i: (i, 0)),
                  pl.BlockSpec((1, 6, d), lambda i: (i // per_batch, 0, 0)),
                  pl.BlockSpec((1, d), lambda i: (0, 0)),
                  pl.BlockSpec((d, LANES), lambda i: (0, 0))],
        out_specs=[pl.BlockSpec((tm * (d // LANES), LANES), lambda i: (i, 0)),
                   pl.BlockSpec((tm, LANES), lambda i: (i, 0))],
        out_shape=[jax.ShapeDtypeStruct((t * (d // LANES), LANES), F32), jax.ShapeDtypeStruct((t, LANES), F32)],
        compiler_params=_params("arbitrary"),
        name="moe_router",
    )(x2, mod, g_ffn.reshape(1, d), wr)


def _row_copy(src_hbm, row, dst_ref, r, sem, chunks):
    src = src_hbm.at[pl.ds(pl.multiple_of(row * chunks, chunks), chunks), :]
    return pltpu.make_async_copy(src, dst_ref.at[pl.ds(pl.multiple_of(r * chunks, chunks), chunks), :], sem)


def _expert_kernel(be_ref, used_ref, cur_idx_ref, nxt_idx_ref, h_hbm, wg_ref, wu_ref, wd_ref, o_ref,
                   xin_ref, xb_ref, acc_ref, sem):
    m = pl.program_id(0)
    f = pl.program_id(1)
    n_used = used_ref[0]
    live = m < n_used
    rows, d = xb_ref.shape
    chunks = d // LANES

    def row_copies(idx_ref, slot, start):
        if not start:
            half = rows * chunks // 2
            for lane in range(2):
                pltpu.make_async_copy(h_hbm.at[pl.ds(0, half), :], xin_ref.at[slot, pl.ds(0, half), :],
                                      sem.at[slot, lane]).wait()
            return

        def body(group, c):
            for u in range(ROWS_PER_TRIP):
                r = ROWS_PER_TRIP * group + u
                _row_copy(h_hbm, idx_ref[0, 0, r], xin_ref.at[slot], r, sem.at[slot, u % 2],
                          chunks).start(priority=u % 2)
            return c
        lax.fori_loop(0, rows // ROWS_PER_TRIP, body, 0)

    @pl.when(jnp.logical_and(live, f == 0))
    def _():
        @pl.when(m == 0)
        def _():
            row_copies(cur_idx_ref, 0, start=True)

        for slot in range(2):
            @pl.when(m % 2 == slot)
            def _(slot=slot):
                row_copies(cur_idx_ref, slot, start=False)
                for s in range(chunks):
                    xb_ref[:, s * LANES:(s + 1) * LANES] = _load_row_tile_chunk(
                        xin_ref.at[slot], s, rows, chunks).astype(BF16)

                @pl.when(m + 1 < n_used)
                def _():
                    row_copies(nxt_idx_ref, 1 - slot, start=True)

        acc_ref[...] = jnp.zeros_like(acc_ref)

    @pl.when(live)
    def _():
        h = xb_ref[...]
        gate = jnp.dot(h, wg_ref[0], preferred_element_type=F32)
        up = jnp.dot(h, wu_ref[0], preferred_element_type=F32)
        act = (gate / (1.0 + jnp.exp(-gate)) * up).astype(BF16)
        acc_ref[...] += jnp.dot(act, wd_ref[0], preferred_element_type=F32)

    @pl.when(f == pl.num_programs(1) - 1)
    def _():
        _store_row_tiles(o_ref, jnp.where(live, acc_ref[...], 0.0))


def _expert_ffn(h_rows, slot_tok, block_expert, n_used, wg, wu, wd):
    d = wg.shape[1]
    chunks = d // LANES
    n_slots = slot_tok.shape[0]
    d_ff = wg.shape[2]
    tm = EXPERT_ROWS
    tf = EXPERT_FF_TILE if d_ff % EXPERT_FF_TILE == 0 else d_ff
    nf = d_ff // tf
    nm = n_slots // tm

    def fcol(m, f, used):
        return jnp.where(m < used[0], f, nf - 1)

    idx_spec = lambda shift: pl.BlockSpec((1, 1, tm), lambda m, f, be, used: (jnp.minimum(m + shift, nm - 1), 0, 0),
                                          memory_space=pltpu.SMEM)
    grid_spec = pltpu.PrefetchScalarGridSpec(
        num_scalar_prefetch=2,
        grid=(nm, nf),
        in_specs=[idx_spec(0), idx_spec(1), pl.BlockSpec(memory_space=pl.ANY),
                  pl.BlockSpec((1, d, tf), lambda m, f, be, used: (be[m], 0, fcol(m, f, used))),
                  pl.BlockSpec((1, d, tf), lambda m, f, be, used: (be[m], 0, fcol(m, f, used))),
                  pl.BlockSpec((1, tf, d), lambda m, f, be, used: (be[m], fcol(m, f, used), 0))],
        out_specs=pl.BlockSpec((tm * chunks, LANES), lambda m, f, be, used: (m, 0)),
        scratch_shapes=[pltpu.VMEM((2, tm * chunks, LANES), F32), pltpu.VMEM((tm, d), BF16),
                        pltpu.VMEM((tm, d), F32), pltpu.SemaphoreType.DMA((2, 2))])
    idx = slot_tok.reshape(nm, 1, tm)
    return pl.pallas_call(
        _expert_kernel,
        grid_spec=grid_spec,
        out_shape=jax.ShapeDtypeStruct((n_slots * chunks, LANES), F32),
        compiler_params=_params("arbitrary", "arbitrary"),
        name="expert_ffn",
    )(block_expert, n_used, idx, idx, h_rows, wg.astype(BF16), wu.astype(BF16), wd.astype(BF16))


def _combine_kernel(d0_ref, d1_ref, ys_hbm, x_ref, gates_ref, mod_ref, fg_ref, o_ref, y0_ref, y1_ref, sem, *,
                    final_norm):
    rows, d = o_ref.shape
    chunks = d // LANES

    def start(group, c):
        for u in range(ROWS_PER_TRIP // 2):
            r = ROWS_PER_TRIP // 2 * group + u
            _row_copy(ys_hbm, d0_ref[0, 0, r], y0_ref, r, sem.at[0], chunks).start(priority=0)
            _row_copy(ys_hbm, d1_ref[0, 0, r], y1_ref, r, sem.at[1], chunks).start(priority=1)
        return c

    lax.fori_loop(0, rows // (ROWS_PER_TRIP // 2), start, 0)
    for k, y_ref in enumerate((y0_ref, y1_ref)):
        pltpu.make_async_copy(ys_hbm.at[pl.ds(0, rows * chunks), :], y_ref, sem.at[k]).wait()
    gates = gates_ref[...]
    for s in range(chunks):
        cols = slice(s * LANES, (s + 1) * LANES)
        y = (_load_row_tile_chunk(y0_ref, s, rows, chunks) * gates[:, 0:1]
             + _load_row_tile_chunk(y1_ref, s, rows, chunks) * gates[:, 1:2])
        o_ref[:, cols] = x_ref[:, cols] + mod_ref[0, 5:6, cols] * y
    if final_norm:
        o_ref[...] = _rms_gain(o_ref[...], fg_ref[...])


def _moe_combine(ys, dest0, dest1, gates, x2, mod, seq, g_final, final_norm):
    t, d = x2.shape
    tm = min(GATHER_ROWS, seq)
    steps = t // tm
    per_batch = seq // tm
    idx_spec = pl.BlockSpec((1, 1, tm), lambda i: (i, 0, 0), memory_space=pltpu.SMEM)
    return pl.pallas_call(
        functools.partial(_combine_kernel, final_norm=final_norm),
        grid=(steps,),
        in_specs=[idx_spec, idx_spec, pl.BlockSpec(memory_space=pl.ANY),
                  pl.BlockSpec((tm, d), lambda i: (i, 0)),
                  pl.BlockSpec((tm, TOP_K_EXPERTS), lambda i: (i, 0)),
                  pl.BlockSpec((1, 6, d), lambda i: (i // per_batch, 0, 0)),
                  pl.BlockSpec((1, d), lambda i: (0, 0))],
        out_specs=pl.BlockSpec((tm, d), lambda i: (i, 0)),
        out_shape=jax.ShapeDtypeStruct((t, d), F32),
        scratch_shapes=[pltpu.VMEM((tm * (d // LANES), LANES), F32), pltpu.VMEM((tm * (d // LANES), LANES), F32),
                        pltpu.SemaphoreType.DMA((2,))],
        compiler_params=_params("arbitrary"),
        name="moe_combine",
    )(dest0.reshape(steps, 1, tm), dest1.reshape(steps, 1, tm), ys, x2, gates, mod, g_final.reshape(1, d))


def _moe_ffn(x2, mod, g_ffn, w_router, wg, wu, wd, seq, g_final, final_norm):
    t, d = x2.shape
    h, logits = _router(x2, mod, g_ffn, w_router, seq)
    top_val, top_idx = lax.top_k(logits[:, :N_EXPERTS], TOP_K_EXPERTS)
    gates = jax.nn.softmax(top_val, axis=-1)

    n_assign = t * TOP_K_EXPERTS
    flat_e = top_idx.reshape(-1).astype(jnp.int32)
    onehot = (flat_e[:, None] == jnp.arange(N_EXPERTS, dtype=jnp.int32)[None, :]).astype(jnp.int32)
    rank = jnp.take_along_axis(jnp.cumsum(onehot, axis=0), flat_e[:, None], axis=1)[:, 0] - 1
    counts = jnp.sum(onehot, axis=0)
    padded = (counts + EXPERT_ROWS - 1) // EXPERT_ROWS * EXPERT_ROWS
    pad_end = jnp.cumsum(padded)
    dest = (pad_end - padded)[flat_e] + rank
    n_slots = (n_assign // EXPERT_ROWS + N_EXPERTS) * EXPERT_ROWS
    n_blocks = n_slots // EXPERT_ROWS
    slot_tok = jnp.zeros((n_slots,), jnp.int32).at[dest].set(jnp.arange(n_assign, dtype=jnp.int32) // TOP_K_EXPERTS)
    block_start = jnp.arange(n_blocks, dtype=jnp.int32) * EXPERT_ROWS
    block_expert = jnp.minimum(jnp.searchsorted(pad_end, block_start, side="right"), N_EXPERTS - 1).astype(jnp.int32)
    n_used = (pad_end[-1:] // EXPERT_ROWS).astype(jnp.int32)

    ys = _expert_ffn(h, slot_tok, block_expert, n_used, wg, wu, wd)
    dest2 = dest.reshape(t, TOP_K_EXPERTS)
    return _moe_combine(ys, dest2[:, 0], dest2[:, 1], gates, x2, mod, seq, g_final, final_norm)


def _token_mixer(x2, mod, g_mix, w_in, g_heads, w_out, batch, seq):
    pa, pb, pc, pd, p_idx, p_w, *pc_views = _in_projection(x2, mod, g_mix, _pack_w_in(w_in), seq)
    shape3 = lambda a: a.reshape(batch, -1, a.shape[-1])
    gh = g_heads.reshape(4, HEADS_PER_MIXER, HEAD_DIM)
    mixed = (_stick_breaking_mixer(shape3(pa), gh[0]),
             _dsa_mixer(shape3(pb), shape3(p_idx), shape3(p_w), gh[1]),
             _dilated_mixer([shape3(v) for v in [pc] + pc_views], gh[2]),
             _moba_mixer(shape3(pd), gh[3]))
    return _out_projection(mixed, w_out.astype(BF16), x2, mod, seq)


def kernel(x, c, w_ada, b_ada, g_mix, w_in, g_heads, w_out, g_ffn, w_ff_gate, w_ff_up, w_ff_down, w_router, w_exp_gate, w_exp_up, w_exp_down, g_final):
    batch, seq, d = x.shape
    depth = w_ada.shape[0]
    mods = _ada_modulation(c, w_ada, b_ada)
    x2 = x.reshape(batch * seq, d)
    for layer in range(depth):
        mod = mods[layer]
        x2 = _token_mixer(x2, mod, g_mix[layer], w_in[layer], g_heads[layer], w_out[layer], batch, seq)
        i = layer // 2
        last = layer == depth - 1
        if layer % 2 == 0:
            x2 = _dense_ffn(x2, mod, g_ffn[layer], w_ff_gate[i], w_ff_up[i], w_ff_down[i], seq, g_final, last)
        else:
            x2 = _moe_ffn(x2, mod, g_ffn[layer], w_router[i], w_exp_gate[i], w_exp_up[i], w_exp_down[i], seq,
                          g_final, last)
    return x2.reshape(batch, seq, d)
```

```python
import functools

import numpy as np
import jax
import jax.numpy as jnp
from jax import lax
from jax.experimental import pallas as pl
from jax.experimental.pallas import tpu as pltpu

F32 = jnp.float32
BF16 = jnp.bfloat16

HEAD_DIM = 64
HEADS_PER_MIXER = 4
MIXER_WIDTH = HEADS_PER_MIXER * HEAD_DIM
QKV_WIDTH = 3 * MIXER_WIDTH
IDX_HEADS = 8
IDX_DIM = 64
DSA_TOPK = 256
SEARCH_STEPS_PER_TEST = 4
DILATIONS = (1, 4, 16)
DILATED_STEPS = 128
MOBA_BLOCK = 256
MOBA_TOPK = 3
MOBA_GROUP = 2
N_EXPERTS = 8
TOP_K_EXPERTS = 2
NORM_EPS = 1e-6
ATTN_SCALE = HEAD_DIM ** -0.5

LANES = 128
SUBLANES = 8
Q_TILE = 128
ROW_TILE = 512
FFN_TILE = 1408
EXPERT_ROWS = 512
EXPERT_FF_TILE = 1792
GATHER_ROWS = 512
ROWS_PER_TRIP = 8
VMEM_LIMIT = 56 * 1024 * 1024
MASKED = -1e30
SB_UNDERFLOW = 104.0
SB_FIRST_BLOCKS = 3

_NT = (((1,), (1,)), ((), ()))
_TN = (((0,), (0,)), ((), ()))


def _alibi_slopes(mixer_pos):
    idx = np.arange(HEADS_PER_MIXER, dtype=np.float32) * 3 + (mixer_pos + 1)
    return tuple(float(s) for s in np.exp2(-8.0 * idx / 12.0).astype(np.float32))


def _params(*semantics):
    return pltpu.CompilerParams(dimension_semantics=semantics, vmem_limit_bytes=VMEM_LIMIT)


def _modulated_norm(x, gain, shift, scale):
    y = x * lax.rsqrt(jnp.mean(x * x, axis=-1, keepdims=True) + NORM_EPS) * gain
    return y * (1.0 + scale) + shift


def _head_norm(acc, gain):
    return acc * lax.rsqrt(jnp.mean(acc * acc, axis=-1, keepdims=True) + NORM_EPS) * gain


def _head_cols(h):
    return slice(h * HEAD_DIM, (h + 1) * HEAD_DIM)


def _paired_loop(n, body, carry):
    carry = lax.fori_loop(0, n // 2, lambda jj, c: body(2 * jj + 1, body(2 * jj, c, 0), 1), carry)
    return lax.cond(n % 2 == 1, lambda c: body(n - 1, c, 0), lambda c: c, carry)


def _ada_kernel(c_ref, w_ref, b_ref, o_ref):
    c = c_ref[...]
    cond = c / (1.0 + jnp.exp(-c))
    o_ref[0, 0] = jnp.dot(cond, w_ref[0], preferred_element_type=F32,
                          precision=lax.Precision.HIGHEST) + b_ref[0, 0]


def _ada_modulation(c, w_ada, b_ada):
    depth, d, _ = w_ada.shape
    b = c.shape[0]
    out = pl.pallas_call(
        _ada_kernel,
        grid=(depth, 6),
        in_specs=[pl.BlockSpec((b, d), lambda l, k: (0, 0)),
                  pl.BlockSpec((1, d, d), lambda l, k: (l, 0, k)),
                  pl.BlockSpec((1, 1, 1, d), lambda l, k: (l, k, 0, 0))],
        out_specs=pl.BlockSpec((1, 1, b, d), lambda l, k: (l, k, 0, 0)),
        out_shape=jax.ShapeDtypeStruct((depth, 6, b, d), F32),
        compiler_params=_params("arbitrary", "arbitrary"),
        name="ada_modulation",
    )(c, w_ada, b_ada.reshape(depth, 6, 1, d))
    return out.transpose(0, 2, 1, 3)


IN_WIDTHS = (QKV_WIDTH, QKV_WIDTH, QKV_WIDTH, QKV_WIDTH, IDX_HEADS * IDX_DIM + LANES, LANES)


def _in_proj_kernel(x_ref, mod_ref, g_ref, w_ref, oa, ob, oc, od, oidx, ow, oc4, oc16, pc_ref):
    h = _modulated_norm(x_ref[...], g_ref[...], mod_ref[0, 0:1, :], mod_ref[0, 1:2, :]).astype(BF16)
    tm = x_ref.shape[0]
    off = 0
    for o_ref, width in zip((oa, ob, oc, od, oidx, ow), IN_WIDTHS):
        val = jnp.dot(h, w_ref[:, off:off + width], preferred_element_type=F32)
        o_ref[...] = val.astype(o_ref.dtype)
        if o_ref is oc:
            for s in range(QKV_WIDTH // LANES):
                pc_ref[s * tm:(s + 1) * tm, :] = val[:, s * LANES:(s + 1) * LANES]
        off += width
    for r, view in zip(DILATIONS[1:], (oc4, oc16)):
        for c in range(r):
            for s in range(QKV_WIDTH // LANES):
                col = c * QKV_WIDTH + s * LANES
                view[:, col:col + LANES] = pc_ref[pl.ds(s * tm + c, tm // r, stride=r), :].astype(view.dtype)


def _pack_w_in(w_in):
    d = w_in.shape[0]
    n_qkv = 4 * QKV_WIDTH
    n_qi = IDX_HEADS * IDX_DIM
    z = lambda n: jnp.zeros((d, n), w_in.dtype)
    return jnp.concatenate([w_in[:, :n_qkv + n_qi + IDX_DIM], z(LANES - IDX_DIM),
                            w_in[:, n_qkv + n_qi + IDX_DIM:], z(LANES - IDX_HEADS)], axis=1).astype(BF16)


def _in_projection(x2, mod, g_mix, w_packed, seq):
    t, d = x2.shape
    tm = min(ROW_TILE, seq)
    per_batch = seq // tm
    dtypes = (BF16, BF16, BF16, BF16, BF16, F32)
    views = DILATIONS[1:]
    return pl.pallas_call(
        _in_proj_kernel,
        grid=(t // tm,),
        in_specs=[pl.BlockSpec((tm, d), lambda i: (i, 0)),
                  pl.BlockSpec((1, 6, d), lambda i: (i // per_batch, 0, 0)),
                  pl.BlockSpec((1, d), lambda i: (0, 0)),
                  pl.BlockSpec(w_packed.shape, lambda i: (0, 0))],
        out_specs=([pl.BlockSpec((tm, w), lambda i: (i, 0)) for w in IN_WIDTHS]
                   + [pl.BlockSpec((tm // r, r * QKV_WIDTH), lambda i: (i, 0)) for r in views]),
        out_shape=([jax.ShapeDtypeStruct((t, w), dt) for w, dt in zip(IN_WIDTHS, dtypes)]
                   + [jax.ShapeDtypeStruct((t // r, r * QKV_WIDTH), BF16) for r in views]),
        scratch_shapes=[pltpu.VMEM((tm * (QKV_WIDTH // LANES), LANES), F32)],
        compiler_params=_params("arbitrary"),
        name="in_projection",
    )(x2, mod, g_mix.reshape(1, d), w_packed)


def _sb_kernel(q_ref, k_ref, v_ref, g_ref, o_ref):
    tq = q_ref.shape[1]
    i = pl.program_id(1)
    row = lax.broadcasted_iota(jnp.int32, (tq, 1), 0)
    lane = lax.broadcasted_iota(jnp.int32, (1, tq), 1)
    later = (lax.broadcasted_iota(jnp.int32, (tq, tq), 0) > lax.broadcasted_iota(jnp.int32, (tq, tq), 1))
    later = jnp.where(later, 1.0, 0.0).astype(BF16)

    qs = [q_ref[0, :, _head_cols(h)] * ATTN_SCALE for h in range(HEADS_PER_MIXER)]

    heads = range(HEADS_PER_MIXER)

    def blocks(starts, keeps, tails):
        z = jnp.concatenate([lax.dot_general(qs[h], k_ref[0, pl.ds(st, tq), _head_cols(h)], _NT,
                                             preferred_element_type=F32) for st in starts for h in heads], axis=0)
        softplus = jnp.maximum(z, 0.0) + jnp.log(1.0 + jnp.exp(-jnp.abs(z)))
        keep = None
        if any(kp is not None for kp in keeps):
            ones = jnp.ones((tq, tq), F32)
            keep = jnp.concatenate([ones if kp is None else kp for kp in keeps for _ in heads], axis=0)
        log_1m = -softplus if keep is None else -softplus * keep
        hi = log_1m.astype(BF16)
        lo = (log_1m - hi.astype(F32)).astype(BF16)
        inside = jnp.dot(hi, later, preferred_element_type=F32) + jnp.dot(lo, later, preferred_element_type=F32)
        block_sum = jnp.sum(log_1m, axis=1, keepdims=True)
        piece = lambda x, b, h: x[(b * HEADS_PER_MIXER + h) * tq:(b * HEADS_PER_MIXER + h + 1) * tq]
        tail_cols = []
        tails = list(tails)
        for b in range(len(starts)):
            tail_cols += tails
            tails = [tails[h] + piece(block_sum, b, h) for h in heads]
        a = jnp.exp(z - softplus + inside + jnp.concatenate(tail_cols, axis=0))
        if keep is not None:
            a = a * keep
        a = a.astype(BF16)
        av = [sum(jnp.dot(piece(a, b, h), v_ref[0, pl.ds(st, tq), _head_cols(h)], preferred_element_type=F32)
                  for b, st in enumerate(starts)) for h in heads]
        return tails, av

    starts = [pl.multiple_of(jnp.maximum(i - n, 0) * tq, tq) for n in range(SB_FIRST_BLOCKS)]
    keeps = [jnp.where(lane < row, 1.0, 0.0)]
    keeps += [jnp.full((tq, tq), jnp.where(i >= n, 1.0, 0.0), F32) for n in range(1, SB_FIRST_BLOCKS)]
    tails, accs = blocks(starts, keeps, [jnp.zeros((tq, 1), F32)] * HEADS_PER_MIXER)
    state = tuple(zip(tails, accs))

    def body(carry):
        j, state = carry
        tails, av = blocks([pl.multiple_of(j * tq, tq)], [None], [tail for tail, _ in state])
        return j - 1, tuple((tails[h], state[h][1] + av[h]) for h in heads)

    def cond(carry):
        j, state = carry
        worst = functools.reduce(jnp.maximum, [tail for tail, _ in state])
        return jnp.logical_and(j >= 0, jnp.max(worst) > -SB_UNDERFLOW)

    _, state = lax.while_loop(cond, body, (i - SB_FIRST_BLOCKS, state))
    for h in range(HEADS_PER_MIXER):
        o_ref[0, :, _head_cols(h)] = _head_norm(state[h][1], g_ref[h:h + 1, :]).astype(o_ref.dtype)


def _stick_breaking_mixer(p, g_heads):
    b, seq, _ = p.shape
    tq = min(Q_TILE, seq)
    return pl.pallas_call(
        _sb_kernel,
        grid=(b, seq // tq),
        in_specs=[pl.BlockSpec((1, tq, MIXER_WIDTH), lambda bi, i: (bi, i, 0)),
                  pl.BlockSpec((1, seq, MIXER_WIDTH), lambda bi, i: (bi, 0, 1)),
                  pl.BlockSpec((1, seq, MIXER_WIDTH), lambda bi, i: (bi, 0, 2)),
                  pl.BlockSpec((HEADS_PER_MIXER, HEAD_DIM), lambda bi, i: (0, 0))],
        out_specs=pl.BlockSpec((1, tq, MIXER_WIDTH), lambda bi, i: (bi, i, 0)),
        out_shape=jax.ShapeDtypeStruct((b, seq, MIXER_WIDTH), BF16),
        compiler_params=_params("arbitrary", "arbitrary"),
        name="stick_breaking",
    )(p, p, p, g_heads)


def _dsa_kernel(q_ref, k_ref, v_ref, qi_ref, ki_ref, w_ref, g_ref, o_ref, sc_ref, sc16_ref, s_ref, p_ref, *, slopes):
    tq = q_ref.shape[1]
    tk = sc_ref.shape[1]
    i = pl.program_id(1)
    n_blocks = ((i + 1) * tq + tk - 1) // tk
    qpos = i * tq + lax.broadcasted_iota(jnp.int32, (1, tq), 1)
    key_row = lax.broadcasted_iota(jnp.int32, (tk, 1), 0)
    neg_inf = float("-inf")

    w_t = w_ref[0].T * (IDX_HEADS ** -0.5 * IDX_DIM ** -0.5)
    w_rows = [w_t[h:h + 1, :] for h in range(IDX_HEADS)]
    qi = qi_ref[0]

    def top_half(x):
        bits = lax.bitcast_convert_type(x, jnp.int32) & jnp.int32(-65536)
        return lax.bitcast_convert_type(bits, F32).astype(BF16)

    def score_body(j, carry, slot):
        start = pl.multiple_of(j * tk, tk)
        ki = ki_ref[0, pl.ds(start, tk), 0:IDX_DIM]
        sc = jnp.zeros((tk, tq), F32)
        for h in range(IDX_HEADS):
            x = lax.dot_general(ki, qi[:, h * IDX_DIM:(h + 1) * IDX_DIM], _NT, preferred_element_type=F32)
            sc = sc + w_rows[h] * jnp.maximum(x, 0.0)
        sc = jnp.where((start + key_row) <= qpos, sc + 0.0, neg_inf)
        sc_ref[j] = sc
        sc16_ref[j] = top_half(sc)
        return carry

    _paired_loop(n_blocks, score_body, 0)

    def count(pred):
        def add_block(j, cnt, slot):
            c = jnp.where(pred(sc_ref[j]), 1.0, 0.0)
            return cnt + jnp.sum(c.reshape(tk // 8, 8, tq), axis=0)

        cnt = _paired_loop(n_blocks, add_block, jnp.zeros((8, tq), F32))
        return jnp.sum(cnt, axis=0, keepdims=True)

    int_min = jnp.int32(-2 ** 31)

    def ordered_to_float(u):
        key = u ^ int_min
        bits = jnp.where(key >= 0, key, key ^ jnp.int32(0x7FFFFFFF))
        return lax.bitcast_convert_type(bits, F32)

    def count_top_half(cand16):
        def add_block(j, cnt, slot):
            c = jnp.where(sc16_ref[j] >= cand16, jnp.ones((), BF16), jnp.zeros((), BF16))
            for g in range(tk // 16):
                cnt = cnt + c[g * 16:(g + 1) * 16]
            return cnt

        cnt = _paired_loop(n_blocks, add_block, jnp.zeros((16, tq), BF16))
        return jnp.sum(cnt.astype(F32), axis=0, keepdims=True)

    def search_body(step, carry, half):
        prefix, n_at_prefix = carry
        cand = prefix | jnp.left_shift(jnp.int32(1), 31 - step)
        cand_f = ordered_to_float(cand)
        n_ge = count_top_half(top_half(cand_f)) if half else count(lambda s: s >= cand_f)
        keep = n_ge >= DSA_TOPK
        return jnp.where(keep, cand, prefix), jnp.where(keep, n_ge, n_at_prefix)

    keep_all = (qpos + 1) <= DSA_TOPK

    def settled(n_at_prefix):
        return jnp.min(jnp.where(jnp.logical_or(keep_all, n_at_prefix == DSA_TOPK), 1.0, 0.0)) > 0.5

    def lower_half_steps(carry):
        step, prefix, n_at_prefix = carry
        state = (prefix, n_at_prefix)
        for u in range(SEARCH_STEPS_PER_TEST):
            state = search_body(step + u, state, half=False)
        return (step + SEARCH_STEPS_PER_TEST,) + state

    carry = (jnp.zeros((1, tq), jnp.int32), jnp.zeros((1, tq), F32))
    carry = lax.fori_loop(0, 16, functools.partial(search_body, half=True), carry)
    _, prefix, n_ge_tau = lax.while_loop(lambda c: jnp.logical_and(c[0] < 32, jnp.logical_not(settled(c[2]))),
                                         lower_half_steps, (jnp.int32(16),) + carry)
    tau = jnp.where(keep_all, float(np.finfo(np.float32).min), ordered_to_float(prefix))
    no_ties = settled(n_ge_tau)

    earlier = (lax.broadcasted_iota(jnp.int32, (tk, tk), 1) < lax.broadcasted_iota(jnp.int32, (tk, tk), 0))
    earlier = jnp.where(earlier, 1.0, 0.0).astype(BF16)
    qs = [q_ref[0, :, _head_cols(h)] * ATTN_SCALE for h in range(HEADS_PER_MIXER)]
    key_bias = [slopes[h] * lax.broadcasted_iota(jnp.int32, (tk, tq), 0).astype(F32)
                for h in range(HEADS_PER_MIXER)]

    def attn_body(j, carry, slot, n_ties=None):
        ties_seen, stats = carry
        s_slot = [s_ref.at[slot * HEADS_PER_MIXER + h] for h in range(HEADS_PER_MIXER)]
        p_slot = [p_ref.at[slot * HEADS_PER_MIXER + h] for h in range(HEADS_PER_MIXER)]
        start = pl.multiple_of(j * tk, tk)
        sc = sc_ref[j]
        if n_ties is None:
            sel = sc >= tau
        else:
            tie = jnp.where(sc == tau, 1.0, 0.0)
            rank = jnp.dot(earlier, tie.astype(BF16), preferred_element_type=F32) + ties_seen
            sel = jnp.where(sc > tau, 1.0, jnp.where(rank < n_ties, tie, 0.0)) > 0.5
            ties_seen = ties_seen + jnp.sum(tie, axis=0, keepdims=True)
        block_pos = (j * tk - i * tq).astype(F32)
        for h in range(HEADS_PER_MIXER):
            k = k_ref[0, pl.ds(start, tk), _head_cols(h)]
            s = lax.dot_general(k, qs[h], _NT, preferred_element_type=F32) + key_bias[h]
            s_slot[h][...] = jnp.where(sel, s, MASKED)
        locals_ = []
        for h in range(HEADS_PER_MIXER):
            s = s_slot[h][...]
            m_loc = jnp.max(s, axis=0, keepdims=True)
            p = jnp.exp(s - m_loc)
            locals_.append((m_loc, jnp.sum(p, axis=0, keepdims=True)))
            p_slot[h][...] = p.astype(BF16)
        new_stats = []
        for h in range(HEADS_PER_MIXER):
            m, l, acc = stats[h]
            m_loc, l_blk = locals_[h]
            v = v_ref[0, pl.ds(start, tk), _head_cols(h)]
            pv_blk = lax.dot_general(v, p_slot[h][...], _TN, preferred_element_type=F32)
            m_blk = m_loc + slopes[h] * block_pos
            m_new = jnp.maximum(m, m_blk)
            w_old = jnp.exp(m - m_new)
            w_blk = jnp.where(m_loc > 0.5 * MASKED, jnp.exp(m_blk - m_new), 0.0)
            new_stats.append((m_new, w_old * l + w_blk * l_blk, w_old * acc + w_blk * pv_blk))
        return ties_seen, tuple(new_stats)

    init = tuple((jnp.full((1, tq), MASKED, F32), jnp.zeros((1, tq), F32), jnp.zeros((HEAD_DIM, tq), F32))
                 for _ in range(HEADS_PER_MIXER))
    start_carry = (jnp.zeros((1, tq), F32), init)

    def attend_without_ties():
        return _paired_loop(n_blocks, attn_body, start_carry)[1]

    def attend_with_ties():
        n_gt = count(lambda s: s > tau)
        n_ties = jnp.where(keep_all, 1e9, DSA_TOPK - n_gt)
        return _paired_loop(n_blocks, functools.partial(attn_body, n_ties=n_ties), start_carry)[1]

    stats = lax.cond(no_ties, attend_without_ties, attend_with_ties)
    outs = []
    for h in range(HEADS_PER_MIXER):
        _, l, acc = stats[h]
        o = acc / l
        outs.append(o * lax.rsqrt(jnp.mean(o * o, axis=0, keepdims=True) + NORM_EPS) * g_ref[:, h:h + 1])
    o_ref[0] = jnp.concatenate(outs, axis=0).T.astype(o_ref.dtype)


def _dsa_mixer(p, p_idx, p_w, g_heads):
    b, seq, _ = p.shape
    tq = min(4 * Q_TILE, seq)
    tk = tq
    assert seq // 16 <= 256, "the packed-bf16 count accumulators are exact only up to 256 adds"
    qi_width = IDX_HEADS * IDX_DIM
    return pl.pallas_call(
        functools.partial(_dsa_kernel, slopes=_alibi_slopes(0)),
        grid=(b, seq // tq),
        in_specs=[pl.BlockSpec((1, tq, MIXER_WIDTH), lambda bi, i: (bi, i, 0)),
                  pl.BlockSpec((1, seq, MIXER_WIDTH), lambda bi, i: (bi, 0, 1)),
                  pl.BlockSpec((1, seq, MIXER_WIDTH), lambda bi, i: (bi, 0, 2)),
                  pl.BlockSpec((1, tq, qi_width), lambda bi, i: (bi, i, 0)),
                  pl.BlockSpec((1, seq, LANES), lambda bi, i: (bi, 0, qi_width // LANES)),
                  pl.BlockSpec((1, tq, LANES), lambda bi, i: (bi, i, 0)),
                  pl.BlockSpec((HEAD_DIM, HEADS_PER_MIXER), lambda bi, i: (0, 0))],
        out_specs=pl.BlockSpec((1, tq, MIXER_WIDTH), lambda bi, i: (bi, i, 0)),
        out_shape=jax.ShapeDtypeStruct((b, seq, MIXER_WIDTH), BF16),
        scratch_shapes=[pltpu.VMEM((seq // tk, tk, tq), F32),
                        pltpu.VMEM((seq // tk, tk, tq), BF16),
                        pltpu.VMEM((2 * HEADS_PER_MIXER, tk, tq), F32),
                        pltpu.VMEM((2 * HEADS_PER_MIXER, tk, tq), BF16)],
        compiler_params=_params("arbitrary", "arbitrary"),
        name="dsa",
    )(p, p, p, p_idx, p_idx, p_w, g_heads.T)


def _band_kernel(q_ref, kp_ref, kc_ref, vp_ref, vc_ref, o_ref, lse_ref, *, dilation, slopes):
    tq = q_ref.shape[1]
    ui = pl.program_id(2)
    heads = range(HEADS_PER_MIXER)
    u_q = jnp.concatenate([ui * tq + lax.broadcasted_iota(jnp.int32, (tq, 1), 0)] * HEADS_PER_MIXER, axis=0)
    u_k = ui * tq - DILATED_STEPS + lax.broadcasted_iota(jnp.int32, (1, DILATED_STEPS + tq), 1)
    steps = u_q - u_k
    valid = jnp.logical_and(jnp.logical_and(steps >= 0, steps <= DILATED_STEPS), u_k >= 0)
    slope_col = jnp.concatenate([jnp.full((tq, 1), slopes[h], F32) for h in heads], axis=0)
    s = jnp.concatenate(
        [lax.dot_general(q_ref[0, :, _head_cols(h)],
                         jnp.concatenate([kp_ref[0, :, _head_cols(h)], kc_ref[0, :, _head_cols(h)]], axis=0),
                         _NT, preferred_element_type=F32) for h in heads], axis=0)
    s = jnp.where(valid, s * ATTN_SCALE - slope_col * (steps * dilation).astype(F32), MASKED)
    m = jnp.max(s, axis=1, keepdims=True)
    e = jnp.exp(s - m)
    den = jnp.sum(e, axis=1, keepdims=True)
    p = e.astype(BF16)
    lse = m + jnp.log(den)
    lane = lax.broadcasted_iota(jnp.int32, (1, LANES), 1)
    lse_all = jnp.zeros((tq, LANES), F32)
    for h in heads:
        rows = slice(h * tq, (h + 1) * tq)
        v = jnp.concatenate([vp_ref[0, :, _head_cols(h)], vc_ref[0, :, _head_cols(h)]], axis=0)
        o_ref[0, :, _head_cols(h)] = jnp.dot(p[rows], v, preferred_element_type=F32) / den[rows]
        lse_all = lse_all + jnp.where(lane == h, lse[rows], 0.0)
    lse_ref[0] = lse_all


def _dilated_branch(view, dilation, slopes):
    b, length, _ = view.shape
    classes = dilation
    tq = min(2 * Q_TILE, length)
    back = tq // DILATED_STEPS
    spec = lambda part, prev: (
        pl.BlockSpec((1, DILATED_STEPS, MIXER_WIDTH),
                     lambda bi, c, ui: (bi, jnp.maximum(ui * back - 1, 0), c * 3 + part)) if prev
        else pl.BlockSpec((1, tq, MIXER_WIDTH), lambda bi, c, ui: (bi, ui, c * 3 + part)))
    out, lse = pl.pallas_call(
        functools.partial(_band_kernel, dilation=dilation, slopes=slopes),
        grid=(b, classes, length // tq),
        in_specs=[spec(0, False), spec(1, True), spec(1, False), spec(2, True), spec(2, False)],
        out_specs=[pl.BlockSpec((1, tq, MIXER_WIDTH), lambda bi, c, ui: (bi, ui, c)),
                   pl.BlockSpec((1, tq, LANES), lambda bi, c, ui: (bi, ui, c))],
        out_shape=[jax.ShapeDtypeStruct((b, length, classes * MIXER_WIDTH), F32),
                   jax.ShapeDtypeStruct((b, length, classes * LANES), F32)],
        compiler_params=_params("arbitrary", "arbitrary", "arbitrary"),
        name=f"dilated_r{dilation}",
    )(view, view, view, view, view)
    return out, lse


def _dilated_merge_kernel(o1, o2, o3, l1, l2, l3, g_ref, o_ref, *scratch):
    tq = o_ref.shape[1]
    chunks = MIXER_WIDTH // LANES
    outs = [[o1[0, :, s * LANES:(s + 1) * LANES] for s in range(chunks)]]
    lses = [l1[0]]
    for r, o_view, l_view, o_nat, l_nat in zip(DILATIONS[1:], (o2, o3), (l2, l3), scratch[0::2], scratch[1::2]):
        for c in range(r):
            for s in range(chunks):
                col = c * MIXER_WIDTH + s * LANES
                o_nat[pl.ds(s * tq + c, tq // r, stride=r), :] = o_view[0, :, col:col + LANES]
            l_nat[pl.ds(c, tq // r, stride=r), :] = l_view[0, :, c * LANES:(c + 1) * LANES]
        outs.append([o_nat[s * tq:(s + 1) * tq, :] for s in range(chunks)])
        lses.append(l_nat[...])
    heads_per_chunk = LANES // HEAD_DIM
    for h in range(HEADS_PER_MIXER):
        sub = slice((h % heads_per_chunk) * HEAD_DIM, (h % heads_per_chunk + 1) * HEAD_DIM)
        lse_h = [l[:, h:h + 1] for l in lses]
        top = functools.reduce(jnp.maximum, lse_h)
        wts = [jnp.exp(l - top) for l in lse_h]
        mixed = sum(w * o[h // heads_per_chunk][:, sub] for w, o in zip(wts, outs)) / sum(wts)
        o_ref[0, :, _head_cols(h)] = _head_norm(mixed, g_ref[h:h + 1, :]).astype(o_ref.dtype)


def _dilated_mixer(views, g_heads):
    b, seq, _ = views[0].shape
    slopes = _alibi_slopes(1)
    branches = [_dilated_branch(v, r, slopes) for v, r in zip(views, DILATIONS)]
    tq = min(ROW_TILE, seq)
    o_specs = [pl.BlockSpec((1, tq // r, r * MIXER_WIDTH), lambda bi, i: (bi, i, 0)) for r in DILATIONS]
    l_specs = [pl.BlockSpec((1, tq // r, r * LANES), lambda bi, i: (bi, i, 0)) for r in DILATIONS]
    scratch = []
    for _ in DILATIONS[1:]:
        scratch += [pltpu.VMEM((tq * (MIXER_WIDTH // LANES), LANES), F32), pltpu.VMEM((tq, LANES), F32)]
    return pl.pallas_call(
        _dilated_merge_kernel,
        grid=(b, seq // tq),
        in_specs=o_specs + l_specs + [pl.BlockSpec((HEADS_PER_MIXER, HEAD_DIM), lambda bi, i: (0, 0))],
        out_specs=pl.BlockSpec((1, tq, MIXER_WIDTH), lambda bi, i: (bi, i, 0)),
        out_shape=jax.ShapeDtypeStruct((b, seq, MIXER_WIDTH), BF16),
        scratch_shapes=scratch,
        compiler_params=_params("arbitrary", "arbitrary"),
        name="dilated_merge",
    )(*[o for o, _ in branches], *[l for _, l in branches], g_heads)


def _moba_kernel(q_ref, k_ref, v_ref, g_ref, o_ref, kmean_ref, chosen_ref, s_ref, p_ref, *, slopes):
    tq = q_ref.shape[1]
    n_kv = k_ref.shape[1] // MOBA_BLOCK
    own = pl.program_id(1)

    @pl.when(own == 0)
    def _():
        kmean_ref[...] = jnp.zeros_like(kmean_ref)
        for n in range(n_kv):
            blk = k_ref[0, n * MOBA_BLOCK:(n + 1) * MOBA_BLOCK, :].astype(F32)
            kmean_ref[n:n + 1, :] = jnp.sum(blk, axis=0, keepdims=True) * (1.0 / MOBA_BLOCK)

    key_row = lax.broadcasted_iota(jnp.int32, (MOBA_BLOCK, 1), 0)
    q_lane = lax.broadcasted_iota(jnp.int32, (1, tq), 1)
    blk_rows = kmean_ref.shape[0]
    blk_row = lax.broadcasted_iota(jnp.int32, (blk_rows, 1), 0)
    blk_f = blk_row.astype(F32)
    neg_inf = float("-inf")
    group = tq // MOBA_BLOCK
    first_own = own * group
    sub_block = sum(jnp.where(q_lane >= g * MOBA_BLOCK, 1, 0) for g in range(1, group)) if group > 1 else 0
    own_blk = first_own + sub_block
    q_local = q_lane - sub_block * MOBA_BLOCK

    qs = []
    for h in range(HEADS_PER_MIXER):
        q = q_ref[0, :, _head_cols(h)]
        gate = lax.dot_general(kmean_ref[:, _head_cols(h)], q.astype(F32), _NT, preferred_element_type=F32,
                               precision=lax.Precision.HIGHEST)
        gate = jnp.where(blk_row < own_blk, gate, neg_inf)
        picks = jnp.zeros((blk_rows, tq), F32)
        for _ in range(MOBA_TOPK):
            top = jnp.max(gate, axis=0, keepdims=True)
            is_top = jnp.logical_and(gate == top, top > neg_inf)
            first = jnp.min(jnp.where(is_top, blk_f, float(blk_rows)), axis=0, keepdims=True)
            pick = blk_f == first
            picks = jnp.where(pick, 1.0, picks)
            gate = jnp.where(pick, neg_inf, gate)
        chosen_ref[h] = picks
        qs.append(q * ATTN_SCALE)

    key_bias = [slopes[h] * lax.broadcasted_iota(jnp.int32, (MOBA_BLOCK, tq), 0).astype(F32)
                for h in range(HEADS_PER_MIXER)]

    def block_softmax(start, keep=None, slot=0):
        s_slot = [s_ref.at[slot * HEADS_PER_MIXER + h] for h in range(HEADS_PER_MIXER)]
        p_slot = [p_ref.at[slot * HEADS_PER_MIXER + h] for h in range(HEADS_PER_MIXER)]
        for h in range(HEADS_PER_MIXER):
            k = k_ref[0, pl.ds(start, MOBA_BLOCK), _head_cols(h)]
            s = lax.dot_general(k, qs[h], _NT, preferred_element_type=F32) + key_bias[h]
            s_slot[h][...] = s if keep is None else jnp.where(keep, s, MASKED)
        pieces = []
        for h in range(HEADS_PER_MIXER):
            s = s_slot[h][...]
            m_loc = jnp.max(s, axis=0, keepdims=True)
            p = jnp.exp(s - m_loc)
            pieces.append((m_loc, jnp.sum(p, axis=0, keepdims=True)))
            p_slot[h][...] = p.astype(BF16)
        out = []
        for h in range(HEADS_PER_MIXER):
            v = v_ref[0, pl.ds(start, MOBA_BLOCK), _head_cols(h)]
            out.append(pieces[h] + (lax.dot_general(v, p_slot[h][...], _TN, preferred_element_type=F32),))
        return out

    def merge(stats, j, pieces, use):
        block_pos = ((j - first_own) * MOBA_BLOCK).astype(F32)
        merged = []
        for h, (m_loc, l_blk, pv_blk) in enumerate(pieces):
            m_blk = jnp.where(use[h], m_loc + slopes[h] * block_pos, MASKED)
            m, l, acc = stats[h]
            m_new = jnp.maximum(m, m_blk)
            w_old = jnp.exp(m - m_new)
            w_blk = jnp.where(use[h], jnp.exp(m_blk - m_new), 0.0)
            merged.append((m_new, w_old * l + w_blk * l_blk, w_old * acc + w_blk * pv_blk))
        return tuple(merged)

    def picked(h, j):
        return chosen_ref[h, pl.ds(j, 1), :] > 0.5

    def body(j, stats, slot):
        start = pl.multiple_of(j * MOBA_BLOCK, MOBA_BLOCK)
        return merge(stats, j, block_softmax(start, slot=slot), [picked(h, j) for h in range(HEADS_PER_MIXER)])

    stats = tuple((jnp.full((1, tq), MASKED, F32), jnp.zeros((1, tq), F32), jnp.zeros((HEAD_DIM, tq), F32))
                  for _ in range(HEADS_PER_MIXER))
    stats = _paired_loop(first_own, body, stats)
    for g in range(group):
        j = first_own + g
        keep = jnp.logical_or(own_blk > j, jnp.logical_and(own_blk == j, key_row <= q_local))
        pieces = block_softmax(pl.multiple_of(j * MOBA_BLOCK, MOBA_BLOCK), keep=keep, slot=g % 2)
        use = [jnp.logical_or(own_blk == j, jnp.logical_and(own_blk > j, picked(h, j)))
               for h in range(HEADS_PER_MIXER)]
        stats = merge(stats, j, pieces, use)
    outs = []
    for h in range(HEADS_PER_MIXER):
        _, l, acc = stats[h]
        o = acc / l
        outs.append(o * lax.rsqrt(jnp.mean(o * o, axis=0, keepdims=True) + NORM_EPS) * g_ref[:, h:h + 1])
    o_ref[0] = jnp.concatenate(outs, axis=0).T.astype(o_ref.dtype)


def _moba_mixer(p, g_heads):
    b, seq, _ = p.shape
    tq = MOBA_GROUP * MOBA_BLOCK if seq % (MOBA_GROUP * MOBA_BLOCK) == 0 else MOBA_BLOCK
    blk_rows =-(-(seq // MOBA_BLOCK) // SUBLANES) * SUBLANES
    return pl.pallas_call(
        functools.partial(_moba_kernel, slopes=_alibi_slopes(2)),
        grid=(b, seq // tq),
        in_specs=[pl.BlockSpec((1, tq, MIXER_WIDTH), lambda bi, i: (bi, i, 0)),
                  pl.BlockSpec((1, seq, MIXER_WIDTH), lambda bi, i: (bi, 0, 1)),
                  pl.BlockSpec((1, seq, MIXER_WIDTH), lambda bi, i: (bi, 0, 2)),
                  pl.BlockSpec((HEAD_DIM, HEADS_PER_MIXER), lambda bi, i: (0, 0))],
        out_specs=pl.BlockSpec((1, tq, MIXER_WIDTH), lambda bi, i: (bi, i, 0)),
        out_shape=jax.ShapeDtypeStruct((b, seq, MIXER_WIDTH), BF16),
        scratch_shapes=[pltpu.VMEM((blk_rows, MIXER_WIDTH), F32),
                        pltpu.VMEM((HEADS_PER_MIXER, blk_rows, tq), F32),
                        pltpu.VMEM((2 * HEADS_PER_MIXER, MOBA_BLOCK, tq), F32),
                        pltpu.VMEM((2 * HEADS_PER_MIXER, MOBA_BLOCK, tq), BF16)],
        compiler_params=_params("arbitrary", "arbitrary"),
        name="moba",
    )(p, p, p, g_heads.T)


def _out_proj_kernel(oa, ob, oc, od, w_ref, x_ref, mod_ref, o_ref):
    acc = jnp.zeros(x_ref.shape, F32)
    for m, o in enumerate((oa, ob, oc, od)):
        acc = acc + jnp.dot(o[...], w_ref[m * MIXER_WIDTH:(m + 1) * MIXER_WIDTH, :], preferred_element_type=F32)
    o_ref[...] = x_ref[...] + mod_ref[0, 2:3, :] * acc


def _out_projection(mixed, w_out, x2, mod, seq):
    t, d = x2.shape
    tm = min(ROW_TILE, seq)
    per_batch = seq // tm
    o_spec = pl.BlockSpec((tm, MIXER_WIDTH), lambda i: (i, 0))
    return pl.pallas_call(
        _out_proj_kernel,
        grid=(t // tm,),
        in_specs=[o_spec] * 4 + [pl.BlockSpec(w_out.shape, lambda i: (0, 0)),
                                 pl.BlockSpec((tm, d), lambda i: (i, 0)),
                                 pl.BlockSpec((1, 6, d), lambda i: (i // per_batch, 0, 0))],
        out_specs=pl.BlockSpec((tm, d), lambda i: (i, 0)),
        out_shape=jax.ShapeDtypeStruct((t, d), F32),
        compiler_params=_params("arbitrary"),
        name="out_projection",
    )(*[o.reshape(t, MIXER_WIDTH) for o in mixed], w_out, x2, mod)


def _rms_gain(x, gain):
    return x * lax.rsqrt(jnp.mean(x * x, axis=-1, keepdims=True) + NORM_EPS) * gain


def _ffn_kernel(x_ref, mod_ref, g_ref, fg_ref, wg_ref, wu_ref, wd_ref, o_ref, h_ref, acc_ref, *, final_norm):
    f = pl.program_id(1)

    @pl.when(f == 0)
    def _():
        h_ref[...] = _modulated_norm(x_ref[...], g_ref[...], mod_ref[0, 3:4, :], mod_ref[0, 4:5, :]).astype(BF16)
        acc_ref[...] = jnp.zeros_like(acc_ref)

    h = h_ref[...]
    gate = jnp.dot(h, wg_ref[...], preferred_element_type=F32)
    up = jnp.dot(h, wu_ref[...], preferred_element_type=F32)
    act = (gate / (1.0 + jnp.exp(-gate)) * up).astype(BF16)
    acc_ref[...] += jnp.dot(act, wd_ref[...], preferred_element_type=F32)

    @pl.when(f == pl.num_programs(1) - 1)
    def _():
        out = x_ref[...] + mod_ref[0, 5:6, :] * acc_ref[...]
        o_ref[...] = _rms_gain(out, fg_ref[...]) if final_norm else out


def _dense_ffn(x2, mod, g_ffn, wg, wu, wd, seq, g_final, final_norm):
    t, d = x2.shape
    d_ff = wg.shape[1]
    tm = min(ROW_TILE, seq)
    tf = FFN_TILE if d_ff % FFN_TILE == 0 else d_ff
    per_batch = seq // tm
    return pl.pallas_call(
        functools.partial(_ffn_kernel, final_norm=final_norm),
        grid=(t // tm, d_ff // tf),
        in_specs=[pl.BlockSpec((tm, d), lambda i, f: (i, 0)),
                  pl.BlockSpec((1, 6, d), lambda i, f: (i // per_batch, 0, 0)),
                  pl.BlockSpec((1, d), lambda i, f: (0, 0)),
                  pl.BlockSpec((1, d), lambda i, f: (0, 0)),
                  pl.BlockSpec((d, tf), lambda i, f: (0, f)),
                  pl.BlockSpec((d, tf), lambda i, f: (0, f)),
                  pl.BlockSpec((tf, d), lambda i, f: (f, 0))],
        out_specs=pl.BlockSpec((tm, d), lambda i, f: (i, 0)),
        out_shape=jax.ShapeDtypeStruct((t, d), F32),
        scratch_shapes=[pltpu.VMEM((tm, d), BF16), pltpu.VMEM((tm, d), F32)],
        compiler_params=_params("arbitrary", "arbitrary"),
        name="dense_ffn",
    )(x2, mod, g_ffn.reshape(1, d), g_final.reshape(1, d), wg.astype(BF16), wu.astype(BF16), wd.astype(BF16))


def _store_row_tiles(dst_ref, value):
    rows, d = value.shape
    chunks = d // LANES
    for s in range(chunks):
        dst_ref[pl.ds(s, rows, stride=chunks), :] = value[:, s * LANES:(s + 1) * LANES]


def _load_row_tile_chunk(src_ref, s, rows, chunks):
    return src_ref[pl.ds(s, rows, stride=chunks), :]


def _router_kernel(x_ref, mod_ref, g_ref, wr_ref, h_ref, logit_ref):
    h = _modulated_norm(x_ref[...], g_ref[...], mod_ref[0, 3:4, :], mod_ref[0, 4:5, :])
    _store_row_tiles(h_ref, h)
    logit_ref[...] = jnp.dot(h, wr_ref[...], preferred_element_type=F32, precision=lax.Precision.HIGHEST)


def _router(x2, mod, g_ffn, w_router, seq):
    t, d = x2.shape
    tm = min(ROW_TILE, seq)
    per_batch = seq // tm
    wr = jnp.zeros((d, LANES), F32).at[:, :N_EXPERTS].set(w_router.astype(F32))
    return pl.pallas_call(
        _router_kernel,
        grid=(t // tm,),
        in_specs=[pl.BlockSpec((tm, d), lambda i: (i, 0)),
                  pl.BlockSpec((1, 6, d), lambda i: (i // per_batch, 0, 0)),
                  pl.BlockSpec((1, d), lambda i: (0, 0)),
                  pl.BlockSpec((d, LANES), lambda i: (0, 0))],
        out_specs=[pl.BlockSpec((tm * (d // LANES), LANES), lambda i: (i, 0)),
                   pl.BlockSpec((tm, LANES), lambda i: (i, 0))],
        out_shape=[jax.ShapeDtypeStruct((t * (d // LANES), LANES), F32), jax.ShapeDtypeStruct((t, LANES), F32)],
        compiler_params=_params("arbitrary"),
        name="moe_router",
    )(x2, mod, g_ffn.reshape(1, d), wr)


def _row_copy(src_hbm, row, dst_ref, r, sem, chunks):
    src = src_hbm.at[pl.ds(pl.multiple_of(row * chunks, chunks), chunks), :]
    return pltpu.make_async_copy(src, dst_ref.at[pl.ds(pl.multiple_of(r * chunks, chunks), chunks), :], sem)


def _expert_kernel(be_ref, used_ref, cur_idx_ref, nxt_idx_ref, h_hbm, wg_ref, wu_ref, wd_ref, o_ref,
                   xin_ref, xb_ref, acc_ref, sem):
    m = pl.program_id(0)
    f = pl.program_id(1)
    n_used = used_ref[0]
    live = m < n_used
    rows, d = xb_ref.shape
    chunks = d // LANES

    def row_copies(idx_ref, slot, start):
        if not start:
            half = rows * chunks // 2
            for lane in range(2):
                pltpu.make_async_copy(h_hbm.at[pl.ds(0, half), :], xin_ref.at[slot, pl.ds(0, half), :],
                                      sem.at[slot, lane]).wait()
            return

        def body(group, c):
            for u in range(ROWS_PER_TRIP):
                r = ROWS_PER_TRIP * group + u
                _row_copy(h_hbm, idx_ref[0, 0, r], xin_ref.at[slot], r, sem.at[slot, u % 2],
                          chunks).start(priority=u % 2)
            return c
        lax.fori_loop(0, rows // ROWS_PER_TRIP, body, 0)

    @pl.when(jnp.logical_and(live, f == 0))
    def _():
        @pl.when(m == 0)
        def _():
            row_copies(cur_idx_ref, 0, start=True)

        for slot in range(2):
            @pl.when(m % 2 == slot)
            def _(slot=slot):
                row_copies(cur_idx_ref, slot, start=False)
                for s in range(chunks):
                    xb_ref[:, s * LANES:(s + 1) * LANES] = _load_row_tile_chunk(
                        xin_ref.at[slot], s, rows, chunks).astype(BF16)

                @pl.when(m + 1 < n_used)
                def _():
                    row_copies(nxt_idx_ref, 1 - slot, start=True)

        acc_ref[...] = jnp.zeros_like(acc_ref)

    @pl.when(live)
    def _():
        h = xb_ref[...]
        gate = jnp.dot(h, wg_ref[0], preferred_element_type=F32)
        up = jnp.dot(h, wu_ref[0], preferred_element_type=F32)
        act = (gate / (1.0 + jnp.exp(-gate)) * up).astype(BF16)
        acc_ref[...] += jnp.dot(act, wd_ref[0], preferred_element_type=F32)

    @pl.when(f == pl.num_programs(1) - 1)
    def _():
        _store_row_tiles(o_ref, jnp.where(live, acc_ref[...], 0.0))


def _expert_ffn(h_rows, slot_tok, block_expert, n_used, wg, wu, wd):
    d = wg.shape[1]
    chunks = d // LANES
    n_slots = slot_tok.shape[0]
    d_ff = wg.shape[2]
    tm = EXPERT_ROWS
    tf = EXPERT_FF_TILE if d_ff % EXPERT_FF_TILE == 0 else d_ff
    nf = d_ff // tf
    nm = n_slots // tm

    def fcol(m, f, used):
        return jnp.where(m < used[0], f, nf - 1)

    idx_spec = lambda shift: pl.BlockSpec((1, 1, tm), lambda m, f, be, used: (jnp.minimum(m + shift, nm - 1), 0, 0),
                                          memory_space=pltpu.SMEM)
    grid_spec = pltpu.PrefetchScalarGridSpec(
        num_scalar_prefetch=2,
        grid=(nm, nf),
        in_specs=[idx_spec(0), idx_spec(1), pl.BlockSpec(memory_space=pl.ANY),
                  pl.BlockSpec((1, d, tf), lambda m, f, be, used: (be[m], 0, fcol(m, f, used))),
                  pl.BlockSpec((1, d, tf), lambda m, f, be, used: (be[m], 0, fcol(m, f, used))),
                  pl.BlockSpec((1, tf, d), lambda m, f, be, used: (be[m], fcol(m, f, used), 0))],
        out_specs=pl.BlockSpec((tm * chunks, LANES), lambda m, f, be, used: (m, 0)),
        scratch_shapes=[pltpu.VMEM((2, tm * chunks, LANES), F32), pltpu.VMEM((tm, d), BF16),
                        pltpu.VMEM((tm, d), F32), pltpu.SemaphoreType.DMA((2, 2))])
    idx = slot_tok.reshape(nm, 1, tm)
    return pl.pallas_call(
        _expert_kernel,
        grid_spec=grid_spec,
        out_shape=jax.ShapeDtypeStruct((n_slots * chunks, LANES), F32),
        compiler_params=_params("arbitrary", "arbitrary"),
        name="expert_ffn",
    )(block_expert, n_used, idx, idx, h_rows, wg.astype(BF16), wu.astype(BF16), wd.astype(BF16))


def _combine_kernel(d0_ref, d1_ref, ys_hbm, x_ref, gates_ref, mod_ref, fg_ref, o_ref, y0_ref, y1_ref, sem, *,
                    final_norm):
    rows, d = o_ref.shape
    chunks = d // LANES

    def start(group, c):
        for u in range(ROWS_PER_TRIP // 2):
            r = ROWS_PER_TRIP // 2 * group + u
            _row_copy(ys_hbm, d0_ref[0, 0, r], y0_ref, r, sem.at[0], chunks).start(priority=0)
            _row_copy(ys_hbm, d1_ref[0, 0, r], y1_ref, r, sem.at[1], chunks).start(priority=1)
        return c

    lax.fori_loop(0, rows // (ROWS_PER_TRIP // 2), start, 0)
    for k, y_ref in enumerate((y0_ref, y1_ref)):
        pltpu.make_async_copy(ys_hbm.at[pl.ds(0, rows * chunks), :], y_ref, sem.at[k]).wait()
    gates = gates_ref[...]
    for s in range(chunks):
        cols = slice(s * LANES, (s + 1) * LANES)
        y = (_load_row_tile_chunk(y0_ref, s, rows, chunks) * gates[:, 0:1]
             + _load_row_tile_chunk(y1_ref, s, rows, chunks) * gates[:, 1:2])
        o_ref[:, cols] = x_ref[:, cols] + mod_ref[0, 5:6, cols] * y
    if final_norm:
        o_ref[...] = _rms_gain(o_ref[...], fg_ref[...])


def _moe_combine(ys, dest0, dest1, gates, x2, mod, seq, g_final, final_norm):
    t, d = x2.shape
    tm = min(GATHER_ROWS, seq)
    steps = t // tm
    per_batch = seq // tm
    idx_spec = pl.BlockSpec((1, 1, tm), lambda i: (i, 0, 0), memory_space=pltpu.SMEM)
    return pl.pallas_call(
        functools.partial(_combine_kernel, final_norm=final_norm),
        grid=(steps,),
        in_specs=[idx_spec, idx_spec, pl.BlockSpec(memory_space=pl.ANY),
                  pl.BlockSpec((tm, d), lambda i: (i, 0)),
                  pl.BlockSpec((tm, TOP_K_EXPERTS), lambda i: (i, 0)),
                  pl.BlockSpec((1, 6, d), lambda i: (i // per_batch, 0, 0)),
                  pl.BlockSpec((1, d), lambda i: (0, 0))],
        out_specs=pl.BlockSpec((tm, d), lambda i: (i, 0)),
        out_shape=jax.ShapeDtypeStruct((t, d), F32),
        scratch_shapes=[pltpu.VMEM((tm * (d // LANES), LANES), F32), pltpu.VMEM((tm * (d // LANES), LANES), F32),
                        pltpu.SemaphoreType.DMA((2,))],
        compiler_params=_params("arbitrary"),
        name="moe_combine",
    )(dest0.reshape(steps, 1, tm), dest1.reshape(steps, 1, tm), ys, x2, gates, mod, g_final.reshape(1, d))


def _moe_ffn(x2, mod, g_ffn, w_router, wg, wu, wd, seq, g_final, final_norm):
    t, d = x2.shape
    h, logits = _router(x2, mod, g_ffn, w_router, seq)
    top_val, top_idx = lax.top_k(logits[:, :N_EXPERTS], TOP_K_EXPERTS)
    gates = jax.nn.softmax(top_val, axis=-1)

    n_assign = t * TOP_K_EXPERTS
    flat_e = top_idx.reshape(-1).astype(jnp.int32)
    onehot = (flat_e[:, None] == jnp.arange(N_EXPERTS, dtype=jnp.int32)[None, :]).astype(jnp.int32)
    rank = jnp.take_along_axis(jnp.cumsum(onehot, axis=0), flat_e[:, None], axis=1)[:, 0] - 1
    counts = jnp.sum(onehot, axis=0)
    padded = (counts + EXPERT_ROWS - 1) // EXPERT_ROWS * EXPERT_ROWS
    pad_end = jnp.cumsum(padded)
    dest = (pad_end - padded)[flat_e] + rank
    n_slots = (n_assign // EXPERT_ROWS + N_EXPERTS) * EXPERT_ROWS
    n_blocks = n_slots // EXPERT_ROWS
    slot_tok = jnp.zeros((n_slots,), jnp.int32).at[dest].set(jnp.arange(n_assign, dtype=jnp.int32) // TOP_K_EXPERTS)
    block_start = jnp.arange(n_blocks, dtype=jnp.int32) * EXPERT_ROWS
    block_expert = jnp.minimum(jnp.searchsorted(pad_end, block_start, side="right"), N_EXPERTS - 1).astype(jnp.int32)
    n_used = (pad_end[-1:] // EXPERT_ROWS).astype(jnp.int32)

    ys = _expert_ffn(h, slot_tok, block_expert, n_used, wg, wu, wd)
    dest2 = dest.reshape(t, TOP_K_EXPERTS)
    return _moe_combine(ys, dest2[:, 0], dest2[:, 1], gates, x2, mod, seq, g_final, final_norm)


def _token_mixer(x2, mod, g_mix, w_in, g_heads, w_out, batch, seq):
    pa, pb, pc, pd, p_idx, p_w, *pc_views = _in_projection(x2, mod, g_mix, _pack_w_in(w_in), seq)
    shape3 = lambda a: a.reshape(batch, -1, a.shape[-1])
    gh = g_heads.reshape(4, HEADS_PER_MIXER, HEAD_DIM)
    mixed = (_stick_breaking_mixer(shape3(pa), gh[0]),
             _dsa_mixer(shape3(pb), shape3(p_idx), shape3(p_w), gh[1]),
             _dilated_mixer([shape3(v) for v in [pc] + pc_views], gh[2]),
             _moba_mixer(shape3(pd), gh[3]))
    return _out_projection(mixed, w_out.astype(BF16), x2, mod, seq)


def kernel(x, c, w_ada, b_ada, g_mix, w_in, g_heads, w_out, g_ffn, w_ff_gate, w_ff_up, w_ff_down, w_router, w_exp_gate, w_exp_up, w_exp_down, g_final):
    batch, seq, d = x.shape
    depth = w_ada.shape[0]
    mods = _ada_modulation(c, w_ada, b_ada)
    x2 = x.reshape(batch * seq, d)
    for layer in range(depth):
        mod = mods[layer]
        x2 = _token_mixer(x2, mod, g_mix[layer], w_in[layer], g_heads[layer], w_out[layer], batch, seq)
        i = layer // 2
        last = layer == depth - 1
        if layer % 2 == 0:
            x2 = _dense_ffn(x2, mod, g_ffn[layer], w_ff_gate[i], w_ff_up[i], w_ff_down[i], seq, g_final, last)
        else:
            x2 = _moe_ffn(x2, mod, g_ffn[layer], w_router[i], w_exp_gate[i], w_exp_up[i], w_exp_down[i], seq,
                          g_final, last)
    return x2.reshape(batch, seq, d)
```

```python
import functools

import numpy as np
import jax
import jax.numpy as jnp
from jax import lax
from jax.experimental import pallas as pl
from jax.experimental.pallas import tpu as pltpu

F32 = jnp.float32
BF16 = jnp.bfloat16

HEAD_DIM = 64
HEADS_PER_MIXER = 4
MIXER_WIDTH = HEADS_PER_MIXER * HEAD_DIM
QKV_WIDTH = 3 * MIXER_WIDTH
IDX_HEADS = 8
IDX_DIM = 64
DSA_TOPK = 256
DILATIONS = (1, 4, 16)
DILATED_STEPS = 128
MOBA_BLOCK = 256
MOBA_TOPK = 3
MOBA_GROUP = 2
N_EXPERTS = 8
TOP_K_EXPERTS = 2
NORM_EPS = 1e-6
ATTN_SCALE = HEAD_DIM ** -0.5

LANES = 128
SUBLANES = 8
Q_TILE = 128
ROW_TILE = 512
FFN_TILE = 1408
EXPERT_ROWS = 512
EXPERT_FF_TILE = 1792
GATHER_ROWS = 512
ROWS_PER_TRIP = 8
VMEM_LIMIT = 56 * 1024 * 1024
MASKED = -1e30
SB_UNDERFLOW = 104.0
SB_FIRST_BLOCKS = 3

_NT = (((1,), (1,)), ((), ()))
_TN = (((0,), (0,)), ((), ()))


def _alibi_slopes(mixer_pos):
    idx = np.arange(HEADS_PER_MIXER, dtype=np.float32) * 3 + (mixer_pos + 1)
    return tuple(float(s) for s in np.exp2(-8.0 * idx / 12.0).astype(np.float32))


def _params(*semantics):
    return pltpu.CompilerParams(dimension_semantics=semantics, vmem_limit_bytes=VMEM_LIMIT)


def _modulated_norm(x, gain, shift, scale):
    y = x * lax.rsqrt(jnp.mean(x * x, axis=-1, keepdims=True) + NORM_EPS) * gain
    return y * (1.0 + scale) + shift


def _head_norm(acc, gain):
    return acc * lax.rsqrt(jnp.mean(acc * acc, axis=-1, keepdims=True) + NORM_EPS) * gain


def _head_cols(h):
    return slice(h * HEAD_DIM, (h + 1) * HEAD_DIM)


def _paired_loop(n, body, carry):
    carry = lax.fori_loop(0, n // 2, lambda jj, c: body(2 * jj + 1, body(2 * jj, c, 0), 1), carry)
    return lax.cond(n % 2 == 1, lambda c: body(n - 1, c, 0), lambda c: c, carry)


def _ada_kernel(c_ref, w_ref, b_ref, o_ref):
    c = c_ref[...]
    cond = c / (1.0 + jnp.exp(-c))
    o_ref[0, 0] = jnp.dot(cond, w_ref[0], preferred_element_type=F32,
                          precision=lax.Precision.HIGHEST) + b_ref[0, 0]


def _ada_modulation(c, w_ada, b_ada):
    depth, d, _ = w_ada.shape
    b = c.shape[0]
    out = pl.pallas_call(
        _ada_kernel,
        grid=(depth, 6),
        in_specs=[pl.BlockSpec((b, d), lambda l, k: (0, 0)),
                  pl.BlockSpec((1, d, d), lambda l, k: (l, 0, k)),
                  pl.BlockSpec((1, 1, 1, d), lambda l, k: (l, k, 0, 0))],
        out_specs=pl.BlockSpec((1, 1, b, d), lambda l, k: (l, k, 0, 0)),
        out_shape=jax.ShapeDtypeStruct((depth, 6, b, d), F32),
        compiler_params=_params("arbitrary", "arbitrary"),
        name="ada_modulation",
    )(c, w_ada, b_ada.reshape(depth, 6, 1, d))
    return out.transpose(0, 2, 1, 3)


IN_WIDTHS = (QKV_WIDTH, QKV_WIDTH, QKV_WIDTH, QKV_WIDTH, IDX_HEADS * IDX_DIM + LANES, LANES)


def _in_proj_kernel(x_ref, mod_ref, g_ref, w_ref, oa, ob, oc, od, oidx, ow, oc4, oc16, pc_ref):
    h = _modulated_norm(x_ref[...], g_ref[...], mod_ref[0, 0:1, :], mod_ref[0, 1:2, :]).astype(BF16)
    tm = x_ref.shape[0]
    off = 0
    for o_ref, width in zip((oa, ob, oc, od, oidx, ow), IN_WIDTHS):
        val = jnp.dot(h, w_ref[:, off:off + width], preferred_element_type=F32)
        o_ref[...] = val.astype(o_ref.dtype)
        if o_ref is oc:
            for s in range(QKV_WIDTH // LANES):
                pc_ref[s * tm:(s + 1) * tm, :] = val[:, s * LANES:(s + 1) * LANES]
        off += width
    for r, view in zip(DILATIONS[1:], (oc4, oc16)):
        for c in range(r):
            for s in range(QKV_WIDTH // LANES):
                col = c * QKV_WIDTH + s * LANES
                view[:, col:col + LANES] = pc_ref[pl.ds(s * tm + c, tm // r, stride=r), :].astype(view.dtype)


def _pack_w_in(w_in):
    d = w_in.shape[0]
    n_qkv = 4 * QKV_WIDTH
    n_qi = IDX_HEADS * IDX_DIM
    z = lambda n: jnp.zeros((d, n), w_in.dtype)
    return jnp.concatenate([w_in[:, :n_qkv + n_qi + IDX_DIM], z(LANES - IDX_DIM),
                            w_in[:, n_qkv + n_qi + IDX_DIM:], z(LANES - IDX_HEADS)], axis=1).astype(BF16)


def _in_projection(x2, mod, g_mix, w_packed, seq):
    t, d = x2.shape
    tm = min(ROW_TILE, seq)
    per_batch = seq // tm
    dtypes = (BF16, BF16, BF16, BF16, BF16, F32)
    views = DILATIONS[1:]
    return pl.pallas_call(
        _in_proj_kernel,
        grid=(t // tm,),
        in_specs=[pl.BlockSpec((tm, d), lambda i: (i, 0)),
                  pl.BlockSpec((1, 6, d), lambda i: (i // per_batch, 0, 0)),
                  pl.BlockSpec((1, d), lambda i: (0, 0)),
                  pl.BlockSpec(w_packed.shape, lambda i: (0, 0))],
        out_specs=([pl.BlockSpec((tm, w), lambda i: (i, 0)) for w in IN_WIDTHS]
                   + [pl.BlockSpec((tm // r, r * QKV_WIDTH), lambda i: (i, 0)) for r in views]),
        out_shape=([jax.ShapeDtypeStruct((t, w), dt) for w, dt in zip(IN_WIDTHS, dtypes)]
                   + [jax.ShapeDtypeStruct((t // r, r * QKV_WIDTH), BF16) for r in views]),
        scratch_shapes=[pltpu.VMEM((tm * (QKV_WIDTH // LANES), LANES), F32)],
        compiler_params=_params("arbitrary"),
        name="in_projection",
    )(x2, mod, g_mix.reshape(1, d), w_packed)


def _sb_kernel(q_ref, k_ref, v_ref, g_ref, o_ref):
    tq = q_ref.shape[1]
    i = pl.program_id(1)
    row = lax.broadcasted_iota(jnp.int32, (tq, 1), 0)
    lane = lax.broadcasted_iota(jnp.int32, (1, tq), 1)
    later = (lax.broadcasted_iota(jnp.int32, (tq, tq), 0) > lax.broadcasted_iota(jnp.int32, (tq, tq), 1))
    later = jnp.where(later, 1.0, 0.0).astype(BF16)

    qs = [q_ref[0, :, _head_cols(h)] * ATTN_SCALE for h in range(HEADS_PER_MIXER)]

    heads = range(HEADS_PER_MIXER)

    def blocks(starts, keeps, tails):
        z = jnp.concatenate([lax.dot_general(qs[h], k_ref[0, pl.ds(st, tq), _head_cols(h)], _NT,
                                             preferred_element_type=F32) for st in starts for h in heads], axis=0)
        softplus = jnp.maximum(z, 0.0) + jnp.log(1.0 + jnp.exp(-jnp.abs(z)))
        keep = None
        if any(kp is not None for kp in keeps):
            ones = jnp.ones((tq, tq), F32)
            keep = jnp.concatenate([ones if kp is None else kp for kp in keeps for _ in heads], axis=0)
        log_1m = -softplus if keep is None else -softplus * keep
        hi = log_1m.astype(BF16)
        lo = (log_1m - hi.astype(F32)).astype(BF16)
        inside = jnp.dot(hi, later, preferred_element_type=F32) + jnp.dot(lo, later, preferred_element_type=F32)
        block_sum = jnp.sum(log_1m, axis=1, keepdims=True)
        piece = lambda x, b, h: x[(b * HEADS_PER_MIXER + h) * tq:(b * HEADS_PER_MIXER + h + 1) * tq]
        tail_cols = []
        tails = list(tails)
        for b in range(len(starts)):
            tail_cols += tails
            tails = [tails[h] + piece(block_sum, b, h) for h in heads]
        a = jnp.exp(z - softplus + inside + jnp.concatenate(tail_cols, axis=0))
        if keep is not None:
            a = a * keep
        a = a.astype(BF16)
        av = [sum(jnp.dot(piece(a, b, h), v_ref[0, pl.ds(st, tq), _head_cols(h)], preferred_element_type=F32)
                  for b, st in enumerate(starts)) for h in heads]
        return tails, av

    starts = [pl.multiple_of(jnp.maximum(i - n, 0) * tq, tq) for n in range(SB_FIRST_BLOCKS)]
    keeps = [jnp.where(lane < row, 1.0, 0.0)]
    keeps += [jnp.full((tq, tq), jnp.where(i >= n, 1.0, 0.0), F32) for n in range(1, SB_FIRST_BLOCKS)]
    tails, accs = blocks(starts, keeps, [jnp.zeros((tq, 1), F32)] * HEADS_PER_MIXER)
    state = tuple(zip(tails, accs))

    def body(carry):
        j, state = carry
        tails, av = blocks([pl.multiple_of(j * tq, tq)], [None], [tail for tail, _ in state])
        return j - 1, tuple((tails[h], state[h][1] + av[h]) for h in heads)

    def cond(carry):
        j, state = carry
        worst = functools.reduce(jnp.maximum, [tail for tail, _ in state])
        return jnp.logical_and(j >= 0, jnp.max(worst) > -SB_UNDERFLOW)

    _, state = lax.while_loop(cond, body, (i - SB_FIRST_BLOCKS, state))
    for h in range(HEADS_PER_MIXER):
        o_ref[0, :, _head_cols(h)] = _head_norm(state[h][1], g_ref[h:h + 1, :]).astype(o_ref.dtype)


def _stick_breaking_mixer(p, g_heads):
    b, seq, _ = p.shape
    tq = min(Q_TILE, seq)
    return pl.pallas_call(
        _sb_kernel,
        grid=(b, seq // tq),
        in_specs=[pl.BlockSpec((1, tq, MIXER_WIDTH), lambda bi, i: (bi, i, 0)),
                  pl.BlockSpec((1, seq, MIXER_WIDTH), lambda bi, i: (bi, 0, 1)),
                  pl.BlockSpec((1, seq, MIXER_WIDTH), lambda bi, i: (bi, 0, 2)),
                  pl.BlockSpec((HEADS_PER_MIXER, HEAD_DIM), lambda bi, i: (0, 0))],
        out_specs=pl.BlockSpec((1, tq, MIXER_WIDTH), lambda bi, i: (bi, i, 0)),
        out_shape=jax.ShapeDtypeStruct((b, seq, MIXER_WIDTH), BF16),
        compiler_params=_params("arbitrary", "arbitrary"),
        name="stick_breaking",
    )(p, p, p, g_heads)


def _dsa_kernel(q_ref, k_ref, v_ref, qi_ref, ki_ref, w_ref, g_ref, o_ref, sc_ref, sc16_ref, s_ref, p_ref, *, slopes):
    tq = q_ref.shape[1]
    tk = sc_ref.shape[1]
    i = pl.program_id(1)
    n_blocks = ((i + 1) * tq + tk - 1) // tk
    qpos = i * tq + lax.broadcasted_iota(jnp.int32, (1, tq), 1)
    key_row = lax.broadcasted_iota(jnp.int32, (tk, 1), 0)
    neg_inf = float("-inf")

    w_t = w_ref[0].T * (IDX_HEADS ** -0.5 * IDX_DIM ** -0.5)
    w_rows = [w_t[h:h + 1, :] for h in range(IDX_HEADS)]
    qi = qi_ref[0]

    def top_half(x):
        bits = lax.bitcast_convert_type(x, jnp.int32) & jnp.int32(-65536)
        return lax.bitcast_convert_type(bits, F32).astype(BF16)

    def score_body(j, carry, slot):
        start = pl.multiple_of(j * tk, tk)
        ki = ki_ref[0, pl.ds(start, tk), 0:IDX_DIM]
        sc = jnp.zeros((tk, tq), F32)
        for h in range(IDX_HEADS):
            x = lax.dot_general(ki, qi[:, h * IDX_DIM:(h + 1) * IDX_DIM], _NT, preferred_element_type=F32)
            sc = sc + w_rows[h] * jnp.maximum(x, 0.0)
        sc = jnp.where((start + key_row) <= qpos, sc + 0.0, neg_inf)
        sc_ref[j] = sc
        sc16_ref[j] = top_half(sc)
        return carry

    _paired_loop(n_blocks, score_body, 0)

    def count(pred):
        def add_block(j, cnt, slot):
            c = jnp.where(pred(sc_ref[j]), 1.0, 0.0)
            return cnt + jnp.sum(c.reshape(tk // 8, 8, tq), axis=0)

        cnt = _paired_loop(n_blocks, add_block, jnp.zeros((8, tq), F32))
        return jnp.sum(cnt, axis=0, keepdims=True)

    int_min = jnp.int32(-2 ** 31)

    def ordered_to_float(u):
        key = u ^ int_min
        bits = jnp.where(key >= 0, key, key ^ jnp.int32(0x7FFFFFFF))
        return lax.bitcast_convert_type(bits, F32)

    def count_top_half(cand16):
        def add_block(j, cnt, slot):
            c = jnp.where(sc16_ref[j] >= cand16, jnp.ones((), BF16), jnp.zeros((), BF16))
            for g in range(tk // 16):
                cnt = cnt + c[g * 16:(g + 1) * 16]
            return cnt

        cnt = _paired_loop(n_blocks, add_block, jnp.zeros((16, tq), BF16))
        return jnp.sum(cnt.astype(F32), axis=0, keepdims=True)

    def search_body(step, carry, half):
        prefix, n_at_prefix = carry
        cand = prefix | jnp.left_shift(jnp.int32(1), 31 - step)
        cand_f = ordered_to_float(cand)
        n_ge = count_top_half(top_half(cand_f)) if half else count(lambda s: s >= cand_f)
        keep = n_ge >= DSA_TOPK
        return jnp.where(keep, cand, prefix), jnp.where(keep, n_ge, n_at_prefix)

    keep_all = (qpos + 1) <= DSA_TOPK

    carry = (jnp.zeros((1, tq), jnp.int32), jnp.zeros((1, tq), F32))
    carry = lax.fori_loop(0, 16, functools.partial(search_body, half=True), carry)
    prefix, n_ge_tau = lax.fori_loop(16, 32, functools.partial(search_body, half=False), carry)
    tau = jnp.where(keep_all, float(np.finfo(np.float32).min), ordered_to_float(prefix))
    no_ties = jnp.min(jnp.where(jnp.logical_or(keep_all, n_ge_tau == DSA_TOPK), 1.0, 0.0)) > 0.5

    earlier = (lax.broadcasted_iota(jnp.int32, (tk, tk), 1) < lax.broadcasted_iota(jnp.int32, (tk, tk), 0))
    earlier = jnp.where(earlier, 1.0, 0.0).astype(BF16)
    qs = [q_ref[0, :, _head_cols(h)] * ATTN_SCALE for h in range(HEADS_PER_MIXER)]
    key_bias = [slopes[h] * lax.broadcasted_iota(jnp.int32, (tk, tq), 0).astype(F32)
                for h in range(HEADS_PER_MIXER)]

    def attn_body(j, carry, slot, n_ties=None):
        ties_seen, stats = carry
        s_slot = [s_ref.at[slot * HEADS_PER_MIXER + h] for h in range(HEADS_PER_MIXER)]
        p_slot = [p_ref.at[slot * HEADS_PER_MIXER + h] for h in range(HEADS_PER_MIXER)]
        start = pl.multiple_of(j * tk, tk)
        sc = sc_ref[j]
        if n_ties is None:
            sel = sc >= tau
        else:
            tie = jnp.where(sc == tau, 1.0, 0.0)
            rank = jnp.dot(earlier, tie.astype(BF16), preferred_element_type=F32) + ties_seen
            sel = jnp.where(sc > tau, 1.0, jnp.where(rank < n_ties, tie, 0.0)) > 0.5
            ties_seen = ties_seen + jnp.sum(tie, axis=0, keepdims=True)
        block_pos = (j * tk - i * tq).astype(F32)
        for h in range(HEADS_PER_MIXER):
            k = k_ref[0, pl.ds(start, tk), _head_cols(h)]
            s = lax.dot_general(k, qs[h], _NT, preferred_element_type=F32) + key_bias[h]
            s_slot[h][...] = jnp.where(sel, s, MASKED)
        locals_ = []
        for h in range(HEADS_PER_MIXER):
            s = s_slot[h][...]
            m_loc = jnp.max(s, axis=0, keepdims=True)
            p = jnp.exp(s - m_loc)
            locals_.append((m_loc, jnp.sum(p, axis=0, keepdims=True)))
            p_slot[h][...] = p.astype(BF16)
        new_stats = []
        for h in range(HEADS_PER_MIXER):
            m, l, acc = stats[h]
            m_loc, l_blk = locals_[h]
            v = v_ref[0, pl.ds(start, tk), _head_cols(h)]
            pv_blk = lax.dot_general(v, p_slot[h][...], _TN, preferred_element_type=F32)
            m_blk = m_loc + slopes[h] * block_pos
            m_new = jnp.maximum(m, m_blk)
            w_old = jnp.exp(m - m_new)
            w_blk = jnp.where(m_loc > 0.5 * MASKED, jnp.exp(m_blk - m_new), 0.0)
            new_stats.append((m_new, w_old * l + w_blk * l_blk, w_old * acc + w_blk * pv_blk))
        return ties_seen, tuple(new_stats)

    init = tuple((jnp.full((1, tq), MASKED, F32), jnp.zeros((1, tq), F32), jnp.zeros((HEAD_DIM, tq), F32))
                 for _ in range(HEADS_PER_MIXER))
    start_carry = (jnp.zeros((1, tq), F32), init)

    def attend_without_ties():
        return _paired_loop(n_blocks, attn_body, start_carry)[1]

    def attend_with_ties():
        n_gt = count(lambda s: s > tau)
        n_ties = jnp.where(keep_all, 1e9, DSA_TOPK - n_gt)
        return _paired_loop(n_blocks, functools.partial(attn_body, n_ties=n_ties), start_carry)[1]

    stats = lax.cond(no_ties, attend_without_ties, attend_with_ties)
    outs = []
    for h in range(HEADS_PER_MIXER):
        _, l, acc = stats[h]
        o = acc / l
        outs.append(o * lax.rsqrt(jnp.mean(o * o, axis=0, keepdims=True) + NORM_EPS) * g_ref[:, h:h + 1])
    o_ref[0] = jnp.concatenate(outs, axis=0).T.astype(o_ref.dtype)


def _dsa_mixer(p, p_idx, p_w, g_heads):
    b, seq, _ = p.shape
    tq = min(4 * Q_TILE, seq)
    tk = tq
    assert seq // 16 <= 256, "the packed-bf16 count accumulators are exact only up to 256 adds"
    qi_width = IDX_HEADS * IDX_DIM
    return pl.pallas_call(
        functools.partial(_dsa_kernel, slopes=_alibi_slopes(0)),
        grid=(b, seq // tq),
        in_specs=[pl.BlockSpec((1, tq, MIXER_WIDTH), lambda bi, i: (bi, i, 0)),
                  pl.BlockSpec((1, seq, MIXER_WIDTH), lambda bi, i: (bi, 0, 1)),
                  pl.BlockSpec((1, seq, MIXER_WIDTH), lambda bi, i: (bi, 0, 2)),
                  pl.BlockSpec((1, tq, qi_width), lambda bi, i: (bi, i, 0)),
                  pl.BlockSpec((1, seq, LANES), lambda bi, i: (bi, 0, qi_width // LANES)),
                  pl.BlockSpec((1, tq, LANES), lambda bi, i: (bi, i, 0)),
                  pl.BlockSpec((HEAD_DIM, HEADS_PER_MIXER), lambda bi, i: (0, 0))],
        out_specs=pl.BlockSpec((1, tq, MIXER_WIDTH), lambda bi, i: (bi, i, 0)),
        out_shape=jax.ShapeDtypeStruct((b, seq, MIXER_WIDTH), BF16),
        scratch_shapes=[pltpu.VMEM((seq // tk, tk, tq), F32),
                        pltpu.VMEM((seq // tk, tk, tq), BF16),
                        pltpu.VMEM((2 * HEADS_PER_MIXER, tk, tq), F32),
                        pltpu.VMEM((2 * HEADS_PER_MIXER, tk, tq), BF16)],
        compiler_params=_params("arbitrary", "arbitrary"),
        name="dsa",
    )(p, p, p, p_idx, p_idx, p_w, g_heads.T)


def _band_kernel(q_ref, kp_ref, kc_ref, vp_ref, vc_ref, o_ref, lse_ref, *, dilation, slopes):
    tq = q_ref.shape[1]
    ui = pl.program_id(2)
    heads = range(HEADS_PER_MIXER)
    u_q = jnp.concatenate([ui * tq + lax.broadcasted_iota(jnp.int32, (tq, 1), 0)] * HEADS_PER_MIXER, axis=0)
    u_k = ui * tq - DILATED_STEPS + lax.broadcasted_iota(jnp.int32, (1, DILATED_STEPS + tq), 1)
    steps = u_q - u_k
    valid = jnp.logical_and(jnp.logical_and(steps >= 0, steps <= DILATED_STEPS), u_k >= 0)
    slope_col = jnp.concatenate([jnp.full((tq, 1), slopes[h], F32) for h in heads], axis=0)
    s = jnp.concatenate(
        [lax.dot_general(q_ref[0, :, _head_cols(h)],
                         jnp.concatenate([kp_ref[0, :, _head_cols(h)], kc_ref[0, :, _head_cols(h)]], axis=0),
                         _NT, preferred_element_type=F32) for h in heads], axis=0)
    s = jnp.where(valid, s * ATTN_SCALE - slope_col * (steps * dilation).astype(F32), MASKED)
    m = jnp.max(s, axis=1, keepdims=True)
    e = jnp.exp(s - m)
    den = jnp.sum(e, axis=1, keepdims=True)
    p = e.astype(BF16)
    lse = m + jnp.log(den)
    lane = lax.broadcasted_iota(jnp.int32, (1, LANES), 1)
    lse_all = jnp.zeros((tq, LANES), F32)
    for h in heads:
        rows = slice(h * tq, (h + 1) * tq)
        v = jnp.concatenate([vp_ref[0, :, _head_cols(h)], vc_ref[0, :, _head_cols(h)]], axis=0)
        o_ref[0, :, _head_cols(h)] = jnp.dot(p[rows], v, preferred_element_type=F32) / den[rows]
        lse_all = lse_all + jnp.where(lane == h, lse[rows], 0.0)
    lse_ref[0] = lse_all


def _dilated_branch(view, dilation, slopes):
    b, length, _ = view.shape
    classes = dilation
    tq = min(2 * Q_TILE, length)
    back = tq // DILATED_STEPS
    spec = lambda part, prev: (
        pl.BlockSpec((1, DILATED_STEPS, MIXER_WIDTH),
                     lambda bi, c, ui: (bi, jnp.maximum(ui * back - 1, 0), c * 3 + part)) if prev
        else pl.BlockSpec((1, tq, MIXER_WIDTH), lambda bi, c, ui: (bi, ui, c * 3 + part)))
    out, lse = pl.pallas_call(
        functools.partial(_band_kernel, dilation=dilation, slopes=slopes),
        grid=(b, classes, length // tq),
        in_specs=[spec(0, False), spec(1, True), spec(1, False), spec(2, True), spec(2, False)],
        out_specs=[pl.BlockSpec((1, tq, MIXER_WIDTH), lambda bi, c, ui: (bi, ui, c)),
                   pl.BlockSpec((1, tq, LANES), lambda bi, c, ui: (bi, ui, c))],
        out_shape=[jax.ShapeDtypeStruct((b, length, classes * MIXER_WIDTH), F32),
                   jax.ShapeDtypeStruct((b, length, classes * LANES), F32)],
        compiler_params=_params("arbitrary", "arbitrary", "arbitrary"),
        name=f"dilated_r{dilation}",
    )(view, view, view, view, view)
    return out, lse


def _merge_dilated(o_refs, l_refs, g_ref, scratch, rows):
    chunks = MIXER_WIDTH // LANES
    outs = [[o_refs[0][:, s * LANES:(s + 1) * LANES] for s in range(chunks)]]
    lses = [l_refs[0][...]]
    for r, o_view, l_view, o_nat, l_nat in zip(DILATIONS[1:], o_refs[1:], l_refs[1:], scratch[0::2], scratch[1::2]):
        for c in range(r):
            for s in range(chunks):
                col = c * MIXER_WIDTH + s * LANES
                o_nat[pl.ds(s * rows + c, rows // r, stride=r), :] = o_view[:, col:col + LANES]
            l_nat[pl.ds(c, rows // r, stride=r), :] = l_view[:, c * LANES:(c + 1) * LANES]
        outs.append([o_nat[s * rows:(s + 1) * rows, :] for s in range(chunks)])
        lses.append(l_nat[...])
    heads_per_chunk = LANES // HEAD_DIM
    merged = []
    for h in range(HEADS_PER_MIXER):
        sub = slice((h % heads_per_chunk) * HEAD_DIM, (h % heads_per_chunk + 1) * HEAD_DIM)
        lse_h = [l[:, h:h + 1] for l in lses]
        top = functools.reduce(jnp.maximum, lse_h)
        wts = [jnp.exp(l - top) for l in lse_h]
        mixed = sum(w * o[h // heads_per_chunk][:, sub] for w, o in zip(wts, outs)) / sum(wts)
        merged.append(_head_norm(mixed, g_ref[h:h + 1, :]))
    return merged


def _dilated_branches(views):
    slopes = _alibi_slopes(1)
    branches = [_dilated_branch(v, r, slopes) for v, r in zip(views, DILATIONS)]
    flat = lambda a: a.reshape(-1, a.shape[-1])
    return [flat(o) for o, _ in branches], [flat(l) for _, l in branches]


def _moba_kernel(q_ref, k_ref, v_ref, g_ref, o_ref, kmean_ref, chosen_ref, s_ref, p_ref, *, slopes):
    tq = q_ref.shape[1]
    n_kv = k_ref.shape[1] // MOBA_BLOCK
    own = pl.program_id(1)

    @pl.when(own == 0)
    def _():
        kmean_ref[...] = jnp.zeros_like(kmean_ref)
        for n in range(n_kv):
            blk = k_ref[0, n * MOBA_BLOCK:(n + 1) * MOBA_BLOCK, :].astype(F32)
            kmean_ref[n:n + 1, :] = jnp.sum(blk, axis=0, keepdims=True) * (1.0 / MOBA_BLOCK)

    key_row = lax.broadcasted_iota(jnp.int32, (MOBA_BLOCK, 1), 0)
    q_lane = lax.broadcasted_iota(jnp.int32, (1, tq), 1)
    blk_rows = kmean_ref.shape[0]
    blk_row = lax.broadcasted_iota(jnp.int32, (blk_rows, 1), 0)
    blk_f = blk_row.astype(F32)
    neg_inf = float("-inf")
    group = tq // MOBA_BLOCK
    first_own = own * group
    sub_block = sum(jnp.where(q_lane >= g * MOBA_BLOCK, 1, 0) for g in range(1, group)) if group > 1 else 0
    own_blk = first_own + sub_block
    q_local = q_lane - sub_block * MOBA_BLOCK

    qs = []
    for h in range(HEADS_PER_MIXER):
        q = q_ref[0, :, _head_cols(h)]
        gate = lax.dot_general(kmean_ref[:, _head_cols(h)], q.astype(F32), _NT, preferred_element_type=F32,
                               precision=lax.Precision.HIGHEST)
        gate = jnp.where(blk_row < own_blk, gate, neg_inf)
        picks = jnp.zeros((blk_rows, tq), F32)
        for _ in range(MOBA_TOPK):
            top = jnp.max(gate, axis=0, keepdims=True)
            is_top = jnp.logical_and(gate == top, top > neg_inf)
            first = jnp.min(jnp.where(is_top, blk_f, float(blk_rows)), axis=0, keepdims=True)
            pick = blk_f == first
            picks = jnp.where(pick, 1.0, picks)
            gate = jnp.where(pick, neg_inf, gate)
        chosen_ref[h] = picks
        qs.append(q * ATTN_SCALE)

    key_bias = [slopes[h] * lax.broadcasted_iota(jnp.int32, (MOBA_BLOCK, tq), 0).astype(F32)
                for h in range(HEADS_PER_MIXER)]

    def block_softmax(start, keep=None, slot=0):
        s_slot = [s_ref.at[slot * HEADS_PER_MIXER + h] for h in range(HEADS_PER_MIXER)]
        p_slot = [p_ref.at[slot * HEADS_PER_MIXER + h] for h in range(HEADS_PER_MIXER)]
        for h in range(HEADS_PER_MIXER):
            k = k_ref[0, pl.ds(start, MOBA_BLOCK), _head_cols(h)]
            s = lax.dot_general(k, qs[h], _NT, preferred_element_type=F32) + key_bias[h]
            s_slot[h][...] = s if keep is None else jnp.where(keep, s, MASKED)
        pieces = []
        for h in range(HEADS_PER_MIXER):
            s = s_slot[h][...]
            m_loc = jnp.max(s, axis=0, keepdims=True)
            p = jnp.exp(s - m_loc)
            pieces.append((m_loc, jnp.sum(p, axis=0, keepdims=True)))
            p_slot[h][...] = p.astype(BF16)
        out = []
        for h in range(HEADS_PER_MIXER):
            v = v_ref[0, pl.ds(start, MOBA_BLOCK), _head_cols(h)]
            out.append(pieces[h] + (lax.dot_general(v, p_slot[h][...], _TN, preferred_element_type=F32),))
        return out

    def merge(stats, j, pieces, use):
        block_pos = ((j - first_own) * MOBA_BLOCK).astype(F32)
        merged = []
        for h, (m_loc, l_blk, pv_blk) in enumerate(pieces):
            m_blk = jnp.where(use[h], m_loc + slopes[h] * block_pos, MASKED)
            m, l, acc = stats[h]
            m_new = jnp.maximum(m, m_blk)
            w_old = jnp.exp(m - m_new)
            w_blk = jnp.where(use[h], jnp.exp(m_blk - m_new), 0.0)
            merged.append((m_new, w_old * l + w_blk * l_blk, w_old * acc + w_blk * pv_blk))
        return tuple(merged)

    def picked(h, j):
        return chosen_ref[h, pl.ds(j, 1), :] > 0.5

    def body(j, stats, slot):
        start = pl.multiple_of(j * MOBA_BLOCK, MOBA_BLOCK)
        return merge(stats, j, block_softmax(start, slot=slot), [picked(h, j) for h in range(HEADS_PER_MIXER)])

    stats = tuple((jnp.full((1, tq), MASKED, F32), jnp.zeros((1, tq), F32), jnp.zeros((HEAD_DIM, tq), F32))
                  for _ in range(HEADS_PER_MIXER))
    stats = _paired_loop(first_own, body, stats)
    for g in range(group):
        j = first_own + g
        keep = jnp.logical_or(own_blk > j, jnp.logical_and(own_blk == j, key_row <= q_local))
        pieces = block_softmax(pl.multiple_of(j * MOBA_BLOCK, MOBA_BLOCK), keep=keep, slot=g % 2)
        use = [jnp.logical_or(own_blk == j, jnp.logical_and(own_blk > j, picked(h, j)))
               for h in range(HEADS_PER_MIXER)]
        stats = merge(stats, j, pieces, use)
    outs = []
    for h in range(HEADS_PER_MIXER):
        _, l, acc = stats[h]
        o = acc / l
        outs.append(o * lax.rsqrt(jnp.mean(o * o, axis=0, keepdims=True) + NORM_EPS) * g_ref[:, h:h + 1])
    o_ref[0] = jnp.concatenate(outs, axis=0).T.astype(o_ref.dtype)


def _moba_mixer(p, g_heads):
    b, seq, _ = p.shape
    tq = MOBA_GROUP * MOBA_BLOCK if seq % (MOBA_GROUP * MOBA_BLOCK) == 0 else MOBA_BLOCK
    blk_rows =-(-(seq // MOBA_BLOCK) // SUBLANES) * SUBLANES
    return pl.pallas_call(
        functools.partial(_moba_kernel, slopes=_alibi_slopes(2)),
        grid=(b, seq // tq),
        in_specs=[pl.BlockSpec((1, tq, MIXER_WIDTH), lambda bi, i: (bi, i, 0)),
                  pl.BlockSpec((1, seq, MIXER_WIDTH), lambda bi, i: (bi, 0, 1)),
                  pl.BlockSpec((1, seq, MIXER_WIDTH), lambda bi, i: (bi, 0, 2)),
                  pl.BlockSpec((HEAD_DIM, HEADS_PER_MIXER), lambda bi, i: (0, 0))],
        out_specs=pl.BlockSpec((1, tq, MIXER_WIDTH), lambda bi, i: (bi, i, 0)),
        out_shape=jax.ShapeDtypeStruct((b, seq, MIXER_WIDTH), BF16),
        scratch_shapes=[pltpu.VMEM((blk_rows, MIXER_WIDTH), F32),
                        pltpu.VMEM((HEADS_PER_MIXER, blk_rows, tq), F32),
                        pltpu.VMEM((2 * HEADS_PER_MIXER, MOBA_BLOCK, tq), F32),
                        pltpu.VMEM((2 * HEADS_PER_MIXER, MOBA_BLOCK, tq), BF16)],
        compiler_params=_params("arbitrary", "arbitrary"),
        name="moba",
    )(p, p, p, g_heads.T)


def _out_proj_kernel(oa, ob, od, o1, o4, o16, l1, l4, l16, gc_ref, w_ref, x_ref, mod_ref, o_ref, *scratch):
    rows = x_ref.shape[0]
    merged_c = _merge_dilated((o1, o4, o16), (l1, l4, l16), gc_ref, scratch, rows)
    oc = jnp.concatenate(merged_c, axis=1).astype(BF16)
    acc = jnp.zeros(x_ref.shape, F32)
    for m, o in enumerate((oa[...], ob[...], oc, od[...])):
        acc = acc + jnp.dot(o, w_ref[m * MIXER_WIDTH:(m + 1) * MIXER_WIDTH, :], preferred_element_type=F32)
    o_ref[...] = x_ref[...] + mod_ref[0, 2:3, :] * acc


def _out_projection(o_a, o_b, o_d, dilated, g_heads_c, w_out, x2, mod, seq):
    t, d = x2.shape
    tm = min(ROW_TILE, seq)
    per_batch = seq // tm
    o_spec = pl.BlockSpec((tm, MIXER_WIDTH), lambda i: (i, 0))
    branch_outs, branch_lses = dilated
    view_specs = ([pl.BlockSpec((tm // r, r * MIXER_WIDTH), lambda i: (i, 0)) for r in DILATIONS]
                  + [pl.BlockSpec((tm // r, r * LANES), lambda i: (i, 0)) for r in DILATIONS])
    scratch = []
    for _ in DILATIONS[1:]:
        scratch += [pltpu.VMEM((tm * (MIXER_WIDTH // LANES), LANES), F32), pltpu.VMEM((tm, LANES), F32)]
    return pl.pallas_call(
        _out_proj_kernel,
        grid=(t // tm,),
        in_specs=[o_spec] * 3 + view_specs + [pl.BlockSpec((HEADS_PER_MIXER, HEAD_DIM), lambda i: (0, 0)),
                                              pl.BlockSpec(w_out.shape, lambda i: (0, 0)),
                                              pl.BlockSpec((tm, d), lambda i: (i, 0)),
                                              pl.BlockSpec((1, 6, d), lambda i: (i // per_batch, 0, 0))],
        out_specs=pl.BlockSpec((tm, d), lambda i: (i, 0)),
        out_shape=jax.ShapeDtypeStruct((t, d), F32),
        scratch_shapes=scratch,
        compiler_params=_params("arbitrary"),
        name="out_projection",
    )(*[o.reshape(t, MIXER_WIDTH) for o in (o_a, o_b, o_d)], *branch_outs, *branch_lses, g_heads_c, w_out, x2, mod)


def _rms_gain(x, gain):
    return x * lax.rsqrt(jnp.mean(x * x, axis=-1, keepdims=True) + NORM_EPS) * gain


def _ffn_kernel(x_ref, mod_ref, g_ref, fg_ref, wg_ref, wu_ref, wd_ref, o_ref, h_ref, acc_ref, *, final_norm):
    f = pl.program_id(1)

    @pl.when(f == 0)
    def _():
        h_ref[...] = _modulated_norm(x_ref[...], g_ref[...], mod_ref[0, 3:4, :], mod_ref[0, 4:5, :]).astype(BF16)
        acc_ref[...] = jnp.zeros_like(acc_ref)

    h = h_ref[...]
    gate = jnp.dot(h, wg_ref[...], preferred_element_type=F32)
    up = jnp.dot(h, wu_ref[...], preferred_element_type=F32)
    act = (gate / (1.0 + jnp.exp(-gate)) * up).astype(BF16)
    acc_ref[...] += jnp.dot(act, wd_ref[...], preferred_element_type=F32)

    @pl.when(f == pl.num_programs(1) - 1)
    def _():
        out = x_ref[...] + mod_ref[0, 5:6, :] * acc_ref[...]
        o_ref[...] = _rms_gain(out, fg_ref[...]) if final_norm else out


def _dense_ffn(x2, mod, g_ffn, wg, wu, wd, seq, g_final, final_norm):
    t, d = x2.shape
    d_ff = wg.shape[1]
    tm = min(ROW_TILE, seq)
    tf = FFN_TILE if d_ff % FFN_TILE == 0 else d_ff
    per_batch = seq // tm
    return pl.pallas_call(
        functools.partial(_ffn_kernel, final_norm=final_norm),
        grid=(t // tm, d_ff // tf),
        in_specs=[pl.BlockSpec((tm, d), lambda i, f: (i, 0)),
                  pl.BlockSpec((1, 6, d), lambda i, f: (i // per_batch, 0, 0)),
                  pl.BlockSpec((1, d), lambda i, f: (0, 0)),
                  pl.BlockSpec((1, d), lambda i, f: (0, 0)),
                  pl.BlockSpec((d, tf), lambda i, f: (0, f)),
                  pl.BlockSpec((d, tf), lambda i, f: (0, f)),
                  pl.BlockSpec((tf, d), lambda i, f: (f, 0))],
        out_specs=pl.BlockSpec((tm, d), lambda i, f: (i, 0)),
        out_shape=jax.ShapeDtypeStruct((t, d), F32),
        scratch_shapes=[pltpu.VMEM((tm, d), BF16), pltpu.VMEM((tm, d), F32)],
        compiler_params=_params("arbitrary", "arbitrary"),
        name="dense_ffn",
    )(x2, mod, g_ffn.reshape(1, d), g_final.reshape(1, d), wg.astype(BF16), wu.astype(BF16), wd.astype(BF16))


def _store_row_tiles(dst_ref, value):
    rows, d = value.shape
    chunks = d // LANES
    for s in range(chunks):
        dst_ref[pl.ds(s, rows, stride=chunks), :] = value[:, s * LANES:(s + 1) * LANES]


def _load_row_tile_chunk(src_ref, s, rows, chunks):
    return src_ref[pl.ds(s, rows, stride=chunks), :]


def _router_kernel(x_ref, mod_ref, g_ref, wr_ref, h_ref, logit_ref):
    h = _modulated_norm(x_ref[...], g_ref[...], mod_ref[0, 3:4, :], mod_ref[0, 4:5, :])
    _store_row_tiles(h_ref, h)
    logit_ref[...] = jnp.dot(h, wr_ref[...], preferred_element_type=F32, precision=lax.Precision.HIGHEST)


def _router(x2, mod, g_ffn, w_router, seq):
    t, d = x2.shape
    tm = min(ROW_TILE, seq)
    per_batch = seq // tm
    wr = jnp.zeros((d, LANES), F32).at[:, :N_EXPERTS].set(w_router.astype(F32))
    return pl.pallas_call(
        _router_kernel,
        grid=(t // tm,),
        in_specs=[pl.BlockSpec((tm, d), lambda i: (i, 0)),
                  pl.BlockSpec((1, 6, d), lambda i: (i // per_batch, 0, 0)),
                  pl.BlockSpec((1, d), lambda i: (0, 0)),
                  pl.BlockSpec((d, LANES), lambda i: (0, 0))],
        out_specs=[pl.BlockSpec((tm * (d // LANES), LANES), lambda i: (i, 0)),
                   pl.BlockSpec((tm, LANES), lambda i: (i, 0))],
        out_shape=[jax.ShapeDtypeStruct((t * (d // LANES), LANES), F32), jax.ShapeDtypeStruct((t, LANES), F32)],
        compiler_params=_params("arbitrary"),
        name="moe_router",
    )(x2, mod, g_ffn.reshape(1, d), wr)


def _row_copy(src_hbm, row, dst_ref, r, sem, chunks):
    src = src_hbm.at[pl.ds(pl.multiple_of(row * chunks, chunks), chunks), :]
    return pltpu.make_async_copy(src, dst_ref.at[pl.ds(pl.multiple_of(r * chunks, chunks), chunks), :], sem)


def _expert_kernel(be_ref, used_ref, cur_idx_ref, nxt_idx_ref, h_hbm, wg_ref, wu_ref, wd_ref, o_ref,
                   xin_ref, xb_ref, acc_ref, sem):
    m = pl.program_id(0)
    f = pl.program_id(1)
    n_used = used_ref[0]
    live = m < n_used
    rows, d = xb_ref.shape
    chunks = d // LANES

    def row_copies(idx_ref, slot, start):
        if not start:
            half = rows * chunks // 2
            for lane in range(2):
                pltpu.make_async_copy(h_hbm.at[pl.ds(0, half), :], xin_ref.at[slot, pl.ds(0, half), :],
                                      sem.at[slot, lane]).wait()
            return

        def body(group, c):
            for u in range(ROWS_PER_TRIP):
                r = ROWS_PER_TRIP * group + u
                _row_copy(h_hbm, idx_ref[0, 0, r], xin_ref.at[slot], r, sem.at[slot, u % 2],
                          chunks).start(priority=u % 2)
            return c
        lax.fori_loop(0, rows // ROWS_PER_TRIP, body, 0)

    @pl.when(jnp.logical_and(live, f == 0))
    def _():
        @pl.when(m == 0)
        def _():
            row_copies(cur_idx_ref, 0, start=True)

        for slot in range(2):
            @pl.when(m % 2 == slot)
            def _(slot=slot):
                row_copies(cur_idx_ref, slot, start=False)
                for s in range(chunks):
                    xb_ref[:, s * LANES:(s + 1) * LANES] = _load_row_tile_chunk(
                        xin_ref.at[slot], s, rows, chunks).astype(BF16)

                @pl.when(m + 1 < n_used)
                def _():
                    row_copies(nxt_idx_ref, 1 - slot, start=True)

        acc_ref[...] = jnp.zeros_like(acc_ref)

    @pl.when(live)
    def _():
        h = xb_ref[...]
        gate = jnp.dot(h, wg_ref[0], preferred_element_type=F32)
        up = jnp.dot(h, wu_ref[0], preferred_element_type=F32)
        act = (gate / (1.0 + jnp.exp(-gate)) * up).astype(BF16)
        acc_ref[...] += jnp.dot(act, wd_ref[0], preferred_element_type=F32)

    @pl.when(f == pl.num_programs(1) - 1)
    def _():
        _store_row_tiles(o_ref, jnp.where(live, acc_ref[...], 0.0))


def _expert_ffn(h_rows, slot_tok, block_expert, n_used, wg, wu, wd):
    d = wg.shape[1]
    chunks = d // LANES
    n_slots = slot_tok.shape[0]
    d_ff = wg.shape[2]
    tm = EXPERT_ROWS
    tf = EXPERT_FF_TILE if d_ff % EXPERT_FF_TILE == 0 else d_ff
    nf = d_ff // tf
    nm = n_slots // tm

    def fcol(m, f, used):
        return jnp.where(m < used[0], f, nf - 1)

    idx_spec = lambda shift: pl.BlockSpec((1, 1, tm), lambda m, f, be, used: (jnp.minimum(m + shift, nm - 1), 0, 0),
                                          memory_space=pltpu.SMEM)
    grid_spec = pltpu.PrefetchScalarGridSpec(
        num_scalar_prefetch=2,
        grid=(nm, nf),
        in_specs=[idx_spec(0), idx_spec(1), pl.BlockSpec(memory_space=pl.ANY),
                  pl.BlockSpec((1, d, tf), lambda m, f, be, used: (be[m], 0, fcol(m, f, used))),
                  pl.BlockSpec((1, d, tf), lambda m, f, be, used: (be[m], 0, fcol(m, f, used))),
                  pl.BlockSpec((1, tf, d), lambda m, f, be, used: (be[m], fcol(m, f, used), 0))],
        out_specs=pl.BlockSpec((tm * chunks, LANES), lambda m, f, be, used: (m, 0)),
        scratch_shapes=[pltpu.VMEM((2, tm * chunks, LANES), F32), pltpu.VMEM((tm, d), BF16),
                        pltpu.VMEM((tm, d), F32), pltpu.SemaphoreType.DMA((2, 2))])
    idx = slot_tok.reshape(nm, 1, tm)
    return pl.pallas_call(
        _expert_kernel,
        grid_spec=grid_spec,
        out_shape=jax.ShapeDtypeStruct((n_slots * chunks, LANES), F32),
        compiler_params=_params("arbitrary", "arbitrary"),
        name="expert_ffn",
    )(block_expert, n_used, idx, idx, h_rows, wg.astype(BF16), wu.astype(BF16), wd.astype(BF16))


def _combine_kernel(d0_ref, d1_ref, ys_hbm, x_ref, gates_ref, mod_ref, fg_ref, o_ref, y0_ref, y1_ref, sem, *,
                    final_norm):
    rows, d = o_ref.shape
    chunks = d // LANES

    def start(group, c):
        for u in range(ROWS_PER_TRIP // 2):
            r = ROWS_PER_TRIP // 2 * group + u
            _row_copy(ys_hbm, d0_ref[0, 0, r], y0_ref, r, sem.at[0], chunks).start(priority=0)
            _row_copy(ys_hbm, d1_ref[0, 0, r], y1_ref, r, sem.at[1], chunks).start(priority=1)
        return c

    lax.fori_loop(0, rows // (ROWS_PER_TRIP // 2), start, 0)
    for k, y_ref in enumerate((y0_ref, y1_ref)):
        pltpu.make_async_copy(ys_hbm.at[pl.ds(0, rows * chunks), :], y_ref, sem.at[k]).wait()
    gates = gates_ref[...]
    for s in range(chunks):
        cols = slice(s * LANES, (s + 1) * LANES)
        y = (_load_row_tile_chunk(y0_ref, s, rows, chunks) * gates[:, 0:1]
             + _load_row_tile_chunk(y1_ref, s, rows, chunks) * gates[:, 1:2])
        o_ref[:, cols] = x_ref[:, cols] + mod_ref[0, 5:6, cols] * y
    if final_norm:
        o_ref[...] = _rms_gain(o_ref[...], fg_ref[...])


def _moe_combine(ys, dest0, dest1, gates, x2, mod, seq, g_final, final_norm):
    t, d = x2.shape
    tm = min(GATHER_ROWS, seq)
    steps = t // tm
    per_batch = seq // tm
    idx_spec = pl.BlockSpec((1, 1, tm), lambda i: (i, 0, 0), memory_space=pltpu.SMEM)
    return pl.pallas_call(
        functools.partial(_combine_kernel, final_norm=final_norm),
        grid=(steps,),
        in_specs=[idx_spec, idx_spec, pl.BlockSpec(memory_space=pl.ANY),
                  pl.BlockSpec((tm, d), lambda i: (i, 0)),
                  pl.BlockSpec((tm, TOP_K_EXPERTS), lambda i: (i, 0)),
                  pl.BlockSpec((1, 6, d), lambda i: (i // per_batch, 0, 0)),
                  pl.BlockSpec((1, d), lambda i: (0, 0))],
        out_specs=pl.BlockSpec((tm, d), lambda i: (i, 0)),
        out_shape=jax.ShapeDtypeStruct((t, d), F32),
        scratch_shapes=[pltpu.VMEM((tm * (d // LANES), LANES), F32), pltpu.VMEM((tm * (d // LANES), LANES), F32),
                        pltpu.SemaphoreType.DMA((2,))],
        compiler_params=_params("arbitrary"),
        name="moe_combine",
    )(dest0.reshape(steps, 1, tm), dest1.reshape(steps, 1, tm), ys, x2, gates, mod, g_final.reshape(1, d))


def _moe_ffn(x2, mod, g_ffn, w_router, wg, wu, wd, seq, g_final, final_norm):
    t, d = x2.shape
    h, logits = _router(x2, mod, g_ffn, w_router, seq)
    top_val, top_idx = lax.top_k(logits[:, :N_EXPERTS], TOP_K_EXPERTS)
    gates = jax.nn.softmax(top_val, axis=-1)

    n_assign = t * TOP_K_EXPERTS
    flat_e = top_idx.reshape(-1).astype(jnp.int32)
    onehot = (flat_e[:, None] == jnp.arange(N_EXPERTS, dtype=jnp.int32)[None, :]).astype(jnp.int32)
    rank = jnp.take_along_axis(jnp.cumsum(onehot, axis=0), flat_e[:, None], axis=1)[:, 0] - 1
    counts = jnp.sum(onehot, axis=0)
    padded = (counts + EXPERT_ROWS - 1) // EXPERT_ROWS * EXPERT_ROWS
    pad_end = jnp.cumsum(padded)
    dest = (pad_end - padded)[flat_e] + rank
    n_slots = (n_assign // EXPERT_ROWS + N_EXPERTS) * EXPERT_ROWS
    n_blocks = n_slots // EXPERT_ROWS
    slot_tok = jnp.zeros((n_slots,), jnp.int32).at[dest].set(jnp.arange(n_assign, dtype=jnp.int32) // TOP_K_EXPERTS)
    block_start = jnp.arange(n_blocks, dtype=jnp.int32) * EXPERT_ROWS
    block_expert = jnp.minimum(jnp.searchsorted(pad_end, block_start, side="right"), N_EXPERTS - 1).astype(jnp.int32)
    n_used = (pad_end[-1:] // EXPERT_ROWS).astype(jnp.int32)

    ys = _expert_ffn(h, slot_tok, block_expert, n_used, wg, wu, wd)
    dest2 = dest.reshape(t, TOP_K_EXPERTS)
    return _moe_combine(ys, dest2[:, 0], dest2[:, 1], gates, x2, mod, seq, g_final, final_norm)


def _token_mixer(x2, mod, g_mix, w_in, g_heads, w_out, batch, seq):
    pa, pb, pc, pd, p_idx, p_w, *pc_views = _in_projection(x2, mod, g_mix, _pack_w_in(w_in), seq)
    shape3 = lambda a: a.reshape(batch, -1, a.shape[-1])
    gh = g_heads.reshape(4, HEADS_PER_MIXER, HEAD_DIM)
    o_a = _stick_breaking_mixer(shape3(pa), gh[0])
    o_b = _dsa_mixer(shape3(pb), shape3(p_idx), shape3(p_w), gh[1])
    dilated = _dilated_branches([shape3(v) for v in [pc] + pc_views])
    o_d = _moba_mixer(shape3(pd), gh[3])
    return _out_projection(o_a, o_b, o_d, dilated, gh[2], w_out.astype(BF16), x2, mod, seq)


def kernel(x, c, w_ada, b_ada, g_mix, w_in, g_heads, w_out, g_ffn, w_ff_gate, w_ff_up, w_ff_down, w_router, w_exp_gate, w_exp_up, w_exp_down, g_final):
    batch, seq, d = x.shape
    depth = w_ada.shape[0]
    mods = _ada_modulation(c, w_ada, b_ada)
    x2 = x.reshape(batch * seq, d)
    for layer in range(depth):
        mod = mods[layer]
        x2 = _token_mixer(x2, mod, g_mix[layer], w_in[layer], g_heads[layer], w_out[layer], batch, seq)
        i = layer // 2
        last = layer == depth - 1
        if layer % 2 == 0:
            x2 = _dense_ffn(x2, mod, g_ffn[layer], w_ff_gate[i], w_ff_up[i], w_ff_down[i], seq, g_final, last)
        else:
            x2 = _moe_ffn(x2, mod, g_ffn[layer], w_router[i], w_exp_gate[i], w_exp_up[i], w_exp_down[i], seq,
                          g_final, last)
    return x2.reshape(batch, seq, d)
```

```python
import functools

import numpy as np
import jax
import jax.numpy as jnp
from jax import lax
from jax.experimental import pallas as pl
from jax.experimental.pallas import tpu as pltpu

F32 = jnp.float32
BF16 = jnp.bfloat16

HEAD_DIM = 64
HEADS_PER_MIXER = 4
MIXER_WIDTH = HEADS_PER_MIXER * HEAD_DIM
QKV_WIDTH = 3 * MIXER_WIDTH
IDX_HEADS = 8
IDX_DIM = 64
DSA_TOPK = 256
DILATIONS = (1, 4, 16)
DILATED_STEPS = 128
MOBA_BLOCK = 256
MOBA_TOPK = 3
MOBA_GROUP = 2
N_EXPERTS = 8
TOP_K_EXPERTS = 2
NORM_EPS = 1e-6
ATTN_SCALE = HEAD_DIM ** -0.5

LANES = 128
SUBLANES = 8
Q_TILE = 128
ROW_TILE = 512
FFN_TILE = 1408
EXPERT_ROWS = 512
EXPERT_FF_TILE = 1792
GATHER_ROWS = 512
ROWS_PER_TRIP = 8
VMEM_LIMIT = 56 * 1024 * 1024
MASKED = -1e30
SB_UNDERFLOW = 104.0
SB_FIRST_BLOCKS = 3

_NT = (((1,), (1,)), ((), ()))
_TN = (((0,), (0,)), ((), ()))


def _alibi_slopes(mixer_pos):
    idx = np.arange(HEADS_PER_MIXER, dtype=np.float32) * 3 + (mixer_pos + 1)
    return tuple(float(s) for s in np.exp2(-8.0 * idx / 12.0).astype(np.float32))


def _params(*semantics):
    return pltpu.CompilerParams(dimension_semantics=semantics, vmem_limit_bytes=VMEM_LIMIT)


def _modulated_norm(x, gain, shift, scale):
    y = x * lax.rsqrt(jnp.mean(x * x, axis=-1, keepdims=True) + NORM_EPS) * gain
    return y * (1.0 + scale) + shift


def _head_norm(acc, gain):
    return acc * lax.rsqrt(jnp.mean(acc * acc, axis=-1, keepdims=True) + NORM_EPS) * gain


def _head_cols(h):
    return slice(h * HEAD_DIM, (h + 1) * HEAD_DIM)


def _paired_loop(n, body, carry):
    carry = lax.fori_loop(0, n // 2, lambda jj, c: body(2 * jj + 1, body(2 * jj, c, 0), 1), carry)
    return lax.cond(n % 2 == 1, lambda c: body(n - 1, c, 0), lambda c: c, carry)


def _ada_kernel(c_ref, w_ref, b_ref, o_ref):
    c = c_ref[...]
    cond = c / (1.0 + jnp.exp(-c))
    o_ref[0, 0] = jnp.dot(cond, w_ref[0], preferred_element_type=F32,
                          precision=lax.Precision.HIGHEST) + b_ref[0, 0]


def _ada_modulation(c, w_ada, b_ada):
    depth, d, _ = w_ada.shape
    b = c.shape[0]
    out = pl.pallas_call(
        _ada_kernel,
        grid=(depth, 6),
        in_specs=[pl.BlockSpec((b, d), lambda l, k: (0, 0)),
                  pl.BlockSpec((1, d, d), lambda l, k: (l, 0, k)),
                  pl.BlockSpec((1, 1, 1, d), lambda l, k: (l, k, 0, 0))],
        out_specs=pl.BlockSpec((1, 1, b, d), lambda l, k: (l, k, 0, 0)),
        out_shape=jax.ShapeDtypeStruct((depth, 6, b, d), F32),
        compiler_params=_params("arbitrary", "arbitrary"),
        name="ada_modulation",
    )(c, w_ada, b_ada.reshape(depth, 6, 1, d))
    return out.transpose(0, 2, 1, 3)


IN_WIDTHS = (QKV_WIDTH, QKV_WIDTH, QKV_WIDTH, QKV_WIDTH, IDX_HEADS * IDX_DIM + LANES, LANES)


def _in_proj_kernel(x_ref, mod_ref, g_ref, w_ref, oa, ob, oc, od, oidx, ow, oc4, oc16, pc_ref):
    h = _modulated_norm(x_ref[...], g_ref[...], mod_ref[0, 0:1, :], mod_ref[0, 1:2, :]).astype(BF16)
    tm = x_ref.shape[0]
    off = 0
    for o_ref, width in zip((oa, ob, oc, od, oidx, ow), IN_WIDTHS):
        val = jnp.dot(h, w_ref[:, off:off + width], preferred_element_type=F32)
        o_ref[...] = val.astype(o_ref.dtype)
        if o_ref is oc:
            for s in range(QKV_WIDTH // LANES):
                pc_ref[s * tm:(s + 1) * tm, :] = val[:, s * LANES:(s + 1) * LANES]
        off += width
    for r, view in zip(DILATIONS[1:], (oc4, oc16)):
        for c in range(r):
            for s in range(QKV_WIDTH // LANES):
                col = c * QKV_WIDTH + s * LANES
                view[:, col:col + LANES] = pc_ref[pl.ds(s * tm + c, tm // r, stride=r), :].astype(view.dtype)


def _pack_w_in(w_in):
    d = w_in.shape[0]
    n_qkv = 4 * QKV_WIDTH
    n_qi = IDX_HEADS * IDX_DIM
    z = lambda n: jnp.zeros((d, n), w_in.dtype)
    return jnp.concatenate([w_in[:, :n_qkv + n_qi + IDX_DIM], z(LANES - IDX_DIM),
                            w_in[:, n_qkv + n_qi + IDX_DIM:], z(LANES - IDX_HEADS)], axis=1).astype(BF16)


def _in_projection(x2, mod, g_mix, w_packed, seq):
    t, d = x2.shape
    tm = min(ROW_TILE, seq)
    per_batch = seq // tm
    dtypes = (BF16, BF16, BF16, BF16, BF16, F32)
    views = DILATIONS[1:]
    return pl.pallas_call(
        _in_proj_kernel,
        grid=(t // tm,),
        in_specs=[pl.BlockSpec((tm, d), lambda i: (i, 0)),
                  pl.BlockSpec((1, 6, d), lambda i: (i // per_batch, 0, 0)),
                  pl.BlockSpec((1, d), lambda i: (0, 0)),
                  pl.BlockSpec(w_packed.shape, lambda i: (0, 0))],
        out_specs=([pl.BlockSpec((tm, w), lambda i: (i, 0)) for w in IN_WIDTHS]
                   + [pl.BlockSpec((tm // r, r * QKV_WIDTH), lambda i: (i, 0)) for r in views]),
        out_shape=([jax.ShapeDtypeStruct((t, w), dt) for w, dt in zip(IN_WIDTHS, dtypes)]
                   + [jax.ShapeDtypeStruct((t // r, r * QKV_WIDTH), BF16) for r in views]),
        scratch_shapes=[pltpu.VMEM((tm * (QKV_WIDTH // LANES), LANES), F32)],
        compiler_params=_params("arbitrary"),
        name="in_projection",
    )(x2, mod, g_mix.reshape(1, d), w_packed)


def _sb_kernel(q_ref, k_ref, v_ref, g_ref, o_ref):
    tq = q_ref.shape[1]
    i = pl.program_id(1)
    row = lax.broadcasted_iota(jnp.int32, (tq, 1), 0)
    lane = lax.broadcasted_iota(jnp.int32, (1, tq), 1)
    later = (lax.broadcasted_iota(jnp.int32, (tq, tq), 0) > lax.broadcasted_iota(jnp.int32, (tq, tq), 1))
    later = jnp.where(later, 1.0, 0.0).astype(BF16)

    qs = [q_ref[0, :, _head_cols(h)] * ATTN_SCALE for h in range(HEADS_PER_MIXER)]

    heads = range(HEADS_PER_MIXER)

    def blocks(starts, keeps, tails):
        z = jnp.concatenate([lax.dot_general(qs[h], k_ref[0, pl.ds(st, tq), _head_cols(h)], _NT,
                                             preferred_element_type=F32) for st in starts for h in heads], axis=0)
        softplus = jnp.maximum(z, 0.0) + jnp.log(1.0 + jnp.exp(-jnp.abs(z)))
        keep = None
        if any(kp is not None for kp in keeps):
            ones = jnp.ones((tq, tq), F32)
            keep = jnp.concatenate([ones if kp is None else kp for kp in keeps for _ in heads], axis=0)
        log_1m = -softplus if keep is None else -softplus * keep
        hi = log_1m.astype(BF16)
        lo = (log_1m - hi.astype(F32)).astype(BF16)
        inside = jnp.dot(hi, later, preferred_element_type=F32) + jnp.dot(lo, later, preferred_element_type=F32)
        block_sum = jnp.sum(log_1m, axis=1, keepdims=True)
        piece = lambda x, b, h: x[(b * HEADS_PER_MIXER + h) * tq:(b * HEADS_PER_MIXER + h + 1) * tq]
        tail_cols = []
        tails = list(tails)
        for b in range(len(starts)):
            tail_cols += tails
            tails = [tails[h] + piece(block_sum, b, h) for h in heads]
        a = jnp.exp(z - softplus + inside + jnp.concatenate(tail_cols, axis=0))
        if keep is not None:
            a = a * keep
        a = a.astype(BF16)
        av = [sum(jnp.dot(piece(a, b, h), v_ref[0, pl.ds(st, tq), _head_cols(h)], preferred_element_type=F32)
                  for b, st in enumerate(starts)) for h in heads]
        return tails, av

    starts = [pl.multiple_of(jnp.maximum(i - n, 0) * tq, tq) for n in range(SB_FIRST_BLOCKS)]
    keeps = [jnp.where(lane < row, 1.0, 0.0)]
    keeps += [jnp.full((tq, tq), jnp.where(i >= n, 1.0, 0.0), F32) for n in range(1, SB_FIRST_BLOCKS)]
    tails, accs = blocks(starts, keeps, [jnp.zeros((tq, 1), F32)] * HEADS_PER_MIXER)
    state = tuple(zip(tails, accs))

    def body(carry):
        j, state = carry
        tails, av = blocks([pl.multiple_of(j * tq, tq)], [None], [tail for tail, _ in state])
        return j - 1, tuple((tails[h], state[h][1] + av[h]) for h in heads)

    def cond(carry):
        j, state = carry
        worst = functools.reduce(jnp.maximum, [tail for tail, _ in state])
        return jnp.logical_and(j >= 0, jnp.max(worst) > -SB_UNDERFLOW)

    _, state = lax.while_loop(cond, body, (i - SB_FIRST_BLOCKS, state))
    for h in range(HEADS_PER_MIXER):
        o_ref[0, :, _head_cols(h)] = _head_norm(state[h][1], g_ref[h:h + 1, :]).astype(o_ref.dtype)


def _stick_breaking_mixer(p, g_heads):
    b, seq, _ = p.shape
    tq = min(Q_TILE, seq)
    return pl.pallas_call(
        _sb_kernel,
        grid=(b, seq // tq),
        in_specs=[pl.BlockSpec((1, tq, MIXER_WIDTH), lambda bi, i: (bi, i, 0)),
                  pl.BlockSpec((1, seq, MIXER_WIDTH), lambda bi, i: (bi, 0, 1)),
                  pl.BlockSpec((1, seq, MIXER_WIDTH), lambda bi, i: (bi, 0, 2)),
                  pl.BlockSpec((HEADS_PER_MIXER, HEAD_DIM), lambda bi, i: (0, 0))],
        out_specs=pl.BlockSpec((1, tq, MIXER_WIDTH), lambda bi, i: (bi, i, 0)),
        out_shape=jax.ShapeDtypeStruct((b, seq, MIXER_WIDTH), BF16),
        compiler_params=_params("arbitrary", "arbitrary"),
        name="stick_breaking",
    )(p, p, p, g_heads)


def _dsa_kernel(q_ref, k_ref, v_ref, qi_ref, ki_ref, w_ref, g_ref, o_ref, sc_ref, sc16_ref, s_ref, p_ref, *, slopes):
    tq = q_ref.shape[1]
    tk = sc_ref.shape[1]
    i = pl.program_id(1)
    n_blocks = ((i + 1) * tq + tk - 1) // tk
    qpos = i * tq + lax.broadcasted_iota(jnp.int32, (1, tq), 1)
    key_row = lax.broadcasted_iota(jnp.int32, (tk, 1), 0)
    neg_inf = float("-inf")

    w_t = w_ref[0].T * (IDX_HEADS ** -0.5 * IDX_DIM ** -0.5)
    w_rows = [w_t[h:h + 1, :] for h in range(IDX_HEADS)]
    qi = qi_ref[0]

    def top_half(x):
        bits = lax.bitcast_convert_type(x, jnp.int32) & jnp.int32(-65536)
        return lax.bitcast_convert_type(bits, F32).astype(BF16)

    def score_body(j, carry, slot):
        start = pl.multiple_of(j * tk, tk)
        ki = ki_ref[0, pl.ds(start, tk), 0:IDX_DIM]
        sc = jnp.zeros((tk, tq), F32)
        for h in range(IDX_HEADS):
            x = lax.dot_general(ki, qi[:, h * IDX_DIM:(h + 1) * IDX_DIM], _NT, preferred_element_type=F32)
            sc = sc + w_rows[h] * jnp.maximum(x, 0.0)
        sc = jnp.where((start + key_row) <= qpos, sc + 0.0, neg_inf)
        sc_ref[j] = sc
        sc16_ref[j] = top_half(sc)
        return carry

    _paired_loop(n_blocks, score_body, 0)

    def count(pred):
        def add_block(j, cnt, slot):
            c = jnp.where(pred(sc_ref[j]), 1.0, 0.0)
            return cnt + jnp.sum(c.reshape(tk // 8, 8, tq), axis=0)

        cnt = _paired_loop(n_blocks, add_block, jnp.zeros((8, tq), F32))
        return jnp.sum(cnt, axis=0, keepdims=True)

    int_min = jnp.int32(-2 ** 31)

    def ordered_to_float(u):
        key = u ^ int_min
        bits = jnp.where(key >= 0, key, key ^ jnp.int32(0x7FFFFFFF))
        return lax.bitcast_convert_type(bits, F32)

    def count_top_half(cand16):
        def add_block(j, cnt, slot):
            c = jnp.where(sc16_ref[j] >= cand16, jnp.ones((), BF16), jnp.zeros((), BF16))
            for g in range(tk // 16):
                cnt = cnt + c[g * 16:(g + 1) * 16]
            return cnt

        cnt = _paired_loop(n_blocks, add_block, jnp.zeros((16, tq), BF16))
        return jnp.sum(cnt.astype(F32), axis=0, keepdims=True)

    def search_body(step, carry, half):
        prefix, n_at_prefix = carry
        cand = prefix | jnp.left_shift(jnp.int32(1), 31 - step)
        cand_f = ordered_to_float(cand)
        n_ge = count_top_half(top_half(cand_f)) if half else count(lambda s: s >= cand_f)
        keep = n_ge >= DSA_TOPK
        return jnp.where(keep, cand, prefix), jnp.where(keep, n_ge, n_at_prefix)

    keep_all = (qpos + 1) <= DSA_TOPK

    carry = (jnp.zeros((1, tq), jnp.int32), jnp.zeros((1, tq), F32))
    carry = lax.fori_loop(0, 16, functools.partial(search_body, half=True), carry)
    prefix, n_ge_tau = lax.fori_loop(16, 32, functools.partial(search_body, half=False), carry)
    tau = jnp.where(keep_all, float(np.finfo(np.float32).min), ordered_to_float(prefix))
    no_ties = jnp.min(jnp.where(jnp.logical_or(keep_all, n_ge_tau == DSA_TOPK), 1.0, 0.0)) > 0.5

    earlier = (lax.broadcasted_iota(jnp.int32, (tk, tk), 1) < lax.broadcasted_iota(jnp.int32, (tk, tk), 0))
    earlier = jnp.where(earlier, 1.0, 0.0).astype(BF16)
    qs = [q_ref[0, :, _head_cols(h)] * ATTN_SCALE for h in range(HEADS_PER_MIXER)]
    key_bias = [slopes[h] * lax.broadcasted_iota(jnp.int32, (tk, tq), 0).astype(F32)
                for h in range(HEADS_PER_MIXER)]

    def attn_body(j, carry, slot, n_ties=None):
        ties_seen, stats = carry
        s_slot = [s_ref.at[slot * HEADS_PER_MIXER + h] for h in range(HEADS_PER_MIXER)]
        p_slot = [p_ref.at[slot * HEADS_PER_MIXER + h] for h in range(HEADS_PER_MIXER)]
        start = pl.multiple_of(j * tk, tk)
        sc = sc_ref[j]
        if n_ties is None:
            sel = sc >= tau
        else:
            tie = jnp.where(sc == tau, 1.0, 0.0)
            rank = jnp.dot(earlier, tie.astype(BF16), preferred_element_type=F32) + ties_seen
            sel = jnp.where(sc > tau, 1.0, jnp.where(rank < n_ties, tie, 0.0)) > 0.5
            ties_seen = ties_seen + jnp.sum(tie, axis=0, keepdims=True)
        block_pos = (j * tk - i * tq).astype(F32)
        for h in range(HEADS_PER_MIXER):
            k = k_ref[0, pl.ds(start, tk), _head_cols(h)]
            s = lax.dot_general(k, qs[h], _NT, preferred_element_type=F32) + key_bias[h]
            s_slot[h][...] = jnp.where(sel, s, MASKED)
        locals_ = []
        for h in range(HEADS_PER_MIXER):
            s = s_slot[h][...]
            m_loc = jnp.max(s, axis=0, keepdims=True)
            p = jnp.exp(s - m_loc)
            locals_.append((m_loc, jnp.sum(p, axis=0, keepdims=True)))
            p_slot[h][...] = p.astype(BF16)
        new_stats = []
        for h in range(HEADS_PER_MIXER):
            m, l, acc = stats[h]
            m_loc, l_blk = locals_[h]
            v = v_ref[0, pl.ds(start, tk), _head_cols(h)]
            pv_blk = lax.dot_general(v, p_slot[h][...], _TN, preferred_element_type=F32)
            m_blk = m_loc + slopes[h] * block_pos
            m_new = jnp.maximum(m, m_blk)
            w_old = jnp.exp(m - m_new)
            w_blk = jnp.where(m_loc > 0.5 * MASKED, jnp.exp(m_blk - m_new), 0.0)
            new_stats.append((m_new, w_old * l + w_blk * l_blk, w_old * acc + w_blk * pv_blk))
        return ties_seen, tuple(new_stats)

    init = tuple((jnp.full((1, tq), MASKED, F32), jnp.zeros((1, tq), F32), jnp.zeros((HEAD_DIM, tq), F32))
                 for _ in range(HEADS_PER_MIXER))
    start_carry = (jnp.zeros((1, tq), F32), init)

    def attend_without_ties():
        return _paired_loop(n_blocks, attn_body, start_carry)[1]

    def attend_with_ties():
        n_gt = count(lambda s: s > tau)
        n_ties = jnp.where(keep_all, 1e9, DSA_TOPK - n_gt)
        return _paired_loop(n_blocks, functools.partial(attn_body, n_ties=n_ties), start_carry)[1]

    stats = lax.cond(no_ties, attend_without_ties, attend_with_ties)
    outs = []
    for h in range(HEADS_PER_MIXER):
        _, l, acc = stats[h]
        o = acc / l
        outs.append(o * lax.rsqrt(jnp.mean(o * o, axis=0, keepdims=True) + NORM_EPS) * g_ref[:, h:h + 1])
    o_ref[0] = jnp.concatenate(outs, axis=0).T.astype(o_ref.dtype)


def _dsa_mixer(p, p_idx, p_w, g_heads):
    b, seq, _ = p.shape
    tq = min(4 * Q_TILE, seq)
    tk = tq
    assert seq // 16 <= 256, "the packed-bf16 count accumulators are exact only up to 256 adds"
    qi_width = IDX_HEADS * IDX_DIM
    return pl.pallas_call(
        functools.partial(_dsa_kernel, slopes=_alibi_slopes(0)),
        grid=(b, seq // tq),
        in_specs=[pl.BlockSpec((1, tq, MIXER_WIDTH), lambda bi, i: (bi, i, 0)),
                  pl.BlockSpec((1, seq, MIXER_WIDTH), lambda bi, i: (bi, 0, 1)),
                  pl.BlockSpec((1, seq, MIXER_WIDTH), lambda bi, i: (bi, 0, 2)),
                  pl.BlockSpec((1, tq, qi_width), lambda bi, i: (bi, i, 0)),
                  pl.BlockSpec((1, seq, LANES), lambda bi, i: (bi, 0, qi_width // LANES)),
                  pl.BlockSpec((1, tq, LANES), lambda bi, i: (bi, i, 0)),
                  pl.BlockSpec((HEAD_DIM, HEADS_PER_MIXER), lambda bi, i: (0, 0))],
        out_specs=pl.BlockSpec((1, tq, MIXER_WIDTH), lambda bi, i: (bi, i, 0)),
        out_shape=jax.ShapeDtypeStruct((b, seq, MIXER_WIDTH), BF16),
        scratch_shapes=[pltpu.VMEM((seq // tk, tk, tq), F32),
                        pltpu.VMEM((seq // tk, tk, tq), BF16),
                        pltpu.VMEM((2 * HEADS_PER_MIXER, tk, tq), F32),
                        pltpu.VMEM((2 * HEADS_PER_MIXER, tk, tq), BF16)],
        compiler_params=_params("arbitrary", "arbitrary"),
        name="dsa",
    )(p, p, p, p_idx, p_idx, p_w, g_heads.T)


def _band_kernel(q_ref, kp_ref, kc_ref, vp_ref, vc_ref, o_ref, lse_ref, *, dilation, slopes):
    tq = q_ref.shape[1]
    ui = pl.program_id(2)
    heads = range(HEADS_PER_MIXER)
    u_q = jnp.concatenate([ui * tq + lax.broadcasted_iota(jnp.int32, (tq, 1), 0)] * HEADS_PER_MIXER, axis=0)
    u_k = ui * tq - DILATED_STEPS + lax.broadcasted_iota(jnp.int32, (1, DILATED_STEPS + tq), 1)
    steps = u_q - u_k
    valid = jnp.logical_and(jnp.logical_and(steps >= 0, steps <= DILATED_STEPS), u_k >= 0)
    slope_col = jnp.concatenate([jnp.full((tq, 1), slopes[h], F32) for h in heads], axis=0)
    s = jnp.concatenate(
        [lax.dot_general(q_ref[0, :, _head_cols(h)],
                         jnp.concatenate([kp_ref[0, :, _head_cols(h)], kc_ref[0, :, _head_cols(h)]], axis=0),
                         _NT, preferred_element_type=F32) for h in heads], axis=0)
    s = jnp.where(valid, s * ATTN_SCALE - slope_col * (steps * dilation).astype(F32), MASKED)
    m = jnp.max(s, axis=1, keepdims=True)
    e = jnp.exp(s - m)
    den = jnp.sum(e, axis=1, keepdims=True)
    p = e.astype(BF16)
    lse = m + jnp.log(den)
    lane = lax.broadcasted_iota(jnp.int32, (1, LANES), 1)
    lse_all = jnp.zeros((tq, LANES), F32)
    for h in heads:
        rows = slice(h * tq, (h + 1) * tq)
        v = jnp.concatenate([vp_ref[0, :, _head_cols(h)], vc_ref[0, :, _head_cols(h)]], axis=0)
        o_ref[0, :, _head_cols(h)] = jnp.dot(p[rows], v, preferred_element_type=F32) / den[rows]
        lse_all = lse_all + jnp.where(lane == h, lse[rows], 0.0)
    lse_ref[0] = lse_all


def _dilated_branch(view, dilation, slopes):
    b, length, _ = view.shape
    classes = dilation
    tq = min(2 * Q_TILE, length)
    back = tq // DILATED_STEPS
    spec = lambda part, prev: (
        pl.BlockSpec((1, DILATED_STEPS, MIXER_WIDTH),
                     lambda bi, c, ui: (bi, jnp.maximum(ui * back - 1, 0), c * 3 + part)) if prev
        else pl.BlockSpec((1, tq, MIXER_WIDTH), lambda bi, c, ui: (bi, ui, c * 3 + part)))
    out, lse = pl.pallas_call(
        functools.partial(_band_kernel, dilation=dilation, slopes=slopes),
        grid=(b, classes, length // tq),
        in_specs=[spec(0, False), spec(1, True), spec(1, False), spec(2, True), spec(2, False)],
        out_specs=[pl.BlockSpec((1, tq, MIXER_WIDTH), lambda bi, c, ui: (bi, ui, c)),
                   pl.BlockSpec((1, tq, LANES), lambda bi, c, ui: (bi, ui, c))],
        out_shape=[jax.ShapeDtypeStruct((b, length, classes * MIXER_WIDTH), F32),
                   jax.ShapeDtypeStruct((b, length, classes * LANES), F32)],
        compiler_params=_params("arbitrary", "arbitrary", "arbitrary"),
        name=f"dilated_r{dilation}",
    )(view, view, view, view, view)
    return out, lse


def _merge_dilated(o_refs, l_refs, g_ref, scratch, rows):
    chunks = MIXER_WIDTH // LANES
    outs = [[o_refs[0][:, s * LANES:(s + 1) * LANES] for s in range(chunks)]]
    lses = [l_refs[0][...]]
    for r, o_view, l_view, o_nat, l_nat in zip(DILATIONS[1:], o_refs[1:], l_refs[1:], scratch[0::2], scratch[1::2]):
        for c in range(r):
            for s in range(chunks):
                col = c * MIXER_WIDTH + s * LANES
                o_nat[pl.ds(s * rows + c, rows // r, stride=r), :] = o_view[:, col:col + LANES]
            l_nat[pl.ds(c, rows // r, stride=r), :] = l_view[:, c * LANES:(c + 1) * LANES]
        outs.append([o_nat[s * rows:(s + 1) * rows, :] for s in range(chunks)])
        lses.append(l_nat[...])
    top = functools.reduce(jnp.maximum, lses)
    wts = [jnp.exp(l - top) for l in lses]
    def head_of(index):
        return sum(jnp.where(index >= h * HEAD_DIM, 1, 0) for h in range(1, HEADS_PER_MIXER))

    head_of_col = head_of(lax.broadcasted_iota(jnp.int32, (LANES, MIXER_WIDTH), 1))
    spread = jnp.where(lax.broadcasted_iota(jnp.int32, (LANES, MIXER_WIDTH), 0) == head_of_col, 1.0, 0.0)
    same_head = (head_of(lax.broadcasted_iota(jnp.int32, (MIXER_WIDTH, MIXER_WIDTH), 0))
                 == head_of(lax.broadcasted_iota(jnp.int32, (MIXER_WIDTH, MIXER_WIDTH), 1)))
    head_mean = jnp.where(same_head, 1.0 / HEAD_DIM, 0.0)

    def times(x, m):
        hi = x.astype(BF16)
        lo = (x - hi.astype(F32)).astype(BF16)
        m = m.astype(BF16)
        return jnp.dot(hi, m, preferred_element_type=F32) + jnp.dot(lo, m, preferred_element_type=F32)

    w_cols = [times(w, spread) for w in wts]
    full = [jnp.concatenate(o, axis=1) for o in outs]
    mixed = sum(w * o for w, o in zip(w_cols, full)) / sum(w_cols)
    return mixed * lax.rsqrt(times(mixed * mixed, head_mean) + NORM_EPS) * g_ref[...]


def _dilated_branches(views):
    slopes = _alibi_slopes(1)
    branches = [_dilated_branch(v, r, slopes) for v, r in zip(views, DILATIONS)]
    flat = lambda a: a.reshape(-1, a.shape[-1])
    return [flat(o) for o, _ in branches], [flat(l) for _, l in branches]


def _moba_kernel(q_ref, k_ref, v_ref, g_ref, o_ref, kmean_ref, chosen_ref, s_ref, p_ref, *, slopes):
    tq = q_ref.shape[1]
    n_kv = k_ref.shape[1] // MOBA_BLOCK
    own = pl.program_id(1)

    @pl.when(own == 0)
    def _():
        kmean_ref[...] = jnp.zeros_like(kmean_ref)
        for n in range(n_kv):
            blk = k_ref[0, n * MOBA_BLOCK:(n + 1) * MOBA_BLOCK, :].astype(F32)
            kmean_ref[n:n + 1, :] = jnp.sum(blk, axis=0, keepdims=True) * (1.0 / MOBA_BLOCK)

    key_row = lax.broadcasted_iota(jnp.int32, (MOBA_BLOCK, 1), 0)
    q_lane = lax.broadcasted_iota(jnp.int32, (1, tq), 1)
    blk_rows = kmean_ref.shape[0]
    blk_row = lax.broadcasted_iota(jnp.int32, (blk_rows, 1), 0)
    blk_f = blk_row.astype(F32)
    neg_inf = float("-inf")
    group = tq // MOBA_BLOCK
    first_own = own * group
    sub_block = sum(jnp.where(q_lane >= g * MOBA_BLOCK, 1, 0) for g in range(1, group)) if group > 1 else 0
    own_blk = first_own + sub_block
    q_local = q_lane - sub_block * MOBA_BLOCK

    qs = []
    for h in range(HEADS_PER_MIXER):
        q = q_ref[0, :, _head_cols(h)]
        gate = lax.dot_general(kmean_ref[:, _head_cols(h)], q.astype(F32), _NT, preferred_element_type=F32,
                               precision=lax.Precision.HIGHEST)
        gate = jnp.where(blk_row < own_blk, gate, neg_inf)
        picks = jnp.zeros((blk_rows, tq), F32)
        for _ in range(MOBA_TOPK):
            top = jnp.max(gate, axis=0, keepdims=True)
            is_top = jnp.logical_and(gate == top, top > neg_inf)
            first = jnp.min(jnp.where(is_top, blk_f, float(blk_rows)), axis=0, keepdims=True)
            pick = blk_f == first
            picks = jnp.where(pick, 1.0, picks)
            gate = jnp.where(pick, neg_inf, gate)
        chosen_ref[h] = picks
        qs.append(q * ATTN_SCALE)

    key_bias = [slopes[h] * lax.broadcasted_iota(jnp.int32, (MOBA_BLOCK, tq), 0).astype(F32)
                for h in range(HEADS_PER_MIXER)]

    def block_softmax(start, keep=None, slot=0):
        s_slot = [s_ref.at[slot * HEADS_PER_MIXER + h] for h in range(HEADS_PER_MIXER)]
        p_slot = [p_ref.at[slot * HEADS_PER_MIXER + h] for h in range(HEADS_PER_MIXER)]
        for h in range(HEADS_PER_MIXER):
            k = k_ref[0, pl.ds(start, MOBA_BLOCK), _head_cols(h)]
            s = lax.dot_general(k, qs[h], _NT, preferred_element_type=F32) + key_bias[h]
            s_slot[h][...] = s if keep is None else jnp.where(keep, s, MASKED)
        pieces = []
        for h in range(HEADS_PER_MIXER):
            s = s_slot[h][...]
            m_loc = jnp.max(s, axis=0, keepdims=True)
            p = jnp.exp(s - m_loc)
            pieces.append((m_loc, jnp.sum(p, axis=0, keepdims=True)))
            p_slot[h][...] = p.astype(BF16)
        out = []
        for h in range(HEADS_PER_MIXER):
            v = v_ref[0, pl.ds(start, MOBA_BLOCK), _head_cols(h)]
            out.append(pieces[h] + (lax.dot_general(v, p_slot[h][...], _TN, preferred_element_type=F32),))
        return out

    def merge(stats, j, pieces, use):
        block_pos = ((j - first_own) * MOBA_BLOCK).astype(F32)
        merged = []
        for h, (m_loc, l_blk, pv_blk) in enumerate(pieces):
            m_blk = jnp.where(use[h], m_loc + slopes[h] * block_pos, MASKED)
            m, l, acc = stats[h]
            m_new = jnp.maximum(m, m_blk)
            w_old = jnp.exp(m - m_new)
            w_blk = jnp.where(use[h], jnp.exp(m_blk - m_new), 0.0)
            merged.append((m_new, w_old * l + w_blk * l_blk, w_old * acc + w_blk * pv_blk))
        return tuple(merged)

    def picked(h, j):
        return chosen_ref[h, pl.ds(j, 1), :] > 0.5

    def body(j, stats, slot):
        start = pl.multiple_of(j * MOBA_BLOCK, MOBA_BLOCK)
        return merge(stats, j, block_softmax(start, slot=slot), [picked(h, j) for h in range(HEADS_PER_MIXER)])

    stats = tuple((jnp.full((1, tq), MASKED, F32), jnp.zeros((1, tq), F32), jnp.zeros((HEAD_DIM, tq), F32))
                  for _ in range(HEADS_PER_MIXER))
    stats = _paired_loop(first_own, body, stats)
    for g in range(group):
        j = first_own + g
        keep = jnp.logical_or(own_blk > j, jnp.logical_and(own_blk == j, key_row <= q_local))
        pieces = block_softmax(pl.multiple_of(j * MOBA_BLOCK, MOBA_BLOCK), keep=keep, slot=g % 2)
        use = [jnp.logical_or(own_blk == j, jnp.logical_and(own_blk > j, picked(h, j)))
               for h in range(HEADS_PER_MIXER)]
        stats = merge(stats, j, pieces, use)
    outs = []
    for h in range(HEADS_PER_MIXER):
        _, l, acc = stats[h]
        o = acc / l
        outs.append(o * lax.rsqrt(jnp.mean(o * o, axis=0, keepdims=True) + NORM_EPS) * g_ref[:, h:h + 1])
    o_ref[0] = jnp.concatenate(outs, axis=0).T.astype(o_ref.dtype)


def _moba_mixer(p, g_heads):
    b, seq, _ = p.shape
    tq = MOBA_GROUP * MOBA_BLOCK if seq % (MOBA_GROUP * MOBA_BLOCK) == 0 else MOBA_BLOCK
    blk_rows =-(-(seq // MOBA_BLOCK) // SUBLANES) * SUBLANES
    return pl.pallas_call(
        functools.partial(_moba_kernel, slopes=_alibi_slopes(2)),
        grid=(b, seq // tq),
        in_specs=[pl.BlockSpec((1, tq, MIXER_WIDTH), lambda bi, i: (bi, i, 0)),
                  pl.BlockSpec((1, seq, MIXER_WIDTH), lambda bi, i: (bi, 0, 1)),
                  pl.BlockSpec((1, seq, MIXER_WIDTH), lambda bi, i: (bi, 0, 2)),
                  pl.BlockSpec((HEAD_DIM, HEADS_PER_MIXER), lambda bi, i: (0, 0))],
        out_specs=pl.BlockSpec((1, tq, MIXER_WIDTH), lambda bi, i: (bi, i, 0)),
        out_shape=jax.ShapeDtypeStruct((b, seq, MIXER_WIDTH), BF16),
        scratch_shapes=[pltpu.VMEM((blk_rows, MIXER_WIDTH), F32),
                        pltpu.VMEM((HEADS_PER_MIXER, blk_rows, tq), F32),
                        pltpu.VMEM((2 * HEADS_PER_MIXER, MOBA_BLOCK, tq), F32),
                        pltpu.VMEM((2 * HEADS_PER_MIXER, MOBA_BLOCK, tq), BF16)],
        compiler_params=_params("arbitrary", "arbitrary"),
        name="moba",
    )(p, p, p, g_heads.T)


def _out_proj_kernel(oa, ob, od, o1, o4, o16, l1, l4, l16, gc_ref, w_ref, x_ref, mod_ref, o_ref, *scratch):
    rows = x_ref.shape[0]
    oc = _merge_dilated((o1, o4, o16), (l1, l4, l16), gc_ref, scratch, rows).astype(BF16)
    acc = jnp.zeros(x_ref.shape, F32)
    for m, o in enumerate((oa[...], ob[...], oc, od[...])):
        acc = acc + jnp.dot(o, w_ref[m * MIXER_WIDTH:(m + 1) * MIXER_WIDTH, :], preferred_element_type=F32)
    o_ref[...] = x_ref[...] + mod_ref[0, 2:3, :] * acc


def _out_projection(o_a, o_b, o_d, dilated, g_heads_c, w_out, x2, mod, seq):
    t, d = x2.shape
    tm = min(ROW_TILE, seq)
    per_batch = seq // tm
    o_spec = pl.BlockSpec((tm, MIXER_WIDTH), lambda i: (i, 0))
    branch_outs, branch_lses = dilated
    view_specs = ([pl.BlockSpec((tm // r, r * MIXER_WIDTH), lambda i: (i, 0)) for r in DILATIONS]
                  + [pl.BlockSpec((tm // r, r * LANES), lambda i: (i, 0)) for r in DILATIONS])
    scratch = []
    for _ in DILATIONS[1:]:
        scratch += [pltpu.VMEM((tm * (MIXER_WIDTH // LANES), LANES), F32), pltpu.VMEM((tm, LANES), F32)]
    return pl.pallas_call(
        _out_proj_kernel,
        grid=(t // tm,),
        in_specs=[o_spec] * 3 + view_specs + [pl.BlockSpec((1, MIXER_WIDTH), lambda i: (0, 0)),
                                              pl.BlockSpec(w_out.shape, lambda i: (0, 0)),
                                              pl.BlockSpec((tm, d), lambda i: (i, 0)),
                                              pl.BlockSpec((1, 6, d), lambda i: (i // per_batch, 0, 0))],
        out_specs=pl.BlockSpec((tm, d), lambda i: (i, 0)),
        out_shape=jax.ShapeDtypeStruct((t, d), F32),
        scratch_shapes=scratch,
        compiler_params=_params("arbitrary"),
        name="out_projection",
    )(*[o.reshape(t, MIXER_WIDTH) for o in (o_a, o_b, o_d)], *branch_outs, *branch_lses,
      g_heads_c.reshape(1, MIXER_WIDTH), w_out, x2, mod)


def _rms_gain(x, gain):
    return x * lax.rsqrt(jnp.mean(x * x, axis=-1, keepdims=True) + NORM_EPS) * gain


def _ffn_kernel(x_ref, mod_ref, g_ref, fg_ref, wg_ref, wu_ref, wd_ref, o_ref, h_ref, acc_ref, *, final_norm):
    f = pl.program_id(1)

    @pl.when(f == 0)
    def _():
        h_ref[...] = _modulated_norm(x_ref[...], g_ref[...], mod_ref[0, 3:4, :], mod_ref[0, 4:5, :]).astype(BF16)
        acc_ref[...] = jnp.zeros_like(acc_ref)

    h = h_ref[...]
    gate = jnp.dot(h, wg_ref[...], preferred_element_type=F32)
    up = jnp.dot(h, wu_ref[...], preferred_element_type=F32)
    act = (gate / (1.0 + jnp.exp(-gate)) * up).astype(BF16)
    acc_ref[...] += jnp.dot(act, wd_ref[...], preferred_element_type=F32)

    @pl.when(f == pl.num_programs(1) - 1)
    def _():
        out = x_ref[...] + mod_ref[0, 5:6, :] * acc_ref[...]
        o_ref[...] = _rms_gain(out, fg_ref[...]) if final_norm else out


def _dense_ffn(x2, mod, g_ffn, wg, wu, wd, seq, g_final, final_norm):
    t, d = x2.shape
    d_ff = wg.shape[1]
    tm = min(ROW_TILE, seq)
    tf = FFN_TILE if d_ff % FFN_TILE == 0 else d_ff
    per_batch = seq // tm
    return pl.pallas_call(
        functools.partial(_ffn_kernel, final_norm=final_norm),
        grid=(t // tm, d_ff // tf),
        in_specs=[pl.BlockSpec((tm, d), lambda i, f: (i, 0)),
                  pl.BlockSpec((1, 6, d), lambda i, f: (i // per_batch, 0, 0)),
                  pl.BlockSpec((1, d), lambda i, f: (0, 0)),
                  pl.BlockSpec((1, d), lambda i, f: (0, 0)),
                  pl.BlockSpec((d, tf), lambda i, f: (0, f)),
                  pl.BlockSpec((d, tf), lambda i, f: (0, f)),
                  pl.BlockSpec((tf, d), lambda i, f: (f, 0))],
        out_specs=pl.BlockSpec((tm, d), lambda i, f: (i, 0)),
        out_shape=jax.ShapeDtypeStruct((t, d), F32),
        scratch_shapes=[pltpu.VMEM((tm, d), BF16), pltpu.VMEM((tm, d), F32)],
        compiler_params=_params("arbitrary", "arbitrary"),
        name="dense_ffn",
    )(x2, mod, g_ffn.reshape(1, d), g_final.reshape(1, d), wg.astype(BF16), wu.astype(BF16), wd.astype(BF16))


def _store_row_tiles(dst_ref, value):
    rows, d = value.shape
    chunks = d // LANES
    for s in range(chunks):
        dst_ref[pl.ds(s, rows, stride=chunks), :] = value[:, s * LANES:(s + 1) * LANES]


def _load_row_tile_chunk(src_ref, s, rows, chunks):
    return src_ref[pl.ds(s, rows, stride=chunks), :]


def _router_kernel(x_ref, mod_ref, g_ref, wr_ref, h_ref, logit_ref):
    h = _modulated_norm(x_ref[...], g_ref[...], mod_ref[0, 3:4, :], mod_ref[0, 4:5, :])
    _store_row_tiles(h_ref, h)
    logit_ref[...] = jnp.dot(h, wr_ref[...], preferred_element_type=F32, precision=lax.Precision.HIGHEST)


def _router(x2, mod, g_ffn, w_router, seq):
    t, d = x2.shape
    tm = min(ROW_TILE, seq)
    per_batch = seq // tm
    wr = jnp.zeros((d, LANES), F32).at[:, :N_EXPERTS].set(w_router.astype(F32))
    return pl.pallas_call(
        _router_kernel,
        grid=(t // tm,),
        in_specs=[pl.BlockSpec((tm, d), lambda i: (i, 0)),
                  pl.BlockSpec((1, 6, d), lambda i: (i // per_batch, 0, 0)),
                  pl.BlockSpec((1, d), lambda i: (0, 0)),
                  pl.BlockSpec((d, LANES), lambda i: (0, 0))],
        out_specs=[pl.BlockSpec((tm * (d // LANES), LANES), lambda i: (i, 0)),
                   pl.BlockSpec((tm, LANES), lambda i: (i, 0))],
        out_shape=[jax.ShapeDtypeStruct((t * (d // LANES), LANES), F32), jax.ShapeDtypeStruct((t, LANES), F32)],
        compiler_params=_params("arbitrary"),
        name="moe_router",
    )(x2, mod, g_ffn.reshape(1, d), wr)


def _row_copy(src_hbm, row, dst_ref, r, sem, chunks):
    src = src_hbm.at[pl.ds(pl.multiple_of(row * chunks, chunks), chunks), :]
    return pltpu.make_async_copy(src, dst_ref.at[pl.ds(pl.multiple_of(r * chunks, chunks), chunks), :], sem)


def _expert_kernel(be_ref, used_ref, cur_idx_ref, nxt_idx_ref, h_hbm, wg_ref, wu_ref, wd_ref, o_ref,
                   xin_ref, xb_ref, acc_ref, sem):
    m = pl.program_id(0)
    f = pl.program_id(1)
    n_used = used_ref[0]
    live = m < n_used
    rows, d = xb_ref.shape
    chunks = d // LANES

    def row_copies(idx_ref, slot, start):
        if not start:
            half = rows * chunks // 2
            for lane in range(2):
                pltpu.make_async_copy(h_hbm.at[pl.ds(0, half), :], xin_ref.at[slot, pl.ds(0, half), :],
                                      sem.at[slot, lane]).wait()
            return

        def body(group, c):
            for u in range(ROWS_PER_TRIP):
                r = ROWS_PER_TRIP * group + u
                _row_copy(h_hbm, idx_ref[0, 0, r], xin_ref.at[slot], r, sem.at[slot, u % 2],
                          chunks).start(priority=u % 2)
            return c
        lax.fori_loop(0, rows // ROWS_PER_TRIP, body, 0)

    @pl.when(jnp.logical_and(live, f == 0))
    def _():
        @pl.when(m == 0)
        def _():
            row_copies(cur_idx_ref, 0, start=True)

        for slot in range(2):
            @pl.when(m % 2 == slot)
            def _(slot=slot):
                row_copies(cur_idx_ref, slot, start=False)
                for s in range(chunks):
                    xb_ref[:, s * LANES:(s + 1) * LANES] = _load_row_tile_chunk(
                        xin_ref.at[slot], s, rows, chunks).astype(BF16)

                @pl.when(m + 1 < n_used)
                def _():
                    row_copies(nxt_idx_ref, 1 - slot, start=True)

        acc_ref[...] = jnp.zeros_like(acc_ref)

    @pl.when(live)
    def _():
        h = xb_ref[...]
        gate = jnp.dot(h, wg_ref[0], preferred_element_type=F32)
        up = jnp.dot(h, wu_ref[0], preferred_element_type=F32)
        act = (gate / (1.0 + jnp.exp(-gate)) * up).astype(BF16)
        acc_ref[...] += jnp.dot(act, wd_ref[0], preferred_element_type=F32)

    @pl.when(f == pl.num_programs(1) - 1)
    def _():
        _store_row_tiles(o_ref, jnp.where(live, acc_ref[...], 0.0))


def _expert_ffn(h_rows, slot_tok, block_expert, n_used, wg, wu, wd):
    d = wg.shape[1]
    chunks = d // LANES
    n_slots = slot_tok.shape[0]
    d_ff = wg.shape[2]
    tm = EXPERT_ROWS
    tf = EXPERT_FF_TILE if d_ff % EXPERT_FF_TILE == 0 else d_ff
    nf = d_ff // tf
    nm = n_slots // tm

    def fcol(m, f, used):
        return jnp.where(m < used[0], f, nf - 1)

    idx_spec = lambda shift: pl.BlockSpec((1, 1, tm), lambda m, f, be, used: (jnp.minimum(m + shift, nm - 1), 0, 0),
                                          memory_space=pltpu.SMEM)
    grid_spec = pltpu.PrefetchScalarGridSpec(
        num_scalar_prefetch=2,
        grid=(nm, nf),
        in_specs=[idx_spec(0), idx_spec(1), pl.BlockSpec(memory_space=pl.ANY),
                  pl.BlockSpec((1, d, tf), lambda m, f, be, used: (be[m], 0, fcol(m, f, used))),
                  pl.BlockSpec((1, d, tf), lambda m, f, be, used: (be[m], 0, fcol(m, f, used))),
                  pl.BlockSpec((1, tf, d), lambda m, f, be, used: (be[m], fcol(m, f, used), 0))],
        out_specs=pl.BlockSpec((tm * chunks, LANES), lambda m, f, be, used: (m, 0)),
        scratch_shapes=[pltpu.VMEM((2, tm * chunks, LANES), F32), pltpu.VMEM((tm, d), BF16),
                        pltpu.VMEM((tm, d), F32), pltpu.SemaphoreType.DMA((2, 2))])
    idx = slot_tok.reshape(nm, 1, tm)
    return pl.pallas_call(
        _expert_kernel,
        grid_spec=grid_spec,
        out_shape=jax.ShapeDtypeStruct((n_slots * chunks, LANES), F32),
        compiler_params=_params("arbitrary", "arbitrary"),
        name="expert_ffn",
    )(block_expert, n_used, idx, idx, h_rows, wg.astype(BF16), wu.astype(BF16), wd.astype(BF16))


def _combine_kernel(d0_ref, d1_ref, ys_hbm, x_ref, gates_ref, mod_ref, fg_ref, o_ref, y0_ref, y1_ref, sem, *,
                    final_norm):
    rows, d = o_ref.shape
    chunks = d // LANES

    def start(group, c):
        for u in range(ROWS_PER_TRIP // 2):
            r = ROWS_PER_TRIP // 2 * group + u
            _row_copy(ys_hbm, d0_ref[0, 0, r], y0_ref, r, sem.at[0], chunks).start(priority=0)
            _row_copy(ys_hbm, d1_ref[0, 0, r], y1_ref, r, sem.at[1], chunks).start(priority=1)
        return c

    lax.fori_loop(0, rows // (ROWS_PER_TRIP // 2), start, 0)
    for k, y_ref in enumerate((y0_ref, y1_ref)):
        pltpu.make_async_copy(ys_hbm.at[pl.ds(0, rows * chunks), :], y_ref, sem.at[k]).wait()
    gates = gates_ref[...]
    for s in range(chunks):
        cols = slice(s * LANES, (s + 1) * LANES)
        y = (_load_row_tile_chunk(y0_ref, s, rows, chunks) * gates[:, 0:1]
             + _load_row_tile_chunk(y1_ref, s, rows, chunks) * gates[:, 1:2])
        o_ref[:, cols] = x_ref[:, cols] + mod_ref[0, 5:6, cols] * y
    if final_norm:
        o_ref[...] = _rms_gain(o_ref[...], fg_ref[...])


def _moe_combine(ys, dest0, dest1, gates, x2, mod, seq, g_final, final_norm):
    t, d = x2.shape
    tm = min(GATHER_ROWS, seq)
    steps = t // tm
    per_batch = seq // tm
    idx_spec = pl.BlockSpec((1, 1, tm), lambda i: (i, 0, 0), memory_space=pltpu.SMEM)
    return pl.pallas_call(
        functools.partial(_combine_kernel, final_norm=final_norm),
        grid=(steps,),
        in_specs=[idx_spec, idx_spec, pl.BlockSpec(memory_space=pl.ANY),
                  pl.BlockSpec((tm, d), lambda i: (i, 0)),
                  pl.BlockSpec((tm, TOP_K_EXPERTS), lambda i: (i, 0)),
                  pl.BlockSpec((1, 6, d), lambda i: (i // per_batch, 0, 0)),
                  pl.BlockSpec((1, d), lambda i: (0, 0))],
        out_specs=pl.BlockSpec((tm, d), lambda i: (i, 0)),
        out_shape=jax.ShapeDtypeStruct((t, d), F32),
        scratch_shapes=[pltpu.VMEM((tm * (d // LANES), LANES), F32), pltpu.VMEM((tm * (d // LANES), LANES), F32),
                        pltpu.SemaphoreType.DMA((2,))],
        compiler_params=_params("arbitrary"),
        name="moe_combine",
    )(dest0.reshape(steps, 1, tm), dest1.reshape(steps, 1, tm), ys, x2, gates, mod, g_final.reshape(1, d))


def _moe_ffn(x2, mod, g_ffn, w_router, wg, wu, wd, seq, g_final, final_norm):
    t, d = x2.shape
    h, logits = _router(x2, mod, g_ffn, w_router, seq)
    top_val, top_idx = lax.top_k(logits[:, :N_EXPERTS], TOP_K_EXPERTS)
    gates = jax.nn.softmax(top_val, axis=-1)

    n_assign = t * TOP_K_EXPERTS
    flat_e = top_idx.reshape(-1).astype(jnp.int32)
    onehot = (flat_e[:, None] == jnp.arange(N_EXPERTS, dtype=jnp.int32)[None, :]).astype(jnp.int32)
    rank = jnp.take_along_axis(jnp.cumsum(onehot, axis=0), flat_e[:, None], axis=1)[:, 0] - 1
    counts = jnp.sum(onehot, axis=0)
    padded = (counts + EXPERT_ROWS - 1) // EXPERT_ROWS * EXPERT_ROWS
    pad_end = jnp.cumsum(padded)
    dest = (pad_end - padded)[flat_e] + rank
    n_slots = (n_assign // EXPERT_ROWS + N_EXPERTS) * EXPERT_ROWS
    n_blocks = n_slots // EXPERT_ROWS
    slot_tok = jnp.zeros((n_slots,), jnp.int32).at[dest].set(jnp.arange(n_assign, dtype=jnp.int32) // TOP_K_EXPERTS)
    block_start = jnp.arange(n_blocks, dtype=jnp.int32) * EXPERT_ROWS
    block_expert = jnp.minimum(jnp.searchsorted(pad_end, block_start, side="right"), N_EXPERTS - 1).astype(jnp.int32)
    n_used = (pad_end[-1:] // EXPERT_ROWS).astype(jnp.int32)

    ys = _expert_ffn(h, slot_tok, block_expert, n_used, wg, wu, wd)
    dest2 = dest.reshape(t, TOP_K_EXPERTS)
    return _moe_combine(ys, dest2[:, 0], dest2[:, 1], gates, x2, mod, seq, g_final, final_norm)


def _token_mixer(x2, mod, g_mix, w_in, g_heads, w_out, batch, seq):
    pa, pb, pc, pd, p_idx, p_w, *pc_views = _in_projection(x2, mod, g_mix, _pack_w_in(w_in), seq)
    shape3 = lambda a: a.reshape(batch, -1, a.shape[-1])
    gh = g_heads.reshape(4, HEADS_PER_MIXER, HEAD_DIM)
    o_a = _stick_breaking_mixer(shape3(pa), gh[0])
    o_b = _dsa_mixer(shape3(pb), shape3(p_idx), shape3(p_w), gh[1])
    dilated = _dilated_branches([shape3(v) for v in [pc] + pc_views])
    o_d = _moba_mixer(shape3(pd), gh[3])
    return _out_projection(o_a, o_b, o_d, dilated, gh[2], w_out.astype(BF16), x2, mod, seq)


def kernel(x, c, w_ada, b_ada, g_mix, w_in, g_heads, w_out, g_ffn, w_ff_gate, w_ff_up, w_ff_down, w_router, w_exp_gate, w_exp_up, w_exp_down, g_final):
    batch, seq, d = x.shape
    depth = w_ada.shape[0]
    mods = _ada_modulation(c, w_ada, b_ada)
    x2 = x.reshape(batch * seq, d)
    for layer in range(depth):
        mod = mods[layer]
        x2 = _token_mixer(x2, mod, g_mix[layer], w_in[layer], g_heads[layer], w_out[layer], batch, seq)
        i = layer // 2
        last = layer == depth - 1
        if layer % 2 == 0:
            x2 = _dense_ffn(x2, mod, g_ffn[layer], w_ff_gate[i], w_ff_up[i], w_ff_down[i], seq, g_final, last)
        else:
            x2 = _moe_ffn(x2, mod, g_ffn[layer], w_router[i], w_exp_gate[i], w_exp_up[i], w_exp_down[i], seq,
                          g_final, last)
    return x2.reshape(batch, seq, d)
```

```python
import functools

import numpy as np
import jax
import jax.numpy as jnp
from jax import lax
from jax.experimental import pallas as pl
from jax.experimental.pallas import tpu as pltpu

F32 = jnp.float32
BF16 = jnp.bfloat16

HEAD_DIM = 64
HEADS_PER_MIXER = 4
MIXER_WIDTH = HEADS_PER_MIXER * HEAD_DIM
QKV_WIDTH = 3 * MIXER_WIDTH
IDX_HEADS = 8
IDX_DIM = 64
DSA_TOPK = 256
DILATIONS = (1, 4, 16)
DILATED_STEPS = 128
MOBA_BLOCK = 256
MOBA_TOPK = 3
MOBA_GROUP = 2
N_EXPERTS = 8
TOP_K_EXPERTS = 2
NORM_EPS = 1e-6
ATTN_SCALE = HEAD_DIM ** -0.5

LANES = 128
SUBLANES = 8
Q_TILE = 128
ROW_TILE = 512
FFN_TILE = 1408
EXPERT_ROWS = 512
EXPERT_FF_TILE = 1792
GATHER_ROWS = 512
ROWS_PER_TRIP = 8
VMEM_LIMIT = 56 * 1024 * 1024
MASKED = -1e30
SB_UNDERFLOW = 104.0
SB_FIRST_BLOCKS = 3

_NT = (((1,), (1,)), ((), ()))
_TN = (((0,), (0,)), ((), ()))


def _alibi_slopes(mixer_pos):
    idx = np.arange(HEADS_PER_MIXER, dtype=np.float32) * 3 + (mixer_pos + 1)
    return tuple(float(s) for s in np.exp2(-8.0 * idx / 12.0).astype(np.float32))


def _params(*semantics):
    return pltpu.CompilerParams(dimension_semantics=semantics, vmem_limit_bytes=VMEM_LIMIT)


def _modulated_norm(x, gain, shift, scale):
    y = x * lax.rsqrt(jnp.mean(x * x, axis=-1, keepdims=True) + NORM_EPS) * gain
    return y * (1.0 + scale) + shift


def _head_norm(acc, gain):
    return acc * lax.rsqrt(jnp.mean(acc * acc, axis=-1, keepdims=True) + NORM_EPS) * gain


def _head_cols(h):
    return slice(h * HEAD_DIM, (h + 1) * HEAD_DIM)


def _paired_loop(n, body, carry):
    carry = lax.fori_loop(0, n // 2, lambda jj, c: body(2 * jj + 1, body(2 * jj, c, 0), 1), carry)
    return lax.cond(n % 2 == 1, lambda c: body(n - 1, c, 0), lambda c: c, carry)


def _ada_kernel(c_ref, w_ref, b_ref, o_ref):
    c = c_ref[...]
    cond = c / (1.0 + jnp.exp(-c))
    o_ref[0, 0] = jnp.dot(cond, w_ref[0], preferred_element_type=F32,
                          precision=lax.Precision.HIGHEST) + b_ref[0, 0]


def _ada_modulation(c, w_ada, b_ada):
    depth, d, _ = w_ada.shape
    b = c.shape[0]
    out = pl.pallas_call(
        _ada_kernel,
        grid=(depth, 6),
        in_specs=[pl.BlockSpec((b, d), lambda l, k: (0, 0)),
                  pl.BlockSpec((1, d, d), lambda l, k: (l, 0, k)),
                  pl.BlockSpec((1, 1, 1, d), lambda l, k: (l, k, 0, 0))],
        out_specs=pl.BlockSpec((1, 1, b, d), lambda l, k: (l, k, 0, 0)),
        out_shape=jax.ShapeDtypeStruct((depth, 6, b, d), F32),
        compiler_params=_params("arbitrary", "arbitrary"),
        name="ada_modulation",
    )(c, w_ada, b_ada.reshape(depth, 6, 1, d))
    return out.transpose(0, 2, 1, 3)


IN_WIDTHS = (QKV_WIDTH, QKV_WIDTH, QKV_WIDTH, QKV_WIDTH, IDX_HEADS * IDX_DIM + LANES, LANES)


def _in_proj_kernel(x_ref, mod_ref, g_ref, w_ref, oa, ob, oc, od, oidx, ow, oc4, oc16, pc_ref):
    h = _modulated_norm(x_ref[...], g_ref[...], mod_ref[0, 0:1, :], mod_ref[0, 1:2, :]).astype(BF16)
    tm = x_ref.shape[0]
    off = 0
    for o_ref, width in zip((oa, ob, oc, od, oidx, ow), IN_WIDTHS):
        val = jnp.dot(h, w_ref[:, off:off + width], preferred_element_type=F32)
        o_ref[...] = val.astype(o_ref.dtype)
        if o_ref is oc:
            for s in range(QKV_WIDTH // LANES):
                pc_ref[s * tm:(s + 1) * tm, :] = val[:, s * LANES:(s + 1) * LANES]
        off += width
    for r, view in zip(DILATIONS[1:], (oc4, oc16)):
        for c in range(r):
            for s in range(QKV_WIDTH // LANES):
                col = c * QKV_WIDTH + s * LANES
                view[:, col:col + LANES] = pc_ref[pl.ds(s * tm + c, tm // r, stride=r), :].astype(view.dtype)


def _pack_w_in(w_in):
    d = w_in.shape[0]
    n_qkv = 4 * QKV_WIDTH
    n_qi = IDX_HEADS * IDX_DIM
    z = lambda n: jnp.zeros((d, n), w_in.dtype)
    return jnp.concatenate([w_in[:, :n_qkv + n_qi + IDX_DIM], z(LANES - IDX_DIM),
                            w_in[:, n_qkv + n_qi + IDX_DIM:], z(LANES - IDX_HEADS)], axis=1).astype(BF16)


def _in_projection(x2, mod, g_mix, w_packed, seq):
    t, d = x2.shape
    tm = min(ROW_TILE, seq)
    per_batch = seq // tm
    dtypes = (BF16, BF16, BF16, BF16, BF16, F32)
    views = DILATIONS[1:]
    return pl.pallas_call(
        _in_proj_kernel,
        grid=(t // tm,),
        in_specs=[pl.BlockSpec((tm, d), lambda i: (i, 0)),
                  pl.BlockSpec((1, 6, d), lambda i: (i // per_batch, 0, 0)),
                  pl.BlockSpec((1, d), lambda i: (0, 0)),
                  pl.BlockSpec(w_packed.shape, lambda i: (0, 0))],
        out_specs=([pl.BlockSpec((tm, w), lambda i: (i, 0)) for w in IN_WIDTHS]
                   + [pl.BlockSpec((tm // r, r * QKV_WIDTH), lambda i: (i, 0)) for r in views]),
        out_shape=([jax.ShapeDtypeStruct((t, w), dt) for w, dt in zip(IN_WIDTHS, dtypes)]
                   + [jax.ShapeDtypeStruct((t // r, r * QKV_WIDTH), BF16) for r in views]),
        scratch_shapes=[pltpu.VMEM((tm * (QKV_WIDTH // LANES), LANES), F32)],
        compiler_params=_params("arbitrary"),
        name="in_projection",
    )(x2, mod, g_mix.reshape(1, d), w_packed)


def _sb_kernel(q_ref, k_ref, v_ref, g_ref, o_ref):
    tq = q_ref.shape[1]
    i = pl.program_id(1)
    key_row = lax.broadcasted_iota(jnp.int32, (tq, 1), 0)
    q_lane = lax.broadcasted_iota(jnp.int32, (1, tq), 1)
    later = (lax.broadcasted_iota(jnp.int32, (tq, tq), 1) > lax.broadcasted_iota(jnp.int32, (tq, tq), 0))
    later = jnp.where(later, 1.0, 0.0).astype(BF16)

    qs = [q_ref[0, :, _head_cols(h)] * ATTN_SCALE for h in range(HEADS_PER_MIXER)]

    heads = range(HEADS_PER_MIXER)

    def blocks(starts, keeps, tails):
        z = jnp.concatenate([lax.dot_general(k_ref[0, pl.ds(st, tq), _head_cols(h)], qs[h], _NT,
                                             preferred_element_type=F32) for st in starts for h in heads], axis=1)
        softplus = jnp.maximum(z, 0.0) + jnp.log(1.0 + jnp.exp(-jnp.abs(z)))
        keep = None
        if any(kp is not None for kp in keeps):
            ones = jnp.ones((tq, tq), F32)
            keep = jnp.concatenate([ones if kp is None else kp for kp in keeps for _ in heads], axis=1)
        log_1m = -softplus if keep is None else -softplus * keep
        hi = log_1m.astype(BF16)
        lo = (log_1m - hi.astype(F32)).astype(BF16)
        inside = jnp.dot(later, hi, preferred_element_type=F32) + jnp.dot(later, lo, preferred_element_type=F32)
        block_sum = jnp.sum(log_1m, axis=0, keepdims=True)
        piece = lambda x, b, h: x[:, (b * HEADS_PER_MIXER + h) * tq:(b * HEADS_PER_MIXER + h + 1) * tq]
        tail_rows = []
        tails = list(tails)
        for b in range(len(starts)):
            tail_rows += tails
            tails = [tails[h] + piece(block_sum, b, h) for h in heads]
        a = jnp.exp(z - softplus + inside + jnp.concatenate(tail_rows, axis=1))
        if keep is not None:
            a = a * keep
        a = a.astype(BF16)
        av = [sum(lax.dot_general(v_ref[0, pl.ds(st, tq), _head_cols(h)], piece(a, b, h), _TN,
                                  preferred_element_type=F32) for b, st in enumerate(starts)) for h in heads]
        return tails, av

    starts = [pl.multiple_of(jnp.maximum(i - n, 0) * tq, tq) for n in range(SB_FIRST_BLOCKS)]
    keeps = [jnp.where(key_row < q_lane, 1.0, 0.0)]
    keeps += [jnp.full((tq, tq), jnp.where(i >= n, 1.0, 0.0), F32) for n in range(1, SB_FIRST_BLOCKS)]
    tails, accs = blocks(starts, keeps, [jnp.zeros((1, tq), F32)] * HEADS_PER_MIXER)
    state = tuple(zip(tails, accs))

    def body(carry):
        j, state = carry
        tails, av = blocks([pl.multiple_of(j * tq, tq)], [None], [tail for tail, _ in state])
        return j - 1, tuple((tails[h], state[h][1] + av[h]) for h in heads)

    def cond(carry):
        j, state = carry
        worst = functools.reduce(jnp.maximum, [tail for tail, _ in state])
        return jnp.logical_and(j >= 0, jnp.max(worst) > -SB_UNDERFLOW)

    _, state = lax.while_loop(cond, body, (i - SB_FIRST_BLOCKS, state))
    outs = []
    for h in range(HEADS_PER_MIXER):
        o = state[h][1]
        outs.append(o * lax.rsqrt(jnp.mean(o * o, axis=0, keepdims=True) + NORM_EPS) * g_ref[:, h:h + 1])
    o_ref[0] = jnp.concatenate(outs, axis=0).T.astype(o_ref.dtype)


def _stick_breaking_mixer(p, g_heads):
    b, seq, _ = p.shape
    tq = min(Q_TILE, seq)
    return pl.pallas_call(
        _sb_kernel,
        grid=(b, seq // tq),
        in_specs=[pl.BlockSpec((1, tq, MIXER_WIDTH), lambda bi, i: (bi, i, 0)),
                  pl.BlockSpec((1, seq, MIXER_WIDTH), lambda bi, i: (bi, 0, 1)),
                  pl.BlockSpec((1, seq, MIXER_WIDTH), lambda bi, i: (bi, 0, 2)),
                  pl.BlockSpec((HEAD_DIM, HEADS_PER_MIXER), lambda bi, i: (0, 0))],
        out_specs=pl.BlockSpec((1, tq, MIXER_WIDTH), lambda bi, i: (bi, i, 0)),
        out_shape=jax.ShapeDtypeStruct((b, seq, MIXER_WIDTH), BF16),
        compiler_params=_params("arbitrary", "arbitrary"),
        name="stick_breaking",
    )(p, p, p, g_heads.T)


def _dsa_kernel(q_ref, k_ref, v_ref, qi_ref, ki_ref, w_ref, g_ref, o_ref, sc_ref, sc16_ref, s_ref, p_ref, *, slopes):
    tq = q_ref.shape[1]
    tk = sc_ref.shape[1]
    i = pl.program_id(1)
    n_blocks = ((i + 1) * tq + tk - 1) // tk
    qpos = i * tq + lax.broadcasted_iota(jnp.int32, (1, tq), 1)
    key_row = lax.broadcasted_iota(jnp.int32, (tk, 1), 0)
    neg_inf = float("-inf")

    w_t = w_ref[0].T * (IDX_HEADS ** -0.5 * IDX_DIM ** -0.5)
    w_rows = [w_t[h:h + 1, :] for h in range(IDX_HEADS)]
    qi = qi_ref[0]

    def top_half(x):
        bits = lax.bitcast_convert_type(x, jnp.int32) & jnp.int32(-65536)
        return lax.bitcast_convert_type(bits, F32).astype(BF16)

    def score_body(j, carry, slot):
        start = pl.multiple_of(j * tk, tk)
        ki = ki_ref[0, pl.ds(start, tk), 0:IDX_DIM]
        sc = jnp.zeros((tk, tq), F32)
        for h in range(IDX_HEADS):
            x = lax.dot_general(ki, qi[:, h * IDX_DIM:(h + 1) * IDX_DIM], _NT, preferred_element_type=F32)
            sc = sc + w_rows[h] * jnp.maximum(x, 0.0)
        sc = jnp.where((start + key_row) <= qpos, sc + 0.0, neg_inf)
        sc_ref[j] = sc
        sc16_ref[j] = top_half(sc)
        return carry

    _paired_loop(n_blocks, score_body, 0)

    def count(pred):
        def add_block(j, cnt, slot):
            c = jnp.where(pred(sc_ref[j]), 1.0, 0.0)
            return cnt + jnp.sum(c.reshape(tk // 8, 8, tq), axis=0)

        cnt = _paired_loop(n_blocks, add_block, jnp.zeros((8, tq), F32))
        return jnp.sum(cnt, axis=0, keepdims=True)

    int_min = jnp.int32(-2 ** 31)

    def ordered_to_float(u):
        key = u ^ int_min
        bits = jnp.where(key >= 0, key, key ^ jnp.int32(0x7FFFFFFF))
        return lax.bitcast_convert_type(bits, F32)

    def count_top_half(cand16):
        def add_block(j, cnt, slot):
            c = jnp.where(sc16_ref[j] >= cand16, jnp.ones((), BF16), jnp.zeros((), BF16))
            for g in range(tk // 16):
                cnt = cnt + c[g * 16:(g + 1) * 16]
            return cnt

        cnt = _paired_loop(n_blocks, add_block, jnp.zeros((16, tq), BF16))
        return jnp.sum(cnt.astype(F32), axis=0, keepdims=True)

    def search_body(step, carry, half):
        prefix, n_at_prefix = carry
        cand = prefix | jnp.left_shift(jnp.int32(1), 31 - step)
        cand_f = ordered_to_float(cand)
        n_ge = count_top_half(top_half(cand_f)) if half else count(lambda s: s >= cand_f)
        keep = n_ge >= DSA_TOPK
        return jnp.where(keep, cand, prefix), jnp.where(keep, n_ge, n_at_prefix)

    keep_all = (qpos + 1) <= DSA_TOPK

    carry = (jnp.zeros((1, tq), jnp.int32), jnp.zeros((1, tq), F32))
    carry = lax.fori_loop(0, 16, functools.partial(search_body, half=True), carry)
    prefix, n_ge_tau = lax.fori_loop(16, 32, functools.partial(search_body, half=False), carry)
    tau = jnp.where(keep_all, float(np.finfo(np.float32).min), ordered_to_float(prefix))
    no_ties = jnp.min(jnp.where(jnp.logical_or(keep_all, n_ge_tau == DSA_TOPK), 1.0, 0.0)) > 0.5

    earlier = (lax.broadcasted_iota(jnp.int32, (tk, tk), 1) < lax.broadcasted_iota(jnp.int32, (tk, tk), 0))
    earlier = jnp.where(earlier, 1.0, 0.0).astype(BF16)
    qs = [q_ref[0, :, _head_cols(h)] * ATTN_SCALE for h in range(HEADS_PER_MIXER)]
    key_bias = [slopes[h] * lax.broadcasted_iota(jnp.int32, (tk, tq), 0).astype(F32)
                for h in range(HEADS_PER_MIXER)]

    def attn_body(j, carry, slot, n_ties=None):
        ties_seen, stats = carry
        s_slot = [s_ref.at[slot * HEADS_PER_MIXER + h] for h in range(HEADS_PER_MIXER)]
        p_slot = [p_ref.at[slot * HEADS_PER_MIXER + h] for h in range(HEADS_PER_MIXER)]
        start = pl.multiple_of(j * tk, tk)
        sc = sc_ref[j]
        if n_ties is None:
            sel = sc >= tau
        else:
            tie = jnp.where(sc == tau, 1.0, 0.0)
            rank = jnp.dot(earlier, tie.astype(BF16), preferred_element_type=F32) + ties_seen
            sel = jnp.where(sc > tau, 1.0, jnp.where(rank < n_ties, tie, 0.0)) > 0.5
            ties_seen = ties_seen + jnp.sum(tie, axis=0, keepdims=True)
        block_pos = (j * tk - i * tq).astype(F32)
        for h in range(HEADS_PER_MIXER):
            k = k_ref[0, pl.ds(start, tk), _head_cols(h)]
            s = lax.dot_general(k, qs[h], _NT, preferred_element_type=F32) + key_bias[h]
            s_slot[h][...] = jnp.where(sel, s, MASKED)
        locals_ = []
        for h in range(HEADS_PER_MIXER):
            s = s_slot[h][...]
            m_loc = jnp.max(s, axis=0, keepdims=True)
            p = jnp.exp(s - m_loc)
            locals_.append((m_loc, jnp.sum(p, axis=0, keepdims=True)))
            p_slot[h][...] = p.astype(BF16)
        new_stats = []
        for h in range(HEADS_PER_MIXER):
            m, l, acc = stats[h]
            m_loc, l_blk = locals_[h]
            v = v_ref[0, pl.ds(start, tk), _head_cols(h)]
            pv_blk = lax.dot_general(v, p_slot[h][...], _TN, preferred_element_type=F32)
            m_blk = m_loc + slopes[h] * block_pos
            m_new = jnp.maximum(m, m_blk)
            w_old = jnp.exp(m - m_new)
            w_blk = jnp.where(m_loc > 0.5 * MASKED, jnp.exp(m_blk - m_new), 0.0)
            new_stats.append((m_new, w_old * l + w_blk * l_blk, w_old * acc + w_blk * pv_blk))
        return ties_seen, tuple(new_stats)

    init = tuple((jnp.full((1, tq), MASKED, F32), jnp.zeros((1, tq), F32), jnp.zeros((HEAD_DIM, tq), F32))
                 for _ in range(HEADS_PER_MIXER))
    start_carry = (jnp.zeros((1, tq), F32), init)

    def attend_without_ties():
        return _paired_loop(n_blocks, attn_body, start_carry)[1]

    def attend_with_ties():
        n_gt = count(lambda s: s > tau)
        n_ties = jnp.where(keep_all, 1e9, DSA_TOPK - n_gt)
        return _paired_loop(n_blocks, functools.partial(attn_body, n_ties=n_ties), start_carry)[1]

    stats = lax.cond(no_ties, attend_without_ties, attend_with_ties)
    outs = []
    for h in range(HEADS_PER_MIXER):
        _, l, acc = stats[h]
        o = acc / l
        outs.append(o * lax.rsqrt(jnp.mean(o * o, axis=0, keepdims=True) + NORM_EPS) * g_ref[:, h:h + 1])
    o_ref[0] = jnp.concatenate(outs, axis=0).T.astype(o_ref.dtype)


def _dsa_mixer(p, p_idx, p_w, g_heads):
    b, seq, _ = p.shape
    tq = min(4 * Q_TILE, seq)
    tk = tq
    assert seq // 16 <= 256, "the packed-bf16 count accumulators are exact only up to 256 adds"
    qi_width = IDX_HEADS * IDX_DIM
    return pl.pallas_call(
        functools.partial(_dsa_kernel, slopes=_alibi_slopes(0)),
        grid=(b, seq // tq),
        in_specs=[pl.BlockSpec((1, tq, MIXER_WIDTH), lambda bi, i: (bi, i, 0)),
                  pl.BlockSpec((1, seq, MIXER_WIDTH), lambda bi, i: (bi, 0, 1)),
                  pl.BlockSpec((1, seq, MIXER_WIDTH), lambda bi, i: (bi, 0, 2)),
                  pl.BlockSpec((1, tq, qi_width), lambda bi, i: (bi, i, 0)),
                  pl.BlockSpec((1, seq, LANES), lambda bi, i: (bi, 0, qi_width // LANES)),
                  pl.BlockSpec((1, tq, LANES), lambda bi, i: (bi, i, 0)),
                  pl.BlockSpec((HEAD_DIM, HEADS_PER_MIXER), lambda bi, i: (0, 0))],
        out_specs=pl.BlockSpec((1, tq, MIXER_WIDTH), lambda bi, i: (bi, i, 0)),
        out_shape=jax.ShapeDtypeStruct((b, seq, MIXER_WIDTH), BF16),
        scratch_shapes=[pltpu.VMEM((seq // tk, tk, tq), F32),
                        pltpu.VMEM((seq // tk, tk, tq), BF16),
                        pltpu.VMEM((2 * HEADS_PER_MIXER, tk, tq), F32),
                        pltpu.VMEM((2 * HEADS_PER_MIXER, tk, tq), BF16)],
        compiler_params=_params("arbitrary", "arbitrary"),
        name="dsa",
    )(p, p, p, p_idx, p_idx, p_w, g_heads.T)


def _band_kernel(q_ref, kp_ref, kc_ref, vp_ref, vc_ref, o_ref, lse_ref, *, dilation, slopes):
    tq = q_ref.shape[1]
    ui = pl.program_id(2)
    heads = range(HEADS_PER_MIXER)
    u_q = jnp.concatenate([ui * tq + lax.broadcasted_iota(jnp.int32, (1, tq), 1)] * HEADS_PER_MIXER, axis=1)
    u_k = ui * tq - DILATED_STEPS + lax.broadcasted_iota(jnp.int32, (DILATED_STEPS + tq, 1), 0)
    steps = u_q - u_k
    valid = jnp.logical_and(jnp.logical_and(steps >= 0, steps <= DILATED_STEPS), u_k >= 0)
    slope_row = jnp.concatenate([jnp.full((1, tq), slopes[h], F32) for h in heads], axis=1)
    s = jnp.concatenate(
        [lax.dot_general(jnp.concatenate([kp_ref[0, :, _head_cols(h)], kc_ref[0, :, _head_cols(h)]], axis=0),
                         q_ref[0, :, _head_cols(h)], _NT, preferred_element_type=F32) for h in heads], axis=1)
    s = jnp.where(valid, s * ATTN_SCALE - slope_row * (steps * dilation).astype(F32), MASKED)
    m = jnp.max(s, axis=0, keepdims=True)
    e = jnp.exp(s - m)
    den = jnp.sum(e, axis=0, keepdims=True)
    p = e.astype(BF16)
    lse = m + jnp.log(den)
    outs = []
    for h in heads:
        cols = slice(h * tq, (h + 1) * tq)
        v = jnp.concatenate([vp_ref[0, :, _head_cols(h)], vc_ref[0, :, _head_cols(h)]], axis=0)
        outs.append(lax.dot_general(v, p[:, cols], _TN, preferred_element_type=F32) / den[:, cols])
    o_ref[0] = jnp.concatenate(outs, axis=0).T
    lse_rows = jnp.concatenate([lse[:, h * tq:(h + 1) * tq] for h in heads]
                               + [jnp.zeros((LANES - HEADS_PER_MIXER, tq), F32)], axis=0)
    lse_ref[0] = lse_rows.T


def _dilated_branch(view, dilation, slopes):
    b, length, _ = view.shape
    classes = dilation
    tq = min(2 * Q_TILE, length)
    back = tq // DILATED_STEPS
    spec = lambda part, prev: (
        pl.BlockSpec((1, DILATED_STEPS, MIXER_WIDTH),
                     lambda bi, c, ui: (bi, jnp.maximum(ui * back - 1, 0), c * 3 + part)) if prev
        else pl.BlockSpec((1, tq, MIXER_WIDTH), lambda bi, c, ui: (bi, ui, c * 3 + part)))
    out, lse = pl.pallas_call(
        functools.partial(_band_kernel, dilation=dilation, slopes=slopes),
        grid=(b, classes, length // tq),
        in_specs=[spec(0, False), spec(1, True), spec(1, False), spec(2, True), spec(2, False)],
        out_specs=[pl.BlockSpec((1, tq, MIXER_WIDTH), lambda bi, c, ui: (bi, ui, c)),
                   pl.BlockSpec((1, tq, LANES), lambda bi, c, ui: (bi, ui, c))],
        out_shape=[jax.ShapeDtypeStruct((b, length, classes * MIXER_WIDTH), F32),
                   jax.ShapeDtypeStruct((b, length, classes * LANES), F32)],
        compiler_params=_params("arbitrary", "arbitrary", "arbitrary"),
        name=f"dilated_r{dilation}",
    )(view, view, view, view, view)
    return out, lse


def _merge_dilated(o_refs, l_refs, g_ref, scratch, rows):
    chunks = MIXER_WIDTH // LANES
    outs = [[o_refs[0][:, s * LANES:(s + 1) * LANES] for s in range(chunks)]]
    lses = [l_refs[0][...]]
    for r, o_view, l_view, o_nat, l_nat in zip(DILATIONS[1:], o_refs[1:], l_refs[1:], scratch[0::2], scratch[1::2]):
        for c in range(r):
            for s in range(chunks):
                col = c * MIXER_WIDTH + s * LANES
                o_nat[pl.ds(s * rows + c, rows // r, stride=r), :] = o_view[:, col:col + LANES]
            l_nat[pl.ds(c, rows // r, stride=r), :] = l_view[:, c * LANES:(c + 1) * LANES]
        outs.append([o_nat[s * rows:(s + 1) * rows, :] for s in range(chunks)])
        lses.append(l_nat[...])
    top = functools.reduce(jnp.maximum, lses)
    wts = [jnp.exp(l - top) for l in lses]
    def head_of(index):
        return sum(jnp.where(index >= h * HEAD_DIM, 1, 0) for h in range(1, HEADS_PER_MIXER))

    head_of_col = head_of(lax.broadcasted_iota(jnp.int32, (LANES, MIXER_WIDTH), 1))
    spread = jnp.where(lax.broadcasted_iota(jnp.int32, (LANES, MIXER_WIDTH), 0) == head_of_col, 1.0, 0.0)
    same_head = (head_of(lax.broadcasted_iota(jnp.int32, (MIXER_WIDTH, MIXER_WIDTH), 0))
                 == head_of(lax.broadcasted_iota(jnp.int32, (MIXER_WIDTH, MIXER_WIDTH), 1)))
    head_mean = jnp.where(same_head, 1.0 / HEAD_DIM, 0.0)

    def times(x, m):
        hi = x.astype(BF16)
        lo = (x - hi.astype(F32)).astype(BF16)
        m = m.astype(BF16)
        return jnp.dot(hi, m, preferred_element_type=F32) + jnp.dot(lo, m, preferred_element_type=F32)

    w_cols = [times(w, spread) for w in wts]
    full = [jnp.concatenate(o, axis=1) for o in outs]
    mixed = sum(w * o for w, o in zip(w_cols, full)) / sum(w_cols)
    return mixed * lax.rsqrt(times(mixed * mixed, head_mean) + NORM_EPS) * g_ref[...]


def _dilated_branches(views):
    slopes = _alibi_slopes(1)
    branches = [_dilated_branch(v, r, slopes) for v, r in zip(views, DILATIONS)]
    flat = lambda a: a.reshape(-1, a.shape[-1])
    return [flat(o) for o, _ in branches], [flat(l) for _, l in branches]


def _moba_kernel(q_ref, k_ref, v_ref, g_ref, o_ref, kmean_ref, chosen_ref, s_ref, p_ref, *, slopes):
    tq = q_ref.shape[1]
    n_kv = k_ref.shape[1] // MOBA_BLOCK
    own = pl.program_id(1)

    @pl.when(own == 0)
    def _():
        kmean_ref[...] = jnp.zeros_like(kmean_ref)
        for n in range(n_kv):
            blk = k_ref[0, n * MOBA_BLOCK:(n + 1) * MOBA_BLOCK, :].astype(F32)
            kmean_ref[n:n + 1, :] = jnp.sum(blk, axis=0, keepdims=True) * (1.0 / MOBA_BLOCK)

    key_row = lax.broadcasted_iota(jnp.int32, (MOBA_BLOCK, 1), 0)
    q_lane = lax.broadcasted_iota(jnp.int32, (1, tq), 1)
    blk_rows = kmean_ref.shape[0]
    blk_row = lax.broadcasted_iota(jnp.int32, (blk_rows, 1), 0)
    blk_f = blk_row.astype(F32)
    neg_inf = float("-inf")
    group = tq // MOBA_BLOCK
    first_own = own * group
    sub_block = sum(jnp.where(q_lane >= g * MOBA_BLOCK, 1, 0) for g in range(1, group)) if group > 1 else 0
    own_blk = first_own + sub_block
    q_local = q_lane - sub_block * MOBA_BLOCK

    qs = []
    for h in range(HEADS_PER_MIXER):
        q = q_ref[0, :, _head_cols(h)]
        gate = lax.dot_general(kmean_ref[:, _head_cols(h)], q.astype(F32), _NT, preferred_element_type=F32,
                               precision=lax.Precision.HIGHEST)
        gate = jnp.where(blk_row < own_blk, gate, neg_inf)
        picks = jnp.zeros((blk_rows, tq), F32)
        for _ in range(MOBA_TOPK):
            top = jnp.max(gate, axis=0, keepdims=True)
            is_top = jnp.logical_and(gate == top, top > neg_inf)
            first = jnp.min(jnp.where(is_top, blk_f, float(blk_rows)), axis=0, keepdims=True)
            pick = blk_f == first
            picks = jnp.where(pick, 1.0, picks)
            gate = jnp.where(pick, neg_inf, gate)
        chosen_ref[h] = picks
        qs.append(q * ATTN_SCALE)

    key_bias = [slopes[h] * lax.broadcasted_iota(jnp.int32, (MOBA_BLOCK, tq), 0).astype(F32)
                for h in range(HEADS_PER_MIXER)]

    def block_softmax(start, keep=None, slot=0):
        s_slot = [s_ref.at[slot * HEADS_PER_MIXER + h] for h in range(HEADS_PER_MIXER)]
        p_slot = [p_ref.at[slot * HEADS_PER_MIXER + h] for h in range(HEADS_PER_MIXER)]
        for h in range(HEADS_PER_MIXER):
            k = k_ref[0, pl.ds(start, MOBA_BLOCK), _head_cols(h)]
            s = lax.dot_general(k, qs[h], _NT, preferred_element_type=F32) + key_bias[h]
            s_slot[h][...] = s if keep is None else jnp.where(keep, s, MASKED)
        pieces = []
        for h in range(HEADS_PER_MIXER):
            s = s_slot[h][...]
            m_loc = jnp.max(s, axis=0, keepdims=True)
            p = jnp.exp(s - m_loc)
            pieces.append((m_loc, jnp.sum(p, axis=0, keepdims=True)))
            p_slot[h][...] = p.astype(BF16)
        out = []
        for h in range(HEADS_PER_MIXER):
            v = v_ref[0, pl.ds(start, MOBA_BLOCK), _head_cols(h)]
            out.append(pieces[h] + (lax.dot_general(v, p_slot[h][...], _TN, preferred_element_type=F32),))
        return out

    def merge(stats, j, pieces, use):
        block_pos = ((j - first_own) * MOBA_BLOCK).astype(F32)
        merged = []
        for h, (m_loc, l_blk, pv_blk) in enumerate(pieces):
            m_blk = jnp.where(use[h], m_loc + slopes[h] * block_pos, MASKED)
            m, l, acc = stats[h]
            m_new = jnp.maximum(m, m_blk)
            w_old = jnp.exp(m - m_new)
            w_blk = jnp.where(use[h], jnp.exp(m_blk - m_new), 0.0)
            merged.append((m_new, w_old * l + w_blk * l_blk, w_old * acc + w_blk * pv_blk))
        return tuple(merged)

    def picked(h, j):
        return chosen_ref[h, pl.ds(j, 1), :] > 0.5

    def body(j, stats, slot):
        start = pl.multiple_of(j * MOBA_BLOCK, MOBA_BLOCK)
        return merge(stats, j, block_softmax(start, slot=slot), [picked(h, j) for h in range(HEADS_PER_MIXER)])

    stats = tuple((jnp.full((1, tq), MASKED, F32), jnp.zeros((1, tq), F32), jnp.zeros((HEAD_DIM, tq), F32))
                  for _ in range(HEADS_PER_MIXER))
    stats = _paired_loop(first_own, body, stats)
    for g in range(group):
        j = first_own + g
        keep = jnp.logical_or(own_blk > j, jnp.logical_and(own_blk == j, key_row <= q_local))
        pieces = block_softmax(pl.multiple_of(j * MOBA_BLOCK, MOBA_BLOCK), keep=keep, slot=g % 2)
        use = [jnp.logical_or(own_blk == j, jnp.logical_and(own_blk > j, picked(h, j)))
               for h in range(HEADS_PER_MIXER)]
        stats = merge(stats, j, pieces, use)
    outs = []
    for h in range(HEADS_PER_MIXER):
        _, l, acc = stats[h]
        o = acc / l
        outs.append(o * lax.rsqrt(jnp.mean(o * o, axis=0, keepdims=True) + NORM_EPS) * g_ref[:, h:h + 1])
    o_ref[0] = jnp.concatenate(outs, axis=0).T.astype(o_ref.dtype)


def _moba_mixer(p, g_heads):
    b, seq, _ = p.shape
    tq = MOBA_GROUP * MOBA_BLOCK if seq % (MOBA_GROUP * MOBA_BLOCK) == 0 else MOBA_BLOCK
    blk_rows =-(-(seq // MOBA_BLOCK) // SUBLANES) * SUBLANES
    return pl.pallas_call(
        functools.partial(_moba_kernel, slopes=_alibi_slopes(2)),
        grid=(b, seq // tq),
        in_specs=[pl.BlockSpec((1, tq, MIXER_WIDTH), lambda bi, i: (bi, i, 0)),
                  pl.BlockSpec((1, seq, MIXER_WIDTH), lambda bi, i: (bi, 0, 1)),
                  pl.BlockSpec((1, seq, MIXER_WIDTH), lambda bi, i: (bi, 0, 2)),
                  pl.BlockSpec((HEAD_DIM, HEADS_PER_MIXER), lambda bi, i: (0, 0))],
        out_specs=pl.BlockSpec((1, tq, MIXER_WIDTH), lambda bi, i: (bi, i, 0)),
        out_shape=jax.ShapeDtypeStruct((b, seq, MIXER_WIDTH), BF16),
        scratch_shapes=[pltpu.VMEM((blk_rows, MIXER_WIDTH), F32),
                        pltpu.VMEM((HEADS_PER_MIXER, blk_rows, tq), F32),
                        pltpu.VMEM((2 * HEADS_PER_MIXER, MOBA_BLOCK, tq), F32),
                        pltpu.VMEM((2 * HEADS_PER_MIXER, MOBA_BLOCK, tq), BF16)],
        compiler_params=_params("arbitrary", "arbitrary"),
        name="moba",
    )(p, p, p, g_heads.T)


def _out_proj_kernel(oa, ob, od, o1, o4, o16, l1, l4, l16, gc_ref, w_ref, x_ref, mod_ref, o_ref, *scratch):
    rows = x_ref.shape[0]
    oc = _merge_dilated((o1, o4, o16), (l1, l4, l16), gc_ref, scratch, rows).astype(BF16)
    acc = jnp.zeros(x_ref.shape, F32)
    for m, o in enumerate((oa[...], ob[...], oc, od[...])):
        acc = acc + jnp.dot(o, w_ref[m * MIXER_WIDTH:(m + 1) * MIXER_WIDTH, :], preferred_element_type=F32)
    o_ref[...] = x_ref[...] + mod_ref[0, 2:3, :] * acc


def _out_projection(o_a, o_b, o_d, dilated, g_heads_c, w_out, x2, mod, seq):
    t, d = x2.shape
    tm = min(ROW_TILE, seq)
    per_batch = seq // tm
    o_spec = pl.BlockSpec((tm, MIXER_WIDTH), lambda i: (i, 0))
    branch_outs, branch_lses = dilated
    view_specs = ([pl.BlockSpec((tm // r, r * MIXER_WIDTH), lambda i: (i, 0)) for r in DILATIONS]
                  + [pl.BlockSpec((tm // r, r * LANES), lambda i: (i, 0)) for r in DILATIONS])
    scratch = []
    for _ in DILATIONS[1:]:
        scratch += [pltpu.VMEM((tm * (MIXER_WIDTH // LANES), LANES), F32), pltpu.VMEM((tm, LANES), F32)]
    return pl.pallas_call(
        _out_proj_kernel,
        grid=(t // tm,),
        in_specs=[o_spec] * 3 + view_specs + [pl.BlockSpec((1, MIXER_WIDTH), lambda i: (0, 0)),
                                              pl.BlockSpec(w_out.shape, lambda i: (0, 0)),
                                              pl.BlockSpec((tm, d), lambda i: (i, 0)),
                                              pl.BlockSpec((1, 6, d), lambda i: (i // per_batch, 0, 0))],
        out_specs=pl.BlockSpec((tm, d), lambda i: (i, 0)),
        out_shape=jax.ShapeDtypeStruct((t, d), F32),
        scratch_shapes=scratch,
        compiler_params=_params("arbitrary"),
        name="out_projection",
    )(*[o.reshape(t, MIXER_WIDTH) for o in (o_a, o_b, o_d)], *branch_outs, *branch_lses,
      g_heads_c.reshape(1, MIXER_WIDTH), w_out, x2, mod)


def _rms_gain(x, gain):
    return x * lax.rsqrt(jnp.mean(x * x, axis=-1, keepdims=True) + NORM_EPS) * gain


def _ffn_kernel(x_ref, mod_ref, g_ref, fg_ref, wg_ref, wu_ref, wd_ref, o_ref, h_ref, acc_ref, *, final_norm):
    f = pl.program_id(1)

    @pl.when(f == 0)
    def _():
        h_ref[...] = _modulated_norm(x_ref[...], g_ref[...], mod_ref[0, 3:4, :], mod_ref[0, 4:5, :]).astype(BF16)
        acc_ref[...] = jnp.zeros_like(acc_ref)

    h = h_ref[...]
    gate = jnp.dot(h, wg_ref[...], preferred_element_type=F32)
    up = jnp.dot(h, wu_ref[...], preferred_element_type=F32)
    act = (gate / (1.0 + jnp.exp(-gate)) * up).astype(BF16)
    acc_ref[...] += jnp.dot(act, wd_ref[...], preferred_element_type=F32)

    @pl.when(f == pl.num_programs(1) - 1)
    def _():
        out = x_ref[...] + mod_ref[0, 5:6, :] * acc_ref[...]
        o_ref[...] = _rms_gain(out, fg_ref[...]) if final_norm else out


def _dense_ffn(x2, mod, g_ffn, wg, wu, wd, seq, g_final, final_norm):
    t, d = x2.shape
    d_ff = wg.shape[1]
    tm = min(ROW_TILE, seq)
    tf = FFN_TILE if d_ff % FFN_TILE == 0 else d_ff
    per_batch = seq // tm
    return pl.pallas_call(
        functools.partial(_ffn_kernel, final_norm=final_norm),
        grid=(t // tm, d_ff // tf),
        in_specs=[pl.BlockSpec((tm, d), lambda i, f: (i, 0)),
                  pl.BlockSpec((1, 6, d), lambda i, f: (i // per_batch, 0, 0)),
                  pl.BlockSpec((1, d), lambda i, f: (0, 0)),
                  pl.BlockSpec((1, d), lambda i, f: (0, 0)),
                  pl.BlockSpec((d, tf), lambda i, f: (0, f)),
                  pl.BlockSpec((d, tf), lambda i, f: (0, f)),
                  pl.BlockSpec((tf, d), lambda i, f: (f, 0))],
        out_specs=pl.BlockSpec((tm, d), lambda i, f: (i, 0)),
        out_shape=jax.ShapeDtypeStruct((t, d), F32),
        scratch_shapes=[pltpu.VMEM((tm, d), BF16), pltpu.VMEM((tm, d), F32)],
        compiler_params=_params("arbitrary", "arbitrary"),
        name="dense_ffn",
    )(x2, mod, g_ffn.reshape(1, d), g_final.reshape(1, d), wg.astype(BF16), wu.astype(BF16), wd.astype(BF16))


def _store_row_tiles(dst_ref, value):
    rows, d = value.shape
    chunks = d // LANES
    for s in range(chunks):
        dst_ref[pl.ds(s, rows, stride=chunks), :] = value[:, s * LANES:(s + 1) * LANES]


def _load_row_tile_chunk(src_ref, s, rows, chunks):
    return src_ref[pl.ds(s, rows, stride=chunks), :]


def _router_kernel(x_ref, mod_ref, g_ref, wr_ref, h_ref, logit_ref):
    h = _modulated_norm(x_ref[...], g_ref[...], mod_ref[0, 3:4, :], mod_ref[0, 4:5, :])
    _store_row_tiles(h_ref, h)
    logit_ref[...] = jnp.dot(h, wr_ref[...], preferred_element_type=F32, precision=lax.Precision.HIGHEST)


def _router(x2, mod, g_ffn, w_router, seq):
    t, d = x2.shape
    tm = min(ROW_TILE, seq)
    per_batch = seq // tm
    wr = jnp.zeros((d, LANES), F32).at[:, :N_EXPERTS].set(w_router.astype(F32))
    return pl.pallas_call(
        _router_kernel,
        grid=(t // tm,),
        in_specs=[pl.BlockSpec((tm, d), lambda i: (i, 0)),
                  pl.BlockSpec((1, 6, d), lambda i: (i // per_batch, 0, 0)),
                  pl.BlockSpec((1, d), lambda i: (0, 0)),
                  pl.BlockSpec((d, LANES), lambda i: (0, 0))],
        out_specs=[pl.BlockSpec((tm * (d // LANES), LANES), lambda i: (i, 0)),
                   pl.BlockSpec((tm, LANES), lambda i: (i, 0))],
        out_shape=[jax.ShapeDtypeStruct((t * (d // LANES), LANES), F32), jax.ShapeDtypeStruct((t, LANES), F32)],
        compiler_params=_params("arbitrary"),
        name="moe_router",
    )(x2, mod, g_ffn.reshape(1, d), wr)


def _row_copy(src_hbm, row, dst_ref, r, sem, chunks):
    src = src_hbm.at[pl.ds(pl.multiple_of(row * chunks, chunks), chunks), :]
    return pltpu.make_async_copy(src, dst_ref.at[pl.ds(pl.multiple_of(r * chunks, chunks), chunks), :], sem)


def _expert_kernel(be_ref, used_ref, cur_idx_ref, nxt_idx_ref, h_hbm, wg_ref, wu_ref, wd_ref, o_ref,
                   xin_ref, xb_ref, acc_ref, sem):
    m = pl.program_id(0)
    f = pl.program_id(1)
    n_used = used_ref[0]
    live = m < n_used
    rows, d = xb_ref.shape
    chunks = d // LANES

    def row_copies(idx_ref, slot, start):
        if not start:
            half = rows * chunks // 2
            for lane in range(2):
                pltpu.make_async_copy(h_hbm.at[pl.ds(0, half), :], xin_ref.at[slot, pl.ds(0, half), :],
                                      sem.at[slot, lane]).wait()
            return

        def body(group, c):
            for u in range(ROWS_PER_TRIP):
                r = ROWS_PER_TRIP * group + u
                _row_copy(h_hbm, idx_ref[0, 0, r], xin_ref.at[slot], r, sem.at[slot, u % 2],
                          chunks).start(priority=u % 2)
            return c
        lax.fori_loop(0, rows // ROWS_PER_TRIP, body, 0)

    @pl.when(jnp.logical_and(live, f == 0))
    def _():
        @pl.when(m == 0)
        def _():
            row_copies(cur_idx_ref, 0, start=True)

        for slot in range(2):
            @pl.when(m % 2 == slot)
            def _(slot=slot):
                row_copies(cur_idx_ref, slot, start=False)
                for s in range(chunks):
                    xb_ref[:, s * LANES:(s + 1) * LANES] = _load_row_tile_chunk(
                        xin_ref.at[slot], s, rows, chunks).astype(BF16)

                @pl.when(m + 1 < n_used)
                def _():
                    row_copies(nxt_idx_ref, 1 - slot, start=True)

        acc_ref[...] = jnp.zeros_like(acc_ref)

    @pl.when(live)
    def _():
        h = xb_ref[...]
        gate = jnp.dot(h, wg_ref[0], preferred_element_type=F32)
        up = jnp.dot(h, wu_ref[0], preferred_element_type=F32)
        act = (gate / (1.0 + jnp.exp(-gate)) * up).astype(BF16)
        acc_ref[...] += jnp.dot(act, wd_ref[0], preferred_element_type=F32)

    @pl.when(f == pl.num_programs(1) - 1)
    def _():
        _store_row_tiles(o_ref, jnp.where(live, acc_ref[...], 0.0))


def _expert_ffn(h_rows, slot_tok, block_expert, n_used, wg, wu, wd):
    d = wg.shape[1]
    chunks = d // LANES
    n_slots = slot_tok.shape[0]
    d_ff = wg.shape[2]
    tm = EXPERT_ROWS
    tf = EXPERT_FF_TILE if d_ff % EXPERT_FF_TILE == 0 else d_ff
    nf = d_ff // tf
    nm = n_slots // tm

    def fcol(m, f, used):
        return jnp.where(m < used[0], f, nf - 1)

    idx_spec = lambda shift: pl.BlockSpec((1, 1, tm), lambda m, f, be, used: (jnp.minimum(m + shift, nm - 1), 0, 0),
                                          memory_space=pltpu.SMEM)
    grid_spec = pltpu.PrefetchScalarGridSpec(
        num_scalar_prefetch=2,
        grid=(nm, nf),
        in_specs=[idx_spec(0), idx_spec(1), pl.BlockSpec(memory_space=pl.ANY),
                  pl.BlockSpec((1, d, tf), lambda m, f, be, used: (be[m], 0, fcol(m, f, used))),
                  pl.BlockSpec((1, d, tf), lambda m, f, be, used: (be[m], 0, fcol(m, f, used))),
                  pl.BlockSpec((1, tf, d), lambda m, f, be, used: (be[m], fcol(m, f, used), 0))],
        out_specs=pl.BlockSpec((tm * chunks, LANES), lambda m, f, be, used: (m, 0)),
        scratch_shapes=[pltpu.VMEM((2, tm * chunks, LANES), F32), pltpu.VMEM((tm, d), BF16),
                        pltpu.VMEM((tm, d), F32), pltpu.SemaphoreType.DMA((2, 2))])
    idx = slot_tok.reshape(nm, 1, tm)
    return pl.pallas_call(
        _expert_kernel,
        grid_spec=grid_spec,
        out_shape=jax.ShapeDtypeStruct((n_slots * chunks, LANES), F32),
        compiler_params=_params("arbitrary", "arbitrary"),
        name="expert_ffn",
    )(block_expert, n_used, idx, idx, h_rows, wg.astype(BF16), wu.astype(BF16), wd.astype(BF16))


def _combine_kernel(d0_ref, d1_ref, ys_hbm, x_ref, gates_ref, mod_ref, fg_ref, o_ref, y0_ref, y1_ref, sem, *,
                    final_norm):
    rows, d = o_ref.shape
    chunks = d // LANES

    def start(group, c):
        for u in range(ROWS_PER_TRIP // 2):
            r = ROWS_PER_TRIP // 2 * group + u
            _row_copy(ys_hbm, d0_ref[0, 0, r], y0_ref, r, sem.at[0], chunks).start(priority=0)
            _row_copy(ys_hbm, d1_ref[0, 0, r], y1_ref, r, sem.at[1], chunks).start(priority=1)
        return c

    lax.fori_loop(0, rows // (ROWS_PER_TRIP // 2), start, 0)
    for k, y_ref in enumerate((y0_ref, y1_ref)):
        pltpu.make_async_copy(ys_hbm.at[pl.ds(0, rows * chunks), :], y_ref, sem.at[k]).wait()
    gates = gates_ref[...]
    for s in range(chunks):
        cols = slice(s * LANES, (s + 1) * LANES)
        y = (_load_row_tile_chunk(y0_ref, s, rows, chunks) * gates[:, 0:1]
             + _load_row_tile_chunk(y1_ref, s, rows, chunks) * gates[:, 1:2])
        o_ref[:, cols] = x_ref[:, cols] + mod_ref[0, 5:6, cols] * y
    if final_norm:
        o_ref[...] = _rms_gain(o_ref[...], fg_ref[...])


def _moe_combine(ys, dest0, dest1, gates, x2, mod, seq, g_final, final_norm):
    t, d = x2.shape
    tm = min(GATHER_ROWS, seq)
    steps = t // tm
    per_batch = seq // tm
    idx_spec = pl.BlockSpec((1, 1, tm), lambda i: (i, 0, 0), memory_space=pltpu.SMEM)
    return pl.pallas_call(
        functools.partial(_combine_kernel, final_norm=final_norm),
        grid=(steps,),
        in_specs=[idx_spec, idx_spec, pl.BlockSpec(memory_space=pl.ANY),
                  pl.BlockSpec((tm, d), lambda i: (i, 0)),
                  pl.BlockSpec((tm, TOP_K_EXPERTS), lambda i: (i, 0)),
                  pl.BlockSpec((1, 6, d), lambda i: (i // per_batch, 0, 0)),
                  pl.BlockSpec((1, d), lambda i: (0, 0))],
        out_specs=pl.BlockSpec((tm, d), lambda i: (i, 0)),
        out_shape=jax.ShapeDtypeStruct((t, d), F32),
        scratch_shapes=[pltpu.VMEM((tm * (d // LANES), LANES), F32), pltpu.VMEM((tm * (d // LANES), LANES), F32),
                        pltpu.SemaphoreType.DMA((2,))],
        compiler_params=_params("arbitrary"),
        name="moe_combine",
    )(dest0.reshape(steps, 1, tm), dest1.reshape(steps, 1, tm), ys, x2, gates, mod, g_final.reshape(1, d))


def _moe_ffn(x2, mod, g_ffn, w_router, wg, wu, wd, seq, g_final, final_norm):
    t, d = x2.shape
    h, logits = _router(x2, mod, g_ffn, w_router, seq)
    top_val, top_idx = lax.top_k(logits[:, :N_EXPERTS], TOP_K_EXPERTS)
    gates = jax.nn.softmax(top_val, axis=-1)

    n_assign = t * TOP_K_EXPERTS
    flat_e = top_idx.reshape(-1).astype(jnp.int32)
    onehot = (flat_e[:, None] == jnp.arange(N_EXPERTS, dtype=jnp.int32)[None, :]).astype(jnp.int32)
    rank = jnp.take_along_axis(jnp.cumsum(onehot, axis=0), flat_e[:, None], axis=1)[:, 0] - 1
    counts = jnp.sum(onehot, axis=0)
    padded = (counts + EXPERT_ROWS - 1) // EXPERT_ROWS * EXPERT_ROWS
    pad_end = jnp.cumsum(padded)
    dest = (pad_end - padded)[flat_e] + rank
    n_slots = (n_assign // EXPERT_ROWS + N_EXPERTS) * EXPERT_ROWS
    n_blocks = n_slots // EXPERT_ROWS
    slot_tok = jnp.zeros((n_slots,), jnp.int32).at[dest].set(jnp.arange(n_assign, dtype=jnp.int32) // TOP_K_EXPERTS)
    block_start = jnp.arange(n_blocks, dtype=jnp.int32) * EXPERT_ROWS
    block_expert = jnp.minimum(jnp.searchsorted(pad_end, block_start, side="right"), N_EXPERTS - 1).astype(jnp.int32)
    n_used = (pad_end[-1:] // EXPERT_ROWS).astype(jnp.int32)

    ys = _expert_ffn(h, slot_tok, block_expert, n_used, wg, wu, wd)
    dest2 = dest.reshape(t, TOP_K_EXPERTS)
    return _moe_combine(ys, dest2[:, 0], dest2[:, 1], gates, x2, mod, seq, g_final, final_norm)


def _token_mixer(x2, mod, g_mix, w_in, g_heads, w_out, batch, seq):
    pa, pb, pc, pd, p_idx, p_w, *pc_views = _in_projection(x2, mod, g_mix, _pack_w_in(w_in), seq)
    shape3 = lambda a: a.reshape(batch, -1, a.shape[-1])
    gh = g_heads.reshape(4, HEADS_PER_MIXER, HEAD_DIM)
    o_a = _stick_breaking_mixer(shape3(pa), gh[0])
    o_b = _dsa_mixer(shape3(pb), shape3(p_idx), shape3(p_w), gh[1])
    dilated = _dilated_branches([shape3(v) for v in [pc] + pc_views])
    o_d = _moba_mixer(shape3(pd), gh[3])
    return _out_projection(o_a, o_b, o_d, dilated, gh[2], w_out.astype(BF16), x2, mod, seq)


def kernel(x, c, w_ada, b_ada, g_mix, w_in, g_heads, w_out, g_ffn, w_ff_gate, w_ff_up, w_ff_down, w_router, w_exp_gate, w_exp_up, w_exp_down, g_final):
    batch, seq, d = x.shape
    depth = w_ada.shape[0]
    mods = _ada_modulation(c, w_ada, b_ada)
    x2 = x.reshape(batch * seq, d)
    for layer in range(depth):
        mod = mods[layer]
        x2 = _token_mixer(x2, mod, g_mix[layer], w_in[layer], g_heads[layer], w_out[layer], batch, seq)
        i = layer // 2
        last = layer == depth - 1
        if layer % 2 == 0:
            x2 = _dense_ffn(x2, mod, g_ffn[layer], w_ff_gate[i], w_ff_up[i], w_ff_down[i], seq, g_final, last)
        else:
            x2 = _moe_ffn(x2, mod, g_ffn[layer], w_router[i], w_exp_gate[i], w_exp_up[i], w_exp_down[i], seq,
                          g_final, last)
    return x2.reshape(batch, seq, d)
```

```python
import functools

import numpy as np
import jax
import jax.numpy as jnp
from jax import lax
from jax.experimental import pallas as pl
from jax.experimental.pallas import tpu as pltpu

F32 = jnp.float32
BF16 = jnp.bfloat16

HEAD_DIM = 64
HEADS_PER_MIXER = 4
MIXER_WIDTH = HEADS_PER_MIXER * HEAD_DIM
QKV_WIDTH = 3 * MIXER_WIDTH
IDX_HEADS = 8
IDX_DIM = 64
DSA_TOPK = 256
DILATIONS = (1, 4, 16)
DILATED_STEPS = 128
MOBA_BLOCK = 256
MOBA_TOPK = 3
MOBA_GROUP = 2
N_EXPERTS = 8
TOP_K_EXPERTS = 2
NORM_EPS = 1e-6
ATTN_SCALE = HEAD_DIM ** -0.5

LANES = 128
SUBLANES = 8
Q_TILE = 128
ROW_TILE = 512
FFN_TILE = 1408
EXPERT_ROWS = 512
EXPERT_FF_TILE = 1792
GATHER_ROWS = 512
ROWS_PER_TRIP = 8
VMEM_LIMIT = 56 * 1024 * 1024
MASKED = -1e30
SB_UNDERFLOW = 104.0
SB_FIRST_BLOCKS = 3

_NT = (((1,), (1,)), ((), ()))
_TN = (((0,), (0,)), ((), ()))


def _alibi_slopes(mixer_pos):
    idx = np.arange(HEADS_PER_MIXER, dtype=np.float32) * 3 + (mixer_pos + 1)
    return tuple(float(s) for s in np.exp2(-8.0 * idx / 12.0).astype(np.float32))


def _params(*semantics):
    return pltpu.CompilerParams(dimension_semantics=semantics, vmem_limit_bytes=VMEM_LIMIT)


def _modulated_norm(x, gain, shift, scale):
    y = x * lax.rsqrt(jnp.mean(x * x, axis=-1, keepdims=True) + NORM_EPS) * gain
    return y * (1.0 + scale) + shift


def _head_norm(acc, gain):
    return acc * lax.rsqrt(jnp.mean(acc * acc, axis=-1, keepdims=True) + NORM_EPS) * gain


def _head_cols(h):
    return slice(h * HEAD_DIM, (h + 1) * HEAD_DIM)


def _paired_loop(n, body, carry):
    carry = lax.fori_loop(0, n // 2, lambda jj, c: body(2 * jj + 1, body(2 * jj, c, 0), 1), carry)
    return lax.cond(n % 2 == 1, lambda c: body(n - 1, c, 0), lambda c: c, carry)


def _ada_kernel(c_ref, w_ref, b_ref, o_ref):
    c = c_ref[...]
    cond = c / (1.0 + jnp.exp(-c))
    o_ref[0, 0] = jnp.dot(cond, w_ref[0], preferred_element_type=F32,
                          precision=lax.Precision.HIGHEST) + b_ref[0, 0]


def _ada_modulation(c, w_ada, b_ada):
    depth, d, _ = w_ada.shape
    b = c.shape[0]
    out = pl.pallas_call(
        _ada_kernel,
        grid=(depth, 6),
        in_specs=[pl.BlockSpec((b, d), lambda l, k: (0, 0)),
                  pl.BlockSpec((1, d, d), lambda l, k: (l, 0, k)),
                  pl.BlockSpec((1, 1, 1, d), lambda l, k: (l, k, 0, 0))],
        out_specs=pl.BlockSpec((1, 1, b, d), lambda l, k: (l, k, 0, 0)),
        out_shape=jax.ShapeDtypeStruct((depth, 6, b, d), F32),
        compiler_params=_params("arbitrary", "arbitrary"),
        name="ada_modulation",
    )(c, w_ada, b_ada.reshape(depth, 6, 1, d))
    return out.transpose(0, 2, 1, 3)


IN_WIDTHS = (QKV_WIDTH, QKV_WIDTH, QKV_WIDTH, QKV_WIDTH, IDX_HEADS * IDX_DIM + LANES, LANES)


def _in_proj_kernel(x_ref, mod_ref, g_ref, w_ref, oa, ob, oc, od, oidx, ow, oc4, oc16, pc_ref):
    h = _modulated_norm(x_ref[...], g_ref[...], mod_ref[0, 0:1, :], mod_ref[0, 1:2, :]).astype(BF16)
    tm = x_ref.shape[0]
    off = 0
    for o_ref, width in zip((oa, ob, oc, od, oidx, ow), IN_WIDTHS):
        val = jnp.dot(h, w_ref[:, off:off + width], preferred_element_type=F32)
        o_ref[...] = val.astype(o_ref.dtype)
        if o_ref is oc:
            for s in range(QKV_WIDTH // LANES):
                pc_ref[s * tm:(s + 1) * tm, :] = val[:, s * LANES:(s + 1) * LANES]
        off += width
    for r, view in zip(DILATIONS[1:], (oc4, oc16)):
        for c in range(r):
            for s in range(QKV_WIDTH // LANES):
                col = c * QKV_WIDTH + s * LANES
                view[:, col:col + LANES] = pc_ref[pl.ds(s * tm + c, tm // r, stride=r), :].astype(view.dtype)


def _pack_w_in(w_in):
    d = w_in.shape[0]
    n_qkv = 4 * QKV_WIDTH
    n_qi = IDX_HEADS * IDX_DIM
    z = lambda n: jnp.zeros((d, n), w_in.dtype)
    return jnp.concatenate([w_in[:, :n_qkv + n_qi + IDX_DIM], z(LANES - IDX_DIM),
                            w_in[:, n_qkv + n_qi + IDX_DIM:], z(LANES - IDX_HEADS)], axis=1).astype(BF16)


def _in_projection(x2, mod, g_mix, w_packed, seq):
    t, d = x2.shape
    tm = min(ROW_TILE, seq)
    per_batch = seq // tm
    dtypes = (BF16, BF16, BF16, BF16, BF16, F32)
    views = DILATIONS[1:]
    return pl.pallas_call(
        _in_proj_kernel,
        grid=(t // tm,),
        in_specs=[pl.BlockSpec((tm, d), lambda i: (i, 0)),
                  pl.BlockSpec((1, 6, d), lambda i: (i // per_batch, 0, 0)),
                  pl.BlockSpec((1, d), lambda i: (0, 0)),
                  pl.BlockSpec(w_packed.shape, lambda i: (0, 0))],
        out_specs=([pl.BlockSpec((tm, w), lambda i: (i, 0)) for w in IN_WIDTHS]
                   + [pl.BlockSpec((tm // r, r * QKV_WIDTH), lambda i: (i, 0)) for r in views]),
        out_shape=([jax.ShapeDtypeStruct((t, w), dt) for w, dt in zip(IN_WIDTHS, dtypes)]
                   + [jax.ShapeDtypeStruct((t // r, r * QKV_WIDTH), BF16) for r in views]),
        scratch_shapes=[pltpu.VMEM((tm * (QKV_WIDTH // LANES), LANES), F32)],
        compiler_params=_params("arbitrary"),
        name="in_projection",
    )(x2, mod, g_mix.reshape(1, d), w_packed)


def _sb_kernel(q_ref, k_ref, v_ref, g_ref, o_ref):
    tq = q_ref.shape[1]
    i = pl.program_id(1)
    key_row = lax.broadcasted_iota(jnp.int32, (tq, 1), 0)
    q_lane = lax.broadcasted_iota(jnp.int32, (1, tq), 1)
    later = (lax.broadcasted_iota(jnp.int32, (tq, tq), 1) > lax.broadcasted_iota(jnp.int32, (tq, tq), 0))
    later = jnp.where(later, 1.0, 0.0).astype(BF16)

    qs = [q_ref[0, :, _head_cols(h)] * ATTN_SCALE for h in range(HEADS_PER_MIXER)]

    heads = range(HEADS_PER_MIXER)

    def blocks(starts, keeps, tails):
        z = jnp.concatenate([lax.dot_general(k_ref[0, pl.ds(st, tq), _head_cols(h)], qs[h], _NT,
                                             preferred_element_type=F32) for st in starts for h in heads], axis=1)
        softplus = jnp.maximum(z, 0.0) + jnp.log(1.0 + jnp.exp(-jnp.abs(z)))
        keep = None
        if any(kp is not None for kp in keeps):
            ones = jnp.ones((tq, tq), F32)
            keep = jnp.concatenate([ones if kp is None else kp for kp in keeps for _ in heads], axis=1)
        log_1m = -softplus if keep is None else -softplus * keep
        hi = log_1m.astype(BF16)
        lo = (log_1m - hi.astype(F32)).astype(BF16)
        inside = jnp.dot(later, hi, preferred_element_type=F32) + jnp.dot(later, lo, preferred_element_type=F32)
        block_sum = jnp.sum(log_1m, axis=0, keepdims=True)
        piece = lambda x, b, h: x[:, (b * HEADS_PER_MIXER + h) * tq:(b * HEADS_PER_MIXER + h + 1) * tq]
        tail_rows = []
        tails = list(tails)
        for b in range(len(starts)):
            tail_rows += tails
            tails = [tails[h] + piece(block_sum, b, h) for h in heads]
        a = jnp.exp(z - softplus + inside + jnp.concatenate(tail_rows, axis=1))
        if keep is not None:
            a = a * keep
        a = a.astype(BF16)
        av = [sum(lax.dot_general(v_ref[0, pl.ds(st, tq), _head_cols(h)], piece(a, b, h), _TN,
                                  preferred_element_type=F32) for b, st in enumerate(starts)) for h in heads]
        return tails, av

    starts = [pl.multiple_of(jnp.maximum(i - n, 0) * tq, tq) for n in range(SB_FIRST_BLOCKS)]
    keeps = [jnp.where(key_row < q_lane, 1.0, 0.0)]
    keeps += [jnp.full((tq, tq), jnp.where(i >= n, 1.0, 0.0), F32) for n in range(1, SB_FIRST_BLOCKS)]
    tails, accs = blocks(starts, keeps, [jnp.zeros((1, tq), F32)] * HEADS_PER_MIXER)
    state = tuple(zip(tails, accs))

    def body(carry):
        j, state = carry
        tails, av = blocks([pl.multiple_of(j * tq, tq)], [None], [tail for tail, _ in state])
        return j - 1, tuple((tails[h], state[h][1] + av[h]) for h in heads)

    def cond(carry):
        j, state = carry
        worst = functools.reduce(jnp.maximum, [tail for tail, _ in state])
        return jnp.logical_and(j >= 0, jnp.max(worst) > -SB_UNDERFLOW)

    _, state = lax.while_loop(cond, body, (i - SB_FIRST_BLOCKS, state))
    outs = []
    for h in range(HEADS_PER_MIXER):
        o = state[h][1]
        outs.append(o * lax.rsqrt(jnp.mean(o * o, axis=0, keepdims=True) + NORM_EPS) * g_ref[:, h:h + 1])
    o_ref[0] = jnp.concatenate(outs, axis=0).T.astype(o_ref.dtype)


def _stick_breaking_mixer(p, g_heads):
    b, seq, _ = p.shape
    tq = min(Q_TILE, seq)
    return pl.pallas_call(
        _sb_kernel,
        grid=(b, seq // tq),
        in_specs=[pl.BlockSpec((1, tq, MIXER_WIDTH), lambda bi, i: (bi, i, 0)),
                  pl.BlockSpec((1, seq, MIXER_WIDTH), lambda bi, i: (bi, 0, 1)),
                  pl.BlockSpec((1, seq, MIXER_WIDTH), lambda bi, i: (bi, 0, 2)),
                  pl.BlockSpec((HEAD_DIM, HEADS_PER_MIXER), lambda bi, i: (0, 0))],
        out_specs=pl.BlockSpec((1, tq, MIXER_WIDTH), lambda bi, i: (bi, i, 0)),
        out_shape=jax.ShapeDtypeStruct((b, seq, MIXER_WIDTH), BF16),
        compiler_params=_params("arbitrary", "arbitrary"),
        name="stick_breaking",
    )(p, p, p, g_heads.T)


def _dsa_kernel(q_ref, k_ref, v_ref, qi_ref, ki_ref, w_ref, g_ref, o_ref, sc_ref, sc16_ref, s_ref, p_ref, *, slopes):
    tq = q_ref.shape[1]
    tk = sc_ref.shape[1]
    i = pl.program_id(1)
    n_blocks = ((i + 1) * tq + tk - 1) // tk
    qpos = i * tq + lax.broadcasted_iota(jnp.int32, (1, tq), 1)
    key_row = lax.broadcasted_iota(jnp.int32, (tk, 1), 0)
    neg_inf = float("-inf")

    w_t = w_ref[0].T * (IDX_HEADS ** -0.5 * IDX_DIM ** -0.5)
    w_rows = [w_t[h:h + 1, :] for h in range(IDX_HEADS)]
    qi = qi_ref[0]

    def top_half(x):
        bits = lax.bitcast_convert_type(x, jnp.int32) & jnp.int32(-65536)
        return lax.bitcast_convert_type(bits, F32).astype(BF16)

    def score_body(j, carry, slot):
        start = pl.multiple_of(j * tk, tk)
        ki = ki_ref[0, pl.ds(start, tk), 0:IDX_DIM]
        sc = jnp.zeros((tk, tq), F32)
        for h in range(IDX_HEADS):
            x = lax.dot_general(ki, qi[:, h * IDX_DIM:(h + 1) * IDX_DIM], _NT, preferred_element_type=F32)
            sc = sc + w_rows[h] * jnp.maximum(x, 0.0)
        sc = jnp.where((start + key_row) <= qpos, sc + 0.0, neg_inf)
        sc_ref[j] = sc
        sc16_ref[j] = top_half(sc)
        return carry

    _paired_loop(n_blocks, score_body, 0)

    def count(pred):
        def add_block(j, cnt, slot):
            c = jnp.where(pred(sc_ref[j]), 1.0, 0.0)
            return cnt + jnp.sum(c.reshape(tk // 8, 8, tq), axis=0)

        cnt = _paired_loop(n_blocks, add_block, jnp.zeros((8, tq), F32))
        return jnp.sum(cnt, axis=0, keepdims=True)

    int_min = jnp.int32(-2 ** 31)

    def ordered_to_float(u):
        key = u ^ int_min
        bits = jnp.where(key >= 0, key, key ^ jnp.int32(0x7FFFFFFF))
        return lax.bitcast_convert_type(bits, F32)

    def count_top_half(cand16):
        def add_block(j, cnt, slot):
            c = jnp.where(sc16_ref[j] >= cand16, jnp.ones((), BF16), jnp.zeros((), BF16))
            for g in range(tk // 16):
                cnt = cnt + c[g * 16:(g + 1) * 16]
            return cnt

        cnt = _paired_loop(n_blocks, add_block, jnp.zeros((16, tq), BF16))
        return jnp.sum(cnt.astype(F32), axis=0, keepdims=True)

    def search_body(step, carry, half):
        prefix, n_at_prefix = carry
        cand = prefix | jnp.left_shift(jnp.int32(1), 31 - step)
        cand_f = ordered_to_float(cand)
        n_ge = count_top_half(top_half(cand_f)) if half else count(lambda s: s >= cand_f)
        keep = n_ge >= DSA_TOPK
        return jnp.where(keep, cand, prefix), jnp.where(keep, n_ge, n_at_prefix)

    keep_all = (qpos + 1) <= DSA_TOPK

    carry = (jnp.zeros((1, tq), jnp.int32), jnp.zeros((1, tq), F32))
    carry = lax.fori_loop(0, 16, functools.partial(search_body, half=True), carry)
    prefix, n_ge_tau = lax.fori_loop(16, 32, functools.partial(search_body, half=False), carry)
    tau = jnp.where(keep_all, float(np.finfo(np.float32).min), ordered_to_float(prefix))
    no_ties = jnp.min(jnp.where(jnp.logical_or(keep_all, n_ge_tau == DSA_TOPK), 1.0, 0.0)) > 0.5

    earlier = (lax.broadcasted_iota(jnp.int32, (tk, tk), 1) < lax.broadcasted_iota(jnp.int32, (tk, tk), 0))
    earlier = jnp.where(earlier, 1.0, 0.0).astype(BF16)
    qs = [q_ref[0, :, _head_cols(h)] * ATTN_SCALE for h in range(HEADS_PER_MIXER)]
    key_bias = [slopes[h] * lax.broadcasted_iota(jnp.int32, (tk, tq), 0).astype(F32)
                for h in range(HEADS_PER_MIXER)]

    def attn_body(j, carry, slot, n_ties=None):
        ties_seen, stats = carry
        s_slot = [s_ref.at[slot * HEADS_PER_MIXER + h] for h in range(HEADS_PER_MIXER)]
        p_slot = [p_ref.at[slot * HEADS_PER_MIXER + h] for h in range(HEADS_PER_MIXER)]
        start = pl.multiple_of(j * tk, tk)
        sc = sc_ref[j]
        if n_ties is None:
            sel = sc >= tau
        else:
            tie = jnp.where(sc == tau, 1.0, 0.0)
            rank = jnp.dot(earlier, tie.astype(BF16), preferred_element_type=F32) + ties_seen
            sel = jnp.where(sc > tau, 1.0, jnp.where(rank < n_ties, tie, 0.0)) > 0.5
            ties_seen = ties_seen + jnp.sum(tie, axis=0, keepdims=True)
        block_pos = (j * tk - i * tq).astype(F32)
        for h in range(HEADS_PER_MIXER):
            k = k_ref[0, pl.ds(start, tk), _head_cols(h)]
            s = lax.dot_general(k, qs[h], _NT, preferred_element_type=F32) + key_bias[h]
            s_slot[h][...] = jnp.where(sel, s, MASKED)
        locals_ = []
        for h in range(HEADS_PER_MIXER):
            s = s_slot[h][...]
            m_loc = jnp.max(s, axis=0, keepdims=True)
            p = jnp.exp(s - m_loc)
            locals_.append((m_loc, jnp.sum(p, axis=0, keepdims=True)))
            p_slot[h][...] = p.astype(BF16)
        new_stats = []
        for h in range(HEADS_PER_MIXER):
            m, l, acc = stats[h]
            m_loc, l_blk = locals_[h]
            v = v_ref[0, pl.ds(start, tk), _head_cols(h)]
            pv_blk = lax.dot_general(v, p_slot[h][...], _TN, preferred_element_type=F32)
            m_blk = m_loc + slopes[h] * block_pos
            m_new = jnp.maximum(m, m_blk)
            w_old = jnp.exp(m - m_new)
            w_blk = jnp.where(m_loc > 0.5 * MASKED, jnp.exp(m_blk - m_new), 0.0)
            new_stats.append((m_new, w_old * l + w_blk * l_blk, w_old * acc + w_blk * pv_blk))
        return ties_seen, tuple(new_stats)

    init = tuple((jnp.full((1, tq), MASKED, F32), jnp.zeros((1, tq), F32), jnp.zeros((HEAD_DIM, tq), F32))
                 for _ in range(HEADS_PER_MIXER))
    start_carry = (jnp.zeros((1, tq), F32), init)

    def attend_without_ties():
        return _paired_loop(n_blocks, attn_body, start_carry)[1]

    def attend_with_ties():
        n_gt = count(lambda s: s > tau)
        n_ties = jnp.where(keep_all, 1e9, DSA_TOPK - n_gt)
        return _paired_loop(n_blocks, functools.partial(attn_body, n_ties=n_ties), start_carry)[1]

    stats = lax.cond(no_ties, attend_without_ties, attend_with_ties)
    outs = []
    for h in range(HEADS_PER_MIXER):
        _, l, acc = stats[h]
        o = acc / l
        outs.append(o * lax.rsqrt(jnp.mean(o * o, axis=0, keepdims=True) + NORM_EPS) * g_ref[:, h:h + 1])
    o_ref[0] = jnp.concatenate(outs, axis=0).T.astype(o_ref.dtype)


def _dsa_mixer(p, p_idx, p_w, g_heads):
    b, seq, _ = p.shape
    tq = min(4 * Q_TILE, seq)
    tk = tq
    assert seq // 16 <= 256, "the packed-bf16 count accumulators are exact only up to 256 adds"
    qi_width = IDX_HEADS * IDX_DIM
    return pl.pallas_call(
        functools.partial(_dsa_kernel, slopes=_alibi_slopes(0)),
        grid=(b, seq // tq),
        in_specs=[pl.BlockSpec((1, tq, MIXER_WIDTH), lambda bi, i: (bi, i, 0)),
                  pl.BlockSpec((1, seq, MIXER_WIDTH), lambda bi, i: (bi, 0, 1)),
                  pl.BlockSpec((1, seq, MIXER_WIDTH), lambda bi, i: (bi, 0, 2)),
                  pl.BlockSpec((1, tq, qi_width), lambda bi, i: (bi, i, 0)),
                  pl.BlockSpec((1, seq, LANES), lambda bi, i: (bi, 0, qi_width // LANES)),
                  pl.BlockSpec((1, tq, LANES), lambda bi, i: (bi, i, 0)),
                  pl.BlockSpec((HEAD_DIM, HEADS_PER_MIXER), lambda bi, i: (0, 0))],
        out_specs=pl.BlockSpec((1, tq, MIXER_WIDTH), lambda bi, i: (bi, i, 0)),
        out_shape=jax.ShapeDtypeStruct((b, seq, MIXER_WIDTH), BF16),
        scratch_shapes=[pltpu.VMEM((seq // tk, tk, tq), F32),
                        pltpu.VMEM((seq // tk, tk, tq), BF16),
                        pltpu.VMEM((2 * HEADS_PER_MIXER, tk, tq), F32),
                        pltpu.VMEM((2 * HEADS_PER_MIXER, tk, tq), BF16)],
        compiler_params=_params("arbitrary", "arbitrary"),
        name="dsa",
    )(p, p, p, p_idx, p_idx, p_w, g_heads.T)


def _band_kernel(q_ref, kp_ref, kc_ref, vp_ref, vc_ref, o_ref, lse_ref, *, dilation, slopes):
    tq = q_ref.shape[1]
    ui = pl.program_id(2)
    heads = range(HEADS_PER_MIXER)
    u_q = jnp.concatenate([ui * tq + lax.broadcasted_iota(jnp.int32, (1, tq), 1)] * HEADS_PER_MIXER, axis=1)
    u_k = ui * tq - DILATED_STEPS + lax.broadcasted_iota(jnp.int32, (DILATED_STEPS + tq, 1), 0)
    steps = u_q - u_k
    valid = jnp.logical_and(jnp.logical_and(steps >= 0, steps <= DILATED_STEPS), u_k >= 0)
    slope_row = jnp.concatenate([jnp.full((1, tq), slopes[h], F32) for h in heads], axis=1)
    s = jnp.concatenate(
        [lax.dot_general(jnp.concatenate([kp_ref[0, :, _head_cols(h)], kc_ref[0, :, _head_cols(h)]], axis=0),
                         q_ref[0, :, _head_cols(h)], _NT, preferred_element_type=F32) for h in heads], axis=1)
    s = jnp.where(valid, s * ATTN_SCALE - slope_row * (steps * dilation).astype(F32), MASKED)
    m = jnp.max(s, axis=0, keepdims=True)
    e = jnp.exp(s - m)
    den = jnp.sum(e, axis=0, keepdims=True)
    p = e.astype(BF16)
    lse = m + jnp.log(den)
    outs = []
    for h in heads:
        cols = slice(h * tq, (h + 1) * tq)
        v = jnp.concatenate([vp_ref[0, :, _head_cols(h)], vc_ref[0, :, _head_cols(h)]], axis=0)
        outs.append(lax.dot_general(v, p[:, cols], _TN, preferred_element_type=F32) / den[:, cols])
    o_ref[0] = jnp.concatenate(outs, axis=0).T
    lse_rows = jnp.concatenate([lse[:, h * tq:(h + 1) * tq] for h in heads]
                               + [jnp.zeros((LANES - HEADS_PER_MIXER, tq), F32)], axis=0)
    lse_ref[0] = lse_rows.T


def _dilated_branch(view, dilation, slopes):
    b, length, _ = view.shape
    classes = dilation
    tq = min(2 * Q_TILE, length)
    back = tq // DILATED_STEPS
    spec = lambda part, prev: (
        pl.BlockSpec((1, DILATED_STEPS, MIXER_WIDTH),
                     lambda bi, c, ui: (bi, jnp.maximum(ui * back - 1, 0), c * 3 + part)) if prev
        else pl.BlockSpec((1, tq, MIXER_WIDTH), lambda bi, c, ui: (bi, ui, c * 3 + part)))
    out, lse = pl.pallas_call(
        functools.partial(_band_kernel, dilation=dilation, slopes=slopes),
        grid=(b, classes, length // tq),
        in_specs=[spec(0, False), spec(1, True), spec(1, False), spec(2, True), spec(2, False)],
        out_specs=[pl.BlockSpec((1, tq, MIXER_WIDTH), lambda bi, c, ui: (bi, ui, c)),
                   pl.BlockSpec((1, tq, LANES), lambda bi, c, ui: (bi, ui, c))],
        out_shape=[jax.ShapeDtypeStruct((b, length, classes * MIXER_WIDTH), F32),
                   jax.ShapeDtypeStruct((b, length, classes * LANES), F32)],
        compiler_params=_params("arbitrary", "arbitrary", "arbitrary"),
        name=f"dilated_r{dilation}",
    )(view, view, view, view, view)
    return out, lse


def _merge_dilated(o_refs, l_refs, g_ref, scratch, rows):
    chunks = MIXER_WIDTH // LANES
    outs = [[o_refs[0][:, s * LANES:(s + 1) * LANES] for s in range(chunks)]]
    lses = [l_refs[0][...]]
    for r, o_view, l_view, o_nat, l_nat in zip(DILATIONS[1:], o_refs[1:], l_refs[1:], scratch[0::2], scratch[1::2]):
        for c in range(r):
            for s in range(chunks):
                col = c * MIXER_WIDTH + s * LANES
                o_nat[pl.ds(s * rows + c, rows // r, stride=r), :] = o_view[:, col:col + LANES]
            l_nat[pl.ds(c, rows // r, stride=r), :] = l_view[:, c * LANES:(c + 1) * LANES]
        outs.append([o_nat[s * rows:(s + 1) * rows, :] for s in range(chunks)])
        lses.append(l_nat[...])
    top = functools.reduce(jnp.maximum, lses)
    wts = [jnp.exp(l - top) for l in lses]
    def head_of(index):
        return sum(jnp.where(index >= h * HEAD_DIM, 1, 0) for h in range(1, HEADS_PER_MIXER))

    head_of_col = head_of(lax.broadcasted_iota(jnp.int32, (LANES, MIXER_WIDTH), 1))
    spread = jnp.where(lax.broadcasted_iota(jnp.int32, (LANES, MIXER_WIDTH), 0) == head_of_col, 1.0, 0.0)
    same_head = (head_of(lax.broadcasted_iota(jnp.int32, (MIXER_WIDTH, MIXER_WIDTH), 0))
                 == head_of(lax.broadcasted_iota(jnp.int32, (MIXER_WIDTH, MIXER_WIDTH), 1)))
    head_mean = jnp.where(same_head, 1.0 / HEAD_DIM, 0.0)

    def times(x, m):
        hi = x.astype(BF16)
        lo = (x - hi.astype(F32)).astype(BF16)
        m = m.astype(BF16)
        return jnp.dot(hi, m, preferred_element_type=F32) + jnp.dot(lo, m, preferred_element_type=F32)

    w_cols = [times(w, spread) for w in wts]
    full = [jnp.concatenate(o, axis=1) for o in outs]
    mixed = sum(w * o for w, o in zip(w_cols, full)) / sum(w_cols)
    return mixed * lax.rsqrt(times(mixed * mixed, head_mean) + NORM_EPS) * g_ref[...]


def _dilated_branches(views):
    slopes = _alibi_slopes(1)
    branches = [_dilated_branch(v, r, slopes) for v, r in zip(views, DILATIONS)]
    flat = lambda a: a.reshape(-1, a.shape[-1])
    return [flat(o) for o, _ in branches], [flat(l) for _, l in branches]


def _moba_kernel(q_ref, k_ref, v_ref, g_ref, o_ref, kmean_ref, chosen_ref, s_ref, p_ref, *, slopes):
    tq = q_ref.shape[1]
    n_kv = k_ref.shape[1] // MOBA_BLOCK
    own = pl.program_id(1)

    @pl.when(own == 0)
    def _():
        kmean_ref[...] = jnp.zeros_like(kmean_ref)
        for n in range(n_kv):
            blk = k_ref[0, n * MOBA_BLOCK:(n + 1) * MOBA_BLOCK, :].astype(F32)
            kmean_ref[n:n + 1, :] = jnp.sum(blk, axis=0, keepdims=True) * (1.0 / MOBA_BLOCK)

    key_row = lax.broadcasted_iota(jnp.int32, (MOBA_BLOCK, 1), 0)
    q_lane = lax.broadcasted_iota(jnp.int32, (1, tq), 1)
    blk_rows = kmean_ref.shape[0]
    blk_row = lax.broadcasted_iota(jnp.int32, (blk_rows, 1), 0)
    blk_f = blk_row.astype(F32)
    neg_inf = float("-inf")
    group = tq // MOBA_BLOCK
    first_own = own * group
    sub_block = sum(jnp.where(q_lane >= g * MOBA_BLOCK, 1, 0) for g in range(1, group)) if group > 1 else 0
    own_blk = first_own + sub_block
    q_local = q_lane - sub_block * MOBA_BLOCK

    qs = []
    for h in range(HEADS_PER_MIXER):
        q = q_ref[0, :, _head_cols(h)]
        gate = lax.dot_general(kmean_ref[:, _head_cols(h)], q.astype(F32), _NT, preferred_element_type=F32,
                               precision=lax.Precision.HIGHEST)
        gate = jnp.where(blk_row < own_blk, gate, neg_inf)
        picks = jnp.zeros((blk_rows, tq), F32)
        for _ in range(MOBA_TOPK):
            top = jnp.max(gate, axis=0, keepdims=True)
            is_top = jnp.logical_and(gate == top, top > neg_inf)
            first = jnp.min(jnp.where(is_top, blk_f, float(blk_rows)), axis=0, keepdims=True)
            pick = blk_f == first
            picks = jnp.where(pick, 1.0, picks)
            gate = jnp.where(pick, neg_inf, gate)
        chosen_ref[h] = picks
        qs.append(q * ATTN_SCALE)

    key_bias = [slopes[h] * lax.broadcasted_iota(jnp.int32, (MOBA_BLOCK, tq), 0).astype(F32)
                for h in range(HEADS_PER_MIXER)]

    def block_softmax(start, keep=None, slot=0):
        s_slot = [s_ref.at[slot * HEADS_PER_MIXER + h] for h in range(HEADS_PER_MIXER)]
        p_slot = [p_ref.at[slot * HEADS_PER_MIXER + h] for h in range(HEADS_PER_MIXER)]
        for h in range(HEADS_PER_MIXER):
            k = k_ref[0, pl.ds(start, MOBA_BLOCK), _head_cols(h)]
            s = lax.dot_general(k, qs[h], _NT, preferred_element_type=F32) + key_bias[h]
            s_slot[h][...] = s if keep is None else jnp.where(keep, s, MASKED)
        pieces = []
        for h in range(HEADS_PER_MIXER):
            s = s_slot[h][...]
            m_loc = jnp.max(s, axis=0, keepdims=True)
            p = jnp.exp(s - m_loc)
            pieces.append((m_loc, jnp.sum(p, axis=0, keepdims=True)))
            p_slot[h][...] = p.astype(BF16)
        out = []
        for h in range(HEADS_PER_MIXER):
            v = v_ref[0, pl.ds(start, MOBA_BLOCK), _head_cols(h)]
            out.append(pieces[h] + (lax.dot_general(v, p_slot[h][...], _TN, preferred_element_type=F32),))
        return out

    def merge(stats, j, pieces, use):
        block_pos = ((j - first_own) * MOBA_BLOCK).astype(F32)
        merged = []
        for h, (m_loc, l_blk, pv_blk) in enumerate(pieces):
            m_blk = jnp.where(use[h], m_loc + slopes[h] * block_pos, MASKED)
            m, l, acc = stats[h]
            m_new = jnp.maximum(m, m_blk)
            w_old = jnp.exp(m - m_new)
            w_blk = jnp.where(use[h], jnp.exp(m_blk - m_new), 0.0)
            merged.append((m_new, w_old * l + w_blk * l_blk, w_old * acc + w_blk * pv_blk))
        return tuple(merged)

    def picked(h, j):
        return chosen_ref[h, pl.ds(j, 1), :] > 0.5

    def body(j, stats, slot):
        start = pl.multiple_of(j * MOBA_BLOCK, MOBA_BLOCK)
        return merge(stats, j, block_softmax(start, slot=slot), [picked(h, j) for h in range(HEADS_PER_MIXER)])

    stats = tuple((jnp.full((1, tq), MASKED, F32), jnp.zeros((1, tq), F32), jnp.zeros((HEAD_DIM, tq), F32))
                  for _ in range(HEADS_PER_MIXER))
    stats = _paired_loop(first_own, body, stats)
    for g in range(group):
        j = first_own + g
        keep = jnp.logical_or(own_blk > j, jnp.logical_and(own_blk == j, key_row <= q_local))
        pieces = block_softmax(pl.multiple_of(j * MOBA_BLOCK, MOBA_BLOCK), keep=keep, slot=g % 2)
        use = [jnp.logical_or(own_blk == j, jnp.logical_and(own_blk > j, picked(h, j)))
               for h in range(HEADS_PER_MIXER)]
        stats = merge(stats, j, pieces, use)
    outs = []
    for h in range(HEADS_PER_MIXER):
        _, l, acc = stats[h]
        o = acc / l
        outs.append(o * lax.rsqrt(jnp.mean(o * o, axis=0, keepdims=True) + NORM_EPS) * g_ref[:, h:h + 1])
    o_ref[0] = jnp.concatenate(outs, axis=0).T.astype(o_ref.dtype)


def _moba_mixer(p, g_heads):
    b, seq, _ = p.shape
    tq = MOBA_GROUP * MOBA_BLOCK if seq % (MOBA_GROUP * MOBA_BLOCK) == 0 else MOBA_BLOCK
    blk_rows =-(-(seq // MOBA_BLOCK) // SUBLANES) * SUBLANES
    return pl.pallas_call(
        functools.partial(_moba_kernel, slopes=_alibi_slopes(2)),
        grid=(b, seq // tq),
        in_specs=[pl.BlockSpec((1, tq, MIXER_WIDTH), lambda bi, i: (bi, i, 0)),
                  pl.BlockSpec((1, seq, MIXER_WIDTH), lambda bi, i: (bi, 0, 1)),
                  pl.BlockSpec((1, seq, MIXER_WIDTH), lambda bi, i: (bi, 0, 2)),
                  pl.BlockSpec((HEAD_DIM, HEADS_PER_MIXER), lambda bi, i: (0, 0))],
        out_specs=pl.BlockSpec((1, tq, MIXER_WIDTH), lambda bi, i: (bi, i, 0)),
        out_shape=jax.ShapeDtypeStruct((b, seq, MIXER_WIDTH), BF16),
        scratch_shapes=[pltpu.VMEM((blk_rows, MIXER_WIDTH), F32),
                        pltpu.VMEM((HEADS_PER_MIXER, blk_rows, tq), F32),
                        pltpu.VMEM((2 * HEADS_PER_MIXER, MOBA_BLOCK, tq), F32),
                        pltpu.VMEM((2 * HEADS_PER_MIXER, MOBA_BLOCK, tq), BF16)],
        compiler_params=_params("arbitrary", "arbitrary"),
        name="moba",
    )(p, p, p, g_heads.T)


def _out_proj_kernel(oa, ob, od, o1, o4, o16, l1, l4, l16, gc_ref, w_ref, x_ref, mod_ref, o_ref, *scratch):
    rows = x_ref.shape[0]
    oc = _merge_dilated((o1, o4, o16), (l1, l4, l16), gc_ref, scratch, rows).astype(BF16)
    acc = jnp.zeros(x_ref.shape, F32)
    for m, o in enumerate((oa[...], ob[...], oc, od[...])):
        acc = acc + jnp.dot(o, w_ref[m * MIXER_WIDTH:(m + 1) * MIXER_WIDTH, :], preferred_element_type=F32)
    o_ref[...] = x_ref[...] + mod_ref[0, 2:3, :] * acc


def _out_projection(o_a, o_b, o_d, dilated, g_heads_c, w_out, x2, mod, seq):
    t, d = x2.shape
    tm = min(ROW_TILE, seq)
    per_batch = seq // tm
    o_spec = pl.BlockSpec((tm, MIXER_WIDTH), lambda i: (i, 0))
    branch_outs, branch_lses = dilated
    view_specs = ([pl.BlockSpec((tm // r, r * MIXER_WIDTH), lambda i: (i, 0)) for r in DILATIONS]
                  + [pl.BlockSpec((tm // r, r * LANES), lambda i: (i, 0)) for r in DILATIONS])
    scratch = []
    for _ in DILATIONS[1:]:
        scratch += [pltpu.VMEM((tm * (MIXER_WIDTH // LANES), LANES), F32), pltpu.VMEM((tm, LANES), F32)]
    return pl.pallas_call(
        _out_proj_kernel,
        grid=(t // tm,),
        in_specs=[o_spec] * 3 + view_specs + [pl.BlockSpec((1, MIXER_WIDTH), lambda i: (0, 0)),
                                              pl.BlockSpec(w_out.shape, lambda i: (0, 0)),
                                              pl.BlockSpec((tm, d), lambda i: (i, 0)),
                                              pl.BlockSpec((1, 6, d), lambda i: (i // per_batch, 0, 0))],
        out_specs=pl.BlockSpec((tm, d), lambda i: (i, 0)),
        out_shape=jax.ShapeDtypeStruct((t, d), F32),
        scratch_shapes=scratch,
        compiler_params=_params("arbitrary"),
        name="out_projection",
    )(*[o.reshape(t, MIXER_WIDTH) for o in (o_a, o_b, o_d)], *branch_outs, *branch_lses,
      g_heads_c.reshape(1, MIXER_WIDTH), w_out, x2, mod)


def _rms_gain(x, gain):
    return x * lax.rsqrt(jnp.mean(x * x, axis=-1, keepdims=True) + NORM_EPS) * gain


def _ffn_kernel(x_ref, mod_ref, g_ref, fg_ref, wg_ref, wu_ref, wd_ref, o_ref, h_ref, acc_ref, *, final_norm):
    f = pl.program_id(1)

    @pl.when(f == 0)
    def _():
        h_ref[...] = _modulated_norm(x_ref[...], g_ref[...], mod_ref[0, 3:4, :], mod_ref[0, 4:5, :]).astype(BF16)
        acc_ref[...] = jnp.zeros_like(acc_ref)

    h = h_ref[...]
    gate = jnp.dot(h, wg_ref[...], preferred_element_type=F32)
    up = jnp.dot(h, wu_ref[...], preferred_element_type=F32)
    act = (gate / (1.0 + jnp.exp(-gate)) * up).astype(BF16)
    acc_ref[...] += jnp.dot(act, wd_ref[...], preferred_element_type=F32)

    @pl.when(f == pl.num_programs(1) - 1)
    def _():
        out = x_ref[...] + mod_ref[0, 5:6, :] * acc_ref[...]
        o_ref[...] = _rms_gain(out, fg_ref[...]) if final_norm else out


def _dense_ffn(x2, mod, g_ffn, wg, wu, wd, seq, g_final, final_norm):
    t, d = x2.shape
    d_ff = wg.shape[1]
    tm = min(ROW_TILE, seq)
    tf = FFN_TILE if d_ff % FFN_TILE == 0 else d_ff
    per_batch = seq // tm
    return pl.pallas_call(
        functools.partial(_ffn_kernel, final_norm=final_norm),
        grid=(t // tm, d_ff // tf),
        in_specs=[pl.BlockSpec((tm, d), lambda i, f: (i, 0)),
                  pl.BlockSpec((1, 6, d), lambda i, f: (i // per_batch, 0, 0)),
                  pl.BlockSpec((1, d), lambda i, f: (0, 0)),
                  pl.BlockSpec((1, d), lambda i, f: (0, 0)),
                  pl.BlockSpec((d, tf), lambda i, f: (0, f)),
                  pl.BlockSpec((d, tf), lambda i, f: (0, f)),
                  pl.BlockSpec((tf, d), lambda i, f: (f, 0))],
        out_specs=pl.BlockSpec((tm, d), lambda i, f: (i, 0)),
        out_shape=jax.ShapeDtypeStruct((t, d), F32),
        scratch_shapes=[pltpu.VMEM((tm, d), BF16), pltpu.VMEM((tm, d), F32)],
        compiler_params=_params("arbitrary", "arbitrary"),
        name="dense_ffn",
    )(x2, mod, g_ffn.reshape(1, d), g_final.reshape(1, d), wg.astype(BF16), wu.astype(BF16), wd.astype(BF16))


def _store_row_tiles(dst_ref, value):
    rows, d = value.shape
    chunks = d // LANES
    for s in range(chunks):
        dst_ref[pl.ds(s, rows, stride=chunks), :] = value[:, s * LANES:(s + 1) * LANES]


def _load_row_tile_chunk(src_ref, s, rows, chunks):
    return src_ref[pl.ds(s, rows, stride=chunks), :]


def _router_kernel(x_ref, mod_ref, g_ref, wr_ref, h_ref, logit_ref):
    h = _modulated_norm(x_ref[...], g_ref[...], mod_ref[0, 3:4, :], mod_ref[0, 4:5, :])
    _store_row_tiles(h_ref, h)
    logit_ref[...] = jnp.dot(h, wr_ref[...], preferred_element_type=F32, precision=lax.Precision.HIGHEST)


def _router(x2, mod, g_ffn, w_router, seq):
    t, d = x2.shape
    tm = min(ROW_TILE, seq)
    per_batch = seq // tm
    wr = jnp.zeros((d, LANES), F32).at[:, :N_EXPERTS].set(w_router.astype(F32))
    return pl.pallas_call(
        _router_kernel,
        grid=(t // tm,),
        in_specs=[pl.BlockSpec((tm, d), lambda i: (i, 0)),
                  pl.BlockSpec((1, 6, d), lambda i: (i // per_batch, 0, 0)),
                  pl.BlockSpec((1, d), lambda i: (0, 0)),
                  pl.BlockSpec((d, LANES), lambda i: (0, 0))],
        out_specs=[pl.BlockSpec((tm * (d // LANES), LANES), lambda i: (i, 0)),
                   pl.BlockSpec((tm, LANES), lambda i: (i, 0))],
        out_shape=[jax.ShapeDtypeStruct((t * (d // LANES), LANES), F32), jax.ShapeDtypeStruct((t, LANES), F32)],
        compiler_params=_params("arbitrary"),
        name="moe_router",
    )(x2, mod, g_ffn.reshape(1, d), wr)


def _row_copy(src_hbm, row, dst_ref, r, sem, chunks):
    src = src_hbm.at[pl.ds(pl.multiple_of(row * chunks, chunks), chunks), :]
    return pltpu.make_async_copy(src, dst_ref.at[pl.ds(pl.multiple_of(r * chunks, chunks), chunks), :], sem)


def _expert_kernel(be_ref, used_ref, cur_idx_ref, nxt_idx_ref, h_hbm, wg_ref, wu_ref, wd_ref, o_ref,
                   xin_ref, xb_ref, acc_ref, sem, *, n_ff_steps):
    m = pl.program_id(0)
    f = pl.program_id(1)
    n_used = used_ref[0]
    live = m < n_used
    rows, d = xb_ref.shape
    chunks = d // LANES

    def row_copies(idx_ref, slot, start):
        if not start:
            half = rows * chunks // 2
            for lane in range(2):
                pltpu.make_async_copy(h_hbm.at[pl.ds(0, half), :], xin_ref.at[slot, pl.ds(0, half), :],
                                      sem.at[slot, lane]).wait()
            return

        def body(group, c):
            for u in range(ROWS_PER_TRIP):
                r = ROWS_PER_TRIP * group + u
                _row_copy(h_hbm, idx_ref[0, 0, r], xin_ref.at[slot], r, sem.at[slot, u % 2],
                          chunks).start(priority=u % 2)
            return c
        lax.fori_loop(0, rows // ROWS_PER_TRIP, body, 0)

    @pl.when(jnp.logical_and(live, f == 0))
    def _():
        @pl.when(m == 0)
        def _():
            row_copies(cur_idx_ref, 0, start=True)

        for slot in range(2):
            @pl.when(m % 2 == slot)
            def _(slot=slot):
                row_copies(cur_idx_ref, slot, start=False)
                for s in range(chunks):
                    xb_ref[:, s * LANES:(s + 1) * LANES] = _load_row_tile_chunk(
                        xin_ref.at[slot], s, rows, chunks).astype(BF16)

        acc_ref[...] = jnp.zeros_like(acc_ref)

    def swiglu_step(start_next):
        h = xb_ref[...]
        if start_next:
            share = rows // n_ff_steps
            nxt = (m + 1) % 2
            for u in range(share):
                r = f * share + u
                _row_copy(h_hbm, nxt_idx_ref[0, 0, r], xin_ref.at[nxt], r, sem.at[nxt, u % 2],
                          chunks).start(priority=u % 2)
        gate = jnp.dot(h, wg_ref[0], preferred_element_type=F32)
        up = jnp.dot(h, wu_ref[0], preferred_element_type=F32)
        act = (gate / (1.0 + jnp.exp(-gate)) * up).astype(BF16)
        acc_ref[...] += jnp.dot(act, wd_ref[0], preferred_element_type=F32)

    next_live = m + 1 < n_used
    pl.when(jnp.logical_and(live, next_live))(functools.partial(swiglu_step, True))
    pl.when(jnp.logical_and(live, jnp.logical_not(next_live)))(functools.partial(swiglu_step, False))

    @pl.when(f == pl.num_programs(1) - 1)
    def _():
        _store_row_tiles(o_ref, jnp.where(live, acc_ref[...], 0.0))


def _expert_ffn(h_rows, slot_tok, block_expert, n_used, wg, wu, wd):
    d = wg.shape[1]
    chunks = d // LANES
    n_slots = slot_tok.shape[0]
    d_ff = wg.shape[2]
    tm = EXPERT_ROWS
    tf = EXPERT_FF_TILE if d_ff % EXPERT_FF_TILE == 0 else d_ff
    nf = d_ff // tf
    nm = n_slots // tm

    def fcol(m, f, used):
        return jnp.where(m < used[0], f, nf - 1)

    idx_spec = lambda shift: pl.BlockSpec((1, 1, tm), lambda m, f, be, used: (jnp.minimum(m + shift, nm - 1), 0, 0),
                                          memory_space=pltpu.SMEM)
    grid_spec = pltpu.PrefetchScalarGridSpec(
        num_scalar_prefetch=2,
        grid=(nm, nf),
        in_specs=[idx_spec(0), idx_spec(1), pl.BlockSpec(memory_space=pl.ANY),
                  pl.BlockSpec((1, d, tf), lambda m, f, be, used: (be[m], 0, fcol(m, f, used))),
                  pl.BlockSpec((1, d, tf), lambda m, f, be, used: (be[m], 0, fcol(m, f, used))),
                  pl.BlockSpec((1, tf, d), lambda m, f, be, used: (be[m], fcol(m, f, used), 0))],
        out_specs=pl.BlockSpec((tm * chunks, LANES), lambda m, f, be, used: (m, 0)),
        scratch_shapes=[pltpu.VMEM((2, tm * chunks, LANES), F32), pltpu.VMEM((tm, d), BF16),
                        pltpu.VMEM((tm, d), F32), pltpu.SemaphoreType.DMA((2, 2))])
    idx = slot_tok.reshape(nm, 1, tm)
    assert tm % (2 * nf) == 0
    return pl.pallas_call(
        functools.partial(_expert_kernel, n_ff_steps=nf),
        grid_spec=grid_spec,
        out_shape=jax.ShapeDtypeStruct((n_slots * chunks, LANES), F32),
        compiler_params=_params("arbitrary", "arbitrary"),
        name="expert_ffn",
    )(block_expert, n_used, idx, idx, h_rows, wg.astype(BF16), wu.astype(BF16), wd.astype(BF16))


def _combine_kernel(d0_ref, d1_ref, ys_hbm, x_ref, gates_ref, mod_ref, fg_ref, o_ref, y0_ref, y1_ref, sem, *,
                    final_norm):
    rows, d = o_ref.shape
    chunks = d // LANES

    def start(group, c):
        for u in range(ROWS_PER_TRIP // 2):
            r = ROWS_PER_TRIP // 2 * group + u
            _row_copy(ys_hbm, d0_ref[0, 0, r], y0_ref, r, sem.at[0], chunks).start(priority=0)
            _row_copy(ys_hbm, d1_ref[0, 0, r], y1_ref, r, sem.at[1], chunks).start(priority=1)
        return c

    lax.fori_loop(0, rows // (ROWS_PER_TRIP // 2), start, 0)
    for k, y_ref in enumerate((y0_ref, y1_ref)):
        pltpu.make_async_copy(ys_hbm.at[pl.ds(0, rows * chunks), :], y_ref, sem.at[k]).wait()
    gates = gates_ref[...]
    for s in range(chunks):
        cols = slice(s * LANES, (s + 1) * LANES)
        y = (_load_row_tile_chunk(y0_ref, s, rows, chunks) * gates[:, 0:1]
             + _load_row_tile_chunk(y1_ref, s, rows, chunks) * gates[:, 1:2])
        o_ref[:, cols] = x_ref[:, cols] + mod_ref[0, 5:6, cols] * y
    if final_norm:
        o_ref[...] = _rms_gain(o_ref[...], fg_ref[...])


def _moe_combine(ys, dest0, dest1, gates, x2, mod, seq, g_final, final_norm):
    t, d = x2.shape
    tm = min(GATHER_ROWS, seq)
    steps = t // tm
    per_batch = seq // tm
    idx_spec = pl.BlockSpec((1, 1, tm), lambda i: (i, 0, 0), memory_space=pltpu.SMEM)
    return pl.pallas_call(
        functools.partial(_combine_kernel, final_norm=final_norm),
        grid=(steps,),
        in_specs=[idx_spec, idx_spec, pl.BlockSpec(memory_space=pl.ANY),
                  pl.BlockSpec((tm, d), lambda i: (i, 0)),
                  pl.BlockSpec((tm, TOP_K_EXPERTS), lambda i: (i, 0)),
                  pl.BlockSpec((1, 6, d), lambda i: (i // per_batch, 0, 0)),
                  pl.BlockSpec((1, d), lambda i: (0, 0))],
        out_specs=pl.BlockSpec((tm, d), lambda i: (i, 0)),
        out_shape=jax.ShapeDtypeStruct((t, d), F32),
        scratch_shapes=[pltpu.VMEM((tm * (d // LANES), LANES), F32), pltpu.VMEM((tm * (d // LANES), LANES), F32),
                        pltpu.SemaphoreType.DMA((2,))],
        compiler_params=_params("arbitrary"),
        name="moe_combine",
    )(dest0.reshape(steps, 1, tm), dest1.reshape(steps, 1, tm), ys, x2, gates, mod, g_final.reshape(1, d))


def _moe_ffn(x2, mod, g_ffn, w_router, wg, wu, wd, seq, g_final, final_norm):
    t, d = x2.shape
    h, logits = _router(x2, mod, g_ffn, w_router, seq)
    top_val, top_idx = lax.top_k(logits[:, :N_EXPERTS], TOP_K_EXPERTS)
    gates = jax.nn.softmax(top_val, axis=-1)

    n_assign = t * TOP_K_EXPERTS
    flat_e = top_idx.reshape(-1).astype(jnp.int32)
    onehot = (flat_e[:, None] == jnp.arange(N_EXPERTS, dtype=jnp.int32)[None, :]).astype(jnp.int32)
    rank = jnp.take_along_axis(jnp.cumsum(onehot, axis=0), flat_e[:, None], axis=1)[:, 0] - 1
    counts = jnp.sum(onehot, axis=0)
    padded = (counts + EXPERT_ROWS - 1) // EXPERT_ROWS * EXPERT_ROWS
    pad_end = jnp.cumsum(padded)
    dest = (pad_end - padded)[flat_e] + rank
    n_slots = (n_assign // EXPERT_ROWS + N_EXPERTS) * EXPERT_ROWS
    n_blocks = n_slots // EXPERT_ROWS
    slot_tok = jnp.zeros((n_slots,), jnp.int32).at[dest].set(jnp.arange(n_assign, dtype=jnp.int32) // TOP_K_EXPERTS)
    block_start = jnp.arange(n_blocks, dtype=jnp.int32) * EXPERT_ROWS
    block_expert = jnp.minimum(jnp.searchsorted(pad_end, block_start, side="right"), N_EXPERTS - 1).astype(jnp.int32)
    n_used = (pad_end[-1:] // EXPERT_ROWS).astype(jnp.int32)

    ys = _expert_ffn(h, slot_tok, block_expert, n_used, wg, wu, wd)
    dest2 = dest.reshape(t, TOP_K_EXPERTS)
    return _moe_combine(ys, dest2[:, 0], dest2[:, 1], gates, x2, mod, seq, g_final, final_norm)


def _token_mixer(x2, mod, g_mix, w_in, g_heads, w_out, batch, seq):
    pa, pb, pc, pd, p_idx, p_w, *pc_views = _in_projection(x2, mod, g_mix, _pack_w_in(w_in), seq)
    shape3 = lambda a: a.reshape(batch, -1, a.shape[-1])
    gh = g_heads.reshape(4, HEADS_PER_MIXER, HEAD_DIM)
    o_a = _stick_breaking_mixer(shape3(pa), gh[0])
    o_b = _dsa_mixer(shape3(pb), shape3(p_idx), shape3(p_w), gh[1])
    dilated = _dilated_branches([shape3(v) for v in [pc] + pc_views])
    o_d = _moba_mixer(shape3(pd), gh[3])
    return _out_projection(o_a, o_b, o_d, dilated, gh[2], w_out.astype(BF16), x2, mod, seq)


def kernel(x, c, w_ada, b_ada, g_mix, w_in, g_heads, w_out, g_ffn, w_ff_gate, w_ff_up, w_ff_down, w_router, w_exp_gate, w_exp_up, w_exp_down, g_final):
    batch, seq, d = x.shape
    depth = w_ada.shape[0]
    mods = _ada_modulation(c, w_ada, b_ada)
    x2 = x.reshape(batch * seq, d)
    for layer in range(depth):
        mod = mods[layer]
        x2 = _token_mixer(x2, mod, g_mix[layer], w_in[layer], g_heads[layer], w_out[layer], batch, seq)
        i = layer // 2
        last = layer == depth - 1
        if layer % 2 == 0:
            x2 = _dense_ffn(x2, mod, g_ffn[layer], w_ff_gate[i], w_ff_up[i], w_ff_down[i], seq, g_final, last)
        else:
            x2 = _moe_ffn(x2, mod, g_ffn[layer], w_router[i], w_exp_gate[i], w_exp_up[i], w_exp_down[i], seq,
                          g_final, last)
    return x2.reshape(batch, seq, d)
```

```python
import functools

import numpy as np
import jax
import jax.numpy as jnp
from jax import lax
from jax.experimental import pallas as pl
from jax.experimental.pallas import tpu as pltpu

F32 = jnp.float32
BF16 = jnp.bfloat16

HEAD_DIM = 64
HEADS_PER_MIXER = 4
MIXER_WIDTH = HEADS_PER_MIXER * HEAD_DIM
QKV_WIDTH = 3 * MIXER_WIDTH
IDX_HEADS = 8
IDX_DIM = 64
DSA_TOPK = 256
DILATIONS = (1, 4, 16)
DILATED_STEPS = 128
MOBA_BLOCK = 256
MOBA_TOPK = 3
MOBA_GROUP = 2
N_EXPERTS = 8
TOP_K_EXPERTS = 2
NORM_EPS = 1e-6
ATTN_SCALE = HEAD_DIM ** -0.5

LANES = 128
SUBLANES = 8
Q_TILE = 128
ROW_TILE = 512
FFN_ROWS = 1024
FFN_TILE = 1408
EXPERT_ROWS = 512
EXPERT_FF_TILE = 1792
GATHER_ROWS = 512
ROWS_PER_TRIP = 8
VMEM_LIMIT = 56 * 1024 * 1024
MASKED = -1e30
SB_UNDERFLOW = 104.0
SB_FIRST_BLOCKS = 3

_NT = (((1,), (1,)), ((), ()))
_TN = (((0,), (0,)), ((), ()))


def _alibi_slopes(mixer_pos):
    idx = np.arange(HEADS_PER_MIXER, dtype=np.float32) * 3 + (mixer_pos + 1)
    return tuple(float(s) for s in np.exp2(-8.0 * idx / 12.0).astype(np.float32))


def _params(*semantics):
    return pltpu.CompilerParams(dimension_semantics=semantics, vmem_limit_bytes=VMEM_LIMIT)


def _modulated_norm(x, gain, shift, scale):
    y = x * lax.rsqrt(jnp.mean(x * x, axis=-1, keepdims=True) + NORM_EPS) * gain
    return y * (1.0 + scale) + shift


def _head_norm(acc, gain):
    return acc * lax.rsqrt(jnp.mean(acc * acc, axis=-1, keepdims=True) + NORM_EPS) * gain


def _head_cols(h):
    return slice(h * HEAD_DIM, (h + 1) * HEAD_DIM)


def _paired_loop(n, body, carry):
    carry = lax.fori_loop(0, n // 2, lambda jj, c: body(2 * jj + 1, body(2 * jj, c, 0), 1), carry)
    return lax.cond(n % 2 == 1, lambda c: body(n - 1, c, 0), lambda c: c, carry)


def _ada_kernel(c_ref, w_ref, b_ref, o_ref):
    c = c_ref[...]
    cond = c / (1.0 + jnp.exp(-c))
    o_ref[0, 0] = jnp.dot(cond, w_ref[0], preferred_element_type=F32,
                          precision=lax.Precision.HIGHEST) + b_ref[0, 0]


def _ada_modulation(c, w_ada, b_ada):
    depth, d, _ = w_ada.shape
    b = c.shape[0]
    out = pl.pallas_call(
        _ada_kernel,
        grid=(depth, 6),
        in_specs=[pl.BlockSpec((b, d), lambda l, k: (0, 0)),
                  pl.BlockSpec((1, d, d), lambda l, k: (l, 0, k)),
                  pl.BlockSpec((1, 1, 1, d), lambda l, k: (l, k, 0, 0))],
        out_specs=pl.BlockSpec((1, 1, b, d), lambda l, k: (l, k, 0, 0)),
        out_shape=jax.ShapeDtypeStruct((depth, 6, b, d), F32),
        compiler_params=_params("arbitrary", "arbitrary"),
        name="ada_modulation",
    )(c, w_ada, b_ada.reshape(depth, 6, 1, d))
    return out.transpose(0, 2, 1, 3)


IN_WIDTHS = (QKV_WIDTH, QKV_WIDTH, QKV_WIDTH, QKV_WIDTH, IDX_HEADS * IDX_DIM + LANES, LANES)


def _in_proj_kernel(x_ref, mod_ref, g_ref, w_ref, oa, ob, oc, od, oidx, ow, oc4, oc16, pc_ref):
    h = _modulated_norm(x_ref[...], g_ref[...], mod_ref[0, 0:1, :], mod_ref[0, 1:2, :]).astype(BF16)
    tm = x_ref.shape[0]
    off = 0
    for o_ref, width in zip((oa, ob, oc, od, oidx, ow), IN_WIDTHS):
        val = jnp.dot(h, w_ref[:, off:off + width], preferred_element_type=F32)
        o_ref[...] = val.astype(o_ref.dtype)
        if o_ref is oc:
            for s in range(QKV_WIDTH // LANES):
                pc_ref[s * tm:(s + 1) * tm, :] = val[:, s * LANES:(s + 1) * LANES]
        off += width
    for r, view in zip(DILATIONS[1:], (oc4, oc16)):
        for c in range(r):
            for s in range(QKV_WIDTH // LANES):
                col = c * QKV_WIDTH + s * LANES
                view[:, col:col + LANES] = pc_ref[pl.ds(s * tm + c, tm // r, stride=r), :].astype(view.dtype)


def _pack_w_in(w_in):
    d = w_in.shape[0]
    n_qkv = 4 * QKV_WIDTH
    n_qi = IDX_HEADS * IDX_DIM
    z = lambda n: jnp.zeros((d, n), w_in.dtype)
    return jnp.concatenate([w_in[:, :n_qkv + n_qi + IDX_DIM], z(LANES - IDX_DIM),
                            w_in[:, n_qkv + n_qi + IDX_DIM:], z(LANES - IDX_HEADS)], axis=1).astype(BF16)


def _in_projection(x2, mod, g_mix, w_packed, seq):
    t, d = x2.shape
    tm = min(ROW_TILE, seq)
    per_batch = seq // tm
    dtypes = (BF16, BF16, BF16, BF16, BF16, F32)
    views = DILATIONS[1:]
    return pl.pallas_call(
        _in_proj_kernel,
        grid=(t // tm,),
        in_specs=[pl.BlockSpec((tm, d), lambda i: (i, 0)),
                  pl.BlockSpec((1, 6, d), lambda i: (i // per_batch, 0, 0)),
                  pl.BlockSpec((1, d), lambda i: (0, 0)),
                  pl.BlockSpec(w_packed.shape, lambda i: (0, 0))],
        out_specs=([pl.BlockSpec((tm, w), lambda i: (i, 0)) for w in IN_WIDTHS]
                   + [pl.BlockSpec((tm // r, r * QKV_WIDTH), lambda i: (i, 0)) for r in views]),
        out_shape=([jax.ShapeDtypeStruct((t, w), dt) for w, dt in zip(IN_WIDTHS, dtypes)]
                   + [jax.ShapeDtypeStruct((t // r, r * QKV_WIDTH), BF16) for r in views]),
        scratch_shapes=[pltpu.VMEM((tm * (QKV_WIDTH // LANES), LANES), F32)],
        compiler_params=_params("arbitrary"),
        name="in_projection",
    )(x2, mod, g_mix.reshape(1, d), w_packed)


def _sb_kernel(q_ref, k_ref, v_ref, g_ref, o_ref):
    tq = q_ref.shape[1]
    i = pl.program_id(1)
    key_row = lax.broadcasted_iota(jnp.int32, (tq, 1), 0)
    q_lane = lax.broadcasted_iota(jnp.int32, (1, tq), 1)
    later = (lax.broadcasted_iota(jnp.int32, (tq, tq), 1) > lax.broadcasted_iota(jnp.int32, (tq, tq), 0))
    later = jnp.where(later, 1.0, 0.0).astype(BF16)

    qs = [q_ref[0, :, _head_cols(h)] * ATTN_SCALE for h in range(HEADS_PER_MIXER)]

    heads = range(HEADS_PER_MIXER)

    def blocks(starts, keeps, tails):
        z = jnp.concatenate([lax.dot_general(k_ref[0, pl.ds(st, tq), _head_cols(h)], qs[h], _NT,
                                             preferred_element_type=F32) for st in starts for h in heads], axis=1)
        softplus = jnp.maximum(z, 0.0) + jnp.log(1.0 + jnp.exp(-jnp.abs(z)))
        keep = None
        if any(kp is not None for kp in keeps):
            ones = jnp.ones((tq, tq), F32)
            keep = jnp.concatenate([ones if kp is None else kp for kp in keeps for _ in heads], axis=1)
        log_1m = -softplus if keep is None else -softplus * keep
        hi = log_1m.astype(BF16)
        lo = (log_1m - hi.astype(F32)).astype(BF16)
        inside = jnp.dot(later, hi, preferred_element_type=F32) + jnp.dot(later, lo, preferred_element_type=F32)
        block_sum = jnp.sum(log_1m, axis=0, keepdims=True)
        piece = lambda x, b, h: x[:, (b * HEADS_PER_MIXER + h) * tq:(b * HEADS_PER_MIXER + h + 1) * tq]
        tail_rows = []
        tails = list(tails)
        for b in range(len(starts)):
            tail_rows += tails
            tails = [tails[h] + piece(block_sum, b, h) for h in heads]
        a = jnp.exp(z - softplus + inside + jnp.concatenate(tail_rows, axis=1))
        if keep is not None:
            a = a * keep
        a = a.astype(BF16)
        av = [sum(lax.dot_general(v_ref[0, pl.ds(st, tq), _head_cols(h)], piece(a, b, h), _TN,
                                  preferred_element_type=F32) for b, st in enumerate(starts)) for h in heads]
        return tails, av

    starts = [pl.multiple_of(jnp.maximum(i - n, 0) * tq, tq) for n in range(SB_FIRST_BLOCKS)]
    keeps = [jnp.where(key_row < q_lane, 1.0, 0.0)]
    keeps += [jnp.full((tq, tq), jnp.where(i >= n, 1.0, 0.0), F32) for n in range(1, SB_FIRST_BLOCKS)]
    tails, accs = blocks(starts, keeps, [jnp.zeros((1, tq), F32)] * HEADS_PER_MIXER)
    state = tuple(zip(tails, accs))

    def body(carry):
        j, state = carry
        tails, av = blocks([pl.multiple_of(j * tq, tq)], [None], [tail for tail, _ in state])
        return j - 1, tuple((tails[h], state[h][1] + av[h]) for h in heads)

    def cond(carry):
        j, state = carry
        worst = functools.reduce(jnp.maximum, [tail for tail, _ in state])
        return jnp.logical_and(j >= 0, jnp.max(worst) > -SB_UNDERFLOW)

    _, state = lax.while_loop(cond, body, (i - SB_FIRST_BLOCKS, state))
    outs = []
    for h in range(HEADS_PER_MIXER):
        o = state[h][1]
        outs.append(o * lax.rsqrt(jnp.mean(o * o, axis=0, keepdims=True) + NORM_EPS) * g_ref[:, h:h + 1])
    o_ref[0] = jnp.concatenate(outs, axis=0).T.astype(o_ref.dtype)


def _stick_breaking_mixer(p, g_heads):
    b, seq, _ = p.shape
    tq = min(Q_TILE, seq)
    return pl.pallas_call(
        _sb_kernel,
        grid=(b, seq // tq),
        in_specs=[pl.BlockSpec((1, tq, MIXER_WIDTH), lambda bi, i: (bi, i, 0)),
                  pl.BlockSpec((1, seq, MIXER_WIDTH), lambda bi, i: (bi, 0, 1)),
                  pl.BlockSpec((1, seq, MIXER_WIDTH), lambda bi, i: (bi, 0, 2)),
                  pl.BlockSpec((HEAD_DIM, HEADS_PER_MIXER), lambda bi, i: (0, 0))],
        out_specs=pl.BlockSpec((1, tq, MIXER_WIDTH), lambda bi, i: (bi, i, 0)),
        out_shape=jax.ShapeDtypeStruct((b, seq, MIXER_WIDTH), BF16),
        compiler_params=_params("arbitrary", "arbitrary"),
        name="stick_breaking",
    )(p, p, p, g_heads.T)


def _dsa_kernel(q_ref, k_ref, v_ref, qi_ref, ki_ref, w_ref, g_ref, o_ref, sc_ref, sc16_ref, s_ref, p_ref, *, slopes):
    tq = q_ref.shape[1]
    tk = sc_ref.shape[1]
    i = pl.program_id(1)
    n_blocks = ((i + 1) * tq + tk - 1) // tk
    qpos = i * tq + lax.broadcasted_iota(jnp.int32, (1, tq), 1)
    key_row = lax.broadcasted_iota(jnp.int32, (tk, 1), 0)
    neg_inf = float("-inf")

    w_t = w_ref[0].T * (IDX_HEADS ** -0.5 * IDX_DIM ** -0.5)
    w_rows = [w_t[h:h + 1, :] for h in range(IDX_HEADS)]
    qi = qi_ref[0]

    def top_half(x):
        bits = lax.bitcast_convert_type(x, jnp.int32) & jnp.int32(-65536)
        return lax.bitcast_convert_type(bits, F32).astype(BF16)

    def score_body(j, carry, slot):
        start = pl.multiple_of(j * tk, tk)
        ki = ki_ref[0, pl.ds(start, tk), 0:IDX_DIM]
        sc = jnp.zeros((tk, tq), F32)
        for h in range(IDX_HEADS):
            x = lax.dot_general(ki, qi[:, h * IDX_DIM:(h + 1) * IDX_DIM], _NT, preferred_element_type=F32)
            sc = sc + w_rows[h] * jnp.maximum(x, 0.0)
        sc = jnp.where((start + key_row) <= qpos, sc + 0.0, neg_inf)
        sc_ref[j] = sc
        sc16_ref[j] = top_half(sc)
        return carry

    _paired_loop(n_blocks, score_body, 0)

    def count(pred):
        def add_block(j, cnt, slot):
            c = jnp.where(pred(sc_ref[j]), 1.0, 0.0)
            return cnt + jnp.sum(c.reshape(tk // 8, 8, tq), axis=0)

        cnt = _paired_loop(n_blocks, add_block, jnp.zeros((8, tq), F32))
        return jnp.sum(cnt, axis=0, keepdims=True)

    int_min = jnp.int32(-2 ** 31)

    def ordered_to_float(u):
        key = u ^ int_min
        bits = jnp.where(key >= 0, key, key ^ jnp.int32(0x7FFFFFFF))
        return lax.bitcast_convert_type(bits, F32)

    def count_top_half(cand16):
        def add_block(j, cnt, slot):
            c = jnp.where(sc16_ref[j] >= cand16, jnp.ones((), BF16), jnp.zeros((), BF16))
            for g in range(tk // 16):
                cnt = cnt + c[g * 16:(g + 1) * 16]
            return cnt

        cnt = _paired_loop(n_blocks, add_block, jnp.zeros((16, tq), BF16))
        return jnp.sum(cnt.astype(F32), axis=0, keepdims=True)

    def search_body(step, carry, half):
        prefix, n_at_prefix = carry
        cand = prefix | jnp.left_shift(jnp.int32(1), 31 - step)
        cand_f = ordered_to_float(cand)
        n_ge = count_top_half(top_half(cand_f)) if half else count(lambda s: s >= cand_f)
        keep = n_ge >= DSA_TOPK
        return jnp.where(keep, cand, prefix), jnp.where(keep, n_ge, n_at_prefix)

    keep_all = (qpos + 1) <= DSA_TOPK

    carry = (jnp.zeros((1, tq), jnp.int32), jnp.zeros((1, tq), F32))
    carry = lax.fori_loop(0, 16, functools.partial(search_body, half=True), carry)
    prefix, n_ge_tau = lax.fori_loop(16, 32, functools.partial(search_body, half=False), carry)
    tau = jnp.where(keep_all, float(np.finfo(np.float32).min), ordered_to_float(prefix))
    no_ties = jnp.min(jnp.where(jnp.logical_or(keep_all, n_ge_tau == DSA_TOPK), 1.0, 0.0)) > 0.5

    earlier = (lax.broadcasted_iota(jnp.int32, (tk, tk), 1) < lax.broadcasted_iota(jnp.int32, (tk, tk), 0))
    earlier = jnp.where(earlier, 1.0, 0.0).astype(BF16)
    qs = [q_ref[0, :, _head_cols(h)] * ATTN_SCALE for h in range(HEADS_PER_MIXER)]
    key_bias = [slopes[h] * lax.broadcasted_iota(jnp.int32, (tk, tq), 0).astype(F32)
                for h in range(HEADS_PER_MIXER)]

    def attn_body(j, carry, slot, n_ties=None):
        ties_seen, stats = carry
        s_slot = [s_ref.at[slot * HEADS_PER_MIXER + h] for h in range(HEADS_PER_MIXER)]
        p_slot = [p_ref.at[slot * HEADS_PER_MIXER + h] for h in range(HEADS_PER_MIXER)]
        start = pl.multiple_of(j * tk, tk)
        sc = sc_ref[j]
        if n_ties is None:
            sel = sc >= tau
        else:
            tie = jnp.where(sc == tau, 1.0, 0.0)
            rank = jnp.dot(earlier, tie.astype(BF16), preferred_element_type=F32) + ties_seen
            sel = jnp.where(sc > tau, 1.0, jnp.where(rank < n_ties, tie, 0.0)) > 0.5
            ties_seen = ties_seen + jnp.sum(tie, axis=0, keepdims=True)
        block_pos = (j * tk - i * tq).astype(F32)
        for h in range(HEADS_PER_MIXER):
            k = k_ref[0, pl.ds(start, tk), _head_cols(h)]
            s = lax.dot_general(k, qs[h], _NT, preferred_element_type=F32) + key_bias[h]
            s_slot[h][...] = jnp.where(sel, s, MASKED)
        locals_ = []
        for h in range(HEADS_PER_MIXER):
            s = s_slot[h][...]
            m_loc = jnp.max(s, axis=0, keepdims=True)
            p = jnp.exp(s - m_loc)
            locals_.append((m_loc, jnp.sum(p, axis=0, keepdims=True)))
            p_slot[h][...] = p.astype(BF16)
        new_stats = []
        for h in range(HEADS_PER_MIXER):
            m, l, acc = stats[h]
            m_loc, l_blk = locals_[h]
            v = v_ref[0, pl.ds(start, tk), _head_cols(h)]
            pv_blk = lax.dot_general(v, p_slot[h][...], _TN, preferred_element_type=F32)
            m_blk = m_loc + slopes[h] * block_pos
            m_new = jnp.maximum(m, m_blk)
            w_old = jnp.exp(m - m_new)
            w_blk = jnp.where(m_loc > 0.5 * MASKED, jnp.exp(m_blk - m_new), 0.0)
            new_stats.append((m_new, w_old * l + w_blk * l_blk, w_old * acc + w_blk * pv_blk))
        return ties_seen, tuple(new_stats)

    init = tuple((jnp.full((1, tq), MASKED, F32), jnp.zeros((1, tq), F32), jnp.zeros((HEAD_DIM, tq), F32))
                 for _ in range(HEADS_PER_MIXER))
    start_carry = (jnp.zeros((1, tq), F32), init)

    def attend_without_ties():
        return _paired_loop(n_blocks, attn_body, start_carry)[1]

    def attend_with_ties():
        n_gt = count(lambda s: s > tau)
        n_ties = jnp.where(keep_all, 1e9, DSA_TOPK - n_gt)
        return _paired_loop(n_blocks, functools.partial(attn_body, n_ties=n_ties), start_carry)[1]

    stats = lax.cond(no_ties, attend_without_ties, attend_with_ties)
    outs = []
    for h in range(HEADS_PER_MIXER):
        _, l, acc = stats[h]
        o = acc / l
        outs.append(o * lax.rsqrt(jnp.mean(o * o, axis=0, keepdims=True) + NORM_EPS) * g_ref[:, h:h + 1])
    o_ref[0] = jnp.concatenate(outs, axis=0).T.astype(o_ref.dtype)


def _dsa_mixer(p, p_idx, p_w, g_heads):
    b, seq, _ = p.shape
    tq = min(4 * Q_TILE, seq)
    tk = tq
    assert seq // 16 <= 256, "the packed-bf16 count accumulators are exact only up to 256 adds"
    qi_width = IDX_HEADS * IDX_DIM
    return pl.pallas_call(
        functools.partial(_dsa_kernel, slopes=_alibi_slopes(0)),
        grid=(b, seq // tq),
        in_specs=[pl.BlockSpec((1, tq, MIXER_WIDTH), lambda bi, i: (bi, i, 0)),
                  pl.BlockSpec((1, seq, MIXER_WIDTH), lambda bi, i: (bi, 0, 1)),
                  pl.BlockSpec((1, seq, MIXER_WIDTH), lambda bi, i: (bi, 0, 2)),
                  pl.BlockSpec((1, tq, qi_width), lambda bi, i: (bi, i, 0)),
                  pl.BlockSpec((1, seq, LANES), lambda bi, i: (bi, 0, qi_width // LANES)),
                  pl.BlockSpec((1, tq, LANES), lambda bi, i: (bi, i, 0)),
                  pl.BlockSpec((HEAD_DIM, HEADS_PER_MIXER), lambda bi, i: (0, 0))],
        out_specs=pl.BlockSpec((1, tq, MIXER_WIDTH), lambda bi, i: (bi, i, 0)),
        out_shape=jax.ShapeDtypeStruct((b, seq, MIXER_WIDTH), BF16),
        scratch_shapes=[pltpu.VMEM((seq // tk, tk, tq), F32),
                        pltpu.VMEM((seq // tk, tk, tq), BF16),
                        pltpu.VMEM((2 * HEADS_PER_MIXER, tk, tq), F32),
                        pltpu.VMEM((2 * HEADS_PER_MIXER, tk, tq), BF16)],
        compiler_params=_params("arbitrary", "arbitrary"),
        name="dsa",
    )(p, p, p, p_idx, p_idx, p_w, g_heads.T)


def _band_kernel(q_ref, kp_ref, kc_ref, vp_ref, vc_ref, o_ref, lse_ref, *, dilation, slopes):
    tq = q_ref.shape[1]
    ui = pl.program_id(2)
    heads = range(HEADS_PER_MIXER)
    u_q = jnp.concatenate([ui * tq + lax.broadcasted_iota(jnp.int32, (1, tq), 1)] * HEADS_PER_MIXER, axis=1)
    u_k = ui * tq - DILATED_STEPS + lax.broadcasted_iota(jnp.int32, (DILATED_STEPS + tq, 1), 0)
    steps = u_q - u_k
    valid = jnp.logical_and(jnp.logical_and(steps >= 0, steps <= DILATED_STEPS), u_k >= 0)
    slope_row = jnp.concatenate([jnp.full((1, tq), slopes[h], F32) for h in heads], axis=1)
    s = jnp.concatenate(
        [lax.dot_general(jnp.concatenate([kp_ref[0, :, _head_cols(h)], kc_ref[0, :, _head_cols(h)]], axis=0),
                         q_ref[0, :, _head_cols(h)], _NT, preferred_element_type=F32) for h in heads], axis=1)
    s = jnp.where(valid, s * ATTN_SCALE - slope_row * (steps * dilation).astype(F32), MASKED)
    m = jnp.max(s, axis=0, keepdims=True)
    e = jnp.exp(s - m)
    den = jnp.sum(e, axis=0, keepdims=True)
    p = e.astype(BF16)
    lse = m + jnp.log(den)
    outs = []
    for h in heads:
        cols = slice(h * tq, (h + 1) * tq)
        v = jnp.concatenate([vp_ref[0, :, _head_cols(h)], vc_ref[0, :, _head_cols(h)]], axis=0)
        outs.append(lax.dot_general(v, p[:, cols], _TN, preferred_element_type=F32) / den[:, cols])
    o_ref[0] = jnp.concatenate(outs, axis=0).T
    lse_rows = jnp.concatenate([lse[:, h * tq:(h + 1) * tq] for h in heads]
                               + [jnp.zeros((LANES - HEADS_PER_MIXER, tq), F32)], axis=0)
    lse_ref[0] = lse_rows.T


def _dilated_branch(view, dilation, slopes):
    b, length, _ = view.shape
    classes = dilation
    tq = min(2 * Q_TILE, length)
    back = tq // DILATED_STEPS
    spec = lambda part, prev: (
        pl.BlockSpec((1, DILATED_STEPS, MIXER_WIDTH),
                     lambda bi, c, ui: (bi, jnp.maximum(ui * back - 1, 0), c * 3 + part)) if prev
        else pl.BlockSpec((1, tq, MIXER_WIDTH), lambda bi, c, ui: (bi, ui, c * 3 + part)))
    out, lse = pl.pallas_call(
        functools.partial(_band_kernel, dilation=dilation, slopes=slopes),
        grid=(b, classes, length // tq),
        in_specs=[spec(0, False), spec(1, True), spec(1, False), spec(2, True), spec(2, False)],
        out_specs=[pl.BlockSpec((1, tq, MIXER_WIDTH), lambda bi, c, ui: (bi, ui, c)),
                   pl.BlockSpec((1, tq, LANES), lambda bi, c, ui: (bi, ui, c))],
        out_shape=[jax.ShapeDtypeStruct((b, length, classes * MIXER_WIDTH), F32),
                   jax.ShapeDtypeStruct((b, length, classes * LANES), F32)],
        compiler_params=_params("arbitrary", "arbitrary", "arbitrary"),
        name=f"dilated_r{dilation}",
    )(view, view, view, view, view)
    return out, lse


def _merge_dilated(o_refs, l_refs, g_ref, scratch, rows):
    chunks = MIXER_WIDTH // LANES
    outs = [[o_refs[0][:, s * LANES:(s + 1) * LANES] for s in range(chunks)]]
    lses = [l_refs[0][...]]
    for r, o_view, l_view, o_nat, l_nat in zip(DILATIONS[1:], o_refs[1:], l_refs[1:], scratch[0::2], scratch[1::2]):
        for c in range(r):
            for s in range(chunks):
                col = c * MIXER_WIDTH + s * LANES
                o_nat[pl.ds(s * rows + c, rows // r, stride=r), :] = o_view[:, col:col + LANES]
            l_nat[pl.ds(c, rows // r, stride=r), :] = l_view[:, c * LANES:(c + 1) * LANES]
        outs.append([o_nat[s * rows:(s + 1) * rows, :] for s in range(chunks)])
        lses.append(l_nat[...])
    top = functools.reduce(jnp.maximum, lses)
    wts = [jnp.exp(l - top) for l in lses]
    def head_of(index):
        return sum(jnp.where(index >= h * HEAD_DIM, 1, 0) for h in range(1, HEADS_PER_MIXER))

    head_of_col = head_of(lax.broadcasted_iota(jnp.int32, (LANES, MIXER_WIDTH), 1))
    spread = jnp.where(lax.broadcasted_iota(jnp.int32, (LANES, MIXER_WIDTH), 0) == head_of_col, 1.0, 0.0)
    same_head = (head_of(lax.broadcasted_iota(jnp.int32, (MIXER_WIDTH, MIXER_WIDTH), 0))
                 == head_of(lax.broadcasted_iota(jnp.int32, (MIXER_WIDTH, MIXER_WIDTH), 1)))
    head_mean = jnp.where(same_head, 1.0 / HEAD_DIM, 0.0)

    def times(x, m):
        hi = x.astype(BF16)
        lo = (x - hi.astype(F32)).astype(BF16)
        m = m.astype(BF16)
        return jnp.dot(hi, m, preferred_element_type=F32) + jnp.dot(lo, m, preferred_element_type=F32)

    w_cols = [times(w, spread) for w in wts]
    full = [jnp.concatenate(o, axis=1) for o in outs]
    mixed = sum(w * o for w, o in zip(w_cols, full)) / sum(w_cols)
    return mixed * lax.rsqrt(times(mixed * mixed, head_mean) + NORM_EPS) * g_ref[...]


def _dilated_branches(views):
    slopes = _alibi_slopes(1)
    branches = [_dilated_branch(v, r, slopes) for v, r in zip(views, DILATIONS)]
    flat = lambda a: a.reshape(-1, a.shape[-1])
    return [flat(o) for o, _ in branches], [flat(l) for _, l in branches]


def _moba_kernel(q_ref, k_ref, v_ref, g_ref, o_ref, kmean_ref, chosen_ref, s_ref, p_ref, *, slopes):
    tq = q_ref.shape[1]
    n_kv = k_ref.shape[1] // MOBA_BLOCK
    own = pl.program_id(1)

    @pl.when(own == 0)
    def _():
        kmean_ref[...] = jnp.zeros_like(kmean_ref)
        for n in range(n_kv):
            blk = k_ref[0, n * MOBA_BLOCK:(n + 1) * MOBA_BLOCK, :].astype(F32)
            kmean_ref[n:n + 1, :] = jnp.sum(blk, axis=0, keepdims=True) * (1.0 / MOBA_BLOCK)

    key_row = lax.broadcasted_iota(jnp.int32, (MOBA_BLOCK, 1), 0)
    q_lane = lax.broadcasted_iota(jnp.int32, (1, tq), 1)
    blk_rows = kmean_ref.shape[0]
    blk_row = lax.broadcasted_iota(jnp.int32, (blk_rows, 1), 0)
    blk_f = blk_row.astype(F32)
    neg_inf = float("-inf")
    group = tq // MOBA_BLOCK
    first_own = own * group
    sub_block = sum(jnp.where(q_lane >= g * MOBA_BLOCK, 1, 0) for g in range(1, group)) if group > 1 else 0
    own_blk = first_own + sub_block
    q_local = q_lane - sub_block * MOBA_BLOCK

    qs = []
    for h in range(HEADS_PER_MIXER):
        q = q_ref[0, :, _head_cols(h)]
        gate = lax.dot_general(kmean_ref[:, _head_cols(h)], q.astype(F32), _NT, preferred_element_type=F32,
                               precision=lax.Precision.HIGHEST)
        gate = jnp.where(blk_row < own_blk, gate, neg_inf)
        picks = jnp.zeros((blk_rows, tq), F32)
        for _ in range(MOBA_TOPK):
            top = jnp.max(gate, axis=0, keepdims=True)
            is_top = jnp.logical_and(gate == top, top > neg_inf)
            first = jnp.min(jnp.where(is_top, blk_f, float(blk_rows)), axis=0, keepdims=True)
            pick = blk_f == first
            picks = jnp.where(pick, 1.0, picks)
            gate = jnp.where(pick, neg_inf, gate)
        chosen_ref[h] = picks
        qs.append(q * ATTN_SCALE)

    key_bias = [slopes[h] * lax.broadcasted_iota(jnp.int32, (MOBA_BLOCK, tq), 0).astype(F32)
                for h in range(HEADS_PER_MIXER)]

    def block_softmax(start, keep=None, slot=0):
        s_slot = [s_ref.at[slot * HEADS_PER_MIXER + h] for h in range(HEADS_PER_MIXER)]
        p_slot = [p_ref.at[slot * HEADS_PER_MIXER + h] for h in range(HEADS_PER_MIXER)]
        for h in range(HEADS_PER_MIXER):
            k = k_ref[0, pl.ds(start, MOBA_BLOCK), _head_cols(h)]
            s = lax.dot_general(k, qs[h], _NT, preferred_element_type=F32) + key_bias[h]
            s_slot[h][...] = s if keep is None else jnp.where(keep, s, MASKED)
        pieces = []
        for h in range(HEADS_PER_MIXER):
            s = s_slot[h][...]
            m_loc = jnp.max(s, axis=0, keepdims=True)
            p = jnp.exp(s - m_loc)
            pieces.append((m_loc, jnp.sum(p, axis=0, keepdims=True)))
            p_slot[h][...] = p.astype(BF16)
        out = []
        for h in range(HEADS_PER_MIXER):
            v = v_ref[0, pl.ds(start, MOBA_BLOCK), _head_cols(h)]
            out.append(pieces[h] + (lax.dot_general(v, p_slot[h][...], _TN, preferred_element_type=F32),))
        return out

    def merge(stats, j, pieces, use):
        block_pos = ((j - first_own) * MOBA_BLOCK).astype(F32)
        merged = []
        for h, (m_loc, l_blk, pv_blk) in enumerate(pieces):
            m_blk = jnp.where(use[h], m_loc + slopes[h] * block_pos, MASKED)
            m, l, acc = stats[h]
            m_new = jnp.maximum(m, m_blk)
            w_old = jnp.exp(m - m_new)
            w_blk = jnp.where(use[h], jnp.exp(m_blk - m_new), 0.0)
            merged.append((m_new, w_old * l + w_blk * l_blk, w_old * acc + w_blk * pv_blk))
        return tuple(merged)

    def picked(h, j):
        return chosen_ref[h, pl.ds(j, 1), :] > 0.5

    def body(j, stats, slot):
        start = pl.multiple_of(j * MOBA_BLOCK, MOBA_BLOCK)
        return merge(stats, j, block_softmax(start, slot=slot), [picked(h, j) for h in range(HEADS_PER_MIXER)])

    stats = tuple((jnp.full((1, tq), MASKED, F32), jnp.zeros((1, tq), F32), jnp.zeros((HEAD_DIM, tq), F32))
                  for _ in range(HEADS_PER_MIXER))
    stats = _paired_loop(first_own, body, stats)
    for g in range(group):
        j = first_own + g
        keep = jnp.logical_or(own_blk > j, jnp.logical_and(own_blk == j, key_row <= q_local))
        pieces = block_softmax(pl.multiple_of(j * MOBA_BLOCK, MOBA_BLOCK), keep=keep, slot=g % 2)
        use = [jnp.logical_or(own_blk == j, jnp.logical_and(own_blk > j, picked(h, j)))
               for h in range(HEADS_PER_MIXER)]
        stats = merge(stats, j, pieces, use)
    outs = []
    for h in range(HEADS_PER_MIXER):
        _, l, acc = stats[h]
        o = acc / l
        outs.append(o * lax.rsqrt(jnp.mean(o * o, axis=0, keepdims=True) + NORM_EPS) * g_ref[:, h:h + 1])
    o_ref[0] = jnp.concatenate(outs, axis=0).T.astype(o_ref.dtype)


def _moba_mixer(p, g_heads):
    b, seq, _ = p.shape
    tq = MOBA_GROUP * MOBA_BLOCK if seq % (MOBA_GROUP * MOBA_BLOCK) == 0 else MOBA_BLOCK
    blk_rows =-(-(seq // MOBA_BLOCK) // SUBLANES) * SUBLANES
    return pl.pallas_call(
        functools.partial(_moba_kernel, slopes=_alibi_slopes(2)),
        grid=(b, seq // tq),
        in_specs=[pl.BlockSpec((1, tq, MIXER_WIDTH), lambda bi, i: (bi, i, 0)),
                  pl.BlockSpec((1, seq, MIXER_WIDTH), lambda bi, i: (bi, 0, 1)),
                  pl.BlockSpec((1, seq, MIXER_WIDTH), lambda bi, i: (bi, 0, 2)),
                  pl.BlockSpec((HEAD_DIM, HEADS_PER_MIXER), lambda bi, i: (0, 0))],
        out_specs=pl.BlockSpec((1, tq, MIXER_WIDTH), lambda bi, i: (bi, i, 0)),
        out_shape=jax.ShapeDtypeStruct((b, seq, MIXER_WIDTH), BF16),
        scratch_shapes=[pltpu.VMEM((blk_rows, MIXER_WIDTH), F32),
                        pltpu.VMEM((HEADS_PER_MIXER, blk_rows, tq), F32),
                        pltpu.VMEM((2 * HEADS_PER_MIXER, MOBA_BLOCK, tq), F32),
                        pltpu.VMEM((2 * HEADS_PER_MIXER, MOBA_BLOCK, tq), BF16)],
        compiler_params=_params("arbitrary", "arbitrary"),
        name="moba",
    )(p, p, p, g_heads.T)


def _out_proj_kernel(oa, ob, od, o1, o4, o16, l1, l4, l16, gc_ref, w_ref, x_ref, mod_ref, o_ref, *scratch):
    rows = x_ref.shape[0]
    oc = _merge_dilated((o1, o4, o16), (l1, l4, l16), gc_ref, scratch, rows).astype(BF16)
    acc = jnp.zeros(x_ref.shape, F32)
    for m, o in enumerate((oa[...], ob[...], oc, od[...])):
        acc = acc + jnp.dot(o, w_ref[m * MIXER_WIDTH:(m + 1) * MIXER_WIDTH, :], preferred_element_type=F32)
    o_ref[...] = x_ref[...] + mod_ref[0, 2:3, :] * acc


def _out_projection(o_a, o_b, o_d, dilated, g_heads_c, w_out, x2, mod, seq):
    t, d = x2.shape
    tm = min(ROW_TILE, seq)
    per_batch = seq // tm
    o_spec = pl.BlockSpec((tm, MIXER_WIDTH), lambda i: (i, 0))
    branch_outs, branch_lses = dilated
    view_specs = ([pl.BlockSpec((tm // r, r * MIXER_WIDTH), lambda i: (i, 0)) for r in DILATIONS]
                  + [pl.BlockSpec((tm // r, r * LANES), lambda i: (i, 0)) for r in DILATIONS])
    scratch = []
    for _ in DILATIONS[1:]:
        scratch += [pltpu.VMEM((tm * (MIXER_WIDTH // LANES), LANES), F32), pltpu.VMEM((tm, LANES), F32)]
    return pl.pallas_call(
        _out_proj_kernel,
        grid=(t // tm,),
        in_specs=[o_spec] * 3 + view_specs + [pl.BlockSpec((1, MIXER_WIDTH), lambda i: (0, 0)),
                                              pl.BlockSpec(w_out.shape, lambda i: (0, 0)),
                                              pl.BlockSpec((tm, d), lambda i: (i, 0)),
                                              pl.BlockSpec((1, 6, d), lambda i: (i // per_batch, 0, 0))],
        out_specs=pl.BlockSpec((tm, d), lambda i: (i, 0)),
        out_shape=jax.ShapeDtypeStruct((t, d), F32),
        scratch_shapes=scratch,
        compiler_params=_params("arbitrary"),
        name="out_projection",
    )(*[o.reshape(t, MIXER_WIDTH) for o in (o_a, o_b, o_d)], *branch_outs, *branch_lses,
      g_heads_c.reshape(1, MIXER_WIDTH), w_out, x2, mod)


def _rms_gain(x, gain):
    return x * lax.rsqrt(jnp.mean(x * x, axis=-1, keepdims=True) + NORM_EPS) * gain


def _ffn_kernel(x_ref, mod_ref, g_ref, fg_ref, wg_ref, wu_ref, wd_ref, o_ref, h_ref, acc_ref, *, final_norm):
    f = pl.program_id(1)

    @pl.when(f == 0)
    def _():
        h_ref[...] = _modulated_norm(x_ref[...], g_ref[...], mod_ref[0, 3:4, :], mod_ref[0, 4:5, :]).astype(BF16)
        acc_ref[...] = jnp.zeros_like(acc_ref)

    h = h_ref[...]
    gate = jnp.dot(h, wg_ref[...], preferred_element_type=F32)
    up = jnp.dot(h, wu_ref[...], preferred_element_type=F32)
    act = (gate / (1.0 + jnp.exp(-gate)) * up).astype(BF16)
    acc_ref[...] += jnp.dot(act, wd_ref[...], preferred_element_type=F32)

    @pl.when(f == pl.num_programs(1) - 1)
    def _():
        out = x_ref[...] + mod_ref[0, 5:6, :] * acc_ref[...]
        o_ref[...] = _rms_gain(out, fg_ref[...]) if final_norm else out


def _dense_ffn(x2, mod, g_ffn, wg, wu, wd, seq, g_final, final_norm):
    t, d = x2.shape
    d_ff = wg.shape[1]
    tm = min(FFN_ROWS, seq)
    tf = FFN_TILE if d_ff % FFN_TILE == 0 else d_ff
    per_batch = seq // tm
    return pl.pallas_call(
        functools.partial(_ffn_kernel, final_norm=final_norm),
        grid=(t // tm, d_ff // tf),
        in_specs=[pl.BlockSpec((tm, d), lambda i, f: (i, 0)),
                  pl.BlockSpec((1, 6, d), lambda i, f: (i // per_batch, 0, 0)),
                  pl.BlockSpec((1, d), lambda i, f: (0, 0)),
                  pl.BlockSpec((1, d), lambda i, f: (0, 0)),
                  pl.BlockSpec((d, tf), lambda i, f: (0, f)),
                  pl.BlockSpec((d, tf), lambda i, f: (0, f)),
                  pl.BlockSpec((tf, d), lambda i, f: (f, 0))],
        out_specs=pl.BlockSpec((tm, d), lambda i, f: (i, 0)),
        out_shape=jax.ShapeDtypeStruct((t, d), F32),
        scratch_shapes=[pltpu.VMEM((tm, d), BF16), pltpu.VMEM((tm, d), F32)],
        compiler_params=_params("arbitrary", "arbitrary"),
        name="dense_ffn",
    )(x2, mod, g_ffn.reshape(1, d), g_final.reshape(1, d), wg.astype(BF16), wu.astype(BF16), wd.astype(BF16))


def _store_row_tiles(dst_ref, value):
    rows, d = value.shape
    chunks = d // LANES
    for s in range(chunks):
        dst_ref[pl.ds(s, rows, stride=chunks), :] = value[:, s * LANES:(s + 1) * LANES]


def _load_row_tile_chunk(src_ref, s, rows, chunks):
    return src_ref[pl.ds(s, rows, stride=chunks), :]


def _router_kernel(x_ref, mod_ref, g_ref, wr_ref, h_ref, logit_ref):
    h = _modulated_norm(x_ref[...], g_ref[...], mod_ref[0, 3:4, :], mod_ref[0, 4:5, :])
    _store_row_tiles(h_ref, h)
    logit_ref[...] = jnp.dot(h, wr_ref[...], preferred_element_type=F32, precision=lax.Precision.HIGHEST)


def _router(x2, mod, g_ffn, w_router, seq):
    t, d = x2.shape
    tm = min(ROW_TILE, seq)
    per_batch = seq // tm
    wr = jnp.zeros((d, LANES), F32).at[:, :N_EXPERTS].set(w_router.astype(F32))
    return pl.pallas_call(
        _router_kernel,
        grid=(t // tm,),
        in_specs=[pl.BlockSpec((tm, d), lambda i: (i, 0)),
                  pl.BlockSpec((1, 6, d), lambda i: (i // per_batch, 0, 0)),
                  pl.BlockSpec((1, d), lambda i: (0, 0)),
                  pl.BlockSpec((d, LANES), lambda i: (0, 0))],
        out_specs=[pl.BlockSpec((tm * (d // LANES), LANES), lambda i: (i, 0)),
                   pl.BlockSpec((tm, LANES), lambda i: (i, 0))],
        out_shape=[jax.ShapeDtypeStruct((t * (d // LANES), LANES), F32), jax.ShapeDtypeStruct((t, LANES), F32)],
        compiler_params=_params("arbitrary"),
        name="moe_router",
    )(x2, mod, g_ffn.reshape(1, d), wr)


def _row_copy(src_hbm, row, dst_ref, r, sem, chunks):
    src = src_hbm.at[pl.ds(pl.multiple_of(row * chunks, chunks), chunks), :]
    return pltpu.make_async_copy(src, dst_ref.at[pl.ds(pl.multiple_of(r * chunks, chunks), chunks), :], sem)


def _expert_kernel(be_ref, used_ref, cur_idx_ref, nxt_idx_ref, h_hbm, wg_ref, wu_ref, wd_ref, o_ref,
                   xin_ref, xb_ref, acc_ref, sem, *, n_ff_steps):
    m = pl.program_id(0)
    f = pl.program_id(1)
    n_used = used_ref[0]
    live = m < n_used
    rows, d = xb_ref.shape
    chunks = d // LANES

    def row_copies(idx_ref, slot, start):
        if not start:
            half = rows * chunks // 2
            for lane in range(2):
                pltpu.make_async_copy(h_hbm.at[pl.ds(0, half), :], xin_ref.at[slot, pl.ds(0, half), :],
                                      sem.at[slot, lane]).wait()
            return

        def body(group, c):
            for u in range(ROWS_PER_TRIP):
                r = ROWS_PER_TRIP * group + u
                _row_copy(h_hbm, idx_ref[0, 0, r], xin_ref.at[slot], r, sem.at[slot, u % 2],
                          chunks).start(priority=u % 2)
            return c
        lax.fori_loop(0, rows // ROWS_PER_TRIP, body, 0)

    @pl.when(jnp.logical_and(live, f == 0))
    def _():
        @pl.when(m == 0)
        def _():
            row_copies(cur_idx_ref, 0, start=True)

        for slot in range(2):
            @pl.when(m % 2 == slot)
            def _(slot=slot):
                row_copies(cur_idx_ref, slot, start=False)
                for s in range(chunks):
                    xb_ref[:, s * LANES:(s + 1) * LANES] = _load_row_tile_chunk(
                        xin_ref.at[slot], s, rows, chunks).astype(BF16)

        acc_ref[...] = jnp.zeros_like(acc_ref)

    def swiglu_step(start_next):
        h = xb_ref[...]
        if start_next:
            share = rows // n_ff_steps
            nxt = (m + 1) % 2
            for u in range(share):
                r = f * share + u
                _row_copy(h_hbm, nxt_idx_ref[0, 0, r], xin_ref.at[nxt], r, sem.at[nxt, u % 2],
                          chunks).start(priority=u % 2)
        gate = jnp.dot(h, wg_ref[0], preferred_element_type=F32)
        up = jnp.dot(h, wu_ref[0], preferred_element_type=F32)
        act = (gate / (1.0 + jnp.exp(-gate)) * up).astype(BF16)
        acc_ref[...] += jnp.dot(act, wd_ref[0], preferred_element_type=F32)

    next_live = m + 1 < n_used
    pl.when(jnp.logical_and(live, next_live))(functools.partial(swiglu_step, True))
    pl.when(jnp.logical_and(live, jnp.logical_not(next_live)))(functools.partial(swiglu_step, False))

    @pl.when(f == pl.num_programs(1) - 1)
    def _():
        _store_row_tiles(o_ref, jnp.where(live, acc_ref[...], 0.0))


def _expert_ffn(h_rows, slot_tok, block_expert, n_used, wg, wu, wd):
    d = wg.shape[1]
    chunks = d // LANES
    n_slots = slot_tok.shape[0]
    d_ff = wg.shape[2]
    tm = EXPERT_ROWS
    tf = EXPERT_FF_TILE if d_ff % EXPERT_FF_TILE == 0 else d_ff
    nf = d_ff // tf
    nm = n_slots // tm

    def fcol(m, f, used):
        return jnp.where(m < used[0], f, nf - 1)

    idx_spec = lambda shift: pl.BlockSpec((1, 1, tm), lambda m, f, be, used: (jnp.minimum(m + shift, nm - 1), 0, 0),
                                          memory_space=pltpu.SMEM)
    grid_spec = pltpu.PrefetchScalarGridSpec(
        num_scalar_prefetch=2,
        grid=(nm, nf),
        in_specs=[idx_spec(0), idx_spec(1), pl.BlockSpec(memory_space=pl.ANY),
                  pl.BlockSpec((1, d, tf), lambda m, f, be, used: (be[m], 0, fcol(m, f, used))),
                  pl.BlockSpec((1, d, tf), lambda m, f, be, used: (be[m], 0, fcol(m, f, used))),
                  pl.BlockSpec((1, tf, d), lambda m, f, be, used: (be[m], fcol(m, f, used), 0))],
        out_specs=pl.BlockSpec((tm * chunks, LANES), lambda m, f, be, used: (m, 0)),
        scratch_shapes=[pltpu.VMEM((2, tm * chunks, LANES), F32), pltpu.VMEM((tm, d), BF16),
                        pltpu.VMEM((tm, d), F32), pltpu.SemaphoreType.DMA((2, 2))])
    idx = slot_tok.reshape(nm, 1, tm)
    assert tm % (2 * nf) == 0
    return pl.pallas_call(
        functools.partial(_expert_kernel, n_ff_steps=nf),
        grid_spec=grid_spec,
        out_shape=jax.ShapeDtypeStruct((n_slots * chunks, LANES), F32),
        compiler_params=_params("arbitrary", "arbitrary"),
        name="expert_ffn",
    )(block_expert, n_used, idx, idx, h_rows, wg.astype(BF16), wu.astype(BF16), wd.astype(BF16))


def _combine_kernel(d0_ref, d1_ref, ys_hbm, x_ref, gates_ref, mod_ref, fg_ref, o_ref, y0_ref, y1_ref, sem, *,
                    final_norm):
    rows, d = o_ref.shape
    chunks = d // LANES

    def start(group, c):
        for u in range(ROWS_PER_TRIP // 2):
            r = ROWS_PER_TRIP // 2 * group + u
            _row_copy(ys_hbm, d0_ref[0, 0, r], y0_ref, r, sem.at[0], chunks).start(priority=0)
            _row_copy(ys_hbm, d1_ref[0, 0, r], y1_ref, r, sem.at[1], chunks).start(priority=1)
        return c

    lax.fori_loop(0, rows // (ROWS_PER_TRIP // 2), start, 0)
    for k, y_ref in enumerate((y0_ref, y1_ref)):
        pltpu.make_async_copy(ys_hbm.at[pl.ds(0, rows * chunks), :], y_ref, sem.at[k]).wait()
    gates = gates_ref[...]
    for s in range(chunks):
        cols = slice(s * LANES, (s + 1) * LANES)
        y = (_load_row_tile_chunk(y0_ref, s, rows, chunks) * gates[:, 0:1]
             + _load_row_tile_chunk(y1_ref, s, rows, chunks) * gates[:, 1:2])
        o_ref[:, cols] = x_ref[:, cols] + mod_ref[0, 5:6, cols] * y
    if final_norm:
        o_ref[...] = _rms_gain(o_ref[...], fg_ref[...])


def _moe_combine(ys, dest0, dest1, gates, x2, mod, seq, g_final, final_norm):
    t, d = x2.shape
    tm = min(GATHER_ROWS, seq)
    steps = t // tm
    per_batch = seq // tm
    idx_spec = pl.BlockSpec((1, 1, tm), lambda i: (i, 0, 0), memory_space=pltpu.SMEM)
    return pl.pallas_call(
        functools.partial(_combine_kernel, final_norm=final_norm),
        grid=(steps,),
        in_specs=[idx_spec, idx_spec, pl.BlockSpec(memory_space=pl.ANY),
                  pl.BlockSpec((tm, d), lambda i: (i, 0)),
                  pl.BlockSpec((tm, TOP_K_EXPERTS), lambda i: (i, 0)),
                  pl.BlockSpec((1, 6, d), lambda i: (i // per_batch, 0, 0)),
                  pl.BlockSpec((1, d), lambda i: (0, 0))],
        out_specs=pl.BlockSpec((tm, d), lambda i: (i, 0)),
        out_shape=jax.ShapeDtypeStruct((t, d), F32),
        scratch_shapes=[pltpu.VMEM((tm * (d // LANES), LANES), F32), pltpu.VMEM((tm * (d // LANES), LANES), F32),
                        pltpu.SemaphoreType.DMA((2,))],
        compiler_params=_params("arbitrary"),
        name="moe_combine",
    )(dest0.reshape(steps, 1, tm), dest1.reshape(steps, 1, tm), ys, x2, gates, mod, g_final.reshape(1, d))


def _moe_ffn(x2, mod, g_ffn, w_router, wg, wu, wd, seq, g_final, final_norm):
    t, d = x2.shape
    h, logits = _router(x2, mod, g_ffn, w_router, seq)
    top_val, top_idx = lax.top_k(logits[:, :N_EXPERTS], TOP_K_EXPERTS)
    gates = jax.nn.softmax(top_val, axis=-1)

    n_assign = t * TOP_K_EXPERTS
    flat_e = top_idx.reshape(-1).astype(jnp.int32)
    onehot = (flat_e[:, None] == jnp.arange(N_EXPERTS, dtype=jnp.int32)[None, :]).astype(jnp.int32)
    rank = jnp.take_along_axis(jnp.cumsum(onehot, axis=0), flat_e[:, None], axis=1)[:, 0] - 1
    counts = jnp.sum(onehot, axis=0)
    padded = (counts + EXPERT_ROWS - 1) // EXPERT_ROWS * EXPERT_ROWS
    pad_end = jnp.cumsum(padded)
    dest = (pad_end - padded)[flat_e] + rank
    n_slots = (n_assign // EXPERT_ROWS + N_EXPERTS) * EXPERT_ROWS
    n_blocks = n_slots // EXPERT_ROWS
    slot_tok = jnp.zeros((n_slots,), jnp.int32).at[dest].set(jnp.arange(n_assign, dtype=jnp.int32) // TOP_K_EXPERTS)
    block_start = jnp.arange(n_blocks, dtype=jnp.int32) * EXPERT_ROWS
    block_expert = jnp.minimum(jnp.searchsorted(pad_end, block_start, side="right"), N_EXPERTS - 1).astype(jnp.int32)
    n_used = (pad_end[-1:] // EXPERT_ROWS).astype(jnp.int32)

    ys = _expert_ffn(h, slot_tok, block_expert, n_used, wg, wu, wd)
    dest2 = dest.reshape(t, TOP_K_EXPERTS)
    return _moe_combine(ys, dest2[:, 0], dest2[:, 1], gates, x2, mod, seq, g_final, final_norm)


def _token_mixer(x2, mod, g_mix, w_in, g_heads, w_out, batch, seq):
    pa, pb, pc, pd, p_idx, p_w, *pc_views = _in_projection(x2, mod, g_mix, _pack_w_in(w_in), seq)
    shape3 = lambda a: a.reshape(batch, -1, a.shape[-1])
    gh = g_heads.reshape(4, HEADS_PER_MIXER, HEAD_DIM)
    o_a = _stick_breaking_mixer(shape3(pa), gh[0])
    o_b = _dsa_mixer(shape3(pb), shape3(p_idx), shape3(p_w), gh[1])
    dilated = _dilated_branches([shape3(v) for v in [pc] + pc_views])
    o_d = _moba_mixer(shape3(pd), gh[3])
    return _out_projection(o_a, o_b, o_d, dilated, gh[2], w_out.astype(BF16), x2, mod, seq)


def kernel(x, c, w_ada, b_ada, g_mix, w_in, g_heads, w_out, g_ffn, w_ff_gate, w_ff_up, w_ff_down, w_router, w_exp_gate, w_exp_up, w_exp_down, g_final):
    batch, seq, d = x.shape
    depth = w_ada.shape[0]
    mods = _ada_modulation(c, w_ada, b_ada)
    x2 = x.reshape(batch * seq, d)
    for layer in range(depth):
        mod = mods[layer]
        x2 = _token_mixer(x2, mod, g_mix[layer], w_in[layer], g_heads[layer], w_out[layer], batch, seq)
        i = layer // 2
        last = layer == depth - 1
        if layer % 2 == 0:
            x2 = _dense_ffn(x2, mod, g_ffn[layer], w_ff_gate[i], w_ff_up[i], w_ff_down[i], seq, g_final, last)
        else:
            x2 = _moe_ffn(x2, mod, g_ffn[layer], w_router[i], w_exp_gate[i], w_exp_up[i], w_exp_down[i], seq,
                          g_final, last)
    return x2.reshape(batch, seq, d)
```

```python
import functools

import numpy as np
import jax
import jax.numpy as jnp
from jax import lax
from jax.experimental import pallas as pl
from jax.experimental.pallas import tpu as pltpu

F32 = jnp.float32
BF16 = jnp.bfloat16

HEAD_DIM = 64
HEADS_PER_MIXER = 4
MIXER_WIDTH = HEADS_PER_MIXER * HEAD_DIM
QKV_WIDTH = 3 * MIXER_WIDTH
IDX_HEADS = 8
IDX_DIM = 64
DSA_TOPK = 256
DILATIONS = (1, 4, 16)
DILATED_STEPS = 128
MOBA_BLOCK = 256
MOBA_TOPK = 3
MOBA_GROUP = 2
N_EXPERTS = 8
TOP_K_EXPERTS = 2
NORM_EPS = 1e-6
ATTN_SCALE = HEAD_DIM ** -0.5

LANES = 128
SUBLANES = 8
BF16_ROWS = 2 * SUBLANES
BF16_EXACT_COUNT = 256
Q_TILE = 128
ROW_TILE = 512
FFN_ROWS = 1024
FFN_TILE = 1408
EXPERT_ROWS = 512
EXPERT_FF_TILE = 1792
GATHER_ROWS = 512
ROWS_PER_TRIP = 8
VMEM_LIMIT = 56 * 1024 * 1024
MASKED = -1e30
SB_UNDERFLOW = 104.0
SB_FIRST_BLOCKS = 3

_NT = (((1,), (1,)), ((), ()))
_TN = (((0,), (0,)), ((), ()))


def _alibi_slopes(mixer_pos):
    idx = np.arange(HEADS_PER_MIXER, dtype=np.float32) * 3 + (mixer_pos + 1)
    return tuple(float(s) for s in np.exp2(-8.0 * idx / 12.0).astype(np.float32))


def _params(*semantics):
    return pltpu.CompilerParams(dimension_semantics=semantics, vmem_limit_bytes=VMEM_LIMIT)


def _modulated_norm(x, gain, shift, scale):
    y = x * lax.rsqrt(jnp.mean(x * x, axis=-1, keepdims=True) + NORM_EPS) * gain
    return y * (1.0 + scale) + shift


def _head_cols(h):
    return slice(h * HEAD_DIM, (h + 1) * HEAD_DIM)


def _paired_loop(n, body, carry):
    carry = lax.fori_loop(0, n // 2, lambda jj, c: body(2 * jj + 1, body(2 * jj, c, 0), 1), carry)
    return lax.cond(n % 2 == 1, lambda c: body(n - 1, c, 0), lambda c: c, carry)


def _ada_kernel(c_ref, w_ref, b_ref, o_ref):
    c = c_ref[...]
    cond = c / (1.0 + jnp.exp(-c))
    o_ref[0, 0] = jnp.dot(cond, w_ref[0], preferred_element_type=F32,
                          precision=lax.Precision.HIGHEST) + b_ref[0, 0]


def _ada_modulation(c, w_ada, b_ada):
    depth, d, _ = w_ada.shape
    b = c.shape[0]
    out = pl.pallas_call(
        _ada_kernel,
        grid=(depth, 6),
        in_specs=[pl.BlockSpec((b, d), lambda l, k: (0, 0)),
                  pl.BlockSpec((1, d, d), lambda l, k: (l, 0, k)),
                  pl.BlockSpec((1, 1, 1, d), lambda l, k: (l, k, 0, 0))],
        out_specs=pl.BlockSpec((1, 1, b, d), lambda l, k: (l, k, 0, 0)),
        out_shape=jax.ShapeDtypeStruct((depth, 6, b, d), F32),
        compiler_params=_params("arbitrary", "arbitrary"),
        name="ada_modulation",
    )(c, w_ada, b_ada.reshape(depth, 6, 1, d))
    return out.transpose(0, 2, 1, 3)


IN_WIDTHS = (QKV_WIDTH, QKV_WIDTH, QKV_WIDTH, QKV_WIDTH, IDX_HEADS * IDX_DIM + LANES, LANES)


def _in_proj_kernel(x_ref, mod_ref, g_ref, w_ref, oa, ob, oc, od, oidx, ow, oc4, oc16, pc_ref):
    h = _modulated_norm(x_ref[...], g_ref[...], mod_ref[0, 0:1, :], mod_ref[0, 1:2, :]).astype(BF16)
    tm = x_ref.shape[0]
    off = 0
    for o_ref, width in zip((oa, ob, oc, od, oidx, ow), IN_WIDTHS):
        val = jnp.dot(h, w_ref[:, off:off + width], preferred_element_type=F32)
        o_ref[...] = val.astype(o_ref.dtype)
        if o_ref is oc:
            for s in range(QKV_WIDTH // LANES):
                pc_ref[s * tm:(s + 1) * tm, :] = val[:, s * LANES:(s + 1) * LANES]
        off += width
    for r, view in zip(DILATIONS[1:], (oc4, oc16)):
        for c in range(r):
            for s in range(QKV_WIDTH // LANES):
                col = c * QKV_WIDTH + s * LANES
                view[:, col:col + LANES] = pc_ref[pl.ds(s * tm + c, tm // r, stride=r), :].astype(view.dtype)


def _pack_w_in(w_in):
    d = w_in.shape[0]
    n_qkv = 4 * QKV_WIDTH
    n_qi = IDX_HEADS * IDX_DIM
    z = lambda n: jnp.zeros((d, n), w_in.dtype)
    return jnp.concatenate([w_in[:, :n_qkv + n_qi + IDX_DIM], z(LANES - IDX_DIM),
                            w_in[:, n_qkv + n_qi + IDX_DIM:], z(LANES - IDX_HEADS)], axis=1).astype(BF16)


def _in_projection(x2, mod, g_mix, w_packed, seq):
    t, d = x2.shape
    tm = min(ROW_TILE, seq)
    per_batch = seq // tm
    dtypes = (BF16, BF16, BF16, BF16, BF16, F32)
    views = DILATIONS[1:]
    return pl.pallas_call(
        _in_proj_kernel,
        grid=(t // tm,),
        in_specs=[pl.BlockSpec((tm, d), lambda i: (i, 0)),
                  pl.BlockSpec((1, 6, d), lambda i: (i // per_batch, 0, 0)),
                  pl.BlockSpec((1, d), lambda i: (0, 0)),
                  pl.BlockSpec(w_packed.shape, lambda i: (0, 0))],
        out_specs=([pl.BlockSpec((tm, w), lambda i: (i, 0)) for w in IN_WIDTHS]
                   + [pl.BlockSpec((tm // r, r * QKV_WIDTH), lambda i: (i, 0)) for r in views]),
        out_shape=([jax.ShapeDtypeStruct((t, w), dt) for w, dt in zip(IN_WIDTHS, dtypes)]
                   + [jax.ShapeDtypeStruct((t // r, r * QKV_WIDTH), BF16) for r in views]),
        scratch_shapes=[pltpu.VMEM((tm * (QKV_WIDTH // LANES), LANES), F32)],
        compiler_params=_params("arbitrary"),
        name="in_projection",
    )(x2, mod, g_mix.reshape(1, d), w_packed)


def _sb_kernel(q_ref, k_ref, v_ref, g_ref, o_ref):
    tq = q_ref.shape[1]
    i = pl.program_id(1)
    key_row = lax.broadcasted_iota(jnp.int32, (tq, 1), 0)
    q_lane = lax.broadcasted_iota(jnp.int32, (1, tq), 1)
    later = (lax.broadcasted_iota(jnp.int32, (tq, tq), 1) > lax.broadcasted_iota(jnp.int32, (tq, tq), 0))
    later = jnp.where(later, 1.0, 0.0).astype(BF16)

    qs = [q_ref[0, :, _head_cols(h)] * ATTN_SCALE for h in range(HEADS_PER_MIXER)]

    heads = range(HEADS_PER_MIXER)

    def blocks(starts, keeps, tails):
        z = jnp.concatenate([lax.dot_general(k_ref[0, pl.ds(st, tq), _head_cols(h)], qs[h], _NT,
                                             preferred_element_type=F32) for st in starts for h in heads], axis=1)
        softplus = jnp.maximum(z, 0.0) + jnp.log(1.0 + jnp.exp(-jnp.abs(z)))
        keep = None
        if any(kp is not None for kp in keeps):
            ones = jnp.ones((tq, tq), F32)
            keep = jnp.concatenate([ones if kp is None else kp for kp in keeps for _ in heads], axis=1)
        log_1m = -softplus if keep is None else -softplus * keep
        hi = log_1m.astype(BF16)
        lo = (log_1m - hi.astype(F32)).astype(BF16)
        inside = jnp.dot(later, hi, preferred_element_type=F32) + jnp.dot(later, lo, preferred_element_type=F32)
        block_sum = jnp.sum(log_1m, axis=0, keepdims=True)
        piece = lambda x, b, h: x[:, (b * HEADS_PER_MIXER + h) * tq:(b * HEADS_PER_MIXER + h + 1) * tq]
        tail_rows = []
        tails = list(tails)
        for b in range(len(starts)):
            tail_rows += tails
            tails = [tails[h] + piece(block_sum, b, h) for h in heads]
        a = jnp.exp(z - softplus + inside + jnp.concatenate(tail_rows, axis=1))
        if keep is not None:
            a = a * keep
        a = a.astype(BF16)
        av = [sum(lax.dot_general(v_ref[0, pl.ds(st, tq), _head_cols(h)], piece(a, b, h), _TN,
                                  preferred_element_type=F32) for b, st in enumerate(starts)) for h in heads]
        return tails, av

    starts = [pl.multiple_of(jnp.maximum(i - n, 0) * tq, tq) for n in range(SB_FIRST_BLOCKS)]
    keeps = [jnp.where(key_row < q_lane, 1.0, 0.0)]
    keeps += [jnp.full((tq, tq), jnp.where(i >= n, 1.0, 0.0), F32) for n in range(1, SB_FIRST_BLOCKS)]
    tails, accs = blocks(starts, keeps, [jnp.zeros((1, tq), F32)] * HEADS_PER_MIXER)
    state = tuple(zip(tails, accs))

    def body(carry):
        j, state = carry
        tails, av = blocks([pl.multiple_of(j * tq, tq)], [None], [tail for tail, _ in state])
        return j - 1, tuple((tails[h], state[h][1] + av[h]) for h in heads)

    def cond(carry):
        j, state = carry
        worst = functools.reduce(jnp.maximum, [tail for tail, _ in state])
        return jnp.logical_and(j >= 0, jnp.max(worst) > -SB_UNDERFLOW)

    _, state = lax.while_loop(cond, body, (i - SB_FIRST_BLOCKS, state))
    outs = []
    for h in range(HEADS_PER_MIXER):
        o = state[h][1]
        outs.append(o * lax.rsqrt(jnp.mean(o * o, axis=0, keepdims=True) + NORM_EPS) * g_ref[:, h:h + 1])
    o_ref[0] = jnp.concatenate(outs, axis=0).T.astype(o_ref.dtype)


def _stick_breaking_mixer(p, g_heads):
    b, seq, _ = p.shape
    tq = min(Q_TILE, seq)
    return pl.pallas_call(
        _sb_kernel,
        grid=(b, seq // tq),
        in_specs=[pl.BlockSpec((1, tq, MIXER_WIDTH), lambda bi, i: (bi, i, 0)),
                  pl.BlockSpec((1, seq, MIXER_WIDTH), lambda bi, i: (bi, 0, 1)),
                  pl.BlockSpec((1, seq, MIXER_WIDTH), lambda bi, i: (bi, 0, 2)),
                  pl.BlockSpec((HEAD_DIM, HEADS_PER_MIXER), lambda bi, i: (0, 0))],
        out_specs=pl.BlockSpec((1, tq, MIXER_WIDTH), lambda bi, i: (bi, i, 0)),
        out_shape=jax.ShapeDtypeStruct((b, seq, MIXER_WIDTH), BF16),
        compiler_params=_params("arbitrary", "arbitrary"),
        name="stick_breaking",
    )(p, p, p, g_heads.T)


def _dsa_kernel(q_ref, k_ref, v_ref, qi_ref, ki_ref, w_ref, g_ref, o_ref, sc_ref, sc16_ref, s_ref, p_ref, *, slopes):
    tq = q_ref.shape[1]
    tk = sc_ref.shape[1]
    i = pl.program_id(1)
    n_blocks = ((i + 1) * tq + tk - 1) // tk
    qpos = i * tq + lax.broadcasted_iota(jnp.int32, (1, tq), 1)
    key_row = lax.broadcasted_iota(jnp.int32, (tk, 1), 0)
    neg_inf = float("-inf")

    w_t = w_ref[0].T * (IDX_HEADS ** -0.5 * IDX_DIM ** -0.5)
    w_rows = [w_t[h:h + 1, :] for h in range(IDX_HEADS)]
    qi = qi_ref[0]

    def top_half(x):
        bits = lax.bitcast_convert_type(x, jnp.int32) & jnp.int32(-65536)
        return lax.bitcast_convert_type(bits, F32).astype(BF16)

    def score_body(j, carry, slot):
        start = pl.multiple_of(j * tk, tk)
        ki = ki_ref[0, pl.ds(start, tk), 0:IDX_DIM]
        sc = jnp.zeros((tk, tq), F32)
        for h in range(IDX_HEADS):
            x = lax.dot_general(ki, qi[:, h * IDX_DIM:(h + 1) * IDX_DIM], _NT, preferred_element_type=F32)
            sc = sc + w_rows[h] * jnp.maximum(x, 0.0)
        sc = jnp.where((start + key_row) <= qpos, sc + 0.0, neg_inf)
        sc_ref[j] = sc
        sc16_ref[j] = top_half(sc)
        return carry

    _paired_loop(n_blocks, score_body, 0)

    def count(pred):
        def add_block(j, cnt, slot):
            c = jnp.where(pred(sc_ref[j]), 1.0, 0.0)
            return cnt + jnp.sum(c.reshape(tk // SUBLANES, SUBLANES, tq), axis=0)

        cnt = _paired_loop(n_blocks, add_block, jnp.zeros((SUBLANES, tq), F32))
        return jnp.sum(cnt, axis=0, keepdims=True)

    int_min = jnp.int32(-2 ** 31)

    def ordered_to_float(u):
        key = u ^ int_min
        bits = jnp.where(key >= 0, key, key ^ jnp.int32(0x7FFFFFFF))
        return lax.bitcast_convert_type(bits, F32)

    def count_top_half(cand16):
        def add_block(j, cnt, slot):
            c = jnp.where(sc16_ref[j] >= cand16, jnp.ones((), BF16), jnp.zeros((), BF16))
            for g in range(tk // BF16_ROWS):
                cnt = cnt + c[g * BF16_ROWS:(g + 1) * BF16_ROWS]
            return cnt

        cnt = _paired_loop(n_blocks, add_block, jnp.zeros((BF16_ROWS, tq), BF16))
        return jnp.sum(cnt.astype(F32), axis=0, keepdims=True)

    def search_body(step, carry, half):
        prefix, n_at_prefix = carry
        cand = prefix | jnp.left_shift(jnp.int32(1), 31 - step)
        cand_f = ordered_to_float(cand)
        n_ge = count_top_half(top_half(cand_f)) if half else count(lambda s: s >= cand_f)
        keep = n_ge >= DSA_TOPK
        return jnp.where(keep, cand, prefix), jnp.where(keep, n_ge, n_at_prefix)

    keep_all = (qpos + 1) <= DSA_TOPK

    carry = (jnp.zeros((1, tq), jnp.int32), jnp.zeros((1, tq), F32))
    carry = lax.fori_loop(0, 16, functools.partial(search_body, half=True), carry)
    prefix, n_ge_tau = lax.fori_loop(16, 32, functools.partial(search_body, half=False), carry)
    tau = jnp.where(keep_all, float(np.finfo(np.float32).min), ordered_to_float(prefix))
    no_ties = jnp.min(jnp.where(jnp.logical_or(keep_all, n_ge_tau == DSA_TOPK), 1.0, 0.0)) > 0.5

    earlier = (lax.broadcasted_iota(jnp.int32, (tk, tk), 1) < lax.broadcasted_iota(jnp.int32, (tk, tk), 0))
    earlier = jnp.where(earlier, 1.0, 0.0).astype(BF16)
    qs = [q_ref[0, :, _head_cols(h)] * ATTN_SCALE for h in range(HEADS_PER_MIXER)]
    key_bias = [slopes[h] * lax.broadcasted_iota(jnp.int32, (tk, tq), 0).astype(F32)
                for h in range(HEADS_PER_MIXER)]

    def attn_body(j, carry, slot, n_ties=None):
        ties_seen, stats = carry
        s_slot = [s_ref.at[slot * HEADS_PER_MIXER + h] for h in range(HEADS_PER_MIXER)]
        p_slot = [p_ref.at[slot * HEADS_PER_MIXER + h] for h in range(HEADS_PER_MIXER)]
        start = pl.multiple_of(j * tk, tk)
        sc = sc_ref[j]
        if n_ties is None:
            sel = sc >= tau
        else:
            tie = jnp.where(sc == tau, 1.0, 0.0)
            rank = jnp.dot(earlier, tie.astype(BF16), preferred_element_type=F32) + ties_seen
            sel = jnp.where(sc > tau, 1.0, jnp.where(rank < n_ties, tie, 0.0)) > 0.5
            ties_seen = ties_seen + jnp.sum(tie, axis=0, keepdims=True)
        block_pos = (j * tk - i * tq).astype(F32)
        for h in range(HEADS_PER_MIXER):
            k = k_ref[0, pl.ds(start, tk), _head_cols(h)]
            s = lax.dot_general(k, qs[h], _NT, preferred_element_type=F32) + key_bias[h]
            s_slot[h][...] = jnp.where(sel, s, MASKED)
        locals_ = []
        for h in range(HEADS_PER_MIXER):
            s = s_slot[h][...]
            m_loc = jnp.max(s, axis=0, keepdims=True)
            p = jnp.exp(s - m_loc)
            locals_.append((m_loc, jnp.sum(p, axis=0, keepdims=True)))
            p_slot[h][...] = p.astype(BF16)
        new_stats = []
        for h in range(HEADS_PER_MIXER):
            m, l, acc = stats[h]
            m_loc, l_blk = locals_[h]
            v = v_ref[0, pl.ds(start, tk), _head_cols(h)]
            pv_blk = lax.dot_general(v, p_slot[h][...], _TN, preferred_element_type=F32)
            m_blk = m_loc + slopes[h] * block_pos
            m_new = jnp.maximum(m, m_blk)
            w_old = jnp.exp(m - m_new)
            w_blk = jnp.where(m_loc > 0.5 * MASKED, jnp.exp(m_blk - m_new), 0.0)
            new_stats.append((m_new, w_old * l + w_blk * l_blk, w_old * acc + w_blk * pv_blk))
        return ties_seen, tuple(new_stats)

    init = tuple((jnp.full((1, tq), MASKED, F32), jnp.zeros((1, tq), F32), jnp.zeros((HEAD_DIM, tq), F32))
                 for _ in range(HEADS_PER_MIXER))
    start_carry = (jnp.zeros((1, tq), F32), init)

    def attend_without_ties():
        return _paired_loop(n_blocks, attn_body, start_carry)[1]

    def attend_with_ties():
        n_gt = count(lambda s: s > tau)
        n_ties = jnp.where(keep_all, 1e9, DSA_TOPK - n_gt)
        return _paired_loop(n_blocks, functools.partial(attn_body, n_ties=n_ties), start_carry)[1]

    stats = lax.cond(no_ties, attend_without_ties, attend_with_ties)
    outs = []
    for h in range(HEADS_PER_MIXER):
        _, l, acc = stats[h]
        o = acc / l
        outs.append(o * lax.rsqrt(jnp.mean(o * o, axis=0, keepdims=True) + NORM_EPS) * g_ref[:, h:h + 1])
    o_ref[0] = jnp.concatenate(outs, axis=0).T.astype(o_ref.dtype)


def _dsa_mixer(p, p_idx, p_w, g_heads):
    b, seq, _ = p.shape
    tq = min(4 * Q_TILE, seq)
    tk = tq
    assert seq // BF16_ROWS <= BF16_EXACT_COUNT, "the packed-bf16 count accumulators would stop being exact"
    qi_width = IDX_HEADS * IDX_DIM
    return pl.pallas_call(
        functools.partial(_dsa_kernel, slopes=_alibi_slopes(0)),
        grid=(b, seq // tq),
        in_specs=[pl.BlockSpec((1, tq, MIXER_WIDTH), lambda bi, i: (bi, i, 0)),
                  pl.BlockSpec((1, seq, MIXER_WIDTH), lambda bi, i: (bi, 0, 1)),
                  pl.BlockSpec((1, seq, MIXER_WIDTH), lambda bi, i: (bi, 0, 2)),
                  pl.BlockSpec((1, tq, qi_width), lambda bi, i: (bi, i, 0)),
                  pl.BlockSpec((1, seq, LANES), lambda bi, i: (bi, 0, qi_width // LANES)),
                  pl.BlockSpec((1, tq, LANES), lambda bi, i: (bi, i, 0)),
                  pl.BlockSpec((HEAD_DIM, HEADS_PER_MIXER), lambda bi, i: (0, 0))],
        out_specs=pl.BlockSpec((1, tq, MIXER_WIDTH), lambda bi, i: (bi, i, 0)),
        out_shape=jax.ShapeDtypeStruct((b, seq, MIXER_WIDTH), BF16),
        scratch_shapes=[pltpu.VMEM((seq // tk, tk, tq), F32),
                        pltpu.VMEM((seq // tk, tk, tq), BF16),
                        pltpu.VMEM((2 * HEADS_PER_MIXER, tk, tq), F32),
                        pltpu.VMEM((2 * HEADS_PER_MIXER, tk, tq), BF16)],
        compiler_params=_params("arbitrary", "arbitrary"),
        name="dsa",
    )(p, p, p, p_idx, p_idx, p_w, g_heads.T)


def _band_kernel(q_ref, kp_ref, kc_ref, vp_ref, vc_ref, o_ref, lse_ref, *, dilation, slopes):
    tq = q_ref.shape[1]
    ui = pl.program_id(2)
    heads = range(HEADS_PER_MIXER)
    u_q = jnp.concatenate([ui * tq + lax.broadcasted_iota(jnp.int32, (1, tq), 1)] * HEADS_PER_MIXER, axis=1)
    u_k = ui * tq - DILATED_STEPS + lax.broadcasted_iota(jnp.int32, (DILATED_STEPS + tq, 1), 0)
    steps = u_q - u_k
    valid = jnp.logical_and(jnp.logical_and(steps >= 0, steps <= DILATED_STEPS), u_k >= 0)
    slope_row = jnp.concatenate([jnp.full((1, tq), slopes[h], F32) for h in heads], axis=1)
    s = jnp.concatenate(
        [lax.dot_general(jnp.concatenate([kp_ref[0, :, _head_cols(h)], kc_ref[0, :, _head_cols(h)]], axis=0),
                         q_ref[0, :, _head_cols(h)], _NT, preferred_element_type=F32) for h in heads], axis=1)
    s = jnp.where(valid, s * ATTN_SCALE - slope_row * (steps * dilation).astype(F32), MASKED)
    m = jnp.max(s, axis=0, keepdims=True)
    e = jnp.exp(s - m)
    den = jnp.sum(e, axis=0, keepdims=True)
    p = e.astype(BF16)
    lse = m + jnp.log(den)
    outs = []
    for h in heads:
        cols = slice(h * tq, (h + 1) * tq)
        v = jnp.concatenate([vp_ref[0, :, _head_cols(h)], vc_ref[0, :, _head_cols(h)]], axis=0)
        outs.append(lax.dot_general(v, p[:, cols], _TN, preferred_element_type=F32) / den[:, cols])
    o_ref[0] = jnp.concatenate(outs, axis=0).T
    lse_rows = jnp.concatenate([lse[:, h * tq:(h + 1) * tq] for h in heads]
                               + [jnp.zeros((LANES - HEADS_PER_MIXER, tq), F32)], axis=0)
    lse_ref[0] = lse_rows.T


def _dilated_branch(view, dilation, slopes):
    b, length, _ = view.shape
    classes = dilation
    tq = min(2 * Q_TILE, length)
    back = tq // DILATED_STEPS
    spec = lambda part, prev: (
        pl.BlockSpec((1, DILATED_STEPS, MIXER_WIDTH),
                     lambda bi, c, ui: (bi, jnp.maximum(ui * back - 1, 0), c * 3 + part)) if prev
        else pl.BlockSpec((1, tq, MIXER_WIDTH), lambda bi, c, ui: (bi, ui, c * 3 + part)))
    out, lse = pl.pallas_call(
        functools.partial(_band_kernel, dilation=dilation, slopes=slopes),
        grid=(b, classes, length // tq),
        in_specs=[spec(0, False), spec(1, True), spec(1, False), spec(2, True), spec(2, False)],
        out_specs=[pl.BlockSpec((1, tq, MIXER_WIDTH), lambda bi, c, ui: (bi, ui, c)),
                   pl.BlockSpec((1, tq, LANES), lambda bi, c, ui: (bi, ui, c))],
        out_shape=[jax.ShapeDtypeStruct((b, length, classes * MIXER_WIDTH), F32),
                   jax.ShapeDtypeStruct((b, length, classes * LANES), F32)],
        compiler_params=_params("arbitrary", "arbitrary", "arbitrary"),
        name=f"dilated_r{dilation}",
    )(view, view, view, view, view)
    return out, lse


def _merge_dilated(o_refs, l_refs, g_ref, scratch, rows):
    chunks = MIXER_WIDTH // LANES
    outs = [[o_refs[0][:, s * LANES:(s + 1) * LANES] for s in range(chunks)]]
    lses = [l_refs[0][...]]
    for r, o_view, l_view, o_nat, l_nat in zip(DILATIONS[1:], o_refs[1:], l_refs[1:], scratch[0::2], scratch[1::2]):
        for c in range(r):
            for s in range(chunks):
                col = c * MIXER_WIDTH + s * LANES
                o_nat[pl.ds(s * rows + c, rows // r, stride=r), :] = o_view[:, col:col + LANES]
            l_nat[pl.ds(c, rows // r, stride=r), :] = l_view[:, c * LANES:(c + 1) * LANES]
        outs.append([o_nat[s * rows:(s + 1) * rows, :] for s in range(chunks)])
        lses.append(l_nat[...])
    top = functools.reduce(jnp.maximum, lses)
    wts = [jnp.exp(l - top) for l in lses]
    def head_of(index):
        return sum(jnp.where(index >= h * HEAD_DIM, 1, 0) for h in range(1, HEADS_PER_MIXER))

    head_of_col = head_of(lax.broadcasted_iota(jnp.int32, (LANES, MIXER_WIDTH), 1))
    spread = jnp.where(lax.broadcasted_iota(jnp.int32, (LANES, MIXER_WIDTH), 0) == head_of_col, 1.0, 0.0)
    same_head = (head_of(lax.broadcasted_iota(jnp.int32, (MIXER_WIDTH, MIXER_WIDTH), 0))
                 == head_of(lax.broadcasted_iota(jnp.int32, (MIXER_WIDTH, MIXER_WIDTH), 1)))
    head_mean = jnp.where(same_head, 1.0 / HEAD_DIM, 0.0)

    def times(x, m):
        hi = x.astype(BF16)
        lo = (x - hi.astype(F32)).astype(BF16)
        m = m.astype(BF16)
        return jnp.dot(hi, m, preferred_element_type=F32) + jnp.dot(lo, m, preferred_element_type=F32)

    w_cols = [times(w, spread) for w in wts]
    full = [jnp.concatenate(o, axis=1) for o in outs]
    mixed = sum(w * o for w, o in zip(w_cols, full)) / sum(w_cols)
    return mixed * lax.rsqrt(times(mixed * mixed, head_mean) + NORM_EPS) * g_ref[...]


def _dilated_branches(views):
    slopes = _alibi_slopes(1)
    branches = [_dilated_branch(v, r, slopes) for v, r in zip(views, DILATIONS)]
    flat = lambda a: a.reshape(-1, a.shape[-1])
    return [flat(o) for o, _ in branches], [flat(l) for _, l in branches]


def _moba_kernel(q_ref, k_ref, v_ref, g_ref, o_ref, kmean_ref, chosen_ref, s_ref, p_ref, *, slopes):
    tq = q_ref.shape[1]
    n_kv = k_ref.shape[1] // MOBA_BLOCK
    own = pl.program_id(1)

    @pl.when(own == 0)
    def _():
        kmean_ref[...] = jnp.zeros_like(kmean_ref)
        for n in range(n_kv):
            blk = k_ref[0, n * MOBA_BLOCK:(n + 1) * MOBA_BLOCK, :].astype(F32)
            kmean_ref[n:n + 1, :] = jnp.sum(blk, axis=0, keepdims=True) * (1.0 / MOBA_BLOCK)

    key_row = lax.broadcasted_iota(jnp.int32, (MOBA_BLOCK, 1), 0)
    q_lane = lax.broadcasted_iota(jnp.int32, (1, tq), 1)
    blk_rows = kmean_ref.shape[0]
    blk_row = lax.broadcasted_iota(jnp.int32, (blk_rows, 1), 0)
    blk_f = blk_row.astype(F32)
    neg_inf = float("-inf")
    group = tq // MOBA_BLOCK
    first_own = own * group
    sub_block = sum(jnp.where(q_lane >= g * MOBA_BLOCK, 1, 0) for g in range(1, group)) if group > 1 else 0
    own_blk = first_own + sub_block
    q_local = q_lane - sub_block * MOBA_BLOCK

    qs = []
    for h in range(HEADS_PER_MIXER):
        q = q_ref[0, :, _head_cols(h)]
        gate = lax.dot_general(kmean_ref[:, _head_cols(h)], q.astype(F32), _NT, preferred_element_type=F32,
                               precision=lax.Precision.HIGHEST)
        gate = jnp.where(blk_row < own_blk, gate, neg_inf)
        picks = jnp.zeros((blk_rows, tq), F32)
        for _ in range(MOBA_TOPK):
            top = jnp.max(gate, axis=0, keepdims=True)
            is_top = jnp.logical_and(gate == top, top > neg_inf)
            first = jnp.min(jnp.where(is_top, blk_f, float(blk_rows)), axis=0, keepdims=True)
            pick = blk_f == first
            picks = jnp.where(pick, 1.0, picks)
            gate = jnp.where(pick, neg_inf, gate)
        chosen_ref[h] = picks
        qs.append(q * ATTN_SCALE)

    key_bias = [slopes[h] * lax.broadcasted_iota(jnp.int32, (MOBA_BLOCK, tq), 0).astype(F32)
                for h in range(HEADS_PER_MIXER)]

    def block_softmax(start, keep=None, slot=0):
        s_slot = [s_ref.at[slot * HEADS_PER_MIXER + h] for h in range(HEADS_PER_MIXER)]
        p_slot = [p_ref.at[slot * HEADS_PER_MIXER + h] for h in range(HEADS_PER_MIXER)]
        for h in range(HEADS_PER_MIXER):
            k = k_ref[0, pl.ds(start, MOBA_BLOCK), _head_cols(h)]
            s = lax.dot_general(k, qs[h], _NT, preferred_element_type=F32) + key_bias[h]
            s_slot[h][...] = s if keep is None else jnp.where(keep, s, MASKED)
        pieces = []
        for h in range(HEADS_PER_MIXER):
            s = s_slot[h][...]
            m_loc = jnp.max(s, axis=0, keepdims=True)
            p = jnp.exp(s - m_loc)
            pieces.append((m_loc, jnp.sum(p, axis=0, keepdims=True)))
            p_slot[h][...] = p.astype(BF16)
        out = []
        for h in range(HEADS_PER_MIXER):
            v = v_ref[0, pl.ds(start, MOBA_BLOCK), _head_cols(h)]
            out.append(pieces[h] + (lax.dot_general(v, p_slot[h][...], _TN, preferred_element_type=F32),))
        return out

    def merge(stats, j, pieces, use):
        block_pos = ((j - first_own) * MOBA_BLOCK).astype(F32)
        merged = []
        for h, (m_loc, l_blk, pv_blk) in enumerate(pieces):
            m_blk = jnp.where(use[h], m_loc + slopes[h] * block_pos, MASKED)
            m, l, acc = stats[h]
            m_new = jnp.maximum(m, m_blk)
            w_old = jnp.exp(m - m_new)
            w_blk = jnp.where(use[h], jnp.exp(m_blk - m_new), 0.0)
            merged.append((m_new, w_old * l + w_blk * l_blk, w_old * acc + w_blk * pv_blk))
        return tuple(merged)

    def picked(h, j):
        return chosen_ref[h, pl.ds(j, 1), :] > 0.5

    def body(j, stats, slot):
        start = pl.multiple_of(j * MOBA_BLOCK, MOBA_BLOCK)
        return merge(stats, j, block_softmax(start, slot=slot), [picked(h, j) for h in range(HEADS_PER_MIXER)])

    stats = tuple((jnp.full((1, tq), MASKED, F32), jnp.zeros((1, tq), F32), jnp.zeros((HEAD_DIM, tq), F32))
                  for _ in range(HEADS_PER_MIXER))
    stats = _paired_loop(first_own, body, stats)
    for g in range(group):
        j = first_own + g
        keep = jnp.logical_or(own_blk > j, jnp.logical_and(own_blk == j, key_row <= q_local))
        pieces = block_softmax(pl.multiple_of(j * MOBA_BLOCK, MOBA_BLOCK), keep=keep, slot=g % 2)
        use = [jnp.logical_or(own_blk == j, jnp.logical_and(own_blk > j, picked(h, j)))
               for h in range(HEADS_PER_MIXER)]
        stats = merge(stats, j, pieces, use)
    outs = []
    for h in range(HEADS_PER_MIXER):
        _, l, acc = stats[h]
        o = acc / l
        outs.append(o * lax.rsqrt(jnp.mean(o * o, axis=0, keepdims=True) + NORM_EPS) * g_ref[:, h:h + 1])
    o_ref[0] = jnp.concatenate(outs, axis=0).T.astype(o_ref.dtype)


def _moba_mixer(p, g_heads):
    b, seq, _ = p.shape
    tq = MOBA_GROUP * MOBA_BLOCK if seq % (MOBA_GROUP * MOBA_BLOCK) == 0 else MOBA_BLOCK
    blk_rows =-(-(seq // MOBA_BLOCK) // SUBLANES) * SUBLANES
    return pl.pallas_call(
        functools.partial(_moba_kernel, slopes=_alibi_slopes(2)),
        grid=(b, seq // tq),
        in_specs=[pl.BlockSpec((1, tq, MIXER_WIDTH), lambda bi, i: (bi, i, 0)),
                  pl.BlockSpec((1, seq, MIXER_WIDTH), lambda bi, i: (bi, 0, 1)),
                  pl.BlockSpec((1, seq, MIXER_WIDTH), lambda bi, i: (bi, 0, 2)),
                  pl.BlockSpec((HEAD_DIM, HEADS_PER_MIXER), lambda bi, i: (0, 0))],
        out_specs=pl.BlockSpec((1, tq, MIXER_WIDTH), lambda bi, i: (bi, i, 0)),
        out_shape=jax.ShapeDtypeStruct((b, seq, MIXER_WIDTH), BF16),
        scratch_shapes=[pltpu.VMEM((blk_rows, MIXER_WIDTH), F32),
                        pltpu.VMEM((HEADS_PER_MIXER, blk_rows, tq), F32),
                        pltpu.VMEM((2 * HEADS_PER_MIXER, MOBA_BLOCK, tq), F32),
                        pltpu.VMEM((2 * HEADS_PER_MIXER, MOBA_BLOCK, tq), BF16)],
        compiler_params=_params("arbitrary", "arbitrary"),
        name="moba",
    )(p, p, p, g_heads.T)


def _out_proj_kernel(oa, ob, od, o1, o4, o16, l1, l4, l16, gc_ref, w_ref, x_ref, mod_ref, o_ref, *scratch):
    rows = x_ref.shape[0]
    oc = _merge_dilated((o1, o4, o16), (l1, l4, l16), gc_ref, scratch, rows).astype(BF16)
    acc = jnp.zeros(x_ref.shape, F32)
    for m, o in enumerate((oa[...], ob[...], oc, od[...])):
        acc = acc + jnp.dot(o, w_ref[m * MIXER_WIDTH:(m + 1) * MIXER_WIDTH, :], preferred_element_type=F32)
    o_ref[...] = x_ref[...] + mod_ref[0, 2:3, :] * acc


def _out_projection(o_a, o_b, o_d, dilated, g_heads_c, w_out, x2, mod, seq):
    t, d = x2.shape
    tm = min(ROW_TILE, seq)
    per_batch = seq // tm
    o_spec = pl.BlockSpec((tm, MIXER_WIDTH), lambda i: (i, 0))
    branch_outs, branch_lses = dilated
    view_specs = ([pl.BlockSpec((tm // r, r * MIXER_WIDTH), lambda i: (i, 0)) for r in DILATIONS]
                  + [pl.BlockSpec((tm // r, r * LANES), lambda i: (i, 0)) for r in DILATIONS])
    scratch = []
    for _ in DILATIONS[1:]:
        scratch += [pltpu.VMEM((tm * (MIXER_WIDTH // LANES), LANES), F32), pltpu.VMEM((tm, LANES), F32)]
    return pl.pallas_call(
        _out_proj_kernel,
        grid=(t // tm,),
        in_specs=[o_spec] * 3 + view_specs + [pl.BlockSpec((1, MIXER_WIDTH), lambda i: (0, 0)),
                                              pl.BlockSpec(w_out.shape, lambda i: (0, 0)),
                                              pl.BlockSpec((tm, d), lambda i: (i, 0)),
                                              pl.BlockSpec((1, 6, d), lambda i: (i // per_batch, 0, 0))],
        out_specs=pl.BlockSpec((tm, d), lambda i: (i, 0)),
        out_shape=jax.ShapeDtypeStruct((t, d), F32),
        scratch_shapes=scratch,
        compiler_params=_params("arbitrary"),
        name="out_projection",
    )(*[o.reshape(t, MIXER_WIDTH) for o in (o_a, o_b, o_d)], *branch_outs, *branch_lses,
      g_heads_c.reshape(1, MIXER_WIDTH), w_out, x2, mod)


def _rms_gain(x, gain):
    return x * lax.rsqrt(jnp.mean(x * x, axis=-1, keepdims=True) + NORM_EPS) * gain


def _ffn_kernel(x_ref, mod_ref, g_ref, fg_ref, wg_ref, wu_ref, wd_ref, o_ref, h_ref, acc_ref, *, final_norm):
    f = pl.program_id(1)

    @pl.when(f == 0)
    def _():
        h_ref[...] = _modulated_norm(x_ref[...], g_ref[...], mod_ref[0, 3:4, :], mod_ref[0, 4:5, :]).astype(BF16)
        acc_ref[...] = jnp.zeros_like(acc_ref)

    h = h_ref[...]
    gate = jnp.dot(h, wg_ref[...], preferred_element_type=F32)
    up = jnp.dot(h, wu_ref[...], preferred_element_type=F32)
    act = (gate / (1.0 + jnp.exp(-gate)) * up).astype(BF16)
    acc_ref[...] += jnp.dot(act, wd_ref[...], preferred_element_type=F32)

    @pl.when(f == pl.num_programs(1) - 1)
    def _():
        out = x_ref[...] + mod_ref[0, 5:6, :] * acc_ref[...]
        o_ref[...] = _rms_gain(out, fg_ref[...]) if final_norm else out


def _dense_ffn(x2, mod, g_ffn, wg, wu, wd, seq, g_final, final_norm):
    t, d = x2.shape
    d_ff = wg.shape[1]
    tm = min(FFN_ROWS, seq)
    tf = FFN_TILE if d_ff % FFN_TILE == 0 else d_ff
    per_batch = seq // tm
    return pl.pallas_call(
        functools.partial(_ffn_kernel, final_norm=final_norm),
        grid=(t // tm, d_ff // tf),
        in_specs=[pl.BlockSpec((tm, d), lambda i, f: (i, 0)),
                  pl.BlockSpec((1, 6, d), lambda i, f: (i // per_batch, 0, 0)),
                  pl.BlockSpec((1, d), lambda i, f: (0, 0)),
                  pl.BlockSpec((1, d), lambda i, f: (0, 0)),
                  pl.BlockSpec((d, tf), lambda i, f: (0, f)),
                  pl.BlockSpec((d, tf), lambda i, f: (0, f)),
                  pl.BlockSpec((tf, d), lambda i, f: (f, 0))],
        out_specs=pl.BlockSpec((tm, d), lambda i, f: (i, 0)),
        out_shape=jax.ShapeDtypeStruct((t, d), F32),
        scratch_shapes=[pltpu.VMEM((tm, d), BF16), pltpu.VMEM((tm, d), F32)],
        compiler_params=_params("arbitrary", "arbitrary"),
        name="dense_ffn",
    )(x2, mod, g_ffn.reshape(1, d), g_final.reshape(1, d), wg.astype(BF16), wu.astype(BF16), wd.astype(BF16))


def _store_row_tiles(dst_ref, value):
    rows, d = value.shape
    chunks = d // LANES
    for s in range(chunks):
        dst_ref[pl.ds(s, rows, stride=chunks), :] = value[:, s * LANES:(s + 1) * LANES]


def _load_row_tile_chunk(src_ref, s, rows, chunks):
    return src_ref[pl.ds(s, rows, stride=chunks), :]


def _router_kernel(x_ref, mod_ref, g_ref, wr_ref, h_ref, logit_ref):
    h = _modulated_norm(x_ref[...], g_ref[...], mod_ref[0, 3:4, :], mod_ref[0, 4:5, :])
    _store_row_tiles(h_ref, h)
    logit_ref[...] = jnp.dot(h, wr_ref[...], preferred_element_type=F32, precision=lax.Precision.HIGHEST)


def _router(x2, mod, g_ffn, w_router, seq):
    t, d = x2.shape
    tm = min(ROW_TILE, seq)
    per_batch = seq // tm
    wr = jnp.zeros((d, LANES), F32).at[:, :N_EXPERTS].set(w_router.astype(F32))
    return pl.pallas_call(
        _router_kernel,
        grid=(t // tm,),
        in_specs=[pl.BlockSpec((tm, d), lambda i: (i, 0)),
                  pl.BlockSpec((1, 6, d), lambda i: (i // per_batch, 0, 0)),
                  pl.BlockSpec((1, d), lambda i: (0, 0)),
                  pl.BlockSpec((d, LANES), lambda i: (0, 0))],
        out_specs=[pl.BlockSpec((tm * (d // LANES), LANES), lambda i: (i, 0)),
                   pl.BlockSpec((tm, LANES), lambda i: (i, 0))],
        out_shape=[jax.ShapeDtypeStruct((t * (d // LANES), LANES), F32), jax.ShapeDtypeStruct((t, LANES), F32)],
        compiler_params=_params("arbitrary"),
        name="moe_router",
    )(x2, mod, g_ffn.reshape(1, d), wr)


def _row_copy(src_hbm, row, dst_ref, r, sem, chunks):
    src = src_hbm.at[pl.ds(pl.multiple_of(row * chunks, chunks), chunks), :]
    return pltpu.make_async_copy(src, dst_ref.at[pl.ds(pl.multiple_of(r * chunks, chunks), chunks), :], sem)


def _expert_kernel(be_ref, used_ref, cur_idx_ref, nxt_idx_ref, h_hbm, wg_ref, wu_ref, wd_ref, o_ref,
                   xin_ref, xb_ref, acc_ref, sem, *, n_ff_steps):
    m = pl.program_id(0)
    f = pl.program_id(1)
    n_used = used_ref[0]
    live = m < n_used
    rows, d = xb_ref.shape
    chunks = d // LANES

    def row_copies(idx_ref, slot, start):
        if not start:
            half = rows * chunks // 2
            for lane in range(2):
                pltpu.make_async_copy(h_hbm.at[pl.ds(0, half), :], xin_ref.at[slot, pl.ds(0, half), :],
                                      sem.at[slot, lane]).wait()
            return

        def body(group, c):
            for u in range(ROWS_PER_TRIP):
                r = ROWS_PER_TRIP * group + u
                _row_copy(h_hbm, idx_ref[0, 0, r], xin_ref.at[slot], r, sem.at[slot, u % 2],
                          chunks).start(priority=u % 2)
            return c
        lax.fori_loop(0, rows // ROWS_PER_TRIP, body, 0)

    @pl.when(jnp.logical_and(live, f == 0))
    def _():
        @pl.when(m == 0)
        def _():
            row_copies(cur_idx_ref, 0, start=True)

        for slot in range(2):
            @pl.when(m % 2 == slot)
            def _(slot=slot):
                row_copies(cur_idx_ref, slot, start=False)
                for s in range(chunks):
                    xb_ref[:, s * LANES:(s + 1) * LANES] = _load_row_tile_chunk(
                        xin_ref.at[slot], s, rows, chunks).astype(BF16)

        acc_ref[...] = jnp.zeros_like(acc_ref)

    def swiglu_step(start_next):
        h = xb_ref[...]
        if start_next:
            share = rows // n_ff_steps
            nxt = (m + 1) % 2
            for u in range(share):
                r = f * share + u
                _row_copy(h_hbm, nxt_idx_ref[0, 0, r], xin_ref.at[nxt], r, sem.at[nxt, u % 2],
                          chunks).start(priority=u % 2)
        gate = jnp.dot(h, wg_ref[0], preferred_element_type=F32)
        up = jnp.dot(h, wu_ref[0], preferred_element_type=F32)
        act = (gate / (1.0 + jnp.exp(-gate)) * up).astype(BF16)
        acc_ref[...] += jnp.dot(act, wd_ref[0], preferred_element_type=F32)

    next_live = m + 1 < n_used
    pl.when(jnp.logical_and(live, next_live))(functools.partial(swiglu_step, True))
    pl.when(jnp.logical_and(live, jnp.logical_not(next_live)))(functools.partial(swiglu_step, False))

    @pl.when(f == pl.num_programs(1) - 1)
    def _():
        _store_row_tiles(o_ref, jnp.where(live, acc_ref[...], 0.0))


def _expert_ffn(h_rows, slot_tok, block_expert, n_used, wg, wu, wd):
    d = wg.shape[1]
    chunks = d // LANES
    n_slots = slot_tok.shape[0]
    d_ff = wg.shape[2]
    tm = EXPERT_ROWS
    tf = EXPERT_FF_TILE if d_ff % EXPERT_FF_TILE == 0 else d_ff
    nf = d_ff // tf
    nm = n_slots // tm

    def fcol(m, f, used):
        return jnp.where(m < used[0], f, nf - 1)

    idx_spec = lambda shift: pl.BlockSpec((1, 1, tm), lambda m, f, be, used: (jnp.minimum(m + shift, nm - 1), 0, 0),
                                          memory_space=pltpu.SMEM)
    grid_spec = pltpu.PrefetchScalarGridSpec(
        num_scalar_prefetch=2,
        grid=(nm, nf),
        in_specs=[idx_spec(0), idx_spec(1), pl.BlockSpec(memory_space=pl.ANY),
                  pl.BlockSpec((1, d, tf), lambda m, f, be, used: (be[m], 0, fcol(m, f, used))),
                  pl.BlockSpec((1, d, tf), lambda m, f, be, used: (be[m], 0, fcol(m, f, used))),
                  pl.BlockSpec((1, tf, d), lambda m, f, be, used: (be[m], fcol(m, f, used), 0))],
        out_specs=pl.BlockSpec((tm * chunks, LANES), lambda m, f, be, used: (m, 0)),
        scratch_shapes=[pltpu.VMEM((2, tm * chunks, LANES), F32), pltpu.VMEM((tm, d), BF16),
                        pltpu.VMEM((tm, d), F32), pltpu.SemaphoreType.DMA((2, 2))])
    idx = slot_tok.reshape(nm, 1, tm)
    assert tm % (2 * nf) == 0
    return pl.pallas_call(
        functools.partial(_expert_kernel, n_ff_steps=nf),
        grid_spec=grid_spec,
        out_shape=jax.ShapeDtypeStruct((n_slots * chunks, LANES), F32),
        compiler_params=_params("arbitrary", "arbitrary"),
        name="expert_ffn",
    )(block_expert, n_used, idx, idx, h_rows, wg.astype(BF16), wu.astype(BF16), wd.astype(BF16))


def _combine_kernel(d0_ref, d1_ref, ys_hbm, x_ref, gates_ref, mod_ref, fg_ref, o_ref, y0_ref, y1_ref, sem, *,
                    final_norm):
    rows, d = o_ref.shape
    chunks = d // LANES

    def start(group, c):
        for u in range(ROWS_PER_TRIP // 2):
            r = ROWS_PER_TRIP // 2 * group + u
            _row_copy(ys_hbm, d0_ref[0, 0, r], y0_ref, r, sem.at[0], chunks).start(priority=0)
            _row_copy(ys_hbm, d1_ref[0, 0, r], y1_ref, r, sem.at[1], chunks).start(priority=1)
        return c

    lax.fori_loop(0, rows // (ROWS_PER_TRIP // 2), start, 0)
    for k, y_ref in enumerate((y0_ref, y1_ref)):
        pltpu.make_async_copy(ys_hbm.at[pl.ds(0, rows * chunks), :], y_ref, sem.at[k]).wait()
    gates = gates_ref[...]
    for s in range(chunks):
        cols = slice(s * LANES, (s + 1) * LANES)
        y = (_load_row_tile_chunk(y0_ref, s, rows, chunks) * gates[:, 0:1]
             + _load_row_tile_chunk(y1_ref, s, rows, chunks) * gates[:, 1:2])
        o_ref[:, cols] = x_ref[:, cols] + mod_ref[0, 5:6, cols] * y
    if final_norm:
        o_ref[...] = _rms_gain(o_ref[...], fg_ref[...])


def _moe_combine(ys, dest0, dest1, gates, x2, mod, seq, g_final, final_norm):
    t, d = x2.shape
    tm = min(GATHER_ROWS, seq)
    steps = t // tm
    per_batch = seq // tm
    idx_spec = pl.BlockSpec((1, 1, tm), lambda i: (i, 0, 0), memory_space=pltpu.SMEM)
    return pl.pallas_call(
        functools.partial(_combine_kernel, final_norm=final_norm),
        grid=(steps,),
        in_specs=[idx_spec, idx_spec, pl.BlockSpec(memory_space=pl.ANY),
                  pl.BlockSpec((tm, d), lambda i: (i, 0)),
                  pl.BlockSpec((tm, TOP_K_EXPERTS), lambda i: (i, 0)),
                  pl.BlockSpec((1, 6, d), lambda i: (i // per_batch, 0, 0)),
                  pl.BlockSpec((1, d), lambda i: (0, 0))],
        out_specs=pl.BlockSpec((tm, d), lambda i: (i, 0)),
        out_shape=jax.ShapeDtypeStruct((t, d), F32),
        scratch_shapes=[pltpu.VMEM((tm * (d // LANES), LANES), F32), pltpu.VMEM((tm * (d // LANES), LANES), F32),
                        pltpu.SemaphoreType.DMA((2,))],
        compiler_params=_params("arbitrary"),
        name="moe_combine",
    )(dest0.reshape(steps, 1, tm), dest1.reshape(steps, 1, tm), ys, x2, gates, mod, g_final.reshape(1, d))


def _moe_ffn(x2, mod, g_ffn, w_router, wg, wu, wd, seq, g_final, final_norm):
    t, d = x2.shape
    h, logits = _router(x2, mod, g_ffn, w_router, seq)
    top_val, top_idx = lax.top_k(logits[:, :N_EXPERTS], TOP_K_EXPERTS)
    gates = jax.nn.softmax(top_val, axis=-1)

    n_assign = t * TOP_K_EXPERTS
    flat_e = top_idx.reshape(-1).astype(jnp.int32)
    onehot = (flat_e[:, None] == jnp.arange(N_EXPERTS, dtype=jnp.int32)[None, :]).astype(jnp.int32)
    rank = jnp.take_along_axis(jnp.cumsum(onehot, axis=0), flat_e[:, None], axis=1)[:, 0] - 1
    counts = jnp.sum(onehot, axis=0)
    padded = (counts + EXPERT_ROWS - 1) // EXPERT_ROWS * EXPERT_ROWS
    pad_end = jnp.cumsum(padded)
    dest = (pad_end - padded)[flat_e] + rank
    n_slots = (n_assign // EXPERT_ROWS + N_EXPERTS) * EXPERT_ROWS
    n_blocks = n_slots // EXPERT_ROWS
    slot_tok = jnp.zeros((n_slots,), jnp.int32).at[dest].set(jnp.arange(n_assign, dtype=jnp.int32) // TOP_K_EXPERTS)
    block_start = jnp.arange(n_blocks, dtype=jnp.int32) * EXPERT_ROWS
    block_expert = jnp.minimum(jnp.searchsorted(pad_end, block_start, side="right"), N_EXPERTS - 1).astype(jnp.int32)
    n_used = (pad_end[-1:] // EXPERT_ROWS).astype(jnp.int32)

    ys = _expert_ffn(h, slot_tok, block_expert, n_used, wg, wu, wd)
    dest2 = dest.reshape(t, TOP_K_EXPERTS)
    return _moe_combine(ys, dest2[:, 0], dest2[:, 1], gates, x2, mod, seq, g_final, final_norm)


def _token_mixer(x2, mod, g_mix, w_in, g_heads, w_out, batch, seq):
    pa, pb, pc, pd, p_idx, p_w, *pc_views = _in_projection(x2, mod, g_mix, _pack_w_in(w_in), seq)
    shape3 = lambda a: a.reshape(batch, -1, a.shape[-1])
    gh = g_heads.reshape(4, HEADS_PER_MIXER, HEAD_DIM)
    o_a = _stick_breaking_mixer(shape3(pa), gh[0])
    o_b = _dsa_mixer(shape3(pb), shape3(p_idx), shape3(p_w), gh[1])
    dilated = _dilated_branches([shape3(v) for v in [pc] + pc_views])
    o_d = _moba_mixer(shape3(pd), gh[3])
    return _out_projection(o_a, o_b, o_d, dilated, gh[2], w_out.astype(BF16), x2, mod, seq)


def kernel(x, c, w_ada, b_ada, g_mix, w_in, g_heads, w_out, g_ffn, w_ff_gate, w_ff_up, w_ff_down, w_router, w_exp_gate, w_exp_up, w_exp_down, g_final):
    batch, seq, d = x.shape
    depth = w_ada.shape[0]
    mods = _ada_modulation(c, w_ada, b_ada)
    x2 = x.reshape(batch * seq, d)
    for layer in range(depth):
        mod = mods[layer]
        x2 = _token_mixer(x2, mod, g_mix[layer], w_in[layer], g_heads[layer], w_out[layer], batch, seq)
        i = layer // 2
        last = layer == depth - 1
        if layer % 2 == 0:
            x2 = _dense_ffn(x2, mod, g_ffn[layer], w_ff_gate[i], w_ff_up[i], w_ff_down[i], seq, g_final, last)
        else:
            x2 = _moe_ffn(x2, mod, g_ffn[layer], w_router[i], w_exp_gate[i], w_exp_up[i], w_exp_down[i], seq,
                          g_final, last)
    return x2.reshape(batch, seq, d)
```

```python
import functools

import numpy as np
import jax
import jax.numpy as jnp
from jax import lax
from jax.experimental import pallas as pl
from jax.experimental.pallas import tpu as pltpu

F32 = jnp.float32
BF16 = jnp.bfloat16

HEAD_DIM = 64
HEADS_PER_MIXER = 4
MIXER_WIDTH = HEADS_PER_MIXER * HEAD_DIM
QKV_WIDTH = 3 * MIXER_WIDTH
IDX_HEADS = 8
IDX_DIM = 64
DSA_TOPK = 256
DILATIONS = (1, 4, 16)
DILATED_STEPS = 128
MOBA_BLOCK = 256
MOBA_TOPK = 3
MOBA_GROUP = 2
N_EXPERTS = 8
TOP_K_EXPERTS = 2
NORM_EPS = 1e-6
ATTN_SCALE = HEAD_DIM ** -0.5
LOG2E = 1.4426950408889634

LANES = 128
SUBLANES = 8
BF16_ROWS = 2 * SUBLANES
BF16_EXACT_COUNT = 256
Q_TILE = 128
ROW_TILE = 512
FFN_ROWS = 1024
FFN_TILE = 1408
EXPERT_ROWS = 512
EXPERT_FF_TILE = 1792
GATHER_ROWS = 512
ROWS_PER_TRIP = 8
VMEM_LIMIT = 56 * 1024 * 1024
MASKED = -1e30
SB_UNDERFLOW = 104.0
SB_FIRST_BLOCKS = 3

_NT = (((1,), (1,)), ((), ()))
_TN = (((0,), (0,)), ((), ()))


def _alibi_slopes(mixer_pos, scale=1.0):
    idx = np.arange(HEADS_PER_MIXER, dtype=np.float32) * 3 + (mixer_pos + 1)
    return tuple(float(s) * scale for s in np.exp2(-8.0 * idx / 12.0).astype(np.float32))


def _params(*semantics):
    return pltpu.CompilerParams(dimension_semantics=semantics, vmem_limit_bytes=VMEM_LIMIT)


def _modulated_norm(x, gain, shift, scale):
    y = x * lax.rsqrt(jnp.mean(x * x, axis=-1, keepdims=True) + NORM_EPS) * gain
    return y * (1.0 + scale) + shift


def _head_cols(h):
    return slice(h * HEAD_DIM, (h + 1) * HEAD_DIM)


def _paired_loop(n, body, carry):
    carry = lax.fori_loop(0, n // 2, lambda jj, c: body(2 * jj + 1, body(2 * jj, c, 0), 1), carry)
    return lax.cond(n % 2 == 1, lambda c: body(n - 1, c, 0), lambda c: c, carry)


def _ada_kernel(c_ref, w_ref, b_ref, o_ref):
    c = c_ref[...]
    cond = c / (1.0 + jnp.exp(-c))
    o_ref[0, 0] = jnp.dot(cond, w_ref[0], preferred_element_type=F32,
                          precision=lax.Precision.HIGHEST) + b_ref[0, 0]


def _ada_modulation(c, w_ada, b_ada):
    depth, d, _ = w_ada.shape
    b = c.shape[0]
    out = pl.pallas_call(
        _ada_kernel,
        grid=(depth, 6),
        in_specs=[pl.BlockSpec((b, d), lambda l, k: (0, 0)),
                  pl.BlockSpec((1, d, d), lambda l, k: (l, 0, k)),
                  pl.BlockSpec((1, 1, 1, d), lambda l, k: (l, k, 0, 0))],
        out_specs=pl.BlockSpec((1, 1, b, d), lambda l, k: (l, k, 0, 0)),
        out_shape=jax.ShapeDtypeStruct((depth, 6, b, d), F32),
        compiler_params=_params("arbitrary", "arbitrary"),
        name="ada_modulation",
    )(c, w_ada, b_ada.reshape(depth, 6, 1, d))
    return out.transpose(0, 2, 1, 3)


IN_WIDTHS = (QKV_WIDTH, QKV_WIDTH, QKV_WIDTH, QKV_WIDTH, IDX_HEADS * IDX_DIM + LANES, LANES)


def _in_proj_kernel(x_ref, mod_ref, g_ref, w_ref, oa, ob, oc, od, oidx, ow, oc4, oc16, pc_ref):
    h = _modulated_norm(x_ref[...], g_ref[...], mod_ref[0, 0:1, :], mod_ref[0, 1:2, :]).astype(BF16)
    tm = x_ref.shape[0]
    off = 0
    for o_ref, width in zip((oa, ob, oc, od, oidx, ow), IN_WIDTHS):
        val = jnp.dot(h, w_ref[:, off:off + width], preferred_element_type=F32)
        o_ref[...] = val.astype(o_ref.dtype)
        if o_ref is oc:
            for s in range(QKV_WIDTH // LANES):
                pc_ref[s * tm:(s + 1) * tm, :] = val[:, s * LANES:(s + 1) * LANES]
        off += width
    for r, view in zip(DILATIONS[1:], (oc4, oc16)):
        for c in range(r):
            for s in range(QKV_WIDTH // LANES):
                col = c * QKV_WIDTH + s * LANES
                view[:, col:col + LANES] = pc_ref[pl.ds(s * tm + c, tm // r, stride=r), :].astype(view.dtype)


def _pack_w_in(w_in):
    d = w_in.shape[0]
    n_qkv = 4 * QKV_WIDTH
    n_qi = IDX_HEADS * IDX_DIM
    z = lambda n: jnp.zeros((d, n), w_in.dtype)
    return jnp.concatenate([w_in[:, :n_qkv + n_qi + IDX_DIM], z(LANES - IDX_DIM),
                            w_in[:, n_qkv + n_qi + IDX_DIM:], z(LANES - IDX_HEADS)], axis=1).astype(BF16)


def _in_projection(x2, mod, g_mix, w_packed, seq):
    t, d = x2.shape
    tm = min(ROW_TILE, seq)
    per_batch = seq // tm
    dtypes = (BF16, BF16, BF16, BF16, BF16, F32)
    views = DILATIONS[1:]
    return pl.pallas_call(
        _in_proj_kernel,
        grid=(t // tm,),
        in_specs=[pl.BlockSpec((tm, d), lambda i: (i, 0)),
                  pl.BlockSpec((1, 6, d), lambda i: (i // per_batch, 0, 0)),
                  pl.BlockSpec((1, d), lambda i: (0, 0)),
                  pl.BlockSpec(w_packed.shape, lambda i: (0, 0))],
        out_specs=([pl.BlockSpec((tm, w), lambda i: (i, 0)) for w in IN_WIDTHS]
                   + [pl.BlockSpec((tm // r, r * QKV_WIDTH), lambda i: (i, 0)) for r in views]),
        out_shape=([jax.ShapeDtypeStruct((t, w), dt) for w, dt in zip(IN_WIDTHS, dtypes)]
                   + [jax.ShapeDtypeStruct((t // r, r * QKV_WIDTH), BF16) for r in views]),
        scratch_shapes=[pltpu.VMEM((tm * (QKV_WIDTH // LANES), LANES), F32)],
        compiler_params=_params("arbitrary"),
        name="in_projection",
    )(x2, mod, g_mix.reshape(1, d), w_packed)


def _sb_kernel(q_ref, k_ref, v_ref, g_ref, o_ref):
    tq = q_ref.shape[1]
    i = pl.program_id(1)
    key_row = lax.broadcasted_iota(jnp.int32, (tq, 1), 0)
    q_lane = lax.broadcasted_iota(jnp.int32, (1, tq), 1)
    later = (lax.broadcasted_iota(jnp.int32, (tq, tq), 1) > lax.broadcasted_iota(jnp.int32, (tq, tq), 0))
    later = jnp.where(later, 1.0, 0.0).astype(BF16)

    qs = [q_ref[0, :, _head_cols(h)] * ATTN_SCALE for h in range(HEADS_PER_MIXER)]

    heads = range(HEADS_PER_MIXER)

    def blocks(starts, keeps, tails):
        z = jnp.concatenate([lax.dot_general(k_ref[0, pl.ds(st, tq), _head_cols(h)], qs[h], _NT,
                                             preferred_element_type=F32) for st in starts for h in heads], axis=1)
        softplus = jnp.maximum(z, 0.0) + jnp.log(1.0 + jnp.exp(-jnp.abs(z)))
        keep = None
        if any(kp is not None for kp in keeps):
            ones = jnp.ones((tq, tq), F32)
            keep = jnp.concatenate([ones if kp is None else kp for kp in keeps for _ in heads], axis=1)
        log_1m = -softplus if keep is None else -softplus * keep
        hi = log_1m.astype(BF16)
        lo = (log_1m - hi.astype(F32)).astype(BF16)
        inside = jnp.dot(later, hi, preferred_element_type=F32) + jnp.dot(later, lo, preferred_element_type=F32)
        block_sum = jnp.sum(log_1m, axis=0, keepdims=True)
        piece = lambda x, b, h: x[:, (b * HEADS_PER_MIXER + h) * tq:(b * HEADS_PER_MIXER + h + 1) * tq]
        tail_rows = []
        tails = list(tails)
        for b in range(len(starts)):
            tail_rows += tails
            tails = [tails[h] + piece(block_sum, b, h) for h in heads]
        a = jnp.exp(z - softplus + inside + jnp.concatenate(tail_rows, axis=1))
        if keep is not None:
            a = a * keep
        a = a.astype(BF16)
        av = [sum(lax.dot_general(v_ref[0, pl.ds(st, tq), _head_cols(h)], piece(a, b, h), _TN,
                                  preferred_element_type=F32) for b, st in enumerate(starts)) for h in heads]
        return tails, av

    starts = [pl.multiple_of(jnp.maximum(i - n, 0) * tq, tq) for n in range(SB_FIRST_BLOCKS)]
    keeps = [jnp.where(key_row < q_lane, 1.0, 0.0)]
    keeps += [jnp.full((tq, tq), jnp.where(i >= n, 1.0, 0.0), F32) for n in range(1, SB_FIRST_BLOCKS)]
    tails, accs = blocks(starts, keeps, [jnp.zeros((1, tq), F32)] * HEADS_PER_MIXER)
    state = tuple(zip(tails, accs))

    def body(carry):
        j, state = carry
        tails, av = blocks([pl.multiple_of(j * tq, tq)], [None], [tail for tail, _ in state])
        return j - 1, tuple((tails[h], state[h][1] + av[h]) for h in heads)

    def cond(carry):
        j, state = carry
        worst = functools.reduce(jnp.maximum, [tail for tail, _ in state])
        return jnp.logical_and(j >= 0, jnp.max(worst) > -SB_UNDERFLOW)

    _, state = lax.while_loop(cond, body, (i - SB_FIRST_BLOCKS, state))
    outs = []
    for h in range(HEADS_PER_MIXER):
        o = state[h][1]
        outs.append(o * lax.rsqrt(jnp.mean(o * o, axis=0, keepdims=True) + NORM_EPS) * g_ref[:, h:h + 1])
    o_ref[0] = jnp.concatenate(outs, axis=0).T.astype(o_ref.dtype)


def _stick_breaking_mixer(p, g_heads):
    b, seq, _ = p.shape
    tq = min(Q_TILE, seq)
    return pl.pallas_call(
        _sb_kernel,
        grid=(b, seq // tq),
        in_specs=[pl.BlockSpec((1, tq, MIXER_WIDTH), lambda bi, i: (bi, i, 0)),
                  pl.BlockSpec((1, seq, MIXER_WIDTH), lambda bi, i: (bi, 0, 1)),
                  pl.BlockSpec((1, seq, MIXER_WIDTH), lambda bi, i: (bi, 0, 2)),
                  pl.BlockSpec((HEAD_DIM, HEADS_PER_MIXER), lambda bi, i: (0, 0))],
        out_specs=pl.BlockSpec((1, tq, MIXER_WIDTH), lambda bi, i: (bi, i, 0)),
        out_shape=jax.ShapeDtypeStruct((b, seq, MIXER_WIDTH), BF16),
        compiler_params=_params("arbitrary", "arbitrary"),
        name="stick_breaking",
    )(p, p, p, g_heads.T)


def _dsa_kernel(q_ref, k_ref, v_ref, qi_ref, ki_ref, w_ref, g_ref, o_ref, sc_ref, sc16_ref, s_ref, p_ref, *, slopes):
    tq = q_ref.shape[1]
    tk = sc_ref.shape[1]
    i = pl.program_id(1)
    n_blocks = ((i + 1) * tq + tk - 1) // tk
    qpos = i * tq + lax.broadcasted_iota(jnp.int32, (1, tq), 1)
    key_row = lax.broadcasted_iota(jnp.int32, (tk, 1), 0)
    neg_inf = float("-inf")

    w_t = w_ref[0].T * (IDX_HEADS ** -0.5 * IDX_DIM ** -0.5)
    w_rows = [w_t[h:h + 1, :] for h in range(IDX_HEADS)]
    qi = qi_ref[0]

    def top_half(x):
        bits = lax.bitcast_convert_type(x, jnp.int32) & jnp.int32(-65536)
        return lax.bitcast_convert_type(bits, F32).astype(BF16)

    def score_body(j, carry, slot):
        start = pl.multiple_of(j * tk, tk)
        ki = ki_ref[0, pl.ds(start, tk), 0:IDX_DIM]
        sc = jnp.zeros((tk, tq), F32)
        for h in range(IDX_HEADS):
            x = lax.dot_general(ki, qi[:, h * IDX_DIM:(h + 1) * IDX_DIM], _NT, preferred_element_type=F32)
            sc = sc + w_rows[h] * jnp.maximum(x, 0.0)
        sc = jnp.where((start + key_row) <= qpos, sc + 0.0, neg_inf)
        sc_ref[j] = sc
        sc16_ref[j] = top_half(sc)
        return carry

    _paired_loop(n_blocks, score_body, 0)

    def count(pred):
        def add_block(j, cnt, slot):
            c = jnp.where(pred(sc_ref[j]), 1.0, 0.0)
            return cnt + jnp.sum(c.reshape(tk // SUBLANES, SUBLANES, tq), axis=0)

        cnt = _paired_loop(n_blocks, add_block, jnp.zeros((SUBLANES, tq), F32))
        return jnp.sum(cnt, axis=0, keepdims=True)

    int_min = jnp.int32(-2 ** 31)

    def ordered_to_float(u):
        key = u ^ int_min
        bits = jnp.where(key >= 0, key, key ^ jnp.int32(0x7FFFFFFF))
        return lax.bitcast_convert_type(bits, F32)

    def count_top_half(cand16):
        def add_block(j, cnt, slot):
            c = jnp.where(sc16_ref[j] >= cand16, jnp.ones((), BF16), jnp.zeros((), BF16))
            for g in range(tk // BF16_ROWS):
                cnt = cnt + c[g * BF16_ROWS:(g + 1) * BF16_ROWS]
            return cnt

        cnt = _paired_loop(n_blocks, add_block, jnp.zeros((BF16_ROWS, tq), BF16))
        return jnp.sum(cnt.astype(F32), axis=0, keepdims=True)

    def search_body(step, carry, half):
        prefix, n_at_prefix = carry
        cand = prefix | jnp.left_shift(jnp.int32(1), 31 - step)
        cand_f = ordered_to_float(cand)
        n_ge = count_top_half(top_half(cand_f)) if half else count(lambda s: s >= cand_f)
        keep = n_ge >= DSA_TOPK
        return jnp.where(keep, cand, prefix), jnp.where(keep, n_ge, n_at_prefix)

    keep_all = (qpos + 1) <= DSA_TOPK

    carry = (jnp.zeros((1, tq), jnp.int32), jnp.zeros((1, tq), F32))
    carry = lax.fori_loop(0, 16, functools.partial(search_body, half=True), carry)
    prefix, n_ge_tau = lax.fori_loop(16, 32, functools.partial(search_body, half=False), carry)
    tau = jnp.where(keep_all, float(np.finfo(np.float32).min), ordered_to_float(prefix))
    no_ties = jnp.min(jnp.where(jnp.logical_or(keep_all, n_ge_tau == DSA_TOPK), 1.0, 0.0)) > 0.5

    earlier = (lax.broadcasted_iota(jnp.int32, (tk, tk), 1) < lax.broadcasted_iota(jnp.int32, (tk, tk), 0))
    earlier = jnp.where(earlier, 1.0, 0.0).astype(BF16)
    qs = [q_ref[0, :, _head_cols(h)] * (ATTN_SCALE * LOG2E) for h in range(HEADS_PER_MIXER)]
    key_bias = [slopes[h] * lax.broadcasted_iota(jnp.int32, (tk, tq), 0).astype(F32)
                for h in range(HEADS_PER_MIXER)]

    def attn_body(j, carry, slot, n_ties=None):
        ties_seen, stats = carry
        s_slot = [s_ref.at[slot * HEADS_PER_MIXER + h] for h in range(HEADS_PER_MIXER)]
        p_slot = [p_ref.at[slot * HEADS_PER_MIXER + h] for h in range(HEADS_PER_MIXER)]
        start = pl.multiple_of(j * tk, tk)
        sc = sc_ref[j]
        if n_ties is None:
            sel = sc >= tau
        else:
            tie = jnp.where(sc == tau, 1.0, 0.0)
            rank = jnp.dot(earlier, tie.astype(BF16), preferred_element_type=F32) + ties_seen
            sel = jnp.where(sc > tau, 1.0, jnp.where(rank < n_ties, tie, 0.0)) > 0.5
            ties_seen = ties_seen + jnp.sum(tie, axis=0, keepdims=True)
        block_pos = (j * tk - i * tq).astype(F32)
        for h in range(HEADS_PER_MIXER):
            k = k_ref[0, pl.ds(start, tk), _head_cols(h)]
            s = lax.dot_general(k, qs[h], _NT, preferred_element_type=F32) + key_bias[h]
            s_slot[h][...] = jnp.where(sel, s, MASKED)
        locals_ = []
        for h in range(HEADS_PER_MIXER):
            s = s_slot[h][...]
            m_loc = jnp.max(s, axis=0, keepdims=True)
            p = jnp.exp2(s - m_loc)
            locals_.append((m_loc, jnp.sum(p, axis=0, keepdims=True)))
            p_slot[h][...] = p.astype(BF16)
        new_stats = []
        for h in range(HEADS_PER_MIXER):
            m, l, acc = stats[h]
            m_loc, l_blk = locals_[h]
            v = v_ref[0, pl.ds(start, tk), _head_cols(h)]
            pv_blk = lax.dot_general(v, p_slot[h][...], _TN, preferred_element_type=F32)
            m_blk = m_loc + slopes[h] * block_pos
            m_new = jnp.maximum(m, m_blk)
            w_old = jnp.exp2(m - m_new)
            w_blk = jnp.where(m_loc > 0.5 * MASKED, jnp.exp2(m_blk - m_new), 0.0)
            new_stats.append((m_new, w_old * l + w_blk * l_blk, w_old * acc + w_blk * pv_blk))
        return ties_seen, tuple(new_stats)

    init = tuple((jnp.full((1, tq), MASKED, F32), jnp.zeros((1, tq), F32), jnp.zeros((HEAD_DIM, tq), F32))
                 for _ in range(HEADS_PER_MIXER))
    start_carry = (jnp.zeros((1, tq), F32), init)

    def attend_without_ties():
        return _paired_loop(n_blocks, attn_body, start_carry)[1]

    def attend_with_ties():
        n_gt = count(lambda s: s > tau)
        n_ties = jnp.where(keep_all, 1e9, DSA_TOPK - n_gt)
        return _paired_loop(n_blocks, functools.partial(attn_body, n_ties=n_ties), start_carry)[1]

    stats = lax.cond(no_ties, attend_without_ties, attend_with_ties)
    outs = []
    for h in range(HEADS_PER_MIXER):
        _, l, acc = stats[h]
        o = acc / l
        outs.append(o * lax.rsqrt(jnp.mean(o * o, axis=0, keepdims=True) + NORM_EPS) * g_ref[:, h:h + 1])
    o_ref[0] = jnp.concatenate(outs, axis=0).T.astype(o_ref.dtype)


def _dsa_mixer(p, p_idx, p_w, g_heads):
    b, seq, _ = p.shape
    tq = min(4 * Q_TILE, seq)
    tk = tq
    assert seq // BF16_ROWS <= BF16_EXACT_COUNT, "the packed-bf16 count accumulators would stop being exact"
    qi_width = IDX_HEADS * IDX_DIM
    return pl.pallas_call(
        functools.partial(_dsa_kernel, slopes=_alibi_slopes(0, LOG2E)),
        grid=(b, seq // tq),
        in_specs=[pl.BlockSpec((1, tq, MIXER_WIDTH), lambda bi, i: (bi, i, 0)),
                  pl.BlockSpec((1, seq, MIXER_WIDTH), lambda bi, i: (bi, 0, 1)),
                  pl.BlockSpec((1, seq, MIXER_WIDTH), lambda bi, i: (bi, 0, 2)),
                  pl.BlockSpec((1, tq, qi_width), lambda bi, i: (bi, i, 0)),
                  pl.BlockSpec((1, seq, LANES), lambda bi, i: (bi, 0, qi_width // LANES)),
                  pl.BlockSpec((1, tq, LANES), lambda bi, i: (bi, i, 0)),
                  pl.BlockSpec((HEAD_DIM, HEADS_PER_MIXER), lambda bi, i: (0, 0))],
        out_specs=pl.BlockSpec((1, tq, MIXER_WIDTH), lambda bi, i: (bi, i, 0)),
        out_shape=jax.ShapeDtypeStruct((b, seq, MIXER_WIDTH), BF16),
        scratch_shapes=[pltpu.VMEM((seq // tk, tk, tq), F32),
                        pltpu.VMEM((seq // tk, tk, tq), BF16),
                        pltpu.VMEM((2 * HEADS_PER_MIXER, tk, tq), F32),
                        pltpu.VMEM((2 * HEADS_PER_MIXER, tk, tq), BF16)],
        compiler_params=_params("arbitrary", "arbitrary"),
        name="dsa",
    )(p, p, p, p_idx, p_idx, p_w, g_heads.T)


def _band_kernel(q_ref, kp_ref, kc_ref, vp_ref, vc_ref, o_ref, lse_ref, *, dilation, slopes):
    tq = q_ref.shape[1]
    ui = pl.program_id(2)
    heads = range(HEADS_PER_MIXER)
    u_q = jnp.concatenate([ui * tq + lax.broadcasted_iota(jnp.int32, (1, tq), 1)] * HEADS_PER_MIXER, axis=1)
    u_k = ui * tq - DILATED_STEPS + lax.broadcasted_iota(jnp.int32, (DILATED_STEPS + tq, 1), 0)
    steps = u_q - u_k
    valid = jnp.logical_and(jnp.logical_and(steps >= 0, steps <= DILATED_STEPS), u_k >= 0)
    slope_row = jnp.concatenate([jnp.full((1, tq), slopes[h], F32) for h in heads], axis=1)
    s = jnp.concatenate(
        [lax.dot_general(jnp.concatenate([kp_ref[0, :, _head_cols(h)], kc_ref[0, :, _head_cols(h)]], axis=0),
                         q_ref[0, :, _head_cols(h)], _NT, preferred_element_type=F32) for h in heads], axis=1)
    s = jnp.where(valid, s * ATTN_SCALE - slope_row * (steps * dilation).astype(F32), MASKED)
    m = jnp.max(s, axis=0, keepdims=True)
    e = jnp.exp(s - m)
    den = jnp.sum(e, axis=0, keepdims=True)
    p = e.astype(BF16)
    lse = m + jnp.log(den)
    outs = []
    for h in heads:
        cols = slice(h * tq, (h + 1) * tq)
        v = jnp.concatenate([vp_ref[0, :, _head_cols(h)], vc_ref[0, :, _head_cols(h)]], axis=0)
        outs.append(lax.dot_general(v, p[:, cols], _TN, preferred_element_type=F32) / den[:, cols])
    o_ref[0] = jnp.concatenate(outs, axis=0).T
    lse_rows = jnp.concatenate([lse[:, h * tq:(h + 1) * tq] for h in heads]
                               + [jnp.zeros((LANES - HEADS_PER_MIXER, tq), F32)], axis=0)
    lse_ref[0] = lse_rows.T


def _dilated_branch(view, dilation, slopes):
    b, length, _ = view.shape
    classes = dilation
    tq = min(2 * Q_TILE, length)
    back = tq // DILATED_STEPS
    spec = lambda part, prev: (
        pl.BlockSpec((1, DILATED_STEPS, MIXER_WIDTH),
                     lambda bi, c, ui: (bi, jnp.maximum(ui * back - 1, 0), c * 3 + part)) if prev
        else pl.BlockSpec((1, tq, MIXER_WIDTH), lambda bi, c, ui: (bi, ui, c * 3 + part)))
    out, lse = pl.pallas_call(
        functools.partial(_band_kernel, dilation=dilation, slopes=slopes),
        grid=(b, classes, length // tq),
        in_specs=[spec(0, False), spec(1, True), spec(1, False), spec(2, True), spec(2, False)],
        out_specs=[pl.BlockSpec((1, tq, MIXER_WIDTH), lambda bi, c, ui: (bi, ui, c)),
                   pl.BlockSpec((1, tq, LANES), lambda bi, c, ui: (bi, ui, c))],
        out_shape=[jax.ShapeDtypeStruct((b, length, classes * MIXER_WIDTH), F32),
                   jax.ShapeDtypeStruct((b, length, classes * LANES), F32)],
        compiler_params=_params("arbitrary", "arbitrary", "arbitrary"),
        name=f"dilated_r{dilation}",
    )(view, view, view, view, view)
    return out, lse


def _merge_dilated(o_refs, l_refs, g_ref, scratch, rows):
    chunks = MIXER_WIDTH // LANES
    outs = [[o_refs[0][:, s * LANES:(s + 1) * LANES] for s in range(chunks)]]
    lses = [l_refs[0][...]]
    for r, o_view, l_view, o_nat, l_nat in zip(DILATIONS[1:], o_refs[1:], l_refs[1:], scratch[0::2], scratch[1::2]):
        for c in range(r):
            for s in range(chunks):
                col = c * MIXER_WIDTH + s * LANES
                o_nat[pl.ds(s * rows + c, rows // r, stride=r), :] = o_view[:, col:col + LANES]
            l_nat[pl.ds(c, rows // r, stride=r), :] = l_view[:, c * LANES:(c + 1) * LANES]
        outs.append([o_nat[s * rows:(s + 1) * rows, :] for s in range(chunks)])
        lses.append(l_nat[...])
    top = functools.reduce(jnp.maximum, lses)
    wts = [jnp.exp(l - top) for l in lses]
    def head_of(index):
        return sum(jnp.where(index >= h * HEAD_DIM, 1, 0) for h in range(1, HEADS_PER_MIXER))

    head_of_col = head_of(lax.broadcasted_iota(jnp.int32, (LANES, MIXER_WIDTH), 1))
    spread = jnp.where(lax.broadcasted_iota(jnp.int32, (LANES, MIXER_WIDTH), 0) == head_of_col, 1.0, 0.0)
    same_head = (head_of(lax.broadcasted_iota(jnp.int32, (MIXER_WIDTH, MIXER_WIDTH), 0))
                 == head_of(lax.broadcasted_iota(jnp.int32, (MIXER_WIDTH, MIXER_WIDTH), 1)))
    head_mean = jnp.where(same_head, 1.0 / HEAD_DIM, 0.0)

    def times(x, m):
        hi = x.astype(BF16)
        lo = (x - hi.astype(F32)).astype(BF16)
        m = m.astype(BF16)
        return jnp.dot(hi, m, preferred_element_type=F32) + jnp.dot(lo, m, preferred_element_type=F32)

    w_cols = [times(w, spread) for w in wts]
    full = [jnp.concatenate(o, axis=1) for o in outs]
    mixed = sum(w * o for w, o in zip(w_cols, full)) / sum(w_cols)
    return mixed * lax.rsqrt(times(mixed * mixed, head_mean) + NORM_EPS) * g_ref[...]


def _dilated_branches(views):
    slopes = _alibi_slopes(1)
    branches = [_dilated_branch(v, r, slopes) for v, r in zip(views, DILATIONS)]
    flat = lambda a: a.reshape(-1, a.shape[-1])
    return [flat(o) for o, _ in branches], [flat(l) for _, l in branches]


def _moba_kernel(q_ref, k_ref, v_ref, g_ref, o_ref, kmean_ref, chosen_ref, s_ref, p_ref, *, slopes):
    tq = q_ref.shape[1]
    n_kv = k_ref.shape[1] // MOBA_BLOCK
    own = pl.program_id(1)

    @pl.when(own == 0)
    def _():
        kmean_ref[...] = jnp.zeros_like(kmean_ref)
        for n in range(n_kv):
            blk = k_ref[0, n * MOBA_BLOCK:(n + 1) * MOBA_BLOCK, :].astype(F32)
            kmean_ref[n:n + 1, :] = jnp.sum(blk, axis=0, keepdims=True) * (1.0 / MOBA_BLOCK)

    key_row = lax.broadcasted_iota(jnp.int32, (MOBA_BLOCK, 1), 0)
    q_lane = lax.broadcasted_iota(jnp.int32, (1, tq), 1)
    blk_rows = kmean_ref.shape[0]
    blk_row = lax.broadcasted_iota(jnp.int32, (blk_rows, 1), 0)
    blk_f = blk_row.astype(F32)
    neg_inf = float("-inf")
    group = tq // MOBA_BLOCK
    first_own = own * group
    sub_block = sum(jnp.where(q_lane >= g * MOBA_BLOCK, 1, 0) for g in range(1, group)) if group > 1 else 0
    own_blk = first_own + sub_block
    q_local = q_lane - sub_block * MOBA_BLOCK

    qs = []
    for h in range(HEADS_PER_MIXER):
        q = q_ref[0, :, _head_cols(h)]
        gate = lax.dot_general(kmean_ref[:, _head_cols(h)], q.astype(F32), _NT, preferred_element_type=F32,
                               precision=lax.Precision.HIGHEST)
        gate = jnp.where(blk_row < own_blk, gate, neg_inf)
        picks = jnp.zeros((blk_rows, tq), F32)
        for _ in range(MOBA_TOPK):
            top = jnp.max(gate, axis=0, keepdims=True)
            is_top = jnp.logical_and(gate == top, top > neg_inf)
            first = jnp.min(jnp.where(is_top, blk_f, float(blk_rows)), axis=0, keepdims=True)
            pick = blk_f == first
            picks = jnp.where(pick, 1.0, picks)
            gate = jnp.where(pick, neg_inf, gate)
        chosen_ref[h] = picks
        qs.append(q * (ATTN_SCALE * LOG2E))

    key_bias = [slopes[h] * lax.broadcasted_iota(jnp.int32, (MOBA_BLOCK, tq), 0).astype(F32)
                for h in range(HEADS_PER_MIXER)]

    def block_softmax(start, keep=None, slot=0):
        s_slot = [s_ref.at[slot * HEADS_PER_MIXER + h] for h in range(HEADS_PER_MIXER)]
        p_slot = [p_ref.at[slot * HEADS_PER_MIXER + h] for h in range(HEADS_PER_MIXER)]
        for h in range(HEADS_PER_MIXER):
            k = k_ref[0, pl.ds(start, MOBA_BLOCK), _head_cols(h)]
            s = lax.dot_general(k, qs[h], _NT, preferred_element_type=F32) + key_bias[h]
            s_slot[h][...] = s if keep is None else jnp.where(keep, s, MASKED)
        pieces = []
        for h in range(HEADS_PER_MIXER):
            s = s_slot[h][...]
            m_loc = jnp.max(s, axis=0, keepdims=True)
            p = jnp.exp2(s - m_loc)
            pieces.append((m_loc, jnp.sum(p, axis=0, keepdims=True)))
            p_slot[h][...] = p.astype(BF16)
        out = []
        for h in range(HEADS_PER_MIXER):
            v = v_ref[0, pl.ds(start, MOBA_BLOCK), _head_cols(h)]
            out.append(pieces[h] + (lax.dot_general(v, p_slot[h][...], _TN, preferred_element_type=F32),))
        return out

    def merge(stats, j, pieces, use):
        block_pos = ((j - first_own) * MOBA_BLOCK).astype(F32)
        merged = []
        for h, (m_loc, l_blk, pv_blk) in enumerate(pieces):
            m_blk = jnp.where(use[h], m_loc + slopes[h] * block_pos, MASKED)
            m, l, acc = stats[h]
            m_new = jnp.maximum(m, m_blk)
            w_old = jnp.exp2(m - m_new)
            w_blk = jnp.where(use[h], jnp.exp2(m_blk - m_new), 0.0)
            merged.append((m_new, w_old * l + w_blk * l_blk, w_old * acc + w_blk * pv_blk))
        return tuple(merged)

    def picked(h, j):
        return chosen_ref[h, pl.ds(j, 1), :] > 0.5

    def body(j, stats, slot):
        start = pl.multiple_of(j * MOBA_BLOCK, MOBA_BLOCK)
        return merge(stats, j, block_softmax(start, slot=slot), [picked(h, j) for h in range(HEADS_PER_MIXER)])

    stats = tuple((jnp.full((1, tq), MASKED, F32), jnp.zeros((1, tq), F32), jnp.zeros((HEAD_DIM, tq), F32))
                  for _ in range(HEADS_PER_MIXER))
    stats = _paired_loop(first_own, body, stats)
    for g in range(group):
        j = first_own + g
        keep = jnp.logical_or(own_blk > j, jnp.logical_and(own_blk == j, key_row <= q_local))
        pieces = block_softmax(pl.multiple_of(j * MOBA_BLOCK, MOBA_BLOCK), keep=keep, slot=g % 2)
        use = [jnp.logical_or(own_blk == j, jnp.logical_and(own_blk > j, picked(h, j)))
               for h in range(HEADS_PER_MIXER)]
        stats = merge(stats, j, pieces, use)
    outs = []
    for h in range(HEADS_PER_MIXER):
        _, l, acc = stats[h]
        o = acc / l
        outs.append(o * lax.rsqrt(jnp.mean(o * o, axis=0, keepdims=True) + NORM_EPS) * g_ref[:, h:h + 1])
    o_ref[0] = jnp.concatenate(outs, axis=0).T.astype(o_ref.dtype)


def _moba_mixer(p, g_heads):
    b, seq, _ = p.shape
    tq = MOBA_GROUP * MOBA_BLOCK if seq % (MOBA_GROUP * MOBA_BLOCK) == 0 else MOBA_BLOCK
    blk_rows =-(-(seq // MOBA_BLOCK) // SUBLANES) * SUBLANES
    return pl.pallas_call(
        functools.partial(_moba_kernel, slopes=_alibi_slopes(2, LOG2E)),
        grid=(b, seq // tq),
        in_specs=[pl.BlockSpec((1, tq, MIXER_WIDTH), lambda bi, i: (bi, i, 0)),
                  pl.BlockSpec((1, seq, MIXER_WIDTH), lambda bi, i: (bi, 0, 1)),
                  pl.BlockSpec((1, seq, MIXER_WIDTH), lambda bi, i: (bi, 0, 2)),
                  pl.BlockSpec((HEAD_DIM, HEADS_PER_MIXER), lambda bi, i: (0, 0))],
        out_specs=pl.BlockSpec((1, tq, MIXER_WIDTH), lambda bi, i: (bi, i, 0)),
        out_shape=jax.ShapeDtypeStruct((b, seq, MIXER_WIDTH), BF16),
        scratch_shapes=[pltpu.VMEM((blk_rows, MIXER_WIDTH), F32),
                        pltpu.VMEM((HEADS_PER_MIXER, blk_rows, tq), F32),
                        pltpu.VMEM((2 * HEADS_PER_MIXER, MOBA_BLOCK, tq), F32),
                        pltpu.VMEM((2 * HEADS_PER_MIXER, MOBA_BLOCK, tq), BF16)],
        compiler_params=_params("arbitrary", "arbitrary"),
        name="moba",
    )(p, p, p, g_heads.T)


def _out_proj_kernel(oa, ob, od, o1, o4, o16, l1, l4, l16, gc_ref, w_ref, x_ref, mod_ref, o_ref, *scratch):
    rows = x_ref.shape[0]
    oc = _merge_dilated((o1, o4, o16), (l1, l4, l16), gc_ref, scratch, rows).astype(BF16)
    acc = jnp.zeros(x_ref.shape, F32)
    for m, o in enumerate((oa[...], ob[...], oc, od[...])):
        acc = acc + jnp.dot(o, w_ref[m * MIXER_WIDTH:(m + 1) * MIXER_WIDTH, :], preferred_element_type=F32)
    o_ref[...] = x_ref[...] + mod_ref[0, 2:3, :] * acc


def _out_projection(o_a, o_b, o_d, dilated, g_heads_c, w_out, x2, mod, seq):
    t, d = x2.shape
    tm = min(ROW_TILE, seq)
    per_batch = seq // tm
    o_spec = pl.BlockSpec((tm, MIXER_WIDTH), lambda i: (i, 0))
    branch_outs, branch_lses = dilated
    view_specs = ([pl.BlockSpec((tm // r, r * MIXER_WIDTH), lambda i: (i, 0)) for r in DILATIONS]
                  + [pl.BlockSpec((tm // r, r * LANES), lambda i: (i, 0)) for r in DILATIONS])
    scratch = []
    for _ in DILATIONS[1:]:
        scratch += [pltpu.VMEM((tm * (MIXER_WIDTH // LANES), LANES), F32), pltpu.VMEM((tm, LANES), F32)]
    return pl.pallas_call(
        _out_proj_kernel,
        grid=(t // tm,),
        in_specs=[o_spec] * 3 + view_specs + [pl.BlockSpec((1, MIXER_WIDTH), lambda i: (0, 0)),
                                              pl.BlockSpec(w_out.shape, lambda i: (0, 0)),
                                              pl.BlockSpec((tm, d), lambda i: (i, 0)),
                                              pl.BlockSpec((1, 6, d), lambda i: (i // per_batch, 0, 0))],
        out_specs=pl.BlockSpec((tm, d), lambda i: (i, 0)),
        out_shape=jax.ShapeDtypeStruct((t, d), F32),
        scratch_shapes=scratch,
        compiler_params=_params("arbitrary"),
        name="out_projection",
    )(*[o.reshape(t, MIXER_WIDTH) for o in (o_a, o_b, o_d)], *branch_outs, *branch_lses,
      g_heads_c.reshape(1, MIXER_WIDTH), w_out, x2, mod)


def _rms_gain(x, gain):
    return x * lax.rsqrt(jnp.mean(x * x, axis=-1, keepdims=True) + NORM_EPS) * gain


def _ffn_kernel(x_ref, mod_ref, g_ref, fg_ref, wg_ref, wu_ref, wd_ref, o_ref, h_ref, acc_ref, *, final_norm):
    f = pl.program_id(1)

    @pl.when(f == 0)
    def _():
        h_ref[...] = _modulated_norm(x_ref[...], g_ref[...], mod_ref[0, 3:4, :], mod_ref[0, 4:5, :]).astype(BF16)
        acc_ref[...] = jnp.zeros_like(acc_ref)

    h = h_ref[...]
    gate = jnp.dot(h, wg_ref[...], preferred_element_type=F32)
    up = jnp.dot(h, wu_ref[...], preferred_element_type=F32)
    act = (gate / (1.0 + jnp.exp(-gate)) * up).astype(BF16)
    acc_ref[...] += jnp.dot(act, wd_ref[...], preferred_element_type=F32)

    @pl.when(f == pl.num_programs(1) - 1)
    def _():
        out = x_ref[...] + mod_ref[0, 5:6, :] * acc_ref[...]
        o_ref[...] = _rms_gain(out, fg_ref[...]) if final_norm else out


def _dense_ffn(x2, mod, g_ffn, wg, wu, wd, seq, g_final, final_norm):
    t, d = x2.shape
    d_ff = wg.shape[1]
    tm = min(FFN_ROWS, seq)
    tf = FFN_TILE if d_ff % FFN_TILE == 0 else d_ff
    per_batch = seq // tm
    return pl.pallas_call(
        functools.partial(_ffn_kernel, final_norm=final_norm),
        grid=(t // tm, d_ff // tf),
        in_specs=[pl.BlockSpec((tm, d), lambda i, f: (i, 0)),
                  pl.BlockSpec((1, 6, d), lambda i, f: (i // per_batch, 0, 0)),
                  pl.BlockSpec((1, d), lambda i, f: (0, 0)),
                  pl.BlockSpec((1, d), lambda i, f: (0, 0)),
                  pl.BlockSpec((d, tf), lambda i, f: (0, f)),
                  pl.BlockSpec((d, tf), lambda i, f: (0, f)),
                  pl.BlockSpec((tf, d), lambda i, f: (f, 0))],
        out_specs=pl.BlockSpec((tm, d), lambda i, f: (i, 0)),
        out_shape=jax.ShapeDtypeStruct((t, d), F32),
        scratch_shapes=[pltpu.VMEM((tm, d), BF16), pltpu.VMEM((tm, d), F32)],
        compiler_params=_params("arbitrary", "arbitrary"),
        name="dense_ffn",
    )(x2, mod, g_ffn.reshape(1, d), g_final.reshape(1, d), wg.astype(BF16), wu.astype(BF16), wd.astype(BF16))


def _store_row_tiles(dst_ref, value):
    rows, d = value.shape
    chunks = d // LANES
    for s in range(chunks):
        dst_ref[pl.ds(s, rows, stride=chunks), :] = value[:, s * LANES:(s + 1) * LANES]


def _load_row_tile_chunk(src_ref, s, rows, chunks):
    return src_ref[pl.ds(s, rows, stride=chunks), :]


def _router_kernel(x_ref, mod_ref, g_ref, wr_ref, h_ref, logit_ref):
    h = _modulated_norm(x_ref[...], g_ref[...], mod_ref[0, 3:4, :], mod_ref[0, 4:5, :])
    _store_row_tiles(h_ref, h)
    logit_ref[...] = jnp.dot(h, wr_ref[...], preferred_element_type=F32, precision=lax.Precision.HIGHEST)


def _router(x2, mod, g_ffn, w_router, seq):
    t, d = x2.shape
    tm = min(ROW_TILE, seq)
    per_batch = seq // tm
    wr = jnp.zeros((d, LANES), F32).at[:, :N_EXPERTS].set(w_router.astype(F32))
    return pl.pallas_call(
        _router_kernel,
        grid=(t // tm,),
        in_specs=[pl.BlockSpec((tm, d), lambda i: (i, 0)),
                  pl.BlockSpec((1, 6, d), lambda i: (i // per_batch, 0, 0)),
                  pl.BlockSpec((1, d), lambda i: (0, 0)),
                  pl.BlockSpec((d, LANES), lambda i: (0, 0))],
        out_specs=[pl.BlockSpec((tm * (d // LANES), LANES), lambda i: (i, 0)),
                   pl.BlockSpec((tm, LANES), lambda i: (i, 0))],
        out_shape=[jax.ShapeDtypeStruct((t * (d // LANES), LANES), F32), jax.ShapeDtypeStruct((t, LANES), F32)],
        compiler_params=_params("arbitrary"),
        name="moe_router",
    )(x2, mod, g_ffn.reshape(1, d), wr)


def _row_copy(src_hbm, row, dst_ref, r, sem, chunks):
    src = src_hbm.at[pl.ds(pl.multiple_of(row * chunks, chunks), chunks), :]
    return pltpu.make_async_copy(src, dst_ref.at[pl.ds(pl.multiple_of(r * chunks, chunks), chunks), :], sem)


def _expert_kernel(be_ref, used_ref, cur_idx_ref, nxt_idx_ref, h_hbm, wg_ref, wu_ref, wd_ref, o_ref,
                   xin_ref, xb_ref, acc_ref, sem, *, n_ff_steps):
    m = pl.program_id(0)
    f = pl.program_id(1)
    n_used = used_ref[0]
    live = m < n_used
    rows, d = xb_ref.shape
    chunks = d // LANES

    def row_copies(idx_ref, slot, start):
        if not start:
            half = rows * chunks // 2
            for lane in range(2):
                pltpu.make_async_copy(h_hbm.at[pl.ds(0, half), :], xin_ref.at[slot, pl.ds(0, half), :],
                                      sem.at[slot, lane]).wait()
            return

        def body(group, c):
            for u in range(ROWS_PER_TRIP):
                r = ROWS_PER_TRIP * group + u
                _row_copy(h_hbm, idx_ref[0, 0, r], xin_ref.at[slot], r, sem.at[slot, u % 2],
                          chunks).start(priority=u % 2)
            return c
        lax.fori_loop(0, rows // ROWS_PER_TRIP, body, 0)

    @pl.when(jnp.logical_and(live, f == 0))
    def _():
        @pl.when(m == 0)
        def _():
            row_copies(cur_idx_ref, 0, start=True)

        for slot in range(2):
            @pl.when(m % 2 == slot)
            def _(slot=slot):
                row_copies(cur_idx_ref, slot, start=False)
                for s in range(chunks):
                    xb_ref[:, s * LANES:(s + 1) * LANES] = _load_row_tile_chunk(
                        xin_ref.at[slot], s, rows, chunks).astype(BF16)

        acc_ref[...] = jnp.zeros_like(acc_ref)

    def swiglu_step(start_next):
        h = xb_ref[...]
        if start_next:
            share = rows // n_ff_steps
            nxt = (m + 1) % 2
            for u in range(share):
                r = f * share + u
                _row_copy(h_hbm, nxt_idx_ref[0, 0, r], xin_ref.at[nxt], r, sem.at[nxt, u % 2],
                          chunks).start(priority=u % 2)
        gate = jnp.dot(h, wg_ref[0], preferred_element_type=F32)
        up = jnp.dot(h, wu_ref[0], preferred_element_type=F32)
        act = (gate / (1.0 + jnp.exp(-gate)) * up).astype(BF16)
        acc_ref[...] += jnp.dot(act, wd_ref[0], preferred_element_type=F32)

    next_live = m + 1 < n_used
    pl.when(jnp.logical_and(live, next_live))(functools.partial(swiglu_step, True))
    pl.when(jnp.logical_and(live, jnp.logical_not(next_live)))(functools.partial(swiglu_step, False))

    @pl.when(f == pl.num_programs(1) - 1)
    def _():
        _store_row_tiles(o_ref, jnp.where(live, acc_ref[...], 0.0))


def _expert_ffn(h_rows, slot_tok, block_expert, n_used, wg, wu, wd):
    d = wg.shape[1]
    chunks = d // LANES
    n_slots = slot_tok.shape[0]
    d_ff = wg.shape[2]
    tm = EXPERT_ROWS
    tf = EXPERT_FF_TILE if d_ff % EXPERT_FF_TILE == 0 else d_ff
    nf = d_ff // tf
    nm = n_slots // tm

    def fcol(m, f, used):
        return jnp.where(m < used[0], f, nf - 1)

    idx_spec = lambda shift: pl.BlockSpec((1, 1, tm), lambda m, f, be, used: (jnp.minimum(m + shift, nm - 1), 0, 0),
                                          memory_space=pltpu.SMEM)
    grid_spec = pltpu.PrefetchScalarGridSpec(
        num_scalar_prefetch=2,
        grid=(nm, nf),
        in_specs=[idx_spec(0), idx_spec(1), pl.BlockSpec(memory_space=pl.ANY),
                  pl.BlockSpec((1, d, tf), lambda m, f, be, used: (be[m], 0, fcol(m, f, used))),
                  pl.BlockSpec((1, d, tf), lambda m, f, be, used: (be[m], 0, fcol(m, f, used))),
                  pl.BlockSpec((1, tf, d), lambda m, f, be, used: (be[m], fcol(m, f, used), 0))],
        out_specs=pl.BlockSpec((tm * chunks, LANES), lambda m, f, be, used: (m, 0)),
        scratch_shapes=[pltpu.VMEM((2, tm * chunks, LANES), F32), pltpu.VMEM((tm, d), BF16),
                        pltpu.VMEM((tm, d), F32), pltpu.SemaphoreType.DMA((2, 2))])
    idx = slot_tok.reshape(nm, 1, tm)
    assert tm % (2 * nf) == 0
    return pl.pallas_call(
        functools.partial(_expert_kernel, n_ff_steps=nf),
        grid_spec=grid_spec,
        out_shape=jax.ShapeDtypeStruct((n_slots * chunks, LANES), F32),
        compiler_params=_params("arbitrary", "arbitrary"),
        name="expert_ffn",
    )(block_expert, n_used, idx, idx, h_rows, wg.astype(BF16), wu.astype(BF16), wd.astype(BF16))


def _combine_kernel(d0_ref, d1_ref, ys_hbm, x_ref, gates_ref, mod_ref, fg_ref, o_ref, y0_ref, y1_ref, sem, *,
                    final_norm):
    rows, d = o_ref.shape
    chunks = d // LANES

    def start(group, c):
        for u in range(ROWS_PER_TRIP // 2):
            r = ROWS_PER_TRIP // 2 * group + u
            _row_copy(ys_hbm, d0_ref[0, 0, r], y0_ref, r, sem.at[0], chunks).start(priority=0)
            _row_copy(ys_hbm, d1_ref[0, 0, r], y1_ref, r, sem.at[1], chunks).start(priority=1)
        return c

    lax.fori_loop(0, rows // (ROWS_PER_TRIP // 2), start, 0)
    for k, y_ref in enumerate((y0_ref, y1_ref)):
        pltpu.make_async_copy(ys_hbm.at[pl.ds(0, rows * chunks), :], y_ref, sem.at[k]).wait()
    gates = gates_ref[...]
    for s in range(chunks):
        cols = slice(s * LANES, (s + 1) * LANES)
        y = (_load_row_tile_chunk(y0_ref, s, rows, chunks) * gates[:, 0:1]
             + _load_row_tile_chunk(y1_ref, s, rows, chunks) * gates[:, 1:2])
        o_ref[:, cols] = x_ref[:, cols] + mod_ref[0, 5:6, cols] * y
    if final_norm:
        o_ref[...] = _rms_gain(o_ref[...], fg_ref[...])


def _moe_combine(ys, dest0, dest1, gates, x2, mod, seq, g_final, final_norm):
    t, d = x2.shape
    tm = min(GATHER_ROWS, seq)
    steps = t // tm
    per_batch = seq // tm
    idx_spec = pl.BlockSpec((1, 1, tm), lambda i: (i, 0, 0), memory_space=pltpu.SMEM)
    return pl.pallas_call(
        functools.partial(_combine_kernel, final_norm=final_norm),
        grid=(steps,),
        in_specs=[idx_spec, idx_spec, pl.BlockSpec(memory_space=pl.ANY),
                  pl.BlockSpec((tm, d), lambda i: (i, 0)),
                  pl.BlockSpec((tm, TOP_K_EXPERTS), lambda i: (i, 0)),
                  pl.BlockSpec((1, 6, d), lambda i: (i // per_batch, 0, 0)),
                  pl.BlockSpec((1, d), lambda i: (0, 0))],
        out_specs=pl.BlockSpec((tm, d), lambda i: (i, 0)),
        out_shape=jax.ShapeDtypeStruct((t, d), F32),
        scratch_shapes=[pltpu.VMEM((tm * (d // LANES), LANES), F32), pltpu.VMEM((tm * (d // LANES), LANES), F32),
                        pltpu.SemaphoreType.DMA((2,))],
        compiler_params=_params("arbitrary"),
        name="moe_combine",
    )(dest0.reshape(steps, 1, tm), dest1.reshape(steps, 1, tm), ys, x2, gates, mod, g_final.reshape(1, d))


def _moe_ffn(x2, mod, g_ffn, w_router, wg, wu, wd, seq, g_final, final_norm):
    t, d = x2.shape
    h, logits = _router(x2, mod, g_ffn, w_router, seq)
    top_val, top_idx = lax.top_k(logits[:, :N_EXPERTS], TOP_K_EXPERTS)
    gates = jax.nn.softmax(top_val, axis=-1)

    n_assign = t * TOP_K_EXPERTS
    flat_e = top_idx.reshape(-1).astype(jnp.int32)
    onehot = (flat_e[:, None] == jnp.arange(N_EXPERTS, dtype=jnp.int32)[None, :]).astype(jnp.int32)
    rank = jnp.take_along_axis(jnp.cumsum(onehot, axis=0), flat_e[:, None], axis=1)[:, 0] - 1
    counts = jnp.sum(onehot, axis=0)
    padded = (counts + EXPERT_ROWS - 1) // EXPERT_ROWS * EXPERT_ROWS
    pad_end = jnp.cumsum(padded)
    dest = (pad_end - padded)[flat_e] + rank
    n_slots = (n_assign // EXPERT_ROWS + N_EXPERTS) * EXPERT_ROWS
    n_blocks = n_slots // EXPERT_ROWS
    slot_tok = jnp.zeros((n_slots,), jnp.int32).at[dest].set(jnp.arange(n_assign, dtype=jnp.int32) // TOP_K_EXPERTS)
    block_start = jnp.arange(n_blocks, dtype=jnp.int32) * EXPERT_ROWS
    block_expert = jnp.minimum(jnp.searchsorted(pad_end, block_start, side="right"), N_EXPERTS - 1).astype(jnp.int32)
    n_used = (pad_end[-1:] // EXPERT_ROWS).astype(jnp.int32)

    ys = _expert_ffn(h, slot_tok, block_expert, n_used, wg, wu, wd)
    dest2 = dest.reshape(t, TOP_K_EXPERTS)
    return _moe_combine(ys, dest2[:, 0], dest2[:, 1], gates, x2, mod, seq, g_final, final_norm)


def _token_mixer(x2, mod, g_mix, w_in, g_heads, w_out, batch, seq):
    pa, pb, pc, pd, p_idx, p_w, *pc_views = _in_projection(x2, mod, g_mix, _pack_w_in(w_in), seq)
    shape3 = lambda a: a.reshape(batch, -1, a.shape[-1])
    gh = g_heads.reshape(4, HEADS_PER_MIXER, HEAD_DIM)
    o_a = _stick_breaking_mixer(shape3(pa), gh[0])
    o_b = _dsa_mixer(shape3(pb), shape3(p_idx), shape3(p_w), gh[1])
    dilated = _dilated_branches([shape3(v) for v in [pc] + pc_views])
    o_d = _moba_mixer(shape3(pd), gh[3])
    return _out_projection(o_a, o_b, o_d, dilated, gh[2], w_out.astype(BF16), x2, mod, seq)


def kernel(x, c, w_ada, b_ada, g_mix, w_in, g_heads, w_out, g_ffn, w_ff_gate, w_ff_up, w_ff_down, w_router, w_exp_gate, w_exp_up, w_exp_down, g_final):
    batch, seq, d = x.shape
    depth = w_ada.shape[0]
    mods = _ada_modulation(c, w_ada, b_ada)
    x2 = x.reshape(batch * seq, d)
    for layer in range(depth):
        mod = mods[layer]
        x2 = _token_mixer(x2, mod, g_mix[layer], w_in[layer], g_heads[layer], w_out[layer], batch, seq)
        i = layer // 2
        last = layer == depth - 1
        if layer % 2 == 0:
            x2 = _dense_ffn(x2, mod, g_ffn[layer], w_ff_gate[i], w_ff_up[i], w_ff_down[i], seq, g_final, last)
        else:
            x2 = _moe_ffn(x2, mod, g_ffn[layer], w_router[i], w_exp_gate[i], w_exp_up[i], w_exp_down[i], seq,
                          g_final, last)
    return x2.reshape(batch, seq, d)
```
